```python
import math
import jax, jax.numpy as jnp
from jax import lax
import numpy as np

D_MODEL = 1024
BATCH = 8
SEQ = 2048
DEPTH = 4

N_META = 16
RMS_EPS = 1e-6
L2_EPS = 1e-6
CONV_K = 4
SB_HEADS = 8
SB_HEAD_DIM = 128
SB_WIDTH = SB_HEADS * SB_HEAD_DIM
SB_BLOCK = 128
GDN_HEADS = 8
GDN_DK = 128
GDN_DV = 128
GDN_QK_WIDTH = GDN_HEADS * GDN_DK
GDN_V_WIDTH = GDN_HEADS * GDN_DV
GDN_CHUNK = 64
SSM_EXPAND = 2
SSM_INNER = SSM_EXPAND * D_MODEL
SSM_HEAD_DIM = 64
SSM_HEADS = SSM_INNER // SSM_HEAD_DIM
SSM_GROUPS = 2
SSM_STATE = 128
SSM_CHUNK = 64
SSM_CONV_CH = SSM_INNER + 2 * SSM_GROUPS * SSM_STATE
N_BRANCH = 3
IN_SPLITS = (SB_WIDTH, SB_WIDTH, SB_WIDTH, SB_WIDTH,
             GDN_QK_WIDTH, GDN_QK_WIDTH, GDN_V_WIDTH, GDN_V_WIDTH, GDN_HEADS, GDN_HEADS,
             SSM_INNER, SSM_CONV_CH, SSM_HEADS,
             N_BRANCH * D_MODEL)
D_IN = 4 * SB_WIDTH + 2 * GDN_QK_WIDTH + 2 * GDN_V_WIDTH + 2 * GDN_HEADS + SSM_INNER + SSM_CONV_CH + SSM_HEADS + N_BRANCH * D_MODEL

kernel_name = 'hybrid_stickbreak_gdn_ssd_block'


def _split(t, sizes):
    out, start = [], 0
    for s in sizes:
        out.append(t[..., start:start + s])
        start += s
    return out


def _rmsnorm(x, g):
    xf = x.astype(jnp.float32)
    y = xf * lax.rsqrt(jnp.mean(xf * xf, axis=-1, keepdims=True) + RMS_EPS)
    return (y * g.astype(jnp.float32)).astype(x.dtype)


def _l2norm(t):
    t = t.astype(jnp.float32)
    return t * lax.rsqrt(jnp.sum(t * t, axis=-1, keepdims=True) + L2_EPS)


def _front_pad(t, n):
    return jnp.pad(t, [(0, 0), (n, 0)] + [(0, 0)] * (t.ndim - 2))


def _causal_dwconv(x, w):
    k, c = w.shape
    return lax.conv_general_dilated(x, w.astype(x.dtype).reshape(k, 1, c), window_strides=(1,),
                                    padding=[(k - 1, 0)], dimension_numbers=('NWC', 'WIO', 'NWC'),
                                    feature_group_count=c)


def _stick_breaking_attention(q, k, v):
    bsz, seq, nh, dh = q.shape
    pad = SB_BLOCK - N_META
    lp = seq + pad
    nb = lp // SB_BLOCK
    scale = dh ** -0.5
    qf, kf, vf = [jnp.swapaxes(_front_pad(t.astype(jnp.float32), pad), 1, 2) for t in (q, k, v)]
    q_blocks = jnp.moveaxis(qf.reshape(bsz, nh, nb, SB_BLOCK, dh), 2, 0)
    key_pos = jnp.arange(lp)

    def one_block(args):
        qb, bi = args
        q_pos = bi * SB_BLOCK + jnp.arange(SB_BLOCK)
        mask = (key_pos[None, :] < q_pos[:, None]) & (key_pos[None, :] >= pad)
        z = jnp.einsum('bhqd,bhkd->bhqk', qb, kf) * scale
        log_keep = jnp.where(mask, jax.nn.log_sigmoid(-z), 0.0)
        later = lax.cumsum(log_keep, axis=3, reverse=True) - log_keep
        w = jnp.where(mask, jnp.exp(jax.nn.log_sigmoid(z) + later), 0.0)
        return jnp.einsum('bhqk,bhkd->bhqd', w, vf)

    out = lax.map(one_block, (q_blocks, jnp.arange(nb)))
    out = jnp.transpose(out, (1, 0, 3, 2, 4)).reshape(bsz, lp, nh, dh)
    return out[:, pad:]


def _gated_delta_rule(q, k, v, g, beta):
    pad = GDN_CHUNK - N_META
    q, k, v, g, beta = [_front_pad(t, pad) for t in (q, k, v, g, beta)]
    bsz, lp, nh, dk = q.shape
    dv = v.shape[-1]
    cl = GDN_CHUNK
    nc = lp // cl

    def chunked(t):
        return t.reshape((bsz, nc, cl) + t.shape[2:])

    q = chunked(q * dk ** -0.5)
    k, v, g, beta = chunked(k), chunked(v), chunked(g), chunked(beta)
    gc = jnp.cumsum(g, axis=2)
    gt = jnp.moveaxis(gc, 2, 3)
    seg = gt[..., :, None] - gt[..., None, :]
    idx = jnp.arange(cl)
    strict = idx[:, None] > idx[None, :]
    incl = idx[:, None] >= idx[None, :]
    dec_strict = jnp.exp(jnp.where(strict, seg, -jnp.inf))
    dec_incl = jnp.exp(jnp.where(incl, seg, -jnp.inf))
    kb = k * beta[..., None]
    m = jnp.einsum('bclhd,bcshd->bchls', kb, k) * dec_strict
    eye = jnp.eye(cl, dtype=m.dtype)
    t_inv = lax.linalg.triangular_solve(m + eye, jnp.broadcast_to(eye, m.shape), left_side=True,
                                        lower=True, unit_diagonal=True)
    u = jnp.einsum('bchls,bcshd->bclhd', t_inv, v * beta[..., None])
    w = jnp.einsum('bchls,bcshd->bclhd', t_inv, kb * jnp.exp(gc)[..., None])
    a_qk = jnp.einsum('bclhd,bcshd->bchls', q, k) * dec_incl
    g_last = gc[:, :, -1]
    q_dec = q * jnp.exp(gc)[..., None]
    k_end = k * jnp.exp(g_last[:, :, None] - gc)[..., None]

    def step(state, inp):
        qd, ke, uc, wc, aqk, gl = inp
        v_new = uc - jnp.einsum('blhk,bhkv->blhv', wc, state)
        o = jnp.einsum('blhk,bhkv->blhv', qd, state) + jnp.einsum('bhls,bshv->blhv', aqk, v_new)
        state = state * jnp.exp(gl)[..., None, None] + jnp.einsum('blhk,blhv->bhkv', ke, v_new)
        return state, o

    s0 = jnp.zeros((bsz, nh, dk, dv), q.dtype)
    xs = tuple(jnp.moveaxis(t, 1, 0) for t in (q_dec, k_end, u, w, a_qk, g_last))
    _, o = lax.scan(step, s0, xs)
    o = jnp.moveaxis(o, 0, 1).reshape(bsz, lp, nh, dv)
    return o[:, pad:]


def _ssd_scan(x, dt, a, b_in, c_in):
    pad = SSM_CHUNK - N_META
    x, dt, b_in, c_in = [_front_pad(t, pad) for t in (x, dt, b_in, c_in)]
    bsz, lp, nh, p = x.shape
    ng, n = b_in.shape[2], b_in.shape[3]
    hg = nh // ng
    cl = SSM_CHUNK
    nc = lp // cl
    xs = (x * dt[..., None]).reshape(bsz, nc, cl, ng, hg, p)
    la = (dt * a).reshape(bsz, nc, cl, ng, hg)
    bc = b_in.reshape(bsz, nc, cl, ng, n)
    cc = c_in.reshape(bsz, nc, cl, ng, n)
    cs = jnp.cumsum(la, axis=2)
    causal = jnp.tril(jnp.ones((cl, cl), dtype=bool))
    seg = cs[:, :, :, None] - cs[:, :, None, :]
    decay = jnp.exp(jnp.where(causal[:, :, None, None], seg, -jnp.inf))
    scores = jnp.einsum('bclgn,bcsgn->bclsg', cc, bc)[..., None] * decay
    y_diag = jnp.einsum('bclsgh,bcsghp->bclghp', scores, xs)
    to_end = jnp.exp(cs[:, :, -1:] - cs)
    states = jnp.einsum('bclgn,bclghp->bcghpn', bc, xs * to_end[..., None])
    chunk_decay = jnp.exp(cs[:, :, -1])

    def step(hst, inp):
        st, cd = inp
        return cd[..., None, None] * hst + st, hst

    h0 = jnp.zeros((bsz, ng, hg, p, n), xs.dtype)
    _, h_prev = lax.scan(step, h0, (jnp.moveaxis(states, 1, 0), jnp.moveaxis(chunk_decay, 1, 0)))
    h_prev = jnp.moveaxis(h_prev, 0, 1)
    y_off = jnp.einsum('bclgn,bcghpn->bclghp', cc, h_prev) * jnp.exp(cs)[..., None]
    y = (y_diag + y_off).reshape(bsz, lp, nh, p)
    return y[:, pad:]


def _hybrid_mixer(u, w_in, gdn_conv_w, gdn_a_log, gdn_dt_bias, gdn_norm_g, ssm_conv_w, ssm_conv_b,
                  ssm_a_log, ssm_dt_bias, ssm_d, ssm_norm_g, w_branch_a, w_branch_b, w_branch_c, w_out):
    f32 = jnp.float32
    dtype = u.dtype
    bsz, seq, _ = u.shape
    proj = u @ w_in
    (sb_q, sb_k, sb_v, sb_z, gd_q, gd_k, gd_v, gd_z, gd_b, gd_a,
     ss_z, ss_xbc, ss_dt, gates) = _split(proj, IN_SPLITS)

    def heads(t, nh):
        return t.reshape(bsz, seq, nh, -1)

    o_a = _stick_breaking_attention(heads(sb_q, SB_HEADS), heads(sb_k, SB_HEADS), heads(sb_v, SB_HEADS))
    o_a = o_a.reshape(bsz, seq, SB_WIDTH).astype(dtype) * jax.nn.silu(sb_z)

    qkv = jax.nn.silu(_causal_dwconv(jnp.concatenate([gd_q, gd_k, gd_v], axis=-1), gdn_conv_w))
    cq, ck, cv = _split(qkv, (GDN_QK_WIDTH, GDN_QK_WIDTH, GDN_V_WIDTH))
    beta = jax.nn.sigmoid(gd_b.astype(f32))
    g = -jnp.exp(gdn_a_log.astype(f32)) * jax.nn.softplus(gd_a.astype(f32) + gdn_dt_bias.astype(f32))
    o_b = _gated_delta_rule(_l2norm(heads(cq, GDN_HEADS)), _l2norm(heads(ck, GDN_HEADS)),
                            heads(cv, GDN_HEADS).astype(f32), g, beta)
    o_b = _rmsnorm(o_b, gdn_norm_g).reshape(bsz, seq, GDN_V_WIDTH).astype(dtype) * jax.nn.silu(gd_z)

    xbc = jax.nn.silu(_causal_dwconv(ss_xbc, ssm_conv_w) + ssm_conv_b)
    sx, sb, sc = _split(xbc, (SSM_INNER, SSM_GROUPS * SSM_STATE, SSM_GROUPS * SSM_STATE))
    dt = jax.nn.softplus(ss_dt.astype(f32) + ssm_dt_bias.astype(f32))
    a = -jnp.exp(ssm_a_log.astype(f32))
    xh = heads(sx, SSM_HEADS).astype(f32)
    y = _ssd_scan(xh, dt, a, heads(sb, SSM_GROUPS).astype(f32), heads(sc, SSM_GROUPS).astype(f32))
    y = y + ssm_d.astype(f32)[:, None] * xh
    y = y.reshape(bsz, seq, SSM_INNER) * jax.nn.silu(ss_z.astype(f32))
    o_c = _rmsnorm(y.reshape(bsz, seq, SSM_GROUPS, -1), ssm_norm_g.reshape(SSM_GROUPS, -1))
    o_c = o_c.reshape(bsz, seq, SSM_INNER).astype(dtype)

    g_a, g_b, g_c = _split(jax.nn.sigmoid(gates), (D_MODEL, D_MODEL, D_MODEL))
    merged = g_a * (o_a @ w_branch_a) + g_b * (o_b @ w_branch_b) + g_c * (o_c @ w_branch_c)
    return merged @ w_out


def _fwd_setup_inputs(seed: int = 0) -> dict:
    key = jax.random.key(seed)
    ks = jax.random.split(key, 20)
    f32 = jnp.float32

    def nrm(k, shape, scale):
        return jax.random.normal(k, shape, f32) * scale

    def gain(k, shape):
        return 1.0 + 0.02 * jax.random.normal(k, shape, f32)

    def dt_bias(k, shape):
        dt = jnp.exp(jax.random.uniform(k, shape, f32, math.log(1e-3), math.log(1e-1)))
        return dt + jnp.log(-jnp.expm1(-dt))

    def a_log(k, shape):
        return jnp.log(jax.random.uniform(k, shape, f32, 1.0, 16.0))

    return {
        'x': jax.random.normal(ks[0], (BATCH, SEQ, D_MODEL), f32),
        'meta_tokens': nrm(ks[1], (N_META, D_MODEL), 1.0),
        'norm_g': gain(ks[2], (DEPTH, D_MODEL)),
        'w_in': nrm(ks[3], (DEPTH, D_MODEL, D_IN), D_MODEL ** -0.5),
        'gdn_conv_w': nrm(ks[4], (DEPTH, CONV_K, 2 * GDN_QK_WIDTH + GDN_V_WIDTH), CONV_K ** -0.5),
        'gdn_a_log': a_log(ks[5], (DEPTH, GDN_HEADS)),
        'gdn_dt_bias': dt_bias(ks[6], (DEPTH, GDN_HEADS)),
        'gdn_norm_g': gain(ks[7], (DEPTH, GDN_DV)),
        'ssm_conv_w': nrm(ks[8], (DEPTH, CONV_K, SSM_CONV_CH), CONV_K ** -0.5),
        'ssm_conv_b': nrm(ks[9], (DEPTH, SSM_CONV_CH), 0.01),
        'ssm_a_log': a_log(ks[10], (DEPTH, SSM_HEADS)),
        'ssm_dt_bias': dt_bias(ks[11], (DEPTH, SSM_HEADS)),
        'ssm_d': 1.0 + 0.1 * jax.random.normal(ks[12], (DEPTH, SSM_HEADS), f32),
        'ssm_norm_g': gain(ks[13], (DEPTH, SSM_INNER)),
        'w_branch_a': nrm(ks[14], (DEPTH, SB_WIDTH, D_MODEL), SB_WIDTH ** -0.5),
        'w_branch_b': nrm(ks[15], (DEPTH, GDN_V_WIDTH, D_MODEL), GDN_V_WIDTH ** -0.5),
        'w_branch_c': nrm(ks[16], (DEPTH, SSM_INNER, D_MODEL), SSM_INNER ** -0.5),
        'w_out': nrm(ks[17], (DEPTH, D_MODEL, D_MODEL), D_MODEL ** -0.5),
        'final_norm_g': gain(ks[18], (D_MODEL,)),
    }


def _fwd_reference(x, meta_tokens, norm_g, w_in, gdn_conv_w, gdn_a_log, gdn_dt_bias, gdn_norm_g, ssm_conv_w,
              ssm_conv_b, ssm_a_log, ssm_dt_bias, ssm_d, ssm_norm_g, w_branch_a, w_branch_b, w_branch_c,
              w_out, final_norm_g):
    bsz = x.shape[0]
    meta = jnp.broadcast_to(meta_tokens.astype(x.dtype)[None], (bsz, N_META, D_MODEL))
    h = jnp.concatenate([meta, x], axis=1)
    for layer in range(DEPTH):
        h = h + _hybrid_mixer(_rmsnorm(h, norm_g[layer]), w_in[layer], gdn_conv_w[layer], gdn_a_log[layer],
                              gdn_dt_bias[layer], gdn_norm_g[layer], ssm_conv_w[layer], ssm_conv_b[layer],
                              ssm_a_log[layer], ssm_dt_bias[layer], ssm_d[layer], ssm_norm_g[layer],
                              w_branch_a[layer], w_branch_b[layer], w_branch_c[layer], w_out[layer])
    return _rmsnorm(h, final_norm_g)[:, N_META:]


import jax as _jax
import jax.numpy as _jnp

TWIN_FORMAT = 'train_step'
FWD_PARAMS = ['x', 'meta_tokens', 'norm_g', 'w_in', 'gdn_conv_w', 'gdn_a_log', 'gdn_dt_bias', 'gdn_norm_g', 'ssm_conv_w', 'ssm_conv_b', 'ssm_a_log', 'ssm_dt_bias', 'ssm_d', 'ssm_norm_g', 'w_branch_a', 'w_branch_b', 'w_branch_c', 'w_out', 'final_norm_g']
TWIN_WEIGHTS = ['meta_tokens', 'norm_g', 'w_in', 'gdn_conv_w', 'gdn_a_log', 'gdn_dt_bias', 'gdn_norm_g', 'ssm_conv_w', 'ssm_conv_b', 'ssm_a_log', 'ssm_dt_bias', 'ssm_d', 'ssm_norm_g', 'w_branch_a', 'w_branch_b', 'w_branch_c', 'w_out', 'final_norm_g']
TWIN_DIFF_INPUT = 'x'
TWIN_INPUTS = ['x', 'meta_tokens', 'norm_g', 'w_in', 'gdn_conv_w', 'gdn_a_log', 'gdn_dt_bias', 'gdn_norm_g', 'ssm_conv_w', 'ssm_conv_b', 'ssm_a_log', 'ssm_dt_bias', 'ssm_d', 'ssm_norm_g', 'w_branch_a', 'w_branch_b', 'w_branch_c', 'w_out', 'final_norm_g', 'loss_target', 'm_meta_tokens', 'm_norm_g', 'm_w_in', 'm_gdn_conv_w', 'm_gdn_a_log', 'm_gdn_dt_bias', 'm_gdn_norm_g', 'm_ssm_conv_w', 'm_ssm_conv_b', 'm_ssm_a_log', 'm_ssm_dt_bias', 'm_ssm_d', 'm_ssm_norm_g', 'm_w_branch_a', 'm_w_branch_b', 'm_w_branch_c', 'm_w_out', 'm_final_norm_g', 'v_meta_tokens', 'v_norm_g', 'v_w_in', 'v_gdn_conv_w', 'v_gdn_a_log', 'v_gdn_dt_bias', 'v_gdn_norm_g', 'v_ssm_conv_w', 'v_ssm_conv_b', 'v_ssm_a_log', 'v_ssm_dt_bias', 'v_ssm_d', 'v_ssm_norm_g', 'v_w_branch_a', 'v_w_branch_b', 'v_w_branch_c', 'v_w_out', 'v_final_norm_g']
TWIN_OUTPUTS = ['loss', 'grad_x', 'grad_meta_tokens', 'grad_norm_g', 'grad_w_in', 'grad_gdn_conv_w', 'grad_gdn_a_log', 'grad_gdn_dt_bias', 'grad_gdn_norm_g', 'grad_ssm_conv_w', 'grad_ssm_conv_b', 'grad_ssm_a_log', 'grad_ssm_dt_bias', 'grad_ssm_d', 'grad_ssm_norm_g', 'grad_w_branch_a', 'grad_w_branch_b', 'grad_w_branch_c', 'grad_w_out', 'grad_final_norm_g', 'delta_meta_tokens', 'delta_norm_g', 'delta_w_in', 'delta_gdn_conv_w', 'delta_gdn_a_log', 'delta_gdn_dt_bias', 'delta_gdn_norm_g', 'delta_ssm_conv_w', 'delta_ssm_conv_b', 'delta_ssm_a_log', 'delta_ssm_dt_bias', 'delta_ssm_d', 'delta_ssm_norm_g', 'delta_w_branch_a', 'delta_w_branch_b', 'delta_w_branch_c', 'delta_w_out', 'delta_final_norm_g', 'new_m_meta_tokens', 'new_m_norm_g', 'new_m_w_in', 'new_m_gdn_conv_w', 'new_m_gdn_a_log', 'new_m_gdn_dt_bias', 'new_m_gdn_norm_g', 'new_m_ssm_conv_w', 'new_m_ssm_conv_b', 'new_m_ssm_a_log', 'new_m_ssm_dt_bias', 'new_m_ssm_d', 'new_m_ssm_norm_g', 'new_m_w_branch_a', 'new_m_w_branch_b', 'new_m_w_branch_c', 'new_m_w_out', 'new_m_final_norm_g', 'new_v_meta_tokens', 'new_v_norm_g', 'new_v_w_in', 'new_v_gdn_conv_w', 'new_v_gdn_a_log', 'new_v_gdn_dt_bias', 'new_v_gdn_norm_g', 'new_v_ssm_conv_w', 'new_v_ssm_conv_b', 'new_v_ssm_a_log', 'new_v_ssm_dt_bias', 'new_v_ssm_d', 'new_v_ssm_norm_g', 'new_v_w_branch_a', 'new_v_w_branch_b', 'new_v_w_branch_c', 'new_v_w_out', 'new_v_final_norm_g']
TWIN_LEAF_KINDS = {'loss': 'loss', 'grad_x': 'grad_x', 'grad_meta_tokens': 'grad_w', 'grad_norm_g': 'grad_w', 'grad_w_in': 'grad_w', 'grad_gdn_conv_w': 'grad_w', 'grad_gdn_a_log': 'grad_w', 'grad_gdn_dt_bias': 'grad_w', 'grad_gdn_norm_g': 'grad_w', 'grad_ssm_conv_w': 'grad_w', 'grad_ssm_conv_b': 'grad_w', 'grad_ssm_a_log': 'grad_w', 'grad_ssm_dt_bias': 'grad_w', 'grad_ssm_d': 'grad_w', 'grad_ssm_norm_g': 'grad_w', 'grad_w_branch_a': 'grad_w', 'grad_w_branch_b': 'grad_w', 'grad_w_branch_c': 'grad_w', 'grad_w_out': 'grad_w', 'grad_final_norm_g': 'grad_w', 'delta_meta_tokens': 'delta_w', 'delta_norm_g': 'delta_w', 'delta_w_in': 'delta_w', 'delta_gdn_conv_w': 'delta_w', 'delta_gdn_a_log': 'delta_w', 'delta_gdn_dt_bias': 'delta_w', 'delta_gdn_norm_g': 'delta_w', 'delta_ssm_conv_w': 'delta_w', 'delta_ssm_conv_b': 'delta_w', 'delta_ssm_a_log': 'delta_w', 'delta_ssm_dt_bias': 'delta_w', 'delta_ssm_d': 'delta_w', 'delta_ssm_norm_g': 'delta_w', 'delta_w_branch_a': 'delta_w', 'delta_w_branch_b': 'delta_w', 'delta_w_branch_c': 'delta_w', 'delta_w_out': 'delta_w', 'delta_final_norm_g': 'delta_w', 'new_m_meta_tokens': 'new_m', 'new_m_norm_g': 'new_m', 'new_m_w_in': 'new_m', 'new_m_gdn_conv_w': 'new_m', 'new_m_gdn_a_log': 'new_m', 'new_m_gdn_dt_bias': 'new_m', 'new_m_gdn_norm_g': 'new_m', 'new_m_ssm_conv_w': 'new_m', 'new_m_ssm_conv_b': 'new_m', 'new_m_ssm_a_log': 'new_m', 'new_m_ssm_dt_bias': 'new_m', 'new_m_ssm_d': 'new_m', 'new_m_ssm_norm_g': 'new_m', 'new_m_w_branch_a': 'new_m', 'new_m_w_branch_b': 'new_m', 'new_m_w_branch_c': 'new_m', 'new_m_w_out': 'new_m', 'new_m_final_norm_g': 'new_m', 'new_v_meta_tokens': 'new_v', 'new_v_norm_g': 'new_v', 'new_v_w_in': 'new_v', 'new_v_gdn_conv_w': 'new_v', 'new_v_gdn_a_log': 'new_v', 'new_v_gdn_dt_bias': 'new_v', 'new_v_gdn_norm_g': 'new_v', 'new_v_ssm_conv_w': 'new_v', 'new_v_ssm_conv_b': 'new_v', 'new_v_ssm_a_log': 'new_v', 'new_v_ssm_dt_bias': 'new_v', 'new_v_ssm_d': 'new_v', 'new_v_ssm_norm_g': 'new_v', 'new_v_w_branch_a': 'new_v', 'new_v_w_branch_b': 'new_v', 'new_v_w_branch_c': 'new_v', 'new_v_w_out': 'new_v', 'new_v_final_norm_g': 'new_v'}


def _forward(args):
    return _fwd_reference(*[args[k] for k in FWD_PARAMS])


def _output_shape():
    out = _jax.eval_shape(lambda: _forward(_fwd_setup_inputs(0)))
    return out.shape, out.dtype

N_MICROBATCH = 1
ADAM_LR = 0.001
ADAM_B1 = 0.9
ADAM_B2 = 0.999
ADAM_EPS = 1e-08
ADAM_WD = 0.01
ADAM_STEP = 10
PER_EXAMPLE_BATCH_AXIS = {'x': 0, 'loss_target': 0}
SHARED_INPUTS = []
_WEIGHT_DTYPES = {'meta_tokens': _jnp.float32, 'norm_g': _jnp.float32, 'w_in': _jnp.float32, 'gdn_conv_w': _jnp.float32, 'gdn_a_log': _jnp.float32, 'gdn_dt_bias': _jnp.float32, 'gdn_norm_g': _jnp.float32, 'ssm_conv_w': _jnp.float32, 'ssm_conv_b': _jnp.float32, 'ssm_a_log': _jnp.float32, 'ssm_dt_bias': _jnp.float32, 'ssm_d': _jnp.float32, 'ssm_norm_g': _jnp.float32, 'w_branch_a': _jnp.float32, 'w_branch_b': _jnp.float32, 'w_branch_c': _jnp.float32, 'w_out': _jnp.float32, 'final_norm_g': _jnp.float32}
MOMENT_SCALE = {'meta_tokens': 6.115580e-03, 'norm_g': 1.202126e-01, 'w_in': 3.033406e-02, 'gdn_conv_w': 2.764757e-02, 'gdn_a_log': 1.914253e-01, 'gdn_dt_bias': 1.856540e-01, 'gdn_norm_g': 9.841715e-02, 'ssm_conv_w': 4.144690e-02, 'ssm_conv_b': 5.982488e-02, 'ssm_a_log': 1.669362e-01, 'ssm_dt_bias': 9.356388e-02, 'ssm_d': 2.422391e-01, 'ssm_norm_g': 4.484810e-02, 'w_branch_a': 2.377536e-02, 'w_branch_b': 3.555001e-02, 'w_branch_c': 6.165896e-02, 'w_out': 7.513461e-02, 'final_norm_g': 1.604445e+01}


def _to_microbatches(a, axis):
    t = _jnp.moveaxis(a, axis, 0)
    t = t.reshape((N_MICROBATCH, t.shape[0] // N_MICROBATCH) + t.shape[1:])
    return _jnp.moveaxis(t, 1, axis + 1)


def setup_inputs(seed: int = 0) -> dict:
    inp = _fwd_setup_inputs(seed)
    key = _jax.random.fold_in(_jax.random.key(seed), 7919)
    shape, _ = _output_shape()
    out = dict(inp)
    out["loss_target"] = _jax.random.normal(_jax.random.fold_in(key, 0), shape, _jnp.float32)
    for i, name in enumerate(TWIN_WEIGHTS):
        w = inp[name].astype(_jnp.float32)
        if MOMENT_SCALE is None:
            s = _jnp.sqrt(_jnp.mean(_jnp.square(w)) + 1e-30)
        else:
            s = MOMENT_SCALE[name]
        km, kv = _jax.random.split(_jax.random.fold_in(key, i + 1))
        out[name] = w
        out["m_" + name] = s * _jax.random.normal(km, w.shape, _jnp.float32)
        out["v_" + name] = (s * s) * _jax.random.uniform(kv, w.shape, _jnp.float32, 0.5, 1.5)
    if N_MICROBATCH > 1:
        for name, axis in PER_EXAMPLE_BATCH_AXIS.items():
            out[name] = _to_microbatches(out[name], axis)
    return {'x': out['x'], 'meta_tokens': out['meta_tokens'], 'norm_g': out['norm_g'], 'w_in': out['w_in'], 'gdn_conv_w': out['gdn_conv_w'], 'gdn_a_log': out['gdn_a_log'], 'gdn_dt_bias': out['gdn_dt_bias'], 'gdn_norm_g': out['gdn_norm_g'], 'ssm_conv_w': out['ssm_conv_w'], 'ssm_conv_b': out['ssm_conv_b'], 'ssm_a_log': out['ssm_a_log'], 'ssm_dt_bias': out['ssm_dt_bias'], 'ssm_d': out['ssm_d'], 'ssm_norm_g': out['ssm_norm_g'], 'w_branch_a': out['w_branch_a'], 'w_branch_b': out['w_branch_b'], 'w_branch_c': out['w_branch_c'], 'w_out': out['w_out'], 'final_norm_g': out['final_norm_g'], 'loss_target': out['loss_target'], 'm_meta_tokens': out['m_meta_tokens'], 'm_norm_g': out['m_norm_g'], 'm_w_in': out['m_w_in'], 'm_gdn_conv_w': out['m_gdn_conv_w'], 'm_gdn_a_log': out['m_gdn_a_log'], 'm_gdn_dt_bias': out['m_gdn_dt_bias'], 'm_gdn_norm_g': out['m_gdn_norm_g'], 'm_ssm_conv_w': out['m_ssm_conv_w'], 'm_ssm_conv_b': out['m_ssm_conv_b'], 'm_ssm_a_log': out['m_ssm_a_log'], 'm_ssm_dt_bias': out['m_ssm_dt_bias'], 'm_ssm_d': out['m_ssm_d'], 'm_ssm_norm_g': out['m_ssm_norm_g'], 'm_w_branch_a': out['m_w_branch_a'], 'm_w_branch_b': out['m_w_branch_b'], 'm_w_branch_c': out['m_w_branch_c'], 'm_w_out': out['m_w_out'], 'm_final_norm_g': out['m_final_norm_g'], 'v_meta_tokens': out['v_meta_tokens'], 'v_norm_g': out['v_norm_g'], 'v_w_in': out['v_w_in'], 'v_gdn_conv_w': out['v_gdn_conv_w'], 'v_gdn_a_log': out['v_gdn_a_log'], 'v_gdn_dt_bias': out['v_gdn_dt_bias'], 'v_gdn_norm_g': out['v_gdn_norm_g'], 'v_ssm_conv_w': out['v_ssm_conv_w'], 'v_ssm_conv_b': out['v_ssm_conv_b'], 'v_ssm_a_log': out['v_ssm_a_log'], 'v_ssm_dt_bias': out['v_ssm_dt_bias'], 'v_ssm_d': out['v_ssm_d'], 'v_ssm_norm_g': out['v_ssm_norm_g'], 'v_w_branch_a': out['v_w_branch_a'], 'v_w_branch_b': out['v_w_branch_b'], 'v_w_branch_c': out['v_w_branch_c'], 'v_w_out': out['v_w_out'], 'v_final_norm_g': out['v_final_norm_g']}


def _loss(weights, diff, rest, loss_target):
    with _jax.named_scope("forward"):
        args = {**rest, TWIN_DIFF_INPUT: diff, **{k: w.astype(_WEIGHT_DTYPES[k]) for k, w in weights.items()}}
        y = _forward(args)
    with _jax.named_scope("loss_head"):
        err = _jnp.square(y.astype(_jnp.float32) - loss_target)
        return 0.5 * _jnp.sum(_jnp.mean(err, axis=-1)) if err.ndim else 0.5 * err


def _adamw(w, g, m, v):
    m = ADAM_B1 * m + (1.0 - ADAM_B1) * g
    v = ADAM_B2 * v + (1.0 - ADAM_B2) * _jnp.square(g)
    m_hat = m / (1.0 - ADAM_B1 ** ADAM_STEP)
    v_hat = v / (1.0 - ADAM_B2 ** ADAM_STEP)
    delta = -ADAM_LR * (m_hat / (_jnp.sqrt(v_hat) + ADAM_EPS) + ADAM_WD * w)
    return delta, m, v


def reference(x, meta_tokens, norm_g, w_in, gdn_conv_w, gdn_a_log, gdn_dt_bias, gdn_norm_g, ssm_conv_w, ssm_conv_b, ssm_a_log, ssm_dt_bias, ssm_d, ssm_norm_g, w_branch_a, w_branch_b, w_branch_c, w_out, final_norm_g, loss_target, m_meta_tokens, m_norm_g, m_w_in, m_gdn_conv_w, m_gdn_a_log, m_gdn_dt_bias, m_gdn_norm_g, m_ssm_conv_w, m_ssm_conv_b, m_ssm_a_log, m_ssm_dt_bias, m_ssm_d, m_ssm_norm_g, m_w_branch_a, m_w_branch_b, m_w_branch_c, m_w_out, m_final_norm_g, v_meta_tokens, v_norm_g, v_w_in, v_gdn_conv_w, v_gdn_a_log, v_gdn_dt_bias, v_gdn_norm_g, v_ssm_conv_w, v_ssm_conv_b, v_ssm_a_log, v_ssm_dt_bias, v_ssm_d, v_ssm_norm_g, v_w_branch_a, v_w_branch_b, v_w_branch_c, v_w_out, v_final_norm_g):
    given = dict(x=x, meta_tokens=meta_tokens, norm_g=norm_g, w_in=w_in, gdn_conv_w=gdn_conv_w, gdn_a_log=gdn_a_log, gdn_dt_bias=gdn_dt_bias, gdn_norm_g=gdn_norm_g, ssm_conv_w=ssm_conv_w, ssm_conv_b=ssm_conv_b, ssm_a_log=ssm_a_log, ssm_dt_bias=ssm_dt_bias, ssm_d=ssm_d, ssm_norm_g=ssm_norm_g, w_branch_a=w_branch_a, w_branch_b=w_branch_b, w_branch_c=w_branch_c, w_out=w_out, final_norm_g=final_norm_g, loss_target=loss_target, m_meta_tokens=m_meta_tokens, m_norm_g=m_norm_g, m_w_in=m_w_in, m_gdn_conv_w=m_gdn_conv_w, m_gdn_a_log=m_gdn_a_log, m_gdn_dt_bias=m_gdn_dt_bias, m_gdn_norm_g=m_gdn_norm_g, m_ssm_conv_w=m_ssm_conv_w, m_ssm_conv_b=m_ssm_conv_b, m_ssm_a_log=m_ssm_a_log, m_ssm_dt_bias=m_ssm_dt_bias, m_ssm_d=m_ssm_d, m_ssm_norm_g=m_ssm_norm_g, m_w_branch_a=m_w_branch_a, m_w_branch_b=m_w_branch_b, m_w_branch_c=m_w_branch_c, m_w_out=m_w_out, m_final_norm_g=m_final_norm_g, v_meta_tokens=v_meta_tokens, v_norm_g=v_norm_g, v_w_in=v_w_in, v_gdn_conv_w=v_gdn_conv_w, v_gdn_a_log=v_gdn_a_log, v_gdn_dt_bias=v_gdn_dt_bias, v_gdn_norm_g=v_gdn_norm_g, v_ssm_conv_w=v_ssm_conv_w, v_ssm_conv_b=v_ssm_conv_b, v_ssm_a_log=v_ssm_a_log, v_ssm_dt_bias=v_ssm_dt_bias, v_ssm_d=v_ssm_d, v_ssm_norm_g=v_ssm_norm_g, v_w_branch_a=v_w_branch_a, v_w_branch_b=v_w_branch_b, v_w_branch_c=v_w_branch_c, v_w_out=v_w_out, v_final_norm_g=v_final_norm_g)
    weights = {n: given[n] for n in TWIN_WEIGHTS}
    shared = {n: given[n] for n in SHARED_INPUTS}
    per_example = {n: given[n] for n in ['x']}
    grad_fn = _jax.value_and_grad(_loss, argnums=(0, 1))

    def one_microbatch(ex, loss_target):
        ex = dict(ex)
        diff = ex.pop(TWIN_DIFF_INPUT)
        return grad_fn(weights, diff, {**shared, **ex}, loss_target)

    if N_MICROBATCH == 1:
        loss, (grad_w, grad_x) = one_microbatch(per_example, given["loss_target"])
    else:
        def body(carry, xs):
            loss_sum, grad_sum = carry
            l_k, (gw_k, gx_k) = one_microbatch(xs[0], xs[1])
            with _jax.named_scope("update"):
                return (loss_sum + l_k, _jax.tree.map(_jnp.add, grad_sum, gw_k)), gx_k

        init = (_jnp.zeros((), _jnp.float32), _jax.tree.map(_jnp.zeros_like, weights))
        (loss, grad_w), grad_x = _jax.lax.scan(body, init, (per_example, given["loss_target"]))
    with _jax.named_scope("update"):
        delta_w, new_m, new_v = {}, {}, {}
        for n in TWIN_WEIGHTS:
            delta_w[n], new_m[n], new_v[n] = _adamw(weights[n], grad_w[n], given["m_" + n], given["v_" + n])
    return (loss, grad_x, *[grad_w[n] for n in TWIN_WEIGHTS], *[delta_w[n] for n in TWIN_WEIGHTS],
            *[new_m[n] for n in TWIN_WEIGHTS], *[new_v[n] for n in TWIN_WEIGHTS])
```

```python
import functools
import math

import jax
import jax.numpy as jnp
from jax import lax
from jax.experimental import pallas as pl
from jax.experimental.pallas import tpu as pltpu

F32 = jnp.float32
BF16 = jnp.bfloat16
HIGHEST = lax.Precision.HIGHEST

N_META = 16
RMS_EPS = 1e-6
L2_EPS = 1e-6
CONV_K = 4
D_MODEL = 1024
HEAD_DIM = 128
N_HEADS = 8
CHUNK = 64
SB_BLOCK = 128
PAD = SB_BLOCK - N_META
SSM_INNER = 2048
SSM_P = 64
SSM_HEADS = 32
SSM_GROUPS = 2
SSM_HG = SSM_HEADS // SSM_GROUPS
SSM_N = 128
VMEM_LIMIT = 56 * 1024 * 1024

_DIMS = {"nn": (((1,), (0,)), ((), ())), "nt": (((1,), (1,)), ((), ())), "tn": (((0,), (0,)), ((), ()))}


def _mm_raw(a, b, mode, hi):
    if hi:
        return lax.dot_general(a.astype(F32), b.astype(F32), _DIMS[mode], precision=HIGHEST, preferred_element_type=F32)
    return lax.dot_general(a.astype(BF16), b.astype(BF16), _DIMS[mode], preferred_element_type=F32)


@functools.partial(jax.custom_vjp, nondiff_argnums=(2, 3))
def _mm(a, b, mode="nn", hi=False):
    return _mm_raw(a, b, mode, hi)


def _mm_fwd(a, b, mode, hi):
    return _mm_raw(a, b, mode, hi), (a, b)


def _mm_bwd(mode, hi, res, g):
    a, b = res
    if mode == "nn":
        return _mm_raw(g, b, "nt", hi), _mm_raw(a, g, "tn", hi)
    if mode == "nt":
        return _mm_raw(g, b, "nn", hi), _mm_raw(g, a, "tn", hi)
    return _mm_raw(b, g, "nt", hi), _mm_raw(a, g, "nn", hi)


_mm.defvjp(_mm_fwd, _mm_bwd)


def _iota2(shape, axis):
    return lax.broadcasted_iota(jnp.int32, shape, axis)


def _inv_unit_lower_raw(m):
    size = m.shape[0]
    eye = (_iota2((size, size), 0) == _iota2((size, size), 1)).astype(F32)
    n = -m
    t = eye + n
    p = n
    steps = int(math.log2(size)) - 1
    for _ in range(steps):
        p = _mm_raw(p, p, "nn", True)
        t = t + _mm_raw(t, p, "nn", True)
    return t


@jax.custom_vjp
def _inv_unit_lower(m):
    return _inv_unit_lower_raw(m)


def _inv_fwd(m):
    t = _inv_unit_lower_raw(m)
    return t, t


def _inv_bwd(t, g):
    return (-_mm_raw(_mm_raw(t, g, "tn", True), t, "nt", True),)


_inv_unit_lower.defvjp(_inv_fwd, _inv_bwd)


def _safe_decay(col, row, keep):
    return jnp.where(keep, jnp.exp(jnp.where(keep, col - row, 0.0)), 0.0)


def _col_to_row(col):
    n = col.shape[0]
    eye = _iota2((n, n), 0) == _iota2((n, n), 1)
    return jnp.sum(jnp.where(eye, col, 0.0), axis=0, keepdims=True)


def _cumsum_col(col):
    n = col.shape[0]
    li, si = _iota2((n, n), 0), _iota2((n, n), 1)
    row = _col_to_row(col)
    c_col = jnp.sum(jnp.where(li >= si, row, 0.0), axis=1, keepdims=True)
    c_row = jnp.sum(jnp.where(li <= si, col, 0.0), axis=0, keepdims=True)
    return c_col, c_row


def _gdn_chunk(q, k, v, g, beta, state):
    cl = q.shape[0]
    li, si = _iota2((cl, cl), 0), _iota2((cl, cl), 1)
    gc_col, gc_row = _cumsum_col(g)
    g_last = jnp.sum(g, axis=0, keepdims=True)
    dec_strict = _safe_decay(gc_col, gc_row, li > si)
    dec_incl = _safe_decay(gc_col, gc_row, li >= si)
    e_gc = jnp.exp(gc_col)
    qs = q * (HEAD_DIM ** -0.5)
    kb = k * beta
    m = _mm(kb, k, "nt") * dec_strict
    t_inv = _inv_unit_lower(m)
    u = _mm(t_inv, v * beta)
    w = _mm(t_inv, kb * e_gc)
    a_qk = _mm(qs, k, "nt") * dec_incl
    q_dec = qs * e_gc
    k_end = k * jnp.exp(g_last - gc_col)
    v_new = u - _mm(w, state)
    o = _mm(q_dec, state) + _mm(a_qk, v_new)
    new_state = state * jnp.exp(g_last) + _mm(k_end, v_new, "tn")
    return o, new_state


def gdn_fwd(qkv, g, beta):
    lp = qkv.shape[0]
    nh = N_HEADS
    nc = lp // CHUNK

    def body(q_ref, k_ref, v_ref, g_ref, b_ref, o_ref, s_ref, state):
        @pl.when(pl.program_id(1) == 0)
        def _():
            state[...] = jnp.zeros_like(state)

        s_in = state[...]
        s_ref[0, 0] = s_in
        o, s_new = _gdn_chunk(q_ref[...], k_ref[...], v_ref[...], g_ref[0], b_ref[0], s_in)
        o_ref[...] = o
        state[...] = s_new

    seq = pl.BlockSpec((CHUNK, HEAD_DIM), lambda h, c: (c, h))
    kseq = pl.BlockSpec((CHUNK, HEAD_DIM), lambda h, c: (c, nh + h))
    vseq = pl.BlockSpec((CHUNK, HEAD_DIM), lambda h, c: (c, 2 * nh + h))
    gate = pl.BlockSpec((1, CHUNK, 1), lambda h, c: (h, c, 0))
    return pl.pallas_call(
        body, name="gdn_fwd", grid=(nh, nc),
        in_specs=[seq, kseq, vseq, gate, gate],
        out_specs=[seq, pl.BlockSpec((1, 1, HEAD_DIM, HEAD_DIM), lambda h, c: (h, c, 0, 0))],
        out_shape=[jax.ShapeDtypeStruct((lp, nh * HEAD_DIM), F32), jax.ShapeDtypeStruct((nh, nc, HEAD_DIM, HEAD_DIM), F32)],
        scratch_shapes=[pltpu.VMEM((HEAD_DIM, HEAD_DIM), F32)],
        compiler_params=pltpu.CompilerParams(dimension_semantics=("arbitrary", "arbitrary")),
    )(qkv, qkv, qkv, g, beta)


def gdn_bwd(qkv, g, beta, states, d_o):
    lp = qkv.shape[0]
    nh = N_HEADS
    nc = lp // CHUNK

    def body(q_ref, k_ref, v_ref, g_ref, b_ref, s_ref, do_ref, dq_ref, dk_ref, dv_ref, dg_ref, db_ref, d_state):
        @pl.when(pl.program_id(1) == 0)
        def _():
            d_state[...] = jnp.zeros_like(d_state)

        _, pull = jax.vjp(_gdn_chunk, q_ref[...], k_ref[...], v_ref[...], g_ref[0], b_ref[0], s_ref[0, 0])
        dq, dk, dv, dg, db, ds = pull((do_ref[...], d_state[...]))
        dq_ref[...] = dq
        dk_ref[...] = dk
        dv_ref[...] = dv
        dg_ref[0] = dg
        db_ref[0] = db
        d_state[...] = ds

    seq = pl.BlockSpec((CHUNK, HEAD_DIM), lambda h, c: (nc - 1 - c, h))
    kseq = pl.BlockSpec((CHUNK, HEAD_DIM), lambda h, c: (nc - 1 - c, nh + h))
    vseq = pl.BlockSpec((CHUNK, HEAD_DIM), lambda h, c: (nc - 1 - c, 2 * nh + h))
    gate = pl.BlockSpec((1, CHUNK, 1), lambda h, c: (h, nc - 1 - c, 0))
    one = jax.ShapeDtypeStruct((lp, nh * HEAD_DIM), F32)
    return pl.pallas_call(
        body, name="gdn_bwd", grid=(nh, nc),
        in_specs=[seq, kseq, vseq, gate, gate, pl.BlockSpec((1, 1, HEAD_DIM, HEAD_DIM), lambda h, c: (h, nc - 1 - c, 0, 0)), seq],
        out_specs=[seq, seq, seq, gate, gate],
        out_shape=[one, one, one, jax.ShapeDtypeStruct(g.shape, F32), jax.ShapeDtypeStruct(g.shape, F32)],
        scratch_shapes=[pltpu.VMEM((HEAD_DIM, HEAD_DIM), F32)],
        compiler_params=pltpu.CompilerParams(dimension_semantics=("arbitrary", "arbitrary")),
    )(qkv, qkv, qkv, g, beta, states, d_o)


def _head_expand():
    width = SSM_HG * SSM_P
    return (_iota2((SSM_HG, width), 1) // SSM_P == _iota2((SSM_HG, width), 0)).astype(F32)


def _ssd_chunk(x, b, c, dt, la, state):
    cl = x.shape[0]
    li, si = _iota2((cl, cl), 0), _iota2((cl, cl), 1)
    causal = li >= si
    expand = _head_expand()
    tri = causal.astype(F32)
    xs = x * _mm(dt, expand, "nn", True)
    la_x = _mm(la, expand, "nn", True)
    cs_x = _mm(tri, la_x, "nn", True)
    last_x = jnp.sum(la_x, axis=0, keepdims=True)
    cs = _mm(tri, la, "nn", True)
    scores = _mm(c, b, "nt")
    y = _mm(c, state) * jnp.exp(cs_x)
    lane_head = _iota2((1, SSM_HG * SSM_P), 1) // SSM_P
    head_id = _iota2((1, SSM_HG), 1)
    for h in range(SSM_HG):
        cs_col = jnp.sum(jnp.where(head_id == h, cs, 0.0), axis=1, keepdims=True)
        decay = _safe_decay(cs_col, _col_to_row(cs_col), causal)
        y = y + _mm(scores * decay, jnp.where(lane_head == h, xs, 0.0))
    new_state = state * jnp.exp(last_x) + _mm(b, xs * jnp.exp(last_x - cs_x), "tn")
    return y, new_state


def ssd_fwd(xbc, dt, la):
    lp = xbc.shape[0]
    nc = lp // CHUNK
    width = SSM_HG * SSM_P

    def body(x_ref, b_ref, c_ref, dt_ref, la_ref, y_ref, s_ref, state):
        @pl.when(pl.program_id(1) == 0)
        def _():
            state[...] = jnp.zeros_like(state)

        s_in = state[...]
        s_ref[0, 0] = s_in
        y, s_new = _ssd_chunk(x_ref[...], b_ref[...], c_ref[...], dt_ref[0], la_ref[0], s_in)
        y_ref[...] = y
        state[...] = s_new

    b_off = SSM_INNER // SSM_N
    xspec = pl.BlockSpec((CHUNK, width), lambda g, k: (k, g))
    bspec = pl.BlockSpec((CHUNK, SSM_N), lambda g, k: (k, b_off + g))
    cspec = pl.BlockSpec((CHUNK, SSM_N), lambda g, k: (k, b_off + SSM_GROUPS + g))
    hspec = pl.BlockSpec((1, CHUNK, SSM_HG), lambda g, k: (g, k, 0))
    return pl.pallas_call(
        body, name="ssd_fwd", grid=(SSM_GROUPS, nc),
        in_specs=[xspec, bspec, cspec, hspec, hspec],
        out_specs=[xspec, pl.BlockSpec((1, 1, SSM_N, width), lambda g, k: (g, k, 0, 0))],
        out_shape=[jax.ShapeDtypeStruct((lp, SSM_INNER), F32), jax.ShapeDtypeStruct((SSM_GROUPS, nc, SSM_N, width), F32)],
        scratch_shapes=[pltpu.VMEM((SSM_N, width), F32)],
        compiler_params=pltpu.CompilerParams(dimension_semantics=("arbitrary", "arbitrary"), vmem_limit_bytes=VMEM_LIMIT),
    )(xbc, xbc, xbc, dt, la)


def ssd_bwd(xbc, dt, la, states, d_y):
    lp = xbc.shape[0]
    nc = lp // CHUNK
    width = SSM_HG * SSM_P

    def body(x_ref, b_ref, c_ref, dt_ref, la_ref, s_ref, dy_ref, dx_ref, db_ref, dc_ref, ddt_ref, dla_ref, d_state):
        @pl.when(pl.program_id(1) == 0)
        def _():
            d_state[...] = jnp.zeros_like(d_state)

        _, pull = jax.vjp(_ssd_chunk, x_ref[...], b_ref[...], c_ref[...], dt_ref[0], la_ref[0], s_ref[0, 0])
        dx, db, dc, ddt, dla, ds = pull((dy_ref[...], d_state[...]))
        dx_ref[...] = dx
        db_ref[...] = db
        dc_ref[...] = dc
        ddt_ref[0] = ddt
        dla_ref[0] = dla
        d_state[...] = ds

    b_off = SSM_INNER // SSM_N
    xspec = pl.BlockSpec((CHUNK, width), lambda g, k: (nc - 1 - k, g))
    bin_spec = pl.BlockSpec((CHUNK, SSM_N), lambda g, k: (nc - 1 - k, b_off + g))
    cin_spec = pl.BlockSpec((CHUNK, SSM_N), lambda g, k: (nc - 1 - k, b_off + SSM_GROUPS + g))
    bspec = pl.BlockSpec((CHUNK, SSM_N), lambda g, k: (nc - 1 - k, g))
    hspec = pl.BlockSpec((1, CHUNK, SSM_HG), lambda g, k: (g, nc - 1 - k, 0))
    small = jax.ShapeDtypeStruct((lp, SSM_GROUPS * SSM_N), F32)
    return pl.pallas_call(
        body, name="ssd_bwd", grid=(SSM_GROUPS, nc),
        in_specs=[xspec, bin_spec, cin_spec, hspec, hspec, pl.BlockSpec((1, 1, SSM_N, width), lambda g, k: (g, nc - 1 - k, 0, 0)), xspec],
        out_specs=[xspec, bspec, bspec, hspec, hspec],
        out_shape=[jax.ShapeDtypeStruct((lp, SSM_INNER), F32), small, small,
                   jax.ShapeDtypeStruct(dt.shape, F32), jax.ShapeDtypeStruct(la.shape, F32)],
        scratch_shapes=[pltpu.VMEM((SSM_N, width), F32)],
        compiler_params=pltpu.CompilerParams(dimension_semantics=("arbitrary", "arbitrary"), vmem_limit_bytes=VMEM_LIMIT),
    )(xbc, xbc, xbc, dt, la, states, d_y)


def _sb_scores(q_scaled, kb, i, j):
    blk = SB_BLOCK
    z = _mm_raw(q_scaled, kb, "nt", False)
    q_pos = i * blk + _iota2((blk, blk), 0)
    k_pos = j * blk + _iota2((blk, blk), 1)
    valid = (k_pos < q_pos) & (k_pos >= PAD)
    sp = jnp.maximum(z, 0.0) + jnp.log(1.0 + jnp.exp(-jnp.abs(z)))
    lk = jnp.where(valid, -sp, 0.0)
    return z, sp, valid, lk


def sb_fwd(src, offs):
    lp = src.shape[0]
    nh = N_HEADS
    nq = lp // SB_BLOCK
    scale = HEAD_DIM ** -0.5
    blk = SB_BLOCK

    def body(q_ref, k_ref, v_ref, o_ref, c_ref):
        i = pl.program_id(1)
        q_scaled = q_ref[...] * scale
        upper = (_iota2((blk, blk), 0) > _iota2((blk, blk), 1)).astype(F32)

        def step(it, carry):
            acc, c = carry
            j = i - it
            rows = pl.ds(pl.multiple_of(j * blk, blk), blk)
            z, sp, valid, lk = _sb_scores(q_scaled, k_ref[rows, :], i, j)
            later = _mm_raw(lk, upper, "nn", True) + c
            w = jnp.where(valid, jnp.exp(z - sp + later), 0.0)
            acc = acc + _mm_raw(w, v_ref[rows, :], "nn", False)
            return acc, c + jnp.sum(lk, axis=1, keepdims=True)

        acc, c = lax.fori_loop(0, i + 1, step, (jnp.zeros((blk, HEAD_DIM), F32), jnp.zeros((blk, 1), F32)))
        o_ref[...] = acc
        c_ref[0] = c

    qspec = pl.BlockSpec((blk, HEAD_DIM), lambda h, i: (i, offs[0] + h))
    kspec = pl.BlockSpec((lp, HEAD_DIM), lambda h, i: (0, offs[1] + h))
    vspec = pl.BlockSpec((lp, HEAD_DIM), lambda h, i: (0, offs[2] + h))
    ospec = pl.BlockSpec((blk, HEAD_DIM), lambda h, i: (i, h))
    cspec = pl.BlockSpec((1, blk, 1), lambda h, i: (h, i, 0))
    return pl.pallas_call(
        body, name="sb_fwd", grid=(nh, nq), in_specs=[qspec, kspec, vspec], out_specs=[ospec, cspec],
        out_shape=[jax.ShapeDtypeStruct((lp, nh * HEAD_DIM), F32), jax.ShapeDtypeStruct((nh, lp, 1), F32)],
        compiler_params=pltpu.CompilerParams(dimension_semantics=("arbitrary", "arbitrary"), vmem_limit_bytes=VMEM_LIMIT),
    )(src, src, src)


def sb_bwd(src, offs, csum, d_o):
    lp = src.shape[0]
    nh = N_HEADS
    nq = lp // SB_BLOCK
    scale = HEAD_DIM ** -0.5
    blk = SB_BLOCK

    def body(q_ref, k_ref, v_ref, c_ref, do_ref, dq_ref, dk_ref, dv_ref):
        i = pl.program_id(1)

        @pl.when(i == 0)
        def _():
            dk_ref[...] = jnp.zeros_like(dk_ref)
            dv_ref[...] = jnp.zeros_like(dv_ref)

        q_scaled = q_ref[...] * scale
        d_out = do_ref[...]
        total = c_ref[0]
        lower_incl = (_iota2((blk, blk), 0) <= _iota2((blk, blk), 1)).astype(F32)
        lower = (_iota2((blk, blk), 0) < _iota2((blk, blk), 1)).astype(F32)

        def step(j, carry):
            acc, cp, ep = carry
            rows = pl.ds(pl.multiple_of(j * blk, blk), blk)
            kb = k_ref[rows, :]
            vb = v_ref[rows, :]
            z, sp, valid, lk = _sb_scores(q_scaled, kb, i, j)
            later = total - cp - _mm_raw(lk, lower_incl, "nn", True)
            w = jnp.where(valid, jnp.exp(z - sp + later), 0.0)
            e = w * _mm_raw(d_out, vb, "nt", False)
            before = ep + _mm_raw(e, lower, "nn", True)
            dz = jnp.where(valid, e * jnp.exp(-sp) - before * jnp.exp(z - sp), 0.0)
            dk_ref[rows, :] += _mm_raw(dz, q_scaled, "tn", False)
            dv_ref[rows, :] += _mm_raw(w, d_out, "tn", False)
            acc = acc + _mm_raw(dz, kb, "nn", False)
            return acc, cp + jnp.sum(lk, axis=1, keepdims=True), ep + jnp.sum(e, axis=1, keepdims=True)

        zero_col = jnp.zeros((blk, 1), F32)
        acc, _, _ = lax.fori_loop(0, i + 1, step, (jnp.zeros((blk, HEAD_DIM), F32), zero_col, zero_col))
        dq_ref[...] = acc * scale

    qspec = pl.BlockSpec((blk, HEAD_DIM), lambda h, i: (i, offs[0] + h))
    kspec = pl.BlockSpec((lp, HEAD_DIM), lambda h, i: (0, offs[1] + h))
    vspec = pl.BlockSpec((lp, HEAD_DIM), lambda h, i: (0, offs[2] + h))
    ospec = pl.BlockSpec((blk, HEAD_DIM), lambda h, i: (i, h))
    fullspec = pl.BlockSpec((lp, HEAD_DIM), lambda h, i: (0, h))
    cspec = pl.BlockSpec((1, blk, 1), lambda h, i: (h, i, 0))
    return pl.pallas_call(
        body, name="sb_bwd", grid=(nh, nq), in_specs=[qspec, kspec, vspec, cspec, ospec], out_specs=[ospec, fullspec, fullspec],
        out_shape=[jax.ShapeDtypeStruct((lp, nh * HEAD_DIM), F32)] * 3,
        compiler_params=pltpu.CompilerParams(dimension_semantics=("arbitrary", "arbitrary"), vmem_limit_bytes=VMEM_LIMIT),
    )(src, src, src, csum, d_o)


def _pick(n, target, unit):
    if n <= target:
        return n
    best = None
    for d in range(unit, target + 1, unit):
        if n % d == 0:
            best = d
    assert best is not None, (n, target, unit)
    return best


def matmul(a, b, mode="nn", *, name, bm=1088, bn=640, bk=2176, residual=None, out_dtype=F32):
    if mode == "nn":
        (m, k), n = a.shape, b.shape[1]
    elif mode == "nt":
        (m, k), n = a.shape, b.shape[0]
    else:
        (k, m), n = a.shape, b.shape[1]
    bm = _pick(m, bm, 128 if mode == "tn" else 8)
    bn = _pick(n, bn, 128 if mode != "nt" else 8)
    bk = _pick(k, bk, 128 if mode != "tn" else 8)
    nk = k // bk

    def body(*refs):
        if residual is None:
            a_ref, b_ref, o_ref, acc = refs
            r_ref = None
        else:
            a_ref, b_ref, r_ref, o_ref, acc = refs
        kk = pl.program_id(2)
        part = _mm_raw(a_ref[...], b_ref[...], mode, False)

        @pl.when(kk == 0)
        def _():
            acc[...] = part

        @pl.when(kk > 0)
        def _():
            acc[...] += part

        @pl.when(kk == nk - 1)
        def _():
            res = acc[...]
            if r_ref is not None:
                res = res + r_ref[...]
            o_ref[...] = res.astype(out_dtype)

    a_spec = pl.BlockSpec((bk, bm), lambda i, j, kk: (kk, i)) if mode == "tn" else pl.BlockSpec((bm, bk), lambda i, j, kk: (i, kk))
    b_spec = pl.BlockSpec((bn, bk), lambda i, j, kk: (j, kk)) if mode == "nt" else pl.BlockSpec((bk, bn), lambda i, j, kk: (kk, j))
    o_spec = pl.BlockSpec((bm, bn), lambda i, j, kk: (i, j))
    ins, specs = [a, b], [a_spec, b_spec]
    if residual is not None:
        ins.append(residual)
        specs.append(o_spec)
    return pl.pallas_call(
        body, name=name, grid=(m // bm, n // bn, nk), in_specs=specs, out_specs=o_spec,
        out_shape=jax.ShapeDtypeStruct((m, n), out_dtype),
        scratch_shapes=[pltpu.VMEM((bm, bn), F32)],
        compiler_params=pltpu.CompilerParams(dimension_semantics=("arbitrary", "arbitrary", "arbitrary"), vmem_limit_bytes=VMEM_LIMIT),
    )(*ins)


def _row_specs(rows, params, width, bm):
    row_specs = [pl.BlockSpec((bm, width), (lambda j, i, off=off: (i, off + j))) for _, off in rows]
    par_specs = [pl.BlockSpec((p.shape[0], width) if per_col else p.shape, ((lambda j, i: (0, j)) if per_col else (lambda j, i: (0, 0))))
                 for p, per_col in params]
    return row_specs, par_specs


def rowmap_fwd(name, fn, rows, params, n_out, *, width, ncol, bm, lp):
    row_specs, par_specs = _row_specs(rows, params, width, bm)
    nr = len(rows)

    def body(*refs):
        ins, outs = refs[:nr + len(params)], refs[nr + len(params):]
        row_ids = pl.program_id(1) * bm + _iota2((bm, 1), 0)
        res = fn(row_ids, *[r[...] for r in ins])
        for o_ref, val in zip(outs, res):
            o_ref[...] = val

    o_spec = pl.BlockSpec((bm, width), lambda j, i: (i, j))
    return pl.pallas_call(
        body, name=name, grid=(ncol, lp // bm), in_specs=row_specs + par_specs, out_specs=[o_spec] * n_out,
        out_shape=[jax.ShapeDtypeStruct((lp, ncol * width), F32)] * n_out,
        compiler_params=pltpu.CompilerParams(dimension_semantics=("arbitrary", "arbitrary"), vmem_limit_bytes=VMEM_LIMIT),
    )(*[a for a, _ in rows], *[p for p, _ in params])


def rowmap_bwd(name, fn, rows, params, d_outs, *, width, ncol, bm, lp):
    row_specs, par_specs = _row_specs(rows, params, width, bm)
    nr, npar, nout = len(rows), len(params), len(d_outs)

    def body(*refs):
        ins = refs[:nr + npar]
        dos = refs[nr + npar:nr + npar + nout]
        d_rows = refs[nr + npar + nout:nr + npar + nout + nr]
        d_pars = refs[nr + npar + nout + nr:]
        j, i = pl.program_id(0), pl.program_id(1)
        row_ids = i * bm + _iota2((bm, 1), 0)
        _, pull = jax.vjp(lambda *xs: tuple(fn(row_ids, *xs)), *[r[...] for r in ins])
        grads = pull(tuple(d[...] for d in dos))
        for ref, val in zip(d_rows, grads[:nr]):
            ref[...] = val
        for ref, val, (_, per_col) in zip(d_pars, grads[nr:], params):
            first = (i == 0) if per_col else ((i == 0) & (j == 0))

            @pl.when(first)
            def _(ref=ref, val=val):
                ref[...] = val

            @pl.when(jnp.logical_not(first))
            def _(ref=ref, val=val):
                ref[...] += val

    o_spec = pl.BlockSpec((bm, width), lambda j, i: (i, j))
    res = pl.pallas_call(
        body, name=name, grid=(ncol, lp // bm), in_specs=row_specs + par_specs + [o_spec] * nout,
        out_specs=[o_spec] * nr + par_specs,
        out_shape=[jax.ShapeDtypeStruct((lp, ncol * width), F32)] * nr + [jax.ShapeDtypeStruct(p.shape, F32) for p, _ in params],
        compiler_params=pltpu.CompilerParams(dimension_semantics=("arbitrary", "arbitrary"), vmem_limit_bytes=VMEM_LIMIT),
    )(*[a for a, _ in rows], *[p for p, _ in params], *d_outs)
    return res[:nr], res[nr:]


def _silu(x):
    return x * jax.nn.sigmoid(x)


def _softplus(x):
    return jnp.maximum(x, 0.0) + jnp.log(1.0 + jnp.exp(-jnp.abs(x)))


def _real_rows(row_ids):
    return (row_ids >= PAD).astype(F32)


def _f_rmsnorm(row_ids, h, g):
    return (h * lax.rsqrt(jnp.mean(h * h, axis=-1, keepdims=True) + RMS_EPS) * g,)


def _f_small_gates(row_ids, small, bias, a_log):
    lane = _iota2(small.shape, 1)
    t = small + bias
    sp = _softplus(t)
    coef = -jnp.exp(a_log)
    keep = _real_rows(row_ids)
    first = jnp.where(lane < 8, jax.nn.sigmoid(t), jnp.where(lane < 16, coef * sp, jnp.where(lane < 48, sp, 0.0)))
    second = jnp.where((lane >= 8) & (lane < 48), coef * sp, 0.0)
    return first * keep, second * keep


def _f_gate_silu(row_ids, o, z):
    return (o * _silu(z),)


def _f_head_norm_gate(row_ids, o, z, g):
    return (o * lax.rsqrt(jnp.mean(o * o, axis=-1, keepdims=True) + RMS_EPS) * g * _silu(z),)


def _f_ssm_out(row_ids, y, xh, z, d_skip, g):
    t = (y + d_skip * xh) * _silu(z)
    return (t * lax.rsqrt(jnp.mean(t * t, axis=-1, keepdims=True) + RMS_EPS) * g,)


def _f_merge(row_ids, pa, pb, pc, ga, gb, gc):
    return (jax.nn.sigmoid(ga) * pa + jax.nn.sigmoid(gb) * pb + jax.nn.sigmoid(gc) * pc,)


def _shift_rows(x, s):
    s = s % x.shape[0]
    return x if s == 0 else pltpu.roll(x, s, 0)


def _conv_pre(x, w, b):
    pre = b
    for kk in range(CONV_K):
        pre = pre + w[kk:kk + 1, :] * _shift_rows(x, CONV_K - 1 - kk)
    return pre


def _conv_post(pre, l2_flag, keep):
    act = _silu(pre)
    nrm = act * lax.rsqrt(jnp.sum(act * act, axis=-1, keepdims=True) + L2_EPS)
    return (l2_flag * nrm + (1.0 - l2_flag) * act) * keep


def conv_fwd(name, src, col_off, w, b, n_l2):
    lp, ch = src.shape[0], w.shape[1]

    def body(x_ref, w_ref, b_ref, o_ref):
        l2_flag = (pl.program_id(0) < n_l2).astype(F32)
        keep = _real_rows(_iota2((lp, 1), 0))
        o_ref[...] = _conv_post(_conv_pre(x_ref[...], w_ref[...], b_ref[...]), l2_flag, keep)

    return pl.pallas_call(
        body, name=name, grid=(ch // HEAD_DIM,),
        in_specs=[pl.BlockSpec((lp, HEAD_DIM), lambda j: (0, col_off + j)), pl.BlockSpec((CONV_K, HEAD_DIM), lambda j: (0, j)),
                  pl.BlockSpec((1, HEAD_DIM), lambda j: (0, j))],
        out_specs=pl.BlockSpec((lp, HEAD_DIM), lambda j: (0, j)),
        out_shape=jax.ShapeDtypeStruct((lp, ch), F32),
        compiler_params=pltpu.CompilerParams(dimension_semantics=("arbitrary",), vmem_limit_bytes=VMEM_LIMIT),
    )(src, w, b)


def conv_bwd(name, src, col_off, w, b, n_l2, d_out):
    lp, ch = src.shape[0], w.shape[1]

    def body(x_ref, w_ref, b_ref, do_ref, dx_ref, dw_ref, db_ref):
        l2_flag = (pl.program_id(0) < n_l2).astype(F32)
        keep = _real_rows(_iota2((lp, 1), 0))
        x, wv = x_ref[...], w_ref[...]
        pre = _conv_pre(x, wv, b_ref[...])
        _, pull = jax.vjp(lambda p: _conv_post(p, l2_flag, keep), pre)
        (d_pre,) = pull(do_ref[...])
        dx = jnp.zeros_like(x)
        for kk in range(CONV_K):
            s = CONV_K - 1 - kk
            dx = dx + wv[kk:kk + 1, :] * _shift_rows(d_pre, -s)
            dw_ref[kk:kk + 1, :] = jnp.sum(d_pre * _shift_rows(x, s), axis=0, keepdims=True)
        dx_ref[...] = dx * keep
        db_ref[...] = jnp.sum(d_pre, axis=0, keepdims=True)

    seq = pl.BlockSpec((lp, HEAD_DIM), lambda j: (0, j))
    wspec = pl.BlockSpec((CONV_K, HEAD_DIM), lambda j: (0, j))
    bspec = pl.BlockSpec((1, HEAD_DIM), lambda j: (0, j))
    return pl.pallas_call(
        body, name=name, grid=(ch // HEAD_DIM,),
        in_specs=[pl.BlockSpec((lp, HEAD_DIM), lambda j: (0, col_off + j)), wspec, bspec, seq],
        out_specs=[seq, wspec, bspec],
        out_shape=[jax.ShapeDtypeStruct((lp, ch), F32), jax.ShapeDtypeStruct(w.shape, F32), jax.ShapeDtypeStruct(b.shape, F32)],
        compiler_params=pltpu.CompilerParams(dimension_semantics=("arbitrary",), vmem_limit_bytes=VMEM_LIMIT),
    )(src, w, b, d_out)


def loss_head(h, target, g):
    lp, d = h.shape
    bm = SB_BLOCK
    first = (PAD + N_META) // bm

    def body(h_ref, t_ref, g_ref, loss_ref, dh_ref, dg_ref):
        i = pl.program_id(0)
        keep = (i >= first).astype(F32)

        def f(hv, gv):
            y = hv * lax.rsqrt(jnp.mean(hv * hv, axis=-1, keepdims=True) + RMS_EPS) * gv
            err = y - t_ref[...]
            return 0.5 * jnp.sum(jnp.mean(err * err, axis=-1, keepdims=True), axis=0, keepdims=True) * keep

        val, pull = jax.vjp(f, h_ref[...], g_ref[...])
        dh, dg = pull(jnp.ones((1, 1), F32))
        dh_ref[...] = dh

        @pl.when(i == 0)
        def _():
            loss_ref[...] = val
            dg_ref[...] = dg

        @pl.when(i > 0)
        def _():
            loss_ref[...] += val
            dg_ref[...] += dg

    row = pl.BlockSpec((bm, d), lambda i: (i, 0))
    return pl.pallas_call(
        body, name="loss_head", grid=(lp // bm,),
        in_specs=[row, pl.BlockSpec((bm, d), lambda i: (jnp.maximum(i - first, 0), 0)), pl.BlockSpec((1, d), lambda i: (0, 0))],
        out_specs=[pl.BlockSpec((1, 1), lambda i: (0, 0)), row, pl.BlockSpec((1, d), lambda i: (0, 0))],
        out_shape=[jax.ShapeDtypeStruct((1, 1), F32), jax.ShapeDtypeStruct((lp, d), F32), jax.ShapeDtypeStruct((1, d), F32)],
        compiler_params=pltpu.CompilerParams(dimension_semantics=("arbitrary",)),
    )(h, target, g)


ADAM_LR, ADAM_B1, ADAM_B2, ADAM_EPS, ADAM_WD, ADAM_STEP = 0.001, 0.9, 0.999, 1e-08, 0.01, 10


def adamw(name, w, g, m, v):
    rows, cols = w.shape
    br = _pick(rows, 128, 8)

    def body(w_ref, g_ref, m_ref, v_ref, d_ref, nm_ref, nv_ref):
        gv = g_ref[...]
        nm = ADAM_B1 * m_ref[...] + (1.0 - ADAM_B1) * gv
        nv = ADAM_B2 * v_ref[...] + (1.0 - ADAM_B2) * (gv * gv)
        m_hat = nm / (1.0 - ADAM_B1 ** ADAM_STEP)
        v_hat = nv / (1.0 - ADAM_B2 ** ADAM_STEP)
        d_ref[...] = -ADAM_LR * (m_hat / (jnp.sqrt(v_hat) + ADAM_EPS) + ADAM_WD * w_ref[...])
        nm_ref[...] = nm
        nv_ref[...] = nv

    spec = pl.BlockSpec((br, cols), lambda i: (i, 0))
    return pl.pallas_call(
        body, name=name, grid=(rows // br,), in_specs=[spec] * 4, out_specs=[spec] * 3,
        out_shape=[jax.ShapeDtypeStruct(w.shape, F32)] * 3,
        compiler_params=pltpu.CompilerParams(dimension_semantics=("arbitrary",), vmem_limit_bytes=VMEM_LIMIT),
    )(w, g, m, v)


def add_pair(name, a, b, out_dtype):
    rows, cols = a.shape
    br = _pick(rows, 512, 32)

    def body(a_ref, b_ref, o_ref):
        o_ref[...] = (a_ref[...].astype(F32) + b_ref[...].astype(F32)).astype(out_dtype)

    spec = pl.BlockSpec((br, cols), lambda i: (i, 0))
    return pl.pallas_call(body, name=name, grid=(rows // br,), in_specs=[spec, spec], out_specs=spec,
                          out_shape=jax.ShapeDtypeStruct(a.shape, out_dtype),
                          compiler_params=pltpu.CompilerParams(dimension_semantics=("arbitrary",)))(a, b)


def add_slots(name, buf):
    _, rows, cols = buf.shape
    br = _pick(rows, 512, 32)

    def body(b_ref, o_ref):
        acc = b_ref[0].astype(F32)
        for s in range(1, 4):
            acc = acc + b_ref[s].astype(F32)
        o_ref[...] = acc

    return pl.pallas_call(body, name=name, grid=(rows // br,),
                          in_specs=[pl.BlockSpec((4, br, cols), lambda i: (0, i, 0))], out_specs=pl.BlockSpec((br, cols), lambda i: (i, 0)),
                          out_shape=jax.ShapeDtypeStruct((rows, cols), F32),
                          compiler_params=pltpu.CompilerParams(dimension_semantics=("arbitrary",)))(buf)


MESH = pl.DeviceIdType.MESH
ANY = pl.BlockSpec(memory_space=pl.ANY)


def _place():
    x, y, c = lax.axis_index("x"), lax.axis_index("y"), lax.axis_index("c")
    return x, y, c, [(1 - x, y), (x, 1 - y), (1 - x, 1 - y)]


def gather_packs(name, pack):
    rows, cols = pack.shape
    half = rows // 2

    def body(p_ref, o_ref, send_sems, recv_sems, local_sem):
        x, y, c, chips = _place()
        mine = 2 * x + y
        my_rows = pl.ds(pl.multiple_of(c * half, 32), half)
        other_rows = pl.ds(pl.multiple_of((1 - c) * half, 32), half)
        local = pltpu.make_async_copy(p_ref, o_ref.at[mine], local_sem)
        local.start()

        def copy(k, src, slot, rws, to):
            return pltpu.make_async_remote_copy(src_ref=src, dst_ref=o_ref.at[slot, rws, :], send_sem=send_sems.at[k],
                                                recv_sem=recv_sems.at[k], device_id=to, device_id_type=MESH)

        first = [copy(j, p_ref.at[my_rows, :], mine, my_rows, (cx, cy, c)) for j, (cx, cy) in enumerate(chips)]
        for cp in first:
            cp.start()
        passed = []
        for j, (cx, cy) in enumerate(chips):
            slot = 2 * cx + cy
            copy(j, p_ref.at[my_rows, :], slot, my_rows, (cx, cy, c)).wait_recv()
            fwd = copy(3 + j, o_ref.at[slot, my_rows, :], slot, my_rows, (x, y, 1 - c))
            fwd.start()
            passed.append(fwd)
        for j, (cx, cy) in enumerate(chips):
            copy(3 + j, p_ref.at[my_rows, :], 2 * cx + cy, other_rows, (x, y, 1 - c)).wait_recv()
        for cp in first + passed:
            cp.wait_send()
        local.wait()

    return pl.pallas_call(
        body, name=name, in_specs=[ANY], out_specs=ANY, out_shape=jax.ShapeDtypeStruct((4, rows, cols), pack.dtype),
        scratch_shapes=[pltpu.SemaphoreType.DMA((6,)), pltpu.SemaphoreType.DMA((6,)), pltpu.SemaphoreType.DMA],
    )(pack)


def pair_split(name, g):
    _, rows, cols = g.shape
    half = rows // 2

    def body(g_ref, mine_ref, theirs_ref, send_sem, recv_sem, local_sem):
        x, y, c, _ = _place()
        my_rows = pl.ds(pl.multiple_of(c * half, 16), half)
        other_rows = pl.ds(pl.multiple_of((1 - c) * half, 16), half)
        local = pltpu.make_async_copy(g_ref.at[:, my_rows, :], mine_ref, local_sem)
        local.start()
        cp = pltpu.make_async_remote_copy(src_ref=g_ref.at[:, other_rows, :], dst_ref=theirs_ref, send_sem=send_sem,
                                          recv_sem=recv_sem, device_id=(x, y, 1 - c), device_id_type=MESH)
        cp.start()
        cp.wait()
        local.wait()

    shape = jax.ShapeDtypeStruct((4, half, cols), g.dtype)
    return pl.pallas_call(
        body, name=name, in_specs=[ANY], out_specs=[ANY, ANY], out_shape=[shape, shape],
        scratch_shapes=[pltpu.SemaphoreType.DMA, pltpu.SemaphoreType.DMA, pltpu.SemaphoreType.DMA],
    )(g)


def chip_exchange(name, a):
    def body(a_ref, o_ref, send_sems, recv_sems, local_sem):
        x, y, c, chips = _place()
        mine = 2 * x + y
        local = pltpu.make_async_copy(a_ref.at[mine], o_ref.at[mine], local_sem)
        local.start()
        copies = [pltpu.make_async_remote_copy(src_ref=a_ref.at[2 * cx + cy], dst_ref=o_ref.at[mine], send_sem=send_sems.at[j],
                                               recv_sem=recv_sems.at[j], device_id=(cx, cy, c), device_id_type=MESH)
                  for j, (cx, cy) in enumerate(chips)]
        for cp in copies:
            cp.start()
        for cp in copies:
            cp.wait()
        local.wait()

    return pl.pallas_call(
        body, name=name, in_specs=[ANY], out_specs=ANY, out_shape=jax.ShapeDtypeStruct(a.shape, a.dtype),
        scratch_shapes=[pltpu.SemaphoreType.DMA((3,)), pltpu.SemaphoreType.DMA((3,)), pltpu.SemaphoreType.DMA],
    )(a)


def pair_join(name, red):
    half, cols = red.shape

    def body(r_ref, o_ref, send_sem, recv_sem, local_sem):
        x, y, c, _ = _place()
        my_rows = pl.ds(pl.multiple_of(c * half, 8), half)
        local = pltpu.make_async_copy(r_ref, o_ref.at[my_rows, :], local_sem)
        local.start()
        cp = pltpu.make_async_remote_copy(src_ref=r_ref, dst_ref=o_ref.at[my_rows, :], send_sem=send_sem, recv_sem=recv_sem,
                                          device_id=(x, y, 1 - c), device_id_type=MESH)
        cp.start()
        cp.wait()
        local.wait()

    return pl.pallas_call(
        body, name=name, in_specs=[ANY], out_specs=ANY, out_shape=jax.ShapeDtypeStruct((2 * half, cols), red.dtype),
        scratch_shapes=[pltpu.SemaphoreType.DMA, pltpu.SemaphoreType.DMA, pltpu.SemaphoreType.DMA],
    )(red)


def reduce_scatter(tag, g, transit):
    _, rows, cols = g.shape
    mine, theirs = pair_split(tag + "_pair_split", g)
    part = add_pair(tag + "_pair_add", mine.reshape(-1, cols), theirs.reshape(-1, cols), transit).reshape(4, rows // 2, cols)
    got = chip_exchange(tag + "_chip_exchange", part)
    red = add_slots(tag + "_chip_add", got)
    return pair_join(tag + "_pair_join", red)


D_IN = 15920
D_PROJ = 16000
_SEGMENTS = ((0, 8192), (8208, 12816), (12848, 15920), (8192, 8208), (12816, 12848))
OFF_SB_Z, OFF_GDN_QKV, OFF_GDN_Z, OFF_SSM_Z, OFF_SSM_XBC, OFF_GATES, OFF_SMALL = 3072, 4096, 7168, 8192, 10240, 12800, 15872
PACK_C = 1024
WEIGHTS = ("meta_tokens", "norm_g", "w_in", "gdn_conv_w", "gdn_a_log", "gdn_dt_bias", "gdn_norm_g", "ssm_conv_w", "ssm_conv_b",
           "ssm_a_log", "ssm_dt_bias", "ssm_d", "ssm_norm_g", "w_branch_a", "w_branch_b", "w_branch_c", "w_out", "final_norm_g")
SHARDED = ("w_in", "w_branch_a", "w_branch_b", "w_branch_c", "w_out", "gdn_conv_w", "ssm_conv_w", "meta_tokens")
SHARD_AXIS = {"w_in": 2, "w_branch_a": 1, "w_branch_b": 1, "w_branch_c": 1, "w_out": 1, "gdn_conv_w": 2, "ssm_conv_w": 2, "meta_tokens": 1}
MATMUL_WEIGHTS = ("w_in", "w_branch_a", "w_branch_b", "w_branch_c", "w_out")
REPLICATED = tuple(n for n in WEIGHTS if n not in SHARDED)


def _regroup(w):
    parts = [w[..., a:b] for a, b in _SEGMENTS]
    return jnp.concatenate(parts + [jnp.zeros(w.shape[:-1] + (D_PROJ - D_IN,), w.dtype)], axis=-1)


def _ungroup(g):
    starts, pos = [], 0
    for a, b in _SEGMENTS:
        starts.append(pos)
        pos += b - a
    order = sorted(range(len(_SEGMENTS)), key=lambda i: _SEGMENTS[i][0])
    return jnp.concatenate([g[..., starts[i]:starts[i] + _SEGMENTS[i][1] - _SEGMENTS[i][0]] for i in order], axis=-1)


def _pack(parts, row_unit=64):
    n = sum(p.shape[0] for p in parts)
    rows = -(-n // (PACK_C * row_unit)) * row_unit
    flat = jnp.concatenate(list(parts) + [jnp.zeros((rows * PACK_C - n,), parts[0].dtype)])
    return flat.reshape(rows, PACK_C)


def _unpack(buf, shapes):
    flat, out, pos = buf.reshape(-1), [], 0
    for shp in shapes:
        n = math.prod(shp)
        out.append(flat[pos:pos + n].reshape(shp))
        pos += n
    return out


def _as_bf16_words(a):
    return lax.bitcast_convert_type(a, BF16).reshape(-1)


def _gather_weights(w):
    parts = [w[n].astype(BF16).reshape(-1) if n in MATMUL_WEIGHTS else _as_bf16_words(w[n]) for n in SHARDED]
    got = gather_packs("gather_weights", _pack(parts))
    shapes = [w[n].shape if n in MATMUL_WEIGHTS else w[n].shape + (2,) for n in SHARDED]
    per_chip = [_unpack(got[s], shapes) for s in range(4)]
    full = {}
    for i, n in enumerate(SHARDED):
        pieces = [per_chip[s][i] if n in MATMUL_WEIGHTS else lax.bitcast_convert_type(per_chip[s][i], F32) for s in range(4)]
        full[n] = jnp.concatenate(pieces, axis=SHARD_AXIS[n])
    return full


def _shard(a, axis, s):
    size = a.shape[axis] // 4
    return lax.slice_in_dim(a, s * size, (s + 1) * size, axis=axis)


def _reduce_grads(g):
    slots = [_pack([_shard(g[n], SHARD_AXIS[n], s).astype(BF16).reshape(-1) for n in SHARDED]) for s in range(4)]
    red = reduce_scatter("grads", jnp.stack(slots), BF16)
    shapes = [_shard(g[n], SHARD_AXIS[n], 0).shape for n in SHARDED]
    out = dict(zip(SHARDED, _unpack(red, shapes)))
    small = _pack([g[n].reshape(-1) for n in REPLICATED], row_unit=32)
    small_red = reduce_scatter("small_grads", jnp.broadcast_to(small[None], (4,) + small.shape), F32)
    return out, small_red


def _layer_params(w, full, l):
    lane = lambda v, lo: jnp.pad(v, (lo, HEAD_DIM - lo - v.shape[0]))[None]
    return dict(
        norm_g=w["norm_g"][l][None], wp=_regroup(full["w_in"][l]),
        gdn_conv_w=full["gdn_conv_w"][l], gdn_conv_b=jnp.zeros((1, 3 * N_HEADS * HEAD_DIM), F32),
        ssm_conv_w=full["ssm_conv_w"][l], ssm_conv_b=w["ssm_conv_b"][l][None],
        bias_vec=lane(w["gdn_dt_bias"][l], 8) + lane(w["ssm_dt_bias"][l], 16),
        alog_vec=lane(w["gdn_a_log"][l], 8) + lane(w["ssm_a_log"][l], 16),
        gdn_norm_g=w["gdn_norm_g"][l][None], d_skip=jnp.repeat(w["ssm_d"][l], SSM_P)[None], ssm_norm_g=w["ssm_norm_g"][l][None],
        wa=full["w_branch_a"][l], wb=full["w_branch_b"][l], wc=full["w_branch_c"][l], wo=full["w_out"][l])


def _heads_major(cols):
    return cols.T[:, :, None]


def _groups_major(cols):
    return jnp.transpose(cols.reshape(cols.shape[0], SSM_GROUPS, SSM_HG), (1, 0, 2))


def _layer_fwd(h, p):
    lp = h.shape[0]
    bm = _pick(lp, 272, 8)
    kw = dict(bm=bm, lp=lp)
    (u,) = rowmap_fwd("rms_fwd", _f_rmsnorm, [(h, 0)], [(p["norm_g"], False)], 1, width=D_MODEL, ncol=1, **kw)
    proj = matmul(u, p["wp"], "nn", name="proj", bm=lp, bn=640)
    o_a_raw, csum = sb_fwd(proj, (0, N_HEADS, 2 * N_HEADS))
    qkv = conv_fwd("gdn_conv_fwd", proj, OFF_GDN_QKV // HEAD_DIM, p["gdn_conv_w"], p["gdn_conv_b"], 2 * N_HEADS)
    first, second = rowmap_fwd("gates_fwd", _f_small_gates, [(proj, OFF_SMALL // HEAD_DIM)],
                               [(p["bias_vec"], False), (p["alog_vec"], False)], 2, width=HEAD_DIM, ncol=1, **kw)
    beta3, g3 = _heads_major(first[:, 0:8]), _heads_major(first[:, 8:16])
    dt3, la3 = _groups_major(first[:, 16:48]), _groups_major(second[:, 16:48])
    o_b_raw, gdn_states = gdn_fwd(qkv, g3, beta3)
    xbc = conv_fwd("ssm_conv_fwd", proj, OFF_SSM_XBC // HEAD_DIM, p["ssm_conv_w"], p["ssm_conv_b"], 0)
    y_raw, ssd_states = ssd_fwd(xbc, dt3, la3)
    (o_a,) = rowmap_fwd("gate_a_fwd", _f_gate_silu, [(o_a_raw, 0), (proj, OFF_SB_Z // 1024)], [], 1, width=1024, ncol=1, **kw)
    (o_b,) = rowmap_fwd("gate_b_fwd", _f_head_norm_gate, [(o_b_raw, 0), (proj, OFF_GDN_Z // HEAD_DIM)], [(p["gdn_norm_g"], False)], 1,
                        width=HEAD_DIM, ncol=N_HEADS, **kw)
    (o_c,) = rowmap_fwd("gate_c_fwd", _f_ssm_out, [(y_raw, 0), (xbc, 0), (proj, OFF_SSM_Z // 1024)],
                        [(p["d_skip"], True), (p["ssm_norm_g"], True)], 1, width=1024, ncol=SSM_GROUPS, **kw)
    pa = matmul(o_a, p["wa"], "nn", name="branch_a", bm=lp // 2, bn=512)
    pb = matmul(o_b, p["wb"], "nn", name="branch_b", bm=lp // 2, bn=512)
    pc = matmul(o_c, p["wc"], "nn", name="branch_c", bm=lp // 2, bn=512)
    merge_rows = [(pa, 0), (pb, 0), (pc, 0)] + [(proj, OFF_GATES // 512 + 2 * i) for i in range(3)]
    (merged,) = rowmap_fwd("merge_fwd", _f_merge, merge_rows, [], 1, width=512, ncol=2, **kw)
    h_out = matmul(merged, p["wo"], "nn", name="out_proj", bm=lp, bn=512, residual=h)
    saved = dict(h=h, u=u, proj=proj, csum=csum, qkv=qkv, beta3=beta3, g3=g3, dt3=dt3, la3=la3, o_a_raw=o_a_raw, o_b_raw=o_b_raw,
                 gdn_states=gdn_states, xbc=xbc, y_raw=y_raw, ssd_states=ssd_states, o_a=o_a, o_b=o_b, o_c=o_c, pa=pa, pb=pb, pc=pc,
                 merged=merged)
    return h_out, saved


def _layer_bwd(d_h, p, s):
    lp = d_h.shape[0]
    bm = _pick(lp, 272, 8)
    kw = dict(bm=bm, lp=lp)
    proj = s["proj"]
    g = {}
    d_merged = matmul(d_h, p["wo"], "nt", name="d_merged", bm=lp, bn=512)
    g["w_out"] = matmul(s["merged"], d_h, "tn", name="g_w_out", bm=512, bn=1024, bk=lp)
    merge_rows = [(s["pa"], 0), (s["pb"], 0), (s["pc"], 0)] + [(proj, OFF_GATES // 512 + 2 * i) for i in range(3)]
    (d_pa, d_pb, d_pc, d_ga, d_gb, d_gc), _ = rowmap_bwd("merge_bwd", _f_merge, merge_rows, [], [d_merged], width=512, ncol=2, **kw)
    g["w_branch_a"] = matmul(s["o_a"], d_pa, "tn", name="g_w_a", bm=512, bn=1024, bk=lp)
    g["w_branch_b"] = matmul(s["o_b"], d_pb, "tn", name="g_w_b", bm=512, bn=1024, bk=lp)
    g["w_branch_c"] = matmul(s["o_c"], d_pc, "tn", name="g_w_c", bm=512, bn=1024, bk=lp)
    d_oa = matmul(d_pa, p["wa"], "nt", name="d_o_a", bm=lp, bn=512)
    d_ob = matmul(d_pb, p["wb"], "nt", name="d_o_b", bm=lp, bn=512)
    d_oc = matmul(d_pc, p["wc"], "nt", name="d_o_c", bm=lp, bn=512)
    (d_oa_raw, d_sbz), _ = rowmap_bwd("gate_a_bwd", _f_gate_silu, [(s["o_a_raw"], 0), (proj, OFF_SB_Z // 1024)], [], [d_oa],
                                      width=1024, ncol=1, **kw)
    (d_ob_raw, d_gdz), (g["gdn_norm_g"],) = rowmap_bwd(
        "gate_b_bwd", _f_head_norm_gate, [(s["o_b_raw"], 0), (proj, OFF_GDN_Z // HEAD_DIM)], [(p["gdn_norm_g"], False)], [d_ob],
        width=HEAD_DIM, ncol=N_HEADS, **kw)
    (d_y, d_xh, d_ssz), (g_dskip, g["ssm_norm_g"]) = rowmap_bwd(
        "gate_c_bwd", _f_ssm_out, [(s["y_raw"], 0), (s["xbc"], 0), (proj, OFF_SSM_Z // 1024)],
        [(p["d_skip"], True), (p["ssm_norm_g"], True)], [d_oc], width=1024, ncol=SSM_GROUPS, **kw)
    g["gdn_norm_g"], g["ssm_norm_g"] = g["gdn_norm_g"][0], g["ssm_norm_g"][0]
    g["ssm_d"] = g_dskip.reshape(SSM_HEADS, SSM_P).sum(axis=1)
    d_q, d_k, d_v = sb_bwd(proj, (0, N_HEADS, 2 * N_HEADS), s["csum"], d_oa_raw)
    d_cq, d_ck, d_cv, d_g3, d_beta3 = gdn_bwd(s["qkv"], s["g3"], s["beta3"], s["gdn_states"], d_ob_raw)
    d_x, d_b, d_c, d_dt3, d_la3 = ssd_bwd(s["xbc"], s["dt3"], s["la3"], s["ssd_states"], d_y)
    d_gdqkv, g["gdn_conv_w"], _ = conv_bwd("gdn_conv_bwd", proj, OFF_GDN_QKV // HEAD_DIM, p["gdn_conv_w"], p["gdn_conv_b"], 2 * N_HEADS,
                                           jnp.concatenate([d_cq, d_ck, d_cv], axis=1))
    d_xbc, g["ssm_conv_w"], g_cb = conv_bwd("ssm_conv_bwd", proj, OFF_SSM_XBC // HEAD_DIM, p["ssm_conv_w"], p["ssm_conv_b"], 0,
                                            jnp.concatenate([d_x + d_xh, d_b, d_c], axis=1))
    g["ssm_conv_b"] = g_cb[0]
    rows_of = lambda t3: jnp.transpose(t3, (1, 0, 2)).reshape(lp, SSM_HEADS)
    zeros = lambda n: jnp.zeros((lp, n), F32)
    d_first = jnp.concatenate([d_beta3[:, :, 0].T, d_g3[:, :, 0].T, rows_of(d_dt3), zeros(HEAD_DIM - 48)], axis=1)
    d_second = jnp.concatenate([zeros(16), rows_of(d_la3), zeros(HEAD_DIM - 48)], axis=1)
    (d_small,), (g_bias, g_alog) = rowmap_bwd("gates_bwd", _f_small_gates, [(proj, OFF_SMALL // HEAD_DIM)],
                                              [(p["bias_vec"], False), (p["alog_vec"], False)], [d_first, d_second],
                                              width=HEAD_DIM, ncol=1, **kw)
    g["gdn_dt_bias"], g["ssm_dt_bias"] = g_bias[0, 8:16], g_bias[0, 16:48]
    g["gdn_a_log"], g["ssm_a_log"] = g_alog[0, 8:16], g_alog[0, 16:48]
    d_proj = jnp.concatenate([d_q, d_k, d_v, d_sbz, d_gdqkv, d_gdz, d_ssz, d_xbc, d_ga, d_gb, d_gc, d_small], axis=1)
    g["w_in"] = _ungroup(matmul(s["u"], d_proj, "tn", name="g_w_in", bm=1024, bn=640, bk=lp))
    d_u = matmul(d_proj, p["wp"], "nt", name="d_u", bm=lp // 2, bn=1024, bk=640)
    (d_hn,), (g_norm,) = rowmap_bwd("rms_bwd", _f_rmsnorm, [(s["h"], 0)], [(p["norm_g"], False)], [d_u], width=D_MODEL, ncol=1, **kw)
    g["norm_g"] = g_norm[0]
    return d_h + d_hn, g


def kernel(x, meta_tokens, norm_g, w_in, gdn_conv_w, gdn_a_log, gdn_dt_bias, gdn_norm_g, ssm_conv_w, ssm_conv_b, ssm_a_log, ssm_dt_bias, ssm_d, ssm_norm_g, w_branch_a, w_branch_b, w_branch_c, w_out, final_norm_g, loss_target, m_meta_tokens, m_norm_g, m_w_in, m_gdn_conv_w, m_gdn_a_log, m_gdn_dt_bias, m_gdn_norm_g, m_ssm_conv_w, m_ssm_conv_b, m_ssm_a_log, m_ssm_dt_bias, m_ssm_d, m_ssm_norm_g, m_w_branch_a, m_w_branch_b, m_w_branch_c, m_w_out, m_final_norm_g, v_meta_tokens, v_norm_g, v_w_in, v_gdn_conv_w, v_gdn_a_log, v_gdn_dt_bias, v_gdn_norm_g, v_ssm_conv_w, v_ssm_conv_b, v_ssm_a_log, v_ssm_dt_bias, v_ssm_d, v_ssm_norm_g, v_w_branch_a, v_w_branch_b, v_w_branch_c, v_w_out, v_final_norm_g):
    w = dict(meta_tokens=meta_tokens, norm_g=norm_g, w_in=w_in, gdn_conv_w=gdn_conv_w, gdn_a_log=gdn_a_log, gdn_dt_bias=gdn_dt_bias,
             gdn_norm_g=gdn_norm_g, ssm_conv_w=ssm_conv_w, ssm_conv_b=ssm_conv_b, ssm_a_log=ssm_a_log, ssm_dt_bias=ssm_dt_bias,
             ssm_d=ssm_d, ssm_norm_g=ssm_norm_g, w_branch_a=w_branch_a, w_branch_b=w_branch_b, w_branch_c=w_branch_c, w_out=w_out,
             final_norm_g=final_norm_g)
    m = dict(meta_tokens=m_meta_tokens, norm_g=m_norm_g, w_in=m_w_in, gdn_conv_w=m_gdn_conv_w, gdn_a_log=m_gdn_a_log,
             gdn_dt_bias=m_gdn_dt_bias, gdn_norm_g=m_gdn_norm_g, ssm_conv_w=m_ssm_conv_w, ssm_conv_b=m_ssm_conv_b,
             ssm_a_log=m_ssm_a_log, ssm_dt_bias=m_ssm_dt_bias, ssm_d=m_ssm_d, ssm_norm_g=m_ssm_norm_g, w_branch_a=m_w_branch_a,
             w_branch_b=m_w_branch_b, w_branch_c=m_w_branch_c, w_out=m_w_out, final_norm_g=m_final_norm_g)
    v = dict(meta_tokens=v_meta_tokens, norm_g=v_norm_g, w_in=v_w_in, gdn_conv_w=v_gdn_conv_w, gdn_a_log=v_gdn_a_log,
             gdn_dt_bias=v_gdn_dt_bias, gdn_norm_g=v_gdn_norm_g, ssm_conv_w=v_ssm_conv_w, ssm_conv_b=v_ssm_conv_b,
             ssm_a_log=v_ssm_a_log, ssm_dt_bias=v_ssm_dt_bias, ssm_d=v_ssm_d, ssm_norm_g=v_ssm_norm_g, w_branch_a=v_w_branch_a,
             w_branch_b=v_w_branch_b, w_branch_c=v_w_branch_c, w_out=v_w_out, final_norm_g=v_final_norm_g)
    depth = norm_g.shape[0]
    full = _gather_weights(w)
    params = [_layer_params(w, full, l) for l in range(depth)]

    h = jnp.concatenate([jnp.zeros((PAD, D_MODEL), F32), full["meta_tokens"], x[0]], axis=0)
    saved = []
    for l in range(depth):
        h, s = _layer_fwd(h, params[l])
        saved.append(s)
    loss, d_h, g_final = loss_head(h, loss_target[0], final_norm_g[None])
    layer_grads = [None] * depth
    for l in reversed(range(depth)):
        d_h, layer_grads[l] = _layer_bwd(d_h, params[l], saved[l])
    grads = {n: jnp.stack([layer_grads[l][n] for l in range(depth)]) for n in WEIGHTS if n not in ("meta_tokens", "final_norm_g")}
    grads["meta_tokens"] = d_h[PAD:PAD + N_META]
    grads["final_norm_g"] = g_final[0]
    grad_x = d_h[PAD + N_META:][None]

    red, small_red = _reduce_grads(grads)
    delta, new_m, new_v = {}, {}, {}
    for n in SHARDED:
        two_d = lambda a: a.reshape(-1, a.shape[-1])
        d2, m2, v2 = adamw("adamw_" + n, two_d(w[n]), two_d(red[n]), two_d(m[n]), two_d(v[n]))
        delta[n], new_m[n], new_v[n] = d2.reshape(w[n].shape), m2.reshape(w[n].shape), v2.reshape(w[n].shape)
    pack_small = lambda d: _pack([d[n].reshape(-1) for n in REPLICATED], row_unit=32)
    small = adamw("adamw_small", pack_small(w), small_red, pack_small(m), pack_small(v))
    shapes = [w[n].shape for n in REPLICATED]
    red.update(zip(REPLICATED, _unpack(small_red, shapes)))
    for d, buf in zip((delta, new_m, new_v), small):
        d.update(zip(REPLICATED, _unpack(buf, shapes)))
    total_loss = lax.psum(loss[0, 0], ("x", "y", "c"))
    return (total_loss, grad_x, *[red[n] for n in WEIGHTS], *[delta[n] for n in WEIGHTS], *[new_m[n] for n in WEIGHTS],
            *[new_v[n] for n in WEIGHTS])
```

```python
import functools
import math

import jax
import jax.numpy as jnp
from jax import lax
from jax.experimental import pallas as pl
from jax.experimental.pallas import tpu as pltpu

F32 = jnp.float32
BF16 = jnp.bfloat16
HIGHEST = lax.Precision.HIGHEST

N_META = 16
RMS_EPS = 1e-6
L2_EPS = 1e-6
CONV_K = 4
D_MODEL = 1024
HEAD_DIM = 128
N_HEADS = 8
CHUNK = 64
SB_BLOCK = 128
PAD = SB_BLOCK - N_META
SSM_INNER = 2048
SSM_P = 64
SSM_HEADS = 32
SSM_GROUPS = 2
SSM_HG = SSM_HEADS // SSM_GROUPS
SSM_N = 128
VMEM_LIMIT = 56 * 1024 * 1024

_DIMS = {"nn": (((1,), (0,)), ((), ())), "nt": (((1,), (1,)), ((), ())), "tn": (((0,), (0,)), ((), ()))}


def _mm_raw(a, b, mode, hi):
    if hi:
        return lax.dot_general(a.astype(F32), b.astype(F32), _DIMS[mode], precision=HIGHEST, preferred_element_type=F32)
    return lax.dot_general(a.astype(BF16), b.astype(BF16), _DIMS[mode], preferred_element_type=F32)


@functools.partial(jax.custom_vjp, nondiff_argnums=(2, 3))
def _mm(a, b, mode="nn", hi=False):
    return _mm_raw(a, b, mode, hi)


def _mm_fwd(a, b, mode, hi):
    return _mm_raw(a, b, mode, hi), (a, b)


def _mm_bwd(mode, hi, res, g):
    a, b = res
    if mode == "nn":
        return _mm_raw(g, b, "nt", hi), _mm_raw(a, g, "tn", hi)
    if mode == "nt":
        return _mm_raw(g, b, "nn", hi), _mm_raw(g, a, "tn", hi)
    return _mm_raw(b, g, "nt", hi), _mm_raw(a, g, "nn", hi)


_mm.defvjp(_mm_fwd, _mm_bwd)


def _iota2(shape, axis):
    return lax.broadcasted_iota(jnp.int32, shape, axis)


def _inv_unit_lower_raw(m):
    size = m.shape[0]
    eye = (_iota2((size, size), 0) == _iota2((size, size), 1)).astype(F32)
    n = -m
    t = eye + n
    p = n
    steps = int(math.log2(size)) - 1
    for _ in range(steps):
        p = _mm_raw(p, p, "nn", True)
        t = t + _mm_raw(t, p, "nn", True)
    return t


@jax.custom_vjp
def _inv_unit_lower(m):
    return _inv_unit_lower_raw(m)


def _inv_fwd(m):
    t = _inv_unit_lower_raw(m)
    return t, t


def _inv_bwd(t, g):
    return (-_mm_raw(_mm_raw(t, g, "tn", True), t, "nt", True),)


_inv_unit_lower.defvjp(_inv_fwd, _inv_bwd)


def _safe_decay(col, row, keep):
    return jnp.where(keep, jnp.exp(jnp.where(keep, col - row, 0.0)), 0.0)


def _col_to_row(col):
    n = col.shape[0]
    eye = _iota2((n, n), 0) == _iota2((n, n), 1)
    return jnp.sum(jnp.where(eye, col, 0.0), axis=0, keepdims=True)


def _cumsum_col(col):
    n = col.shape[0]
    li, si = _iota2((n, n), 0), _iota2((n, n), 1)
    row = _col_to_row(col)
    c_col = jnp.sum(jnp.where(li >= si, row, 0.0), axis=1, keepdims=True)
    c_row = jnp.sum(jnp.where(li <= si, col, 0.0), axis=0, keepdims=True)
    return c_col, c_row


def _gdn_chunk(q, k, v, g, beta, state):
    cl = q.shape[0]
    li, si = _iota2((cl, cl), 0), _iota2((cl, cl), 1)
    gc_col, gc_row = _cumsum_col(g)
    g_last = jnp.sum(g, axis=0, keepdims=True)
    dec_strict = _safe_decay(gc_col, gc_row, li > si)
    dec_incl = _safe_decay(gc_col, gc_row, li >= si)
    e_gc = jnp.exp(gc_col)
    qs = q * (HEAD_DIM ** -0.5)
    kb = k * beta
    m = _mm(kb, k, "nt") * dec_strict
    t_inv = _inv_unit_lower(m)
    u = _mm(t_inv, v * beta)
    w = _mm(t_inv, kb * e_gc)
    a_qk = _mm(qs, k, "nt") * dec_incl
    q_dec = qs * e_gc
    k_end = k * jnp.exp(g_last - gc_col)
    v_new = u - _mm(w, state)
    o = _mm(q_dec, state) + _mm(a_qk, v_new)
    new_state = state * jnp.exp(g_last) + _mm(k_end, v_new, "tn")
    return o, new_state


def gdn_fwd(qkv, gates):
    lp = qkv.shape[0]
    nh = N_HEADS
    nc = lp // CHUNK
    width = nh * HEAD_DIM

    def body(qkv_ref, gt_ref, o_ref, s_ref, state):
        @pl.when(pl.program_id(0) == 0)
        def _():
            state[...] = jnp.zeros_like(state)

        gt = gt_ref[...]
        for h in range(nh):
            lanes = slice(h * HEAD_DIM, (h + 1) * HEAD_DIM)
            s_in = state[h]
            s_ref[0, h] = s_in
            o, s_new = _gdn_chunk(qkv_ref[:, lanes], qkv_ref[:, width + h * HEAD_DIM:width + (h + 1) * HEAD_DIM],
                                  qkv_ref[:, 2 * width + h * HEAD_DIM:2 * width + (h + 1) * HEAD_DIM],
                                  gt[:, nh + h:nh + h + 1], gt[:, h:h + 1], s_in)
            o_ref[:, lanes] = o
            state[h] = s_new

    return pl.pallas_call(
        body, name="gdn_fwd", grid=(nc,),
        in_specs=[pl.BlockSpec((CHUNK, 3 * width), lambda c: (c, 0)), pl.BlockSpec((CHUNK, HEAD_DIM), lambda c: (c, 0))],
        out_specs=[pl.BlockSpec((CHUNK, width), lambda c: (c, 0)), pl.BlockSpec((1, nh, HEAD_DIM, HEAD_DIM), lambda c: (c, 0, 0, 0))],
        out_shape=[jax.ShapeDtypeStruct((lp, width), F32), jax.ShapeDtypeStruct((nc, nh, HEAD_DIM, HEAD_DIM), F32)],
        scratch_shapes=[pltpu.VMEM((nh, HEAD_DIM, HEAD_DIM), F32)],
        compiler_params=pltpu.CompilerParams(dimension_semantics=("arbitrary",), vmem_limit_bytes=VMEM_LIMIT),
    )(qkv, gates)


def gdn_bwd(qkv, gates, states, d_o):
    lp = qkv.shape[0]
    nh = N_HEADS
    nc = lp // CHUNK
    width = nh * HEAD_DIM

    def body(qkv_ref, gt_ref, s_ref, do_ref, dqkv_ref, dgt_ref, d_state):
        @pl.when(pl.program_id(0) == 0)
        def _():
            d_state[...] = jnp.zeros_like(d_state)

        gt = gt_ref[...]
        lane = _iota2((CHUNK, HEAD_DIM), 1)
        d_gt = jnp.zeros((CHUNK, HEAD_DIM), F32)
        for h in range(nh):
            lanes = slice(h * HEAD_DIM, (h + 1) * HEAD_DIM)
            k_lanes = slice(width + h * HEAD_DIM, width + (h + 1) * HEAD_DIM)
            v_lanes = slice(2 * width + h * HEAD_DIM, 2 * width + (h + 1) * HEAD_DIM)
            _, pull = jax.vjp(_gdn_chunk, qkv_ref[:, lanes], qkv_ref[:, k_lanes], qkv_ref[:, v_lanes],
                              gt[:, nh + h:nh + h + 1], gt[:, h:h + 1], s_ref[0, h])
            dq, dk, dv, dg, db, ds = pull((do_ref[:, lanes], d_state[h]))
            dqkv_ref[:, lanes] = dq
            dqkv_ref[:, k_lanes] = dk
            dqkv_ref[:, v_lanes] = dv
            d_gt = d_gt + jnp.where(lane == h, db, 0.0) + jnp.where(lane == nh + h, dg, 0.0)
            d_state[h] = ds
        dgt_ref[...] = d_gt

    rev = lambda c: (nc - 1 - c, 0)
    return pl.pallas_call(
        body, name="gdn_bwd", grid=(nc,),
        in_specs=[pl.BlockSpec((CHUNK, 3 * width), rev), pl.BlockSpec((CHUNK, HEAD_DIM), rev),
                  pl.BlockSpec((1, nh, HEAD_DIM, HEAD_DIM), lambda c: (nc - 1 - c, 0, 0, 0)), pl.BlockSpec((CHUNK, width), rev)],
        out_specs=[pl.BlockSpec((CHUNK, 3 * width), rev), pl.BlockSpec((CHUNK, HEAD_DIM), rev)],
        out_shape=[jax.ShapeDtypeStruct((lp, 3 * width), F32), jax.ShapeDtypeStruct((lp, HEAD_DIM), F32)],
        scratch_shapes=[pltpu.VMEM((nh, HEAD_DIM, HEAD_DIM), F32)],
        compiler_params=pltpu.CompilerParams(dimension_semantics=("arbitrary",), vmem_limit_bytes=VMEM_LIMIT),
    )(qkv, gates, states, d_o)


def _head_expand():
    width = SSM_HG * SSM_P
    return (_iota2((SSM_HG, width), 1) // SSM_P == _iota2((SSM_HG, width), 0)).astype(F32)


def _ssd_chunk(x, b, c, dt, la, state):
    cl = x.shape[0]
    li, si = _iota2((cl, cl), 0), _iota2((cl, cl), 1)
    causal = li >= si
    expand = _head_expand()
    tri = causal.astype(F32)
    xs = x * _mm(dt, expand, "nn", True)
    la_x = _mm(la, expand, "nn", True)
    cs_x = _mm(tri, la_x, "nn", True)
    last_x = jnp.sum(la_x, axis=0, keepdims=True)
    cs = _mm(tri, la, "nn", True)
    scores = _mm(c, b, "nt")
    y = _mm(c, state) * jnp.exp(cs_x)
    lane_head = _iota2((1, SSM_HG * SSM_P), 1) // SSM_P
    head_id = _iota2((1, SSM_HG), 1)
    for h in range(SSM_HG):
        cs_col = jnp.sum(jnp.where(head_id == h, cs, 0.0), axis=1, keepdims=True)
        decay = _safe_decay(cs_col, _col_to_row(cs_col), causal)
        y = y + _mm(scores * decay, jnp.where(lane_head == h, xs, 0.0))
    new_state = state * jnp.exp(last_x) + _mm(b, xs * jnp.exp(last_x - cs_x), "tn")
    return y, new_state


GATE_DT = 16


def _place_lanes(v, lo):
    n = v.shape[1]
    sel = (_iota2((n, HEAD_DIM), 1) == _iota2((n, HEAD_DIM), 0) + lo).astype(F32)
    return _mm_raw(v, sel, "nn", True)


def ssd_fwd(xbc, first, second):
    lp = xbc.shape[0]
    nc = lp // CHUNK
    width = SSM_HG * SSM_P
    b_off, c_off = SSM_INNER, SSM_INNER + SSM_GROUPS * SSM_N

    def body(x_ref, f_ref, s2_ref, y_ref, s_ref, state):
        @pl.when(pl.program_id(0) == 0)
        def _():
            state[...] = jnp.zeros_like(state)

        f, s2 = f_ref[...], s2_ref[...]
        for g in range(SSM_GROUPS):
            lo = GATE_DT + g * SSM_HG
            s_in = state[g]
            s_ref[0, g] = s_in
            y, s_new = _ssd_chunk(x_ref[:, g * width:(g + 1) * width], x_ref[:, b_off + g * SSM_N:b_off + (g + 1) * SSM_N],
                                  x_ref[:, c_off + g * SSM_N:c_off + (g + 1) * SSM_N], f[:, lo:lo + SSM_HG], s2[:, lo:lo + SSM_HG], s_in)
            y_ref[:, g * width:(g + 1) * width] = y
            state[g] = s_new

    row = lambda cols: pl.BlockSpec((CHUNK, cols), lambda k: (k, 0))
    return pl.pallas_call(
        body, name="ssd_fwd", grid=(nc,),
        in_specs=[row(xbc.shape[1]), row(HEAD_DIM), row(HEAD_DIM)],
        out_specs=[row(SSM_INNER), pl.BlockSpec((1, SSM_GROUPS, SSM_N, width), lambda k: (k, 0, 0, 0))],
        out_shape=[jax.ShapeDtypeStruct((lp, SSM_INNER), F32), jax.ShapeDtypeStruct((nc, SSM_GROUPS, SSM_N, width), F32)],
        scratch_shapes=[pltpu.VMEM((SSM_GROUPS, SSM_N, width), F32)],
        compiler_params=pltpu.CompilerParams(dimension_semantics=("arbitrary",), vmem_limit_bytes=VMEM_LIMIT),
    )(xbc, first, second)


def ssd_bwd(xbc, first, second, states, d_y, d_xh):
    lp = xbc.shape[0]
    nc = lp // CHUNK
    width = SSM_HG * SSM_P
    b_off, c_off = SSM_INNER, SSM_INNER + SSM_GROUPS * SSM_N

    def body(x_ref, f_ref, s2_ref, s_ref, dy_ref, dxh_ref, dx_ref, df_ref, ds2_ref, d_state):
        @pl.when(pl.program_id(0) == 0)
        def _():
            d_state[...] = jnp.zeros_like(d_state)

        f, s2 = f_ref[...], s2_ref[...]
        d_f = jnp.zeros((CHUNK, HEAD_DIM), F32)
        d_s2 = jnp.zeros((CHUNK, HEAD_DIM), F32)
        for g in range(SSM_GROUPS):
            lo = GATE_DT + g * SSM_HG
            x_l = slice(g * width, (g + 1) * width)
            b_l = slice(b_off + g * SSM_N, b_off + (g + 1) * SSM_N)
            c_l = slice(c_off + g * SSM_N, c_off + (g + 1) * SSM_N)
            _, pull = jax.vjp(_ssd_chunk, x_ref[:, x_l], x_ref[:, b_l], x_ref[:, c_l], f[:, lo:lo + SSM_HG], s2[:, lo:lo + SSM_HG],
                              s_ref[0, g])
            dx, db, dc, ddt, dla, ds = pull((dy_ref[:, x_l], d_state[g]))
            dx_ref[:, x_l] = dx + dxh_ref[:, x_l]
            dx_ref[:, b_l] = db
            dx_ref[:, c_l] = dc
            d_f = d_f + _place_lanes(ddt, lo)
            d_s2 = d_s2 + _place_lanes(dla, lo)
            d_state[g] = ds
        df_ref[...] = d_f
        ds2_ref[...] = d_s2

    row = lambda cols: pl.BlockSpec((CHUNK, cols), lambda k: (nc - 1 - k, 0))
    gate_shape = jax.ShapeDtypeStruct((lp, HEAD_DIM), F32)
    return pl.pallas_call(
        body, name="ssd_bwd", grid=(nc,),
        in_specs=[row(xbc.shape[1]), row(HEAD_DIM), row(HEAD_DIM),
                  pl.BlockSpec((1, SSM_GROUPS, SSM_N, width), lambda k: (nc - 1 - k, 0, 0, 0)), row(SSM_INNER), row(SSM_INNER)],
        out_specs=[row(xbc.shape[1]), row(HEAD_DIM), row(HEAD_DIM)],
        out_shape=[jax.ShapeDtypeStruct(xbc.shape, F32), gate_shape, gate_shape],
        scratch_shapes=[pltpu.VMEM((SSM_GROUPS, SSM_N, width), F32)],
        compiler_params=pltpu.CompilerParams(dimension_semantics=("arbitrary",), vmem_limit_bytes=VMEM_LIMIT),
    )(xbc, first, second, states, d_y, d_xh)


SB_QROWS = 544


def _mm_tri(a, tri):
    hi = a.astype(BF16)
    lo = (a - hi.astype(F32)).astype(BF16)
    t = tri.astype(BF16)
    dims = _DIMS["nn"]
    return lax.dot_general(hi, t, dims, preferred_element_type=F32) + lax.dot_general(lo, t, dims, preferred_element_type=F32)


def _sb_scores(q_scaled, kb, row0, j):
    shape = (q_scaled.shape[0], SB_BLOCK)
    z = _mm_raw(q_scaled, kb, "nt", False)
    q_pos = row0 + _iota2(shape, 0)
    k_pos = j * SB_BLOCK + _iota2(shape, 1)
    valid = (k_pos < q_pos) & (k_pos >= PAD)
    sp = jnp.maximum(z, 0.0) + jnp.log(1.0 + jnp.exp(-jnp.abs(z)))
    lk = jnp.where(valid, -sp, 0.0)
    return z, sp, valid, lk


def sb_fwd(src, offs):
    lp = src.shape[0]
    nh = N_HEADS
    qb = _pick(lp, SB_QROWS, 8)
    scale = HEAD_DIM ** -0.5
    blk = SB_BLOCK

    def body(q_ref, k_ref, v_ref, o_ref, c_ref):
        i = pl.program_id(1)
        q_scaled = q_ref[...] * scale
        upper = _iota2((blk, blk), 0) > _iota2((blk, blk), 1)
        n_blocks = ((i + 1) * qb + blk - 1) // blk

        def step(it, carry):
            acc, c = carry
            j = n_blocks - 1 - it
            rows = pl.ds(pl.multiple_of(j * blk, blk), blk)
            z, sp, valid, lk = _sb_scores(q_scaled, k_ref[rows, :], i * qb, j)
            later = _mm_tri(lk, upper) + c
            w = jnp.where(valid, jnp.exp(z - sp + later), 0.0)
            acc = acc + _mm_raw(w, v_ref[rows, :], "nn", False)
            return acc, c + jnp.sum(lk, axis=1, keepdims=True)

        acc, c = lax.fori_loop(0, n_blocks, step, (jnp.zeros((qb, HEAD_DIM), F32), jnp.zeros((qb, 1), F32)))
        o_ref[...] = acc
        c_ref[0] = c

    qspec = pl.BlockSpec((qb, HEAD_DIM), lambda h, i: (i, offs[0] + h))
    kspec = pl.BlockSpec((lp, HEAD_DIM), lambda h, i: (0, offs[1] + h))
    vspec = pl.BlockSpec((lp, HEAD_DIM), lambda h, i: (0, offs[2] + h))
    ospec = pl.BlockSpec((qb, HEAD_DIM), lambda h, i: (i, h))
    cspec = pl.BlockSpec((1, qb, 1), lambda h, i: (h, i, 0))
    return pl.pallas_call(
        body, name="sb_fwd", grid=(nh, lp // qb), in_specs=[qspec, kspec, vspec], out_specs=[ospec, cspec],
        out_shape=[jax.ShapeDtypeStruct((lp, nh * HEAD_DIM), F32), jax.ShapeDtypeStruct((nh, lp, 1), F32)],
        compiler_params=pltpu.CompilerParams(dimension_semantics=("arbitrary", "arbitrary"), vmem_limit_bytes=VMEM_LIMIT),
    )(src, src, src)


def sb_bwd(src, offs, csum, d_o):
    lp = src.shape[0]
    nh = N_HEADS
    qb = _pick(lp, SB_QROWS, 8)
    scale = HEAD_DIM ** -0.5
    blk = SB_BLOCK

    def body(q_ref, k_ref, v_ref, c_ref, do_ref, dq_ref, dk_ref, dv_ref):
        i = pl.program_id(1)

        @pl.when(i == 0)
        def _():
            dk_ref[...] = jnp.zeros_like(dk_ref)
            dv_ref[...] = jnp.zeros_like(dv_ref)

        q_scaled = q_ref[...] * scale
        d_out = do_ref[...]
        total = c_ref[0]
        lower_incl = _iota2((blk, blk), 0) <= _iota2((blk, blk), 1)
        lower = _iota2((blk, blk), 0) < _iota2((blk, blk), 1)
        n_blocks = ((i + 1) * qb + blk - 1) // blk

        def step(j, carry):
            acc, cp, ep = carry
            rows = pl.ds(pl.multiple_of(j * blk, blk), blk)
            kb = k_ref[rows, :]
            vb = v_ref[rows, :]
            z, sp, valid, lk = _sb_scores(q_scaled, kb, i * qb, j)
            later = total - cp - _mm_tri(lk, lower_incl)
            w = jnp.where(valid, jnp.exp(z - sp + later), 0.0)
            e = w * _mm_raw(d_out, vb, "nt", False)
            before = ep + _mm_tri(e, lower)
            dz = jnp.where(valid, e * jnp.exp(-sp) - before * jnp.exp(z - sp), 0.0)
            dk_ref[rows, :] += _mm_raw(dz, q_scaled, "tn", False)
            dv_ref[rows, :] += _mm_raw(w, d_out, "tn", False)
            acc = acc + _mm_raw(dz, kb, "nn", False)
            return acc, cp + jnp.sum(lk, axis=1, keepdims=True), ep + jnp.sum(e, axis=1, keepdims=True)

        zero_col = jnp.zeros((qb, 1), F32)
        acc, _, _ = lax.fori_loop(0, n_blocks, step, (jnp.zeros((qb, HEAD_DIM), F32), zero_col, zero_col))
        dq_ref[...] = acc * scale

    qspec = pl.BlockSpec((qb, HEAD_DIM), lambda h, i: (i, offs[0] + h))
    kspec = pl.BlockSpec((lp, HEAD_DIM), lambda h, i: (0, offs[1] + h))
    vspec = pl.BlockSpec((lp, HEAD_DIM), lambda h, i: (0, offs[2] + h))
    ospec = pl.BlockSpec((qb, HEAD_DIM), lambda h, i: (i, h))
    fullspec = pl.BlockSpec((lp, HEAD_DIM), lambda h, i: (0, h))
    cspec = pl.BlockSpec((1, qb, 1), lambda h, i: (h, i, 0))
    return pl.pallas_call(
        body, name="sb_bwd", grid=(nh, lp // qb), in_specs=[qspec, kspec, vspec, cspec, ospec], out_specs=[ospec, fullspec, fullspec],
        out_shape=[jax.ShapeDtypeStruct((lp, nh * HEAD_DIM), F32)] * 3,
        compiler_params=pltpu.CompilerParams(dimension_semantics=("arbitrary", "arbitrary"), vmem_limit_bytes=VMEM_LIMIT),
    )(src, src, src, csum, d_o)


def _pick(n, target, unit):
    if n <= target:
        return n
    best = None
    for d in range(unit, target + 1, unit):
        if n % d == 0:
            best = d
    assert best is not None, (n, target, unit)
    return best


def matmul(a, b, mode="nn", *, name, bm=1088, bn=640, bk=2176, residual=None, out_dtype=F32):
    if mode == "nn":
        (m, k), n = a.shape, b.shape[1]
    elif mode == "nt":
        (m, k), n = a.shape, b.shape[0]
    else:
        (k, m), n = a.shape, b.shape[1]
    bm = _pick(m, bm, 128 if mode == "tn" else 8)
    bn = _pick(n, bn, 128 if mode != "nt" else 8)
    bk = _pick(k, bk, 128 if mode != "tn" else 8)
    nk = k // bk

    def body(*refs):
        if residual is None:
            a_ref, b_ref, o_ref, acc = refs
            r_ref = None
        else:
            a_ref, b_ref, r_ref, o_ref, acc = refs
        kk = pl.program_id(2)
        part = _mm_raw(a_ref[...], b_ref[...], mode, False)

        @pl.when(kk == 0)
        def _():
            acc[...] = part

        @pl.when(kk > 0)
        def _():
            acc[...] += part

        @pl.when(kk == nk - 1)
        def _():
            res = acc[...]
            if r_ref is not None:
                res = res + r_ref[...]
            o_ref[...] = res.astype(out_dtype)

    a_spec = pl.BlockSpec((bk, bm), lambda i, j, kk: (kk, i)) if mode == "tn" else pl.BlockSpec((bm, bk), lambda i, j, kk: (i, kk))
    b_spec = pl.BlockSpec((bn, bk), lambda i, j, kk: (j, kk)) if mode == "nt" else pl.BlockSpec((bk, bn), lambda i, j, kk: (kk, j))
    o_spec = pl.BlockSpec((bm, bn), lambda i, j, kk: (i, j))
    ins, specs = [a, b], [a_spec, b_spec]
    if residual is not None:
        ins.append(residual)
        specs.append(o_spec)
    return pl.pallas_call(
        body, name=name, grid=(m // bm, n // bn, nk), in_specs=specs, out_specs=o_spec,
        out_shape=jax.ShapeDtypeStruct((m, n), out_dtype),
        scratch_shapes=[pltpu.VMEM((bm, bn), F32)],
        compiler_params=pltpu.CompilerParams(dimension_semantics=("arbitrary", "arbitrary", "arbitrary"), vmem_limit_bytes=VMEM_LIMIT),
    )(*ins)


def _row_specs(rows, params, width, bm):
    row_specs = [pl.BlockSpec((bm, width), (lambda j, i, off=off: (i, off + j))) for _, off in rows]
    par_specs = [pl.BlockSpec((p.shape[0], width) if per_col else p.shape, ((lambda j, i: (0, j)) if per_col else (lambda j, i: (0, 0))))
                 for p, per_col in params]
    return row_specs, par_specs


def rowmap_fwd(name, fn, rows, params, n_out, *, width, ncol, bm, lp):
    row_specs, par_specs = _row_specs(rows, params, width, bm)
    nr = len(rows)

    def body(*refs):
        ins, outs = refs[:nr + len(params)], refs[nr + len(params):]
        row_ids = pl.program_id(1) * bm + _iota2((bm, 1), 0)
        res = fn(row_ids, *[r[...] for r in ins])
        for o_ref, val in zip(outs, res):
            o_ref[...] = val

    o_spec = pl.BlockSpec((bm, width), lambda j, i: (i, j))
    return pl.pallas_call(
        body, name=name, grid=(ncol, lp // bm), in_specs=row_specs + par_specs, out_specs=[o_spec] * n_out,
        out_shape=[jax.ShapeDtypeStruct((lp, ncol * width), F32)] * n_out,
        compiler_params=pltpu.CompilerParams(dimension_semantics=("arbitrary", "arbitrary"), vmem_limit_bytes=VMEM_LIMIT),
    )(*[a for a, _ in rows], *[p for p, _ in params])


def rowmap_bwd(name, fn, rows, params, d_outs, *, width, ncol, bm, lp):
    row_specs, par_specs = _row_specs(rows, params, width, bm)
    nr, npar, nout = len(rows), len(params), len(d_outs)

    def body(*refs):
        ins = refs[:nr + npar]
        dos = refs[nr + npar:nr + npar + nout]
        d_rows = refs[nr + npar + nout:nr + npar + nout + nr]
        d_pars = refs[nr + npar + nout + nr:]
        j, i = pl.program_id(0), pl.program_id(1)
        row_ids = i * bm + _iota2((bm, 1), 0)
        _, pull = jax.vjp(lambda *xs: tuple(fn(row_ids, *xs)), *[r[...] for r in ins])
        grads = pull(tuple(d[...] for d in dos))
        for ref, val in zip(d_rows, grads[:nr]):
            ref[...] = val
        for ref, val, (_, per_col) in zip(d_pars, grads[nr:], params):
            first = (i == 0) if per_col else ((i == 0) & (j == 0))

            @pl.when(first)
            def _(ref=ref, val=val):
                ref[...] = val

            @pl.when(jnp.logical_not(first))
            def _(ref=ref, val=val):
                ref[...] += val

    o_spec = pl.BlockSpec((bm, width), lambda j, i: (i, j))
    res = pl.pallas_call(
        body, name=name, grid=(ncol, lp // bm), in_specs=row_specs + par_specs + [o_spec] * nout,
        out_specs=[o_spec] * nr + par_specs,
        out_shape=[jax.ShapeDtypeStruct((lp, ncol * width), F32)] * nr + [jax.ShapeDtypeStruct(p.shape, F32) for p, _ in params],
        compiler_params=pltpu.CompilerParams(dimension_semantics=("arbitrary", "arbitrary"), vmem_limit_bytes=VMEM_LIMIT),
    )(*[a for a, _ in rows], *[p for p, _ in params], *d_outs)
    return res[:nr], res[nr:]


def _silu(x):
    return x * jax.nn.sigmoid(x)


def _softplus(x):
    return jnp.maximum(x, 0.0) + jnp.log(1.0 + jnp.exp(-jnp.abs(x)))


def _real_rows(row_ids):
    return (row_ids >= PAD).astype(F32)


def _f_rmsnorm(row_ids, h, g):
    return (h * lax.rsqrt(jnp.mean(h * h, axis=-1, keepdims=True) + RMS_EPS) * g,)


def _f_small_gates(row_ids, small, bias, a_log):
    lane = _iota2(small.shape, 1)
    t = small + bias
    sp = _softplus(t)
    coef = -jnp.exp(a_log)
    keep = _real_rows(row_ids)
    first = jnp.where(lane < 8, jax.nn.sigmoid(t), jnp.where(lane < 16, coef * sp, jnp.where(lane < 48, sp, 0.0)))
    second = jnp.where((lane >= 8) & (lane < 48), coef * sp, 0.0)
    return first * keep, second * keep


def _f_gate_silu(row_ids, o, z):
    return (o * _silu(z),)


def _f_head_norm_gate(row_ids, o, z, g):
    return (o * lax.rsqrt(jnp.mean(o * o, axis=-1, keepdims=True) + RMS_EPS) * g * _silu(z),)


def _f_ssm_out(row_ids, y, xh, z, d_skip, g):
    t = (y + d_skip * xh) * _silu(z)
    return (t * lax.rsqrt(jnp.mean(t * t, axis=-1, keepdims=True) + RMS_EPS) * g,)


def _f_merge(row_ids, pa, pb, pc, ga, gb, gc):
    return (jax.nn.sigmoid(ga) * pa + jax.nn.sigmoid(gb) * pb + jax.nn.sigmoid(gc) * pc,)


def _shift_rows(x, s):
    s = s % x.shape[0]
    return x if s == 0 else pltpu.roll(x, s, 0)


def _conv_pre(x, w, b):
    pre = b
    for kk in range(CONV_K):
        pre = pre + w[kk:kk + 1, :] * _shift_rows(x, CONV_K - 1 - kk)
    return pre


def _conv_post(pre, l2_flag, keep):
    act = _silu(pre)
    nrm = act * lax.rsqrt(jnp.sum(act * act, axis=-1, keepdims=True) + L2_EPS)
    return (l2_flag * nrm + (1.0 - l2_flag) * act) * keep


def conv_fwd(name, src, col_off, w, b, n_l2):
    lp, ch = src.shape[0], w.shape[1]

    def body(x_ref, w_ref, b_ref, o_ref):
        l2_flag = (pl.program_id(0) < n_l2).astype(F32)
        keep = _real_rows(_iota2((lp, 1), 0))
        o_ref[...] = _conv_post(_conv_pre(x_ref[...], w_ref[...], b_ref[...]), l2_flag, keep)

    return pl.pallas_call(
        body, name=name, grid=(ch // HEAD_DIM,),
        in_specs=[pl.BlockSpec((lp, HEAD_DIM), lambda j: (0, col_off + j)), pl.BlockSpec((CONV_K, HEAD_DIM), lambda j: (0, j)),
                  pl.BlockSpec((1, HEAD_DIM), lambda j: (0, j))],
        out_specs=pl.BlockSpec((lp, HEAD_DIM), lambda j: (0, j)),
        out_shape=jax.ShapeDtypeStruct((lp, ch), F32),
        compiler_params=pltpu.CompilerParams(dimension_semantics=("arbitrary",), vmem_limit_bytes=VMEM_LIMIT),
    )(src, w, b)


def conv_bwd(name, src, col_off, w, b, n_l2, d_out):
    lp, ch = src.shape[0], w.shape[1]

    def body(x_ref, w_ref, b_ref, do_ref, dx_ref, dw_ref, db_ref):
        l2_flag = (pl.program_id(0) < n_l2).astype(F32)
        keep = _real_rows(_iota2((lp, 1), 0))
        x, wv = x_ref[...], w_ref[...]
        pre = _conv_pre(x, wv, b_ref[...])
        _, pull = jax.vjp(lambda p: _conv_post(p, l2_flag, keep), pre)
        (d_pre,) = pull(do_ref[...])
        dx = jnp.zeros_like(x)
        for kk in range(CONV_K):
            s = CONV_K - 1 - kk
            dx = dx + wv[kk:kk + 1, :] * _shift_rows(d_pre, -s)
            dw_ref[kk:kk + 1, :] = jnp.sum(d_pre * _shift_rows(x, s), axis=0, keepdims=True)
        dx_ref[...] = dx * keep
        db_ref[...] = jnp.sum(d_pre, axis=0, keepdims=True)

    seq = pl.BlockSpec((lp, HEAD_DIM), lambda j: (0, j))
    wspec = pl.BlockSpec((CONV_K, HEAD_DIM), lambda j: (0, j))
    bspec = pl.BlockSpec((1, HEAD_DIM), lambda j: (0, j))
    return pl.pallas_call(
        body, name=name, grid=(ch // HEAD_DIM,),
        in_specs=[pl.BlockSpec((lp, HEAD_DIM), lambda j: (0, col_off + j)), wspec, bspec, seq],
        out_specs=[seq, wspec, bspec],
        out_shape=[jax.ShapeDtypeStruct((lp, ch), F32), jax.ShapeDtypeStruct(w.shape, F32), jax.ShapeDtypeStruct(b.shape, F32)],
        compiler_params=pltpu.CompilerParams(dimension_semantics=("arbitrary",), vmem_limit_bytes=VMEM_LIMIT),
    )(src, w, b, d_out)


def loss_head(h, target, g):
    lp, d = h.shape
    bm = SB_BLOCK
    first = (PAD + N_META) // bm

    def body(h_ref, t_ref, g_ref, loss_ref, dh_ref, dg_ref):
        i = pl.program_id(0)
        keep = (i >= first).astype(F32)

        def f(hv, gv):
            y = hv * lax.rsqrt(jnp.mean(hv * hv, axis=-1, keepdims=True) + RMS_EPS) * gv
            err = y - t_ref[...]
            return 0.5 * jnp.sum(jnp.mean(err * err, axis=-1, keepdims=True), axis=0, keepdims=True) * keep

        val, pull = jax.vjp(f, h_ref[...], g_ref[...])
        dh, dg = pull(jnp.ones((1, 1), F32))
        dh_ref[...] = dh

        @pl.when(i == 0)
        def _():
            loss_ref[...] = val
            dg_ref[...] = dg

        @pl.when(i > 0)
        def _():
            loss_ref[...] += val
            dg_ref[...] += dg

    row = pl.BlockSpec((bm, d), lambda i: (i, 0))
    return pl.pallas_call(
        body, name="loss_head", grid=(lp // bm,),
        in_specs=[row, pl.BlockSpec((bm, d), lambda i: (jnp.maximum(i - first, 0), 0)), pl.BlockSpec((1, d), lambda i: (0, 0))],
        out_specs=[pl.BlockSpec((1, 1), lambda i: (0, 0)), row, pl.BlockSpec((1, d), lambda i: (0, 0))],
        out_shape=[jax.ShapeDtypeStruct((1, 1), F32), jax.ShapeDtypeStruct((lp, d), F32), jax.ShapeDtypeStruct((1, d), F32)],
        compiler_params=pltpu.CompilerParams(dimension_semantics=("arbitrary",)),
    )(h, target, g)


ADAM_LR, ADAM_B1, ADAM_B2, ADAM_EPS, ADAM_WD, ADAM_STEP = 0.001, 0.9, 0.999, 1e-08, 0.01, 10


def adamw(name, w, g, m, v):
    rows, cols = w.shape
    br = _pick(rows, 128, 8)

    def body(w_ref, g_ref, m_ref, v_ref, d_ref, nm_ref, nv_ref):
        gv = g_ref[...]
        nm = ADAM_B1 * m_ref[...] + (1.0 - ADAM_B1) * gv
        nv = ADAM_B2 * v_ref[...] + (1.0 - ADAM_B2) * (gv * gv)
        m_hat = nm / (1.0 - ADAM_B1 ** ADAM_STEP)
        v_hat = nv / (1.0 - ADAM_B2 ** ADAM_STEP)
        d_ref[...] = -ADAM_LR * (m_hat / (jnp.sqrt(v_hat) + ADAM_EPS) + ADAM_WD * w_ref[...])
        nm_ref[...] = nm
        nv_ref[...] = nv

    spec = pl.BlockSpec((br, cols), lambda i: (i, 0))
    return pl.pallas_call(
        body, name=name, grid=(rows // br,), in_specs=[spec] * 4, out_specs=[spec] * 3,
        out_shape=[jax.ShapeDtypeStruct(w.shape, F32)] * 3,
        compiler_params=pltpu.CompilerParams(dimension_semantics=("arbitrary",), vmem_limit_bytes=VMEM_LIMIT),
    )(w, g, m, v)


MESH = pl.DeviceIdType.MESH
ANY = pl.BlockSpec(memory_space=pl.ANY)
D2D_PIECES = 16
ICI_PIECES = 4


def _place():
    x, y, c = lax.axis_index("x"), lax.axis_index("y"), lax.axis_index("c")
    return x, y, c, [(1 - x, y), (x, 1 - y), (1 - x, 1 - y)]


def _pieces(rows, n, unit):
    per = -(-rows // (n * unit)) * unit
    return [(s, min(per, rows - s)) for s in range(0, rows, per)]


def _row_unit(dtype):
    return 16 if dtype == BF16 else 8


def _scalar(v):
    return jnp.reshape(v, (1,)).astype(jnp.int32)


def gather_shards(name, packs):
    n = len(packs)

    def body(*refs):
        p_refs, o_refs = refs[:n], refs[n:2 * n]
        send_sems, recv_sems, local_sems = refs[2 * n:]
        x, y, c, chips = _place()
        mine = 2 * x + y

        def half_rows(b, which, start=0, size=None):
            half = p_refs[b].shape[0] // 2
            return pl.ds(pl.multiple_of(which * half + start, _row_unit(p_refs[b].dtype)), half if size is None else size)

        def remote(b, k, src, slot, rws, to):
            return pltpu.make_async_remote_copy(src_ref=src, dst_ref=o_refs[b].at[slot, rws, :], send_sem=send_sems.at[b, k],
                                                recv_sem=recv_sems.at[b, k], device_id=to, device_id_type=MESH)

        done = []
        for b, (p_ref, o_ref) in enumerate(zip(p_refs, o_refs)):
            rows, unit = p_ref.shape[0], _row_unit(p_ref.dtype)
            for start, size in _pieces(rows, D2D_PIECES, unit):
                pltpu.make_async_copy(p_ref.at[pl.ds(start, size), :], o_ref.at[mine, pl.ds(start, size), :], local_sems.at[b]).start()
            done.append(pltpu.make_async_copy(p_ref, o_ref.at[mine], local_sems.at[b]))
            for j, (cx, cy) in enumerate(chips):
                for start, size in _pieces(rows // 2, ICI_PIECES, unit):
                    rws = half_rows(b, c, start, size)
                    remote(b, j, p_ref.at[rws, :], mine, rws, (cx, cy, c)).start()
        sends = []
        for b, (p_ref, o_ref) in enumerate(zip(p_refs, o_refs)):
            rows, unit = p_ref.shape[0], _row_unit(p_ref.dtype)
            my_rows = half_rows(b, c)
            for j, (cx, cy) in enumerate(chips):
                slot = 2 * cx + cy
                first = remote(b, j, p_ref.at[my_rows, :], slot, my_rows, (cx, cy, c))
                first.wait_recv()
                sends.append(first)
                for start, size in _pieces(rows // 2, D2D_PIECES, unit):
                    rws = half_rows(b, c, start, size)
                    remote(b, 3 + j, o_ref.at[slot, rws, :], slot, rws, (x, y, 1 - c)).start()
                sends.append(remote(b, 3 + j, o_ref.at[slot, my_rows, :], slot, my_rows, (x, y, 1 - c)))
        for b, p_ref in enumerate(p_refs):
            other = half_rows(b, 1 - c)
            for j, (cx, cy) in enumerate(chips):
                remote(b, 3 + j, p_ref.at[other, :], 2 * cx + cy, other, (x, y, 1 - c)).wait_recv()
        for cp in sends:
            cp.wait_send()
        for cp in done:
            cp.wait()

    return pl.pallas_call(
        body, name=name, in_specs=[ANY] * n, out_specs=[ANY] * n,
        out_shape=[jax.ShapeDtypeStruct((4,) + p.shape, p.dtype) for p in packs],
        scratch_shapes=[pltpu.SemaphoreType.DMA((n, 6)), pltpu.SemaphoreType.DMA((n, 6)), pltpu.SemaphoreType.DMA((n,))],
    )(*packs)


def pair_split(name, bufs):
    n = len(bufs)

    def body(*refs):
        g_refs, t_refs = refs[:n], refs[n:2 * n]
        send_sems, recv_sems = refs[2 * n:]
        x, y, c, _ = _place()
        waits = []
        for b, (g_ref, t_ref) in enumerate(zip(g_refs, t_refs)):
            half = g_ref.shape[1] // 2
            unit = _row_unit(g_ref.dtype)

            def copy(slots, start, size):
                theirs = pl.ds(pl.multiple_of((1 - c) * half + start, unit), size)
                return pltpu.make_async_remote_copy(src_ref=g_ref.at[slots, theirs, :], dst_ref=t_ref.at[slots, pl.ds(start, size), :],
                                                    send_sem=send_sems.at[b], recv_sem=recv_sems.at[b], device_id=(x, y, 1 - c),
                                                    device_id_type=MESH)

            for s in range(4):
                for start, size in _pieces(half, D2D_PIECES // 4, unit):
                    copy(s, start, size).start()
            waits.append(copy(slice(None), 0, half))
        for cp in waits:
            cp.wait()

    return pl.pallas_call(
        body, name=name, in_specs=[ANY] * n, out_specs=[ANY] * n,
        out_shape=[jax.ShapeDtypeStruct((4, g.shape[1] // 2, g.shape[2]), g.dtype) for g in bufs],
        scratch_shapes=[pltpu.SemaphoreType.DMA((n,)), pltpu.SemaphoreType.DMA((n,))],
    )(*bufs)


def pair_add(name, g, theirs, transit):
    _, rows, cols = g.shape
    half = rows // 2
    br = _pick(half, 128, 16)

    def body(c_ref, g_ref, t_ref, o_ref):
        o_ref[...] = (g_ref[...].astype(F32) + t_ref[...].astype(F32)).astype(transit)

    blk = (4, br, cols)
    return pl.pallas_call(
        body, name=name,
        grid_spec=pltpu.PrefetchScalarGridSpec(
            num_scalar_prefetch=1, grid=(half // br,),
            in_specs=[pl.BlockSpec((4, None, br, cols), lambda i, c: (0, c[0], i, 0)), pl.BlockSpec(blk, lambda i, c: (0, i, 0))],
            out_specs=pl.BlockSpec(blk, lambda i, c: (0, i, 0))),
        out_shape=jax.ShapeDtypeStruct((4, half, cols), transit),
        compiler_params=pltpu.CompilerParams(dimension_semantics=("arbitrary",), vmem_limit_bytes=VMEM_LIMIT),
    )(_scalar(lax.axis_index("c")), g.reshape(4, 2, half, cols), theirs)


def chip_exchange(name, parts):
    n = len(parts)

    def body(*refs):
        a_refs, o_refs = refs[:n], refs[n:2 * n]
        send_sems, recv_sems = refs[2 * n:]
        x, y, c, chips = _place()
        mine = 2 * x + y
        waits = []
        for b, (a_ref, o_ref) in enumerate(zip(a_refs, o_refs)):
            rows = a_ref.shape[1]
            unit = _row_unit(a_ref.dtype)
            for j, (cx, cy) in enumerate(chips):
                def copy(start, size):
                    rws = pl.ds(start, size)
                    return pltpu.make_async_remote_copy(src_ref=a_ref.at[2 * cx + cy, rws, :], dst_ref=o_ref.at[mine, rws, :],
                                                        send_sem=send_sems.at[b, j], recv_sem=recv_sems.at[b, j],
                                                        device_id=(cx, cy, c), device_id_type=MESH)
                for start, size in _pieces(rows, ICI_PIECES, unit):
                    copy(start, size).start()
                waits.append(copy(0, rows))
        for cp in waits:
            cp.wait()

    return pl.pallas_call(
        body, name=name, in_specs=[ANY] * n, out_specs=[ANY] * n,
        out_shape=[jax.ShapeDtypeStruct(a.shape, a.dtype) for a in parts],
        scratch_shapes=[pltpu.SemaphoreType.DMA((n, 3)), pltpu.SemaphoreType.DMA((n, 3))],
    )(*parts)


def chip_add(name, got, part):
    _, rows, cols = got.shape
    br = _pick(rows, 128, 16)

    def body(m_ref, got_ref, part_ref, o_ref):
        mine = m_ref[0]
        for s in range(4):
            @pl.when(mine == s)
            def _(s=s):
                val = part_ref[...].astype(F32)
                o_ref[...] = val if s == 0 else o_ref[...] + val

            @pl.when(mine != s)
            def _(s=s):
                val = got_ref[s].astype(F32)
                o_ref[...] = val if s == 0 else o_ref[...] + val

    return pl.pallas_call(
        body, name=name,
        grid_spec=pltpu.PrefetchScalarGridSpec(
            num_scalar_prefetch=1, grid=(rows // br,),
            in_specs=[pl.BlockSpec((4, br, cols), lambda i, m: (0, i, 0)), pl.BlockSpec((None, br, cols), lambda i, m: (m[0], i, 0))],
            out_specs=pl.BlockSpec((br, cols), lambda i, m: (i, 0))),
        out_shape=jax.ShapeDtypeStruct((rows, cols), F32),
        compiler_params=pltpu.CompilerParams(dimension_semantics=("arbitrary",), vmem_limit_bytes=VMEM_LIMIT),
    )(_scalar(2 * lax.axis_index("x") + lax.axis_index("y")), got, part)


def pair_join(name, reds):
    n = len(reds)

    def body(*refs):
        r_refs, o_refs = refs[:n], refs[n:2 * n]
        send_sems, recv_sems, local_sems = refs[2 * n:]
        x, y, c, _ = _place()
        waits = []
        for b, (r_ref, o_ref) in enumerate(zip(r_refs, o_refs)):
            half = r_ref.shape[0]
            unit = _row_unit(r_ref.dtype)
            for start, size in _pieces(half, D2D_PIECES, unit):
                dst = pl.ds(pl.multiple_of(c * half + start, unit), size)
                pltpu.make_async_copy(r_ref.at[pl.ds(start, size), :], o_ref.at[dst, :], local_sems.at[b]).start()
                pltpu.make_async_remote_copy(src_ref=r_ref.at[pl.ds(start, size), :], dst_ref=o_ref.at[dst, :], send_sem=send_sems.at[b],
                                             recv_sem=recv_sems.at[b], device_id=(x, y, 1 - c), device_id_type=MESH).start()
            my_rows = pl.ds(pl.multiple_of(c * half, unit), half)
            waits.append(pltpu.make_async_remote_copy(src_ref=r_ref, dst_ref=o_ref.at[my_rows, :], send_sem=send_sems.at[b],
                                                      recv_sem=recv_sems.at[b], device_id=(x, y, 1 - c), device_id_type=MESH))
            waits.append(pltpu.make_async_copy(r_ref, o_ref.at[my_rows, :], local_sems.at[b]))
        for cp in waits:
            cp.wait()

    return pl.pallas_call(
        body, name=name, in_specs=[ANY] * n, out_specs=[ANY] * n,
        out_shape=[jax.ShapeDtypeStruct((2 * r.shape[0], r.shape[1]), r.dtype) for r in reds],
        scratch_shapes=[pltpu.SemaphoreType.DMA((n,)), pltpu.SemaphoreType.DMA((n,)), pltpu.SemaphoreType.DMA((n,))],
    )(*reds)


def reduce_scatter(tag, bufs, transits):
    theirs = pair_split(tag + "_pair_split", bufs)
    parts = [pair_add(f"{tag}_pair_add_{i}", g, t, tr) for i, (g, t, tr) in enumerate(zip(bufs, theirs, transits))]
    got = chip_exchange(tag + "_chip_exchange", parts)
    reds = [chip_add(f"{tag}_chip_add_{i}", gt, p) for i, (gt, p) in enumerate(zip(got, parts))]
    return pair_join(tag + "_pair_join", reds)


D_IN = 15920
D_PROJ = 16000
_SEGMENTS = ((0, 8192), (8208, 12816), (12848, 15920), (8192, 8208), (12816, 12848))
OFF_SB_Z, OFF_GDN_QKV, OFF_GDN_Z, OFF_SSM_Z, OFF_SSM_XBC, OFF_GATES, OFF_SMALL = 3072, 4096, 7168, 8192, 10240, 12800, 15872
PACK_C = 1024
WEIGHTS = ("meta_tokens", "norm_g", "w_in", "gdn_conv_w", "gdn_a_log", "gdn_dt_bias", "gdn_norm_g", "ssm_conv_w", "ssm_conv_b",
           "ssm_a_log", "ssm_dt_bias", "ssm_d", "ssm_norm_g", "w_branch_a", "w_branch_b", "w_branch_c", "w_out", "final_norm_g")
SHARDED = ("w_in", "w_branch_a", "w_branch_b", "w_branch_c", "w_out", "gdn_conv_w", "ssm_conv_w", "meta_tokens")
SHARD_AXIS = {"w_in": 2, "w_branch_a": 1, "w_branch_b": 1, "w_branch_c": 1, "w_out": 1, "gdn_conv_w": 2, "ssm_conv_w": 2, "meta_tokens": 1}
BRANCH = ("w_branch_a", "w_branch_b", "w_branch_c", "w_out")
EXACT = ("gdn_conv_w", "ssm_conv_w", "meta_tokens")
REPLICATED = tuple(n for n in WEIGHTS if n not in SHARDED)


def _regroup(w):
    parts = [w[..., a:b] for a, b in _SEGMENTS]
    return jnp.concatenate(parts + [jnp.zeros(w.shape[:-1] + (D_PROJ - D_IN,), w.dtype)], axis=-1)


def _ungroup(g):
    starts, pos = [], 0
    for a, b in _SEGMENTS:
        starts.append(pos)
        pos += b - a
    order = sorted(range(len(_SEGMENTS)), key=lambda i: _SEGMENTS[i][0])
    return jnp.concatenate([g[..., starts[i]:starts[i] + _SEGMENTS[i][1] - _SEGMENTS[i][0]] for i in order], axis=-1)


def _pack(parts, row_unit=64):
    n = sum(p.shape[0] for p in parts)
    rows = -(-n // (PACK_C * row_unit)) * row_unit
    flat = jnp.concatenate(list(parts) + [jnp.zeros((rows * PACK_C - n,), parts[0].dtype)])
    return flat.reshape(rows, PACK_C)


def _unpack(buf, shapes):
    flat, out, pos = buf.reshape(-1), [], 0
    for shp in shapes:
        n = math.prod(shp)
        out.append(flat[pos:pos + n].reshape(shp))
        pos += n
    return out


def _as_bf16_words(a):
    return lax.bitcast_convert_type(a, BF16).reshape(-1)


def _gather_weights(w):
    a = w["w_in"].astype(BF16).reshape(-1, w["w_in"].shape[-1])
    b = jnp.concatenate([w[n].reshape(-1, D_MODEL) for n in BRANCH], axis=0).astype(BF16)
    s = _pack([_as_bf16_words(w[n]) for n in EXACT])
    got_a, got_b, got_s = gather_shards("gather_weights", [a, b, s])
    full = {"w_in": jnp.concatenate([got_a[c].reshape(w["w_in"].shape) for c in range(4)], axis=2)}
    pos = 0
    for n in BRANCH:
        rows = w[n].shape[0] * w[n].shape[1]
        full[n] = jnp.concatenate([got_b[c, pos:pos + rows].reshape(w[n].shape) for c in range(4)], axis=1)
        pos += rows
    per_chip = [_unpack(got_s[c], [w[n].shape + (2,) for n in EXACT]) for c in range(4)]
    for i, n in enumerate(EXACT):
        full[n] = jnp.concatenate([lax.bitcast_convert_type(per_chip[c][i], F32) for c in range(4)], axis=SHARD_AXIS[n])
    return full


def _shard(a, axis, s):
    size = a.shape[axis] // 4
    return lax.slice_in_dim(a, s * size, (s + 1) * size, axis=axis)


def _reduce_grads(g, w):
    cols = w["w_in"].shape[-1]
    buf_a = jnp.stack([_shard(g["w_in"], 2, s).astype(BF16).reshape(-1, cols) for s in range(4)])
    buf_b = jnp.stack([jnp.concatenate([_shard(g[n], 1, s).reshape(-1, D_MODEL) for n in BRANCH], axis=0).astype(BF16) for s in range(4)])
    buf_s = jnp.stack([_pack([_shard(g[n], SHARD_AXIS[n], s).astype(BF16).reshape(-1) for n in EXACT], row_unit=32) for s in range(4)])
    small = _pack([g[n].reshape(-1) for n in REPLICATED], row_unit=32)
    buf_r = jnp.broadcast_to(small[None], (4,) + small.shape)
    red_a, red_b, red_s, red_r = reduce_scatter("grads", [buf_a, buf_b, buf_s, buf_r], [BF16, BF16, BF16, F32])
    out = {"w_in": red_a.reshape(w["w_in"].shape)}
    pos = 0
    for n in BRANCH:
        rows = w[n].shape[0] * w[n].shape[1]
        out[n] = red_b[pos:pos + rows].reshape(w[n].shape)
        pos += rows
    out.update(zip(EXACT, _unpack(red_s, [w[n].shape for n in EXACT])))
    return out, red_r


def _layer_params(w, full, l):
    lane = lambda v, lo: jnp.pad(v, (lo, HEAD_DIM - lo - v.shape[0]))[None]
    return dict(
        norm_g=w["norm_g"][l][None], wp=_regroup(full["w_in"][l]),
        gdn_conv_w=full["gdn_conv_w"][l], gdn_conv_b=jnp.zeros((1, 3 * N_HEADS * HEAD_DIM), F32),
        ssm_conv_w=full["ssm_conv_w"][l], ssm_conv_b=w["ssm_conv_b"][l][None],
        bias_vec=lane(w["gdn_dt_bias"][l], 8) + lane(w["ssm_dt_bias"][l], 16),
        alog_vec=lane(w["gdn_a_log"][l], 8) + lane(w["ssm_a_log"][l], 16),
        gdn_norm_g=w["gdn_norm_g"][l][None], d_skip=jnp.repeat(w["ssm_d"][l], SSM_P)[None], ssm_norm_g=w["ssm_norm_g"][l][None],
        wa=full["w_branch_a"][l], wb=full["w_branch_b"][l], wc=full["w_branch_c"][l], wo=full["w_out"][l])


def _layer_fwd(h, p):
    lp = h.shape[0]
    bm = _pick(lp, 272, 8)
    kw = dict(bm=bm, lp=lp)
    (u,) = rowmap_fwd("rms_fwd", _f_rmsnorm, [(h, 0)], [(p["norm_g"], False)], 1, width=D_MODEL, ncol=1, **kw)
    proj = matmul(u, p["wp"], "nn", name="proj", bm=lp, bn=640)
    o_a_raw, csum = sb_fwd(proj, (0, N_HEADS, 2 * N_HEADS))
    qkv = conv_fwd("gdn_conv_fwd", proj, OFF_GDN_QKV // HEAD_DIM, p["gdn_conv_w"], p["gdn_conv_b"], 2 * N_HEADS)
    first, second = rowmap_fwd("gates_fwd", _f_small_gates, [(proj, OFF_SMALL // HEAD_DIM)],
                               [(p["bias_vec"], False), (p["alog_vec"], False)], 2, width=HEAD_DIM, ncol=1, **kw)
    o_b_raw, gdn_states = gdn_fwd(qkv, first)
    xbc = conv_fwd("ssm_conv_fwd", proj, OFF_SSM_XBC // HEAD_DIM, p["ssm_conv_w"], p["ssm_conv_b"], 0)
    y_raw, ssd_states = ssd_fwd(xbc, first, second)
    (o_a,) = rowmap_fwd("gate_a_fwd", _f_gate_silu, [(o_a_raw, 0), (proj, OFF_SB_Z // 1024)], [], 1, width=1024, ncol=1, **kw)
    (o_b,) = rowmap_fwd("gate_b_fwd", _f_head_norm_gate, [(o_b_raw, 0), (proj, OFF_GDN_Z // HEAD_DIM)], [(p["gdn_norm_g"], False)], 1,
                        width=HEAD_DIM, ncol=N_HEADS, **kw)
    (o_c,) = rowmap_fwd("gate_c_fwd", _f_ssm_out, [(y_raw, 0), (xbc, 0), (proj, OFF_SSM_Z // 1024)],
                        [(p["d_skip"], True), (p["ssm_norm_g"], True)], 1, width=1024, ncol=SSM_GROUPS, **kw)
    pa = matmul(o_a, p["wa"], "nn", name="branch_a", bm=lp // 2, bn=512)
    pb = matmul(o_b, p["wb"], "nn", name="branch_b", bm=lp // 2, bn=512)
    pc = matmul(o_c, p["wc"], "nn", name="branch_c", bm=lp // 2, bn=512)
    merge_rows = [(pa, 0), (pb, 0), (pc, 0)] + [(proj, OFF_GATES // 512 + 2 * i) for i in range(3)]
    (merged,) = rowmap_fwd("merge_fwd", _f_merge, merge_rows, [], 1, width=512, ncol=2, **kw)
    h_out = matmul(merged, p["wo"], "nn", name="out_proj", bm=lp, bn=512, residual=h)
    saved = dict(h=h, u=u, proj=proj, csum=csum, qkv=qkv, first=first, second=second, o_a_raw=o_a_raw, o_b_raw=o_b_raw,
                 gdn_states=gdn_states, xbc=xbc, y_raw=y_raw, ssd_states=ssd_states, o_a=o_a, o_b=o_b, o_c=o_c, pa=pa, pb=pb, pc=pc,
                 merged=merged)
    return h_out, saved


def _layer_bwd(d_h, p, s):
    lp = d_h.shape[0]
    bm = _pick(lp, 272, 8)
    kw = dict(bm=bm, lp=lp)
    proj = s["proj"]
    g = {}
    d_merged = matmul(d_h, p["wo"], "nt", name="d_merged", bm=lp, bn=512)
    g["w_out"] = matmul(s["merged"], d_h, "tn", name="g_w_out", bm=512, bn=1024, bk=lp)
    merge_rows = [(s["pa"], 0), (s["pb"], 0), (s["pc"], 0)] + [(proj, OFF_GATES // 512 + 2 * i) for i in range(3)]
    (d_pa, d_pb, d_pc, d_ga, d_gb, d_gc), _ = rowmap_bwd("merge_bwd", _f_merge, merge_rows, [], [d_merged], width=512, ncol=2, **kw)
    g["w_branch_a"] = matmul(s["o_a"], d_pa, "tn", name="g_w_a", bm=512, bn=1024, bk=lp)
    g["w_branch_b"] = matmul(s["o_b"], d_pb, "tn", name="g_w_b", bm=512, bn=1024, bk=lp)
    g["w_branch_c"] = matmul(s["o_c"], d_pc, "tn", name="g_w_c", bm=512, bn=1024, bk=lp)
    d_oa = matmul(d_pa, p["wa"], "nt", name="d_o_a", bm=lp, bn=512)
    d_ob = matmul(d_pb, p["wb"], "nt", name="d_o_b", bm=lp, bn=512)
    d_oc = matmul(d_pc, p["wc"], "nt", name="d_o_c", bm=lp, bn=512)
    (d_oa_raw, d_sbz), _ = rowmap_bwd("gate_a_bwd", _f_gate_silu, [(s["o_a_raw"], 0), (proj, OFF_SB_Z // 1024)], [], [d_oa],
                                      width=1024, ncol=1, **kw)
    (d_ob_raw, d_gdz), (g["gdn_norm_g"],) = rowmap_bwd(
        "gate_b_bwd", _f_head_norm_gate, [(s["o_b_raw"], 0), (proj, OFF_GDN_Z // HEAD_DIM)], [(p["gdn_norm_g"], False)], [d_ob],
        width=HEAD_DIM, ncol=N_HEADS, **kw)
    (d_y, d_xh, d_ssz), (g_dskip, g["ssm_norm_g"]) = rowmap_bwd(
        "gate_c_bwd", _f_ssm_out, [(s["y_raw"], 0), (s["xbc"], 0), (proj, OFF_SSM_Z // 1024)],
        [(p["d_skip"], True), (p["ssm_norm_g"], True)], [d_oc], width=1024, ncol=SSM_GROUPS, **kw)
    g["gdn_norm_g"], g["ssm_norm_g"] = g["gdn_norm_g"][0], g["ssm_norm_g"][0]
    g["ssm_d"] = g_dskip.reshape(SSM_HEADS, SSM_P).sum(axis=1)
    d_q, d_k, d_v = sb_bwd(proj, (0, N_HEADS, 2 * N_HEADS), s["csum"], d_oa_raw)
    d_qkv, d_first_gdn = gdn_bwd(s["qkv"], s["first"], s["gdn_states"], d_ob_raw)
    d_xbc_out, d_first_ssd, d_second = ssd_bwd(s["xbc"], s["first"], s["second"], s["ssd_states"], d_y, d_xh)
    d_gdqkv, g["gdn_conv_w"], _ = conv_bwd("gdn_conv_bwd", proj, OFF_GDN_QKV // HEAD_DIM, p["gdn_conv_w"], p["gdn_conv_b"], 2 * N_HEADS,
                                           d_qkv)
    d_xbc, g["ssm_conv_w"], g_cb = conv_bwd("ssm_conv_bwd", proj, OFF_SSM_XBC // HEAD_DIM, p["ssm_conv_w"], p["ssm_conv_b"], 0, d_xbc_out)
    g["ssm_conv_b"] = g_cb[0]
    d_first = d_first_gdn + d_first_ssd
    (d_small,), (g_bias, g_alog) = rowmap_bwd("gates_bwd", _f_small_gates, [(proj, OFF_SMALL // HEAD_DIM)],
                                              [(p["bias_vec"], False), (p["alog_vec"], False)], [d_first, d_second],
                                              width=HEAD_DIM, ncol=1, **kw)
    g["gdn_dt_bias"], g["ssm_dt_bias"] = g_bias[0, 8:16], g_bias[0, 16:48]
    g["gdn_a_log"], g["ssm_a_log"] = g_alog[0, 8:16], g_alog[0, 16:48]
    d_proj = jnp.concatenate([d_q, d_k, d_v, d_sbz, d_gdqkv, d_gdz, d_ssz, d_xbc, d_ga, d_gb, d_gc, d_small], axis=1)
    g["w_in"] = _ungroup(matmul(s["u"], d_proj, "tn", name="g_w_in", bm=1024, bn=640, bk=lp, out_dtype=BF16))
    d_u = matmul(d_proj, p["wp"], "nt", name="d_u", bm=lp // 2, bn=1024, bk=640)
    (d_hn,), (g_norm,) = rowmap_bwd("rms_bwd", _f_rmsnorm, [(s["h"], 0)], [(p["norm_g"], False)], [d_u], width=D_MODEL, ncol=1, **kw)
    g["norm_g"] = g_norm[0]
    return d_h + d_hn, g


def kernel(x, meta_tokens, norm_g, w_in, gdn_conv_w, gdn_a_log, gdn_dt_bias, gdn_norm_g, ssm_conv_w, ssm_conv_b, ssm_a_log, ssm_dt_bias, ssm_d, ssm_norm_g, w_branch_a, w_branch_b, w_branch_c, w_out, final_norm_g, loss_target, m_meta_tokens, m_norm_g, m_w_in, m_gdn_conv_w, m_gdn_a_log, m_gdn_dt_bias, m_gdn_norm_g, m_ssm_conv_w, m_ssm_conv_b, m_ssm_a_log, m_ssm_dt_bias, m_ssm_d, m_ssm_norm_g, m_w_branch_a, m_w_branch_b, m_w_branch_c, m_w_out, m_final_norm_g, v_meta_tokens, v_norm_g, v_w_in, v_gdn_conv_w, v_gdn_a_log, v_gdn_dt_bias, v_gdn_norm_g, v_ssm_conv_w, v_ssm_conv_b, v_ssm_a_log, v_ssm_dt_bias, v_ssm_d, v_ssm_norm_g, v_w_branch_a, v_w_branch_b, v_w_branch_c, v_w_out, v_final_norm_g):
    w = dict(meta_tokens=meta_tokens, norm_g=norm_g, w_in=w_in, gdn_conv_w=gdn_conv_w, gdn_a_log=gdn_a_log, gdn_dt_bias=gdn_dt_bias,
             gdn_norm_g=gdn_norm_g, ssm_conv_w=ssm_conv_w, ssm_conv_b=ssm_conv_b, ssm_a_log=ssm_a_log, ssm_dt_bias=ssm_dt_bias,
             ssm_d=ssm_d, ssm_norm_g=ssm_norm_g, w_branch_a=w_branch_a, w_branch_b=w_branch_b, w_branch_c=w_branch_c, w_out=w_out,
             final_norm_g=final_norm_g)
    m = dict(meta_tokens=m_meta_tokens, norm_g=m_norm_g, w_in=m_w_in, gdn_conv_w=m_gdn_conv_w, gdn_a_log=m_gdn_a_log,
             gdn_dt_bias=m_gdn_dt_bias, gdn_norm_g=m_gdn_norm_g, ssm_conv_w=m_ssm_conv_w, ssm_conv_b=m_ssm_conv_b,
             ssm_a_log=m_ssm_a_log, ssm_dt_bias=m_ssm_dt_bias, ssm_d=m_ssm_d, ssm_norm_g=m_ssm_norm_g, w_branch_a=m_w_branch_a,
             w_branch_b=m_w_branch_b, w_branch_c=m_w_branch_c, w_out=m_w_out, final_norm_g=m_final_norm_g)
    v = dict(meta_tokens=v_meta_tokens, norm_g=v_norm_g, w_in=v_w_in, gdn_conv_w=v_gdn_conv_w, gdn_a_log=v_gdn_a_log,
             gdn_dt_bias=v_gdn_dt_bias, gdn_norm_g=v_gdn_norm_g, ssm_conv_w=v_ssm_conv_w, ssm_conv_b=v_ssm_conv_b,
             ssm_a_log=v_ssm_a_log, ssm_dt_bias=v_ssm_dt_bias, ssm_d=v_ssm_d, ssm_norm_g=v_ssm_norm_g, w_branch_a=v_w_branch_a,
             w_branch_b=v_w_branch_b, w_branch_c=v_w_branch_c, w_out=v_w_out, final_norm_g=v_final_norm_g)
    depth = norm_g.shape[0]
    full = _gather_weights(w)
    params = [_layer_params(w, full, l) for l in range(depth)]

    h = jnp.concatenate([jnp.zeros((PAD, D_MODEL), F32), full["meta_tokens"], x[0]], axis=0)
    saved = []
    for l in range(depth):
        h, s = _layer_fwd(h, params[l])
        saved.append(s)
    loss, d_h, g_final = loss_head(h, loss_target[0], final_norm_g[None])
    layer_grads = [None] * depth
    for l in reversed(range(depth)):
        d_h, layer_grads[l] = _layer_bwd(d_h, params[l], saved[l])
    grads = {n: jnp.stack([layer_grads[l][n] for l in range(depth)]) for n in WEIGHTS if n not in ("meta_tokens", "final_norm_g")}
    grads["meta_tokens"] = d_h[PAD:PAD + N_META]
    grads["final_norm_g"] = g_final[0]
    grad_x = d_h[PAD + N_META:][None]

    red, small_red = _reduce_grads(grads, w)
    delta, new_m, new_v = {}, {}, {}
    for n in SHARDED:
        two_d = lambda a: a.reshape(-1, a.shape[-1])
        d2, m2, v2 = adamw("adamw_" + n, two_d(w[n]), two_d(red[n]), two_d(m[n]), two_d(v[n]))
        delta[n], new_m[n], new_v[n] = d2.reshape(w[n].shape), m2.reshape(w[n].shape), v2.reshape(w[n].shape)
    pack_small = lambda d: _pack([d[n].reshape(-1) for n in REPLICATED], row_unit=32)
    small = adamw("adamw_small", pack_small(w), small_red, pack_small(m), pack_small(v))
    shapes = [w[n].shape for n in REPLICATED]
    red.update(zip(REPLICATED, _unpack(small_red, shapes)))
    for d, buf in zip((delta, new_m, new_v), small):
        d.update(zip(REPLICATED, _unpack(buf, shapes)))
    total_loss = lax.psum(loss[0, 0], ("x", "y", "c"))
    return (total_loss, grad_x, *[red[n] for n in WEIGHTS], *[delta[n] for n in WEIGHTS], *[new_m[n] for n in WEIGHTS],
            *[new_v[n] for n in WEIGHTS])
```

```python
import functools
import math

import jax
import jax.numpy as jnp
from jax import lax
from jax.experimental import pallas as pl
from jax.experimental.pallas import tpu as pltpu

F32 = jnp.float32
BF16 = jnp.bfloat16

N_META = 16
RMS_EPS = 1e-6
L2_EPS = 1e-6
CONV_K = 4
D_MODEL = 1024
HEAD_DIM = 128
N_HEADS = 8
CHUNK = 64
SB_BLOCK = 128
PAD = SB_BLOCK - N_META
SSM_INNER = 2048
SSM_P = 64
SSM_HEADS = 32
SSM_GROUPS = 2
SSM_HG = SSM_HEADS // SSM_GROUPS
SSM_N = 128
VMEM_LIMIT = 56 * 1024 * 1024

def _dims(mode, ndim):
    lhs, rhs = {"nn": (1, 0), "nt": (1, 1), "tn": (0, 0)}[mode]
    off = ndim - 2
    return (((lhs + off,), (rhs + off,)), (tuple(range(off)), tuple(range(off))))


def _dot(a, b, mode):
    return lax.dot_general(a, b, _dims(mode, a.ndim), preferred_element_type=F32)


def _halves(a):
    hi = a.astype(BF16)
    return hi, (a - hi.astype(F32)).astype(BF16)


def _mm_raw(a, b, mode, kind):
    if kind == "bf16":
        return _dot(a.astype(BF16), b.astype(BF16), mode)
    if kind == "lhs01":
        hi, lo = _halves(b)
        a = a.astype(BF16)
        return _dot(a, hi, mode) + _dot(a, lo, mode)
    if kind == "rhs01":
        hi, lo = _halves(a)
        b = b.astype(BF16)
        return _dot(hi, b, mode) + _dot(lo, b, mode)
    a_hi, a_lo = _halves(a)
    b_hi, b_lo = _halves(b)
    return _dot(a_hi, b_hi, mode) + (_dot(a_hi, b_lo, mode) + _dot(a_lo, b_hi, mode))


@functools.partial(jax.custom_vjp, nondiff_argnums=(2, 3))
def _mm(a, b, mode="nn", kind="bf16"):
    return _mm_raw(a, b, mode, kind)


def _mm_fwd(a, b, mode, kind):
    return _mm_raw(a, b, mode, kind), (a, b)


def _mm_bwd(mode, kind, res, g):
    a, b = res
    if kind == "lhs01":
        return jnp.zeros_like(a), _mm_raw(a, g, {"nn": "tn", "tn": "nn"}[mode], "lhs01")
    if kind == "rhs01":
        return _mm_raw(g, b, {"nn": "nt", "nt": "nn"}[mode], "rhs01"), jnp.zeros_like(b)
    if mode == "nn":
        return _mm_raw(g, b, "nt", kind), _mm_raw(a, g, "tn", kind)
    if mode == "nt":
        return _mm_raw(g, b, "nn", kind), _mm_raw(g, a, "tn", kind)
    return _mm_raw(b, g, "nt", kind), _mm_raw(a, g, "nn", kind)


_mm.defvjp(_mm_fwd, _mm_bwd)


def _iota2(shape, axis):
    return lax.broadcasted_iota(jnp.int32, shape, axis)


def _inv_unit_lower_raw(m):
    size = m.shape[-1]
    eye = (_iota2((size, size), 0) == _iota2((size, size), 1)).astype(F32)
    n = -m
    t = eye + n
    p = n
    steps = int(math.log2(size)) - 1
    for _ in range(steps):
        p = _mm_raw(p, p, "nn", "x3")
        t = t + _mm_raw(t, p, "nn", "x3")
    return t


@jax.custom_vjp
def _inv_unit_lower(m):
    return _inv_unit_lower_raw(m)


def _inv_fwd(m):
    t = _inv_unit_lower_raw(m)
    return t, t


def _inv_bwd(t, g):
    return (-_mm_raw(_mm_raw(t, g, "tn", "x3"), t, "nt", "x3"),)


_inv_unit_lower.defvjp(_inv_fwd, _inv_bwd)


def _safe_decay(col, row, keep):
    return jnp.where(keep, jnp.exp(jnp.where(keep, col - row, 0.0)), 0.0)


def _col_to_row(col):
    n = col.shape[-2]
    eye = _iota2((n, n), 0) == _iota2((n, n), 1)
    return jnp.sum(jnp.where(eye, col, 0.0), axis=-2, keepdims=True)


def _cumsum_col(col):
    n = col.shape[-2]
    li, si = _iota2((n, n), 0), _iota2((n, n), 1)
    row = _col_to_row(col)
    c_col = jnp.sum(jnp.where(li >= si, row, 0.0), axis=-1, keepdims=True)
    c_row = jnp.sum(jnp.where(li <= si, col, 0.0), axis=-2, keepdims=True)
    return c_col, c_row


def _gdn_chunk(q, k, v, g, beta, state):
    cl = q.shape[-2]
    li, si = _iota2((cl, cl), 0), _iota2((cl, cl), 1)
    gc_col, gc_row = _cumsum_col(g)
    g_last = jnp.sum(g, axis=-2, keepdims=True)
    dec_strict = _safe_decay(gc_col, gc_row, li > si)
    dec_incl = _safe_decay(gc_col, gc_row, li >= si)
    e_gc = jnp.exp(gc_col)
    qs = q * (HEAD_DIM ** -0.5)
    kb = k * beta
    m = _mm(kb, k, "nt") * dec_strict
    t_inv = _inv_unit_lower(m)
    u = _mm(t_inv, v * beta)
    w = _mm(t_inv, kb * e_gc)
    a_qk = _mm(qs, k, "nt") * dec_incl
    q_dec = qs * e_gc
    k_end = k * jnp.exp(g_last - gc_col)
    v_new = u - _mm(w, state)
    o = _mm(q_dec, state) + _mm(a_qk, v_new)
    new_state = state * jnp.exp(g_last) + _mm(k_end, v_new, "tn")
    return o, new_state


def _gdn_operands(qkv_ref, gt):
    nh, width = N_HEADS, N_HEADS * HEAD_DIM
    heads = lambda off: jnp.stack([qkv_ref[:, off + h * HEAD_DIM:off + (h + 1) * HEAD_DIM] for h in range(nh)])
    cols = lambda off: jnp.stack([gt[:, off + h:off + h + 1] for h in range(nh)])
    return heads(0), heads(width), heads(2 * width), cols(nh), cols(0)


def gdn_fwd(qkv, gates):
    lp = qkv.shape[0]
    nh = N_HEADS
    nc = lp // CHUNK
    width = nh * HEAD_DIM

    def body(qkv_ref, gt_ref, o_ref, s_ref, state):
        @pl.when(pl.program_id(0) == 0)
        def _():
            state[...] = jnp.zeros_like(state)

        s_in = state[...]
        s_ref[0] = s_in
        o, s_new = _gdn_chunk(*_gdn_operands(qkv_ref, gt_ref[...]), s_in)
        for h in range(nh):
            o_ref[:, h * HEAD_DIM:(h + 1) * HEAD_DIM] = o[h]
        state[...] = s_new

    return pl.pallas_call(
        body, name="gdn_fwd", grid=(nc,),
        in_specs=[pl.BlockSpec((CHUNK, 3 * width), lambda c: (c, 0)), pl.BlockSpec((CHUNK, HEAD_DIM), lambda c: (c, 0))],
        out_specs=[pl.BlockSpec((CHUNK, width), lambda c: (c, 0)), pl.BlockSpec((1, nh, HEAD_DIM, HEAD_DIM), lambda c: (c, 0, 0, 0))],
        out_shape=[jax.ShapeDtypeStruct((lp, width), F32), jax.ShapeDtypeStruct((nc, nh, HEAD_DIM, HEAD_DIM), F32)],
        scratch_shapes=[pltpu.VMEM((nh, HEAD_DIM, HEAD_DIM), F32)],
        compiler_params=pltpu.CompilerParams(dimension_semantics=("arbitrary",), vmem_limit_bytes=VMEM_LIMIT),
    )(qkv, gates)


def gdn_bwd(qkv, gates, states, d_o):
    lp = qkv.shape[0]
    nh = N_HEADS
    nc = lp // CHUNK
    width = nh * HEAD_DIM

    def body(qkv_ref, gt_ref, s_ref, do_ref, dqkv_ref, dgt_ref, d_state):
        @pl.when(pl.program_id(0) == 0)
        def _():
            d_state[...] = jnp.zeros_like(d_state)

        _, pull = jax.vjp(_gdn_chunk, *_gdn_operands(qkv_ref, gt_ref[...]), s_ref[0])
        d_o = jnp.stack([do_ref[:, h * HEAD_DIM:(h + 1) * HEAD_DIM] for h in range(nh)])
        dq, dk, dv, dg, db, ds = pull((d_o, d_state[...]))
        lane = _iota2((CHUNK, HEAD_DIM), 1)
        d_gt = jnp.zeros((CHUNK, HEAD_DIM), F32)
        for h in range(nh):
            for part, val in enumerate((dq, dk, dv)):
                dqkv_ref[:, part * width + h * HEAD_DIM:part * width + (h + 1) * HEAD_DIM] = val[h]
            d_gt = d_gt + jnp.where(lane == h, db[h], 0.0) + jnp.where(lane == nh + h, dg[h], 0.0)
        dgt_ref[...] = d_gt
        d_state[...] = ds

    rev = lambda c: (nc - 1 - c, 0)
    return pl.pallas_call(
        body, name="gdn_bwd", grid=(nc,),
        in_specs=[pl.BlockSpec((CHUNK, 3 * width), rev), pl.BlockSpec((CHUNK, HEAD_DIM), rev),
                  pl.BlockSpec((1, nh, HEAD_DIM, HEAD_DIM), lambda c: (nc - 1 - c, 0, 0, 0)), pl.BlockSpec((CHUNK, width), rev)],
        out_specs=[pl.BlockSpec((CHUNK, 3 * width), rev), pl.BlockSpec((CHUNK, HEAD_DIM), rev)],
        out_shape=[jax.ShapeDtypeStruct((lp, 3 * width), F32), jax.ShapeDtypeStruct((lp, HEAD_DIM), F32)],
        scratch_shapes=[pltpu.VMEM((nh, HEAD_DIM, HEAD_DIM), F32)],
        compiler_params=pltpu.CompilerParams(dimension_semantics=("arbitrary",), vmem_limit_bytes=VMEM_LIMIT),
    )(qkv, gates, states, d_o)


def _head_expand():
    width = SSM_HG * SSM_P
    return (_iota2((SSM_HG, width), 1) // SSM_P == _iota2((SSM_HG, width), 0)).astype(F32)


def _ssd_chunk(x, b, c, dt, la, state):
    cl = x.shape[0]
    li, si = _iota2((cl, cl), 0), _iota2((cl, cl), 1)
    causal = li >= si
    expand = _head_expand()
    tri = causal.astype(F32)
    xs = x * _mm(dt, expand, "nn", "rhs01")
    la_x = _mm(la, expand, "nn", "rhs01")
    cs_x = _mm(tri, la_x, "nn", "lhs01")
    last_x = jnp.sum(la_x, axis=0, keepdims=True)
    cs = _mm(tri, la, "nn", "lhs01")
    scores = _mm(c, b, "nt")
    head_id = _iota2((1, SSM_HG), 1)
    per_tile = HEAD_DIM // SSM_P
    tile_head = _iota2((1, HEAD_DIM), 1) // SSM_P
    within = []
    for t in range(SSM_HG // per_tile):
        xs_t = xs[:, t * HEAD_DIM:(t + 1) * HEAD_DIM]
        acc = jnp.zeros((cl, HEAD_DIM), F32)
        for hh in range(per_tile):
            cs_col = jnp.sum(jnp.where(head_id == t * per_tile + hh, cs, 0.0), axis=1, keepdims=True)
            decay = _safe_decay(cs_col, _col_to_row(cs_col), causal)
            acc = acc + _mm(scores * decay, jnp.where(tile_head == hh, xs_t, 0.0))
        within.append(acc)
    y = _mm(c, state) * jnp.exp(cs_x) + jnp.concatenate(within, axis=1)
    new_state = state * jnp.exp(last_x) + _mm(b, xs * jnp.exp(last_x - cs_x), "tn")
    return y, new_state


GATE_DT = 16


def _place_lanes(v, lo):
    n = v.shape[1]
    sel = (_iota2((n, HEAD_DIM), 1) == _iota2((n, HEAD_DIM), 0) + lo).astype(F32)
    return _mm_raw(v, sel, "nn", "rhs01")


def ssd_fwd(xbc, first, second):
    lp = xbc.shape[0]
    nc = lp // CHUNK
    width = SSM_HG * SSM_P
    b_off, c_off = SSM_INNER, SSM_INNER + SSM_GROUPS * SSM_N

    def body(x_ref, f_ref, s2_ref, y_ref, s_ref, state):
        @pl.when(pl.program_id(0) == 0)
        def _():
            state[...] = jnp.zeros_like(state)

        f, s2 = f_ref[...], s2_ref[...]
        for g in range(SSM_GROUPS):
            lo = GATE_DT + g * SSM_HG
            s_in = state[g]
            s_ref[0, g] = s_in
            y, s_new = _ssd_chunk(x_ref[:, g * width:(g + 1) * width], x_ref[:, b_off + g * SSM_N:b_off + (g + 1) * SSM_N],
                                  x_ref[:, c_off + g * SSM_N:c_off + (g + 1) * SSM_N], f[:, lo:lo + SSM_HG], s2[:, lo:lo + SSM_HG], s_in)
            y_ref[:, g * width:(g + 1) * width] = y
            state[g] = s_new

    row = lambda cols: pl.BlockSpec((CHUNK, cols), lambda k: (k, 0))
    return pl.pallas_call(
        body, name="ssd_fwd", grid=(nc,),
        in_specs=[row(xbc.shape[1]), row(HEAD_DIM), row(HEAD_DIM)],
        out_specs=[row(SSM_INNER), pl.BlockSpec((1, SSM_GROUPS, SSM_N, width), lambda k: (k, 0, 0, 0))],
        out_shape=[jax.ShapeDtypeStruct((lp, SSM_INNER), F32), jax.ShapeDtypeStruct((nc, SSM_GROUPS, SSM_N, width), F32)],
        scratch_shapes=[pltpu.VMEM((SSM_GROUPS, SSM_N, width), F32)],
        compiler_params=pltpu.CompilerParams(dimension_semantics=("arbitrary",), vmem_limit_bytes=VMEM_LIMIT),
    )(xbc, first, second)


def ssd_bwd(xbc, first, second, states, d_y, d_xh):
    lp = xbc.shape[0]
    nc = lp // CHUNK
    width = SSM_HG * SSM_P
    b_off, c_off = SSM_INNER, SSM_INNER + SSM_GROUPS * SSM_N

    def body(x_ref, f_ref, s2_ref, s_ref, dy_ref, dxh_ref, dx_ref, df_ref, ds2_ref, d_state):
        @pl.when(pl.program_id(0) == 0)
        def _():
            d_state[...] = jnp.zeros_like(d_state)

        f, s2 = f_ref[...], s2_ref[...]
        d_f = jnp.zeros((CHUNK, HEAD_DIM), F32)
        d_s2 = jnp.zeros((CHUNK, HEAD_DIM), F32)
        for g in range(SSM_GROUPS):
            lo = GATE_DT + g * SSM_HG
            x_l = slice(g * width, (g + 1) * width)
            b_l = slice(b_off + g * SSM_N, b_off + (g + 1) * SSM_N)
            c_l = slice(c_off + g * SSM_N, c_off + (g + 1) * SSM_N)
            _, pull = jax.vjp(_ssd_chunk, x_ref[:, x_l], x_ref[:, b_l], x_ref[:, c_l], f[:, lo:lo + SSM_HG], s2[:, lo:lo + SSM_HG],
                              s_ref[0, g])
            dx, db, dc, ddt, dla, ds = pull((dy_ref[:, x_l], d_state[g]))
            dx_ref[:, x_l] = dx + dxh_ref[:, x_l]
            dx_ref[:, b_l] = db
            dx_ref[:, c_l] = dc
            d_f = d_f + _place_lanes(ddt, lo)
            d_s2 = d_s2 + _place_lanes(dla, lo)
            d_state[g] = ds
        df_ref[...] = d_f
        ds2_ref[...] = d_s2

    row = lambda cols: pl.BlockSpec((CHUNK, cols), lambda k: (nc - 1 - k, 0))
    gate_shape = jax.ShapeDtypeStruct((lp, HEAD_DIM), F32)
    return pl.pallas_call(
        body, name="ssd_bwd", grid=(nc,),
        in_specs=[row(xbc.shape[1]), row(HEAD_DIM), row(HEAD_DIM),
                  pl.BlockSpec((1, SSM_GROUPS, SSM_N, width), lambda k: (nc - 1 - k, 0, 0, 0)), row(SSM_INNER), row(SSM_INNER)],
        out_specs=[row(xbc.shape[1]), row(HEAD_DIM), row(HEAD_DIM)],
        out_shape=[jax.ShapeDtypeStruct(xbc.shape, F32), gate_shape, gate_shape],
        scratch_shapes=[pltpu.VMEM((SSM_GROUPS, SSM_N, width), F32)],
        compiler_params=pltpu.CompilerParams(dimension_semantics=("arbitrary",), vmem_limit_bytes=VMEM_LIMIT),
    )(xbc, first, second, states, d_y, d_xh)


SB_QROWS = 544


def _mm_tri(a, tri):
    return _mm_raw(a, tri.astype(BF16), "nn", "rhs01")


def _sb_scores(q_scaled, kb, row0, j):
    shape = (q_scaled.shape[0], SB_BLOCK)
    z = _mm_raw(q_scaled, kb, "nt", "bf16")
    q_pos = row0 + _iota2(shape, 0)
    k_pos = j * SB_BLOCK + _iota2(shape, 1)
    valid = (k_pos < q_pos) & (k_pos >= PAD)
    sp = jnp.maximum(z, 0.0) + jnp.log(1.0 + jnp.exp(-jnp.abs(z)))
    lk = jnp.where(valid, -sp, 0.0)
    return z, sp, valid, lk


def sb_fwd(src, offs):
    lp = src.shape[0]
    nh = N_HEADS
    qb = _pick(lp, SB_QROWS, 8)
    scale = HEAD_DIM ** -0.5
    blk = SB_BLOCK

    def body(q_ref, k_ref, v_ref, o_ref, c_ref):
        i = pl.program_id(1)
        q_scaled = q_ref[...] * scale
        upper = _iota2((blk, blk), 0) > _iota2((blk, blk), 1)
        n_blocks = ((i + 1) * qb + blk - 1) // blk

        def step(it, carry):
            acc, c = carry
            j = n_blocks - 1 - it
            rows = pl.ds(pl.multiple_of(j * blk, blk), blk)
            z, sp, valid, lk = _sb_scores(q_scaled, k_ref[rows, :], i * qb, j)
            later = _mm_tri(lk, upper) + c
            w = jnp.where(valid, jnp.exp(z - sp + later), 0.0)
            acc = acc + _mm_raw(w, v_ref[rows, :], "nn", "bf16")
            return acc, c + jnp.sum(lk, axis=1, keepdims=True)

        acc, c = lax.fori_loop(0, n_blocks, step, (jnp.zeros((qb, HEAD_DIM), F32), jnp.zeros((qb, 1), F32)))
        o_ref[...] = acc
        c_ref[0] = c

    qspec = pl.BlockSpec((qb, HEAD_DIM), lambda h, i: (i, offs[0] + h))
    kspec = pl.BlockSpec((lp, HEAD_DIM), lambda h, i: (0, offs[1] + h))
    vspec = pl.BlockSpec((lp, HEAD_DIM), lambda h, i: (0, offs[2] + h))
    ospec = pl.BlockSpec((qb, HEAD_DIM), lambda h, i: (i, h))
    cspec = pl.BlockSpec((1, qb, 1), lambda h, i: (h, i, 0))
    return pl.pallas_call(
        body, name="sb_fwd", grid=(nh, lp // qb), in_specs=[qspec, kspec, vspec], out_specs=[ospec, cspec],
        out_shape=[jax.ShapeDtypeStruct((lp, nh * HEAD_DIM), F32), jax.ShapeDtypeStruct((nh, lp, 1), F32)],
        compiler_params=pltpu.CompilerParams(dimension_semantics=("arbitrary", "arbitrary"), vmem_limit_bytes=VMEM_LIMIT),
    )(src, src, src)


def sb_bwd(src, offs, csum, d_o):
    lp = src.shape[0]
    nh = N_HEADS
    qb = _pick(lp, SB_QROWS, 8)
    scale = HEAD_DIM ** -0.5
    blk = SB_BLOCK

    def body(q_ref, k_ref, v_ref, c_ref, do_ref, dq_ref, dk_ref, dv_ref):
        i = pl.program_id(1)

        @pl.when(i == 0)
        def _():
            dk_ref[...] = jnp.zeros_like(dk_ref)
            dv_ref[...] = jnp.zeros_like(dv_ref)

        q_scaled = q_ref[...] * scale
        d_out = do_ref[...]
        total = c_ref[0]
        lower_incl = _iota2((blk, blk), 0) <= _iota2((blk, blk), 1)
        lower = _iota2((blk, blk), 0) < _iota2((blk, blk), 1)
        n_blocks = ((i + 1) * qb + blk - 1) // blk

        def step(j, carry):
            acc, cp, ep = carry
            rows = pl.ds(pl.multiple_of(j * blk, blk), blk)
            kb = k_ref[rows, :]
            vb = v_ref[rows, :]
            z, sp, valid, lk = _sb_scores(q_scaled, kb, i * qb, j)
            later = total - cp - _mm_tri(lk, lower_incl)
            w = jnp.where(valid, jnp.exp(z - sp + later), 0.0)
            e = w * _mm_raw(d_out, vb, "nt", "bf16")
            before = ep + _mm_tri(e, lower)
            dz = jnp.where(valid, e * jnp.exp(-sp) - before * jnp.exp(z - sp), 0.0)
            dk_ref[rows, :] += _mm_raw(dz, q_scaled, "tn", "bf16")
            dv_ref[rows, :] += _mm_raw(w, d_out, "tn", "bf16")
            acc = acc + _mm_raw(dz, kb, "nn", "bf16")
            return acc, cp + jnp.sum(lk, axis=1, keepdims=True), ep + jnp.sum(e, axis=1, keepdims=True)

        zero_col = jnp.zeros((qb, 1), F32)
        acc, _, _ = lax.fori_loop(0, n_blocks, step, (jnp.zeros((qb, HEAD_DIM), F32), zero_col, zero_col))
        dq_ref[...] = acc * scale

    qspec = pl.BlockSpec((qb, HEAD_DIM), lambda h, i: (i, offs[0] + h))
    kspec = pl.BlockSpec((lp, HEAD_DIM), lambda h, i: (0, offs[1] + h))
    vspec = pl.BlockSpec((lp, HEAD_DIM), lambda h, i: (0, offs[2] + h))
    ospec = pl.BlockSpec((qb, HEAD_DIM), lambda h, i: (i, h))
    fullspec = pl.BlockSpec((lp, HEAD_DIM), lambda h, i: (0, h))
    cspec = pl.BlockSpec((1, qb, 1), lambda h, i: (h, i, 0))
    return pl.pallas_call(
        body, name="sb_bwd", grid=(nh, lp // qb), in_specs=[qspec, kspec, vspec, cspec, ospec], out_specs=[ospec, fullspec, fullspec],
        out_shape=[jax.ShapeDtypeStruct((lp, nh * HEAD_DIM), F32)] * 3,
        compiler_params=pltpu.CompilerParams(dimension_semantics=("arbitrary", "arbitrary"), vmem_limit_bytes=VMEM_LIMIT),
    )(src, src, src, csum, d_o)


def _pick(n, target, unit):
    if n <= target:
        return n
    best = None
    for d in range(unit, target + 1, unit):
        if n % d == 0:
            best = d
    assert best is not None, (n, target, unit)
    return best


def matmul(a, b, mode="nn", *, name, bm=1088, bn=640, bk=2176, residual=None, out_dtype=F32):
    if mode == "nn":
        (m, k), n = a.shape, b.shape[1]
    elif mode == "nt":
        (m, k), n = a.shape, b.shape[0]
    else:
        (k, m), n = a.shape, b.shape[1]
    bm = _pick(m, bm, 128 if mode == "tn" else 8)
    bn = _pick(n, bn, 128 if mode != "nt" else 8)
    bk = _pick(k, bk, 128 if mode != "tn" else 8)
    nk = k // bk

    def body(*refs):
        if residual is None:
            a_ref, b_ref, o_ref, acc = refs
            r_ref = None
        else:
            a_ref, b_ref, r_ref, o_ref, acc = refs
        kk = pl.program_id(2)
        part = _mm_raw(a_ref[...], b_ref[...], mode, "bf16")

        @pl.when(kk == 0)
        def _():
            acc[...] = part

        @pl.when(kk > 0)
        def _():
            acc[...] += part

        @pl.when(kk == nk - 1)
        def _():
            res = acc[...]
            if r_ref is not None:
                res = res + r_ref[...]
            o_ref[...] = res.astype(out_dtype)

    a_spec = pl.BlockSpec((bk, bm), lambda i, j, kk: (kk, i)) if mode == "tn" else pl.BlockSpec((bm, bk), lambda i, j, kk: (i, kk))
    b_spec = pl.BlockSpec((bn, bk), lambda i, j, kk: (j, kk)) if mode == "nt" else pl.BlockSpec((bk, bn), lambda i, j, kk: (kk, j))
    o_spec = pl.BlockSpec((bm, bn), lambda i, j, kk: (i, j))
    ins, specs = [a, b], [a_spec, b_spec]
    if residual is not None:
        ins.append(residual)
        specs.append(o_spec)
    return pl.pallas_call(
        body, name=name, grid=(m // bm, n // bn, nk), in_specs=specs, out_specs=o_spec,
        out_shape=jax.ShapeDtypeStruct((m, n), out_dtype),
        scratch_shapes=[pltpu.VMEM((bm, bn), F32)],
        compiler_params=pltpu.CompilerParams(dimension_semantics=("arbitrary", "arbitrary", "arbitrary"), vmem_limit_bytes=VMEM_LIMIT),
    )(*ins)


def _row_specs(rows, params, width, bm):
    row_specs = [pl.BlockSpec((bm, width), (lambda j, i, off=off: (i, off + j))) for _, off in rows]
    par_specs = [pl.BlockSpec((p.shape[0], width) if per_col else p.shape, ((lambda j, i: (0, j)) if per_col else (lambda j, i: (0, 0))))
                 for p, per_col in params]
    return row_specs, par_specs


def rowmap_fwd(name, fn, rows, params, n_out, *, width, ncol, bm, lp):
    row_specs, par_specs = _row_specs(rows, params, width, bm)
    nr = len(rows)

    def body(*refs):
        ins, outs = refs[:nr + len(params)], refs[nr + len(params):]
        row_ids = pl.program_id(1) * bm + _iota2((bm, 1), 0)
        res = fn(row_ids, *[r[...] for r in ins])
        for o_ref, val in zip(outs, res):
            o_ref[...] = val

    o_spec = pl.BlockSpec((bm, width), lambda j, i: (i, j))
    return pl.pallas_call(
        body, name=name, grid=(ncol, lp // bm), in_specs=row_specs + par_specs, out_specs=[o_spec] * n_out,
        out_shape=[jax.ShapeDtypeStruct((lp, ncol * width), F32)] * n_out,
        compiler_params=pltpu.CompilerParams(dimension_semantics=("arbitrary", "arbitrary"), vmem_limit_bytes=VMEM_LIMIT),
    )(*[a for a, _ in rows], *[p for p, _ in params])


def rowmap_bwd(name, fn, rows, params, d_outs, *, width, ncol, bm, lp):
    row_specs, par_specs = _row_specs(rows, params, width, bm)
    nr, npar, nout = len(rows), len(params), len(d_outs)

    def body(*refs):
        ins = refs[:nr + npar]
        dos = refs[nr + npar:nr + npar + nout]
        d_rows = refs[nr + npar + nout:nr + npar + nout + nr]
        d_pars = refs[nr + npar + nout + nr:]
        j, i = pl.program_id(0), pl.program_id(1)
        row_ids = i * bm + _iota2((bm, 1), 0)
        _, pull = jax.vjp(lambda *xs: tuple(fn(row_ids, *xs)), *[r[...] for r in ins])
        grads = pull(tuple(d[...] for d in dos))
        for ref, val in zip(d_rows, grads[:nr]):
            ref[...] = val
        for ref, val, (_, per_col) in zip(d_pars, grads[nr:], params):
            first = (i == 0) if per_col else ((i == 0) & (j == 0))

            @pl.when(first)
            def _(ref=ref, val=val):
                ref[...] = val

            @pl.when(jnp.logical_not(first))
            def _(ref=ref, val=val):
                ref[...] += val

    o_spec = pl.BlockSpec((bm, width), lambda j, i: (i, j))
    res = pl.pallas_call(
        body, name=name, grid=(ncol, lp // bm), in_specs=row_specs + par_specs + [o_spec] * nout,
        out_specs=[o_spec] * nr + par_specs,
        out_shape=[jax.ShapeDtypeStruct((lp, ncol * width), F32)] * nr + [jax.ShapeDtypeStruct(p.shape, F32) for p, _ in params],
        compiler_params=pltpu.CompilerParams(dimension_semantics=("arbitrary", "arbitrary"), vmem_limit_bytes=VMEM_LIMIT),
    )(*[a for a, _ in rows], *[p for p, _ in params], *d_outs)
    return res[:nr], res[nr:]


def _silu(x):
    return x * jax.nn.sigmoid(x)


def _softplus(x):
    return jnp.maximum(x, 0.0) + jnp.log(1.0 + jnp.exp(-jnp.abs(x)))


def _real_rows(row_ids):
    return (row_ids >= PAD).astype(F32)


def _f_rmsnorm(row_ids, h, g):
    return (h * lax.rsqrt(jnp.mean(h * h, axis=-1, keepdims=True) + RMS_EPS) * g,)


def _f_small_gates(row_ids, small, bias, a_log):
    lane = _iota2(small.shape, 1)
    t = small + bias
    sp = _softplus(t)
    coef = -jnp.exp(a_log)
    keep = _real_rows(row_ids)
    first = jnp.where(lane < 8, jax.nn.sigmoid(t), jnp.where(lane < 16, coef * sp, jnp.where(lane < 48, sp, 0.0)))
    second = jnp.where((lane >= 8) & (lane < 48), coef * sp, 0.0)
    return first * keep, second * keep


def _f_gate_silu(row_ids, o, z):
    return (o * _silu(z),)


def _f_head_norm_gate(row_ids, o, z, g):
    return (o * lax.rsqrt(jnp.mean(o * o, axis=-1, keepdims=True) + RMS_EPS) * g * _silu(z),)


def _f_ssm_out(row_ids, y, xh, z, d_skip, g):
    t = (y + d_skip * xh) * _silu(z)
    return (t * lax.rsqrt(jnp.mean(t * t, axis=-1, keepdims=True) + RMS_EPS) * g,)


def _f_merge(row_ids, pa, pb, pc, ga, gb, gc):
    return (jax.nn.sigmoid(ga) * pa + jax.nn.sigmoid(gb) * pb + jax.nn.sigmoid(gc) * pc,)


def _shift_rows(x, s):
    s = s % x.shape[0]
    return x if s == 0 else pltpu.roll(x, s, 0)


def _conv_pre(x, w, b):
    pre = b
    for kk in range(CONV_K):
        pre = pre + w[kk:kk + 1, :] * _shift_rows(x, CONV_K - 1 - kk)
    return pre


def _conv_post(pre, l2_flag, keep):
    act = _silu(pre)
    nrm = act * lax.rsqrt(jnp.sum(act * act, axis=-1, keepdims=True) + L2_EPS)
    return (l2_flag * nrm + (1.0 - l2_flag) * act) * keep


def conv_fwd(name, src, col_off, w, b, n_l2):
    lp, ch = src.shape[0], w.shape[1]

    def body(x_ref, w_ref, b_ref, o_ref):
        l2_flag = (pl.program_id(0) < n_l2).astype(F32)
        keep = _real_rows(_iota2((lp, 1), 0))
        o_ref[...] = _conv_post(_conv_pre(x_ref[...], w_ref[...], b_ref[...]), l2_flag, keep)

    return pl.pallas_call(
        body, name=name, grid=(ch // HEAD_DIM,),
        in_specs=[pl.BlockSpec((lp, HEAD_DIM), lambda j: (0, col_off + j)), pl.BlockSpec((CONV_K, HEAD_DIM), lambda j: (0, j)),
                  pl.BlockSpec((1, HEAD_DIM), lambda j: (0, j))],
        out_specs=pl.BlockSpec((lp, HEAD_DIM), lambda j: (0, j)),
        out_shape=jax.ShapeDtypeStruct((lp, ch), F32),
        compiler_params=pltpu.CompilerParams(dimension_semantics=("arbitrary",), vmem_limit_bytes=VMEM_LIMIT),
    )(src, w, b)


def conv_bwd(name, src, col_off, w, b, n_l2, d_out):
    lp, ch = src.shape[0], w.shape[1]

    def body(x_ref, w_ref, b_ref, do_ref, dx_ref, dw_ref, db_ref):
        l2_flag = (pl.program_id(0) < n_l2).astype(F32)
        keep = _real_rows(_iota2((lp, 1), 0))
        x, wv = x_ref[...], w_ref[...]
        pre = _conv_pre(x, wv, b_ref[...])
        _, pull = jax.vjp(lambda p: _conv_post(p, l2_flag, keep), pre)
        (d_pre,) = pull(do_ref[...])
        dx = jnp.zeros_like(x)
        for kk in range(CONV_K):
            s = CONV_K - 1 - kk
            dx = dx + wv[kk:kk + 1, :] * _shift_rows(d_pre, -s)
            dw_ref[kk:kk + 1, :] = jnp.sum(d_pre * _shift_rows(x, s), axis=0, keepdims=True)
        dx_ref[...] = dx * keep
        db_ref[...] = jnp.sum(d_pre, axis=0, keepdims=True)

    seq = pl.BlockSpec((lp, HEAD_DIM), lambda j: (0, j))
    wspec = pl.BlockSpec((CONV_K, HEAD_DIM), lambda j: (0, j))
    bspec = pl.BlockSpec((1, HEAD_DIM), lambda j: (0, j))
    return pl.pallas_call(
        body, name=name, grid=(ch // HEAD_DIM,),
        in_specs=[pl.BlockSpec((lp, HEAD_DIM), lambda j: (0, col_off + j)), wspec, bspec, seq],
        out_specs=[seq, wspec, bspec],
        out_shape=[jax.ShapeDtypeStruct((lp, ch), F32), jax.ShapeDtypeStruct(w.shape, F32), jax.ShapeDtypeStruct(b.shape, F32)],
        compiler_params=pltpu.CompilerParams(dimension_semantics=("arbitrary",), vmem_limit_bytes=VMEM_LIMIT),
    )(src, w, b, d_out)


def loss_head(h, target, g):
    lp, d = h.shape
    bm = SB_BLOCK
    first = (PAD + N_META) // bm

    def body(h_ref, t_ref, g_ref, loss_ref, dh_ref, dg_ref):
        i = pl.program_id(0)
        keep = (i >= first).astype(F32)

        def f(hv, gv):
            y = hv * lax.rsqrt(jnp.mean(hv * hv, axis=-1, keepdims=True) + RMS_EPS) * gv
            err = y - t_ref[...]
            return 0.5 * jnp.sum(jnp.mean(err * err, axis=-1, keepdims=True), axis=0, keepdims=True) * keep

        val, pull = jax.vjp(f, h_ref[...], g_ref[...])
        dh, dg = pull(jnp.ones((1, 1), F32))
        dh_ref[...] = dh

        @pl.when(i == 0)
        def _():
            loss_ref[...] = val
            dg_ref[...] = dg

        @pl.when(i > 0)
        def _():
            loss_ref[...] += val
            dg_ref[...] += dg

    row = pl.BlockSpec((bm, d), lambda i: (i, 0))
    return pl.pallas_call(
        body, name="loss_head", grid=(lp // bm,),
        in_specs=[row, pl.BlockSpec((bm, d), lambda i: (jnp.maximum(i - first, 0), 0)), pl.BlockSpec((1, d), lambda i: (0, 0))],
        out_specs=[pl.BlockSpec((1, 1), lambda i: (0, 0)), row, pl.BlockSpec((1, d), lambda i: (0, 0))],
        out_shape=[jax.ShapeDtypeStruct((1, 1), F32), jax.ShapeDtypeStruct((lp, d), F32), jax.ShapeDtypeStruct((1, d), F32)],
        compiler_params=pltpu.CompilerParams(dimension_semantics=("arbitrary",)),
    )(h, target, g)


ADAM_LR, ADAM_B1, ADAM_B2, ADAM_EPS, ADAM_WD, ADAM_STEP = 0.001, 0.9, 0.999, 1e-08, 0.01, 10


def adamw(name, w, g, m, v):
    rows, cols = w.shape
    br = _pick(rows, 128, 8)

    def body(w_ref, g_ref, m_ref, v_ref, d_ref, nm_ref, nv_ref):
        gv = g_ref[...]
        nm = ADAM_B1 * m_ref[...] + (1.0 - ADAM_B1) * gv
        nv = ADAM_B2 * v_ref[...] + (1.0 - ADAM_B2) * (gv * gv)
        m_hat = nm / (1.0 - ADAM_B1 ** ADAM_STEP)
        v_hat = nv / (1.0 - ADAM_B2 ** ADAM_STEP)
        d_ref[...] = -ADAM_LR * (m_hat / (jnp.sqrt(v_hat) + ADAM_EPS) + ADAM_WD * w_ref[...])
        nm_ref[...] = nm
        nv_ref[...] = nv

    spec = pl.BlockSpec((br, cols), lambda i: (i, 0))
    return pl.pallas_call(
        body, name=name, grid=(rows // br,), in_specs=[spec] * 4, out_specs=[spec] * 3,
        out_shape=[jax.ShapeDtypeStruct(w.shape, F32)] * 3,
        compiler_params=pltpu.CompilerParams(dimension_semantics=("arbitrary",), vmem_limit_bytes=VMEM_LIMIT),
    )(w, g, m, v)


MESH = pl.DeviceIdType.MESH
ANY = pl.BlockSpec(memory_space=pl.ANY)
D2D_PIECES = 16
ICI_PIECES = 4


def _place():
    x, y, c = lax.axis_index("x"), lax.axis_index("y"), lax.axis_index("c")
    return x, y, c, [(1 - x, y), (x, 1 - y), (1 - x, 1 - y)]


def _pieces(rows, n, unit):
    per = -(-rows // (n * unit)) * unit
    return [(s, min(per, rows - s)) for s in range(0, rows, per)]


def _row_unit(dtype):
    return 16 if dtype == BF16 else 8


def _scalar(v):
    return jnp.reshape(v, (1,)).astype(jnp.int32)


def place_shard(name, pack):
    rows, cols = pack.shape
    br = _pick(rows, 256, 16)

    def body(m_ref, p_ref, o_ref):
        o_ref[...] = p_ref[...]

    return pl.pallas_call(
        body, name=name,
        grid_spec=pltpu.PrefetchScalarGridSpec(
            num_scalar_prefetch=1, grid=(rows // br,),
            in_specs=[pl.BlockSpec((br, cols), lambda i, m: (i, 0))],
            out_specs=pl.BlockSpec((None, br, cols), lambda i, m: (m[0], i, 0))),
        out_shape=jax.ShapeDtypeStruct((4, rows, cols), pack.dtype),
        compiler_params=pltpu.CompilerParams(dimension_semantics=("arbitrary",), vmem_limit_bytes=VMEM_LIMIT),
    )(_scalar(2 * lax.axis_index("x") + lax.axis_index("y")), pack)


def gather_shards(name, placed):
    n = len(placed)

    def body(*refs):
        o_refs = refs[n:2 * n]
        send_sems, recv_sems = refs[2 * n:]
        x, y, c, chips = _place()
        mine = 2 * x + y

        def half_rows(b, which, start=0, size=None):
            half = o_refs[b].shape[1] // 2
            return pl.ds(pl.multiple_of(which * half + start, _row_unit(o_refs[b].dtype)), half if size is None else size)

        def remote(b, k, slot, rws, to):
            piece = o_refs[b].at[slot, rws, :]
            return pltpu.make_async_remote_copy(src_ref=piece, dst_ref=piece, send_sem=send_sems.at[b, k], recv_sem=recv_sems.at[b, k],
                                                device_id=to, device_id_type=MESH)

        for b, o_ref in enumerate(o_refs):
            for j, (cx, cy) in enumerate(chips):
                for start, size in _pieces(o_ref.shape[1] // 2, ICI_PIECES, _row_unit(o_ref.dtype)):
                    remote(b, j, mine, half_rows(b, c, start, size), (cx, cy, c)).start()
        sends = []
        for b, o_ref in enumerate(o_refs):
            for j, (cx, cy) in enumerate(chips):
                slot = 2 * cx + cy
                sends.append(remote(b, j, mine, half_rows(b, c), (cx, cy, c)))
                remote(b, j, slot, half_rows(b, c), (cx, cy, c)).wait_recv()
                for start, size in _pieces(o_ref.shape[1] // 2, D2D_PIECES, _row_unit(o_ref.dtype)):
                    remote(b, 3 + j, slot, half_rows(b, c, start, size), (x, y, 1 - c)).start()
                sends.append(remote(b, 3 + j, slot, half_rows(b, c), (x, y, 1 - c)))
        for b in range(n):
            for j, (cx, cy) in enumerate(chips):
                remote(b, 3 + j, 2 * cx + cy, half_rows(b, 1 - c), (x, y, 1 - c)).wait_recv()
        for cp in sends:
            cp.wait_send()

    return pl.pallas_call(
        body, name=name, in_specs=[ANY] * n, out_specs=[ANY] * n,
        out_shape=[jax.ShapeDtypeStruct(p.shape, p.dtype) for p in placed],
        input_output_aliases={i: i for i in range(n)},
        scratch_shapes=[pltpu.SemaphoreType.DMA((n, 6)), pltpu.SemaphoreType.DMA((n, 6))],
    )(*placed)


def pair_split(name, bufs):
    n = len(bufs)

    def body(*refs):
        g_refs, t_refs = refs[:n], refs[n:2 * n]
        send_sems, recv_sems = refs[2 * n:]
        x, y, c, _ = _place()
        waits = []
        for b, (g_ref, t_ref) in enumerate(zip(g_refs, t_refs)):
            half = g_ref.shape[1] // 2
            unit = _row_unit(g_ref.dtype)

            def copy(slots, start, size):
                theirs = pl.ds(pl.multiple_of((1 - c) * half + start, unit), size)
                return pltpu.make_async_remote_copy(src_ref=g_ref.at[slots, theirs, :], dst_ref=t_ref.at[slots, pl.ds(start, size), :],
                                                    send_sem=send_sems.at[b], recv_sem=recv_sems.at[b], device_id=(x, y, 1 - c),
                                                    device_id_type=MESH)

            for s in range(4):
                for start, size in _pieces(half, D2D_PIECES // 4, unit):
                    copy(s, start, size).start()
            waits.append(copy(slice(None), 0, half))
        for cp in waits:
            cp.wait()

    return pl.pallas_call(
        body, name=name, in_specs=[ANY] * n, out_specs=[ANY] * n,
        out_shape=[jax.ShapeDtypeStruct((4, g.shape[1] // 2, g.shape[2]), g.dtype) for g in bufs],
        scratch_shapes=[pltpu.SemaphoreType.DMA((n,)), pltpu.SemaphoreType.DMA((n,))],
    )(*bufs)


def pair_add(name, g, theirs, transit):
    _, rows, cols = g.shape
    half = rows // 2
    br = _pick(half, 128, 16)

    def body(c_ref, g_ref, t_ref, o_ref):
        o_ref[...] = (g_ref[...].astype(F32) + t_ref[...].astype(F32)).astype(transit)

    blk = (4, br, cols)
    return pl.pallas_call(
        body, name=name,
        grid_spec=pltpu.PrefetchScalarGridSpec(
            num_scalar_prefetch=1, grid=(half // br,),
            in_specs=[pl.BlockSpec((4, None, br, cols), lambda i, c: (0, c[0], i, 0)), pl.BlockSpec(blk, lambda i, c: (0, i, 0))],
            out_specs=pl.BlockSpec(blk, lambda i, c: (0, i, 0))),
        out_shape=jax.ShapeDtypeStruct((4, half, cols), transit),
        compiler_params=pltpu.CompilerParams(dimension_semantics=("arbitrary",), vmem_limit_bytes=VMEM_LIMIT),
    )(_scalar(lax.axis_index("c")), g.reshape(4, 2, half, cols), theirs)


def chip_exchange(name, parts):
    n = len(parts)

    def body(*refs):
        a_refs, o_refs = refs[:n], refs[n:2 * n]
        send_sems, recv_sems = refs[2 * n:]
        x, y, c, chips = _place()
        mine = 2 * x + y
        waits = []
        for b, (a_ref, o_ref) in enumerate(zip(a_refs, o_refs)):
            rows = a_ref.shape[1]
            unit = _row_unit(a_ref.dtype)
            for j, (cx, cy) in enumerate(chips):
                def copy(start, size):
                    rws = pl.ds(start, size)
                    return pltpu.make_async_remote_copy(src_ref=a_ref.at[2 * cx + cy, rws, :], dst_ref=o_ref.at[mine, rws, :],
                                                        send_sem=send_sems.at[b, j], recv_sem=recv_sems.at[b, j],
                                                        device_id=(cx, cy, c), device_id_type=MESH)
                for start, size in _pieces(rows, ICI_PIECES, unit):
                    copy(start, size).start()
                waits.append(copy(0, rows))
        for cp in waits:
            cp.wait()

    return pl.pallas_call(
        body, name=name, in_specs=[ANY] * n, out_specs=[ANY] * n,
        out_shape=[jax.ShapeDtypeStruct(a.shape, a.dtype) for a in parts],
        scratch_shapes=[pltpu.SemaphoreType.DMA((n, 3)), pltpu.SemaphoreType.DMA((n, 3))],
    )(*parts)


def chip_add(name, got, part):
    _, rows, cols = got.shape
    br = _pick(rows, 128, 16)
    nblk = rows // br

    def body(m_ref, c_ref, got_ref, part_ref, o_ref):
        mine = m_ref[0]
        for s in range(4):
            @pl.when(mine == s)
            def _(s=s):
                val = part_ref[...].astype(F32)
                o_ref[...] = val if s == 0 else o_ref[...] + val

            @pl.when(mine != s)
            def _(s=s):
                val = got_ref[s].astype(F32)
                o_ref[...] = val if s == 0 else o_ref[...] + val

    return pl.pallas_call(
        body, name=name,
        grid_spec=pltpu.PrefetchScalarGridSpec(
            num_scalar_prefetch=2, grid=(nblk,),
            in_specs=[pl.BlockSpec((4, br, cols), lambda i, m, c: (0, i, 0)),
                      pl.BlockSpec((None, br, cols), lambda i, m, c: (m[0], i, 0))],
            out_specs=pl.BlockSpec((br, cols), lambda i, m, c: (c[0] * nblk + i, 0))),
        out_shape=jax.ShapeDtypeStruct((2 * rows, cols), F32),
        compiler_params=pltpu.CompilerParams(dimension_semantics=("arbitrary",), vmem_limit_bytes=VMEM_LIMIT),
    )(_scalar(2 * lax.axis_index("x") + lax.axis_index("y")), _scalar(lax.axis_index("c")), got, part)


def pair_join(name, fulls):
    n = len(fulls)

    def body(*refs):
        o_refs = refs[n:2 * n]
        send_sems, recv_sems = refs[2 * n:]
        x, y, c, _ = _place()
        waits = []
        for b, o_ref in enumerate(o_refs):
            half = o_ref.shape[0] // 2
            unit = _row_unit(o_ref.dtype)

            def copy(start, size):
                piece = o_ref.at[pl.ds(pl.multiple_of(c * half + start, unit), size), :]
                return pltpu.make_async_remote_copy(src_ref=piece, dst_ref=piece, send_sem=send_sems.at[b], recv_sem=recv_sems.at[b],
                                                    device_id=(x, y, 1 - c), device_id_type=MESH)

            for start, size in _pieces(half, D2D_PIECES, unit):
                copy(start, size).start()
            waits.append(copy(0, half))
        for cp in waits:
            cp.wait()

    return pl.pallas_call(
        body, name=name, in_specs=[ANY] * n, out_specs=[ANY] * n,
        out_shape=[jax.ShapeDtypeStruct(f.shape, f.dtype) for f in fulls],
        input_output_aliases={i: i for i in range(n)},
        scratch_shapes=[pltpu.SemaphoreType.DMA((n,)), pltpu.SemaphoreType.DMA((n,))],
    )(*fulls)


def reduce_scatter(tag, bufs, transits):
    theirs = pair_split(tag + "_pair_split", bufs)
    parts = [pair_add(f"{tag}_pair_add_{i}", g, t, tr) for i, (g, t, tr) in enumerate(zip(bufs, theirs, transits))]
    got = chip_exchange(tag + "_chip_exchange", parts)
    reds = [chip_add(f"{tag}_chip_add_{i}", gt, p) for i, (gt, p) in enumerate(zip(got, parts))]
    return pair_join(tag + "_pair_join", reds)


D_IN = 15920
D_PROJ = 16000
_SEGMENTS = ((0, 8192), (8208, 12816), (12848, 15920), (8192, 8208), (12816, 12848))
OFF_SB_Z, OFF_GDN_QKV, OFF_GDN_Z, OFF_SSM_Z, OFF_SSM_XBC, OFF_GATES, OFF_SMALL = 3072, 4096, 7168, 8192, 10240, 12800, 15872
PACK_C = 1024
WEIGHTS = ("meta_tokens", "norm_g", "w_in", "gdn_conv_w", "gdn_a_log", "gdn_dt_bias", "gdn_norm_g", "ssm_conv_w", "ssm_conv_b",
           "ssm_a_log", "ssm_dt_bias", "ssm_d", "ssm_norm_g", "w_branch_a", "w_branch_b", "w_branch_c", "w_out", "final_norm_g")
SHARDED = ("w_in", "w_branch_a", "w_branch_b", "w_branch_c", "w_out", "gdn_conv_w", "ssm_conv_w", "meta_tokens")
SHARD_AXIS = {"w_in": 2, "w_branch_a": 1, "w_branch_b": 1, "w_branch_c": 1, "w_out": 1, "gdn_conv_w": 2, "ssm_conv_w": 2, "meta_tokens": 1}
BRANCH = ("w_branch_a", "w_branch_b", "w_branch_c", "w_out")
EXACT = ("gdn_conv_w", "ssm_conv_w", "meta_tokens")
REPLICATED = tuple(n for n in WEIGHTS if n not in SHARDED)


def _regroup(w):
    parts = [w[..., a:b] for a, b in _SEGMENTS]
    return jnp.concatenate(parts + [jnp.zeros(w.shape[:-1] + (D_PROJ - D_IN,), w.dtype)], axis=-1)


def _ungroup(g):
    starts, pos = [], 0
    for a, b in _SEGMENTS:
        starts.append(pos)
        pos += b - a
    order = sorted(range(len(_SEGMENTS)), key=lambda i: _SEGMENTS[i][0])
    return jnp.concatenate([g[..., starts[i]:starts[i] + _SEGMENTS[i][1] - _SEGMENTS[i][0]] for i in order], axis=-1)


def _pack(parts, row_unit=64):
    n = sum(p.shape[0] for p in parts)
    rows = -(-n // (PACK_C * row_unit)) * row_unit
    flat = jnp.concatenate(list(parts) + [jnp.zeros((rows * PACK_C - n,), parts[0].dtype)])
    return flat.reshape(rows, PACK_C)


def _unpack(buf, shapes):
    flat, out, pos = buf.reshape(-1), [], 0
    for shp in shapes:
        n = math.prod(shp)
        out.append(flat[pos:pos + n].reshape(shp))
        pos += n
    return out


def _as_bf16_words(a):
    return lax.bitcast_convert_type(a, BF16).reshape(-1)


def _gather_weights(w):
    a = w["w_in"].astype(BF16).reshape(-1, w["w_in"].shape[-1])
    b = jnp.concatenate([w[n].reshape(-1, D_MODEL) for n in BRANCH], axis=0).astype(BF16)
    s = _pack([_as_bf16_words(w[n]) for n in EXACT])
    got_a, got_b, got_s = gather_shards("gather_weights", [place_shard(f"place_{i}", p) for i, p in enumerate((a, b, s))])
    full = {"w_in": jnp.concatenate([got_a[c].reshape(w["w_in"].shape) for c in range(4)], axis=2)}
    pos = 0
    for n in BRANCH:
        rows = w[n].shape[0] * w[n].shape[1]
        full[n] = jnp.concatenate([got_b[c, pos:pos + rows].reshape(w[n].shape) for c in range(4)], axis=1)
        pos += rows
    per_chip = [_unpack(got_s[c], [w[n].shape + (2,) for n in EXACT]) for c in range(4)]
    for i, n in enumerate(EXACT):
        full[n] = jnp.concatenate([lax.bitcast_convert_type(per_chip[c][i], F32) for c in range(4)], axis=SHARD_AXIS[n])
    return full


def _shard(a, axis, s):
    size = a.shape[axis] // 4
    return lax.slice_in_dim(a, s * size, (s + 1) * size, axis=axis)


def _reduce_grads(g, w):
    cols = w["w_in"].shape[-1]
    buf_a = jnp.stack([_shard(g["w_in"], 2, s).astype(BF16).reshape(-1, cols) for s in range(4)])
    buf_b = jnp.stack([jnp.concatenate([_shard(g[n], 1, s).reshape(-1, D_MODEL) for n in BRANCH], axis=0).astype(BF16) for s in range(4)])
    buf_s = jnp.stack([_pack([_shard(g[n], SHARD_AXIS[n], s).astype(BF16).reshape(-1) for n in EXACT], row_unit=32) for s in range(4)])
    small = _pack([g[n].reshape(-1) for n in REPLICATED], row_unit=32)
    buf_r = jnp.broadcast_to(small[None], (4,) + small.shape)
    red_a, red_b, red_s, red_r = reduce_scatter("grads", [buf_a, buf_b, buf_s, buf_r], [BF16, BF16, BF16, F32])
    out = {"w_in": red_a.reshape(w["w_in"].shape)}
    pos = 0
    for n in BRANCH:
        rows = w[n].shape[0] * w[n].shape[1]
        out[n] = red_b[pos:pos + rows].reshape(w[n].shape)
        pos += rows
    out.update(zip(EXACT, _unpack(red_s, [w[n].shape for n in EXACT])))
    return out, red_r


def _layer_params(w, full, l):
    lane = lambda v, lo: jnp.pad(v, (lo, HEAD_DIM - lo - v.shape[0]))[None]
    return dict(
        norm_g=w["norm_g"][l][None], wp=_regroup(full["w_in"][l]),
        gdn_conv_w=full["gdn_conv_w"][l], gdn_conv_b=jnp.zeros((1, 3 * N_HEADS * HEAD_DIM), F32),
        ssm_conv_w=full["ssm_conv_w"][l], ssm_conv_b=w["ssm_conv_b"][l][None],
        bias_vec=lane(w["gdn_dt_bias"][l], 8) + lane(w["ssm_dt_bias"][l], 16),
        alog_vec=lane(w["gdn_a_log"][l], 8) + lane(w["ssm_a_log"][l], 16),
        gdn_norm_g=w["gdn_norm_g"][l][None], d_skip=jnp.repeat(w["ssm_d"][l], SSM_P)[None], ssm_norm_g=w["ssm_norm_g"][l][None],
        wa=full["w_branch_a"][l], wb=full["w_branch_b"][l], wc=full["w_branch_c"][l], wo=full["w_out"][l])


def _layer_fwd(h, p):
    lp = h.shape[0]
    bm = _pick(lp, 272, 8)
    kw = dict(bm=bm, lp=lp)
    (u,) = rowmap_fwd("rms_fwd", _f_rmsnorm, [(h, 0)], [(p["norm_g"], False)], 1, width=D_MODEL, ncol=1, **kw)
    proj = matmul(u, p["wp"], "nn", name="proj", bm=lp, bn=640)
    o_a_raw, csum = sb_fwd(proj, (0, N_HEADS, 2 * N_HEADS))
    qkv = conv_fwd("gdn_conv_fwd", proj, OFF_GDN_QKV // HEAD_DIM, p["gdn_conv_w"], p["gdn_conv_b"], 2 * N_HEADS)
    first, second = rowmap_fwd("gates_fwd", _f_small_gates, [(proj, OFF_SMALL // HEAD_DIM)],
                               [(p["bias_vec"], False), (p["alog_vec"], False)], 2, width=HEAD_DIM, ncol=1, **kw)
    o_b_raw, gdn_states = gdn_fwd(qkv, first)
    xbc = conv_fwd("ssm_conv_fwd", proj, OFF_SSM_XBC // HEAD_DIM, p["ssm_conv_w"], p["ssm_conv_b"], 0)
    y_raw, ssd_states = ssd_fwd(xbc, first, second)
    (o_a,) = rowmap_fwd("gate_a_fwd", _f_gate_silu, [(o_a_raw, 0), (proj, OFF_SB_Z // 1024)], [], 1, width=1024, ncol=1, **kw)
    (o_b,) = rowmap_fwd("gate_b_fwd", _f_head_norm_gate, [(o_b_raw, 0), (proj, OFF_GDN_Z // HEAD_DIM)], [(p["gdn_norm_g"], False)], 1,
                        width=HEAD_DIM, ncol=N_HEADS, **kw)
    (o_c,) = rowmap_fwd("gate_c_fwd", _f_ssm_out, [(y_raw, 0), (xbc, 0), (proj, OFF_SSM_Z // 1024)],
                        [(p["d_skip"], True), (p["ssm_norm_g"], True)], 1, width=1024, ncol=SSM_GROUPS, **kw)
    pa = matmul(o_a, p["wa"], "nn", name="branch_a", bm=lp // 2, bn=512)
    pb = matmul(o_b, p["wb"], "nn", name="branch_b", bm=lp // 2, bn=512)
    pc = matmul(o_c, p["wc"], "nn", name="branch_c", bm=lp // 2, bn=512)
    merge_rows = [(pa, 0), (pb, 0), (pc, 0)] + [(proj, OFF_GATES // 512 + 2 * i) for i in range(3)]
    (merged,) = rowmap_fwd("merge_fwd", _f_merge, merge_rows, [], 1, width=512, ncol=2, **kw)
    h_out = matmul(merged, p["wo"], "nn", name="out_proj", bm=lp, bn=512, residual=h)
    saved = dict(h=h, u=u, proj=proj, csum=csum, qkv=qkv, first=first, second=second, o_a_raw=o_a_raw, o_b_raw=o_b_raw,
                 gdn_states=gdn_states, xbc=xbc, y_raw=y_raw, ssd_states=ssd_states, o_a=o_a, o_b=o_b, o_c=o_c, pa=pa, pb=pb, pc=pc,
                 merged=merged)
    return h_out, saved


def _layer_bwd(d_h, p, s):
    lp = d_h.shape[0]
    bm = _pick(lp, 272, 8)
    kw = dict(bm=bm, lp=lp)
    proj = s["proj"]
    g = {}
    d_merged = matmul(d_h, p["wo"], "nt", name="d_merged", bm=lp, bn=512)
    g["w_out"] = matmul(s["merged"], d_h, "tn", name="g_w_out", bm=512, bn=1024, bk=lp)
    merge_rows = [(s["pa"], 0), (s["pb"], 0), (s["pc"], 0)] + [(proj, OFF_GATES // 512 + 2 * i) for i in range(3)]
    (d_pa, d_pb, d_pc, d_ga, d_gb, d_gc), _ = rowmap_bwd("merge_bwd", _f_merge, merge_rows, [], [d_merged], width=512, ncol=2, **kw)
    g["w_branch_a"] = matmul(s["o_a"], d_pa, "tn", name="g_w_a", bm=512, bn=1024, bk=lp)
    g["w_branch_b"] = matmul(s["o_b"], d_pb, "tn", name="g_w_b", bm=512, bn=1024, bk=lp)
    g["w_branch_c"] = matmul(s["o_c"], d_pc, "tn", name="g_w_c", bm=512, bn=1024, bk=lp)
    d_oa = matmul(d_pa, p["wa"], "nt", name="d_o_a", bm=lp, bn=512)
    d_ob = matmul(d_pb, p["wb"], "nt", name="d_o_b", bm=lp, bn=512)
    d_oc = matmul(d_pc, p["wc"], "nt", name="d_o_c", bm=lp, bn=512)
    (d_oa_raw, d_sbz), _ = rowmap_bwd("gate_a_bwd", _f_gate_silu, [(s["o_a_raw"], 0), (proj, OFF_SB_Z // 1024)], [], [d_oa],
                                      width=1024, ncol=1, **kw)
    (d_ob_raw, d_gdz), (g["gdn_norm_g"],) = rowmap_bwd(
        "gate_b_bwd", _f_head_norm_gate, [(s["o_b_raw"], 0), (proj, OFF_GDN_Z // HEAD_DIM)], [(p["gdn_norm_g"], False)], [d_ob],
        width=HEAD_DIM, ncol=N_HEADS, **kw)
    (d_y, d_xh, d_ssz), (g_dskip, g["ssm_norm_g"]) = rowmap_bwd(
        "gate_c_bwd", _f_ssm_out, [(s["y_raw"], 0), (s["xbc"], 0), (proj, OFF_SSM_Z // 1024)],
        [(p["d_skip"], True), (p["ssm_norm_g"], True)], [d_oc], width=1024, ncol=SSM_GROUPS, **kw)
    g["gdn_norm_g"], g["ssm_norm_g"] = g["gdn_norm_g"][0], g["ssm_norm_g"][0]
    g["ssm_d"] = g_dskip.reshape(SSM_HEADS, SSM_P).sum(axis=1)
    d_q, d_k, d_v = sb_bwd(proj, (0, N_HEADS, 2 * N_HEADS), s["csum"], d_oa_raw)
    d_qkv, d_first_gdn = gdn_bwd(s["qkv"], s["first"], s["gdn_states"], d_ob_raw)
    d_xbc_out, d_first_ssd, d_second = ssd_bwd(s["xbc"], s["first"], s["second"], s["ssd_states"], d_y, d_xh)
    d_gdqkv, g["gdn_conv_w"], _ = conv_bwd("gdn_conv_bwd", proj, OFF_GDN_QKV // HEAD_DIM, p["gdn_conv_w"], p["gdn_conv_b"], 2 * N_HEADS,
                                           d_qkv)
    d_xbc, g["ssm_conv_w"], g_cb = conv_bwd("ssm_conv_bwd", proj, OFF_SSM_XBC // HEAD_DIM, p["ssm_conv_w"], p["ssm_conv_b"], 0, d_xbc_out)
    g["ssm_conv_b"] = g_cb[0]
    d_first = d_first_gdn + d_first_ssd
    (d_small,), (g_bias, g_alog) = rowmap_bwd("gates_bwd", _f_small_gates, [(proj, OFF_SMALL // HEAD_DIM)],
                                              [(p["bias_vec"], False), (p["alog_vec"], False)], [d_first, d_second],
                                              width=HEAD_DIM, ncol=1, **kw)
    g["gdn_dt_bias"], g["ssm_dt_bias"] = g_bias[0, 8:16], g_bias[0, 16:48]
    g["gdn_a_log"], g["ssm_a_log"] = g_alog[0, 8:16], g_alog[0, 16:48]
    d_proj = jnp.concatenate([d_q, d_k, d_v, d_sbz, d_gdqkv, d_gdz, d_ssz, d_xbc, d_ga, d_gb, d_gc, d_small], axis=1)
    g["w_in"] = _ungroup(matmul(s["u"], d_proj, "tn", name="g_w_in", bm=1024, bn=640, bk=lp, out_dtype=BF16))
    d_u = matmul(d_proj, p["wp"], "nt", name="d_u", bm=lp // 2, bn=1024, bk=640)
    (d_hn,), (g_norm,) = rowmap_bwd("rms_bwd", _f_rmsnorm, [(s["h"], 0)], [(p["norm_g"], False)], [d_u], width=D_MODEL, ncol=1, **kw)
    g["norm_g"] = g_norm[0]
    return d_h + d_hn, g


def kernel(x, meta_tokens, norm_g, w_in, gdn_conv_w, gdn_a_log, gdn_dt_bias, gdn_norm_g, ssm_conv_w, ssm_conv_b, ssm_a_log, ssm_dt_bias, ssm_d, ssm_norm_g, w_branch_a, w_branch_b, w_branch_c, w_out, final_norm_g, loss_target, m_meta_tokens, m_norm_g, m_w_in, m_gdn_conv_w, m_gdn_a_log, m_gdn_dt_bias, m_gdn_norm_g, m_ssm_conv_w, m_ssm_conv_b, m_ssm_a_log, m_ssm_dt_bias, m_ssm_d, m_ssm_norm_g, m_w_branch_a, m_w_branch_b, m_w_branch_c, m_w_out, m_final_norm_g, v_meta_tokens, v_norm_g, v_w_in, v_gdn_conv_w, v_gdn_a_log, v_gdn_dt_bias, v_gdn_norm_g, v_ssm_conv_w, v_ssm_conv_b, v_ssm_a_log, v_ssm_dt_bias, v_ssm_d, v_ssm_norm_g, v_w_branch_a, v_w_branch_b, v_w_branch_c, v_w_out, v_final_norm_g):
    w = dict(meta_tokens=meta_tokens, norm_g=norm_g, w_in=w_in, gdn_conv_w=gdn_conv_w, gdn_a_log=gdn_a_log, gdn_dt_bias=gdn_dt_bias,
             gdn_norm_g=gdn_norm_g, ssm_conv_w=ssm_conv_w, ssm_conv_b=ssm_conv_b, ssm_a_log=ssm_a_log, ssm_dt_bias=ssm_dt_bias,
             ssm_d=ssm_d, ssm_norm_g=ssm_norm_g, w_branch_a=w_branch_a, w_branch_b=w_branch_b, w_branch_c=w_branch_c, w_out=w_out,
             final_norm_g=final_norm_g)
    m = dict(meta_tokens=m_meta_tokens, norm_g=m_norm_g, w_in=m_w_in, gdn_conv_w=m_gdn_conv_w, gdn_a_log=m_gdn_a_log,
             gdn_dt_bias=m_gdn_dt_bias, gdn_norm_g=m_gdn_norm_g, ssm_conv_w=m_ssm_conv_w, ssm_conv_b=m_ssm_conv_b,
             ssm_a_log=m_ssm_a_log, ssm_dt_bias=m_ssm_dt_bias, ssm_d=m_ssm_d, ssm_norm_g=m_ssm_norm_g, w_branch_a=m_w_branch_a,
             w_branch_b=m_w_branch_b, w_branch_c=m_w_branch_c, w_out=m_w_out, final_norm_g=m_final_norm_g)
    v = dict(meta_tokens=v_meta_tokens, norm_g=v_norm_g, w_in=v_w_in, gdn_conv_w=v_gdn_conv_w, gdn_a_log=v_gdn_a_log,
             gdn_dt_bias=v_gdn_dt_bias, gdn_norm_g=v_gdn_norm_g, ssm_conv_w=v_ssm_conv_w, ssm_conv_b=v_ssm_conv_b,
             ssm_a_log=v_ssm_a_log, ssm_dt_bias=v_ssm_dt_bias, ssm_d=v_ssm_d, ssm_norm_g=v_ssm_norm_g, w_branch_a=v_w_branch_a,
             w_branch_b=v_w_branch_b, w_branch_c=v_w_branch_c, w_out=v_w_out, final_norm_g=v_final_norm_g)
    depth = norm_g.shape[0]
    full = _gather_weights(w)
    params = [_layer_params(w, full, l) for l in range(depth)]

    h = jnp.concatenate([jnp.zeros((PAD, D_MODEL), F32), full["meta_tokens"], x[0]], axis=0)
    saved = []
    for l in range(depth):
        h, s = _layer_fwd(h, params[l])
        saved.append(s)
    loss, d_h, g_final = loss_head(h, loss_target[0], final_norm_g[None])
    layer_grads = [None] * depth
    for l in reversed(range(depth)):
        d_h, layer_grads[l] = _layer_bwd(d_h, params[l], saved[l])
    grads = {n: jnp.stack([layer_grads[l][n] for l in range(depth)]) for n in WEIGHTS if n not in ("meta_tokens", "final_norm_g")}
    grads["meta_tokens"] = d_h[PAD:PAD + N_META]
    grads["final_norm_g"] = g_final[0]
    grad_x = d_h[PAD + N_META:][None]

    red, small_red = _reduce_grads(grads, w)
    delta, new_m, new_v = {}, {}, {}
    for n in SHARDED:
        two_d = lambda a: a.reshape(-1, a.shape[-1])
        d2, m2, v2 = adamw("adamw_" + n, two_d(w[n]), two_d(red[n]), two_d(m[n]), two_d(v[n]))
        delta[n], new_m[n], new_v[n] = d2.reshape(w[n].shape), m2.reshape(w[n].shape), v2.reshape(w[n].shape)
    pack_small = lambda d: _pack([d[n].reshape(-1) for n in REPLICATED], row_unit=32)
    small = adamw("adamw_small", pack_small(w), small_red, pack_small(m), pack_small(v))
    shapes = [w[n].shape for n in REPLICATED]
    red.update(zip(REPLICATED, _unpack(small_red, shapes)))
    for d, buf in zip((delta, new_m, new_v), small):
        d.update(zip(REPLICATED, _unpack(buf, shapes)))
    total_loss = lax.psum(loss[0, 0], ("x", "y", "c"))
    return (total_loss, grad_x, *[red[n] for n in WEIGHTS], *[delta[n] for n in WEIGHTS], *[new_m[n] for n in WEIGHTS],
            *[new_v[n] for n in WEIGHTS])
```

```python
import functools
import math

import jax
import jax.numpy as jnp
from jax import lax
from jax.experimental import pallas as pl
from jax.experimental.pallas import tpu as pltpu

F32 = jnp.float32
BF16 = jnp.bfloat16

N_META = 16
RMS_EPS = 1e-6
L2_EPS = 1e-6
CONV_K = 4
D_MODEL = 1024
HEAD_DIM = 128
N_HEADS = 8
CHUNK = 64
SB_BLOCK = 128
PAD = SB_BLOCK - N_META
SSM_INNER = 2048
SSM_P = 64
SSM_HEADS = 32
SSM_GROUPS = 2
SSM_HG = SSM_HEADS // SSM_GROUPS
SSM_N = 128
VMEM_LIMIT = 56 * 1024 * 1024

def _dims(mode, ndim):
    lhs, rhs = {"nn": (1, 0), "nt": (1, 1), "tn": (0, 0)}[mode]
    off = ndim - 2
    return (((lhs + off,), (rhs + off,)), (tuple(range(off)), tuple(range(off))))


def _dot(a, b, mode):
    return lax.dot_general(a, b, _dims(mode, a.ndim), preferred_element_type=F32)


def _halves(a):
    hi = a.astype(BF16)
    return hi, (a - hi.astype(F32)).astype(BF16)


def _mm_raw(a, b, mode, kind):
    if kind == "bf16":
        return _dot(a.astype(BF16), b.astype(BF16), mode)
    if kind == "lhs01":
        hi, lo = _halves(b)
        a = a.astype(BF16)
        return _dot(a, hi, mode) + _dot(a, lo, mode)
    if kind == "rhs01":
        hi, lo = _halves(a)
        b = b.astype(BF16)
        return _dot(hi, b, mode) + _dot(lo, b, mode)
    a_hi, a_lo = _halves(a)
    b_hi, b_lo = _halves(b)
    return _dot(a_hi, b_hi, mode) + (_dot(a_hi, b_lo, mode) + _dot(a_lo, b_hi, mode))


@functools.partial(jax.custom_vjp, nondiff_argnums=(2, 3))
def _mm(a, b, mode="nn", kind="bf16"):
    return _mm_raw(a, b, mode, kind)


def _mm_fwd(a, b, mode, kind):
    return _mm_raw(a, b, mode, kind), (a, b)


def _mm_bwd(mode, kind, res, g):
    a, b = res
    if kind == "lhs01":
        return jnp.zeros_like(a), _mm_raw(a, g, {"nn": "tn", "tn": "nn"}[mode], "lhs01")
    if kind == "rhs01":
        return _mm_raw(g, b, {"nn": "nt", "nt": "nn"}[mode], "rhs01"), jnp.zeros_like(b)
    if mode == "nn":
        return _mm_raw(g, b, "nt", kind), _mm_raw(a, g, "tn", kind)
    if mode == "nt":
        return _mm_raw(g, b, "nn", kind), _mm_raw(g, a, "tn", kind)
    return _mm_raw(b, g, "nt", kind), _mm_raw(a, g, "nn", kind)


_mm.defvjp(_mm_fwd, _mm_bwd)


def _iota2(shape, axis):
    return lax.broadcasted_iota(jnp.int32, shape, axis)


def _inv_unit_lower_raw(m):
    size = m.shape[-1]
    eye = (_iota2((size, size), 0) == _iota2((size, size), 1)).astype(F32)
    n = -m
    t = eye + n
    p = n
    steps = int(math.log2(size)) - 1
    for _ in range(steps):
        p = _mm_raw(p, p, "nn", "x3")
        t = t + _mm_raw(t, p, "nn", "x3")
    return t


@jax.custom_vjp
def _inv_unit_lower(m):
    return _inv_unit_lower_raw(m)


def _inv_fwd(m):
    t = _inv_unit_lower_raw(m)
    return t, t


def _inv_bwd(t, g):
    return (-_mm_raw(_mm_raw(t, g, "tn", "x3"), t, "nt", "x3"),)


_inv_unit_lower.defvjp(_inv_fwd, _inv_bwd)


def _safe_decay(col, row, keep):
    return jnp.where(keep, jnp.exp(jnp.where(keep, col - row, 0.0)), 0.0)


def _col_to_row(col):
    n = col.shape[-2]
    eye = _iota2((n, n), 0) == _iota2((n, n), 1)
    return jnp.sum(jnp.where(eye, col, 0.0), axis=-2, keepdims=True)


def _cumsum_col(col):
    n = col.shape[-2]
    li, si = _iota2((n, n), 0), _iota2((n, n), 1)
    row = _col_to_row(col)
    c_col = jnp.sum(jnp.where(li >= si, row, 0.0), axis=-1, keepdims=True)
    c_row = jnp.sum(jnp.where(li <= si, col, 0.0), axis=-2, keepdims=True)
    return c_col, c_row


def _gdn_chunk(q, k, v, g, beta, state):
    cl = q.shape[-2]
    li, si = _iota2((cl, cl), 0), _iota2((cl, cl), 1)
    gc_col, gc_row = _cumsum_col(g)
    g_last = jnp.sum(g, axis=-2, keepdims=True)
    dec_strict = _safe_decay(gc_col, gc_row, li > si)
    dec_incl = _safe_decay(gc_col, gc_row, li >= si)
    e_gc = jnp.exp(gc_col)
    qs = q * (HEAD_DIM ** -0.5)
    kb = k * beta
    m = _mm(kb, k, "nt") * dec_strict
    t_inv = _inv_unit_lower(m)
    u = _mm(t_inv, v * beta)
    w = _mm(t_inv, kb * e_gc)
    a_qk = _mm(qs, k, "nt") * dec_incl
    q_dec = qs * e_gc
    k_end = k * jnp.exp(g_last - gc_col)
    v_new = u - _mm(w, state)
    o = _mm(q_dec, state) + _mm(a_qk, v_new)
    new_state = state * jnp.exp(g_last) + _mm(k_end, v_new, "tn")
    return o, new_state


def _gdn_operands(qkv_ref, gt):
    nh, width = N_HEADS, N_HEADS * HEAD_DIM
    heads = lambda off: jnp.stack([qkv_ref[:, off + h * HEAD_DIM:off + (h + 1) * HEAD_DIM] for h in range(nh)])
    cols = lambda off: jnp.stack([gt[:, off + h:off + h + 1] for h in range(nh)])
    return heads(0), heads(width), heads(2 * width), cols(nh), cols(0)


def gdn_fwd(qkv, gates):
    lp = qkv.shape[0]
    nh = N_HEADS
    nc = lp // CHUNK
    width = nh * HEAD_DIM

    def body(qkv_ref, gt_ref, o_ref, s_ref, state):
        @pl.when(pl.program_id(0) == 0)
        def _():
            state[...] = jnp.zeros_like(state)

        s_in = state[...]
        s_ref[0] = s_in
        o, s_new = _gdn_chunk(*_gdn_operands(qkv_ref, gt_ref[...]), s_in)
        for h in range(nh):
            o_ref[:, h * HEAD_DIM:(h + 1) * HEAD_DIM] = o[h]
        state[...] = s_new

    return pl.pallas_call(
        body, name="gdn_fwd", grid=(nc,),
        in_specs=[pl.BlockSpec((CHUNK, 3 * width), lambda c: (c, 0)), pl.BlockSpec((CHUNK, HEAD_DIM), lambda c: (c, 0))],
        out_specs=[pl.BlockSpec((CHUNK, width), lambda c: (c, 0)), pl.BlockSpec((1, nh, HEAD_DIM, HEAD_DIM), lambda c: (c, 0, 0, 0))],
        out_shape=[jax.ShapeDtypeStruct((lp, width), F32), jax.ShapeDtypeStruct((nc, nh, HEAD_DIM, HEAD_DIM), F32)],
        scratch_shapes=[pltpu.VMEM((nh, HEAD_DIM, HEAD_DIM), F32)],
        compiler_params=pltpu.CompilerParams(dimension_semantics=("arbitrary",), vmem_limit_bytes=VMEM_LIMIT),
    )(qkv, gates)


def gdn_bwd(qkv, gates, states, d_o):
    lp = qkv.shape[0]
    nh = N_HEADS
    nc = lp // CHUNK
    width = nh * HEAD_DIM

    def body(qkv_ref, gt_ref, s_ref, do_ref, dqkv_ref, dgt_ref, d_state):
        @pl.when(pl.program_id(0) == 0)
        def _():
            d_state[...] = jnp.zeros_like(d_state)

        _, pull = jax.vjp(_gdn_chunk, *_gdn_operands(qkv_ref, gt_ref[...]), s_ref[0])
        d_o = jnp.stack([do_ref[:, h * HEAD_DIM:(h + 1) * HEAD_DIM] for h in range(nh)])
        dq, dk, dv, dg, db, ds = pull((d_o, d_state[...]))
        lane = _iota2((CHUNK, HEAD_DIM), 1)
        d_gt = jnp.zeros((CHUNK, HEAD_DIM), F32)
        for h in range(nh):
            for part, val in enumerate((dq, dk, dv)):
                dqkv_ref[:, part * width + h * HEAD_DIM:part * width + (h + 1) * HEAD_DIM] = val[h]
            d_gt = d_gt + jnp.where(lane == h, db[h], 0.0) + jnp.where(lane == nh + h, dg[h], 0.0)
        dgt_ref[...] = d_gt
        d_state[...] = ds

    rev = lambda c: (nc - 1 - c, 0)
    return pl.pallas_call(
        body, name="gdn_bwd", grid=(nc,),
        in_specs=[pl.BlockSpec((CHUNK, 3 * width), rev), pl.BlockSpec((CHUNK, HEAD_DIM), rev),
                  pl.BlockSpec((1, nh, HEAD_DIM, HEAD_DIM), lambda c: (nc - 1 - c, 0, 0, 0)), pl.BlockSpec((CHUNK, width), rev)],
        out_specs=[pl.BlockSpec((CHUNK, 3 * width), rev), pl.BlockSpec((CHUNK, HEAD_DIM), rev)],
        out_shape=[jax.ShapeDtypeStruct((lp, 3 * width), F32), jax.ShapeDtypeStruct((lp, HEAD_DIM), F32)],
        scratch_shapes=[pltpu.VMEM((nh, HEAD_DIM, HEAD_DIM), F32)],
        compiler_params=pltpu.CompilerParams(dimension_semantics=("arbitrary",), vmem_limit_bytes=VMEM_LIMIT),
    )(qkv, gates, states, d_o)


def _head_expand():
    width = SSM_HG * SSM_P
    return (_iota2((SSM_HG, width), 1) // SSM_P == _iota2((SSM_HG, width), 0)).astype(F32)


def _ssd_chunk(x, b, c, dt, la, state):
    cl = x.shape[0]
    li, si = _iota2((cl, cl), 0), _iota2((cl, cl), 1)
    causal = li >= si
    expand = _head_expand()
    tri = causal.astype(F32)
    xs = x * _mm(dt, expand, "nn", "rhs01")
    la_x = _mm(la, expand, "nn", "rhs01")
    cs_x = _mm(tri, la_x, "nn", "lhs01")
    last_x = jnp.sum(la_x, axis=0, keepdims=True)
    cs = _mm(tri, la, "nn", "lhs01")
    scores = _mm(c, b, "nt")
    head_id = _iota2((1, SSM_HG), 1)
    per_tile = HEAD_DIM // SSM_P
    tile_head = _iota2((1, HEAD_DIM), 1) // SSM_P
    within = []
    for t in range(SSM_HG // per_tile):
        xs_t = xs[:, t * HEAD_DIM:(t + 1) * HEAD_DIM]
        acc = jnp.zeros((cl, HEAD_DIM), F32)
        for hh in range(per_tile):
            cs_col = jnp.sum(jnp.where(head_id == t * per_tile + hh, cs, 0.0), axis=1, keepdims=True)
            decay = _safe_decay(cs_col, _col_to_row(cs_col), causal)
            acc = acc + _mm(scores * decay, jnp.where(tile_head == hh, xs_t, 0.0))
        within.append(acc)
    y = _mm(c, state) * jnp.exp(cs_x) + jnp.concatenate(within, axis=1)
    new_state = state * jnp.exp(last_x) + _mm(b, xs * jnp.exp(last_x - cs_x), "tn")
    return y, new_state


GATE_DT = 16


def _place_lanes(v, lo):
    n = v.shape[1]
    sel = (_iota2((n, HEAD_DIM), 1) == _iota2((n, HEAD_DIM), 0) + lo).astype(F32)
    return _mm_raw(v, sel, "nn", "rhs01")


def ssd_fwd(xbc, first, second):
    lp = xbc.shape[0]
    nc = lp // CHUNK
    width = SSM_HG * SSM_P
    b_off, c_off = SSM_INNER, SSM_INNER + SSM_GROUPS * SSM_N

    def body(x_ref, f_ref, s2_ref, y_ref, s_ref, state):
        @pl.when(pl.program_id(0) == 0)
        def _():
            state[...] = jnp.zeros_like(state)

        f, s2 = f_ref[...], s2_ref[...]
        for g in range(SSM_GROUPS):
            lo = GATE_DT + g * SSM_HG
            s_in = state[g]
            s_ref[0, g] = s_in
            y, s_new = _ssd_chunk(x_ref[:, g * width:(g + 1) * width], x_ref[:, b_off + g * SSM_N:b_off + (g + 1) * SSM_N],
                                  x_ref[:, c_off + g * SSM_N:c_off + (g + 1) * SSM_N], f[:, lo:lo + SSM_HG], s2[:, lo:lo + SSM_HG], s_in)
            y_ref[:, g * width:(g + 1) * width] = y
            state[g] = s_new

    row = lambda cols: pl.BlockSpec((CHUNK, cols), lambda k: (k, 0))
    return pl.pallas_call(
        body, name="ssd_fwd", grid=(nc,),
        in_specs=[row(xbc.shape[1]), row(HEAD_DIM), row(HEAD_DIM)],
        out_specs=[row(SSM_INNER), pl.BlockSpec((1, SSM_GROUPS, SSM_N, width), lambda k: (k, 0, 0, 0))],
        out_shape=[jax.ShapeDtypeStruct((lp, SSM_INNER), F32), jax.ShapeDtypeStruct((nc, SSM_GROUPS, SSM_N, width), F32)],
        scratch_shapes=[pltpu.VMEM((SSM_GROUPS, SSM_N, width), F32)],
        compiler_params=pltpu.CompilerParams(dimension_semantics=("arbitrary",), vmem_limit_bytes=VMEM_LIMIT),
    )(xbc, first, second)


def ssd_bwd(xbc, first, second, states, d_y, d_xh):
    lp = xbc.shape[0]
    nc = lp // CHUNK
    width = SSM_HG * SSM_P
    b_off, c_off = SSM_INNER, SSM_INNER + SSM_GROUPS * SSM_N

    def body(x_ref, f_ref, s2_ref, s_ref, dy_ref, dxh_ref, dx_ref, df_ref, ds2_ref, d_state):
        @pl.when(pl.program_id(0) == 0)
        def _():
            d_state[...] = jnp.zeros_like(d_state)

        f, s2 = f_ref[...], s2_ref[...]
        d_f = jnp.zeros((CHUNK, HEAD_DIM), F32)
        d_s2 = jnp.zeros((CHUNK, HEAD_DIM), F32)
        for g in range(SSM_GROUPS):
            lo = GATE_DT + g * SSM_HG
            x_l = slice(g * width, (g + 1) * width)
            b_l = slice(b_off + g * SSM_N, b_off + (g + 1) * SSM_N)
            c_l = slice(c_off + g * SSM_N, c_off + (g + 1) * SSM_N)
            _, pull = jax.vjp(_ssd_chunk, x_ref[:, x_l], x_ref[:, b_l], x_ref[:, c_l], f[:, lo:lo + SSM_HG], s2[:, lo:lo + SSM_HG],
                              s_ref[0, g])
            dx, db, dc, ddt, dla, ds = pull((dy_ref[:, x_l], d_state[g]))
            dx_ref[:, x_l] = dx + dxh_ref[:, x_l]
            dx_ref[:, b_l] = db
            dx_ref[:, c_l] = dc
            d_f = d_f + _place_lanes(ddt, lo)
            d_s2 = d_s2 + _place_lanes(dla, lo)
            d_state[g] = ds
        df_ref[...] = d_f
        ds2_ref[...] = d_s2

    row = lambda cols: pl.BlockSpec((CHUNK, cols), lambda k: (nc - 1 - k, 0))
    gate_shape = jax.ShapeDtypeStruct((lp, HEAD_DIM), F32)
    return pl.pallas_call(
        body, name="ssd_bwd", grid=(nc,),
        in_specs=[row(xbc.shape[1]), row(HEAD_DIM), row(HEAD_DIM),
                  pl.BlockSpec((1, SSM_GROUPS, SSM_N, width), lambda k: (nc - 1 - k, 0, 0, 0)), row(SSM_INNER), row(SSM_INNER)],
        out_specs=[row(xbc.shape[1]), row(HEAD_DIM), row(HEAD_DIM)],
        out_shape=[jax.ShapeDtypeStruct(xbc.shape, F32), gate_shape, gate_shape],
        scratch_shapes=[pltpu.VMEM((SSM_GROUPS, SSM_N, width), F32)],
        compiler_params=pltpu.CompilerParams(dimension_semantics=("arbitrary",), vmem_limit_bytes=VMEM_LIMIT),
    )(xbc, first, second, states, d_y, d_xh)


SB_QROWS = 544


def _mm_tri(a, tri):
    return _mm_raw(a, tri.astype(BF16), "nn", "rhs01")


def _sb_scores(q_scaled, kb, row0, j):
    shape = (q_scaled.shape[0], SB_BLOCK)
    z = _mm_raw(q_scaled, kb, "nt", "bf16")
    q_pos = row0 + _iota2(shape, 0)
    k_pos = j * SB_BLOCK + _iota2(shape, 1)
    valid = (k_pos < q_pos) & (k_pos >= PAD)
    sp = jnp.maximum(z, 0.0) + jnp.log(1.0 + jnp.exp(-jnp.abs(z)))
    lk = jnp.where(valid, -sp, 0.0)
    return z, sp, valid, lk


def sb_fwd(src, offs, ride=None):
    lp = src.shape[0]
    nh = N_HEADS
    qb = _pick(lp, SB_QROWS, 8)
    scale = HEAD_DIM ** -0.5
    blk = SB_BLOCK

    n_in = len(ride.ins) if ride else 0
    n_out = len(ride.out_shapes) if ride else 0

    def body(*refs):
        q_ref, k_ref, v_ref = refs[:3]
        o_ref, c_ref = refs[3 + n_in:5 + n_in]
        ride_refs = (refs[3:3 + n_in], refs[5 + n_in:5 + n_in + n_out], refs[5 + n_in + n_out:])
        i = pl.program_id(1)
        if ride:
            @pl.when((pl.program_id(0) == 0) & (i == 0))
            def _():
                ride.start(*ride_refs)
        q_scaled = q_ref[...] * scale
        upper = _iota2((blk, blk), 0) > _iota2((blk, blk), 1)
        n_blocks = ((i + 1) * qb + blk - 1) // blk

        def step(it, carry):
            acc, c = carry
            j = n_blocks - 1 - it
            rows = pl.ds(pl.multiple_of(j * blk, blk), blk)
            z, sp, valid, lk = _sb_scores(q_scaled, k_ref[rows, :], i * qb, j)
            later = _mm_tri(lk, upper) + c
            w = jnp.where(valid, jnp.exp(z - sp + later), 0.0)
            acc = acc + _mm_raw(w, v_ref[rows, :], "nn", "bf16")
            return acc, c + jnp.sum(lk, axis=1, keepdims=True)

        acc, c = lax.fori_loop(0, n_blocks, step, (jnp.zeros((qb, HEAD_DIM), F32), jnp.zeros((qb, 1), F32)))
        o_ref[...] = acc
        c_ref[0] = c
        if ride:
            @pl.when((pl.program_id(0) == nh - 1) & (i == lp // qb - 1))
            def _():
                ride.finish(*ride_refs)

    qspec = pl.BlockSpec((qb, HEAD_DIM), lambda h, i: (i, offs[0] + h))
    kspec = pl.BlockSpec((lp, HEAD_DIM), lambda h, i: (0, offs[1] + h))
    vspec = pl.BlockSpec((lp, HEAD_DIM), lambda h, i: (0, offs[2] + h))
    ospec = pl.BlockSpec((qb, HEAD_DIM), lambda h, i: (i, h))
    cspec = pl.BlockSpec((1, qb, 1), lambda h, i: (h, i, 0))
    return pl.pallas_call(
        body, name="sb_fwd", grid=(nh, lp // qb), in_specs=[qspec, kspec, vspec] + [ANY] * n_in, out_specs=[ospec, cspec] + [ANY] * n_out,
        out_shape=[jax.ShapeDtypeStruct((lp, nh * HEAD_DIM), F32), jax.ShapeDtypeStruct((nh, lp, 1), F32)]
        + (ride.out_shapes if ride else []),
        scratch_shapes=ride.sems if ride else [],
        input_output_aliases={3 + k: 2 + k for k in range(n_in)} if ride and ride.alias else {},
        compiler_params=pltpu.CompilerParams(dimension_semantics=("arbitrary", "arbitrary"), vmem_limit_bytes=VMEM_LIMIT),
    )(src, src, src, *(ride.ins if ride else []))


def sb_bwd(src, offs, csum, d_o, ride=None):
    lp = src.shape[0]
    nh = N_HEADS
    qb = _pick(lp, SB_QROWS, 8)
    scale = HEAD_DIM ** -0.5
    blk = SB_BLOCK

    n_in = len(ride.ins) if ride else 0
    n_out = len(ride.out_shapes) if ride else 0

    def body(*refs):
        q_ref, k_ref, v_ref, c_ref, do_ref = refs[:5]
        dq_ref, dk_ref, dv_ref = refs[5 + n_in:8 + n_in]
        ride_refs = (refs[5:5 + n_in], refs[8 + n_in:8 + n_in + n_out], refs[8 + n_in + n_out:])
        i = pl.program_id(1)
        if ride:
            @pl.when((pl.program_id(0) == 0) & (i == 0))
            def _():
                ride.start(*ride_refs)

        @pl.when(i == 0)
        def _():
            dk_ref[...] = jnp.zeros_like(dk_ref)
            dv_ref[...] = jnp.zeros_like(dv_ref)

        q_scaled = q_ref[...] * scale
        d_out = do_ref[...]
        total = c_ref[0]
        lower_incl = _iota2((blk, blk), 0) <= _iota2((blk, blk), 1)
        lower = _iota2((blk, blk), 0) < _iota2((blk, blk), 1)
        n_blocks = ((i + 1) * qb + blk - 1) // blk

        def step(j, carry):
            acc, cp, ep = carry
            rows = pl.ds(pl.multiple_of(j * blk, blk), blk)
            kb = k_ref[rows, :]
            vb = v_ref[rows, :]
            z, sp, valid, lk = _sb_scores(q_scaled, kb, i * qb, j)
            later = total - cp - _mm_tri(lk, lower_incl)
            w = jnp.where(valid, jnp.exp(z - sp + later), 0.0)
            e = w * _mm_raw(d_out, vb, "nt", "bf16")
            before = ep + _mm_tri(e, lower)
            dz = jnp.where(valid, e * jnp.exp(-sp) - before * jnp.exp(z - sp), 0.0)
            dk_ref[rows, :] += _mm_raw(dz, q_scaled, "tn", "bf16")
            dv_ref[rows, :] += _mm_raw(w, d_out, "tn", "bf16")
            acc = acc + _mm_raw(dz, kb, "nn", "bf16")
            return acc, cp + jnp.sum(lk, axis=1, keepdims=True), ep + jnp.sum(e, axis=1, keepdims=True)

        zero_col = jnp.zeros((qb, 1), F32)
        acc, _, _ = lax.fori_loop(0, n_blocks, step, (jnp.zeros((qb, HEAD_DIM), F32), zero_col, zero_col))
        dq_ref[...] = acc * scale
        if ride:
            @pl.when((pl.program_id(0) == nh - 1) & (i == lp // qb - 1))
            def _():
                ride.finish(*ride_refs)

    qspec = pl.BlockSpec((qb, HEAD_DIM), lambda h, i: (i, offs[0] + h))
    kspec = pl.BlockSpec((lp, HEAD_DIM), lambda h, i: (0, offs[1] + h))
    vspec = pl.BlockSpec((lp, HEAD_DIM), lambda h, i: (0, offs[2] + h))
    ospec = pl.BlockSpec((qb, HEAD_DIM), lambda h, i: (i, h))
    fullspec = pl.BlockSpec((lp, HEAD_DIM), lambda h, i: (0, h))
    cspec = pl.BlockSpec((1, qb, 1), lambda h, i: (h, i, 0))
    return pl.pallas_call(
        body, name="sb_bwd", grid=(nh, lp // qb), in_specs=[qspec, kspec, vspec, cspec, ospec] + [ANY] * n_in,
        out_specs=[ospec, fullspec, fullspec] + [ANY] * n_out,
        out_shape=[jax.ShapeDtypeStruct((lp, nh * HEAD_DIM), F32)] * 3 + (ride.out_shapes if ride else []),
        scratch_shapes=ride.sems if ride else [],
        input_output_aliases={5 + k: 3 + k for k in range(n_in)} if ride and ride.alias else {},
        compiler_params=pltpu.CompilerParams(dimension_semantics=("arbitrary", "arbitrary"), vmem_limit_bytes=VMEM_LIMIT),
    )(src, src, src, csum, d_o, *(ride.ins if ride else []))


def _pick(n, target, unit):
    if n <= target:
        return n
    best = None
    for d in range(unit, target + 1, unit):
        if n % d == 0:
            best = d
    assert best is not None, (n, target, unit)
    return best


def matmul(a, b, mode="nn", *, name, bm=1088, bn=640, bk=2176, residual=None, out_dtype=F32, b_koff=0):
    if mode == "nn":
        (m, k), n = a.shape, b.shape[1]
    elif mode == "nt":
        (m, k), n = a.shape, b.shape[0]
    else:
        (k, m), n = a.shape, b.shape[1]
    assert b_koff == 0 or mode == "nt"
    bm = _pick(m, bm, 128 if mode == "tn" else 8)
    bn = _pick(n, bn, 128 if mode != "nt" else 8)
    bk = _pick(k, bk, 128 if mode != "tn" else 8)
    nk = k // bk

    def body(*refs):
        if residual is None:
            a_ref, b_ref, o_ref, acc = refs
            r_ref = None
        else:
            a_ref, b_ref, r_ref, o_ref, acc = refs
        kk = pl.program_id(2)
        part = _mm_raw(a_ref[...], b_ref[...], mode, "bf16")

        @pl.when(kk == 0)
        def _():
            acc[...] = part

        @pl.when(kk > 0)
        def _():
            acc[...] += part

        @pl.when(kk == nk - 1)
        def _():
            res = acc[...]
            if r_ref is not None:
                res = res + r_ref[...]
            o_ref[...] = res.astype(out_dtype)

    a_spec = pl.BlockSpec((bk, bm), lambda i, j, kk: (kk, i)) if mode == "tn" else pl.BlockSpec((bm, bk), lambda i, j, kk: (i, kk))
    b_spec = pl.BlockSpec((bn, bk), lambda i, j, kk: (j, b_koff + kk)) if mode == "nt" else pl.BlockSpec((bk, bn), lambda i, j, kk: (kk, j))
    o_spec = pl.BlockSpec((bm, bn), lambda i, j, kk: (i, j))
    ins, specs = [a, b], [a_spec, b_spec]
    if residual is not None:
        ins.append(residual)
        specs.append(o_spec)
    return pl.pallas_call(
        body, name=name, grid=(m // bm, n // bn, nk), in_specs=specs, out_specs=o_spec,
        out_shape=jax.ShapeDtypeStruct((m, n), out_dtype),
        scratch_shapes=[pltpu.VMEM((bm, bn), F32)],
        compiler_params=pltpu.CompilerParams(dimension_semantics=("arbitrary", "arbitrary", "arbitrary"), vmem_limit_bytes=VMEM_LIMIT),
    )(*ins)


def _row_specs(rows, params, width, bm):
    row_specs = [pl.BlockSpec((bm, width), (lambda j, i, off=off: (i, off + j))) for _, off in rows]
    par_specs = [pl.BlockSpec((p.shape[0], width) if per_col else p.shape, ((lambda j, i: (0, j)) if per_col else (lambda j, i: (0, 0))))
                 for p, per_col in params]
    return row_specs, par_specs


def rowmap_fwd(name, fn, rows, params, n_out, *, width, ncol, bm, lp):
    row_specs, par_specs = _row_specs(rows, params, width, bm)
    nr = len(rows)

    def body(*refs):
        ins, outs = refs[:nr + len(params)], refs[nr + len(params):]
        row_ids = pl.program_id(1) * bm + _iota2((bm, 1), 0)
        res = fn(row_ids, *[r[...] for r in ins])
        for o_ref, val in zip(outs, res):
            o_ref[...] = val

    o_spec = pl.BlockSpec((bm, width), lambda j, i: (i, j))
    return pl.pallas_call(
        body, name=name, grid=(ncol, lp // bm), in_specs=row_specs + par_specs, out_specs=[o_spec] * n_out,
        out_shape=[jax.ShapeDtypeStruct((lp, ncol * width), F32)] * n_out,
        compiler_params=pltpu.CompilerParams(dimension_semantics=("arbitrary", "arbitrary"), vmem_limit_bytes=VMEM_LIMIT),
    )(*[a for a, _ in rows], *[p for p, _ in params])


def rowmap_bwd(name, fn, rows, params, d_outs, *, width, ncol, bm, lp):
    row_specs, par_specs = _row_specs(rows, params, width, bm)
    nr, npar, nout = len(rows), len(params), len(d_outs)

    def body(*refs):
        ins = refs[:nr + npar]
        dos = refs[nr + npar:nr + npar + nout]
        d_rows = refs[nr + npar + nout:nr + npar + nout + nr]
        d_pars = refs[nr + npar + nout + nr:]
        j, i = pl.program_id(0), pl.program_id(1)
        row_ids = i * bm + _iota2((bm, 1), 0)
        _, pull = jax.vjp(lambda *xs: tuple(fn(row_ids, *xs)), *[r[...] for r in ins])
        grads = pull(tuple(d[...] for d in dos))
        for ref, val in zip(d_rows, grads[:nr]):
            ref[...] = val
        for ref, val, (_, per_col) in zip(d_pars, grads[nr:], params):
            first = (i == 0) if per_col else ((i == 0) & (j == 0))

            @pl.when(first)
            def _(ref=ref, val=val):
                ref[...] = val

            @pl.when(jnp.logical_not(first))
            def _(ref=ref, val=val):
                ref[...] += val

    o_spec = pl.BlockSpec((bm, width), lambda j, i: (i, j))
    res = pl.pallas_call(
        body, name=name, grid=(ncol, lp // bm), in_specs=row_specs + par_specs + [o_spec] * nout,
        out_specs=[o_spec] * nr + par_specs,
        out_shape=[jax.ShapeDtypeStruct((lp, ncol * width), F32)] * nr + [jax.ShapeDtypeStruct(p.shape, F32) for p, _ in params],
        compiler_params=pltpu.CompilerParams(dimension_semantics=("arbitrary", "arbitrary"), vmem_limit_bytes=VMEM_LIMIT),
    )(*[a for a, _ in rows], *[p for p, _ in params], *d_outs)
    return res[:nr], res[nr:]


def _silu(x):
    return x * jax.nn.sigmoid(x)


def _softplus(x):
    return jnp.maximum(x, 0.0) + jnp.log(1.0 + jnp.exp(-jnp.abs(x)))


def _real_rows(row_ids):
    return (row_ids >= PAD).astype(F32)


def _f_rmsnorm(row_ids, h, g):
    return (h * lax.rsqrt(jnp.mean(h * h, axis=-1, keepdims=True) + RMS_EPS) * g,)


def _f_small_gates(row_ids, small, bias, a_log):
    lane = _iota2(small.shape, 1)
    t = small + bias
    sp = _softplus(t)
    coef = -jnp.exp(a_log)
    keep = _real_rows(row_ids)
    first = jnp.where(lane < 8, jax.nn.sigmoid(t), jnp.where(lane < 16, coef * sp, jnp.where(lane < 48, sp, 0.0)))
    second = jnp.where((lane >= 8) & (lane < 48), coef * sp, 0.0)
    return first * keep, second * keep


def _f_gate_silu(row_ids, o, z):
    return (o * _silu(z),)


def _f_head_norm_gate(row_ids, o, z, g):
    out = []
    for h in range(o.shape[1] // HEAD_DIM):
        oh = o[:, h * HEAD_DIM:(h + 1) * HEAD_DIM]
        out.append(oh * lax.rsqrt(jnp.mean(oh * oh, axis=-1, keepdims=True) + RMS_EPS) * g)
    return (jnp.concatenate(out, axis=1) * _silu(z),)


def _f_ssm_out(row_ids, y, xh, z, d_skip, g):
    t = (y + d_skip * xh) * _silu(z)
    return (t * lax.rsqrt(jnp.mean(t * t, axis=-1, keepdims=True) + RMS_EPS) * g,)


def _f_merge(row_ids, pa, pb, pc, ga, gb, gc):
    return (jax.nn.sigmoid(ga) * pa + jax.nn.sigmoid(gb) * pb + jax.nn.sigmoid(gc) * pc,)


def _shift_rows(x, s):
    s = s % x.shape[0]
    return x if s == 0 else pltpu.roll(x, s, 0)


def _conv_pre(x, w, b):
    pre = b
    for kk in range(CONV_K):
        pre = pre + w[kk:kk + 1, :] * _shift_rows(x, CONV_K - 1 - kk)
    return pre


def _conv_post(pre, l2_flag, keep):
    act = _silu(pre)
    nrm = act * lax.rsqrt(jnp.sum(act * act, axis=-1, keepdims=True) + L2_EPS)
    return (l2_flag * nrm + (1.0 - l2_flag) * act) * keep


def conv_fwd(name, src, col_off, w, b, n_l2):
    lp, ch = src.shape[0], w.shape[1]

    def body(x_ref, w_ref, b_ref, o_ref):
        l2_flag = (pl.program_id(0) < n_l2).astype(F32)
        keep = _real_rows(_iota2((lp, 1), 0))
        o_ref[...] = _conv_post(_conv_pre(x_ref[...], w_ref[...], b_ref[...]), l2_flag, keep)

    return pl.pallas_call(
        body, name=name, grid=(ch // HEAD_DIM,),
        in_specs=[pl.BlockSpec((lp, HEAD_DIM), lambda j: (0, col_off + j)), pl.BlockSpec((CONV_K, HEAD_DIM), lambda j: (0, j)),
                  pl.BlockSpec((1, HEAD_DIM), lambda j: (0, j))],
        out_specs=pl.BlockSpec((lp, HEAD_DIM), lambda j: (0, j)),
        out_shape=jax.ShapeDtypeStruct((lp, ch), F32),
        compiler_params=pltpu.CompilerParams(dimension_semantics=("arbitrary",), vmem_limit_bytes=VMEM_LIMIT),
    )(src, w, b)


def conv_bwd(name, src, col_off, w, b, n_l2, d_out):
    lp, ch = src.shape[0], w.shape[1]

    def body(x_ref, w_ref, b_ref, do_ref, dx_ref, dw_ref, db_ref):
        l2_flag = (pl.program_id(0) < n_l2).astype(F32)
        keep = _real_rows(_iota2((lp, 1), 0))
        x, wv = x_ref[...], w_ref[...]
        pre = _conv_pre(x, wv, b_ref[...])
        _, pull = jax.vjp(lambda p: _conv_post(p, l2_flag, keep), pre)
        (d_pre,) = pull(do_ref[...])
        dx = jnp.zeros_like(x)
        for kk in range(CONV_K):
            s = CONV_K - 1 - kk
            dx = dx + wv[kk:kk + 1, :] * _shift_rows(d_pre, -s)
            dw_ref[kk:kk + 1, :] = jnp.sum(d_pre * _shift_rows(x, s), axis=0, keepdims=True)
        dx_ref[...] = dx * keep
        db_ref[...] = jnp.sum(d_pre, axis=0, keepdims=True)

    seq = pl.BlockSpec((lp, HEAD_DIM), lambda j: (0, j))
    wspec = pl.BlockSpec((CONV_K, HEAD_DIM), lambda j: (0, j))
    bspec = pl.BlockSpec((1, HEAD_DIM), lambda j: (0, j))
    return pl.pallas_call(
        body, name=name, grid=(ch // HEAD_DIM,),
        in_specs=[pl.BlockSpec((lp, HEAD_DIM), lambda j: (0, col_off + j)), wspec, bspec, seq],
        out_specs=[seq, wspec, bspec],
        out_shape=[jax.ShapeDtypeStruct((lp, ch), F32), jax.ShapeDtypeStruct(w.shape, F32), jax.ShapeDtypeStruct(b.shape, F32)],
        compiler_params=pltpu.CompilerParams(dimension_semantics=("arbitrary",), vmem_limit_bytes=VMEM_LIMIT),
    )(src, w, b, d_out)


def loss_head(h, target, g):
    lp, d = h.shape
    bm = SB_BLOCK
    first = (PAD + N_META) // bm

    def body(h_ref, t_ref, g_ref, loss_ref, dh_ref, dg_ref):
        i = pl.program_id(0)
        keep = (i >= first).astype(F32)

        def f(hv, gv):
            y = hv * lax.rsqrt(jnp.mean(hv * hv, axis=-1, keepdims=True) + RMS_EPS) * gv
            err = y - t_ref[...]
            return 0.5 * jnp.sum(jnp.mean(err * err, axis=-1, keepdims=True), axis=0, keepdims=True) * keep

        val, pull = jax.vjp(f, h_ref[...], g_ref[...])
        dh, dg = pull(jnp.ones((1, 1), F32))
        dh_ref[...] = dh

        @pl.when(i == 0)
        def _():
            loss_ref[...] = val
            dg_ref[...] = dg

        @pl.when(i > 0)
        def _():
            loss_ref[...] += val
            dg_ref[...] += dg

    row = pl.BlockSpec((bm, d), lambda i: (i, 0))
    return pl.pallas_call(
        body, name="loss_head", grid=(lp // bm,),
        in_specs=[row, pl.BlockSpec((bm, d), lambda i: (jnp.maximum(i - first, 0), 0)), pl.BlockSpec((1, d), lambda i: (0, 0))],
        out_specs=[pl.BlockSpec((1, 1), lambda i: (0, 0)), row, pl.BlockSpec((1, d), lambda i: (0, 0))],
        out_shape=[jax.ShapeDtypeStruct((1, 1), F32), jax.ShapeDtypeStruct((lp, d), F32), jax.ShapeDtypeStruct((1, d), F32)],
        compiler_params=pltpu.CompilerParams(dimension_semantics=("arbitrary",)),
    )(h, target, g)


ADAM_LR, ADAM_B1, ADAM_B2, ADAM_EPS, ADAM_WD, ADAM_STEP = 0.001, 0.9, 0.999, 1e-08, 0.01, 10


def adamw(name, w, g, m, v):
    lead = w.shape[:-2]
    rows, cols = w.shape[-2:]
    br = _pick(rows, 128, 8)

    def body(w_ref, g_ref, m_ref, v_ref, d_ref, nm_ref, nv_ref):
        gv = g_ref[...]
        nm = ADAM_B1 * m_ref[...] + (1.0 - ADAM_B1) * gv
        nv = ADAM_B2 * v_ref[...] + (1.0 - ADAM_B2) * (gv * gv)
        m_hat = nm / (1.0 - ADAM_B1 ** ADAM_STEP)
        v_hat = nv / (1.0 - ADAM_B2 ** ADAM_STEP)
        d_ref[...] = -ADAM_LR * (m_hat / (jnp.sqrt(v_hat) + ADAM_EPS) + ADAM_WD * w_ref[...])
        nm_ref[...] = nm
        nv_ref[...] = nv

    if lead:
        spec = pl.BlockSpec((None, br, cols), lambda s, i: (s, i, 0))
        grid = (lead[0], rows // br)
    else:
        spec = pl.BlockSpec((br, cols), lambda i: (i, 0))
        grid = (rows // br,)
    return pl.pallas_call(
        body, name=name, grid=grid, in_specs=[spec] * 4, out_specs=[spec] * 3,
        out_shape=[jax.ShapeDtypeStruct(w.shape, F32)] * 3,
        compiler_params=pltpu.CompilerParams(dimension_semantics=("arbitrary",) * len(grid), vmem_limit_bytes=VMEM_LIMIT),
    )(w, g, m, v)


MESH = pl.DeviceIdType.MESH
ANY = pl.BlockSpec(memory_space=pl.ANY)
D2D_PIECES = 16
ICI_PIECES = 4


def _place():
    x, y, c = lax.axis_index("x"), lax.axis_index("y"), lax.axis_index("c")
    return x, y, c, [(1 - x, y), (x, 1 - y), (1 - x, 1 - y)]


def _pieces(rows, n, unit):
    per = -(-rows // (n * unit)) * unit
    return [(s, min(per, rows - s)) for s in range(0, rows, per)]


def _row_unit(dtype):
    return 16 if dtype == BF16 else 8


def _scalar(v):
    return jnp.reshape(v, (1,)).astype(jnp.int32)


def place_shard(name, pack):
    rows, cols = pack.shape
    br = _pick(rows, 256, 16)

    def body(m_ref, p_ref, o_ref):
        o_ref[...] = p_ref[...]

    return pl.pallas_call(
        body, name=name,
        grid_spec=pltpu.PrefetchScalarGridSpec(
            num_scalar_prefetch=1, grid=(rows // br,),
            in_specs=[pl.BlockSpec((br, cols), lambda i, m: (i, 0))],
            out_specs=pl.BlockSpec((None, br, cols), lambda i, m: (m[0], i, 0))),
        out_shape=jax.ShapeDtypeStruct((4, rows, cols), pack.dtype),
        compiler_params=pltpu.CompilerParams(dimension_semantics=("arbitrary",), vmem_limit_bytes=VMEM_LIMIT),
    )(_scalar(2 * lax.axis_index("x") + lax.axis_index("y")), pack)


class Ride:
    def __init__(self, ins, out_shapes, alias, sems, start, finish):
        self.ins, self.out_shapes, self.alias, self.sems, self.start, self.finish = list(ins), out_shapes, alias, sems, start, finish


def _gather_parts(o_refs, send_sems, recv_sems):
    x, y, c, chips = _place()
    mine = 2 * x + y

    def half_rows(b, which, start=0, size=None):
        half = o_refs[b].shape[1] // 2
        return pl.ds(pl.multiple_of(which * half + start, _row_unit(o_refs[b].dtype)), half if size is None else size)

    def remote(b, k, slot, rws, to):
        piece = o_refs[b].at[slot, rws, :]
        return pltpu.make_async_remote_copy(src_ref=piece, dst_ref=piece, send_sem=send_sems.at[b, k], recv_sem=recv_sems.at[b, k],
                                            device_id=to, device_id_type=MESH)

    return x, y, c, chips, mine, half_rows, remote


def _gather_start(o_refs, send_sems, recv_sems):
    x, y, c, chips, mine, half_rows, remote = _gather_parts(o_refs, send_sems, recv_sems)
    for b, o_ref in enumerate(o_refs):
        for j, (cx, cy) in enumerate(chips):
            for start, size in _pieces(o_ref.shape[1] // 2, ICI_PIECES, _row_unit(o_ref.dtype)):
                remote(b, j, mine, half_rows(b, c, start, size), (cx, cy, c)).start()


def _gather_finish(o_refs, send_sems, recv_sems):
    x, y, c, chips, mine, half_rows, remote = _gather_parts(o_refs, send_sems, recv_sems)
    sends = []
    for b, o_ref in enumerate(o_refs):
        for j, (cx, cy) in enumerate(chips):
            slot = 2 * cx + cy
            sends.append(remote(b, j, mine, half_rows(b, c), (cx, cy, c)))
            remote(b, j, slot, half_rows(b, c), (cx, cy, c)).wait_recv()
            for start, size in _pieces(o_ref.shape[1] // 2, D2D_PIECES, _row_unit(o_ref.dtype)):
                remote(b, 3 + j, slot, half_rows(b, c, start, size), (x, y, 1 - c)).start()
            sends.append(remote(b, 3 + j, slot, half_rows(b, c), (x, y, 1 - c)))
    for b in range(len(o_refs)):
        for j, (cx, cy) in enumerate(chips):
            remote(b, 3 + j, 2 * cx + cy, half_rows(b, 1 - c), (x, y, 1 - c)).wait_recv()
    for cp in sends:
        cp.wait_send()


def gather_ride(placed):
    n = len(placed)
    return Ride(placed, [jax.ShapeDtypeStruct(p.shape, p.dtype) for p in placed], True,
                [pltpu.SemaphoreType.DMA((n, 6)), pltpu.SemaphoreType.DMA((n, 6))],
                lambda ins, outs, sems: _gather_start(outs, *sems), lambda ins, outs, sems: _gather_finish(outs, *sems))


def gather_shards(name, placed):
    n = len(placed)

    def body(*refs):
        o_refs, sems = refs[n:2 * n], refs[2 * n:]
        _gather_start(o_refs, *sems)
        _gather_finish(o_refs, *sems)

    return pl.pallas_call(
        body, name=name, in_specs=[ANY] * n, out_specs=[ANY] * n,
        out_shape=[jax.ShapeDtypeStruct(p.shape, p.dtype) for p in placed],
        input_output_aliases={i: i for i in range(n)},
        scratch_shapes=[pltpu.SemaphoreType.DMA((n, 6)), pltpu.SemaphoreType.DMA((n, 6))],
    )(*placed)


def pair_split(name, bufs):
    n = len(bufs)

    def body(*refs):
        g_refs, t_refs = refs[:n], refs[n:2 * n]
        send_sems, recv_sems = refs[2 * n:]
        x, y, c, _ = _place()
        waits = []
        for b, (g_ref, t_ref) in enumerate(zip(g_refs, t_refs)):
            half = g_ref.shape[2]
            unit = _row_unit(g_ref.dtype)

            def copy(slots, start, size):
                rws = pl.ds(start, size)
                return pltpu.make_async_remote_copy(src_ref=g_ref.at[slots, 1 - c, rws, :], dst_ref=t_ref.at[slots, rws, :],
                                                    send_sem=send_sems.at[b], recv_sem=recv_sems.at[b], device_id=(x, y, 1 - c),
                                                    device_id_type=MESH)

            for s in range(4):
                for start, size in _pieces(half, D2D_PIECES // 4, unit):
                    copy(s, start, size).start()
            waits.append(copy(slice(None), 0, half))
        for cp in waits:
            cp.wait()

    return pl.pallas_call(
        body, name=name, in_specs=[ANY] * n, out_specs=[ANY] * n,
        out_shape=[jax.ShapeDtypeStruct((4,) + g.shape[2:], g.dtype) for g in bufs],
        scratch_shapes=[pltpu.SemaphoreType.DMA((n,)), pltpu.SemaphoreType.DMA((n,))],
    )(*bufs)


def pair_add(name, g, theirs, transit):
    _, _, half, cols = g.shape
    br = _pick(half, 128, 16)

    def body(c_ref, g_ref, t_ref, o_ref):
        o_ref[...] = (g_ref[...].astype(F32) + t_ref[...].astype(F32)).astype(transit)

    blk = (4, br, cols)
    return pl.pallas_call(
        body, name=name,
        grid_spec=pltpu.PrefetchScalarGridSpec(
            num_scalar_prefetch=1, grid=(half // br,),
            in_specs=[pl.BlockSpec((4, None, br, cols), lambda i, c: (0, c[0], i, 0)), pl.BlockSpec(blk, lambda i, c: (0, i, 0))],
            out_specs=pl.BlockSpec(blk, lambda i, c: (0, i, 0))),
        out_shape=jax.ShapeDtypeStruct((4, half, cols), transit),
        compiler_params=pltpu.CompilerParams(dimension_semantics=("arbitrary",), vmem_limit_bytes=VMEM_LIMIT),
    )(_scalar(lax.axis_index("c")), g, theirs)


def _exchange_copies(a_refs, o_refs, send_sems, recv_sems, start):
    x, y, c, chips = _place()
    mine = 2 * x + y
    waits = []
    for b, (a_ref, o_ref) in enumerate(zip(a_refs, o_refs)):
        rows = a_ref.shape[1]
        for j, (cx, cy) in enumerate(chips):
            def copy(first, size):
                rws = pl.ds(first, size)
                return pltpu.make_async_remote_copy(src_ref=a_ref.at[2 * cx + cy, rws, :], dst_ref=o_ref.at[mine, rws, :],
                                                    send_sem=send_sems.at[b, j], recv_sem=recv_sems.at[b, j],
                                                    device_id=(cx, cy, c), device_id_type=MESH)
            if start:
                for first, size in _pieces(rows, ICI_PIECES, _row_unit(a_ref.dtype)):
                    copy(first, size).start()
            else:
                waits.append(copy(0, rows))
    return waits


def _exchange_finish(a_refs, o_refs, send_sems, recv_sems):
    for cp in _exchange_copies(a_refs, o_refs, send_sems, recv_sems, False):
        cp.wait()


def exchange_ride(parts):
    n = len(parts)
    return Ride(parts, [jax.ShapeDtypeStruct(a.shape, a.dtype) for a in parts], False,
                [pltpu.SemaphoreType.DMA((n, 3)), pltpu.SemaphoreType.DMA((n, 3))],
                lambda ins, outs, sems: _exchange_copies(ins, outs, *sems, True), lambda ins, outs, sems: _exchange_finish(ins, outs, *sems))


def chip_exchange(name, parts):
    n = len(parts)

    def body(*refs):
        a_refs, o_refs, sems = refs[:n], refs[n:2 * n], refs[2 * n:]
        _exchange_copies(a_refs, o_refs, *sems, True)
        _exchange_finish(a_refs, o_refs, *sems)

    return pl.pallas_call(
        body, name=name, in_specs=[ANY] * n, out_specs=[ANY] * n,
        out_shape=[jax.ShapeDtypeStruct(a.shape, a.dtype) for a in parts],
        scratch_shapes=[pltpu.SemaphoreType.DMA((n, 3)), pltpu.SemaphoreType.DMA((n, 3))],
    )(*parts)


def chip_add(name, got, part):
    _, rows, cols = got.shape
    br = _pick(rows, 128, 16)
    nblk = rows // br

    def body(m_ref, c_ref, got_ref, part_ref, o_ref):
        mine = m_ref[0]
        for s in range(4):
            @pl.when(mine == s)
            def _(s=s):
                val = part_ref[...].astype(F32)
                o_ref[...] = val if s == 0 else o_ref[...] + val

            @pl.when(mine != s)
            def _(s=s):
                val = got_ref[s].astype(F32)
                o_ref[...] = val if s == 0 else o_ref[...] + val

    return pl.pallas_call(
        body, name=name,
        grid_spec=pltpu.PrefetchScalarGridSpec(
            num_scalar_prefetch=2, grid=(nblk,),
            in_specs=[pl.BlockSpec((4, br, cols), lambda i, m, c: (0, i, 0)),
                      pl.BlockSpec((None, br, cols), lambda i, m, c: (m[0], i, 0))],
            out_specs=pl.BlockSpec((br, cols), lambda i, m, c: (c[0] * nblk + i, 0))),
        out_shape=jax.ShapeDtypeStruct((2 * rows, cols), F32),
        compiler_params=pltpu.CompilerParams(dimension_semantics=("arbitrary",), vmem_limit_bytes=VMEM_LIMIT),
    )(_scalar(2 * lax.axis_index("x") + lax.axis_index("y")), _scalar(lax.axis_index("c")), got, part)


def pair_join(name, fulls):
    n = len(fulls)

    def body(*refs):
        o_refs = refs[n:2 * n]
        send_sems, recv_sems = refs[2 * n:]
        x, y, c, _ = _place()
        waits = []
        for b, o_ref in enumerate(o_refs):
            half = o_ref.shape[0] // 2
            unit = _row_unit(o_ref.dtype)

            def copy(start, size):
                piece = o_ref.at[pl.ds(pl.multiple_of(c * half + start, unit), size), :]
                return pltpu.make_async_remote_copy(src_ref=piece, dst_ref=piece, send_sem=send_sems.at[b], recv_sem=recv_sems.at[b],
                                                    device_id=(x, y, 1 - c), device_id_type=MESH)

            for start, size in _pieces(half, D2D_PIECES, unit):
                copy(start, size).start()
            waits.append(copy(0, half))
        for cp in waits:
            cp.wait()

    return pl.pallas_call(
        body, name=name, in_specs=[ANY] * n, out_specs=[ANY] * n,
        out_shape=[jax.ShapeDtypeStruct(f.shape, f.dtype) for f in fulls],
        input_output_aliases={i: i for i in range(n)},
        scratch_shapes=[pltpu.SemaphoreType.DMA((n,)), pltpu.SemaphoreType.DMA((n,))],
    )(*fulls)


D_IN = 15920
D_PROJ = 16000
_SEGMENTS = ((0, 8192), (8208, 12816), (12848, 15920), (8192, 8208), (12816, 12848))
OFF_SB_Z, OFF_GDN_QKV, OFF_GDN_Z, OFF_SSM_Z, OFF_SSM_XBC, OFF_GATES, OFF_SMALL = 3072, 4096, 7168, 8192, 10240, 12800, 15872
PACK_C = 1024
WEIGHTS = ("meta_tokens", "norm_g", "w_in", "gdn_conv_w", "gdn_a_log", "gdn_dt_bias", "gdn_norm_g", "ssm_conv_w", "ssm_conv_b",
           "ssm_a_log", "ssm_dt_bias", "ssm_d", "ssm_norm_g", "w_branch_a", "w_branch_b", "w_branch_c", "w_out", "final_norm_g")
SHARDED = ("w_in", "w_branch_a", "w_branch_b", "w_branch_c", "w_out", "gdn_conv_w", "ssm_conv_w", "meta_tokens")
SHARD_AXIS = {"w_in": 2, "w_branch_a": 1, "w_branch_b": 1, "w_branch_c": 1, "w_out": 1, "gdn_conv_w": 2, "ssm_conv_w": 2, "meta_tokens": 1}
BRANCH = ("w_branch_a", "w_branch_b", "w_branch_c", "w_out")
EXACT = ("gdn_conv_w", "ssm_conv_w", "meta_tokens")
REPLICATED = tuple(n for n in WEIGHTS if n not in SHARDED)


def _regrouped_from_shards(shard_cols):
    out = []
    for a, b in _SEGMENTS:
        while a < b:
            chip = a // shard_cols
            stop = min(b, (chip + 1) * shard_cols)
            out.append((chip, a - chip * shard_cols, stop - chip * shard_cols))
            a = stop
    return out


def _shard_from_regrouped(chip, shard_cols):
    lo, hi = chip * shard_cols, (chip + 1) * shard_cols
    out, pos = [], 0
    starts = {}
    for a, b in _SEGMENTS:
        starts[(a, b)] = pos
        pos += b - a
    for a, b in sorted(_SEGMENTS):
        s0, s1 = max(a, lo), min(b, hi)
        if s0 < s1:
            out.append((starts[(a, b)] + s0 - a, starts[(a, b)] + s1 - a))
    return out


def _pack(parts, row_unit=64):
    n = sum(p.shape[0] for p in parts)
    rows = -(-n // (PACK_C * row_unit)) * row_unit
    flat = jnp.concatenate(list(parts) + [jnp.zeros((rows * PACK_C - n,), parts[0].dtype)])
    return flat.reshape(rows, PACK_C)


def _unpack(buf, shapes):
    flat, out, pos = buf.reshape(-1), [], 0
    for shp in shapes:
        n = math.prod(shp)
        out.append(flat[pos:pos + n].reshape(shp))
        pos += n
    return out


def _as_bf16_words(a):
    return lax.bitcast_convert_type(a, BF16).reshape(-1)


BRANCH_ROWS = (D_MODEL // 4, D_MODEL // 4, SSM_INNER // 4, D_MODEL // 4)


def _place_weights(w):
    depth = w["w_in"].shape[0]
    layers = []
    for l in range(depth):
        a = w["w_in"][l].astype(BF16)
        b = jnp.concatenate([w[n][l] for n in BRANCH], axis=0).astype(BF16)
        layers.append([place_shard("place_w_in", a), place_shard("place_branch", b)])
    small = place_shard("place_exact", _pack([_as_bf16_words(w[n]) for n in EXACT]))
    return layers, small


def _exact_weights(w, got_s):
    per_chip = [_unpack(got_s[c], [w[n].shape + (2,) for n in EXACT]) for c in range(4)]
    return {n: jnp.concatenate([lax.bitcast_convert_type(per_chip[c][i], F32) for c in range(4)], axis=SHARD_AXIS[n])
            for i, n in enumerate(EXACT)}


def _layer_weights(got_a, got_b):
    d_model, shard_cols = got_a.shape[1:]
    pad = jnp.zeros((d_model, D_PROJ - D_IN), BF16)
    out = {"wp": jnp.concatenate([got_a[c, :, lo:hi] for c, lo, hi in _regrouped_from_shards(shard_cols)] + [pad], axis=1)}
    pos = 0
    for n, rows in zip(BRANCH, BRANCH_ROWS):
        out[n] = jnp.concatenate([got_b[c, pos:pos + rows] for c in range(4)], axis=0)
        pos += rows
    return out


def _shard(a, axis, s):
    size = a.shape[axis] // 4
    return lax.slice_in_dim(a, s * size, (s + 1) * size, axis=axis)


def _layer_grad_buffers(g, shard_cols):
    buf_a = jnp.stack([jnp.concatenate([g["w_in"][:, lo:hi] for lo, hi in _shard_from_regrouped(s, shard_cols)], axis=1)
                       for s in range(4)]).astype(BF16)
    buf_b = jnp.stack([jnp.concatenate([_shard(g[n], 0, s) for n in BRANCH], axis=0) for s in range(4)]).astype(BF16)
    return [buf_a, buf_b]


def _split_halves(buf):
    return buf.reshape(4, 2, buf.shape[1] // 2, buf.shape[2])


def _start_reduce(tag, bufs, transits):
    bufs = [_split_halves(g) for g in bufs]
    theirs = pair_split(tag + "_pair_split", bufs)
    return [pair_add(f"{tag}_pair_add_{i}", g, t, tr) for i, (g, t, tr) in enumerate(zip(bufs, theirs, transits))]


def _layer_params(w, exact, weights, l):
    lane = lambda v, lo: jnp.pad(v, (lo, HEAD_DIM - lo - v.shape[0]))[None]
    return dict(
        norm_g=w["norm_g"][l][None], wp=weights["wp"],
        gdn_conv_w=exact["gdn_conv_w"][l], gdn_conv_b=jnp.zeros((1, 3 * N_HEADS * HEAD_DIM), F32),
        ssm_conv_w=exact["ssm_conv_w"][l], ssm_conv_b=w["ssm_conv_b"][l][None],
        bias_vec=lane(w["gdn_dt_bias"][l], 8) + lane(w["ssm_dt_bias"][l], 16),
        alog_vec=lane(w["gdn_a_log"][l], 8) + lane(w["ssm_a_log"][l], 16),
        gdn_norm_g=w["gdn_norm_g"][l][None], d_skip=jnp.repeat(w["ssm_d"][l], SSM_P)[None], ssm_norm_g=w["ssm_norm_g"][l][None],
        wa=weights["w_branch_a"], wb=weights["w_branch_b"], wc=weights["w_branch_c"], wo=weights["w_out"])


def _layer_fwd(h, p, ride):
    lp = h.shape[0]
    bm = _pick(lp, 272, 8)
    kw = dict(bm=bm, lp=lp)
    (u,) = rowmap_fwd("rms_fwd", _f_rmsnorm, [(h, 0)], [(p["norm_g"], False)], 1, width=D_MODEL, ncol=1, **kw)
    proj = matmul(u, p["wp"], "nn", name="proj", bm=lp, bn=640)
    o_a_raw, csum, *rode = sb_fwd(proj, (0, N_HEADS, 2 * N_HEADS), ride)
    qkv = conv_fwd("gdn_conv_fwd", proj, OFF_GDN_QKV // HEAD_DIM, p["gdn_conv_w"], p["gdn_conv_b"], 2 * N_HEADS)
    first, second = rowmap_fwd("gates_fwd", _f_small_gates, [(proj, OFF_SMALL // HEAD_DIM)],
                               [(p["bias_vec"], False), (p["alog_vec"], False)], 2, width=HEAD_DIM, ncol=1, **kw)
    o_b_raw, gdn_states = gdn_fwd(qkv, first)
    xbc = conv_fwd("ssm_conv_fwd", proj, OFF_SSM_XBC // HEAD_DIM, p["ssm_conv_w"], p["ssm_conv_b"], 0)
    y_raw, ssd_states = ssd_fwd(xbc, first, second)
    (o_a,) = rowmap_fwd("gate_a_fwd", _f_gate_silu, [(o_a_raw, 0), (proj, OFF_SB_Z // 1024)], [], 1, width=1024, ncol=1, **kw)
    (o_b,) = rowmap_fwd("gate_b_fwd", _f_head_norm_gate, [(o_b_raw, 0), (proj, OFF_GDN_Z // 1024)], [(p["gdn_norm_g"], False)], 1,
                        width=1024, ncol=1, **kw)
    (o_c,) = rowmap_fwd("gate_c_fwd", _f_ssm_out, [(y_raw, 0), (xbc, 0), (proj, OFF_SSM_Z // 1024)],
                        [(p["d_skip"], True), (p["ssm_norm_g"], True)], 1, width=1024, ncol=SSM_GROUPS, **kw)
    pa = matmul(o_a, p["wa"], "nn", name="branch_a", bm=lp // 2, bn=512)
    pb = matmul(o_b, p["wb"], "nn", name="branch_b", bm=lp // 2, bn=512)
    pc = matmul(o_c, p["wc"], "nn", name="branch_c", bm=lp // 2, bn=512)
    merge_rows = [(pa, 0), (pb, 0), (pc, 0)] + [(proj, OFF_GATES // 512 + 2 * i) for i in range(3)]
    (merged,) = rowmap_fwd("merge_fwd", _f_merge, merge_rows, [], 1, width=512, ncol=2, **kw)
    h_out = matmul(merged, p["wo"], "nn", name="out_proj", bm=lp, bn=512, residual=h)
    saved = dict(h=h, u=u, proj=proj, csum=csum, qkv=qkv, first=first, second=second, o_a_raw=o_a_raw, o_b_raw=o_b_raw,
                 gdn_states=gdn_states, xbc=xbc, y_raw=y_raw, ssd_states=ssd_states, o_a=o_a, o_b=o_b, o_c=o_c, pa=pa, pb=pb, pc=pc,
                 merged=merged)
    return h_out, saved, rode


def _layer_bwd(d_h, p, s, ride):
    lp = d_h.shape[0]
    bm = _pick(lp, 272, 8)
    kw = dict(bm=bm, lp=lp)
    proj = s["proj"]
    g = {}
    d_merged = matmul(d_h, p["wo"], "nt", name="d_merged", bm=lp, bn=512)
    g["w_out"] = matmul(s["merged"], d_h, "tn", name="g_w_out", bm=512, bn=1024, bk=lp)
    merge_rows = [(s["pa"], 0), (s["pb"], 0), (s["pc"], 0)] + [(proj, OFF_GATES // 512 + 2 * i) for i in range(3)]
    (d_pa, d_pb, d_pc, d_ga, d_gb, d_gc), _ = rowmap_bwd("merge_bwd", _f_merge, merge_rows, [], [d_merged], width=512, ncol=2, **kw)
    g["w_branch_a"] = matmul(s["o_a"], d_pa, "tn", name="g_w_a", bm=512, bn=1024, bk=lp)
    g["w_branch_b"] = matmul(s["o_b"], d_pb, "tn", name="g_w_b", bm=512, bn=1024, bk=lp)
    g["w_branch_c"] = matmul(s["o_c"], d_pc, "tn", name="g_w_c", bm=512, bn=1024, bk=lp)
    d_oa = matmul(d_pa, p["wa"], "nt", name="d_o_a", bm=lp, bn=512)
    d_ob = matmul(d_pb, p["wb"], "nt", name="d_o_b", bm=lp, bn=512)
    d_oc = matmul(d_pc, p["wc"], "nt", name="d_o_c", bm=lp, bn=512)
    (d_oa_raw, d_sbz), _ = rowmap_bwd("gate_a_bwd", _f_gate_silu, [(s["o_a_raw"], 0), (proj, OFF_SB_Z // 1024)], [], [d_oa],
                                      width=1024, ncol=1, **kw)
    (d_ob_raw, d_gdz), (g["gdn_norm_g"],) = rowmap_bwd(
        "gate_b_bwd", _f_head_norm_gate, [(s["o_b_raw"], 0), (proj, OFF_GDN_Z // 1024)], [(p["gdn_norm_g"], False)], [d_ob],
        width=1024, ncol=1, **kw)
    (d_y, d_xh, d_ssz), (g_dskip, g["ssm_norm_g"]) = rowmap_bwd(
        "gate_c_bwd", _f_ssm_out, [(s["y_raw"], 0), (s["xbc"], 0), (proj, OFF_SSM_Z // 1024)],
        [(p["d_skip"], True), (p["ssm_norm_g"], True)], [d_oc], width=1024, ncol=SSM_GROUPS, **kw)
    g["gdn_norm_g"], g["ssm_norm_g"] = g["gdn_norm_g"][0], g["ssm_norm_g"][0]
    g["ssm_d"] = g_dskip.reshape(SSM_HEADS, SSM_P).sum(axis=1)
    d_q, d_k, d_v, *rode = sb_bwd(proj, (0, N_HEADS, 2 * N_HEADS), s["csum"], d_oa_raw, ride)
    d_qkv, d_first_gdn = gdn_bwd(s["qkv"], s["first"], s["gdn_states"], d_ob_raw)
    d_xbc_out, d_first_ssd, d_second = ssd_bwd(s["xbc"], s["first"], s["second"], s["ssd_states"], d_y, d_xh)
    d_gdqkv, g["gdn_conv_w"], _ = conv_bwd("gdn_conv_bwd", proj, OFF_GDN_QKV // HEAD_DIM, p["gdn_conv_w"], p["gdn_conv_b"], 2 * N_HEADS,
                                           d_qkv)
    d_xbc, g["ssm_conv_w"], g_cb = conv_bwd("ssm_conv_bwd", proj, OFF_SSM_XBC // HEAD_DIM, p["ssm_conv_w"], p["ssm_conv_b"], 0, d_xbc_out)
    g["ssm_conv_b"] = g_cb[0]
    d_first = d_first_gdn + d_first_ssd
    (d_small,), (g_bias, g_alog) = rowmap_bwd("gates_bwd", _f_small_gates, [(proj, OFF_SMALL // HEAD_DIM)],
                                              [(p["bias_vec"], False), (p["alog_vec"], False)], [d_first, d_second],
                                              width=HEAD_DIM, ncol=1, **kw)
    g["gdn_dt_bias"], g["ssm_dt_bias"] = g_bias[0, 8:16], g_bias[0, 16:48]
    g["gdn_a_log"], g["ssm_a_log"] = g_alog[0, 8:16], g_alog[0, 16:48]
    pieces = [d_q, d_k, d_v, d_sbz, d_gdqkv, d_gdz, d_ssz, d_xbc, d_ga, d_gb, d_gc, d_small]
    g_pieces, d_u, col = [], None, 0
    for i, piece in enumerate(pieces):
        width = piece.shape[1]
        bk = min(width, 512)
        g_pieces.append(matmul(s["u"], piece, "tn", name=f"g_w_in_{i}", bm=1024, bn=min(width, 1024), bk=lp, out_dtype=BF16))
        d_u = matmul(piece, p["wp"], "nt", name=f"d_u_{i}", bm=lp, bn=1024, bk=bk, b_koff=col // bk, residual=d_u)
        col += width
    g["w_in"] = jnp.concatenate(g_pieces, axis=1)
    (d_hn,), (g_norm,) = rowmap_bwd("rms_bwd", _f_rmsnorm, [(s["h"], 0)], [(p["norm_g"], False)], [d_u], width=D_MODEL, ncol=1, **kw)
    g["norm_g"] = g_norm[0]
    return d_h + d_hn, g, rode


def kernel(x, meta_tokens, norm_g, w_in, gdn_conv_w, gdn_a_log, gdn_dt_bias, gdn_norm_g, ssm_conv_w, ssm_conv_b, ssm_a_log, ssm_dt_bias, ssm_d, ssm_norm_g, w_branch_a, w_branch_b, w_branch_c, w_out, final_norm_g, loss_target, m_meta_tokens, m_norm_g, m_w_in, m_gdn_conv_w, m_gdn_a_log, m_gdn_dt_bias, m_gdn_norm_g, m_ssm_conv_w, m_ssm_conv_b, m_ssm_a_log, m_ssm_dt_bias, m_ssm_d, m_ssm_norm_g, m_w_branch_a, m_w_branch_b, m_w_branch_c, m_w_out, m_final_norm_g, v_meta_tokens, v_norm_g, v_w_in, v_gdn_conv_w, v_gdn_a_log, v_gdn_dt_bias, v_gdn_norm_g, v_ssm_conv_w, v_ssm_conv_b, v_ssm_a_log, v_ssm_dt_bias, v_ssm_d, v_ssm_norm_g, v_w_branch_a, v_w_branch_b, v_w_branch_c, v_w_out, v_final_norm_g):
    w = dict(meta_tokens=meta_tokens, norm_g=norm_g, w_in=w_in, gdn_conv_w=gdn_conv_w, gdn_a_log=gdn_a_log, gdn_dt_bias=gdn_dt_bias,
             gdn_norm_g=gdn_norm_g, ssm_conv_w=ssm_conv_w, ssm_conv_b=ssm_conv_b, ssm_a_log=ssm_a_log, ssm_dt_bias=ssm_dt_bias,
             ssm_d=ssm_d, ssm_norm_g=ssm_norm_g, w_branch_a=w_branch_a, w_branch_b=w_branch_b, w_branch_c=w_branch_c, w_out=w_out,
             final_norm_g=final_norm_g)
    m = dict(meta_tokens=m_meta_tokens, norm_g=m_norm_g, w_in=m_w_in, gdn_conv_w=m_gdn_conv_w, gdn_a_log=m_gdn_a_log,
             gdn_dt_bias=m_gdn_dt_bias, gdn_norm_g=m_gdn_norm_g, ssm_conv_w=m_ssm_conv_w, ssm_conv_b=m_ssm_conv_b,
             ssm_a_log=m_ssm_a_log, ssm_dt_bias=m_ssm_dt_bias, ssm_d=m_ssm_d, ssm_norm_g=m_ssm_norm_g, w_branch_a=m_w_branch_a,
             w_branch_b=m_w_branch_b, w_branch_c=m_w_branch_c, w_out=m_w_out, final_norm_g=m_final_norm_g)
    v = dict(meta_tokens=v_meta_tokens, norm_g=v_norm_g, w_in=v_w_in, gdn_conv_w=v_gdn_conv_w, gdn_a_log=v_gdn_a_log,
             gdn_dt_bias=v_gdn_dt_bias, gdn_norm_g=v_gdn_norm_g, ssm_conv_w=v_ssm_conv_w, ssm_conv_b=v_ssm_conv_b,
             ssm_a_log=v_ssm_a_log, ssm_dt_bias=v_ssm_dt_bias, ssm_d=v_ssm_d, ssm_norm_g=v_ssm_norm_g, w_branch_a=v_w_branch_a,
             w_branch_b=v_w_branch_b, w_branch_c=v_w_branch_c, w_out=v_w_out, final_norm_g=v_final_norm_g)
    depth = norm_g.shape[0]
    shard_cols = w_in.shape[-1]
    placed, placed_small = _place_weights(w)
    got_a, got_b, got_s = gather_shards("gather_first", placed[0] + [placed_small])
    exact = _exact_weights(w, got_s)

    h = jnp.concatenate([jnp.zeros((PAD, D_MODEL), F32), exact["meta_tokens"], x[0]], axis=0)
    params, saved = [], []
    for l in range(depth):
        params.append(_layer_params(w, exact, _layer_weights(got_a, got_b), l))
        h, s, rode = _layer_fwd(h, params[l], gather_ride(placed[l + 1]) if l + 1 < depth else None)
        saved.append(s)
        if rode:
            got_a, got_b = rode
    loss, d_h, g_final = loss_head(h, loss_target[0], final_norm_g[None])

    layer_grads, reds, waiting = [None] * depth, [None] * depth, None
    for l in reversed(range(depth)):
        d_h, layer_grads[l], rode = _layer_bwd(d_h, params[l], saved[l], exchange_ride(waiting) if waiting else None)
        if waiting:
            reds[l + 1] = [chip_add(f"grads_chip_add_{i}", gt, p) for i, (gt, p) in enumerate(zip(rode, waiting))]
        waiting = _start_reduce("grads", _layer_grad_buffers(layer_grads[l], shard_cols), [BF16, BF16])
    grads = {n: jnp.stack([layer_grads[l][n] for l in range(depth)]) for n in WEIGHTS
             if n not in ("meta_tokens", "final_norm_g", "w_in") + BRANCH}
    grads["meta_tokens"] = d_h[PAD:PAD + N_META]
    grads["final_norm_g"] = g_final[0]
    grad_x = d_h[PAD + N_META:][None]
    buf_s = jnp.stack([_pack([_shard(grads[n], SHARD_AXIS[n], s).astype(BF16).reshape(-1) for n in EXACT], row_unit=32) for s in range(4)])
    small = _pack([grads[n].reshape(-1) for n in REPLICATED], row_unit=32)
    last = waiting + _start_reduce("small_grads", [buf_s, jnp.broadcast_to(small[None], (4,) + small.shape)], [BF16, F32])
    got = chip_exchange("grads_chip_exchange", last)
    sums = [chip_add(f"grads_chip_add_{i}", gt, p) for i, (gt, p) in enumerate(zip(got, last))]
    reds[0] = sums[:2]
    joined = pair_join("grads_pair_join", [r for layer in reds for r in layer] + sums[2:])
    red = {"w_in": jnp.stack(joined[0:2 * depth:2])}
    pos = 0
    for n, rows in zip(BRANCH, BRANCH_ROWS):
        red[n] = jnp.stack([joined[2 * l + 1][pos:pos + rows] for l in range(depth)])
        pos += rows
    red.update(zip(EXACT, _unpack(joined[-2], [w[n].shape for n in EXACT])))
    small_red = joined[-1]
    delta, new_m, new_v = {}, {}, {}
    for n in SHARDED:
        delta[n], new_m[n], new_v[n] = adamw("adamw_" + n, w[n], red[n], m[n], v[n])
    pack_small = lambda d: _pack([d[n].reshape(-1) for n in REPLICATED], row_unit=32)
    small = adamw("adamw_small", pack_small(w), small_red, pack_small(m), pack_small(v))
    shapes = [w[n].shape for n in REPLICATED]
    red.update(zip(REPLICATED, _unpack(small_red, shapes)))
    for d, buf in zip((delta, new_m, new_v), small):
        d.update(zip(REPLICATED, _unpack(buf, shapes)))
    total_loss = lax.psum(loss[0, 0], ("x", "y", "c"))
    return (total_loss, grad_x, *[red[n] for n in WEIGHTS], *[delta[n] for n in WEIGHTS], *[new_m[n] for n in WEIGHTS],
            *[new_v[n] for n in WEIGHTS])
```

```python
import functools
import math

import jax
import jax.numpy as jnp
from jax import lax
from jax.experimental import pallas as pl
from jax.experimental.pallas import tpu as pltpu

F32 = jnp.float32
BF16 = jnp.bfloat16

N_META = 16
RMS_EPS = 1e-6
L2_EPS = 1e-6
CONV_K = 4
D_MODEL = 1024
HEAD_DIM = 128
N_HEADS = 8
CHUNK = 64
SB_BLOCK = 128
PAD = SB_BLOCK - N_META
SSM_INNER = 2048
SSM_P = 64
SSM_HEADS = 32
SSM_GROUPS = 2
SSM_HG = SSM_HEADS // SSM_GROUPS
SSM_N = 128
VMEM_LIMIT = 56 * 1024 * 1024

def _dims(mode, ndim):
    lhs, rhs = {"nn": (1, 0), "nt": (1, 1), "tn": (0, 0)}[mode]
    off = ndim - 2
    return (((lhs + off,), (rhs + off,)), (tuple(range(off)), tuple(range(off))))


def _dot(a, b, mode):
    return lax.dot_general(a, b, _dims(mode, a.ndim), preferred_element_type=F32)


def _halves(a):
    hi = a.astype(BF16)
    return hi, (a - hi.astype(F32)).astype(BF16)


def _mm_raw(a, b, mode, kind):
    if kind == "bf16":
        return _dot(a.astype(BF16), b.astype(BF16), mode)
    if kind == "lhs01":
        hi, lo = _halves(b)
        a = a.astype(BF16)
        return _dot(a, hi, mode) + _dot(a, lo, mode)
    if kind == "rhs01":
        hi, lo = _halves(a)
        b = b.astype(BF16)
        return _dot(hi, b, mode) + _dot(lo, b, mode)
    a_hi, a_lo = _halves(a)
    b_hi, b_lo = _halves(b)
    return _dot(a_hi, b_hi, mode) + (_dot(a_hi, b_lo, mode) + _dot(a_lo, b_hi, mode))


@functools.partial(jax.custom_vjp, nondiff_argnums=(2, 3))
def _mm(a, b, mode="nn", kind="bf16"):
    return _mm_raw(a, b, mode, kind)


def _mm_fwd(a, b, mode, kind):
    return _mm_raw(a, b, mode, kind), (a, b)


def _mm_bwd(mode, kind, res, g):
    a, b = res
    if kind == "lhs01":
        return jnp.zeros_like(a), _mm_raw(a, g, {"nn": "tn", "tn": "nn"}[mode], "lhs01")
    if kind == "rhs01":
        return _mm_raw(g, b, {"nn": "nt", "nt": "nn"}[mode], "rhs01"), jnp.zeros_like(b)
    if mode == "nn":
        return _mm_raw(g, b, "nt", kind), _mm_raw(a, g, "tn", kind)
    if mode == "nt":
        return _mm_raw(g, b, "nn", kind), _mm_raw(g, a, "tn", kind)
    return _mm_raw(b, g, "nt", kind), _mm_raw(a, g, "nn", kind)


_mm.defvjp(_mm_fwd, _mm_bwd)


def _iota2(shape, axis):
    return lax.broadcasted_iota(jnp.int32, shape, axis)


def _inv_unit_lower_raw(m):
    size = m.shape[-1]
    eye = (_iota2((size, size), 0) == _iota2((size, size), 1)).astype(F32)
    n = -m
    t = eye + n
    p = n
    steps = int(math.log2(size)) - 1
    for _ in range(steps):
        p = _mm_raw(p, p, "nn", "x3")
        t = t + _mm_raw(t, p, "nn", "x3")
    return t


@jax.custom_vjp
def _inv_unit_lower(m):
    return _inv_unit_lower_raw(m)


def _inv_fwd(m):
    t = _inv_unit_lower_raw(m)
    return t, t


def _inv_bwd(t, g):
    return (-_mm_raw(_mm_raw(t, g, "tn", "x3"), t, "nt", "x3"),)


_inv_unit_lower.defvjp(_inv_fwd, _inv_bwd)


def _safe_decay(col, row, keep):
    return jnp.where(keep, jnp.exp(jnp.where(keep, col - row, 0.0)), 0.0)


def _col_to_row(col):
    n = col.shape[-2]
    eye = _iota2((n, n), 0) == _iota2((n, n), 1)
    return jnp.sum(jnp.where(eye, col, 0.0), axis=-2, keepdims=True)


def _cumsum_col(col):
    n = col.shape[-2]
    li, si = _iota2((n, n), 0), _iota2((n, n), 1)
    row = _col_to_row(col)
    c_col = jnp.sum(jnp.where(li >= si, row, 0.0), axis=-1, keepdims=True)
    c_row = jnp.sum(jnp.where(li <= si, col, 0.0), axis=-2, keepdims=True)
    return c_col, c_row


def _gdn_chunk(q, k, v, g, beta, state):
    cl = q.shape[-2]
    li, si = _iota2((cl, cl), 0), _iota2((cl, cl), 1)
    gc_col, gc_row = _cumsum_col(g)
    g_last = jnp.sum(g, axis=-2, keepdims=True)
    dec_strict = _safe_decay(gc_col, gc_row, li > si)
    dec_incl = _safe_decay(gc_col, gc_row, li >= si)
    e_gc = jnp.exp(gc_col)
    qs = q * (HEAD_DIM ** -0.5)
    kb = k * beta
    m = _mm(kb, k, "nt") * dec_strict
    t_inv = _inv_unit_lower(m)
    u = _mm(t_inv, v * beta)
    w = _mm(t_inv, kb * e_gc)
    a_qk = _mm(qs, k, "nt") * dec_incl
    q_dec = qs * e_gc
    k_end = k * jnp.exp(g_last - gc_col)
    v_new = u - _mm(w, state)
    o = _mm(q_dec, state) + _mm(a_qk, v_new)
    new_state = state * jnp.exp(g_last) + _mm(k_end, v_new, "tn")
    return o, new_state


def _gdn_operands(qkv_ref, gt):
    nh, width = N_HEADS, N_HEADS * HEAD_DIM
    heads = lambda off: jnp.stack([qkv_ref[:, off + h * HEAD_DIM:off + (h + 1) * HEAD_DIM] for h in range(nh)])
    cols = lambda off: jnp.stack([gt[:, off + h:off + h + 1] for h in range(nh)])
    return heads(0), heads(width), heads(2 * width), cols(nh), cols(0)


def gdn_fwd(qkv, gates):
    lp = qkv.shape[0]
    nh = N_HEADS
    nc = lp // CHUNK
    width = nh * HEAD_DIM

    def body(qkv_ref, gt_ref, o_ref, s_ref, state):
        @pl.when(pl.program_id(0) == 0)
        def _():
            state[...] = jnp.zeros_like(state)

        s_in = state[...]
        s_ref[0] = s_in
        o, s_new = _gdn_chunk(*_gdn_operands(qkv_ref, gt_ref[...]), s_in)
        for h in range(nh):
            o_ref[:, h * HEAD_DIM:(h + 1) * HEAD_DIM] = o[h]
        state[...] = s_new

    return pl.pallas_call(
        body, name="gdn_fwd", grid=(nc,),
        in_specs=[pl.BlockSpec((CHUNK, 3 * width), lambda c: (c, 0)), pl.BlockSpec((CHUNK, HEAD_DIM), lambda c: (c, 0))],
        out_specs=[pl.BlockSpec((CHUNK, width), lambda c: (c, 0)), pl.BlockSpec((1, nh, HEAD_DIM, HEAD_DIM), lambda c: (c, 0, 0, 0))],
        out_shape=[jax.ShapeDtypeStruct((lp, width), F32), jax.ShapeDtypeStruct((nc, nh, HEAD_DIM, HEAD_DIM), F32)],
        scratch_shapes=[pltpu.VMEM((nh, HEAD_DIM, HEAD_DIM), F32)],
        compiler_params=pltpu.CompilerParams(dimension_semantics=("arbitrary",), vmem_limit_bytes=VMEM_LIMIT),
    )(qkv, gates)


def gdn_bwd(qkv, gates, states, d_o):
    lp = qkv.shape[0]
    nh = N_HEADS
    nc = lp // CHUNK
    width = nh * HEAD_DIM

    def body(qkv_ref, gt_ref, s_ref, do_ref, dqkv_ref, dgt_ref, d_state):
        @pl.when(pl.program_id(0) == 0)
        def _():
            d_state[...] = jnp.zeros_like(d_state)

        _, pull = jax.vjp(_gdn_chunk, *_gdn_operands(qkv_ref, gt_ref[...]), s_ref[0])
        d_o = jnp.stack([do_ref[:, h * HEAD_DIM:(h + 1) * HEAD_DIM] for h in range(nh)])
        dq, dk, dv, dg, db, ds = pull((d_o, d_state[...]))
        lane = _iota2((CHUNK, HEAD_DIM), 1)
        d_gt = jnp.zeros((CHUNK, HEAD_DIM), F32)
        for h in range(nh):
            for part, val in enumerate((dq, dk, dv)):
                dqkv_ref[:, part * width + h * HEAD_DIM:part * width + (h + 1) * HEAD_DIM] = val[h]
            d_gt = d_gt + jnp.where(lane == h, db[h], 0.0) + jnp.where(lane == nh + h, dg[h], 0.0)
        dgt_ref[...] = d_gt
        d_state[...] = ds

    rev = lambda c: (nc - 1 - c, 0)
    return pl.pallas_call(
        body, name="gdn_bwd", grid=(nc,),
        in_specs=[pl.BlockSpec((CHUNK, 3 * width), rev), pl.BlockSpec((CHUNK, HEAD_DIM), rev),
                  pl.BlockSpec((1, nh, HEAD_DIM, HEAD_DIM), lambda c: (nc - 1 - c, 0, 0, 0)), pl.BlockSpec((CHUNK, width), rev)],
        out_specs=[pl.BlockSpec((CHUNK, 3 * width), rev), pl.BlockSpec((CHUNK, HEAD_DIM), rev)],
        out_shape=[jax.ShapeDtypeStruct((lp, 3 * width), F32), jax.ShapeDtypeStruct((lp, HEAD_DIM), F32)],
        scratch_shapes=[pltpu.VMEM((nh, HEAD_DIM, HEAD_DIM), F32)],
        compiler_params=pltpu.CompilerParams(dimension_semantics=("arbitrary",), vmem_limit_bytes=VMEM_LIMIT),
    )(qkv, gates, states, d_o)


def _head_expand():
    width = SSM_HG * SSM_P
    return (_iota2((SSM_HG, width), 1) // SSM_P == _iota2((SSM_HG, width), 0)).astype(F32)


def _ssd_chunk(x, b, c, dt, la, state):
    cl = x.shape[0]
    li, si = _iota2((cl, cl), 0), _iota2((cl, cl), 1)
    causal = li >= si
    expand = _head_expand()
    tri = causal.astype(F32)
    xs = x * _mm(dt, expand, "nn", "rhs01")
    la_x = _mm(la, expand, "nn", "rhs01")
    cs_x = _mm(tri, la_x, "nn", "lhs01")
    last_x = jnp.sum(la_x, axis=0, keepdims=True)
    cs = _mm(tri, la, "nn", "lhs01")
    scores = _mm(c, b, "nt")
    head_id = _iota2((1, SSM_HG), 1)
    per_tile = HEAD_DIM // SSM_P
    tile_head = _iota2((1, HEAD_DIM), 1) // SSM_P
    within = []
    for t in range(SSM_HG // per_tile):
        xs_t = xs[:, t * HEAD_DIM:(t + 1) * HEAD_DIM]
        acc = jnp.zeros((cl, HEAD_DIM), F32)
        for hh in range(per_tile):
            cs_col = jnp.sum(jnp.where(head_id == t * per_tile + hh, cs, 0.0), axis=1, keepdims=True)
            decay = _safe_decay(cs_col, _col_to_row(cs_col), causal)
            acc = acc + _mm(scores * decay, jnp.where(tile_head == hh, xs_t, 0.0))
        within.append(acc)
    y = _mm(c, state) * jnp.exp(cs_x) + jnp.concatenate(within, axis=1)
    new_state = state * jnp.exp(last_x) + _mm(b, xs * jnp.exp(last_x - cs_x), "tn")
    return y, new_state


GATE_DT = 16


def _place_lanes(v, lo):
    n = v.shape[1]
    sel = (_iota2((n, HEAD_DIM), 1) == _iota2((n, HEAD_DIM), 0) + lo).astype(F32)
    return _mm_raw(v, sel, "nn", "rhs01")


def ssd_fwd(xbc, first, second):
    lp = xbc.shape[0]
    nc = lp // CHUNK
    width = SSM_HG * SSM_P
    b_off, c_off = SSM_INNER, SSM_INNER + SSM_GROUPS * SSM_N

    def body(x_ref, f_ref, s2_ref, y_ref, s_ref, state):
        @pl.when(pl.program_id(0) == 0)
        def _():
            state[...] = jnp.zeros_like(state)

        f, s2 = f_ref[...], s2_ref[...]
        for g in range(SSM_GROUPS):
            lo = GATE_DT + g * SSM_HG
            s_in = state[g]
            s_ref[0, g] = s_in
            y, s_new = _ssd_chunk(x_ref[:, g * width:(g + 1) * width], x_ref[:, b_off + g * SSM_N:b_off + (g + 1) * SSM_N],
                                  x_ref[:, c_off + g * SSM_N:c_off + (g + 1) * SSM_N], f[:, lo:lo + SSM_HG], s2[:, lo:lo + SSM_HG], s_in)
            y_ref[:, g * width:(g + 1) * width] = y
            state[g] = s_new

    row = lambda cols: pl.BlockSpec((CHUNK, cols), lambda k: (k, 0))
    return pl.pallas_call(
        body, name="ssd_fwd", grid=(nc,),
        in_specs=[row(xbc.shape[1]), row(HEAD_DIM), row(HEAD_DIM)],
        out_specs=[row(SSM_INNER), pl.BlockSpec((1, SSM_GROUPS, SSM_N, width), lambda k: (k, 0, 0, 0))],
        out_shape=[jax.ShapeDtypeStruct((lp, SSM_INNER), F32), jax.ShapeDtypeStruct((nc, SSM_GROUPS, SSM_N, width), F32)],
        scratch_shapes=[pltpu.VMEM((SSM_GROUPS, SSM_N, width), F32)],
        compiler_params=pltpu.CompilerParams(dimension_semantics=("arbitrary",), vmem_limit_bytes=VMEM_LIMIT),
    )(xbc, first, second)


def ssd_bwd(xbc, first, second, states, d_y, d_xh):
    lp = xbc.shape[0]
    nc = lp // CHUNK
    width = SSM_HG * SSM_P
    b_off, c_off = SSM_INNER, SSM_INNER + SSM_GROUPS * SSM_N

    def body(x_ref, f_ref, s2_ref, s_ref, dy_ref, dxh_ref, dx_ref, df_ref, ds2_ref, d_state):
        @pl.when(pl.program_id(0) == 0)
        def _():
            d_state[...] = jnp.zeros_like(d_state)

        f, s2 = f_ref[...], s2_ref[...]
        d_f = jnp.zeros((CHUNK, HEAD_DIM), F32)
        d_s2 = jnp.zeros((CHUNK, HEAD_DIM), F32)
        for g in range(SSM_GROUPS):
            lo = GATE_DT + g * SSM_HG
            x_l = slice(g * width, (g + 1) * width)
            b_l = slice(b_off + g * SSM_N, b_off + (g + 1) * SSM_N)
            c_l = slice(c_off + g * SSM_N, c_off + (g + 1) * SSM_N)
            _, pull = jax.vjp(_ssd_chunk, x_ref[:, x_l], x_ref[:, b_l], x_ref[:, c_l], f[:, lo:lo + SSM_HG], s2[:, lo:lo + SSM_HG],
                              s_ref[0, g])
            dx, db, dc, ddt, dla, ds = pull((dy_ref[:, x_l], d_state[g]))
            dx_ref[:, x_l] = dx + dxh_ref[:, x_l]
            dx_ref[:, b_l] = db
            dx_ref[:, c_l] = dc
            d_f = d_f + _place_lanes(ddt, lo)
            d_s2 = d_s2 + _place_lanes(dla, lo)
            d_state[g] = ds
        df_ref[...] = d_f
        ds2_ref[...] = d_s2

    row = lambda cols: pl.BlockSpec((CHUNK, cols), lambda k: (nc - 1 - k, 0))
    gate_shape = jax.ShapeDtypeStruct((lp, HEAD_DIM), F32)
    return pl.pallas_call(
        body, name="ssd_bwd", grid=(nc,),
        in_specs=[row(xbc.shape[1]), row(HEAD_DIM), row(HEAD_DIM),
                  pl.BlockSpec((1, SSM_GROUPS, SSM_N, width), lambda k: (nc - 1 - k, 0, 0, 0)), row(SSM_INNER), row(SSM_INNER)],
        out_specs=[row(xbc.shape[1]), row(HEAD_DIM), row(HEAD_DIM)],
        out_shape=[jax.ShapeDtypeStruct(xbc.shape, F32), gate_shape, gate_shape],
        scratch_shapes=[pltpu.VMEM((SSM_GROUPS, SSM_N, width), F32)],
        compiler_params=pltpu.CompilerParams(dimension_semantics=("arbitrary",), vmem_limit_bytes=VMEM_LIMIT),
    )(xbc, first, second, states, d_y, d_xh)


SB_QROWS = 544


def _mm_tri(a, tri):
    return _mm_raw(a, tri.astype(BF16), "nn", "rhs01")


def _sb_scores(q_scaled, kb, row0, j):
    shape = (q_scaled.shape[0], SB_BLOCK)
    z = _mm_raw(q_scaled, kb, "nt", "bf16")
    q_pos = row0 + _iota2(shape, 0)
    k_pos = j * SB_BLOCK + _iota2(shape, 1)
    valid = (k_pos < q_pos) & (k_pos >= PAD)
    sp = jnp.maximum(z, 0.0) + jnp.log(1.0 + jnp.exp(-jnp.abs(z)))
    lk = jnp.where(valid, -sp, 0.0)
    return z, sp, valid, lk


def sb_fwd(src, offs, ride=None):
    lp = src.shape[0]
    nh = N_HEADS
    qb = _pick(lp, SB_QROWS, 8)
    scale = HEAD_DIM ** -0.5
    blk = SB_BLOCK

    n_in = len(ride.ins) if ride else 0
    n_out = len(ride.out_shapes) if ride else 0

    def body(*refs):
        q_ref, k_ref, v_ref = refs[:3]
        o_ref, c_ref = refs[3 + n_in:5 + n_in]
        ride_refs = (refs[3:3 + n_in], refs[5 + n_in:5 + n_in + n_out], refs[5 + n_in + n_out:])
        i = pl.program_id(1)
        if ride:
            @pl.when((pl.program_id(0) == 0) & (i == 0))
            def _():
                ride.start(*ride_refs)
        q_scaled = q_ref[...] * scale
        upper = _iota2((blk, blk), 0) > _iota2((blk, blk), 1)
        n_blocks = ((i + 1) * qb + blk - 1) // blk

        def step(it, carry):
            acc, c = carry
            j = n_blocks - 1 - it
            rows = pl.ds(pl.multiple_of(j * blk, blk), blk)
            z, sp, valid, lk = _sb_scores(q_scaled, k_ref[rows, :], i * qb, j)
            later = _mm_tri(lk, upper) + c
            w = jnp.where(valid, jnp.exp(z - sp + later), 0.0)
            acc = acc + _mm_raw(w, v_ref[rows, :], "nn", "bf16")
            return acc, c + jnp.sum(lk, axis=1, keepdims=True)

        acc, c = lax.fori_loop(0, n_blocks, step, (jnp.zeros((qb, HEAD_DIM), F32), jnp.zeros((qb, 1), F32)))
        o_ref[...] = acc
        c_ref[0] = c
        if ride:
            @pl.when((pl.program_id(0) == nh - 1) & (i == lp // qb - 1))
            def _():
                ride.finish(*ride_refs)

    qspec = pl.BlockSpec((qb, HEAD_DIM), lambda h, i: (i, offs[0] + h))
    kspec = pl.BlockSpec((lp, HEAD_DIM), lambda h, i: (0, offs[1] + h))
    vspec = pl.BlockSpec((lp, HEAD_DIM), lambda h, i: (0, offs[2] + h))
    ospec = pl.BlockSpec((qb, HEAD_DIM), lambda h, i: (i, h))
    cspec = pl.BlockSpec((1, qb, 1), lambda h, i: (h, i, 0))
    return pl.pallas_call(
        body, name="sb_fwd", grid=(nh, lp // qb), in_specs=[qspec, kspec, vspec] + [ANY] * n_in, out_specs=[ospec, cspec] + [ANY] * n_out,
        out_shape=[jax.ShapeDtypeStruct((lp, nh * HEAD_DIM), F32), jax.ShapeDtypeStruct((nh, lp, 1), F32)]
        + (ride.out_shapes if ride else []),
        scratch_shapes=ride.sems if ride else [],
        input_output_aliases={3 + k: 2 + k for k in range(n_in)} if ride and ride.alias else {},
        compiler_params=pltpu.CompilerParams(dimension_semantics=("arbitrary", "arbitrary"), vmem_limit_bytes=VMEM_LIMIT),
    )(src, src, src, *(ride.ins if ride else []))


def sb_bwd(src, offs, csum, d_o, ride=None):
    lp = src.shape[0]
    nh = N_HEADS
    qb = _pick(lp, SB_QROWS, 8)
    scale = HEAD_DIM ** -0.5
    blk = SB_BLOCK

    n_in = len(ride.ins) if ride else 0
    n_out = len(ride.out_shapes) if ride else 0

    def body(*refs):
        q_ref, k_ref, v_ref, c_ref, do_ref = refs[:5]
        dq_ref, dk_out, dv_out = refs[5 + n_in:8 + n_in]
        ride_refs = (refs[5:5 + n_in], refs[8 + n_in:8 + n_in + n_out], refs[10 + n_in + n_out:])
        dk_ref, dv_ref = refs[8 + n_in + n_out:10 + n_in + n_out]
        i = pl.program_id(1)
        if ride:
            @pl.when((pl.program_id(0) == 0) & (i == 0))
            def _():
                ride.start(*ride_refs)

        @pl.when(i == 0)
        def _():
            dk_ref[...] = jnp.zeros_like(dk_ref)
            dv_ref[...] = jnp.zeros_like(dv_ref)

        q_scaled = q_ref[...] * scale
        d_out = do_ref[...]
        total = c_ref[0]
        lower_incl = _iota2((blk, blk), 0) <= _iota2((blk, blk), 1)
        lower = _iota2((blk, blk), 0) < _iota2((blk, blk), 1)
        n_blocks = ((i + 1) * qb + blk - 1) // blk

        def step(j, carry):
            acc, cp, ep = carry
            rows = pl.ds(pl.multiple_of(j * blk, blk), blk)
            kb = k_ref[rows, :]
            vb = v_ref[rows, :]
            z, sp, valid, lk = _sb_scores(q_scaled, kb, i * qb, j)
            later = total - cp - _mm_tri(lk, lower_incl)
            w = jnp.where(valid, jnp.exp(z - sp + later), 0.0)
            e = w * _mm_raw(d_out, vb, "nt", "bf16")
            before = ep + _mm_tri(e, lower)
            dz = jnp.where(valid, e * jnp.exp(-sp) - before * jnp.exp(z - sp), 0.0)
            dk_ref[rows, :] += _mm_raw(dz, q_scaled, "tn", "bf16")
            dv_ref[rows, :] += _mm_raw(w, d_out, "tn", "bf16")
            acc = acc + _mm_raw(dz, kb, "nn", "bf16")
            return acc, cp + jnp.sum(lk, axis=1, keepdims=True), ep + jnp.sum(e, axis=1, keepdims=True)

        zero_col = jnp.zeros((qb, 1), F32)
        acc, _, _ = lax.fori_loop(0, n_blocks, step, (jnp.zeros((qb, HEAD_DIM), F32), zero_col, zero_col))
        dq_ref[...] = (acc * scale).astype(dq_ref.dtype)

        @pl.when(i == lp // qb - 1)
        def _():
            dk_out[...] = dk_ref[...].astype(dk_out.dtype)
            dv_out[...] = dv_ref[...].astype(dv_out.dtype)

        if ride:
            @pl.when((pl.program_id(0) == nh - 1) & (i == lp // qb - 1))
            def _():
                ride.finish(*ride_refs)

    qspec = pl.BlockSpec((qb, HEAD_DIM), lambda h, i: (i, offs[0] + h))
    kspec = pl.BlockSpec((lp, HEAD_DIM), lambda h, i: (0, offs[1] + h))
    vspec = pl.BlockSpec((lp, HEAD_DIM), lambda h, i: (0, offs[2] + h))
    ospec = pl.BlockSpec((qb, HEAD_DIM), lambda h, i: (i, h))
    fullspec = pl.BlockSpec((lp, HEAD_DIM), lambda h, i: (0, h))
    cspec = pl.BlockSpec((1, qb, 1), lambda h, i: (h, i, 0))
    return pl.pallas_call(
        body, name="sb_bwd", grid=(nh, lp // qb), in_specs=[qspec, kspec, vspec, cspec, ospec] + [ANY] * n_in,
        out_specs=[ospec, fullspec, fullspec] + [ANY] * n_out,
        out_shape=[jax.ShapeDtypeStruct((lp, nh * HEAD_DIM), BF16)] * 3 + (ride.out_shapes if ride else []),
        scratch_shapes=[pltpu.VMEM((lp, HEAD_DIM), F32)] * 2 + (ride.sems if ride else []),
        input_output_aliases={5 + k: 3 + k for k in range(n_in)} if ride and ride.alias else {},
        compiler_params=pltpu.CompilerParams(dimension_semantics=("arbitrary", "arbitrary"), vmem_limit_bytes=VMEM_LIMIT),
    )(src, src, src, csum, d_o, *(ride.ins if ride else []))


def _pick(n, target, unit):
    if n <= target:
        return n
    best = None
    for d in range(unit, target + 1, unit):
        if n % d == 0:
            best = d
    assert best is not None, (n, target, unit)
    return best


def matmul(a, b, mode="nn", *, name, bm=1088, bn=640, bk=2176, residual=None, out_dtype=F32, b_koff=0):
    if mode == "nn":
        (m, k), n = a.shape, b.shape[1]
    elif mode == "nt":
        (m, k), n = a.shape, b.shape[0]
    else:
        (k, m), n = a.shape, b.shape[1]
    assert b_koff == 0 or mode == "nt"
    bm = _pick(m, bm, 128 if mode == "tn" else 8)
    bn = _pick(n, bn, 128 if mode != "nt" else 8)
    bk = _pick(k, bk, 128 if mode != "tn" else 8)
    nk = k // bk

    def body(*refs):
        if residual is None:
            a_ref, b_ref, o_ref, acc = refs
            r_ref = None
        else:
            a_ref, b_ref, r_ref, o_ref, acc = refs
        kk = pl.program_id(2)
        part = _mm_raw(a_ref[...], b_ref[...], mode, "bf16")

        @pl.when(kk == 0)
        def _():
            acc[...] = part

        @pl.when(kk > 0)
        def _():
            acc[...] += part

        @pl.when(kk == nk - 1)
        def _():
            res = acc[...]
            if r_ref is not None:
                res = res + r_ref[...]
            o_ref[...] = res.astype(out_dtype)

    a_spec = pl.BlockSpec((bk, bm), lambda i, j, kk: (kk, i)) if mode == "tn" else pl.BlockSpec((bm, bk), lambda i, j, kk: (i, kk))
    b_spec = pl.BlockSpec((bn, bk), lambda i, j, kk: (j, b_koff + kk)) if mode == "nt" else pl.BlockSpec((bk, bn), lambda i, j, kk: (kk, j))
    o_spec = pl.BlockSpec((bm, bn), lambda i, j, kk: (i, j))
    ins, specs = [a, b], [a_spec, b_spec]
    if residual is not None:
        ins.append(residual)
        specs.append(o_spec)
    return pl.pallas_call(
        body, name=name, grid=(m // bm, n // bn, nk), in_specs=specs, out_specs=o_spec,
        out_shape=jax.ShapeDtypeStruct((m, n), out_dtype),
        scratch_shapes=[pltpu.VMEM((bm, bn), F32)],
        compiler_params=pltpu.CompilerParams(dimension_semantics=("arbitrary", "arbitrary", "arbitrary"), vmem_limit_bytes=VMEM_LIMIT),
    )(*ins)


def _row_specs(rows, params, width, bm):
    row_specs = [pl.BlockSpec((bm, width), (lambda j, i, off=off: (i, off + j))) for _, off in rows]
    par_specs = [pl.BlockSpec((p.shape[0], width) if per_col else p.shape, ((lambda j, i: (0, j)) if per_col else (lambda j, i: (0, 0))))
                 for p, per_col in params]
    return row_specs, par_specs


def rowmap_fwd(name, fn, rows, params, n_out, *, width, ncol, bm, lp, out_dtypes=None):
    out_dtypes = out_dtypes or [F32] * n_out
    row_specs, par_specs = _row_specs(rows, params, width, bm)
    nr = len(rows)

    def body(*refs):
        ins, outs = refs[:nr + len(params)], refs[nr + len(params):]
        row_ids = pl.program_id(1) * bm + _iota2((bm, 1), 0)
        res = fn(row_ids, *[r[...].astype(F32) for r in ins])
        for o_ref, val in zip(outs, res):
            o_ref[...] = val.astype(o_ref.dtype)

    o_spec = pl.BlockSpec((bm, width), lambda j, i: (i, j))
    return pl.pallas_call(
        body, name=name, grid=(ncol, lp // bm), in_specs=row_specs + par_specs, out_specs=[o_spec] * n_out,
        out_shape=[jax.ShapeDtypeStruct((lp, ncol * width), dt) for dt in out_dtypes],
        compiler_params=pltpu.CompilerParams(dimension_semantics=("arbitrary", "arbitrary"), vmem_limit_bytes=VMEM_LIMIT),
    )(*[a for a, _ in rows], *[p for p, _ in params])


def rowmap_bwd(name, fn, rows, params, d_outs, *, width, ncol, bm, lp, d_row_dtypes=None):
    d_row_dtypes = d_row_dtypes or [F32] * len(rows)
    row_specs, par_specs = _row_specs(rows, params, width, bm)
    nr, npar, nout = len(rows), len(params), len(d_outs)

    def body(*refs):
        ins = refs[:nr + npar]
        dos = refs[nr + npar:nr + npar + nout]
        d_rows = refs[nr + npar + nout:nr + npar + nout + nr]
        d_pars = refs[nr + npar + nout + nr:]
        j, i = pl.program_id(0), pl.program_id(1)
        row_ids = i * bm + _iota2((bm, 1), 0)
        _, pull = jax.vjp(lambda *xs: tuple(fn(row_ids, *xs)), *[r[...].astype(F32) for r in ins])
        grads = pull(tuple(d[...].astype(F32) for d in dos))
        for ref, val in zip(d_rows, grads[:nr]):
            ref[...] = val.astype(ref.dtype)
        for ref, val, (_, per_col) in zip(d_pars, grads[nr:], params):
            first = (i == 0) if per_col else ((i == 0) & (j == 0))

            @pl.when(first)
            def _(ref=ref, val=val):
                ref[...] = val

            @pl.when(jnp.logical_not(first))
            def _(ref=ref, val=val):
                ref[...] += val

    o_spec = pl.BlockSpec((bm, width), lambda j, i: (i, j))
    res = pl.pallas_call(
        body, name=name, grid=(ncol, lp // bm), in_specs=row_specs + par_specs + [o_spec] * nout,
        out_specs=[o_spec] * nr + par_specs,
        out_shape=[jax.ShapeDtypeStruct((lp, ncol * width), dt) for dt in d_row_dtypes]
        + [jax.ShapeDtypeStruct(p.shape, F32) for p, _ in params],
        compiler_params=pltpu.CompilerParams(dimension_semantics=("arbitrary", "arbitrary"), vmem_limit_bytes=VMEM_LIMIT),
    )(*[a for a, _ in rows], *[p for p, _ in params], *d_outs)
    return res[:nr], res[nr:]


def _silu(x):
    return x * jax.nn.sigmoid(x)


def _softplus(x):
    return jnp.maximum(x, 0.0) + jnp.log(1.0 + jnp.exp(-jnp.abs(x)))


def _real_rows(row_ids):
    return (row_ids >= PAD).astype(F32)


def _f_rmsnorm(row_ids, h, g):
    return (h * lax.rsqrt(jnp.mean(h * h, axis=-1, keepdims=True) + RMS_EPS) * g,)


def _f_small_gates(row_ids, small, bias, a_log):
    lane = _iota2(small.shape, 1)
    t = small + bias
    sp = _softplus(t)
    coef = -jnp.exp(a_log)
    keep = _real_rows(row_ids)
    first = jnp.where(lane < 8, jax.nn.sigmoid(t), jnp.where(lane < 16, coef * sp, jnp.where(lane < 48, sp, 0.0)))
    second = jnp.where((lane >= 8) & (lane < 48), coef * sp, 0.0)
    return first * keep, second * keep


def _f_gate_silu(row_ids, o, z):
    return (o * _silu(z),)


def _f_head_norm_gate(row_ids, o, z, g):
    out = []
    for h in range(o.shape[1] // HEAD_DIM):
        oh = o[:, h * HEAD_DIM:(h + 1) * HEAD_DIM]
        out.append(oh * lax.rsqrt(jnp.mean(oh * oh, axis=-1, keepdims=True) + RMS_EPS) * g)
    return (jnp.concatenate(out, axis=1) * _silu(z),)


def _f_ssm_out(row_ids, y, xh, z, d_skip, g):
    t = (y + d_skip * xh) * _silu(z)
    return (t * lax.rsqrt(jnp.mean(t * t, axis=-1, keepdims=True) + RMS_EPS) * g,)


def _f_merge(row_ids, pa, pb, pc, ga, gb, gc):
    return (jax.nn.sigmoid(ga) * pa + jax.nn.sigmoid(gb) * pb + jax.nn.sigmoid(gc) * pc,)


def _shift_rows(x, s):
    s = s % x.shape[0]
    return x if s == 0 else pltpu.roll(x, s, 0)


def _conv_pre(x, w, b):
    pre = b
    for kk in range(CONV_K):
        pre = pre + w[kk:kk + 1, :] * _shift_rows(x, CONV_K - 1 - kk)
    return pre


def _conv_post(pre, l2_flag, keep):
    act = _silu(pre)
    nrm = act * lax.rsqrt(jnp.sum(act * act, axis=-1, keepdims=True) + L2_EPS)
    return (l2_flag * nrm + (1.0 - l2_flag) * act) * keep


def conv_fwd(name, src, col_off, w, b, n_l2):
    lp, ch = src.shape[0], w.shape[1]

    def body(x_ref, w_ref, b_ref, o_ref):
        l2_flag = (pl.program_id(0) < n_l2).astype(F32)
        keep = _real_rows(_iota2((lp, 1), 0))
        o_ref[...] = _conv_post(_conv_pre(x_ref[...], w_ref[...], b_ref[...]), l2_flag, keep)

    return pl.pallas_call(
        body, name=name, grid=(ch // HEAD_DIM,),
        in_specs=[pl.BlockSpec((lp, HEAD_DIM), lambda j: (0, col_off + j)), pl.BlockSpec((CONV_K, HEAD_DIM), lambda j: (0, j)),
                  pl.BlockSpec((1, HEAD_DIM), lambda j: (0, j))],
        out_specs=pl.BlockSpec((lp, HEAD_DIM), lambda j: (0, j)),
        out_shape=jax.ShapeDtypeStruct((lp, ch), F32),
        compiler_params=pltpu.CompilerParams(dimension_semantics=("arbitrary",), vmem_limit_bytes=VMEM_LIMIT),
    )(src, w, b)


def conv_bwd(name, src, col_off, w, b, n_l2, d_out):
    lp, ch = src.shape[0], w.shape[1]

    def body(x_ref, w_ref, b_ref, do_ref, dx_ref, dw_ref, db_ref):
        l2_flag = (pl.program_id(0) < n_l2).astype(F32)
        keep = _real_rows(_iota2((lp, 1), 0))
        x, wv = x_ref[...], w_ref[...]
        pre = _conv_pre(x, wv, b_ref[...])
        _, pull = jax.vjp(lambda p: _conv_post(p, l2_flag, keep), pre)
        (d_pre,) = pull(do_ref[...])
        dx = jnp.zeros_like(x)
        for kk in range(CONV_K):
            s = CONV_K - 1 - kk
            dx = dx + wv[kk:kk + 1, :] * _shift_rows(d_pre, -s)
            dw_ref[kk:kk + 1, :] = jnp.sum(d_pre * _shift_rows(x, s), axis=0, keepdims=True)
        dx_ref[...] = (dx * keep).astype(dx_ref.dtype)
        db_ref[...] = jnp.sum(d_pre, axis=0, keepdims=True)

    seq = pl.BlockSpec((lp, HEAD_DIM), lambda j: (0, j))
    wspec = pl.BlockSpec((CONV_K, HEAD_DIM), lambda j: (0, j))
    bspec = pl.BlockSpec((1, HEAD_DIM), lambda j: (0, j))
    return pl.pallas_call(
        body, name=name, grid=(ch // HEAD_DIM,),
        in_specs=[pl.BlockSpec((lp, HEAD_DIM), lambda j: (0, col_off + j)), wspec, bspec, seq],
        out_specs=[seq, wspec, bspec],
        out_shape=[jax.ShapeDtypeStruct((lp, ch), BF16), jax.ShapeDtypeStruct(w.shape, F32), jax.ShapeDtypeStruct(b.shape, F32)],
        compiler_params=pltpu.CompilerParams(dimension_semantics=("arbitrary",), vmem_limit_bytes=VMEM_LIMIT),
    )(src, w, b, d_out)


def loss_head(h, target, g):
    lp, d = h.shape
    bm = SB_BLOCK
    first = (PAD + N_META) // bm

    def body(h_ref, t_ref, g_ref, loss_ref, dh_ref, dg_ref):
        i = pl.program_id(0)
        keep = (i >= first).astype(F32)

        def f(hv, gv):
            y = hv * lax.rsqrt(jnp.mean(hv * hv, axis=-1, keepdims=True) + RMS_EPS) * gv
            err = y - t_ref[...]
            return 0.5 * jnp.sum(jnp.mean(err * err, axis=-1, keepdims=True), axis=0, keepdims=True) * keep

        val, pull = jax.vjp(f, h_ref[...], g_ref[...])
        dh, dg = pull(jnp.ones((1, 1), F32))
        dh_ref[...] = dh

        @pl.when(i == 0)
        def _():
            loss_ref[...] = val
            dg_ref[...] = dg

        @pl.when(i > 0)
        def _():
            loss_ref[...] += val
            dg_ref[...] += dg

    row = pl.BlockSpec((bm, d), lambda i: (i, 0))
    return pl.pallas_call(
        body, name="loss_head", grid=(lp // bm,),
        in_specs=[row, pl.BlockSpec((bm, d), lambda i: (jnp.maximum(i - first, 0), 0)), pl.BlockSpec((1, d), lambda i: (0, 0))],
        out_specs=[pl.BlockSpec((1, 1), lambda i: (0, 0)), row, pl.BlockSpec((1, d), lambda i: (0, 0))],
        out_shape=[jax.ShapeDtypeStruct((1, 1), F32), jax.ShapeDtypeStruct((lp, d), F32), jax.ShapeDtypeStruct((1, d), F32)],
        compiler_params=pltpu.CompilerParams(dimension_semantics=("arbitrary",)),
    )(h, target, g)


ADAM_LR, ADAM_B1, ADAM_B2, ADAM_EPS, ADAM_WD, ADAM_STEP = 0.001, 0.9, 0.999, 1e-08, 0.01, 10


def adamw(name, w, g, m, v):
    lead = w.shape[:-2]
    rows, cols = w.shape[-2:]
    br = _pick(rows, 128, 8)

    def body(w_ref, g_ref, m_ref, v_ref, d_ref, nm_ref, nv_ref):
        gv = g_ref[...]
        nm = ADAM_B1 * m_ref[...] + (1.0 - ADAM_B1) * gv
        nv = ADAM_B2 * v_ref[...] + (1.0 - ADAM_B2) * (gv * gv)
        m_hat = nm / (1.0 - ADAM_B1 ** ADAM_STEP)
        v_hat = nv / (1.0 - ADAM_B2 ** ADAM_STEP)
        d_ref[...] = -ADAM_LR * (m_hat / (jnp.sqrt(v_hat) + ADAM_EPS) + ADAM_WD * w_ref[...])
        nm_ref[...] = nm
        nv_ref[...] = nv

    if lead:
        spec = pl.BlockSpec((None, br, cols), lambda s, i: (s, i, 0))
        grid = (lead[0], rows // br)
    else:
        spec = pl.BlockSpec((br, cols), lambda i: (i, 0))
        grid = (rows // br,)
    return pl.pallas_call(
        body, name=name, grid=grid, in_specs=[spec] * 4, out_specs=[spec] * 3,
        out_shape=[jax.ShapeDtypeStruct(w.shape, F32)] * 3,
        compiler_params=pltpu.CompilerParams(dimension_semantics=("arbitrary",) * len(grid), vmem_limit_bytes=VMEM_LIMIT),
    )(w, g, m, v)


MESH = pl.DeviceIdType.MESH
ANY = pl.BlockSpec(memory_space=pl.ANY)
D2D_PIECES = 16
ICI_PIECES = 4


def _place():
    x, y, c = lax.axis_index("x"), lax.axis_index("y"), lax.axis_index("c")
    return x, y, c, [(1 - x, y), (x, 1 - y), (1 - x, 1 - y)]


def _pieces(rows, n, unit):
    per = -(-rows // (n * unit)) * unit
    return [(s, min(per, rows - s)) for s in range(0, rows, per)]


def _row_unit(dtype):
    return 16 if dtype == BF16 else 8


def _scalar(v):
    return jnp.reshape(v, (1,)).astype(jnp.int32)


def place_shard(name, pack):
    rows, cols = pack.shape
    br = _pick(rows, 256, 16)

    def body(m_ref, p_ref, o_ref):
        o_ref[...] = p_ref[...]

    return pl.pallas_call(
        body, name=name,
        grid_spec=pltpu.PrefetchScalarGridSpec(
            num_scalar_prefetch=1, grid=(rows // br,),
            in_specs=[pl.BlockSpec((br, cols), lambda i, m: (i, 0))],
            out_specs=pl.BlockSpec((None, br, cols), lambda i, m: (m[0], i, 0))),
        out_shape=jax.ShapeDtypeStruct((4, rows, cols), pack.dtype),
        compiler_params=pltpu.CompilerParams(dimension_semantics=("arbitrary",), vmem_limit_bytes=VMEM_LIMIT),
    )(_scalar(2 * lax.axis_index("x") + lax.axis_index("y")), pack)


class Ride:
    def __init__(self, ins, out_shapes, alias, sems, start, finish):
        self.ins, self.out_shapes, self.alias, self.sems, self.start, self.finish = list(ins), out_shapes, alias, sems, start, finish


def _gather_parts(o_refs, send_sems, recv_sems):
    x, y, c, chips = _place()
    mine = 2 * x + y

    def half_rows(b, which, start=0, size=None):
        half = o_refs[b].shape[1] // 2
        return pl.ds(pl.multiple_of(which * half + start, _row_unit(o_refs[b].dtype)), half if size is None else size)

    def remote(b, k, slot, rws, to):
        piece = o_refs[b].at[slot, rws, :]
        return pltpu.make_async_remote_copy(src_ref=piece, dst_ref=piece, send_sem=send_sems.at[b, k], recv_sem=recv_sems.at[b, k],
                                            device_id=to, device_id_type=MESH)

    return x, y, c, chips, mine, half_rows, remote


def _gather_start(o_refs, send_sems, recv_sems):
    x, y, c, chips, mine, half_rows, remote = _gather_parts(o_refs, send_sems, recv_sems)
    for b, o_ref in enumerate(o_refs):
        for j, (cx, cy) in enumerate(chips):
            for start, size in _pieces(o_ref.shape[1] // 2, ICI_PIECES, _row_unit(o_ref.dtype)):
                remote(b, j, mine, half_rows(b, c, start, size), (cx, cy, c)).start()


def _gather_finish(o_refs, send_sems, recv_sems):
    x, y, c, chips, mine, half_rows, remote = _gather_parts(o_refs, send_sems, recv_sems)
    sends = []
    for b, o_ref in enumerate(o_refs):
        for j, (cx, cy) in enumerate(chips):
            slot = 2 * cx + cy
            sends.append(remote(b, j, mine, half_rows(b, c), (cx, cy, c)))
            remote(b, j, slot, half_rows(b, c), (cx, cy, c)).wait_recv()
            for start, size in _pieces(o_ref.shape[1] // 2, D2D_PIECES, _row_unit(o_ref.dtype)):
                remote(b, 3 + j, slot, half_rows(b, c, start, size), (x, y, 1 - c)).start()
            sends.append(remote(b, 3 + j, slot, half_rows(b, c), (x, y, 1 - c)))
    for b in range(len(o_refs)):
        for j, (cx, cy) in enumerate(chips):
            remote(b, 3 + j, 2 * cx + cy, half_rows(b, 1 - c), (x, y, 1 - c)).wait_recv()
    for cp in sends:
        cp.wait_send()


def gather_ride(placed):
    n = len(placed)
    return Ride(placed, [jax.ShapeDtypeStruct(p.shape, p.dtype) for p in placed], True,
                [pltpu.SemaphoreType.DMA((n, 6)), pltpu.SemaphoreType.DMA((n, 6))],
                lambda ins, outs, sems: _gather_start(outs, *sems), lambda ins, outs, sems: _gather_finish(outs, *sems))


def gather_shards(name, placed):
    n = len(placed)

    def body(*refs):
        o_refs, sems = refs[n:2 * n], refs[2 * n:]
        _gather_start(o_refs, *sems)
        _gather_finish(o_refs, *sems)

    return pl.pallas_call(
        body, name=name, in_specs=[ANY] * n, out_specs=[ANY] * n,
        out_shape=[jax.ShapeDtypeStruct(p.shape, p.dtype) for p in placed],
        input_output_aliases={i: i for i in range(n)},
        scratch_shapes=[pltpu.SemaphoreType.DMA((n, 6)), pltpu.SemaphoreType.DMA((n, 6))],
    )(*placed)


def pair_split(name, bufs):
    n = len(bufs)

    def body(*refs):
        g_refs, t_refs = refs[:n], refs[n:2 * n]
        send_sems, recv_sems = refs[2 * n:]
        x, y, c, _ = _place()
        waits = []
        for b, (g_ref, t_ref) in enumerate(zip(g_refs, t_refs)):
            half = g_ref.shape[2]
            unit = _row_unit(g_ref.dtype)

            def copy(slots, start, size):
                rws = pl.ds(start, size)
                return pltpu.make_async_remote_copy(src_ref=g_ref.at[slots, 1 - c, rws, :], dst_ref=t_ref.at[slots, rws, :],
                                                    send_sem=send_sems.at[b], recv_sem=recv_sems.at[b], device_id=(x, y, 1 - c),
                                                    device_id_type=MESH)

            for s in range(4):
                for start, size in _pieces(half, D2D_PIECES // 4, unit):
                    copy(s, start, size).start()
            waits.append(copy(slice(None), 0, half))
        for cp in waits:
            cp.wait()

    return pl.pallas_call(
        body, name=name, in_specs=[ANY] * n, out_specs=[ANY] * n,
        out_shape=[jax.ShapeDtypeStruct((4,) + g.shape[2:], g.dtype) for g in bufs],
        scratch_shapes=[pltpu.SemaphoreType.DMA((n,)), pltpu.SemaphoreType.DMA((n,))],
    )(*bufs)


def pair_add(name, g, theirs, transit):
    _, _, half, cols = g.shape
    br = _pick(half, 128, 16)

    def body(c_ref, g_ref, t_ref, o_ref):
        o_ref[...] = (g_ref[...].astype(F32) + t_ref[...].astype(F32)).astype(transit)

    blk = (4, br, cols)
    return pl.pallas_call(
        body, name=name,
        grid_spec=pltpu.PrefetchScalarGridSpec(
            num_scalar_prefetch=1, grid=(half // br,),
            in_specs=[pl.BlockSpec((4, None, br, cols), lambda i, c: (0, c[0], i, 0)), pl.BlockSpec(blk, lambda i, c: (0, i, 0))],
            out_specs=pl.BlockSpec(blk, lambda i, c: (0, i, 0))),
        out_shape=jax.ShapeDtypeStruct((4, half, cols), transit),
        compiler_params=pltpu.CompilerParams(dimension_semantics=("arbitrary",), vmem_limit_bytes=VMEM_LIMIT),
    )(_scalar(lax.axis_index("c")), g, theirs)


def _exchange_copies(a_refs, o_refs, send_sems, recv_sems, start):
    x, y, c, chips = _place()
    mine = 2 * x + y
    waits = []
    for b, (a_ref, o_ref) in enumerate(zip(a_refs, o_refs)):
        rows = a_ref.shape[1]
        for j, (cx, cy) in enumerate(chips):
            def copy(first, size):
                rws = pl.ds(first, size)
                return pltpu.make_async_remote_copy(src_ref=a_ref.at[2 * cx + cy, rws, :], dst_ref=o_ref.at[mine, rws, :],
                                                    send_sem=send_sems.at[b, j], recv_sem=recv_sems.at[b, j],
                                                    device_id=(cx, cy, c), device_id_type=MESH)
            if start:
                for first, size in _pieces(rows, ICI_PIECES, _row_unit(a_ref.dtype)):
                    copy(first, size).start()
            else:
                waits.append(copy(0, rows))
    return waits


def _exchange_finish(a_refs, o_refs, send_sems, recv_sems):
    for cp in _exchange_copies(a_refs, o_refs, send_sems, recv_sems, False):
        cp.wait()


def exchange_ride(parts):
    n = len(parts)
    return Ride(parts, [jax.ShapeDtypeStruct(a.shape, a.dtype) for a in parts], False,
                [pltpu.SemaphoreType.DMA((n, 3)), pltpu.SemaphoreType.DMA((n, 3))],
                lambda ins, outs, sems: _exchange_copies(ins, outs, *sems, True), lambda ins, outs, sems: _exchange_finish(ins, outs, *sems))


def chip_exchange(name, parts):
    n = len(parts)

    def body(*refs):
        a_refs, o_refs, sems = refs[:n], refs[n:2 * n], refs[2 * n:]
        _exchange_copies(a_refs, o_refs, *sems, True)
        _exchange_finish(a_refs, o_refs, *sems)

    return pl.pallas_call(
        body, name=name, in_specs=[ANY] * n, out_specs=[ANY] * n,
        out_shape=[jax.ShapeDtypeStruct(a.shape, a.dtype) for a in parts],
        scratch_shapes=[pltpu.SemaphoreType.DMA((n, 3)), pltpu.SemaphoreType.DMA((n, 3))],
    )(*parts)


def chip_add(name, got, part):
    _, rows, cols = got.shape
    br = _pick(rows, 128, 16)
    nblk = rows // br

    def body(m_ref, c_ref, got_ref, part_ref, o_ref):
        mine = m_ref[0]
        for s in range(4):
            @pl.when(mine == s)
            def _(s=s):
                val = part_ref[...].astype(F32)
                o_ref[...] = val if s == 0 else o_ref[...] + val

            @pl.when(mine != s)
            def _(s=s):
                val = got_ref[s].astype(F32)
                o_ref[...] = val if s == 0 else o_ref[...] + val

    return pl.pallas_call(
        body, name=name,
        grid_spec=pltpu.PrefetchScalarGridSpec(
            num_scalar_prefetch=2, grid=(nblk,),
            in_specs=[pl.BlockSpec((4, br, cols), lambda i, m, c: (0, i, 0)),
                      pl.BlockSpec((None, br, cols), lambda i, m, c: (m[0], i, 0))],
            out_specs=pl.BlockSpec((br, cols), lambda i, m, c: (c[0] * nblk + i, 0))),
        out_shape=jax.ShapeDtypeStruct((2 * rows, cols), F32),
        compiler_params=pltpu.CompilerParams(dimension_semantics=("arbitrary",), vmem_limit_bytes=VMEM_LIMIT),
    )(_scalar(2 * lax.axis_index("x") + lax.axis_index("y")), _scalar(lax.axis_index("c")), got, part)


def pair_join(name, fulls):
    n = len(fulls)

    def body(*refs):
        o_refs = refs[n:2 * n]
        send_sems, recv_sems = refs[2 * n:]
        x, y, c, _ = _place()
        waits = []
        for b, o_ref in enumerate(o_refs):
            half = o_ref.shape[0] // 2
            unit = _row_unit(o_ref.dtype)

            def copy(start, size):
                piece = o_ref.at[pl.ds(pl.multiple_of(c * half + start, unit), size), :]
                return pltpu.make_async_remote_copy(src_ref=piece, dst_ref=piece, send_sem=send_sems.at[b], recv_sem=recv_sems.at[b],
                                                    device_id=(x, y, 1 - c), device_id_type=MESH)

            for start, size in _pieces(half, D2D_PIECES, unit):
                copy(start, size).start()
            waits.append(copy(0, half))
        for cp in waits:
            cp.wait()

    return pl.pallas_call(
        body, name=name, in_specs=[ANY] * n, out_specs=[ANY] * n,
        out_shape=[jax.ShapeDtypeStruct(f.shape, f.dtype) for f in fulls],
        input_output_aliases={i: i for i in range(n)},
        scratch_shapes=[pltpu.SemaphoreType.DMA((n,)), pltpu.SemaphoreType.DMA((n,))],
    )(*fulls)


D_IN = 15920
D_PROJ = 16000
_SEGMENTS = ((0, 8192), (8208, 12816), (12848, 15920), (8192, 8208), (12816, 12848))
OFF_SB_Z, OFF_GDN_QKV, OFF_GDN_Z, OFF_SSM_Z, OFF_SSM_XBC, OFF_GATES, OFF_SMALL = 3072, 4096, 7168, 8192, 10240, 12800, 15872
PACK_C = 1024
WEIGHTS = ("meta_tokens", "norm_g", "w_in", "gdn_conv_w", "gdn_a_log", "gdn_dt_bias", "gdn_norm_g", "ssm_conv_w", "ssm_conv_b",
           "ssm_a_log", "ssm_dt_bias", "ssm_d", "ssm_norm_g", "w_branch_a", "w_branch_b", "w_branch_c", "w_out", "final_norm_g")
SHARDED = ("w_in", "w_branch_a", "w_branch_b", "w_branch_c", "w_out", "gdn_conv_w", "ssm_conv_w", "meta_tokens")
SHARD_AXIS = {"w_in": 2, "w_branch_a": 1, "w_branch_b": 1, "w_branch_c": 1, "w_out": 1, "gdn_conv_w": 2, "ssm_conv_w": 2, "meta_tokens": 1}
BRANCH = ("w_branch_a", "w_branch_b", "w_branch_c", "w_out")
EXACT = ("gdn_conv_w", "ssm_conv_w", "meta_tokens")
REPLICATED = tuple(n for n in WEIGHTS if n not in SHARDED)


def _regrouped_from_shards(shard_cols):
    out = []
    for a, b in _SEGMENTS:
        while a < b:
            chip = a // shard_cols
            stop = min(b, (chip + 1) * shard_cols)
            out.append((chip, a - chip * shard_cols, stop - chip * shard_cols))
            a = stop
    return out


def _shard_from_regrouped(chip, shard_cols):
    lo, hi = chip * shard_cols, (chip + 1) * shard_cols
    out, pos = [], 0
    starts = {}
    for a, b in _SEGMENTS:
        starts[(a, b)] = pos
        pos += b - a
    for a, b in sorted(_SEGMENTS):
        s0, s1 = max(a, lo), min(b, hi)
        if s0 < s1:
            out.append((starts[(a, b)] + s0 - a, starts[(a, b)] + s1 - a))
    return out


def _pack(parts, row_unit=64):
    n = sum(p.shape[0] for p in parts)
    rows = -(-n // (PACK_C * row_unit)) * row_unit
    flat = jnp.concatenate(list(parts) + [jnp.zeros((rows * PACK_C - n,), parts[0].dtype)])
    return flat.reshape(rows, PACK_C)


def _unpack(buf, shapes):
    flat, out, pos = buf.reshape(-1), [], 0
    for shp in shapes:
        n = math.prod(shp)
        out.append(flat[pos:pos + n].reshape(shp))
        pos += n
    return out


def _as_bf16_words(a):
    return lax.bitcast_convert_type(a, BF16).reshape(-1)


BRANCH_ROWS = (D_MODEL // 4, D_MODEL // 4, SSM_INNER // 4, D_MODEL // 4)


def _place_weights(w):
    depth = w["w_in"].shape[0]
    layers = []
    for l in range(depth):
        a = w["w_in"][l].astype(BF16)
        b = jnp.concatenate([w[n][l] for n in BRANCH], axis=0).astype(BF16)
        layers.append([place_shard("place_w_in", a), place_shard("place_branch", b)])
    small = place_shard("place_exact", _pack([_as_bf16_words(w[n]) for n in EXACT]))
    return layers, small


def _exact_weights(w, got_s):
    per_chip = [_unpack(got_s[c], [w[n].shape + (2,) for n in EXACT]) for c in range(4)]
    return {n: jnp.concatenate([lax.bitcast_convert_type(per_chip[c][i], F32) for c in range(4)], axis=SHARD_AXIS[n])
            for i, n in enumerate(EXACT)}


def _layer_weights(got_a, got_b):
    d_model, shard_cols = got_a.shape[1:]
    pad = jnp.zeros((d_model, D_PROJ - D_IN), BF16)
    out = {"wp": jnp.concatenate([got_a[c, :, lo:hi] for c, lo, hi in _regrouped_from_shards(shard_cols)] + [pad], axis=1)}
    pos = 0
    for n, rows in zip(BRANCH, BRANCH_ROWS):
        out[n] = jnp.concatenate([got_b[c, pos:pos + rows] for c in range(4)], axis=0)
        pos += rows
    return out


def _shard(a, axis, s):
    size = a.shape[axis] // 4
    return lax.slice_in_dim(a, s * size, (s + 1) * size, axis=axis)


def _layer_grad_buffers(g, shard_cols):
    buf_a = jnp.stack([jnp.concatenate([g["w_in"][:, lo:hi] for lo, hi in _shard_from_regrouped(s, shard_cols)], axis=1)
                       for s in range(4)]).astype(BF16)
    buf_b = jnp.stack([jnp.concatenate([_shard(g[n], 0, s) for n in BRANCH], axis=0) for s in range(4)]).astype(BF16)
    return [buf_a, buf_b]


def _split_halves(buf):
    return buf.reshape(4, 2, buf.shape[1] // 2, buf.shape[2])


def _start_reduce(tag, bufs, transits):
    bufs = [_split_halves(g) for g in bufs]
    theirs = pair_split(tag + "_pair_split", bufs)
    return [pair_add(f"{tag}_pair_add_{i}", g, t, tr) for i, (g, t, tr) in enumerate(zip(bufs, theirs, transits))]


def _layer_params(w, exact, weights, l):
    lane = lambda v, lo: jnp.pad(v, (lo, HEAD_DIM - lo - v.shape[0]))[None]
    return dict(
        norm_g=w["norm_g"][l][None], wp=weights["wp"],
        gdn_conv_w=exact["gdn_conv_w"][l], gdn_conv_b=jnp.zeros((1, 3 * N_HEADS * HEAD_DIM), F32),
        ssm_conv_w=exact["ssm_conv_w"][l], ssm_conv_b=w["ssm_conv_b"][l][None],
        bias_vec=lane(w["gdn_dt_bias"][l], 8) + lane(w["ssm_dt_bias"][l], 16),
        alog_vec=lane(w["gdn_a_log"][l], 8) + lane(w["ssm_a_log"][l], 16),
        gdn_norm_g=w["gdn_norm_g"][l][None], d_skip=jnp.repeat(w["ssm_d"][l], SSM_P)[None], ssm_norm_g=w["ssm_norm_g"][l][None],
        wa=weights["w_branch_a"], wb=weights["w_branch_b"], wc=weights["w_branch_c"], wo=weights["w_out"])


def _layer_fwd(h, p, ride):
    lp = h.shape[0]
    bm = _pick(lp, 272, 8)
    kw = dict(bm=bm, lp=lp)
    (u,) = rowmap_fwd("rms_fwd", _f_rmsnorm, [(h, 0)], [(p["norm_g"], False)], 1, width=D_MODEL, ncol=1, out_dtypes=[BF16], **kw)
    proj = matmul(u, p["wp"], "nn", name="proj", bm=lp, bn=640)
    o_a_raw, csum, *rode = sb_fwd(proj, (0, N_HEADS, 2 * N_HEADS), ride)
    qkv = conv_fwd("gdn_conv_fwd", proj, OFF_GDN_QKV // HEAD_DIM, p["gdn_conv_w"], p["gdn_conv_b"], 2 * N_HEADS)
    first, second = rowmap_fwd("gates_fwd", _f_small_gates, [(proj, OFF_SMALL // HEAD_DIM)],
                               [(p["bias_vec"], False), (p["alog_vec"], False)], 2, width=HEAD_DIM, ncol=1, **kw)
    o_b_raw, gdn_states = gdn_fwd(qkv, first)
    xbc = conv_fwd("ssm_conv_fwd", proj, OFF_SSM_XBC // HEAD_DIM, p["ssm_conv_w"], p["ssm_conv_b"], 0)
    y_raw, ssd_states = ssd_fwd(xbc, first, second)
    (o_a,) = rowmap_fwd("gate_a_fwd", _f_gate_silu, [(o_a_raw, 0), (proj, OFF_SB_Z // 1024)], [], 1, width=1024, ncol=1,
                        out_dtypes=[BF16], **kw)
    (o_b,) = rowmap_fwd("gate_b_fwd", _f_head_norm_gate, [(o_b_raw, 0), (proj, OFF_GDN_Z // 1024)], [(p["gdn_norm_g"], False)], 1,
                        width=1024, ncol=1, out_dtypes=[BF16], **kw)
    (o_c,) = rowmap_fwd("gate_c_fwd", _f_ssm_out, [(y_raw, 0), (xbc, 0), (proj, OFF_SSM_Z // 1024)],
                        [(p["d_skip"], True), (p["ssm_norm_g"], True)], 1, width=1024, ncol=SSM_GROUPS, out_dtypes=[BF16], **kw)
    pa = matmul(o_a, p["wa"], "nn", name="branch_a", bm=lp // 2, bn=512)
    pb = matmul(o_b, p["wb"], "nn", name="branch_b", bm=lp // 2, bn=512)
    pc = matmul(o_c, p["wc"], "nn", name="branch_c", bm=lp // 2, bn=512)
    merge_rows = [(pa, 0), (pb, 0), (pc, 0)] + [(proj, OFF_GATES // 512 + 2 * i) for i in range(3)]
    (merged,) = rowmap_fwd("merge_fwd", _f_merge, merge_rows, [], 1, width=512, ncol=2, out_dtypes=[BF16], **kw)
    h_out = matmul(merged, p["wo"], "nn", name="out_proj", bm=lp, bn=512, residual=h)
    saved = dict(h=h, u=u, proj=proj, csum=csum, qkv=qkv, first=first, second=second, o_a_raw=o_a_raw, o_b_raw=o_b_raw,
                 gdn_states=gdn_states, xbc=xbc, y_raw=y_raw, ssd_states=ssd_states, o_a=o_a, o_b=o_b, o_c=o_c, pa=pa, pb=pb, pc=pc,
                 merged=merged)
    return h_out, saved, rode


def _layer_bwd(d_h, p, s, ride):
    lp = d_h.shape[0]
    bm = _pick(lp, 272, 8)
    kw = dict(bm=bm, lp=lp)
    proj = s["proj"]
    g = {}
    d_merged = matmul(d_h, p["wo"], "nt", name="d_merged", bm=lp, bn=512)
    g["w_out"] = matmul(s["merged"], d_h, "tn", name="g_w_out", bm=512, bn=1024, bk=lp)
    merge_rows = [(s["pa"], 0), (s["pb"], 0), (s["pc"], 0)] + [(proj, OFF_GATES // 512 + 2 * i) for i in range(3)]
    (d_pa, d_pb, d_pc, d_ga, d_gb, d_gc), _ = rowmap_bwd("merge_bwd", _f_merge, merge_rows, [], [d_merged], width=512, ncol=2,
                                                         d_row_dtypes=[BF16] * 6, **kw)
    g["w_branch_a"] = matmul(s["o_a"], d_pa, "tn", name="g_w_a", bm=512, bn=1024, bk=lp)
    g["w_branch_b"] = matmul(s["o_b"], d_pb, "tn", name="g_w_b", bm=512, bn=1024, bk=lp)
    g["w_branch_c"] = matmul(s["o_c"], d_pc, "tn", name="g_w_c", bm=512, bn=1024, bk=lp)
    d_oa = matmul(d_pa, p["wa"], "nt", name="d_o_a", bm=lp, bn=512)
    d_ob = matmul(d_pb, p["wb"], "nt", name="d_o_b", bm=lp, bn=512)
    d_oc = matmul(d_pc, p["wc"], "nt", name="d_o_c", bm=lp, bn=512)
    (d_oa_raw, d_sbz), _ = rowmap_bwd("gate_a_bwd", _f_gate_silu, [(s["o_a_raw"], 0), (proj, OFF_SB_Z // 1024)], [], [d_oa],
                                      width=1024, ncol=1, d_row_dtypes=[F32, BF16], **kw)
    (d_ob_raw, d_gdz), (g["gdn_norm_g"],) = rowmap_bwd(
        "gate_b_bwd", _f_head_norm_gate, [(s["o_b_raw"], 0), (proj, OFF_GDN_Z // 1024)], [(p["gdn_norm_g"], False)], [d_ob],
        width=1024, ncol=1, d_row_dtypes=[F32, BF16], **kw)
    (d_y, d_xh, d_ssz), (g_dskip, g["ssm_norm_g"]) = rowmap_bwd(
        "gate_c_bwd", _f_ssm_out, [(s["y_raw"], 0), (s["xbc"], 0), (proj, OFF_SSM_Z // 1024)],
        [(p["d_skip"], True), (p["ssm_norm_g"], True)], [d_oc], width=1024, ncol=SSM_GROUPS, d_row_dtypes=[F32, F32, BF16], **kw)
    g["gdn_norm_g"], g["ssm_norm_g"] = g["gdn_norm_g"][0], g["ssm_norm_g"][0]
    g["ssm_d"] = g_dskip.reshape(SSM_HEADS, SSM_P).sum(axis=1)
    d_q, d_k, d_v, *rode = sb_bwd(proj, (0, N_HEADS, 2 * N_HEADS), s["csum"], d_oa_raw, ride)
    d_qkv, d_first_gdn = gdn_bwd(s["qkv"], s["first"], s["gdn_states"], d_ob_raw)
    d_xbc_out, d_first_ssd, d_second = ssd_bwd(s["xbc"], s["first"], s["second"], s["ssd_states"], d_y, d_xh)
    d_gdqkv, g["gdn_conv_w"], _ = conv_bwd("gdn_conv_bwd", proj, OFF_GDN_QKV // HEAD_DIM, p["gdn_conv_w"], p["gdn_conv_b"], 2 * N_HEADS,
                                           d_qkv)
    d_xbc, g["ssm_conv_w"], g_cb = conv_bwd("ssm_conv_bwd", proj, OFF_SSM_XBC // HEAD_DIM, p["ssm_conv_w"], p["ssm_conv_b"], 0, d_xbc_out)
    g["ssm_conv_b"] = g_cb[0]
    d_first = d_first_gdn + d_first_ssd
    (d_small,), (g_bias, g_alog) = rowmap_bwd("gates_bwd", _f_small_gates, [(proj, OFF_SMALL // HEAD_DIM)],
                                              [(p["bias_vec"], False), (p["alog_vec"], False)], [d_first, d_second],
                                              width=HEAD_DIM, ncol=1, d_row_dtypes=[BF16], **kw)
    g["gdn_dt_bias"], g["ssm_dt_bias"] = g_bias[0, 8:16], g_bias[0, 16:48]
    g["gdn_a_log"], g["ssm_a_log"] = g_alog[0, 8:16], g_alog[0, 16:48]
    d_proj = jnp.concatenate([d_q, d_k, d_v, d_sbz, d_gdqkv, d_gdz, d_ssz, d_xbc, d_ga, d_gb, d_gc, d_small], axis=1)
    g["w_in"] = matmul(s["u"], d_proj, "tn", name="g_w_in", bm=1024, bn=640, bk=lp, out_dtype=BF16)
    d_u = matmul(d_proj, p["wp"], "nt", name="d_u", bm=lp // 2, bn=1024, bk=1600)
    (d_hn,), (g_norm,) = rowmap_bwd("rms_bwd", _f_rmsnorm, [(s["h"], 0)], [(p["norm_g"], False)], [d_u], width=D_MODEL, ncol=1, **kw)
    g["norm_g"] = g_norm[0]
    return d_h + d_hn, g, rode


def kernel(x, meta_tokens, norm_g, w_in, gdn_conv_w, gdn_a_log, gdn_dt_bias, gdn_norm_g, ssm_conv_w, ssm_conv_b, ssm_a_log, ssm_dt_bias, ssm_d, ssm_norm_g, w_branch_a, w_branch_b, w_branch_c, w_out, final_norm_g, loss_target, m_meta_tokens, m_norm_g, m_w_in, m_gdn_conv_w, m_gdn_a_log, m_gdn_dt_bias, m_gdn_norm_g, m_ssm_conv_w, m_ssm_conv_b, m_ssm_a_log, m_ssm_dt_bias, m_ssm_d, m_ssm_norm_g, m_w_branch_a, m_w_branch_b, m_w_branch_c, m_w_out, m_final_norm_g, v_meta_tokens, v_norm_g, v_w_in, v_gdn_conv_w, v_gdn_a_log, v_gdn_dt_bias, v_gdn_norm_g, v_ssm_conv_w, v_ssm_conv_b, v_ssm_a_log, v_ssm_dt_bias, v_ssm_d, v_ssm_norm_g, v_w_branch_a, v_w_branch_b, v_w_branch_c, v_w_out, v_final_norm_g):
    w = dict(meta_tokens=meta_tokens, norm_g=norm_g, w_in=w_in, gdn_conv_w=gdn_conv_w, gdn_a_log=gdn_a_log, gdn_dt_bias=gdn_dt_bias,
             gdn_norm_g=gdn_norm_g, ssm_conv_w=ssm_conv_w, ssm_conv_b=ssm_conv_b, ssm_a_log=ssm_a_log, ssm_dt_bias=ssm_dt_bias,
             ssm_d=ssm_d, ssm_norm_g=ssm_norm_g, w_branch_a=w_branch_a, w_branch_b=w_branch_b, w_branch_c=w_branch_c, w_out=w_out,
             final_norm_g=final_norm_g)
    m = dict(meta_tokens=m_meta_tokens, norm_g=m_norm_g, w_in=m_w_in, gdn_conv_w=m_gdn_conv_w, gdn_a_log=m_gdn_a_log,
             gdn_dt_bias=m_gdn_dt_bias, gdn_norm_g=m_gdn_norm_g, ssm_conv_w=m_ssm_conv_w, ssm_conv_b=m_ssm_conv_b,
             ssm_a_log=m_ssm_a_log, ssm_dt_bias=m_ssm_dt_bias, ssm_d=m_ssm_d, ssm_norm_g=m_ssm_norm_g, w_branch_a=m_w_branch_a,
             w_branch_b=m_w_branch_b, w_branch_c=m_w_branch_c, w_out=m_w_out, final_norm_g=m_final_norm_g)
    v = dict(meta_tokens=v_meta_tokens, norm_g=v_norm_g, w_in=v_w_in, gdn_conv_w=v_gdn_conv_w, gdn_a_log=v_gdn_a_log,
             gdn_dt_bias=v_gdn_dt_bias, gdn_norm_g=v_gdn_norm_g, ssm_conv_w=v_ssm_conv_w, ssm_conv_b=v_ssm_conv_b,
             ssm_a_log=v_ssm_a_log, ssm_dt_bias=v_ssm_dt_bias, ssm_d=v_ssm_d, ssm_norm_g=v_ssm_norm_g, w_branch_a=v_w_branch_a,
             w_branch_b=v_w_branch_b, w_branch_c=v_w_branch_c, w_out=v_w_out, final_norm_g=v_final_norm_g)
    depth = norm_g.shape[0]
    shard_cols = w_in.shape[-1]
    placed, placed_small = _place_weights(w)
    got_a, got_b, got_s = gather_shards("gather_first", placed[0] + [placed_small])
    exact = _exact_weights(w, got_s)

    h = jnp.concatenate([jnp.zeros((PAD, D_MODEL), F32), exact["meta_tokens"], x[0]], axis=0)
    params, saved = [], []
    for l in range(depth):
        params.append(_layer_params(w, exact, _layer_weights(got_a, got_b), l))
        h, s, rode = _layer_fwd(h, params[l], gather_ride(placed[l + 1]) if l + 1 < depth else None)
        saved.append(s)
        if rode:
            got_a, got_b = rode
    loss, d_h, g_final = loss_head(h, loss_target[0], final_norm_g[None])

    layer_grads, reds, waiting = [None] * depth, [None] * depth, None
    for l in reversed(range(depth)):
        d_h, layer_grads[l], rode = _layer_bwd(d_h, params[l], saved[l], exchange_ride(waiting) if waiting else None)
        if waiting:
            reds[l + 1] = [chip_add(f"grads_chip_add_{i}", gt, p) for i, (gt, p) in enumerate(zip(rode, waiting))]
        waiting = _start_reduce("grads", _layer_grad_buffers(layer_grads[l], shard_cols), [BF16, BF16])
    grads = {n: jnp.stack([layer_grads[l][n] for l in range(depth)]) for n in WEIGHTS
             if n not in ("meta_tokens", "final_norm_g", "w_in") + BRANCH}
    grads["meta_tokens"] = d_h[PAD:PAD + N_META]
    grads["final_norm_g"] = g_final[0]
    grad_x = d_h[PAD + N_META:][None]
    buf_s = jnp.stack([_pack([_shard(grads[n], SHARD_AXIS[n], s).astype(BF16).reshape(-1) for n in EXACT], row_unit=32) for s in range(4)])
    small = _pack([grads[n].reshape(-1) for n in REPLICATED], row_unit=32)
    last = waiting + _start_reduce("small_grads", [buf_s, jnp.broadcast_to(small[None], (4,) + small.shape)], [BF16, F32])
    got = chip_exchange("grads_chip_exchange", last)
    sums = [chip_add(f"grads_chip_add_{i}", gt, p) for i, (gt, p) in enumerate(zip(got, last))]
    reds[0] = sums[:2]
    joined = pair_join("grads_pair_join", [r for layer in reds for r in layer] + sums[2:])
    red = {"w_in": jnp.stack(joined[0:2 * depth:2])}
    pos = 0
    for n, rows in zip(BRANCH, BRANCH_ROWS):
        red[n] = jnp.stack([joined[2 * l + 1][pos:pos + rows] for l in range(depth)])
        pos += rows
    red.update(zip(EXACT, _unpack(joined[-2], [w[n].shape for n in EXACT])))
    small_red = joined[-1]
    delta, new_m, new_v = {}, {}, {}
    for n in SHARDED:
        delta[n], new_m[n], new_v[n] = adamw("adamw_" + n, w[n], red[n], m[n], v[n])
    pack_small = lambda d: _pack([d[n].reshape(-1) for n in REPLICATED], row_unit=32)
    small = adamw("adamw_small", pack_small(w), small_red, pack_small(m), pack_small(v))
    shapes = [w[n].shape for n in REPLICATED]
    red.update(zip(REPLICATED, _unpack(small_red, shapes)))
    for d, buf in zip((delta, new_m, new_v), small):
        d.update(zip(REPLICATED, _unpack(buf, shapes)))
    total_loss = lax.psum(loss[0, 0], ("x", "y", "c"))
    return (total_loss, grad_x, *[red[n] for n in WEIGHTS], *[delta[n] for n in WEIGHTS], *[new_m[n] for n in WEIGHTS],
            *[new_v[n] for n in WEIGHTS])
```

```python
import functools
import math

import jax
import jax.numpy as jnp
from jax import lax
from jax.experimental import pallas as pl
from jax.experimental.pallas import tpu as pltpu

F32 = jnp.float32
BF16 = jnp.bfloat16

N_META = 16
RMS_EPS = 1e-6
L2_EPS = 1e-6
CONV_K = 4
D_MODEL = 1024
HEAD_DIM = 128
N_HEADS = 8
CHUNK = 64
SB_BLOCK = 128
PAD = SB_BLOCK - N_META
SSM_INNER = 2048
SSM_P = 64
SSM_HEADS = 32
SSM_GROUPS = 2
SSM_HG = SSM_HEADS // SSM_GROUPS
SSM_N = 128
VMEM_LIMIT = 56 * 1024 * 1024

def _dims(mode, ndim):
    lhs, rhs = {"nn": (1, 0), "nt": (1, 1), "tn": (0, 0)}[mode]
    off = ndim - 2
    return (((lhs + off,), (rhs + off,)), (tuple(range(off)), tuple(range(off))))


def _dot(a, b, mode):
    return lax.dot_general(a, b, _dims(mode, a.ndim), preferred_element_type=F32)


def _halves(a):
    hi = a.astype(BF16)
    return hi, (a - hi.astype(F32)).astype(BF16)


def _mm_raw(a, b, mode, kind):
    if kind == "bf16":
        return _dot(a.astype(BF16), b.astype(BF16), mode)
    if kind == "lhs01":
        hi, lo = _halves(b)
        a = a.astype(BF16)
        return _dot(a, hi, mode) + _dot(a, lo, mode)
    if kind == "rhs01":
        hi, lo = _halves(a)
        b = b.astype(BF16)
        return _dot(hi, b, mode) + _dot(lo, b, mode)
    a_hi, a_lo = _halves(a)
    b_hi, b_lo = _halves(b)
    return _dot(a_hi, b_hi, mode) + (_dot(a_hi, b_lo, mode) + _dot(a_lo, b_hi, mode))


@functools.partial(jax.custom_vjp, nondiff_argnums=(2, 3))
def _mm(a, b, mode="nn", kind="bf16"):
    return _mm_raw(a, b, mode, kind)


def _mm_fwd(a, b, mode, kind):
    return _mm_raw(a, b, mode, kind), (a, b)


def _mm_bwd(mode, kind, res, g):
    a, b = res
    if kind == "lhs01":
        return jnp.zeros_like(a), _mm_raw(a, g, {"nn": "tn", "tn": "nn"}[mode], "lhs01")
    if kind == "rhs01":
        return _mm_raw(g, b, {"nn": "nt", "nt": "nn"}[mode], "rhs01"), jnp.zeros_like(b)
    if mode == "nn":
        return _mm_raw(g, b, "nt", kind), _mm_raw(a, g, "tn", kind)
    if mode == "nt":
        return _mm_raw(g, b, "nn", kind), _mm_raw(g, a, "tn", kind)
    return _mm_raw(b, g, "nt", kind), _mm_raw(a, g, "nn", kind)


_mm.defvjp(_mm_fwd, _mm_bwd)


def _iota2(shape, axis):
    return lax.broadcasted_iota(jnp.int32, shape, axis)


def _inv_unit_lower_raw(m):
    size = m.shape[-1]
    eye = (_iota2((size, size), 0) == _iota2((size, size), 1)).astype(F32)
    n = -m
    t = eye + n
    p = n
    steps = int(math.log2(size)) - 1
    for _ in range(steps):
        p = _mm_raw(p, p, "nn", "x3")
        t = t + _mm_raw(t, p, "nn", "x3")
    return t


@jax.custom_vjp
def _inv_unit_lower(m):
    return _inv_unit_lower_raw(m)


def _inv_fwd(m):
    t = _inv_unit_lower_raw(m)
    return t, t


def _inv_bwd(t, g):
    return (-_mm_raw(_mm_raw(t, g, "tn", "x3"), t, "nt", "x3"),)


_inv_unit_lower.defvjp(_inv_fwd, _inv_bwd)


def _safe_decay(col, row, keep):
    return jnp.where(keep, jnp.exp(jnp.where(keep, col - row, 0.0)), 0.0)


def _col_to_row(col):
    n = col.shape[-2]
    eye = _iota2((n, n), 0) == _iota2((n, n), 1)
    return jnp.sum(jnp.where(eye, col, 0.0), axis=-2, keepdims=True)


def _cumsum_col(col):
    n = col.shape[-2]
    li, si = _iota2((n, n), 0), _iota2((n, n), 1)
    row = _col_to_row(col)
    c_col = jnp.sum(jnp.where(li >= si, row, 0.0), axis=-1, keepdims=True)
    c_row = jnp.sum(jnp.where(li <= si, col, 0.0), axis=-2, keepdims=True)
    return c_col, c_row


def _gdn_chunk(q, k, v, g, beta, state):
    cl = q.shape[-2]
    li, si = _iota2((cl, cl), 0), _iota2((cl, cl), 1)
    gc_col, gc_row = _cumsum_col(g)
    g_last = jnp.sum(g, axis=-2, keepdims=True)
    dec_strict = _safe_decay(gc_col, gc_row, li > si)
    dec_incl = _safe_decay(gc_col, gc_row, li >= si)
    e_gc = jnp.exp(gc_col)
    qs = q * (HEAD_DIM ** -0.5)
    kb = k * beta
    m = _mm(kb, k, "nt") * dec_strict
    t_inv = _inv_unit_lower(m)
    u = _mm(t_inv, v * beta)
    w = _mm(t_inv, kb * e_gc)
    a_qk = _mm(qs, k, "nt") * dec_incl
    q_dec = qs * e_gc
    k_end = k * jnp.exp(g_last - gc_col)
    v_new = u - _mm(w, state)
    o = _mm(q_dec, state) + _mm(a_qk, v_new)
    new_state = state * jnp.exp(g_last) + _mm(k_end, v_new, "tn")
    return o, new_state


def _gdn_operands(qkv_ref, gt):
    nh, width = N_HEADS, N_HEADS * HEAD_DIM
    heads = lambda off: jnp.stack([qkv_ref[:, off + h * HEAD_DIM:off + (h + 1) * HEAD_DIM] for h in range(nh)])
    cols = lambda off: jnp.stack([gt[:, off + h:off + h + 1] for h in range(nh)])
    return heads(0), heads(width), heads(2 * width), cols(nh), cols(0)


def gdn_fwd(qkv, gates):
    lp = qkv.shape[0]
    nh = N_HEADS
    nc = lp // CHUNK
    width = nh * HEAD_DIM

    def body(qkv_ref, gt_ref, o_ref, s_ref, state):
        @pl.when(pl.program_id(0) == 0)
        def _():
            state[...] = jnp.zeros_like(state)

        s_in = state[...]
        s_ref[0] = s_in
        o, s_new = _gdn_chunk(*_gdn_operands(qkv_ref, gt_ref[...]), s_in)
        for h in range(nh):
            o_ref[:, h * HEAD_DIM:(h + 1) * HEAD_DIM] = o[h]
        state[...] = s_new

    return pl.pallas_call(
        body, name="gdn_fwd", grid=(nc,),
        in_specs=[pl.BlockSpec((CHUNK, 3 * width), lambda c: (c, 0)), pl.BlockSpec((CHUNK, HEAD_DIM), lambda c: (c, 0))],
        out_specs=[pl.BlockSpec((CHUNK, width), lambda c: (c, 0)), pl.BlockSpec((1, nh, HEAD_DIM, HEAD_DIM), lambda c: (c, 0, 0, 0))],
        out_shape=[jax.ShapeDtypeStruct((lp, width), F32), jax.ShapeDtypeStruct((nc, nh, HEAD_DIM, HEAD_DIM), F32)],
        scratch_shapes=[pltpu.VMEM((nh, HEAD_DIM, HEAD_DIM), F32)],
        compiler_params=pltpu.CompilerParams(dimension_semantics=("arbitrary",), vmem_limit_bytes=VMEM_LIMIT),
    )(qkv, gates)


def gdn_bwd(qkv, gates, states, d_o):
    lp = qkv.shape[0]
    nh = N_HEADS
    nc = lp // CHUNK
    width = nh * HEAD_DIM

    def body(qkv_ref, gt_ref, s_ref, do_ref, dqkv_ref, dgt_ref, d_state):
        @pl.when(pl.program_id(0) == 0)
        def _():
            d_state[...] = jnp.zeros_like(d_state)

        _, pull = jax.vjp(_gdn_chunk, *_gdn_operands(qkv_ref, gt_ref[...]), s_ref[0])
        d_o = jnp.stack([do_ref[:, h * HEAD_DIM:(h + 1) * HEAD_DIM] for h in range(nh)])
        dq, dk, dv, dg, db, ds = pull((d_o, d_state[...]))
        lane = _iota2((CHUNK, HEAD_DIM), 1)
        d_gt = jnp.zeros((CHUNK, HEAD_DIM), F32)
        for h in range(nh):
            for part, val in enumerate((dq, dk, dv)):
                dqkv_ref[:, part * width + h * HEAD_DIM:part * width + (h + 1) * HEAD_DIM] = val[h]
            d_gt = d_gt + jnp.where(lane == h, db[h], 0.0) + jnp.where(lane == nh + h, dg[h], 0.0)
        dgt_ref[...] = d_gt
        d_state[...] = ds

    rev = lambda c: (nc - 1 - c, 0)
    return pl.pallas_call(
        body, name="gdn_bwd", grid=(nc,),
        in_specs=[pl.BlockSpec((CHUNK, 3 * width), rev), pl.BlockSpec((CHUNK, HEAD_DIM), rev),
                  pl.BlockSpec((1, nh, HEAD_DIM, HEAD_DIM), lambda c: (nc - 1 - c, 0, 0, 0)), pl.BlockSpec((CHUNK, width), rev)],
        out_specs=[pl.BlockSpec((CHUNK, 3 * width), rev), pl.BlockSpec((CHUNK, HEAD_DIM), rev)],
        out_shape=[jax.ShapeDtypeStruct((lp, 3 * width), F32), jax.ShapeDtypeStruct((lp, HEAD_DIM), F32)],
        scratch_shapes=[pltpu.VMEM((nh, HEAD_DIM, HEAD_DIM), F32)],
        compiler_params=pltpu.CompilerParams(dimension_semantics=("arbitrary",), vmem_limit_bytes=VMEM_LIMIT),
    )(qkv, gates, states, d_o)


def _head_expand():
    width = SSM_HG * SSM_P
    return (_iota2((SSM_HG, width), 1) // SSM_P == _iota2((SSM_HG, width), 0)).astype(F32)


def _ssd_chunk(x, b, c, dt, la, state):
    cl = x.shape[0]
    li, si = _iota2((cl, cl), 0), _iota2((cl, cl), 1)
    causal = li >= si
    expand = _head_expand()
    tri = causal.astype(F32)
    xs = x * _mm(dt, expand, "nn", "rhs01")
    la_x = _mm(la, expand, "nn", "rhs01")
    cs_x = _mm(tri, la_x, "nn", "lhs01")
    last_x = jnp.sum(la_x, axis=0, keepdims=True)
    cs = _mm(tri, la, "nn", "lhs01")
    scores = _mm(c, b, "nt")
    head_id = _iota2((1, SSM_HG), 1)
    per_tile = HEAD_DIM // SSM_P
    tile_head = _iota2((1, HEAD_DIM), 1) // SSM_P
    within = []
    for t in range(SSM_HG // per_tile):
        xs_t = xs[:, t * HEAD_DIM:(t + 1) * HEAD_DIM]
        acc = jnp.zeros((cl, HEAD_DIM), F32)
        for hh in range(per_tile):
            cs_col = jnp.sum(jnp.where(head_id == t * per_tile + hh, cs, 0.0), axis=1, keepdims=True)
            decay = _safe_decay(cs_col, _col_to_row(cs_col), causal)
            acc = acc + _mm(scores * decay, jnp.where(tile_head == hh, xs_t, 0.0))
        within.append(acc)
    y = _mm(c, state) * jnp.exp(cs_x) + jnp.concatenate(within, axis=1)
    new_state = state * jnp.exp(last_x) + _mm(b, xs * jnp.exp(last_x - cs_x), "tn")
    return y, new_state


GATE_DT = 16


def _place_lanes(v, lo):
    n = v.shape[1]
    sel = (_iota2((n, HEAD_DIM), 1) == _iota2((n, HEAD_DIM), 0) + lo).astype(F32)
    return _mm_raw(v, sel, "nn", "rhs01")


def ssd_fwd(xbc, first, second):
    lp = xbc.shape[0]
    nc = lp // CHUNK
    width = SSM_HG * SSM_P
    b_off, c_off = SSM_INNER, SSM_INNER + SSM_GROUPS * SSM_N

    def body(x_ref, f_ref, s2_ref, y_ref, s_ref, state):
        @pl.when(pl.program_id(0) == 0)
        def _():
            state[...] = jnp.zeros_like(state)

        f, s2 = f_ref[...], s2_ref[...]
        for g in range(SSM_GROUPS):
            lo = GATE_DT + g * SSM_HG
            s_in = state[g]
            s_ref[0, g] = s_in
            y, s_new = _ssd_chunk(x_ref[:, g * width:(g + 1) * width], x_ref[:, b_off + g * SSM_N:b_off + (g + 1) * SSM_N],
                                  x_ref[:, c_off + g * SSM_N:c_off + (g + 1) * SSM_N], f[:, lo:lo + SSM_HG], s2[:, lo:lo + SSM_HG], s_in)
            y_ref[:, g * width:(g + 1) * width] = y
            state[g] = s_new

    row = lambda cols: pl.BlockSpec((CHUNK, cols), lambda k: (k, 0))
    return pl.pallas_call(
        body, name="ssd_fwd", grid=(nc,),
        in_specs=[row(xbc.shape[1]), row(HEAD_DIM), row(HEAD_DIM)],
        out_specs=[row(SSM_INNER), pl.BlockSpec((1, SSM_GROUPS, SSM_N, width), lambda k: (k, 0, 0, 0))],
        out_shape=[jax.ShapeDtypeStruct((lp, SSM_INNER), F32), jax.ShapeDtypeStruct((nc, SSM_GROUPS, SSM_N, width), F32)],
        scratch_shapes=[pltpu.VMEM((SSM_GROUPS, SSM_N, width), F32)],
        compiler_params=pltpu.CompilerParams(dimension_semantics=("arbitrary",), vmem_limit_bytes=VMEM_LIMIT),
    )(xbc, first, second)


def ssd_bwd(xbc, first, second, states, d_y, d_xh):
    lp = xbc.shape[0]
    nc = lp // CHUNK
    width = SSM_HG * SSM_P
    b_off, c_off = SSM_INNER, SSM_INNER + SSM_GROUPS * SSM_N

    def body(x_ref, f_ref, s2_ref, s_ref, dy_ref, dxh_ref, dx_ref, df_ref, ds2_ref, d_state):
        @pl.when(pl.program_id(0) == 0)
        def _():
            d_state[...] = jnp.zeros_like(d_state)

        f, s2 = f_ref[...], s2_ref[...]
        d_f = jnp.zeros((CHUNK, HEAD_DIM), F32)
        d_s2 = jnp.zeros((CHUNK, HEAD_DIM), F32)
        for g in range(SSM_GROUPS):
            lo = GATE_DT + g * SSM_HG
            x_l = slice(g * width, (g + 1) * width)
            b_l = slice(b_off + g * SSM_N, b_off + (g + 1) * SSM_N)
            c_l = slice(c_off + g * SSM_N, c_off + (g + 1) * SSM_N)
            _, pull = jax.vjp(_ssd_chunk, x_ref[:, x_l], x_ref[:, b_l], x_ref[:, c_l], f[:, lo:lo + SSM_HG], s2[:, lo:lo + SSM_HG],
                              s_ref[0, g])
            dx, db, dc, ddt, dla, ds = pull((dy_ref[:, x_l], d_state[g]))
            dx_ref[:, x_l] = dx + dxh_ref[:, x_l]
            dx_ref[:, b_l] = db
            dx_ref[:, c_l] = dc
            d_f = d_f + _place_lanes(ddt, lo)
            d_s2 = d_s2 + _place_lanes(dla, lo)
            d_state[g] = ds
        df_ref[...] = d_f
        ds2_ref[...] = d_s2

    row = lambda cols: pl.BlockSpec((CHUNK, cols), lambda k: (nc - 1 - k, 0))
    gate_shape = jax.ShapeDtypeStruct((lp, HEAD_DIM), F32)
    return pl.pallas_call(
        body, name="ssd_bwd", grid=(nc,),
        in_specs=[row(xbc.shape[1]), row(HEAD_DIM), row(HEAD_DIM),
                  pl.BlockSpec((1, SSM_GROUPS, SSM_N, width), lambda k: (nc - 1 - k, 0, 0, 0)), row(SSM_INNER), row(SSM_INNER)],
        out_specs=[row(xbc.shape[1]), row(HEAD_DIM), row(HEAD_DIM)],
        out_shape=[jax.ShapeDtypeStruct(xbc.shape, F32), gate_shape, gate_shape],
        scratch_shapes=[pltpu.VMEM((SSM_GROUPS, SSM_N, width), F32)],
        compiler_params=pltpu.CompilerParams(dimension_semantics=("arbitrary",), vmem_limit_bytes=VMEM_LIMIT),
    )(xbc, first, second, states, d_y, d_xh)


SB_QROWS = 544


def _mm_tri(a, tri):
    return _mm_raw(a, tri.astype(BF16), "nn", "rhs01")


def _sb_scores(q_scaled, kb, row0, j):
    shape = (q_scaled.shape[0], SB_BLOCK)
    z = _mm_raw(q_scaled, kb, "nt", "bf16")
    q_pos = row0 + _iota2(shape, 0)
    k_pos = j * SB_BLOCK + _iota2(shape, 1)
    valid = (k_pos < q_pos) & (k_pos >= PAD)
    sp = jnp.maximum(z, 0.0) + jnp.log(1.0 + jnp.exp(-jnp.abs(z)))
    lk = jnp.where(valid, -sp, 0.0)
    return z, sp, valid, lk


SB_DEAD = -110.0


def sb_fwd(src, offs, ride=None):
    lp = src.shape[0]
    nh = N_HEADS
    qb = _pick(lp, SB_QROWS, 8)
    scale = HEAD_DIM ** -0.5
    blk = SB_BLOCK

    n_in = len(ride.ins) if ride else 0
    n_out = len(ride.out_shapes) if ride else 0

    def body(*refs):
        q_ref, k_ref, v_ref = refs[:3]
        o_ref = refs[3 + n_in]
        ride_refs = (refs[3:3 + n_in], refs[4 + n_in:4 + n_in + n_out], refs[4 + n_in + n_out:])
        i = pl.program_id(1)
        if ride:
            @pl.when((pl.program_id(0) == 0) & (i == 0))
            def _():
                ride.start(*ride_refs)
        q_scaled = q_ref[...] * scale
        upper = _iota2((blk, blk), 0) > _iota2((blk, blk), 1)
        n_blocks = ((i + 1) * qb + blk - 1) // blk

        def live(state):
            it, _, c = state
            return (it < n_blocks) & (jnp.max(c) > SB_DEAD)

        def step(state):
            it, acc, c = state
            j = n_blocks - 1 - it
            rows = pl.ds(pl.multiple_of(j * blk, blk), blk)
            z, sp, valid, lk = _sb_scores(q_scaled, k_ref[rows, :], i * qb, j)
            later = _mm_tri(lk, upper) + c
            w = jnp.where(valid, jnp.exp(z - sp + later), 0.0)
            acc = acc + _mm_raw(w, v_ref[rows, :], "nn", "bf16")
            return it + 1, acc, c + jnp.sum(lk, axis=1, keepdims=True)

        _, acc, _ = lax.while_loop(live, step, (jnp.int32(0), jnp.zeros((qb, HEAD_DIM), F32), jnp.zeros((qb, 1), F32)))
        o_ref[...] = acc
        if ride:
            @pl.when((pl.program_id(0) == nh - 1) & (i == lp // qb - 1))
            def _():
                ride.finish(*ride_refs)

    qspec = pl.BlockSpec((qb, HEAD_DIM), lambda h, i: (i, offs[0] + h))
    kspec = pl.BlockSpec((lp, HEAD_DIM), lambda h, i: (0, offs[1] + h))
    vspec = pl.BlockSpec((lp, HEAD_DIM), lambda h, i: (0, offs[2] + h))
    ospec = pl.BlockSpec((qb, HEAD_DIM), lambda h, i: (i, h))
    return pl.pallas_call(
        body, name="sb_fwd", grid=(nh, lp // qb), in_specs=[qspec, kspec, vspec] + [ANY] * n_in, out_specs=[ospec] + [ANY] * n_out,
        out_shape=[jax.ShapeDtypeStruct((lp, nh * HEAD_DIM), F32)] + (ride.out_shapes if ride else []),
        scratch_shapes=ride.sems if ride else [],
        input_output_aliases={3 + k: 1 + k for k in range(n_in)} if ride and ride.alias else {},
        compiler_params=pltpu.CompilerParams(dimension_semantics=("arbitrary", "arbitrary"), vmem_limit_bytes=VMEM_LIMIT),
    )(src, src, src, *(ride.ins if ride else []))


def sb_bwd(src, offs, d_o, ride=None):
    lp = src.shape[0]
    nh = N_HEADS
    qb = _pick(lp, SB_QROWS, 8)
    scale = HEAD_DIM ** -0.5
    blk = SB_BLOCK

    n_in = len(ride.ins) if ride else 0
    n_out = len(ride.out_shapes) if ride else 0

    def body(*refs):
        q_ref, k_ref, v_ref, do_ref = refs[:4]
        dq_ref, dk_out, dv_out = refs[4 + n_in:7 + n_in]
        ride_refs = (refs[4:4 + n_in], refs[7 + n_in:7 + n_in + n_out], refs[9 + n_in + n_out:])
        dk_ref, dv_ref = refs[7 + n_in + n_out:9 + n_in + n_out]
        i = pl.program_id(1)
        if ride:
            @pl.when((pl.program_id(0) == 0) & (i == 0))
            def _():
                ride.start(*ride_refs)

        @pl.when(i == 0)
        def _():
            dk_ref[...] = jnp.zeros_like(dk_ref)
            dv_ref[...] = jnp.zeros_like(dv_ref)

        q_scaled = q_ref[...] * scale
        d_out = do_ref[...]
        lower_incl = _iota2((blk, blk), 0) <= _iota2((blk, blk), 1)
        lower = _iota2((blk, blk), 0) < _iota2((blk, blk), 1)
        n_blocks = ((i + 1) * qb + blk - 1) // blk

        def live(state):
            it, c = state
            return (it < n_blocks) & (jnp.max(c) > SB_DEAD)

        def count(state):
            it, c = state
            rows = pl.ds(pl.multiple_of((n_blocks - 1 - it) * blk, blk), blk)
            _, _, _, lk = _sb_scores(q_scaled, k_ref[rows, :], i * qb, n_blocks - 1 - it)
            return it + 1, c + jnp.sum(lk, axis=1, keepdims=True)

        n_live, total = lax.while_loop(live, count, (jnp.int32(0), jnp.zeros((qb, 1), F32)))

        def step(j, carry):
            acc, cp, ep = carry
            rows = pl.ds(pl.multiple_of(j * blk, blk), blk)
            kb = k_ref[rows, :]
            vb = v_ref[rows, :]
            z, sp, valid, lk = _sb_scores(q_scaled, kb, i * qb, j)
            later = total - cp - _mm_tri(lk, lower_incl)
            w = jnp.where(valid, jnp.exp(z - sp + later), 0.0)
            e = w * _mm_raw(d_out, vb, "nt", "bf16")
            before = ep + _mm_tri(e, lower)
            dz = jnp.where(valid, e * jnp.exp(-sp) - before * jnp.exp(z - sp), 0.0)
            dk_ref[rows, :] += _mm_raw(dz, q_scaled, "tn", "bf16")
            dv_ref[rows, :] += _mm_raw(w, d_out, "tn", "bf16")
            acc = acc + _mm_raw(dz, kb, "nn", "bf16")
            return acc, cp + jnp.sum(lk, axis=1, keepdims=True), ep + jnp.sum(e, axis=1, keepdims=True)

        zero_col = jnp.zeros((qb, 1), F32)
        acc, _, _ = lax.fori_loop(n_blocks - n_live, n_blocks, step, (jnp.zeros((qb, HEAD_DIM), F32), zero_col, zero_col))
        dq_ref[...] = (acc * scale).astype(dq_ref.dtype)

        @pl.when(i == lp // qb - 1)
        def _():
            dk_out[...] = dk_ref[...].astype(dk_out.dtype)
            dv_out[...] = dv_ref[...].astype(dv_out.dtype)

        if ride:
            @pl.when((pl.program_id(0) == nh - 1) & (i == lp // qb - 1))
            def _():
                ride.finish(*ride_refs)

    qspec = pl.BlockSpec((qb, HEAD_DIM), lambda h, i: (i, offs[0] + h))
    kspec = pl.BlockSpec((lp, HEAD_DIM), lambda h, i: (0, offs[1] + h))
    vspec = pl.BlockSpec((lp, HEAD_DIM), lambda h, i: (0, offs[2] + h))
    ospec = pl.BlockSpec((qb, HEAD_DIM), lambda h, i: (i, h))
    fullspec = pl.BlockSpec((lp, HEAD_DIM), lambda h, i: (0, h))
    return pl.pallas_call(
        body, name="sb_bwd", grid=(nh, lp // qb), in_specs=[qspec, kspec, vspec, ospec] + [ANY] * n_in,
        out_specs=[ospec, fullspec, fullspec] + [ANY] * n_out,
        out_shape=[jax.ShapeDtypeStruct((lp, nh * HEAD_DIM), BF16)] * 3 + (ride.out_shapes if ride else []),
        scratch_shapes=[pltpu.VMEM((lp, HEAD_DIM), F32)] * 2 + (ride.sems if ride else []),
        input_output_aliases={4 + k: 3 + k for k in range(n_in)} if ride and ride.alias else {},
        compiler_params=pltpu.CompilerParams(dimension_semantics=("arbitrary", "arbitrary"), vmem_limit_bytes=VMEM_LIMIT),
    )(src, src, src, d_o, *(ride.ins if ride else []))


def _pick(n, target, unit):
    if n <= target:
        return n
    best = None
    for d in range(unit, target + 1, unit):
        if n % d == 0:
            best = d
    assert best is not None, (n, target, unit)
    return best


def matmul(a, b, mode="nn", *, name, bm=1088, bn=640, bk=2176, residual=None, out_dtype=F32, b_koff=0):
    if mode == "nn":
        (m, k), n = a.shape, b.shape[1]
    elif mode == "nt":
        (m, k), n = a.shape, b.shape[0]
    else:
        (k, m), n = a.shape, b.shape[1]
    assert b_koff == 0 or mode == "nt"
    bm = _pick(m, bm, 128 if mode == "tn" else 8)
    bn = _pick(n, bn, 128 if mode != "nt" else 8)
    bk = _pick(k, bk, 128 if mode != "tn" else 8)
    nk = k // bk

    def body(*refs):
        if residual is None:
            a_ref, b_ref, o_ref, acc = refs
            r_ref = None
        else:
            a_ref, b_ref, r_ref, o_ref, acc = refs
        kk = pl.program_id(2)
        part = _mm_raw(a_ref[...], b_ref[...], mode, "bf16")

        @pl.when(kk == 0)
        def _():
            acc[...] = part

        @pl.when(kk > 0)
        def _():
            acc[...] += part

        @pl.when(kk == nk - 1)
        def _():
            res = acc[...]
            if r_ref is not None:
                res = res + r_ref[...]
            o_ref[...] = res.astype(out_dtype)

    a_spec = pl.BlockSpec((bk, bm), lambda i, j, kk: (kk, i)) if mode == "tn" else pl.BlockSpec((bm, bk), lambda i, j, kk: (i, kk))
    b_spec = pl.BlockSpec((bn, bk), lambda i, j, kk: (j, b_koff + kk)) if mode == "nt" else pl.BlockSpec((bk, bn), lambda i, j, kk: (kk, j))
    o_spec = pl.BlockSpec((bm, bn), lambda i, j, kk: (i, j))
    ins, specs = [a, b], [a_spec, b_spec]
    if residual is not None:
        ins.append(residual)
        specs.append(o_spec)
    return pl.pallas_call(
        body, name=name, grid=(m // bm, n // bn, nk), in_specs=specs, out_specs=o_spec,
        out_shape=jax.ShapeDtypeStruct((m, n), out_dtype),
        scratch_shapes=[pltpu.VMEM((bm, bn), F32)],
        compiler_params=pltpu.CompilerParams(dimension_semantics=("arbitrary", "arbitrary", "arbitrary"), vmem_limit_bytes=VMEM_LIMIT),
    )(*ins)


def _row_specs(rows, params, width, bm):
    row_specs = [pl.BlockSpec((bm, width), (lambda j, i, off=off: (i, off + j))) for _, off in rows]
    par_specs = [pl.BlockSpec((p.shape[0], width) if per_col else p.shape, ((lambda j, i: (0, j)) if per_col else (lambda j, i: (0, 0))))
                 for p, per_col in params]
    return row_specs, par_specs


def rowmap_fwd(name, fn, rows, params, n_out, *, width, ncol, bm, lp, out_dtypes=None):
    out_dtypes = out_dtypes or [F32] * n_out
    row_specs, par_specs = _row_specs(rows, params, width, bm)
    nr = len(rows)

    def body(*refs):
        ins, outs = refs[:nr + len(params)], refs[nr + len(params):]
        row_ids = pl.program_id(1) * bm + _iota2((bm, 1), 0)
        res = fn(row_ids, *[r[...].astype(F32) for r in ins])
        for o_ref, val in zip(outs, res):
            o_ref[...] = val.astype(o_ref.dtype)

    o_spec = pl.BlockSpec((bm, width), lambda j, i: (i, j))
    return pl.pallas_call(
        body, name=name, grid=(ncol, lp // bm), in_specs=row_specs + par_specs, out_specs=[o_spec] * n_out,
        out_shape=[jax.ShapeDtypeStruct((lp, ncol * width), dt) for dt in out_dtypes],
        compiler_params=pltpu.CompilerParams(dimension_semantics=("arbitrary", "arbitrary"), vmem_limit_bytes=VMEM_LIMIT),
    )(*[a for a, _ in rows], *[p for p, _ in params])


def rowmap_bwd(name, fn, rows, params, d_outs, *, width, ncol, bm, lp, d_row_dtypes=None):
    d_row_dtypes = d_row_dtypes or [F32] * len(rows)
    row_specs, par_specs = _row_specs(rows, params, width, bm)
    nr, npar, nout = len(rows), len(params), len(d_outs)

    def body(*refs):
        ins = refs[:nr + npar]
        dos = refs[nr + npar:nr + npar + nout]
        d_rows = refs[nr + npar + nout:nr + npar + nout + nr]
        d_pars = refs[nr + npar + nout + nr:]
        j, i = pl.program_id(0), pl.program_id(1)
        row_ids = i * bm + _iota2((bm, 1), 0)
        _, pull = jax.vjp(lambda *xs: tuple(fn(row_ids, *xs)), *[r[...].astype(F32) for r in ins])
        grads = pull(tuple(d[...].astype(F32) for d in dos))
        for ref, val in zip(d_rows, grads[:nr]):
            ref[...] = val.astype(ref.dtype)
        for ref, val, (_, per_col) in zip(d_pars, grads[nr:], params):
            first = (i == 0) if per_col else ((i == 0) & (j == 0))

            @pl.when(first)
            def _(ref=ref, val=val):
                ref[...] = val

            @pl.when(jnp.logical_not(first))
            def _(ref=ref, val=val):
                ref[...] += val

    o_spec = pl.BlockSpec((bm, width), lambda j, i: (i, j))
    res = pl.pallas_call(
        body, name=name, grid=(ncol, lp // bm), in_specs=row_specs + par_specs + [o_spec] * nout,
        out_specs=[o_spec] * nr + par_specs,
        out_shape=[jax.ShapeDtypeStruct((lp, ncol * width), dt) for dt in d_row_dtypes]
        + [jax.ShapeDtypeStruct(p.shape, F32) for p, _ in params],
        compiler_params=pltpu.CompilerParams(dimension_semantics=("arbitrary", "arbitrary"), vmem_limit_bytes=VMEM_LIMIT),
    )(*[a for a, _ in rows], *[p for p, _ in params], *d_outs)
    return res[:nr], res[nr:]


def _silu(x):
    return x * jax.nn.sigmoid(x)


def _softplus(x):
    return jnp.maximum(x, 0.0) + jnp.log(1.0 + jnp.exp(-jnp.abs(x)))


def _real_rows(row_ids):
    return (row_ids >= PAD).astype(F32)


def _f_rmsnorm(row_ids, h, g):
    return (h * lax.rsqrt(jnp.mean(h * h, axis=-1, keepdims=True) + RMS_EPS) * g,)


def _f_small_gates(row_ids, small, bias, a_log):
    lane = _iota2(small.shape, 1)
    t = small + bias
    sp = _softplus(t)
    coef = -jnp.exp(a_log)
    keep = _real_rows(row_ids)
    first = jnp.where(lane < 8, jax.nn.sigmoid(t), jnp.where(lane < 16, coef * sp, jnp.where(lane < 48, sp, 0.0)))
    second = jnp.where((lane >= 8) & (lane < 48), coef * sp, 0.0)
    return first * keep, second * keep


def _f_gate_silu(row_ids, o, z):
    return (o * _silu(z),)


def _f_head_norm_gate(row_ids, o, z, g):
    out = []
    for h in range(o.shape[1] // HEAD_DIM):
        oh = o[:, h * HEAD_DIM:(h + 1) * HEAD_DIM]
        out.append(oh * lax.rsqrt(jnp.mean(oh * oh, axis=-1, keepdims=True) + RMS_EPS) * g)
    return (jnp.concatenate(out, axis=1) * _silu(z),)


def _f_ssm_out(row_ids, y, xh, z, d_skip, g):
    t = (y + d_skip * xh) * _silu(z)
    return (t * lax.rsqrt(jnp.mean(t * t, axis=-1, keepdims=True) + RMS_EPS) * g,)


def _f_merge(row_ids, pa, pb, pc, ga, gb, gc):
    return (jax.nn.sigmoid(ga) * pa + jax.nn.sigmoid(gb) * pb + jax.nn.sigmoid(gc) * pc,)


def _shift_rows(x, s):
    s = s % x.shape[0]
    return x if s == 0 else pltpu.roll(x, s, 0)


def _conv_pre(x, w, b):
    pre = b
    for kk in range(CONV_K):
        pre = pre + w[kk:kk + 1, :] * _shift_rows(x, CONV_K - 1 - kk)
    return pre


def _conv_post(pre, l2_flag, keep):
    act = _silu(pre)
    nrm = act * lax.rsqrt(jnp.sum(act * act, axis=-1, keepdims=True) + L2_EPS)
    return (l2_flag * nrm + (1.0 - l2_flag) * act) * keep


def conv_fwd(name, src, col_off, w, b, n_l2):
    lp, ch = src.shape[0], w.shape[1]

    def body(x_ref, w_ref, b_ref, o_ref):
        l2_flag = (pl.program_id(0) < n_l2).astype(F32)
        keep = _real_rows(_iota2((lp, 1), 0))
        o_ref[...] = _conv_post(_conv_pre(x_ref[...], w_ref[...], b_ref[...]), l2_flag, keep)

    return pl.pallas_call(
        body, name=name, grid=(ch // HEAD_DIM,),
        in_specs=[pl.BlockSpec((lp, HEAD_DIM), lambda j: (0, col_off + j)), pl.BlockSpec((CONV_K, HEAD_DIM), lambda j: (0, j)),
                  pl.BlockSpec((1, HEAD_DIM), lambda j: (0, j))],
        out_specs=pl.BlockSpec((lp, HEAD_DIM), lambda j: (0, j)),
        out_shape=jax.ShapeDtypeStruct((lp, ch), F32),
        compiler_params=pltpu.CompilerParams(dimension_semantics=("arbitrary",), vmem_limit_bytes=VMEM_LIMIT),
    )(src, w, b)


def conv_bwd(name, src, col_off, w, b, n_l2, d_out):
    lp, ch = src.shape[0], w.shape[1]

    def body(x_ref, w_ref, b_ref, do_ref, dx_ref, dw_ref, db_ref):
        l2_flag = (pl.program_id(0) < n_l2).astype(F32)
        keep = _real_rows(_iota2((lp, 1), 0))
        x, wv = x_ref[...], w_ref[...]
        pre = _conv_pre(x, wv, b_ref[...])
        _, pull = jax.vjp(lambda p: _conv_post(p, l2_flag, keep), pre)
        (d_pre,) = pull(do_ref[...])
        dx = jnp.zeros_like(x)
        for kk in range(CONV_K):
            s = CONV_K - 1 - kk
            dx = dx + wv[kk:kk + 1, :] * _shift_rows(d_pre, -s)
            dw_ref[kk:kk + 1, :] = jnp.sum(d_pre * _shift_rows(x, s), axis=0, keepdims=True)
        dx_ref[...] = (dx * keep).astype(dx_ref.dtype)
        db_ref[...] = jnp.sum(d_pre, axis=0, keepdims=True)

    seq = pl.BlockSpec((lp, HEAD_DIM), lambda j: (0, j))
    wspec = pl.BlockSpec((CONV_K, HEAD_DIM), lambda j: (0, j))
    bspec = pl.BlockSpec((1, HEAD_DIM), lambda j: (0, j))
    return pl.pallas_call(
        body, name=name, grid=(ch // HEAD_DIM,),
        in_specs=[pl.BlockSpec((lp, HEAD_DIM), lambda j: (0, col_off + j)), wspec, bspec, seq],
        out_specs=[seq, wspec, bspec],
        out_shape=[jax.ShapeDtypeStruct((lp, ch), BF16), jax.ShapeDtypeStruct(w.shape, F32), jax.ShapeDtypeStruct(b.shape, F32)],
        compiler_params=pltpu.CompilerParams(dimension_semantics=("arbitrary",), vmem_limit_bytes=VMEM_LIMIT),
    )(src, w, b, d_out)


def loss_head(h, target, g):
    lp, d = h.shape
    bm = SB_BLOCK
    first = (PAD + N_META) // bm

    def body(h_ref, t_ref, g_ref, loss_ref, dh_ref, dg_ref):
        i = pl.program_id(0)
        keep = (i >= first).astype(F32)

        def f(hv, gv):
            y = hv * lax.rsqrt(jnp.mean(hv * hv, axis=-1, keepdims=True) + RMS_EPS) * gv
            err = y - t_ref[...]
            return 0.5 * jnp.sum(jnp.mean(err * err, axis=-1, keepdims=True), axis=0, keepdims=True) * keep

        val, pull = jax.vjp(f, h_ref[...], g_ref[...])
        dh, dg = pull(jnp.ones((1, 1), F32))
        dh_ref[...] = dh

        @pl.when(i == 0)
        def _():
            loss_ref[...] = val
            dg_ref[...] = dg

        @pl.when(i > 0)
        def _():
            loss_ref[...] += val
            dg_ref[...] += dg

    row = pl.BlockSpec((bm, d), lambda i: (i, 0))
    return pl.pallas_call(
        body, name="loss_head", grid=(lp // bm,),
        in_specs=[row, pl.BlockSpec((bm, d), lambda i: (jnp.maximum(i - first, 0), 0)), pl.BlockSpec((1, d), lambda i: (0, 0))],
        out_specs=[pl.BlockSpec((1, 1), lambda i: (0, 0)), row, pl.BlockSpec((1, d), lambda i: (0, 0))],
        out_shape=[jax.ShapeDtypeStruct((1, 1), F32), jax.ShapeDtypeStruct((lp, d), F32), jax.ShapeDtypeStruct((1, d), F32)],
        compiler_params=pltpu.CompilerParams(dimension_semantics=("arbitrary",)),
    )(h, target, g)


ADAM_LR, ADAM_B1, ADAM_B2, ADAM_EPS, ADAM_WD, ADAM_STEP = 0.001, 0.9, 0.999, 1e-08, 0.01, 10


def adamw(name, w, g, m, v, echo_g=False):
    lead = w.shape[:-2]
    rows, cols = w.shape[-2:]
    br = _pick(rows, 128, 8)
    n_out = 4 if echo_g else 3

    def body(w_ref, g_ref, m_ref, v_ref, d_ref, nm_ref, nv_ref, *echo):
        gv = g_ref[...]
        nm = ADAM_B1 * m_ref[...] + (1.0 - ADAM_B1) * gv
        nv = ADAM_B2 * v_ref[...] + (1.0 - ADAM_B2) * (gv * gv)
        m_hat = nm / (1.0 - ADAM_B1 ** ADAM_STEP)
        v_hat = nv / (1.0 - ADAM_B2 ** ADAM_STEP)
        d_ref[...] = -ADAM_LR * (m_hat / (jnp.sqrt(v_hat) + ADAM_EPS) + ADAM_WD * w_ref[...])
        nm_ref[...] = nm
        nv_ref[...] = nv
        for e_ref in echo:
            e_ref[...] = gv

    if lead and rows <= 8:
        bl = _pick(lead[0], 32, 1)
        spec = pl.BlockSpec((bl, rows, cols), lambda s: (s, 0, 0))
        grid = (lead[0] // bl,)
    elif lead:
        spec = pl.BlockSpec((None, br, cols), lambda s, i: (s, i, 0))
        grid = (lead[0], rows // br)
    else:
        spec = pl.BlockSpec((br, cols), lambda i: (i, 0))
        grid = (rows // br,)
    return pl.pallas_call(
        body, name=name, grid=grid, in_specs=[spec] * 4, out_specs=[spec] * n_out,
        out_shape=[jax.ShapeDtypeStruct(w.shape, F32)] * n_out,
        compiler_params=pltpu.CompilerParams(dimension_semantics=("arbitrary",) * len(grid), vmem_limit_bytes=VMEM_LIMIT),
    )(w, g, m, v)


MESH = pl.DeviceIdType.MESH
ANY = pl.BlockSpec(memory_space=pl.ANY)
D2D_PIECES = 16
ICI_PIECES = 4


def _place():
    x, y, c = lax.axis_index("x"), lax.axis_index("y"), lax.axis_index("c")
    return x, y, c, [(1 - x, y), (x, 1 - y), (1 - x, 1 - y)]


def _pieces(rows, n, unit):
    per = -(-rows // (n * unit)) * unit
    return [(s, min(per, rows - s)) for s in range(0, rows, per)]


def _row_unit(dtype):
    return 16 if dtype == BF16 else 8


def _scalar(v):
    return jnp.reshape(v, (1,)).astype(jnp.int32)


def place_shard(name, pack):
    rows, cols = pack.shape
    br = _pick(rows, 256, 16)

    def body(m_ref, p_ref, o_ref):
        o_ref[...] = p_ref[...]

    return pl.pallas_call(
        body, name=name,
        grid_spec=pltpu.PrefetchScalarGridSpec(
            num_scalar_prefetch=1, grid=(rows // br,),
            in_specs=[pl.BlockSpec((br, cols), lambda i, m: (i, 0))],
            out_specs=pl.BlockSpec((None, br, cols), lambda i, m: (m[0], i, 0))),
        out_shape=jax.ShapeDtypeStruct((4, rows, cols), pack.dtype),
        compiler_params=pltpu.CompilerParams(dimension_semantics=("arbitrary",), vmem_limit_bytes=VMEM_LIMIT),
    )(_scalar(2 * lax.axis_index("x") + lax.axis_index("y")), pack)


class Ride:
    def __init__(self, ins, out_shapes, alias, sems, start, finish):
        self.ins, self.out_shapes, self.alias, self.sems, self.start, self.finish = list(ins), out_shapes, alias, sems, start, finish


def _gather_parts(o_refs, send_sems, recv_sems):
    x, y, c, chips = _place()
    mine = 2 * x + y

    def half_rows(b, which, start=0, size=None):
        half = o_refs[b].shape[1] // 2
        return pl.ds(pl.multiple_of(which * half + start, _row_unit(o_refs[b].dtype)), half if size is None else size)

    def remote(b, k, slot, rws, to):
        piece = o_refs[b].at[slot, rws, :]
        return pltpu.make_async_remote_copy(src_ref=piece, dst_ref=piece, send_sem=send_sems.at[b, k], recv_sem=recv_sems.at[b, k],
                                            device_id=to, device_id_type=MESH)

    return x, y, c, chips, mine, half_rows, remote


def _gather_start(o_refs, send_sems, recv_sems):
    x, y, c, chips, mine, half_rows, remote = _gather_parts(o_refs, send_sems, recv_sems)
    for b, o_ref in enumerate(o_refs):
        for j, (cx, cy) in enumerate(chips):
            for start, size in _pieces(o_ref.shape[1] // 2, ICI_PIECES, _row_unit(o_ref.dtype)):
                remote(b, j, mine, half_rows(b, c, start, size), (cx, cy, c)).start()


def _gather_finish(o_refs, send_sems, recv_sems):
    x, y, c, chips, mine, half_rows, remote = _gather_parts(o_refs, send_sems, recv_sems)
    sends = []
    for b, o_ref in enumerate(o_refs):
        for j, (cx, cy) in enumerate(chips):
            slot = 2 * cx + cy
            sends.append(remote(b, j, mine, half_rows(b, c), (cx, cy, c)))
            remote(b, j, slot, half_rows(b, c), (cx, cy, c)).wait_recv()
            for start, size in _pieces(o_ref.shape[1] // 2, D2D_PIECES, _row_unit(o_ref.dtype)):
                remote(b, 3 + j, slot, half_rows(b, c, start, size), (x, y, 1 - c)).start()
            sends.append(remote(b, 3 + j, slot, half_rows(b, c), (x, y, 1 - c)))
    for b in range(len(o_refs)):
        for j, (cx, cy) in enumerate(chips):
            remote(b, 3 + j, 2 * cx + cy, half_rows(b, 1 - c), (x, y, 1 - c)).wait_recv()
    for cp in sends:
        cp.wait_send()


def gather_ride(placed):
    n = len(placed)
    return Ride(placed, [jax.ShapeDtypeStruct(p.shape, p.dtype) for p in placed], True,
                [pltpu.SemaphoreType.DMA((n, 6)), pltpu.SemaphoreType.DMA((n, 6))],
                lambda ins, outs, sems: _gather_start(outs, *sems), lambda ins, outs, sems: _gather_finish(outs, *sems))


def gather_shards(name, placed):
    n = len(placed)

    def body(*refs):
        o_refs, sems = refs[n:2 * n], refs[2 * n:]
        _gather_start(o_refs, *sems)
        _gather_finish(o_refs, *sems)

    return pl.pallas_call(
        body, name=name, in_specs=[ANY] * n, out_specs=[ANY] * n,
        out_shape=[jax.ShapeDtypeStruct(p.shape, p.dtype) for p in placed],
        input_output_aliases={i: i for i in range(n)},
        scratch_shapes=[pltpu.SemaphoreType.DMA((n, 6)), pltpu.SemaphoreType.DMA((n, 6))],
    )(*placed)


def pair_split(name, bufs):
    n = len(bufs)

    def body(*refs):
        g_refs, t_refs = refs[:n], refs[n:2 * n]
        send_sems, recv_sems = refs[2 * n:]
        x, y, c, _ = _place()
        waits = []
        for b, (g_ref, t_ref) in enumerate(zip(g_refs, t_refs)):
            half = g_ref.shape[2]
            unit = _row_unit(g_ref.dtype)

            def copy(slots, start, size):
                rws = pl.ds(start, size)
                return pltpu.make_async_remote_copy(src_ref=g_ref.at[slots, 1 - c, rws, :], dst_ref=t_ref.at[slots, rws, :],
                                                    send_sem=send_sems.at[b], recv_sem=recv_sems.at[b], device_id=(x, y, 1 - c),
                                                    device_id_type=MESH)

            for s in range(4):
                for start, size in _pieces(half, D2D_PIECES // 4, unit):
                    copy(s, start, size).start()
            waits.append(copy(slice(None), 0, half))
        for cp in waits:
            cp.wait()

    return pl.pallas_call(
        body, name=name, in_specs=[ANY] * n, out_specs=[ANY] * n,
        out_shape=[jax.ShapeDtypeStruct((4,) + g.shape[2:], g.dtype) for g in bufs],
        scratch_shapes=[pltpu.SemaphoreType.DMA((n,)), pltpu.SemaphoreType.DMA((n,))],
    )(*bufs)


def pair_add(name, g, theirs, transit):
    _, _, half, cols = g.shape
    br = _pick(half, 128, 16)

    def body(c_ref, g_ref, t_ref, o_ref):
        o_ref[...] = (g_ref[...].astype(F32) + t_ref[...].astype(F32)).astype(transit)

    blk = (4, br, cols)
    return pl.pallas_call(
        body, name=name,
        grid_spec=pltpu.PrefetchScalarGridSpec(
            num_scalar_prefetch=1, grid=(half // br,),
            in_specs=[pl.BlockSpec((4, None, br, cols), lambda i, c: (0, c[0], i, 0)), pl.BlockSpec(blk, lambda i, c: (0, i, 0))],
            out_specs=pl.BlockSpec(blk, lambda i, c: (0, i, 0))),
        out_shape=jax.ShapeDtypeStruct((4, half, cols), transit),
        compiler_params=pltpu.CompilerParams(dimension_semantics=("arbitrary",), vmem_limit_bytes=VMEM_LIMIT),
    )(_scalar(lax.axis_index("c")), g, theirs)


def _exchange_copies(a_refs, o_refs, send_sems, recv_sems, start):
    x, y, c, chips = _place()
    mine = 2 * x + y
    waits = []
    for b, (a_ref, o_ref) in enumerate(zip(a_refs, o_refs)):
        rows = a_ref.shape[1]
        for j, (cx, cy) in enumerate(chips):
            def copy(first, size):
                rws = pl.ds(first, size)
                return pltpu.make_async_remote_copy(src_ref=a_ref.at[2 * cx + cy, rws, :], dst_ref=o_ref.at[mine, rws, :],
                                                    send_sem=send_sems.at[b, j], recv_sem=recv_sems.at[b, j],
                                                    device_id=(cx, cy, c), device_id_type=MESH)
            if start:
                for first, size in _pieces(rows, ICI_PIECES, _row_unit(a_ref.dtype)):
                    copy(first, size).start()
            else:
                waits.append(copy(0, rows))
    return waits


def _exchange_finish(a_refs, o_refs, send_sems, recv_sems):
    for cp in _exchange_copies(a_refs, o_refs, send_sems, recv_sems, False):
        cp.wait()


def exchange_ride(parts):
    n = len(parts)
    return Ride(parts, [jax.ShapeDtypeStruct(a.shape, a.dtype) for a in parts], False,
                [pltpu.SemaphoreType.DMA((n, 3)), pltpu.SemaphoreType.DMA((n, 3))],
                lambda ins, outs, sems: _exchange_copies(ins, outs, *sems, True), lambda ins, outs, sems: _exchange_finish(ins, outs, *sems))


def chip_exchange(name, parts):
    n = len(parts)

    def body(*refs):
        a_refs, o_refs, sems = refs[:n], refs[n:2 * n], refs[2 * n:]
        _exchange_copies(a_refs, o_refs, *sems, True)
        _exchange_finish(a_refs, o_refs, *sems)

    return pl.pallas_call(
        body, name=name, in_specs=[ANY] * n, out_specs=[ANY] * n,
        out_shape=[jax.ShapeDtypeStruct(a.shape, a.dtype) for a in parts],
        scratch_shapes=[pltpu.SemaphoreType.DMA((n, 3)), pltpu.SemaphoreType.DMA((n, 3))],
    )(*parts)


def chip_add(name, got, part):
    _, rows, cols = got.shape
    br = _pick(rows, 128, 16)
    nblk = rows // br

    def body(m_ref, c_ref, got_ref, part_ref, o_ref):
        mine = m_ref[0]
        for s in range(4):
            @pl.when(mine == s)
            def _(s=s):
                val = part_ref[...].astype(F32)
                o_ref[...] = val if s == 0 else o_ref[...] + val

            @pl.when(mine != s)
            def _(s=s):
                val = got_ref[s].astype(F32)
                o_ref[...] = val if s == 0 else o_ref[...] + val

    return pl.pallas_call(
        body, name=name,
        grid_spec=pltpu.PrefetchScalarGridSpec(
            num_scalar_prefetch=2, grid=(nblk,),
            in_specs=[pl.BlockSpec((4, br, cols), lambda i, m, c: (0, i, 0)),
                      pl.BlockSpec((None, br, cols), lambda i, m, c: (m[0], i, 0))],
            out_specs=pl.BlockSpec((br, cols), lambda i, m, c: (c[0] * nblk + i, 0))),
        out_shape=jax.ShapeDtypeStruct((2 * rows, cols), F32),
        compiler_params=pltpu.CompilerParams(dimension_semantics=("arbitrary",), vmem_limit_bytes=VMEM_LIMIT),
    )(_scalar(2 * lax.axis_index("x") + lax.axis_index("y")), _scalar(lax.axis_index("c")), got, part)


def pair_join(name, fulls):
    n = len(fulls)

    def body(*refs):
        o_refs = refs[n:2 * n]
        send_sems, recv_sems = refs[2 * n:]
        x, y, c, _ = _place()
        waits = []
        for b, o_ref in enumerate(o_refs):
            half = o_ref.shape[0] // 2
            unit = _row_unit(o_ref.dtype)

            def copy(start, size):
                piece = o_ref.at[pl.ds(pl.multiple_of(c * half + start, unit), size), :]
                return pltpu.make_async_remote_copy(src_ref=piece, dst_ref=piece, send_sem=send_sems.at[b], recv_sem=recv_sems.at[b],
                                                    device_id=(x, y, 1 - c), device_id_type=MESH)

            for start, size in _pieces(half, D2D_PIECES, unit):
                copy(start, size).start()
            waits.append(copy(0, half))
        for cp in waits:
            cp.wait()

    return pl.pallas_call(
        body, name=name, in_specs=[ANY] * n, out_specs=[ANY] * n,
        out_shape=[jax.ShapeDtypeStruct(f.shape, f.dtype) for f in fulls],
        input_output_aliases={i: i for i in range(n)},
        scratch_shapes=[pltpu.SemaphoreType.DMA((n,)), pltpu.SemaphoreType.DMA((n,))],
    )(*fulls)


D_IN = 15920
D_PROJ = 16000
_SEGMENTS = ((0, 8192), (8208, 12816), (12848, 15920), (8192, 8208), (12816, 12848))
OFF_SB_Z, OFF_GDN_QKV, OFF_GDN_Z, OFF_SSM_Z, OFF_SSM_XBC, OFF_GATES, OFF_SMALL = 3072, 4096, 7168, 8192, 10240, 12800, 15872
PACK_C = 1024
WEIGHTS = ("meta_tokens", "norm_g", "w_in", "gdn_conv_w", "gdn_a_log", "gdn_dt_bias", "gdn_norm_g", "ssm_conv_w", "ssm_conv_b",
           "ssm_a_log", "ssm_dt_bias", "ssm_d", "ssm_norm_g", "w_branch_a", "w_branch_b", "w_branch_c", "w_out", "final_norm_g")
SHARDED = ("w_in", "w_branch_a", "w_branch_b", "w_branch_c", "w_out", "gdn_conv_w", "ssm_conv_w", "meta_tokens")
SHARD_AXIS = {"w_in": 2, "w_branch_a": 1, "w_branch_b": 1, "w_branch_c": 1, "w_out": 1, "gdn_conv_w": 2, "ssm_conv_w": 2, "meta_tokens": 1}
BRANCH = ("w_branch_a", "w_branch_b", "w_branch_c", "w_out")
EXACT = ("gdn_conv_w", "ssm_conv_w", "meta_tokens")
REPLICATED = tuple(n for n in WEIGHTS if n not in SHARDED)


def _regrouped_from_shards(shard_cols):
    out = []
    for a, b in _SEGMENTS:
        while a < b:
            chip = a // shard_cols
            stop = min(b, (chip + 1) * shard_cols)
            out.append((chip, a - chip * shard_cols, stop - chip * shard_cols))
            a = stop
    return out


def _shard_from_regrouped(chip, shard_cols):
    lo, hi = chip * shard_cols, (chip + 1) * shard_cols
    out, pos = [], 0
    starts = {}
    for a, b in _SEGMENTS:
        starts[(a, b)] = pos
        pos += b - a
    for a, b in sorted(_SEGMENTS):
        s0, s1 = max(a, lo), min(b, hi)
        if s0 < s1:
            out.append((starts[(a, b)] + s0 - a, starts[(a, b)] + s1 - a))
    return out


def _pack(parts, row_unit=64):
    n = sum(p.shape[0] for p in parts)
    rows = -(-n // (PACK_C * row_unit)) * row_unit
    flat = jnp.concatenate(list(parts) + [jnp.zeros((rows * PACK_C - n,), parts[0].dtype)])
    return flat.reshape(rows, PACK_C)


def _unpack(buf, shapes):
    flat, out, pos = buf.reshape(-1), [], 0
    for shp in shapes:
        n = math.prod(shp)
        out.append(flat[pos:pos + n].reshape(shp))
        pos += n
    return out


def _as_bf16_words(a):
    return lax.bitcast_convert_type(a, BF16).reshape(-1)


BRANCH_ROWS = (D_MODEL // 4, D_MODEL // 4, SSM_INNER // 4, D_MODEL // 4)


def _place_weights(w):
    depth = w["w_in"].shape[0]
    layers = []
    for l in range(depth):
        a = w["w_in"][l].astype(BF16)
        b = jnp.concatenate([w[n][l] for n in BRANCH], axis=0).astype(BF16)
        layers.append([place_shard("place_w_in", a), place_shard("place_branch", b)])
    small = place_shard("place_exact", _pack([_as_bf16_words(w[n]) for n in EXACT]))
    return layers, small


def _exact_weights(w, got_s):
    per_chip = [_unpack(got_s[c], [w[n].shape + (2,) for n in EXACT]) for c in range(4)]
    return {n: jnp.concatenate([lax.bitcast_convert_type(per_chip[c][i], F32) for c in range(4)], axis=SHARD_AXIS[n])
            for i, n in enumerate(EXACT)}


def _layer_weights(got_a, got_b):
    d_model, shard_cols = got_a.shape[1:]
    pad = jnp.zeros((d_model, D_PROJ - D_IN), BF16)
    out = {"wp": jnp.concatenate([got_a[c, :, lo:hi] for c, lo, hi in _regrouped_from_shards(shard_cols)] + [pad], axis=1)}
    pos = 0
    for n, rows in zip(BRANCH, BRANCH_ROWS):
        out[n] = jnp.concatenate([got_b[c, pos:pos + rows] for c in range(4)], axis=0)
        pos += rows
    return out


def _shard(a, axis, s):
    size = a.shape[axis] // 4
    return lax.slice_in_dim(a, s * size, (s + 1) * size, axis=axis)


def _layer_grad_buffers(g, shard_cols):
    buf_a = jnp.stack([jnp.concatenate([g["w_in"][:, lo:hi] for lo, hi in _shard_from_regrouped(s, shard_cols)], axis=1)
                       for s in range(4)]).astype(BF16)
    buf_b = jnp.stack([jnp.concatenate([_shard(g[n], 0, s) for n in BRANCH], axis=0) for s in range(4)]).astype(BF16)
    return [buf_a, buf_b]


def _split_halves(buf):
    return buf.reshape(4, 2, buf.shape[1] // 2, buf.shape[2])


def _start_reduce(tag, bufs, transits):
    bufs = [_split_halves(g) for g in bufs]
    theirs = pair_split(tag + "_pair_split", bufs)
    return [pair_add(f"{tag}_pair_add_{i}", g, t, tr) for i, (g, t, tr) in enumerate(zip(bufs, theirs, transits))]


def _layer_params(w, exact, weights, l):
    lane = lambda v, lo: jnp.pad(v, (lo, HEAD_DIM - lo - v.shape[0]))[None]
    return dict(
        norm_g=w["norm_g"][l][None], wp=weights["wp"],
        gdn_conv_w=exact["gdn_conv_w"][l], gdn_conv_b=jnp.zeros((1, 3 * N_HEADS * HEAD_DIM), F32),
        ssm_conv_w=exact["ssm_conv_w"][l], ssm_conv_b=w["ssm_conv_b"][l][None],
        bias_vec=lane(w["gdn_dt_bias"][l], 8) + lane(w["ssm_dt_bias"][l], 16),
        alog_vec=lane(w["gdn_a_log"][l], 8) + lane(w["ssm_a_log"][l], 16),
        gdn_norm_g=w["gdn_norm_g"][l][None], d_skip=jnp.repeat(w["ssm_d"][l], SSM_P)[None], ssm_norm_g=w["ssm_norm_g"][l][None],
        wa=weights["w_branch_a"], wb=weights["w_branch_b"], wc=weights["w_branch_c"], wo=weights["w_out"])


def _layer_fwd(h, p, ride):
    lp = h.shape[0]
    bm = _pick(lp, 272, 8)
    kw = dict(bm=bm, lp=lp)
    (u,) = rowmap_fwd("rms_fwd", _f_rmsnorm, [(h, 0)], [(p["norm_g"], False)], 1, width=D_MODEL, ncol=1, out_dtypes=[BF16], **kw)
    proj = matmul(u, p["wp"], "nn", name="proj", bm=lp, bn=640)
    o_a_raw, *rode = sb_fwd(proj, (0, N_HEADS, 2 * N_HEADS), ride)
    qkv = conv_fwd("gdn_conv_fwd", proj, OFF_GDN_QKV // HEAD_DIM, p["gdn_conv_w"], p["gdn_conv_b"], 2 * N_HEADS)
    first, second = rowmap_fwd("gates_fwd", _f_small_gates, [(proj, OFF_SMALL // HEAD_DIM)],
                               [(p["bias_vec"], False), (p["alog_vec"], False)], 2, width=HEAD_DIM, ncol=1, **kw)
    o_b_raw, gdn_states = gdn_fwd(qkv, first)
    xbc = conv_fwd("ssm_conv_fwd", proj, OFF_SSM_XBC // HEAD_DIM, p["ssm_conv_w"], p["ssm_conv_b"], 0)
    y_raw, ssd_states = ssd_fwd(xbc, first, second)
    (o_a,) = rowmap_fwd("gate_a_fwd", _f_gate_silu, [(o_a_raw, 0), (proj, OFF_SB_Z // 1024)], [], 1, width=1024, ncol=1,
                        out_dtypes=[BF16], **kw)
    (o_b,) = rowmap_fwd("gate_b_fwd", _f_head_norm_gate, [(o_b_raw, 0), (proj, OFF_GDN_Z // 1024)], [(p["gdn_norm_g"], False)], 1,
                        width=1024, ncol=1, out_dtypes=[BF16], **kw)
    (o_c,) = rowmap_fwd("gate_c_fwd", _f_ssm_out, [(y_raw, 0), (xbc, 0), (proj, OFF_SSM_Z // 1024)],
                        [(p["d_skip"], True), (p["ssm_norm_g"], True)], 1, width=1024, ncol=SSM_GROUPS, out_dtypes=[BF16], **kw)
    pa = matmul(o_a, p["wa"], "nn", name="branch_a", bm=lp // 2, bn=512)
    pb = matmul(o_b, p["wb"], "nn", name="branch_b", bm=lp // 2, bn=512)
    pc = matmul(o_c, p["wc"], "nn", name="branch_c", bm=lp // 2, bn=512)
    merge_rows = [(pa, 0), (pb, 0), (pc, 0)] + [(proj, OFF_GATES // 512 + 2 * i) for i in range(3)]
    (merged,) = rowmap_fwd("merge_fwd", _f_merge, merge_rows, [], 1, width=512, ncol=2, out_dtypes=[BF16], **kw)
    h_out = matmul(merged, p["wo"], "nn", name="out_proj", bm=lp, bn=512, residual=h)
    saved = dict(h=h, u=u, proj=proj, qkv=qkv, first=first, second=second, o_a_raw=o_a_raw, o_b_raw=o_b_raw,
                 gdn_states=gdn_states, xbc=xbc, y_raw=y_raw, ssd_states=ssd_states, o_a=o_a, o_b=o_b, o_c=o_c, pa=pa, pb=pb, pc=pc,
                 merged=merged)
    return h_out, saved, rode


def _layer_bwd(d_h, p, s, ride):
    lp = d_h.shape[0]
    bm = _pick(lp, 272, 8)
    kw = dict(bm=bm, lp=lp)
    proj = s["proj"]
    g = {}
    d_merged = matmul(d_h, p["wo"], "nt", name="d_merged", bm=lp, bn=512)
    g["w_out"] = matmul(s["merged"], d_h, "tn", name="g_w_out", bm=512, bn=1024, bk=lp)
    merge_rows = [(s["pa"], 0), (s["pb"], 0), (s["pc"], 0)] + [(proj, OFF_GATES // 512 + 2 * i) for i in range(3)]
    (d_pa, d_pb, d_pc, d_ga, d_gb, d_gc), _ = rowmap_bwd("merge_bwd", _f_merge, merge_rows, [], [d_merged], width=512, ncol=2,
                                                         d_row_dtypes=[BF16] * 6, **kw)
    g["w_branch_a"] = matmul(s["o_a"], d_pa, "tn", name="g_w_a", bm=512, bn=1024, bk=lp)
    g["w_branch_b"] = matmul(s["o_b"], d_pb, "tn", name="g_w_b", bm=512, bn=1024, bk=lp)
    g["w_branch_c"] = matmul(s["o_c"], d_pc, "tn", name="g_w_c", bm=512, bn=1024, bk=lp)
    d_oa = matmul(d_pa, p["wa"], "nt", name="d_o_a", bm=lp, bn=512)
    d_ob = matmul(d_pb, p["wb"], "nt", name="d_o_b", bm=lp, bn=512)
    d_oc = matmul(d_pc, p["wc"], "nt", name="d_o_c", bm=lp, bn=512)
    (d_oa_raw, d_sbz), _ = rowmap_bwd("gate_a_bwd", _f_gate_silu, [(s["o_a_raw"], 0), (proj, OFF_SB_Z // 1024)], [], [d_oa],
                                      width=1024, ncol=1, d_row_dtypes=[F32, BF16], **kw)
    (d_ob_raw, d_gdz), (g["gdn_norm_g"],) = rowmap_bwd(
        "gate_b_bwd", _f_head_norm_gate, [(s["o_b_raw"], 0), (proj, OFF_GDN_Z // 1024)], [(p["gdn_norm_g"], False)], [d_ob],
        width=1024, ncol=1, d_row_dtypes=[F32, BF16], **kw)
    (d_y, d_xh, d_ssz), (g_dskip, g["ssm_norm_g"]) = rowmap_bwd(
        "gate_c_bwd", _f_ssm_out, [(s["y_raw"], 0), (s["xbc"], 0), (proj, OFF_SSM_Z // 1024)],
        [(p["d_skip"], True), (p["ssm_norm_g"], True)], [d_oc], width=1024, ncol=SSM_GROUPS, d_row_dtypes=[F32, F32, BF16], **kw)
    g["gdn_norm_g"], g["ssm_norm_g"] = g["gdn_norm_g"][0], g["ssm_norm_g"][0]
    g["ssm_d"] = g_dskip.reshape(SSM_HEADS, SSM_P).sum(axis=1)
    d_q, d_k, d_v, *rode = sb_bwd(proj, (0, N_HEADS, 2 * N_HEADS), d_oa_raw, ride)
    d_qkv, d_first_gdn = gdn_bwd(s["qkv"], s["first"], s["gdn_states"], d_ob_raw)
    d_xbc_out, d_first_ssd, d_second = ssd_bwd(s["xbc"], s["first"], s["second"], s["ssd_states"], d_y, d_xh)
    d_gdqkv, g["gdn_conv_w"], _ = conv_bwd("gdn_conv_bwd", proj, OFF_GDN_QKV // HEAD_DIM, p["gdn_conv_w"], p["gdn_conv_b"], 2 * N_HEADS,
                                           d_qkv)
    d_xbc, g["ssm_conv_w"], g_cb = conv_bwd("ssm_conv_bwd", proj, OFF_SSM_XBC // HEAD_DIM, p["ssm_conv_w"], p["ssm_conv_b"], 0, d_xbc_out)
    g["ssm_conv_b"] = g_cb[0]
    d_first = d_first_gdn + d_first_ssd
    (d_small,), (g_bias, g_alog) = rowmap_bwd("gates_bwd", _f_small_gates, [(proj, OFF_SMALL // HEAD_DIM)],
                                              [(p["bias_vec"], False), (p["alog_vec"], False)], [d_first, d_second],
                                              width=HEAD_DIM, ncol=1, d_row_dtypes=[BF16], **kw)
    g["gdn_dt_bias"], g["ssm_dt_bias"] = g_bias[0, 8:16], g_bias[0, 16:48]
    g["gdn_a_log"], g["ssm_a_log"] = g_alog[0, 8:16], g_alog[0, 16:48]
    d_proj = jnp.concatenate([d_q, d_k, d_v, d_sbz, d_gdqkv, d_gdz, d_ssz, d_xbc, d_ga, d_gb, d_gc, d_small], axis=1)
    g["w_in"] = matmul(s["u"], d_proj, "tn", name="g_w_in", bm=1024, bn=640, bk=lp, out_dtype=BF16)
    d_u = matmul(d_proj, p["wp"], "nt", name="d_u", bm=lp // 2, bn=1024, bk=1600)
    (d_hn,), (g_norm,) = rowmap_bwd("rms_bwd", _f_rmsnorm, [(s["h"], 0)], [(p["norm_g"], False)], [d_u], width=D_MODEL, ncol=1, **kw)
    g["norm_g"] = g_norm[0]
    return d_h + d_hn, g, rode


def kernel(x, meta_tokens, norm_g, w_in, gdn_conv_w, gdn_a_log, gdn_dt_bias, gdn_norm_g, ssm_conv_w, ssm_conv_b, ssm_a_log, ssm_dt_bias, ssm_d, ssm_norm_g, w_branch_a, w_branch_b, w_branch_c, w_out, final_norm_g, loss_target, m_meta_tokens, m_norm_g, m_w_in, m_gdn_conv_w, m_gdn_a_log, m_gdn_dt_bias, m_gdn_norm_g, m_ssm_conv_w, m_ssm_conv_b, m_ssm_a_log, m_ssm_dt_bias, m_ssm_d, m_ssm_norm_g, m_w_branch_a, m_w_branch_b, m_w_branch_c, m_w_out, m_final_norm_g, v_meta_tokens, v_norm_g, v_w_in, v_gdn_conv_w, v_gdn_a_log, v_gdn_dt_bias, v_gdn_norm_g, v_ssm_conv_w, v_ssm_conv_b, v_ssm_a_log, v_ssm_dt_bias, v_ssm_d, v_ssm_norm_g, v_w_branch_a, v_w_branch_b, v_w_branch_c, v_w_out, v_final_norm_g):
    w = dict(meta_tokens=meta_tokens, norm_g=norm_g, w_in=w_in, gdn_conv_w=gdn_conv_w, gdn_a_log=gdn_a_log, gdn_dt_bias=gdn_dt_bias,
             gdn_norm_g=gdn_norm_g, ssm_conv_w=ssm_conv_w, ssm_conv_b=ssm_conv_b, ssm_a_log=ssm_a_log, ssm_dt_bias=ssm_dt_bias,
             ssm_d=ssm_d, ssm_norm_g=ssm_norm_g, w_branch_a=w_branch_a, w_branch_b=w_branch_b, w_branch_c=w_branch_c, w_out=w_out,
             final_norm_g=final_norm_g)
    m = dict(meta_tokens=m_meta_tokens, norm_g=m_norm_g, w_in=m_w_in, gdn_conv_w=m_gdn_conv_w, gdn_a_log=m_gdn_a_log,
             gdn_dt_bias=m_gdn_dt_bias, gdn_norm_g=m_gdn_norm_g, ssm_conv_w=m_ssm_conv_w, ssm_conv_b=m_ssm_conv_b,
             ssm_a_log=m_ssm_a_log, ssm_dt_bias=m_ssm_dt_bias, ssm_d=m_ssm_d, ssm_norm_g=m_ssm_norm_g, w_branch_a=m_w_branch_a,
             w_branch_b=m_w_branch_b, w_branch_c=m_w_branch_c, w_out=m_w_out, final_norm_g=m_final_norm_g)
    v = dict(meta_tokens=v_meta_tokens, norm_g=v_norm_g, w_in=v_w_in, gdn_conv_w=v_gdn_conv_w, gdn_a_log=v_gdn_a_log,
             gdn_dt_bias=v_gdn_dt_bias, gdn_norm_g=v_gdn_norm_g, ssm_conv_w=v_ssm_conv_w, ssm_conv_b=v_ssm_conv_b,
             ssm_a_log=v_ssm_a_log, ssm_dt_bias=v_ssm_dt_bias, ssm_d=v_ssm_d, ssm_norm_g=v_ssm_norm_g, w_branch_a=v_w_branch_a,
             w_branch_b=v_w_branch_b, w_branch_c=v_w_branch_c, w_out=v_w_out, final_norm_g=v_final_norm_g)
    depth = norm_g.shape[0]
    shard_cols = w_in.shape[-1]
    placed, placed_small = _place_weights(w)
    got_a, got_b, got_s = gather_shards("gather_first", placed[0] + [placed_small])
    exact = _exact_weights(w, got_s)

    h = jnp.concatenate([jnp.zeros((PAD, D_MODEL), F32), exact["meta_tokens"], x[0]], axis=0)
    params, saved = [], []
    for l in range(depth):
        params.append(_layer_params(w, exact, _layer_weights(got_a, got_b), l))
        h, s, rode = _layer_fwd(h, params[l], gather_ride(placed[l + 1]) if l + 1 < depth else None)
        saved.append(s)
        if rode:
            got_a, got_b = rode
    loss, d_h, g_final = loss_head(h, loss_target[0], final_norm_g[None])

    layer_grads, reds, waiting = [None] * depth, [None] * depth, None
    for l in reversed(range(depth)):
        d_h, layer_grads[l], rode = _layer_bwd(d_h, params[l], saved[l], exchange_ride(waiting) if waiting else None)
        if waiting:
            reds[l + 1] = [chip_add(f"grads_chip_add_{i}", gt, p) for i, (gt, p) in enumerate(zip(rode, waiting))]
        waiting = _start_reduce("grads", _layer_grad_buffers(layer_grads[l], shard_cols), [BF16, BF16])
    grads = {n: jnp.stack([layer_grads[l][n] for l in range(depth)]) for n in WEIGHTS
             if n not in ("meta_tokens", "final_norm_g", "w_in") + BRANCH}
    grads["meta_tokens"] = d_h[PAD:PAD + N_META]
    grads["final_norm_g"] = g_final[0]
    grad_x = d_h[PAD + N_META:][None]
    buf_s = jnp.stack([_pack([_shard(grads[n], SHARD_AXIS[n], s).astype(BF16).reshape(-1) for n in EXACT], row_unit=32) for s in range(4)])
    small = _pack([grads[n].reshape(-1) for n in REPLICATED], row_unit=32)
    last = waiting + _start_reduce("small_grads", [buf_s, jnp.broadcast_to(small[None], (4,) + small.shape)], [BF16, F32])
    got = chip_exchange("grads_chip_exchange", last)
    sums = [chip_add(f"grads_chip_add_{i}", gt, p) for i, (gt, p) in enumerate(zip(got, last))]
    reds[0] = sums[:2]
    joined = pair_join("grads_pair_join", [r for layer in reds for r in layer] + sums[2:])
    red = {"w_in": jnp.stack(joined[0:2 * depth:2])}
    pos = 0
    for n, rows in zip(BRANCH, BRANCH_ROWS):
        red[n] = jnp.stack([joined[2 * l + 1][pos:pos + rows] for l in range(depth)])
        pos += rows
    red.update(zip(EXACT, _unpack(joined[-2], [w[n].shape for n in EXACT])))
    small_red = joined[-1]
    delta, new_m, new_v = {}, {}, {}
    for n in SHARDED:
        if w[n].shape[-1] % HEAD_DIM:
            to_view, from_view = (lambda a: jnp.transpose(a, (2, 0, 1))), (lambda a: jnp.transpose(a, (1, 2, 0)))
            delta[n], new_m[n], new_v[n], red[n] = [from_view(a) for a in adamw("adamw_" + n, to_view(w[n]), to_view(red[n]),
                                                                                to_view(m[n]), to_view(v[n]), echo_g=True)]
        else:
            delta[n], new_m[n], new_v[n] = adamw("adamw_" + n, w[n], red[n], m[n], v[n])
    pack_small = lambda d: _pack([d[n].reshape(-1) for n in REPLICATED], row_unit=32)
    small = adamw("adamw_small", pack_small(w), small_red, pack_small(m), pack_small(v))
    shapes = [w[n].shape for n in REPLICATED]
    red.update(zip(REPLICATED, _unpack(small_red, shapes)))
    for d, buf in zip((delta, new_m, new_v), small):
        d.update(zip(REPLICATED, _unpack(buf, shapes)))
    total_loss = lax.psum(loss[0, 0], ("x", "y", "c"))
    return (total_loss, grad_x, *[red[n] for n in WEIGHTS], *[delta[n] for n in WEIGHTS], *[new_m[n] for n in WEIGHTS],
            *[new_v[n] for n in WEIGHTS])
```

```python
import functools
import math

import jax
import jax.numpy as jnp
from jax import lax
from jax.experimental import pallas as pl
from jax.experimental.pallas import tpu as pltpu

F32 = jnp.float32
BF16 = jnp.bfloat16

N_META = 16
RMS_EPS = 1e-6
L2_EPS = 1e-6
CONV_K = 4
D_MODEL = 1024
HEAD_DIM = 128
N_HEADS = 8
CHUNK = 64
SB_BLOCK = 128
PAD = SB_BLOCK - N_META
SSM_INNER = 2048
SSM_P = 64
SSM_HEADS = 32
SSM_GROUPS = 2
SSM_HG = SSM_HEADS // SSM_GROUPS
SSM_N = 128
VMEM_LIMIT = 56 * 1024 * 1024

def _dims(mode, ndim):
    lhs, rhs = {"nn": (1, 0), "nt": (1, 1), "tn": (0, 0)}[mode]
    off = ndim - 2
    return (((lhs + off,), (rhs + off,)), (tuple(range(off)), tuple(range(off))))


def _dot(a, b, mode):
    return lax.dot_general(a, b, _dims(mode, a.ndim), preferred_element_type=F32)


def _halves(a):
    hi = a.astype(BF16)
    return hi, (a - hi.astype(F32)).astype(BF16)


def _mm_raw(a, b, mode, kind):
    if kind == "bf16":
        return _dot(a.astype(BF16), b.astype(BF16), mode)
    if kind == "lhs01":
        hi, lo = _halves(b)
        a = a.astype(BF16)
        return _dot(a, hi, mode) + _dot(a, lo, mode)
    if kind == "rhs01":
        hi, lo = _halves(a)
        b = b.astype(BF16)
        return _dot(hi, b, mode) + _dot(lo, b, mode)
    a_hi, a_lo = _halves(a)
    b_hi, b_lo = _halves(b)
    return _dot(a_hi, b_hi, mode) + (_dot(a_hi, b_lo, mode) + _dot(a_lo, b_hi, mode))


@functools.partial(jax.custom_vjp, nondiff_argnums=(2, 3))
def _mm(a, b, mode="nn", kind="bf16"):
    return _mm_raw(a, b, mode, kind)


def _mm_fwd(a, b, mode, kind):
    return _mm_raw(a, b, mode, kind), (a, b)


def _mm_bwd(mode, kind, res, g):
    a, b = res
    if kind == "lhs01":
        return jnp.zeros_like(a), _mm_raw(a, g, {"nn": "tn", "tn": "nn"}[mode], "lhs01")
    if kind == "rhs01":
        return _mm_raw(g, b, {"nn": "nt", "nt": "nn"}[mode], "rhs01"), jnp.zeros_like(b)
    if mode == "nn":
        return _mm_raw(g, b, "nt", kind), _mm_raw(a, g, "tn", kind)
    if mode == "nt":
        return _mm_raw(g, b, "nn", kind), _mm_raw(g, a, "tn", kind)
    return _mm_raw(b, g, "nt", kind), _mm_raw(a, g, "nn", kind)


_mm.defvjp(_mm_fwd, _mm_bwd)


def _iota2(shape, axis):
    return lax.broadcasted_iota(jnp.int32, shape, axis)


def _inv_unit_lower_raw(m):
    size = m.shape[-1]
    eye = (_iota2((size, size), 0) == _iota2((size, size), 1)).astype(F32)
    n = -m
    t = eye + n
    p = n
    steps = int(math.log2(size)) - 1
    for _ in range(steps):
        p = _mm_raw(p, p, "nn", "x3")
        t = t + _mm_raw(t, p, "nn", "x3")
    return t


@jax.custom_vjp
def _inv_unit_lower(m):
    return _inv_unit_lower_raw(m)


def _inv_fwd(m):
    t = _inv_unit_lower_raw(m)
    return t, t


def _inv_bwd(t, g):
    return (-_mm_raw(_mm_raw(t, g, "tn", "x3"), t, "nt", "x3"),)


_inv_unit_lower.defvjp(_inv_fwd, _inv_bwd)


def _safe_decay(col, row, keep):
    return jnp.where(keep, jnp.exp(jnp.where(keep, col - row, 0.0)), 0.0)


def _col_to_row(col):
    n = col.shape[-2]
    eye = _iota2((n, n), 0) == _iota2((n, n), 1)
    return jnp.sum(jnp.where(eye, col, 0.0), axis=-2, keepdims=True)


def _cumsum_col(col):
    n = col.shape[-2]
    li, si = _iota2((n, n), 0), _iota2((n, n), 1)
    row = _col_to_row(col)
    c_col = jnp.sum(jnp.where(li >= si, row, 0.0), axis=-1, keepdims=True)
    c_row = jnp.sum(jnp.where(li <= si, col, 0.0), axis=-2, keepdims=True)
    return c_col, c_row


def _gdn_chunk(q, k, v, g, beta, state):
    cl = q.shape[-2]
    li, si = _iota2((cl, cl), 0), _iota2((cl, cl), 1)
    gc_col, gc_row = _cumsum_col(g)
    g_last = jnp.sum(g, axis=-2, keepdims=True)
    dec_strict = _safe_decay(gc_col, gc_row, li > si)
    dec_incl = _safe_decay(gc_col, gc_row, li >= si)
    e_gc = jnp.exp(gc_col)
    qs = q * (HEAD_DIM ** -0.5)
    kb = k * beta
    m = _mm(kb, k, "nt") * dec_strict
    t_inv = _inv_unit_lower(m)
    u = _mm(t_inv, v * beta)
    w = _mm(t_inv, kb * e_gc)
    a_qk = _mm(qs, k, "nt") * dec_incl
    q_dec = qs * e_gc
    k_end = k * jnp.exp(g_last - gc_col)
    v_new = u - _mm(w, state)
    o = _mm(q_dec, state) + _mm(a_qk, v_new)
    new_state = state * jnp.exp(g_last) + _mm(k_end, v_new, "tn")
    return o, new_state


def _gdn_operands(qkv_ref, gt):
    nh, width = N_HEADS, N_HEADS * HEAD_DIM
    heads = lambda off: jnp.stack([qkv_ref[:, off + h * HEAD_DIM:off + (h + 1) * HEAD_DIM] for h in range(nh)])
    cols = lambda off: jnp.stack([gt[:, off + h:off + h + 1] for h in range(nh)])
    return heads(0), heads(width), heads(2 * width), cols(nh), cols(0)


def gdn_fwd(qkv, gates):
    lp = qkv.shape[0]
    nh = N_HEADS
    nc = lp // CHUNK
    width = nh * HEAD_DIM

    def body(qkv_ref, gt_ref, o_ref, s_ref, state):
        @pl.when(pl.program_id(0) == 0)
        def _():
            state[...] = jnp.zeros_like(state)

        s_in = state[...]
        s_ref[0] = s_in
        o, s_new = _gdn_chunk(*_gdn_operands(qkv_ref, gt_ref[...]), s_in)
        for h in range(nh):
            o_ref[:, h * HEAD_DIM:(h + 1) * HEAD_DIM] = o[h]
        state[...] = s_new

    return pl.pallas_call(
        body, name="gdn_fwd", grid=(nc,),
        in_specs=[pl.BlockSpec((CHUNK, 3 * width), lambda c: (c, 0)), pl.BlockSpec((CHUNK, HEAD_DIM), lambda c: (c, 0))],
        out_specs=[pl.BlockSpec((CHUNK, width), lambda c: (c, 0)), pl.BlockSpec((1, nh, HEAD_DIM, HEAD_DIM), lambda c: (c, 0, 0, 0))],
        out_shape=[jax.ShapeDtypeStruct((lp, width), F32), jax.ShapeDtypeStruct((nc, nh, HEAD_DIM, HEAD_DIM), F32)],
        scratch_shapes=[pltpu.VMEM((nh, HEAD_DIM, HEAD_DIM), F32)],
        compiler_params=pltpu.CompilerParams(dimension_semantics=("arbitrary",), vmem_limit_bytes=VMEM_LIMIT),
    )(qkv, gates)


def gdn_bwd(qkv, gates, states, d_o):
    lp = qkv.shape[0]
    nh = N_HEADS
    nc = lp // CHUNK
    width = nh * HEAD_DIM

    def body(qkv_ref, gt_ref, s_ref, do_ref, dqkv_ref, dgt_ref, d_state):
        @pl.when(pl.program_id(0) == 0)
        def _():
            d_state[...] = jnp.zeros_like(d_state)

        _, pull = jax.vjp(_gdn_chunk, *_gdn_operands(qkv_ref, gt_ref[...]), s_ref[0])
        d_o = jnp.stack([do_ref[:, h * HEAD_DIM:(h + 1) * HEAD_DIM] for h in range(nh)])
        dq, dk, dv, dg, db, ds = pull((d_o, d_state[...]))
        lane = _iota2((CHUNK, HEAD_DIM), 1)
        d_gt = jnp.zeros((CHUNK, HEAD_DIM), F32)
        for h in range(nh):
            for part, val in enumerate((dq, dk, dv)):
                dqkv_ref[:, part * width + h * HEAD_DIM:part * width + (h + 1) * HEAD_DIM] = val[h]
            d_gt = d_gt + jnp.where(lane == h, db[h], 0.0) + jnp.where(lane == nh + h, dg[h], 0.0)
        dgt_ref[...] = d_gt
        d_state[...] = ds

    rev = lambda c: (nc - 1 - c, 0)
    return pl.pallas_call(
        body, name="gdn_bwd", grid=(nc,),
        in_specs=[pl.BlockSpec((CHUNK, 3 * width), rev), pl.BlockSpec((CHUNK, HEAD_DIM), rev),
                  pl.BlockSpec((1, nh, HEAD_DIM, HEAD_DIM), lambda c: (nc - 1 - c, 0, 0, 0)), pl.BlockSpec((CHUNK, width), rev)],
        out_specs=[pl.BlockSpec((CHUNK, 3 * width), rev), pl.BlockSpec((CHUNK, HEAD_DIM), rev)],
        out_shape=[jax.ShapeDtypeStruct((lp, 3 * width), F32), jax.ShapeDtypeStruct((lp, HEAD_DIM), F32)],
        scratch_shapes=[pltpu.VMEM((nh, HEAD_DIM, HEAD_DIM), F32)],
        compiler_params=pltpu.CompilerParams(dimension_semantics=("arbitrary",), vmem_limit_bytes=VMEM_LIMIT),
    )(qkv, gates, states, d_o)


def _head_expand():
    width = SSM_HG * SSM_P
    return (_iota2((SSM_HG, width), 1) // SSM_P == _iota2((SSM_HG, width), 0)).astype(F32)


def _ssd_chunk(x, b, c, dt, la, state):
    cl = x.shape[0]
    li, si = _iota2((cl, cl), 0), _iota2((cl, cl), 1)
    causal = li >= si
    expand = _head_expand()
    tri = causal.astype(F32)
    xs = x * _mm(dt, expand, "nn", "rhs01")
    la_x = _mm(la, expand, "nn", "rhs01")
    cs_x = _mm(tri, la_x, "nn", "lhs01")
    last_x = jnp.sum(la_x, axis=0, keepdims=True)
    cs = _mm(tri, la, "nn", "lhs01")
    scores = _mm(c, b, "nt")
    head_id = _iota2((1, SSM_HG), 1)
    per_tile = HEAD_DIM // SSM_P
    tile_head = _iota2((1, HEAD_DIM), 1) // SSM_P
    within = []
    for t in range(SSM_HG // per_tile):
        xs_t = xs[:, t * HEAD_DIM:(t + 1) * HEAD_DIM]
        acc = jnp.zeros((cl, HEAD_DIM), F32)
        for hh in range(per_tile):
            cs_col = jnp.sum(jnp.where(head_id == t * per_tile + hh, cs, 0.0), axis=1, keepdims=True)
            decay = _safe_decay(cs_col, _col_to_row(cs_col), causal)
            acc = acc + _mm(scores * decay, jnp.where(tile_head == hh, xs_t, 0.0))
        within.append(acc)
    y = _mm(c, state) * jnp.exp(cs_x) + jnp.concatenate(within, axis=1)
    new_state = state * jnp.exp(last_x) + _mm(b, xs * jnp.exp(last_x - cs_x), "tn")
    return y, new_state


GATE_DT = 16


def _place_lanes(v, lo):
    n = v.shape[1]
    sel = (_iota2((n, HEAD_DIM), 1) == _iota2((n, HEAD_DIM), 0) + lo).astype(F32)
    return _mm_raw(v, sel, "nn", "rhs01")


def ssd_fwd(xbc, first, second):
    lp = xbc.shape[0]
    nc = lp // CHUNK
    width = SSM_HG * SSM_P
    b_off, c_off = SSM_INNER, SSM_INNER + SSM_GROUPS * SSM_N

    def body(x_ref, f_ref, s2_ref, y_ref, s_ref, state):
        @pl.when(pl.program_id(0) == 0)
        def _():
            state[...] = jnp.zeros_like(state)

        f, s2 = f_ref[...], s2_ref[...]
        for g in range(SSM_GROUPS):
            lo = GATE_DT + g * SSM_HG
            s_in = state[g]
            s_ref[0, g] = s_in
            y, s_new = _ssd_chunk(x_ref[:, g * width:(g + 1) * width], x_ref[:, b_off + g * SSM_N:b_off + (g + 1) * SSM_N],
                                  x_ref[:, c_off + g * SSM_N:c_off + (g + 1) * SSM_N], f[:, lo:lo + SSM_HG], s2[:, lo:lo + SSM_HG], s_in)
            y_ref[:, g * width:(g + 1) * width] = y
            state[g] = s_new

    row = lambda cols: pl.BlockSpec((CHUNK, cols), lambda k: (k, 0))
    return pl.pallas_call(
        body, name="ssd_fwd", grid=(nc,),
        in_specs=[row(xbc.shape[1]), row(HEAD_DIM), row(HEAD_DIM)],
        out_specs=[row(SSM_INNER), pl.BlockSpec((1, SSM_GROUPS, SSM_N, width), lambda k: (k, 0, 0, 0))],
        out_shape=[jax.ShapeDtypeStruct((lp, SSM_INNER), F32), jax.ShapeDtypeStruct((nc, SSM_GROUPS, SSM_N, width), F32)],
        scratch_shapes=[pltpu.VMEM((SSM_GROUPS, SSM_N, width), F32)],
        compiler_params=pltpu.CompilerParams(dimension_semantics=("arbitrary",), vmem_limit_bytes=VMEM_LIMIT),
    )(xbc, first, second)


def ssd_bwd(xbc, first, second, states, d_y, d_xh):
    lp = xbc.shape[0]
    nc = lp // CHUNK
    width = SSM_HG * SSM_P
    b_off, c_off = SSM_INNER, SSM_INNER + SSM_GROUPS * SSM_N

    def body(x_ref, f_ref, s2_ref, s_ref, dy_ref, dxh_ref, dx_ref, df_ref, ds2_ref, d_state):
        @pl.when(pl.program_id(0) == 0)
        def _():
            d_state[...] = jnp.zeros_like(d_state)

        f, s2 = f_ref[...], s2_ref[...]
        d_f = jnp.zeros((CHUNK, HEAD_DIM), F32)
        d_s2 = jnp.zeros((CHUNK, HEAD_DIM), F32)
        for g in range(SSM_GROUPS):
            lo = GATE_DT + g * SSM_HG
            x_l = slice(g * width, (g + 1) * width)
            b_l = slice(b_off + g * SSM_N, b_off + (g + 1) * SSM_N)
            c_l = slice(c_off + g * SSM_N, c_off + (g + 1) * SSM_N)
            _, pull = jax.vjp(_ssd_chunk, x_ref[:, x_l], x_ref[:, b_l], x_ref[:, c_l], f[:, lo:lo + SSM_HG], s2[:, lo:lo + SSM_HG],
                              s_ref[0, g])
            dx, db, dc, ddt, dla, ds = pull((dy_ref[:, x_l], d_state[g]))
            dx_ref[:, x_l] = dx + dxh_ref[:, x_l]
            dx_ref[:, b_l] = db
            dx_ref[:, c_l] = dc
            d_f = d_f + _place_lanes(ddt, lo)
            d_s2 = d_s2 + _place_lanes(dla, lo)
            d_state[g] = ds
        df_ref[...] = d_f
        ds2_ref[...] = d_s2

    row = lambda cols: pl.BlockSpec((CHUNK, cols), lambda k: (nc - 1 - k, 0))
    gate_shape = jax.ShapeDtypeStruct((lp, HEAD_DIM), F32)
    return pl.pallas_call(
        body, name="ssd_bwd", grid=(nc,),
        in_specs=[row(xbc.shape[1]), row(HEAD_DIM), row(HEAD_DIM),
                  pl.BlockSpec((1, SSM_GROUPS, SSM_N, width), lambda k: (nc - 1 - k, 0, 0, 0)), row(SSM_INNER), row(SSM_INNER)],
        out_specs=[row(xbc.shape[1]), row(HEAD_DIM), row(HEAD_DIM)],
        out_shape=[jax.ShapeDtypeStruct(xbc.shape, F32), gate_shape, gate_shape],
        scratch_shapes=[pltpu.VMEM((SSM_GROUPS, SSM_N, width), F32)],
        compiler_params=pltpu.CompilerParams(dimension_semantics=("arbitrary",), vmem_limit_bytes=VMEM_LIMIT),
    )(xbc, first, second, states, d_y, d_xh)


SB_QROWS = 544


def _mm_tri(a, tri):
    return _mm_raw(a, tri.astype(BF16), "nn", "rhs01")


def _sb_scores(q_scaled, kb, row0, j):
    shape = (q_scaled.shape[0], SB_BLOCK)
    z = _mm_raw(q_scaled, kb, "nt", "bf16")
    q_pos = row0 + _iota2(shape, 0)
    k_pos = j * SB_BLOCK + _iota2(shape, 1)
    valid = (k_pos < q_pos) & (k_pos >= PAD)
    sp = jnp.maximum(z, 0.0) + jnp.log(1.0 + jnp.exp(-jnp.abs(z)))
    lk = jnp.where(valid, -sp, 0.0)
    return z, sp, valid, lk


SB_DEAD = -110.0


def sb_fwd(src, offs, ride=None):
    lp = src.shape[0]
    nh = N_HEADS
    qb = _pick(lp, SB_QROWS, 8)
    scale = HEAD_DIM ** -0.5
    blk = SB_BLOCK

    n_in = len(ride.ins) if ride else 0
    n_out = len(ride.out_shapes) if ride else 0

    def body(*refs):
        q_ref, k_ref, v_ref = refs[:3]
        o_ref = refs[3 + n_in]
        ride_refs = (refs[3:3 + n_in], refs[4 + n_in:4 + n_in + n_out], refs[4 + n_in + n_out:])
        i = pl.program_id(1)
        if ride:
            @pl.when((pl.program_id(0) == 0) & (i == 0))
            def _():
                ride.start(*ride_refs)
        q_scaled = q_ref[...] * scale
        upper = _iota2((blk, blk), 0) > _iota2((blk, blk), 1)
        n_blocks = ((i + 1) * qb + blk - 1) // blk

        def live(state):
            it, _, c = state
            return (it < n_blocks) & (jnp.max(c) > SB_DEAD)

        def step(state):
            it, acc, c = state
            j = n_blocks - 1 - it
            rows = pl.ds(pl.multiple_of(j * blk, blk), blk)
            z, sp, valid, lk = _sb_scores(q_scaled, k_ref[rows, :], i * qb, j)
            later = _mm_tri(lk, upper) + c
            w = jnp.where(valid, jnp.exp(z - sp + later), 0.0)
            acc = acc + _mm_raw(w, v_ref[rows, :], "nn", "bf16")
            return it + 1, acc, c + jnp.sum(lk, axis=1, keepdims=True)

        _, acc, _ = lax.while_loop(live, step, (jnp.int32(0), jnp.zeros((qb, HEAD_DIM), F32), jnp.zeros((qb, 1), F32)))
        o_ref[...] = acc
        if ride:
            @pl.when((pl.program_id(0) == nh - 1) & (i == lp // qb - 1))
            def _():
                ride.finish(*ride_refs)

    qspec = pl.BlockSpec((qb, HEAD_DIM), lambda h, i: (i, offs[0] + h))
    kspec = pl.BlockSpec((lp, HEAD_DIM), lambda h, i: (0, offs[1] + h))
    vspec = pl.BlockSpec((lp, HEAD_DIM), lambda h, i: (0, offs[2] + h))
    ospec = pl.BlockSpec((qb, HEAD_DIM), lambda h, i: (i, h))
    return pl.pallas_call(
        body, name="sb_fwd", grid=(nh, lp // qb), in_specs=[qspec, kspec, vspec] + [ANY] * n_in, out_specs=[ospec] + [ANY] * n_out,
        out_shape=[jax.ShapeDtypeStruct((lp, nh * HEAD_DIM), F32)] + (ride.out_shapes if ride else []),
        scratch_shapes=ride.sems if ride else [],
        input_output_aliases={3 + k: 1 + k for k in range(n_in)} if ride and ride.alias else {},
        compiler_params=pltpu.CompilerParams(dimension_semantics=("arbitrary", "arbitrary"), vmem_limit_bytes=VMEM_LIMIT),
    )(src, src, src, *(ride.ins if ride else []))


def sb_bwd(src, offs, d_o, ride=None):
    lp = src.shape[0]
    nh = N_HEADS
    qb = _pick(lp, SB_QROWS, 8)
    scale = HEAD_DIM ** -0.5
    blk = SB_BLOCK

    n_in = len(ride.ins) if ride else 0
    n_out = len(ride.out_shapes) if ride else 0

    def body(*refs):
        q_ref, k_ref, v_ref, do_ref = refs[:4]
        dq_ref, dk_out, dv_out = refs[4 + n_in:7 + n_in]
        ride_refs = (refs[4:4 + n_in], refs[7 + n_in:7 + n_in + n_out], refs[9 + n_in + n_out:])
        dk_ref, dv_ref = refs[7 + n_in + n_out:9 + n_in + n_out]
        i = pl.program_id(1)
        if ride:
            @pl.when((pl.program_id(0) == 0) & (i == 0))
            def _():
                ride.start(*ride_refs)

        @pl.when(i == 0)
        def _():
            dk_ref[...] = jnp.zeros_like(dk_ref)
            dv_ref[...] = jnp.zeros_like(dv_ref)

        q_scaled = q_ref[...] * scale
        d_out = do_ref[...]
        lower_incl = _iota2((blk, blk), 0) <= _iota2((blk, blk), 1)
        lower = _iota2((blk, blk), 0) < _iota2((blk, blk), 1)
        n_blocks = ((i + 1) * qb + blk - 1) // blk

        def live(state):
            it, c = state
            return (it < n_blocks) & (jnp.max(c) > SB_DEAD)

        def count(state):
            it, c = state
            rows = pl.ds(pl.multiple_of((n_blocks - 1 - it) * blk, blk), blk)
            _, _, _, lk = _sb_scores(q_scaled, k_ref[rows, :], i * qb, n_blocks - 1 - it)
            return it + 1, c + jnp.sum(lk, axis=1, keepdims=True)

        n_live, total = lax.while_loop(live, count, (jnp.int32(0), jnp.zeros((qb, 1), F32)))

        def step(j, carry):
            acc, cp, ep = carry
            rows = pl.ds(pl.multiple_of(j * blk, blk), blk)
            kb = k_ref[rows, :]
            vb = v_ref[rows, :]
            z, sp, valid, lk = _sb_scores(q_scaled, kb, i * qb, j)
            later = total - cp - _mm_tri(lk, lower_incl)
            w = jnp.where(valid, jnp.exp(z - sp + later), 0.0)
            e = w * _mm_raw(d_out, vb, "nt", "bf16")
            before = ep + _mm_tri(e, lower)
            dz = jnp.where(valid, e * jnp.exp(-sp) - before * jnp.exp(z - sp), 0.0)
            dk_ref[rows, :] += _mm_raw(dz, q_scaled, "tn", "bf16")
            dv_ref[rows, :] += _mm_raw(w, d_out, "tn", "bf16")
            acc = acc + _mm_raw(dz, kb, "nn", "bf16")
            return acc, cp + jnp.sum(lk, axis=1, keepdims=True), ep + jnp.sum(e, axis=1, keepdims=True)

        zero_col = jnp.zeros((qb, 1), F32)
        acc, _, _ = lax.fori_loop(n_blocks - n_live, n_blocks, step, (jnp.zeros((qb, HEAD_DIM), F32), zero_col, zero_col))
        dq_ref[...] = (acc * scale).astype(dq_ref.dtype)

        @pl.when(i == lp // qb - 1)
        def _():
            dk_out[...] = dk_ref[...].astype(dk_out.dtype)
            dv_out[...] = dv_ref[...].astype(dv_out.dtype)

        if ride:
            @pl.when((pl.program_id(0) == nh - 1) & (i == lp // qb - 1))
            def _():
                ride.finish(*ride_refs)

    qspec = pl.BlockSpec((qb, HEAD_DIM), lambda h, i: (i, offs[0] + h))
    kspec = pl.BlockSpec((lp, HEAD_DIM), lambda h, i: (0, offs[1] + h))
    vspec = pl.BlockSpec((lp, HEAD_DIM), lambda h, i: (0, offs[2] + h))
    ospec = pl.BlockSpec((qb, HEAD_DIM), lambda h, i: (i, h))
    fullspec = pl.BlockSpec((lp, HEAD_DIM), lambda h, i: (0, h))
    return pl.pallas_call(
        body, name="sb_bwd", grid=(nh, lp // qb), in_specs=[qspec, kspec, vspec, ospec] + [ANY] * n_in,
        out_specs=[ospec, fullspec, fullspec] + [ANY] * n_out,
        out_shape=[jax.ShapeDtypeStruct((lp, nh * HEAD_DIM), BF16)] * 3 + (ride.out_shapes if ride else []),
        scratch_shapes=[pltpu.VMEM((lp, HEAD_DIM), F32)] * 2 + (ride.sems if ride else []),
        input_output_aliases={4 + k: 3 + k for k in range(n_in)} if ride and ride.alias else {},
        compiler_params=pltpu.CompilerParams(dimension_semantics=("arbitrary", "arbitrary"), vmem_limit_bytes=VMEM_LIMIT),
    )(src, src, src, d_o, *(ride.ins if ride else []))


def _pick(n, target, unit):
    if n <= target:
        return n
    best = None
    for d in range(unit, target + 1, unit):
        if n % d == 0:
            best = d
    assert best is not None, (n, target, unit)
    return best


def matmul(a, b, mode="nn", *, name, bm=1088, bn=640, bk=2176, residual=None, out_dtype=F32, b_koff=0):
    if mode == "nn":
        (m, k), n = a.shape, b.shape[1]
    elif mode == "nt":
        (m, k), n = a.shape, b.shape[0]
    else:
        (k, m), n = a.shape, b.shape[1]
    assert b_koff == 0 or mode == "nt"
    bm = _pick(m, bm, 128 if mode == "tn" else 8)
    bn = _pick(n, bn, 128 if mode != "nt" else 8)
    bk = _pick(k, bk, 128 if mode != "tn" else 8)
    nk = k // bk

    def body(*refs):
        if residual is None:
            a_ref, b_ref, o_ref, acc = refs
            r_ref = None
        else:
            a_ref, b_ref, r_ref, o_ref, acc = refs
        kk = pl.program_id(2)
        part = _mm_raw(a_ref[...], b_ref[...], mode, "bf16")

        @pl.when(kk == 0)
        def _():
            acc[...] = part

        @pl.when(kk > 0)
        def _():
            acc[...] += part

        @pl.when(kk == nk - 1)
        def _():
            res = acc[...]
            if r_ref is not None:
                res = res + r_ref[...]
            o_ref[...] = res.astype(out_dtype)

    a_spec = pl.BlockSpec((bk, bm), lambda i, j, kk: (kk, i)) if mode == "tn" else pl.BlockSpec((bm, bk), lambda i, j, kk: (i, kk))
    b_spec = pl.BlockSpec((bn, bk), lambda i, j, kk: (j, b_koff + kk)) if mode == "nt" else pl.BlockSpec((bk, bn), lambda i, j, kk: (kk, j))
    o_spec = pl.BlockSpec((bm, bn), lambda i, j, kk: (i, j))
    ins, specs = [a, b], [a_spec, b_spec]
    if residual is not None:
        ins.append(residual)
        specs.append(o_spec)
    return pl.pallas_call(
        body, name=name, grid=(m // bm, n // bn, nk), in_specs=specs, out_specs=o_spec,
        out_shape=jax.ShapeDtypeStruct((m, n), out_dtype),
        scratch_shapes=[pltpu.VMEM((bm, bn), F32)],
        compiler_params=pltpu.CompilerParams(dimension_semantics=("arbitrary", "arbitrary", "arbitrary"), vmem_limit_bytes=VMEM_LIMIT),
    )(*ins)


def _row_specs(rows, params, width, bm):
    row_specs = [pl.BlockSpec((bm, width), (lambda j, i, off=off: (i, off + j))) for _, off in rows]
    par_specs = [pl.BlockSpec((p.shape[0], width) if per_col else p.shape, ((lambda j, i: (0, j)) if per_col else (lambda j, i: (0, 0))))
                 for p, per_col in params]
    return row_specs, par_specs


def rowmap_fwd(name, fn, rows, params, n_out, *, width, ncol, bm, lp, out_dtypes=None):
    out_dtypes = out_dtypes or [F32] * n_out
    row_specs, par_specs = _row_specs(rows, params, width, bm)
    nr = len(rows)

    def body(*refs):
        ins, outs = refs[:nr + len(params)], refs[nr + len(params):]
        row_ids = pl.program_id(1) * bm + _iota2((bm, 1), 0)
        res = fn(row_ids, *[r[...].astype(F32) for r in ins])
        for o_ref, val in zip(outs, res):
            o_ref[...] = val.astype(o_ref.dtype)

    o_spec = pl.BlockSpec((bm, width), lambda j, i: (i, j))
    return pl.pallas_call(
        body, name=name, grid=(ncol, lp // bm), in_specs=row_specs + par_specs, out_specs=[o_spec] * n_out,
        out_shape=[jax.ShapeDtypeStruct((lp, ncol * width), dt) for dt in out_dtypes],
        compiler_params=pltpu.CompilerParams(dimension_semantics=("arbitrary", "arbitrary"), vmem_limit_bytes=VMEM_LIMIT),
    )(*[a for a, _ in rows], *[p for p, _ in params])


def rowmap_bwd(name, fn, rows, params, d_outs, *, width, ncol, bm, lp, d_row_dtypes=None):
    d_row_dtypes = d_row_dtypes or [F32] * len(rows)
    row_specs, par_specs = _row_specs(rows, params, width, bm)
    nr, npar, nout = len(rows), len(params), len(d_outs)

    def body(*refs):
        ins = refs[:nr + npar]
        dos = refs[nr + npar:nr + npar + nout]
        d_rows = refs[nr + npar + nout:nr + npar + nout + nr]
        d_pars = refs[nr + npar + nout + nr:]
        j, i = pl.program_id(0), pl.program_id(1)
        row_ids = i * bm + _iota2((bm, 1), 0)
        _, pull = jax.vjp(lambda *xs: tuple(fn(row_ids, *xs)), *[r[...].astype(F32) for r in ins])
        grads = pull(tuple(d[...].astype(F32) for d in dos))
        for ref, val in zip(d_rows, grads[:nr]):
            ref[...] = val.astype(ref.dtype)
        for ref, val, (_, per_col) in zip(d_pars, grads[nr:], params):
            first = (i == 0) if per_col else ((i == 0) & (j == 0))

            @pl.when(first)
            def _(ref=ref, val=val):
                ref[...] = val

            @pl.when(jnp.logical_not(first))
            def _(ref=ref, val=val):
                ref[...] += val

    o_spec = pl.BlockSpec((bm, width), lambda j, i: (i, j))
    res = pl.pallas_call(
        body, name=name, grid=(ncol, lp // bm), in_specs=row_specs + par_specs + [o_spec] * nout,
        out_specs=[o_spec] * nr + par_specs,
        out_shape=[jax.ShapeDtypeStruct((lp, ncol * width), dt) for dt in d_row_dtypes]
        + [jax.ShapeDtypeStruct(p.shape, F32) for p, _ in params],
        compiler_params=pltpu.CompilerParams(dimension_semantics=("arbitrary", "arbitrary"), vmem_limit_bytes=VMEM_LIMIT),
    )(*[a for a, _ in rows], *[p for p, _ in params], *d_outs)
    return res[:nr], res[nr:]


def _silu(x):
    return x * jax.nn.sigmoid(x)


def _softplus(x):
    return jnp.maximum(x, 0.0) + jnp.log(1.0 + jnp.exp(-jnp.abs(x)))


def _real_rows(row_ids):
    return (row_ids >= PAD).astype(F32)


def _f_rmsnorm(row_ids, h, g):
    return (h * lax.rsqrt(jnp.mean(h * h, axis=-1, keepdims=True) + RMS_EPS) * g,)


def _f_small_gates(row_ids, small, bias, a_log):
    lane = _iota2(small.shape, 1)
    t = small + bias
    sp = _softplus(t)
    coef = -jnp.exp(a_log)
    keep = _real_rows(row_ids)
    first = jnp.where(lane < 8, jax.nn.sigmoid(t), jnp.where(lane < 16, coef * sp, jnp.where(lane < 48, sp, 0.0)))
    second = jnp.where((lane >= 8) & (lane < 48), coef * sp, 0.0)
    return first * keep, second * keep


def _f_gate_silu(row_ids, o, z):
    return (o * _silu(z),)


def _f_head_norm_gate(row_ids, o, z, g):
    out = []
    for h in range(o.shape[1] // HEAD_DIM):
        oh = o[:, h * HEAD_DIM:(h + 1) * HEAD_DIM]
        out.append(oh * lax.rsqrt(jnp.mean(oh * oh, axis=-1, keepdims=True) + RMS_EPS) * g)
    return (jnp.concatenate(out, axis=1) * _silu(z),)


def _f_ssm_out(row_ids, y, xh, z, d_skip, g):
    t = (y + d_skip * xh) * _silu(z)
    return (t * lax.rsqrt(jnp.mean(t * t, axis=-1, keepdims=True) + RMS_EPS) * g,)


def _f_merge(row_ids, pa, pb, pc, ga, gb, gc):
    return (jax.nn.sigmoid(ga) * pa + jax.nn.sigmoid(gb) * pb + jax.nn.sigmoid(gc) * pc,)


CONV_TILE = SB_BLOCK
HALO = 8


def _conv_taps(x_ref, t):
    start = pl.multiple_of(jnp.maximum(t * CONV_TILE - HALO, 0), HALO)
    raw = x_ref[pl.ds(start, CONV_TILE + HALO), :]
    ext = jnp.where(t == 0, pltpu.roll(raw, HALO, 0), raw)
    return [(ext if s == 0 else pltpu.roll(ext, s, 0))[HALO:] for s in range(CONV_K - 1, -1, -1)]


def _conv_pre(taps, w, b):
    pre = b
    for kk in range(CONV_K):
        pre = pre + w[kk:kk + 1, :] * taps[kk]
    return pre


def _conv_post(pre, l2_flag, keep):
    act = _silu(pre)
    nrm = act * lax.rsqrt(jnp.sum(act * act, axis=-1, keepdims=True) + L2_EPS)
    return (l2_flag * nrm + (1.0 - l2_flag) * act) * keep


def _tile_rows(t):
    return pl.ds(pl.multiple_of(t * CONV_TILE, CONV_TILE), CONV_TILE), _real_rows(t * CONV_TILE + _iota2((CONV_TILE, 1), 0))


def conv_fwd(name, src, col_off, w, b, n_l2):
    lp, ch = src.shape[0], w.shape[1]

    def body(x_ref, w_ref, b_ref, o_ref):
        l2_flag = (pl.program_id(0) < n_l2).astype(F32)
        wv, bv = w_ref[...], b_ref[...]

        def tile(t, carry):
            rows, keep = _tile_rows(t)
            o_ref[rows, :] = _conv_post(_conv_pre(_conv_taps(x_ref, t), wv, bv), l2_flag, keep)
            return carry

        lax.fori_loop(0, lp // CONV_TILE, tile, 0)

    return pl.pallas_call(
        body, name=name, grid=(ch // HEAD_DIM,),
        in_specs=[pl.BlockSpec((lp, HEAD_DIM), lambda j: (0, col_off + j)), pl.BlockSpec((CONV_K, HEAD_DIM), lambda j: (0, j)),
                  pl.BlockSpec((1, HEAD_DIM), lambda j: (0, j))],
        out_specs=pl.BlockSpec((lp, HEAD_DIM), lambda j: (0, j)),
        out_shape=jax.ShapeDtypeStruct((lp, ch), F32),
        compiler_params=pltpu.CompilerParams(dimension_semantics=("arbitrary",), vmem_limit_bytes=VMEM_LIMIT),
    )(src, w, b)


def conv_bwd(name, src, col_off, w, b, n_l2, d_out):
    lp, ch = src.shape[0], w.shape[1]
    n_tiles = lp // CONV_TILE

    def body(x_ref, w_ref, b_ref, do_ref, dx_ref, dw_ref, db_ref):
        l2_flag = (pl.program_id(0) < n_l2).astype(F32)
        wv, bv = w_ref[...], b_ref[...]

        def tile(it, carry):
            after, d_w, d_b = carry
            t = n_tiles - 1 - it
            rows, keep = _tile_rows(t)
            taps = _conv_taps(x_ref, t)
            _, pull = jax.vjp(lambda p: _conv_post(p, l2_flag, keep), _conv_pre(taps, wv, bv))
            (d_pre,) = pull(do_ref[rows, :])
            ext = jnp.concatenate([d_pre, after], axis=0)
            dx = jnp.zeros_like(d_pre)
            for kk in range(CONV_K):
                s = CONV_K - 1 - kk
                dx = dx + wv[kk:kk + 1, :] * (d_pre if s == 0 else pltpu.roll(ext, CONV_TILE + HALO - s, 0)[:CONV_TILE])
            dx_ref[rows, :] = (dx * keep).astype(dx_ref.dtype)
            d_w = [acc + jnp.sum(d_pre * tap, axis=0, keepdims=True) for acc, tap in zip(d_w, taps)]
            return d_pre[:HALO], d_w, d_b + jnp.sum(d_pre, axis=0, keepdims=True)

        zero = jnp.zeros((1, HEAD_DIM), F32)
        _, d_w, d_b = lax.fori_loop(0, n_tiles, tile, (jnp.zeros((HALO, HEAD_DIM), F32), [zero] * CONV_K, zero))
        for kk in range(CONV_K):
            dw_ref[kk:kk + 1, :] = d_w[kk]
        db_ref[...] = d_b

    seq = pl.BlockSpec((lp, HEAD_DIM), lambda j: (0, j))
    wspec = pl.BlockSpec((CONV_K, HEAD_DIM), lambda j: (0, j))
    bspec = pl.BlockSpec((1, HEAD_DIM), lambda j: (0, j))
    return pl.pallas_call(
        body, name=name, grid=(ch // HEAD_DIM,),
        in_specs=[pl.BlockSpec((lp, HEAD_DIM), lambda j: (0, col_off + j)), wspec, bspec, seq],
        out_specs=[seq, wspec, bspec],
        out_shape=[jax.ShapeDtypeStruct((lp, ch), BF16), jax.ShapeDtypeStruct(w.shape, F32), jax.ShapeDtypeStruct(b.shape, F32)],
        compiler_params=pltpu.CompilerParams(dimension_semantics=("arbitrary",), vmem_limit_bytes=VMEM_LIMIT),
    )(src, w, b, d_out)


def loss_head(h, target, g):
    lp, d = h.shape
    bm = SB_BLOCK
    first = (PAD + N_META) // bm

    def body(h_ref, t_ref, g_ref, loss_ref, dh_ref, dg_ref):
        i = pl.program_id(0)
        keep = (i >= first).astype(F32)

        def f(hv, gv):
            y = hv * lax.rsqrt(jnp.mean(hv * hv, axis=-1, keepdims=True) + RMS_EPS) * gv
            err = y - t_ref[...]
            return 0.5 * jnp.sum(jnp.mean(err * err, axis=-1, keepdims=True), axis=0, keepdims=True) * keep

        val, pull = jax.vjp(f, h_ref[...], g_ref[...])
        dh, dg = pull(jnp.ones((1, 1), F32))
        dh_ref[...] = dh

        @pl.when(i == 0)
        def _():
            loss_ref[...] = val
            dg_ref[...] = dg

        @pl.when(i > 0)
        def _():
            loss_ref[...] += val
            dg_ref[...] += dg

    row = pl.BlockSpec((bm, d), lambda i: (i, 0))
    return pl.pallas_call(
        body, name="loss_head", grid=(lp // bm,),
        in_specs=[row, pl.BlockSpec((bm, d), lambda i: (jnp.maximum(i - first, 0), 0)), pl.BlockSpec((1, d), lambda i: (0, 0))],
        out_specs=[pl.BlockSpec((1, 1), lambda i: (0, 0)), row, pl.BlockSpec((1, d), lambda i: (0, 0))],
        out_shape=[jax.ShapeDtypeStruct((1, 1), F32), jax.ShapeDtypeStruct((lp, d), F32), jax.ShapeDtypeStruct((1, d), F32)],
        compiler_params=pltpu.CompilerParams(dimension_semantics=("arbitrary",)),
    )(h, target, g)


ADAM_LR, ADAM_B1, ADAM_B2, ADAM_EPS, ADAM_WD, ADAM_STEP = 0.001, 0.9, 0.999, 1e-08, 0.01, 10


def adamw(name, w, g, m, v, echo_g=False):
    lead = w.shape[:-2]
    rows, cols = w.shape[-2:]
    br = _pick(rows, 128, 8)
    n_out = 4 if echo_g else 3

    def body(w_ref, g_ref, m_ref, v_ref, d_ref, nm_ref, nv_ref, *echo):
        gv = g_ref[...]
        nm = ADAM_B1 * m_ref[...] + (1.0 - ADAM_B1) * gv
        nv = ADAM_B2 * v_ref[...] + (1.0 - ADAM_B2) * (gv * gv)
        m_hat = nm / (1.0 - ADAM_B1 ** ADAM_STEP)
        v_hat = nv / (1.0 - ADAM_B2 ** ADAM_STEP)
        d_ref[...] = -ADAM_LR * (m_hat / (jnp.sqrt(v_hat) + ADAM_EPS) + ADAM_WD * w_ref[...])
        nm_ref[...] = nm
        nv_ref[...] = nv
        for e_ref in echo:
            e_ref[...] = gv

    if lead and rows <= 8:
        bl = _pick(lead[0], 32, 1)
        spec = pl.BlockSpec((bl, rows, cols), lambda s: (s, 0, 0))
        grid = (lead[0] // bl,)
    elif lead:
        spec = pl.BlockSpec((None, br, cols), lambda s, i: (s, i, 0))
        grid = (lead[0], rows // br)
    else:
        spec = pl.BlockSpec((br, cols), lambda i: (i, 0))
        grid = (rows // br,)
    return pl.pallas_call(
        body, name=name, grid=grid, in_specs=[spec] * 4, out_specs=[spec] * n_out,
        out_shape=[jax.ShapeDtypeStruct(w.shape, F32)] * n_out,
        compiler_params=pltpu.CompilerParams(dimension_semantics=("arbitrary",) * len(grid), vmem_limit_bytes=VMEM_LIMIT),
    )(w, g, m, v)


MESH = pl.DeviceIdType.MESH
ANY = pl.BlockSpec(memory_space=pl.ANY)
D2D_PIECES = 16
ICI_PIECES = 4


def _place():
    x, y, c = lax.axis_index("x"), lax.axis_index("y"), lax.axis_index("c")
    return x, y, c, [(1 - x, y), (x, 1 - y), (1 - x, 1 - y)]


def _pieces(rows, n, unit):
    per = -(-rows // (n * unit)) * unit
    return [(s, min(per, rows - s)) for s in range(0, rows, per)]


def _row_unit(dtype):
    return 16 if dtype == BF16 else 8


def _scalar(v):
    return jnp.reshape(v, (1,)).astype(jnp.int32)


def place_shard(name, pack):
    rows, cols = pack.shape
    br = _pick(rows, 256, 16)

    def body(m_ref, p_ref, o_ref):
        o_ref[...] = p_ref[...]

    return pl.pallas_call(
        body, name=name,
        grid_spec=pltpu.PrefetchScalarGridSpec(
            num_scalar_prefetch=1, grid=(rows // br,),
            in_specs=[pl.BlockSpec((br, cols), lambda i, m: (i, 0))],
            out_specs=pl.BlockSpec((None, br, cols), lambda i, m: (m[0], i, 0))),
        out_shape=jax.ShapeDtypeStruct((4, rows, cols), pack.dtype),
        compiler_params=pltpu.CompilerParams(dimension_semantics=("arbitrary",), vmem_limit_bytes=VMEM_LIMIT),
    )(_scalar(2 * lax.axis_index("x") + lax.axis_index("y")), pack)


class Ride:
    def __init__(self, ins, out_shapes, alias, sems, start, finish):
        self.ins, self.out_shapes, self.alias, self.sems, self.start, self.finish = list(ins), out_shapes, alias, sems, start, finish


def _gather_parts(o_refs, send_sems, recv_sems):
    x, y, c, chips = _place()
    mine = 2 * x + y

    def half_rows(b, which, start=0, size=None):
        half = o_refs[b].shape[1] // 2
        return pl.ds(pl.multiple_of(which * half + start, _row_unit(o_refs[b].dtype)), half if size is None else size)

    def remote(b, k, slot, rws, to):
        piece = o_refs[b].at[slot, rws, :]
        return pltpu.make_async_remote_copy(src_ref=piece, dst_ref=piece, send_sem=send_sems.at[b, k], recv_sem=recv_sems.at[b, k],
                                            device_id=to, device_id_type=MESH)

    return x, y, c, chips, mine, half_rows, remote


def _gather_start(o_refs, send_sems, recv_sems):
    x, y, c, chips, mine, half_rows, remote = _gather_parts(o_refs, send_sems, recv_sems)
    for b, o_ref in enumerate(o_refs):
        for j, (cx, cy) in enumerate(chips):
            for start, size in _pieces(o_ref.shape[1] // 2, ICI_PIECES, _row_unit(o_ref.dtype)):
                remote(b, j, mine, half_rows(b, c, start, size), (cx, cy, c)).start()


def _gather_finish(o_refs, send_sems, recv_sems):
    x, y, c, chips, mine, half_rows, remote = _gather_parts(o_refs, send_sems, recv_sems)
    sends = []
    for b, o_ref in enumerate(o_refs):
        for j, (cx, cy) in enumerate(chips):
            slot = 2 * cx + cy
            sends.append(remote(b, j, mine, half_rows(b, c), (cx, cy, c)))
            remote(b, j, slot, half_rows(b, c), (cx, cy, c)).wait_recv()
            for start, size in _pieces(o_ref.shape[1] // 2, D2D_PIECES, _row_unit(o_ref.dtype)):
                remote(b, 3 + j, slot, half_rows(b, c, start, size), (x, y, 1 - c)).start()
            sends.append(remote(b, 3 + j, slot, half_rows(b, c), (x, y, 1 - c)))
    for b in range(len(o_refs)):
        for j, (cx, cy) in enumerate(chips):
            remote(b, 3 + j, 2 * cx + cy, half_rows(b, 1 - c), (x, y, 1 - c)).wait_recv()
    for cp in sends:
        cp.wait_send()


def gather_ride(placed):
    n = len(placed)
    return Ride(placed, [jax.ShapeDtypeStruct(p.shape, p.dtype) for p in placed], True,
                [pltpu.SemaphoreType.DMA((n, 6)), pltpu.SemaphoreType.DMA((n, 6))],
                lambda ins, outs, sems: _gather_start(outs, *sems), lambda ins, outs, sems: _gather_finish(outs, *sems))


def gather_shards(name, placed):
    n = len(placed)

    def body(*refs):
        o_refs, sems = refs[n:2 * n], refs[2 * n:]
        _gather_start(o_refs, *sems)
        _gather_finish(o_refs, *sems)

    return pl.pallas_call(
        body, name=name, in_specs=[ANY] * n, out_specs=[ANY] * n,
        out_shape=[jax.ShapeDtypeStruct(p.shape, p.dtype) for p in placed],
        input_output_aliases={i: i for i in range(n)},
        scratch_shapes=[pltpu.SemaphoreType.DMA((n, 6)), pltpu.SemaphoreType.DMA((n, 6))],
    )(*placed)


def pair_split(name, bufs):
    n = len(bufs)

    def body(*refs):
        g_refs, t_refs = refs[:n], refs[n:2 * n]
        send_sems, recv_sems = refs[2 * n:]
        x, y, c, _ = _place()
        waits = []
        for b, (g_ref, t_ref) in enumerate(zip(g_refs, t_refs)):
            half = g_ref.shape[2]
            unit = _row_unit(g_ref.dtype)

            def copy(slots, start, size):
                rws = pl.ds(start, size)
                return pltpu.make_async_remote_copy(src_ref=g_ref.at[slots, 1 - c, rws, :], dst_ref=t_ref.at[slots, rws, :],
                                                    send_sem=send_sems.at[b], recv_sem=recv_sems.at[b], device_id=(x, y, 1 - c),
                                                    device_id_type=MESH)

            for s in range(4):
                for start, size in _pieces(half, D2D_PIECES // 4, unit):
                    copy(s, start, size).start()
            waits.append(copy(slice(None), 0, half))
        for cp in waits:
            cp.wait()

    return pl.pallas_call(
        body, name=name, in_specs=[ANY] * n, out_specs=[ANY] * n,
        out_shape=[jax.ShapeDtypeStruct((4,) + g.shape[2:], g.dtype) for g in bufs],
        scratch_shapes=[pltpu.SemaphoreType.DMA((n,)), pltpu.SemaphoreType.DMA((n,))],
    )(*bufs)


def pair_add(name, g, theirs, transit):
    _, _, half, cols = g.shape
    br = _pick(half, 128, 16)

    def body(c_ref, g_ref, t_ref, o_ref):
        o_ref[...] = (g_ref[...].astype(F32) + t_ref[...].astype(F32)).astype(transit)

    blk = (4, br, cols)
    return pl.pallas_call(
        body, name=name,
        grid_spec=pltpu.PrefetchScalarGridSpec(
            num_scalar_prefetch=1, grid=(half // br,),
            in_specs=[pl.BlockSpec((4, None, br, cols), lambda i, c: (0, c[0], i, 0)), pl.BlockSpec(blk, lambda i, c: (0, i, 0))],
            out_specs=pl.BlockSpec(blk, lambda i, c: (0, i, 0))),
        out_shape=jax.ShapeDtypeStruct((4, half, cols), transit),
        compiler_params=pltpu.CompilerParams(dimension_semantics=("arbitrary",), vmem_limit_bytes=VMEM_LIMIT),
    )(_scalar(lax.axis_index("c")), g, theirs)


def _exchange_copies(a_refs, o_refs, send_sems, recv_sems, start):
    x, y, c, chips = _place()
    mine = 2 * x + y
    waits = []
    for b, (a_ref, o_ref) in enumerate(zip(a_refs, o_refs)):
        rows = a_ref.shape[1]
        for j, (cx, cy) in enumerate(chips):
            def copy(first, size):
                rws = pl.ds(first, size)
                return pltpu.make_async_remote_copy(src_ref=a_ref.at[2 * cx + cy, rws, :], dst_ref=o_ref.at[mine, rws, :],
                                                    send_sem=send_sems.at[b, j], recv_sem=recv_sems.at[b, j],
                                                    device_id=(cx, cy, c), device_id_type=MESH)
            if start:
                for first, size in _pieces(rows, ICI_PIECES, _row_unit(a_ref.dtype)):
                    copy(first, size).start()
            else:
                waits.append(copy(0, rows))
    return waits


def _exchange_finish(a_refs, o_refs, send_sems, recv_sems):
    for cp in _exchange_copies(a_refs, o_refs, send_sems, recv_sems, False):
        cp.wait()


def exchange_ride(parts):
    n = len(parts)
    return Ride(parts, [jax.ShapeDtypeStruct(a.shape, a.dtype) for a in parts], False,
                [pltpu.SemaphoreType.DMA((n, 3)), pltpu.SemaphoreType.DMA((n, 3))],
                lambda ins, outs, sems: _exchange_copies(ins, outs, *sems, True), lambda ins, outs, sems: _exchange_finish(ins, outs, *sems))


def chip_exchange(name, parts):
    n = len(parts)

    def body(*refs):
        a_refs, o_refs, sems = refs[:n], refs[n:2 * n], refs[2 * n:]
        _exchange_copies(a_refs, o_refs, *sems, True)
        _exchange_finish(a_refs, o_refs, *sems)

    return pl.pallas_call(
        body, name=name, in_specs=[ANY] * n, out_specs=[ANY] * n,
        out_shape=[jax.ShapeDtypeStruct(a.shape, a.dtype) for a in parts],
        scratch_shapes=[pltpu.SemaphoreType.DMA((n, 3)), pltpu.SemaphoreType.DMA((n, 3))],
    )(*parts)


def chip_add(name, got, part):
    _, rows, cols = got.shape
    br = _pick(rows, 128, 16)
    nblk = rows // br

    def body(m_ref, c_ref, got_ref, part_ref, o_ref):
        mine = m_ref[0]
        for s in range(4):
            @pl.when(mine == s)
            def _(s=s):
                val = part_ref[...].astype(F32)
                o_ref[...] = val if s == 0 else o_ref[...] + val

            @pl.when(mine != s)
            def _(s=s):
                val = got_ref[s].astype(F32)
                o_ref[...] = val if s == 0 else o_ref[...] + val

    return pl.pallas_call(
        body, name=name,
        grid_spec=pltpu.PrefetchScalarGridSpec(
            num_scalar_prefetch=2, grid=(nblk,),
            in_specs=[pl.BlockSpec((4, br, cols), lambda i, m, c: (0, i, 0)),
                      pl.BlockSpec((None, br, cols), lambda i, m, c: (m[0], i, 0))],
            out_specs=pl.BlockSpec((br, cols), lambda i, m, c: (c[0] * nblk + i, 0))),
        out_shape=jax.ShapeDtypeStruct((2 * rows, cols), F32),
        compiler_params=pltpu.CompilerParams(dimension_semantics=("arbitrary",), vmem_limit_bytes=VMEM_LIMIT),
    )(_scalar(2 * lax.axis_index("x") + lax.axis_index("y")), _scalar(lax.axis_index("c")), got, part)


def pair_join(name, fulls):
    n = len(fulls)

    def body(*refs):
        o_refs = refs[n:2 * n]
        send_sems, recv_sems = refs[2 * n:]
        x, y, c, _ = _place()
        waits = []
        for b, o_ref in enumerate(o_refs):
            half = o_ref.shape[0] // 2
            unit = _row_unit(o_ref.dtype)

            def copy(start, size):
                piece = o_ref.at[pl.ds(pl.multiple_of(c * half + start, unit), size), :]
                return pltpu.make_async_remote_copy(src_ref=piece, dst_ref=piece, send_sem=send_sems.at[b], recv_sem=recv_sems.at[b],
                                                    device_id=(x, y, 1 - c), device_id_type=MESH)

            for start, size in _pieces(half, D2D_PIECES, unit):
                copy(start, size).start()
            waits.append(copy(0, half))
        for cp in waits:
            cp.wait()

    return pl.pallas_call(
        body, name=name, in_specs=[ANY] * n, out_specs=[ANY] * n,
        out_shape=[jax.ShapeDtypeStruct(f.shape, f.dtype) for f in fulls],
        input_output_aliases={i: i for i in range(n)},
        scratch_shapes=[pltpu.SemaphoreType.DMA((n,)), pltpu.SemaphoreType.DMA((n,))],
    )(*fulls)


D_IN = 15920
D_PROJ = 16000
_SEGMENTS = ((0, 8192), (8208, 12816), (12848, 15920), (8192, 8208), (12816, 12848))
OFF_SB_Z, OFF_GDN_QKV, OFF_GDN_Z, OFF_SSM_Z, OFF_SSM_XBC, OFF_GATES, OFF_SMALL = 3072, 4096, 7168, 8192, 10240, 12800, 15872
PACK_C = 1024
WEIGHTS = ("meta_tokens", "norm_g", "w_in", "gdn_conv_w", "gdn_a_log", "gdn_dt_bias", "gdn_norm_g", "ssm_conv_w", "ssm_conv_b",
           "ssm_a_log", "ssm_dt_bias", "ssm_d", "ssm_norm_g", "w_branch_a", "w_branch_b", "w_branch_c", "w_out", "final_norm_g")
SHARDED = ("w_in", "w_branch_a", "w_branch_b", "w_branch_c", "w_out", "gdn_conv_w", "ssm_conv_w", "meta_tokens")
SHARD_AXIS = {"w_in": 2, "w_branch_a": 1, "w_branch_b": 1, "w_branch_c": 1, "w_out": 1, "gdn_conv_w": 2, "ssm_conv_w": 2, "meta_tokens": 1}
BRANCH = ("w_branch_a", "w_branch_b", "w_branch_c", "w_out")
EXACT = ("gdn_conv_w", "ssm_conv_w", "meta_tokens")
REPLICATED = tuple(n for n in WEIGHTS if n not in SHARDED)


def _regrouped_from_shards(shard_cols):
    out = []
    for a, b in _SEGMENTS:
        while a < b:
            chip = a // shard_cols
            stop = min(b, (chip + 1) * shard_cols)
            out.append((chip, a - chip * shard_cols, stop - chip * shard_cols))
            a = stop
    return out


def _shard_from_regrouped(chip, shard_cols):
    lo, hi = chip * shard_cols, (chip + 1) * shard_cols
    out, pos = [], 0
    starts = {}
    for a, b in _SEGMENTS:
        starts[(a, b)] = pos
        pos += b - a
    for a, b in sorted(_SEGMENTS):
        s0, s1 = max(a, lo), min(b, hi)
        if s0 < s1:
            out.append((starts[(a, b)] + s0 - a, starts[(a, b)] + s1 - a))
    return out


def _pack(parts, row_unit=64):
    n = sum(p.shape[0] for p in parts)
    rows = -(-n // (PACK_C * row_unit)) * row_unit
    flat = jnp.concatenate(list(parts) + [jnp.zeros((rows * PACK_C - n,), parts[0].dtype)])
    return flat.reshape(rows, PACK_C)


def _unpack(buf, shapes):
    flat, out, pos = buf.reshape(-1), [], 0
    for shp in shapes:
        n = math.prod(shp)
        out.append(flat[pos:pos + n].reshape(shp))
        pos += n
    return out


def _as_bf16_words(a):
    return lax.bitcast_convert_type(a, BF16).reshape(-1)


BRANCH_ROWS = (D_MODEL // 4, D_MODEL // 4, SSM_INNER // 4, D_MODEL // 4)


def _place_weights(w):
    depth = w["w_in"].shape[0]
    layers = []
    for l in range(depth):
        a = w["w_in"][l].astype(BF16)
        b = jnp.concatenate([w[n][l] for n in BRANCH], axis=0).astype(BF16)
        layers.append([place_shard("place_w_in", a), place_shard("place_branch", b)])
    small = place_shard("place_exact", _pack([_as_bf16_words(w[n]) for n in EXACT]))
    return layers, small


def _exact_weights(w, got_s):
    per_chip = [_unpack(got_s[c], [w[n].shape + (2,) for n in EXACT]) for c in range(4)]
    return {n: jnp.concatenate([lax.bitcast_convert_type(per_chip[c][i], F32) for c in range(4)], axis=SHARD_AXIS[n])
            for i, n in enumerate(EXACT)}


def _layer_weights(got_a, got_b):
    d_model, shard_cols = got_a.shape[1:]
    pad = jnp.zeros((d_model, D_PROJ - D_IN), BF16)
    out = {"wp": jnp.concatenate([got_a[c, :, lo:hi] for c, lo, hi in _regrouped_from_shards(shard_cols)] + [pad], axis=1)}
    pos = 0
    for n, rows in zip(BRANCH, BRANCH_ROWS):
        out[n] = jnp.concatenate([got_b[c, pos:pos + rows] for c in range(4)], axis=0)
        pos += rows
    return out


def _shard(a, axis, s):
    size = a.shape[axis] // 4
    return lax.slice_in_dim(a, s * size, (s + 1) * size, axis=axis)


def _layer_grad_buffers(g, shard_cols):
    buf_a = jnp.stack([jnp.concatenate([g["w_in"][:, lo:hi] for lo, hi in _shard_from_regrouped(s, shard_cols)], axis=1)
                       for s in range(4)]).astype(BF16)
    buf_b = jnp.stack([jnp.concatenate([_shard(g[n], 0, s) for n in BRANCH], axis=0) for s in range(4)]).astype(BF16)
    return [buf_a, buf_b]


def _split_halves(buf):
    return buf.reshape(4, 2, buf.shape[1] // 2, buf.shape[2])


def _start_reduce(tag, bufs, transits):
    bufs = [_split_halves(g) for g in bufs]
    theirs = pair_split(tag + "_pair_split", bufs)
    return [pair_add(f"{tag}_pair_add_{i}", g, t, tr) for i, (g, t, tr) in enumerate(zip(bufs, theirs, transits))]


def _layer_params(w, exact, weights, l):
    lane = lambda v, lo: jnp.pad(v, (lo, HEAD_DIM - lo - v.shape[0]))[None]
    return dict(
        norm_g=w["norm_g"][l][None], wp=weights["wp"],
        gdn_conv_w=exact["gdn_conv_w"][l], gdn_conv_b=jnp.zeros((1, 3 * N_HEADS * HEAD_DIM), F32),
        ssm_conv_w=exact["ssm_conv_w"][l], ssm_conv_b=w["ssm_conv_b"][l][None],
        bias_vec=lane(w["gdn_dt_bias"][l], 8) + lane(w["ssm_dt_bias"][l], 16),
        alog_vec=lane(w["gdn_a_log"][l], 8) + lane(w["ssm_a_log"][l], 16),
        gdn_norm_g=w["gdn_norm_g"][l][None], d_skip=jnp.repeat(w["ssm_d"][l], SSM_P)[None], ssm_norm_g=w["ssm_norm_g"][l][None],
        wa=weights["w_branch_a"], wb=weights["w_branch_b"], wc=weights["w_branch_c"], wo=weights["w_out"])


def _layer_fwd(h, p, ride):
    lp = h.shape[0]
    bm = _pick(lp, 272, 8)
    kw = dict(bm=bm, lp=lp)
    (u,) = rowmap_fwd("rms_fwd", _f_rmsnorm, [(h, 0)], [(p["norm_g"], False)], 1, width=D_MODEL, ncol=1, out_dtypes=[BF16], **kw)
    proj = matmul(u, p["wp"], "nn", name="proj", bm=lp, bn=640)
    o_a_raw, *rode = sb_fwd(proj, (0, N_HEADS, 2 * N_HEADS), ride)
    qkv = conv_fwd("gdn_conv_fwd", proj, OFF_GDN_QKV // HEAD_DIM, p["gdn_conv_w"], p["gdn_conv_b"], 2 * N_HEADS)
    first, second = rowmap_fwd("gates_fwd", _f_small_gates, [(proj, OFF_SMALL // HEAD_DIM)],
                               [(p["bias_vec"], False), (p["alog_vec"], False)], 2, width=HEAD_DIM, ncol=1, **kw)
    o_b_raw, gdn_states = gdn_fwd(qkv, first)
    xbc = conv_fwd("ssm_conv_fwd", proj, OFF_SSM_XBC // HEAD_DIM, p["ssm_conv_w"], p["ssm_conv_b"], 0)
    y_raw, ssd_states = ssd_fwd(xbc, first, second)
    (o_a,) = rowmap_fwd("gate_a_fwd", _f_gate_silu, [(o_a_raw, 0), (proj, OFF_SB_Z // 1024)], [], 1, width=1024, ncol=1,
                        out_dtypes=[BF16], **kw)
    (o_b,) = rowmap_fwd("gate_b_fwd", _f_head_norm_gate, [(o_b_raw, 0), (proj, OFF_GDN_Z // 1024)], [(p["gdn_norm_g"], False)], 1,
                        width=1024, ncol=1, out_dtypes=[BF16], **kw)
    (o_c,) = rowmap_fwd("gate_c_fwd", _f_ssm_out, [(y_raw, 0), (xbc, 0), (proj, OFF_SSM_Z // 1024)],
                        [(p["d_skip"], True), (p["ssm_norm_g"], True)], 1, width=1024, ncol=SSM_GROUPS, out_dtypes=[BF16], **kw)
    pa = matmul(o_a, p["wa"], "nn", name="branch_a", bm=lp // 2, bn=512)
    pb = matmul(o_b, p["wb"], "nn", name="branch_b", bm=lp // 2, bn=512)
    pc = matmul(o_c, p["wc"], "nn", name="branch_c", bm=lp // 2, bn=512)
    merge_rows = [(pa, 0), (pb, 0), (pc, 0)] + [(proj, OFF_GATES // 512 + 2 * i) for i in range(3)]
    (merged,) = rowmap_fwd("merge_fwd", _f_merge, merge_rows, [], 1, width=512, ncol=2, out_dtypes=[BF16], **kw)
    h_out = matmul(merged, p["wo"], "nn", name="out_proj", bm=lp, bn=512, residual=h)
    saved = dict(h=h, u=u, proj=proj, qkv=qkv, first=first, second=second, o_a_raw=o_a_raw, o_b_raw=o_b_raw,
                 gdn_states=gdn_states, xbc=xbc, y_raw=y_raw, ssd_states=ssd_states, o_a=o_a, o_b=o_b, o_c=o_c, pa=pa, pb=pb, pc=pc,
                 merged=merged)
    return h_out, saved, rode


def _layer_bwd(d_h, p, s, ride):
    lp = d_h.shape[0]
    bm = _pick(lp, 272, 8)
    kw = dict(bm=bm, lp=lp)
    proj = s["proj"]
    g = {}
    d_merged = matmul(d_h, p["wo"], "nt", name="d_merged", bm=lp, bn=512)
    g["w_out"] = matmul(s["merged"], d_h, "tn", name="g_w_out", bm=512, bn=1024, bk=lp)
    merge_rows = [(s["pa"], 0), (s["pb"], 0), (s["pc"], 0)] + [(proj, OFF_GATES // 512 + 2 * i) for i in range(3)]
    (d_pa, d_pb, d_pc, d_ga, d_gb, d_gc), _ = rowmap_bwd("merge_bwd", _f_merge, merge_rows, [], [d_merged], width=512, ncol=2,
                                                         d_row_dtypes=[BF16] * 6, **kw)
    g["w_branch_a"] = matmul(s["o_a"], d_pa, "tn", name="g_w_a", bm=512, bn=1024, bk=lp)
    g["w_branch_b"] = matmul(s["o_b"], d_pb, "tn", name="g_w_b", bm=512, bn=1024, bk=lp)
    g["w_branch_c"] = matmul(s["o_c"], d_pc, "tn", name="g_w_c", bm=512, bn=1024, bk=lp)
    d_oa = matmul(d_pa, p["wa"], "nt", name="d_o_a", bm=lp, bn=512)
    d_ob = matmul(d_pb, p["wb"], "nt", name="d_o_b", bm=lp, bn=512)
    d_oc = matmul(d_pc, p["wc"], "nt", name="d_o_c", bm=lp, bn=512)
    (d_oa_raw, d_sbz), _ = rowmap_bwd("gate_a_bwd", _f_gate_silu, [(s["o_a_raw"], 0), (proj, OFF_SB_Z // 1024)], [], [d_oa],
                                      width=1024, ncol=1, d_row_dtypes=[F32, BF16], **kw)
    (d_ob_raw, d_gdz), (g["gdn_norm_g"],) = rowmap_bwd(
        "gate_b_bwd", _f_head_norm_gate, [(s["o_b_raw"], 0), (proj, OFF_GDN_Z // 1024)], [(p["gdn_norm_g"], False)], [d_ob],
        width=1024, ncol=1, d_row_dtypes=[F32, BF16], **kw)
    (d_y, d_xh, d_ssz), (g_dskip, g["ssm_norm_g"]) = rowmap_bwd(
        "gate_c_bwd", _f_ssm_out, [(s["y_raw"], 0), (s["xbc"], 0), (proj, OFF_SSM_Z // 1024)],
        [(p["d_skip"], True), (p["ssm_norm_g"], True)], [d_oc], width=1024, ncol=SSM_GROUPS, d_row_dtypes=[F32, F32, BF16], **kw)
    g["gdn_norm_g"], g["ssm_norm_g"] = g["gdn_norm_g"][0], g["ssm_norm_g"][0]
    g["ssm_d"] = g_dskip.reshape(SSM_HEADS, SSM_P).sum(axis=1)
    d_q, d_k, d_v, *rode = sb_bwd(proj, (0, N_HEADS, 2 * N_HEADS), d_oa_raw, ride)
    d_qkv, d_first_gdn = gdn_bwd(s["qkv"], s["first"], s["gdn_states"], d_ob_raw)
    d_xbc_out, d_first_ssd, d_second = ssd_bwd(s["xbc"], s["first"], s["second"], s["ssd_states"], d_y, d_xh)
    d_gdqkv, g["gdn_conv_w"], _ = conv_bwd("gdn_conv_bwd", proj, OFF_GDN_QKV // HEAD_DIM, p["gdn_conv_w"], p["gdn_conv_b"], 2 * N_HEADS,
                                           d_qkv)
    d_xbc, g["ssm_conv_w"], g_cb = conv_bwd("ssm_conv_bwd", proj, OFF_SSM_XBC // HEAD_DIM, p["ssm_conv_w"], p["ssm_conv_b"], 0, d_xbc_out)
    g["ssm_conv_b"] = g_cb[0]
    d_first = d_first_gdn + d_first_ssd
    (d_small,), (g_bias, g_alog) = rowmap_bwd("gates_bwd", _f_small_gates, [(proj, OFF_SMALL // HEAD_DIM)],
                                              [(p["bias_vec"], False), (p["alog_vec"], False)], [d_first, d_second],
                                              width=HEAD_DIM, ncol=1, d_row_dtypes=[BF16], **kw)
    g["gdn_dt_bias"], g["ssm_dt_bias"] = g_bias[0, 8:16], g_bias[0, 16:48]
    g["gdn_a_log"], g["ssm_a_log"] = g_alog[0, 8:16], g_alog[0, 16:48]
    d_proj = jnp.concatenate([d_q, d_k, d_v, d_sbz, d_gdqkv, d_gdz, d_ssz, d_xbc, d_ga, d_gb, d_gc, d_small], axis=1)
    g["w_in"] = matmul(s["u"], d_proj, "tn", name="g_w_in", bm=1024, bn=640, bk=lp, out_dtype=BF16)
    d_u = matmul(d_proj, p["wp"], "nt", name="d_u", bm=lp // 2, bn=1024, bk=1600)
    (d_hn,), (g_norm,) = rowmap_bwd("rms_bwd", _f_rmsnorm, [(s["h"], 0)], [(p["norm_g"], False)], [d_u], width=D_MODEL, ncol=1, **kw)
    g["norm_g"] = g_norm[0]
    return d_h + d_hn, g, rode


def kernel(x, meta_tokens, norm_g, w_in, gdn_conv_w, gdn_a_log, gdn_dt_bias, gdn_norm_g, ssm_conv_w, ssm_conv_b, ssm_a_log, ssm_dt_bias, ssm_d, ssm_norm_g, w_branch_a, w_branch_b, w_branch_c, w_out, final_norm_g, loss_target, m_meta_tokens, m_norm_g, m_w_in, m_gdn_conv_w, m_gdn_a_log, m_gdn_dt_bias, m_gdn_norm_g, m_ssm_conv_w, m_ssm_conv_b, m_ssm_a_log, m_ssm_dt_bias, m_ssm_d, m_ssm_norm_g, m_w_branch_a, m_w_branch_b, m_w_branch_c, m_w_out, m_final_norm_g, v_meta_tokens, v_norm_g, v_w_in, v_gdn_conv_w, v_gdn_a_log, v_gdn_dt_bias, v_gdn_norm_g, v_ssm_conv_w, v_ssm_conv_b, v_ssm_a_log, v_ssm_dt_bias, v_ssm_d, v_ssm_norm_g, v_w_branch_a, v_w_branch_b, v_w_branch_c, v_w_out, v_final_norm_g):
    w = dict(meta_tokens=meta_tokens, norm_g=norm_g, w_in=w_in, gdn_conv_w=gdn_conv_w, gdn_a_log=gdn_a_log, gdn_dt_bias=gdn_dt_bias,
             gdn_norm_g=gdn_norm_g, ssm_conv_w=ssm_conv_w, ssm_conv_b=ssm_conv_b, ssm_a_log=ssm_a_log, ssm_dt_bias=ssm_dt_bias,
             ssm_d=ssm_d, ssm_norm_g=ssm_norm_g, w_branch_a=w_branch_a, w_branch_b=w_branch_b, w_branch_c=w_branch_c, w_out=w_out,
             final_norm_g=final_norm_g)
    m = dict(meta_tokens=m_meta_tokens, norm_g=m_norm_g, w_in=m_w_in, gdn_conv_w=m_gdn_conv_w, gdn_a_log=m_gdn_a_log,
             gdn_dt_bias=m_gdn_dt_bias, gdn_norm_g=m_gdn_norm_g, ssm_conv_w=m_ssm_conv_w, ssm_conv_b=m_ssm_conv_b,
             ssm_a_log=m_ssm_a_log, ssm_dt_bias=m_ssm_dt_bias, ssm_d=m_ssm_d, ssm_norm_g=m_ssm_norm_g, w_branch_a=m_w_branch_a,
             w_branch_b=m_w_branch_b, w_branch_c=m_w_branch_c, w_out=m_w_out, final_norm_g=m_final_norm_g)
    v = dict(meta_tokens=v_meta_tokens, norm_g=v_norm_g, w_in=v_w_in, gdn_conv_w=v_gdn_conv_w, gdn_a_log=v_gdn_a_log,
             gdn_dt_bias=v_gdn_dt_bias, gdn_norm_g=v_gdn_norm_g, ssm_conv_w=v_ssm_conv_w, ssm_conv_b=v_ssm_conv_b,
             ssm_a_log=v_ssm_a_log, ssm_dt_bias=v_ssm_dt_bias, ssm_d=v_ssm_d, ssm_norm_g=v_ssm_norm_g, w_branch_a=v_w_branch_a,
             w_branch_b=v_w_branch_b, w_branch_c=v_w_branch_c, w_out=v_w_out, final_norm_g=v_final_norm_g)
    depth = norm_g.shape[0]
    shard_cols = w_in.shape[-1]
    placed, placed_small = _place_weights(w)
    got_a, got_b, got_s = gather_shards("gather_first", placed[0] + [placed_small])
    exact = _exact_weights(w, got_s)

    h = jnp.concatenate([jnp.zeros((PAD, D_MODEL), F32), exact["meta_tokens"], x[0]], axis=0)
    params, saved = [], []
    for l in range(depth):
        params.append(_layer_params(w, exact, _layer_weights(got_a, got_b), l))
        h, s, rode = _layer_fwd(h, params[l], gather_ride(placed[l + 1]) if l + 1 < depth else None)
        saved.append(s)
        if rode:
            got_a, got_b = rode
    loss, d_h, g_final = loss_head(h, loss_target[0], final_norm_g[None])

    layer_grads, reds, waiting = [None] * depth, [None] * depth, None
    for l in reversed(range(depth)):
        d_h, layer_grads[l], rode = _layer_bwd(d_h, params[l], saved[l], exchange_ride(waiting) if waiting else None)
        if waiting:
            reds[l + 1] = [chip_add(f"grads_chip_add_{i}", gt, p) for i, (gt, p) in enumerate(zip(rode, waiting))]
        waiting = _start_reduce("grads", _layer_grad_buffers(layer_grads[l], shard_cols), [BF16, BF16])
    grads = {n: jnp.stack([layer_grads[l][n] for l in range(depth)]) for n in WEIGHTS
             if n not in ("meta_tokens", "final_norm_g", "w_in") + BRANCH}
    grads["meta_tokens"] = d_h[PAD:PAD + N_META]
    grads["final_norm_g"] = g_final[0]
    grad_x = d_h[PAD + N_META:][None]
    buf_s = jnp.stack([_pack([_shard(grads[n], SHARD_AXIS[n], s).astype(BF16).reshape(-1) for n in EXACT], row_unit=32) for s in range(4)])
    small = _pack([grads[n].reshape(-1) for n in REPLICATED], row_unit=32)
    last = waiting + _start_reduce("small_grads", [buf_s, jnp.broadcast_to(small[None], (4,) + small.shape)], [BF16, F32])
    got = chip_exchange("grads_chip_exchange", last)
    sums = [chip_add(f"grads_chip_add_{i}", gt, p) for i, (gt, p) in enumerate(zip(got, last))]
    reds[0] = sums[:2]
    joined = pair_join("grads_pair_join", [r for layer in reds for r in layer] + sums[2:])
    red = {"w_in": jnp.stack(joined[0:2 * depth:2])}
    pos = 0
    for n, rows in zip(BRANCH, BRANCH_ROWS):
        red[n] = jnp.stack([joined[2 * l + 1][pos:pos + rows] for l in range(depth)])
        pos += rows
    red.update(zip(EXACT, _unpack(joined[-2], [w[n].shape for n in EXACT])))
    small_red = joined[-1]
    delta, new_m, new_v = {}, {}, {}
    for n in SHARDED:
        if w[n].shape[-1] % HEAD_DIM:
            to_view, from_view = (lambda a: jnp.transpose(a, (2, 0, 1))), (lambda a: jnp.transpose(a, (1, 2, 0)))
            delta[n], new_m[n], new_v[n], red[n] = [from_view(a) for a in adamw("adamw_" + n, to_view(w[n]), to_view(red[n]),
                                                                                to_view(m[n]), to_view(v[n]), echo_g=True)]
        else:
            delta[n], new_m[n], new_v[n] = adamw("adamw_" + n, w[n], red[n], m[n], v[n])
    pack_small = lambda d: _pack([d[n].reshape(-1) for n in REPLICATED], row_unit=32)
    small = adamw("adamw_small", pack_small(w), small_red, pack_small(m), pack_small(v))
    shapes = [w[n].shape for n in REPLICATED]
    red.update(zip(REPLICATED, _unpack(small_red, shapes)))
    for d, buf in zip((delta, new_m, new_v), small):
        d.update(zip(REPLICATED, _unpack(buf, shapes)))
    total_loss = lax.psum(loss[0, 0], ("x", "y", "c"))
    return (total_loss, grad_x, *[red[n] for n in WEIGHTS], *[delta[n] for n in WEIGHTS], *[new_m[n] for n in WEIGHTS],
            *[new_v[n] for n in WEIGHTS])
```

```python
import functools
import math

import jax
import jax.numpy as jnp
from jax import lax
from jax.experimental import pallas as pl
from jax.experimental.pallas import tpu as pltpu

F32 = jnp.float32
BF16 = jnp.bfloat16

N_META = 16
RMS_EPS = 1e-6
L2_EPS = 1e-6
CONV_K = 4
D_MODEL = 1024
HEAD_DIM = 128
N_HEADS = 8
CHUNK = 64
SB_BLOCK = 128
PAD = SB_BLOCK - N_META
SSM_INNER = 2048
SSM_P = 64
SSM_HEADS = 32
SSM_GROUPS = 2
SSM_HG = SSM_HEADS // SSM_GROUPS
SSM_N = 128
VMEM_LIMIT = 56 * 1024 * 1024

def _dims(mode, ndim):
    lhs, rhs = {"nn": (1, 0), "nt": (1, 1), "tn": (0, 0)}[mode]
    off = ndim - 2
    return (((lhs + off,), (rhs + off,)), (tuple(range(off)), tuple(range(off))))


def _dot(a, b, mode):
    return lax.dot_general(a, b, _dims(mode, a.ndim), preferred_element_type=F32)


def _halves(a):
    hi = a.astype(BF16)
    return hi, (a - hi.astype(F32)).astype(BF16)


def _mm_raw(a, b, mode, kind):
    if kind == "bf16":
        return _dot(a.astype(BF16), b.astype(BF16), mode)
    if kind == "lhs01":
        hi, lo = _halves(b)
        a = a.astype(BF16)
        return _dot(a, hi, mode) + _dot(a, lo, mode)
    if kind == "rhs01":
        hi, lo = _halves(a)
        b = b.astype(BF16)
        return _dot(hi, b, mode) + _dot(lo, b, mode)
    a_hi, a_lo = _halves(a)
    b_hi, b_lo = _halves(b)
    return _dot(a_hi, b_hi, mode) + (_dot(a_hi, b_lo, mode) + _dot(a_lo, b_hi, mode))


@functools.partial(jax.custom_vjp, nondiff_argnums=(2, 3))
def _mm(a, b, mode="nn", kind="bf16"):
    return _mm_raw(a, b, mode, kind)


def _mm_fwd(a, b, mode, kind):
    return _mm_raw(a, b, mode, kind), (a, b)


def _mm_bwd(mode, kind, res, g):
    a, b = res
    if kind == "lhs01":
        return jnp.zeros_like(a), _mm_raw(a, g, {"nn": "tn", "tn": "nn"}[mode], "lhs01")
    if kind == "rhs01":
        return _mm_raw(g, b, {"nn": "nt", "nt": "nn"}[mode], "rhs01"), jnp.zeros_like(b)
    if mode == "nn":
        return _mm_raw(g, b, "nt", kind), _mm_raw(a, g, "tn", kind)
    if mode == "nt":
        return _mm_raw(g, b, "nn", kind), _mm_raw(g, a, "tn", kind)
    return _mm_raw(b, g, "nt", kind), _mm_raw(a, g, "nn", kind)


_mm.defvjp(_mm_fwd, _mm_bwd)


def _iota2(shape, axis):
    return lax.broadcasted_iota(jnp.int32, shape, axis)


def _inv_unit_lower_raw(m):
    size = m.shape[-1]
    eye = (_iota2((size, size), 0) == _iota2((size, size), 1)).astype(F32)
    n = -m
    t = eye + n
    p = n
    steps = int(math.log2(size)) - 1
    for _ in range(steps):
        p = _mm_raw(p, p, "nn", "x3")
        t = t + _mm_raw(t, p, "nn", "x3")
    return t


@jax.custom_vjp
def _inv_unit_lower(m):
    return _inv_unit_lower_raw(m)


def _inv_fwd(m):
    t = _inv_unit_lower_raw(m)
    return t, t


def _inv_bwd(t, g):
    return (-_mm_raw(_mm_raw(t, g, "tn", "x3"), t, "nt", "x3"),)


_inv_unit_lower.defvjp(_inv_fwd, _inv_bwd)


@jax.custom_vjp
def _inv_known(m, t):
    return t


_inv_known.defvjp(lambda m, t: (t, t), lambda t, g: (_inv_bwd(t, g)[0], jnp.zeros_like(t)))


def _safe_decay(col, row, keep):
    return jnp.where(keep, jnp.exp(jnp.where(keep, col - row, 0.0)), 0.0)


def _col_to_row(col):
    n = col.shape[-2]
    eye = _iota2((n, n), 0) == _iota2((n, n), 1)
    return jnp.sum(jnp.where(eye, col, 0.0), axis=-2, keepdims=True)


def _cumsum_col(col):
    n = col.shape[-2]
    li, si = _iota2((n, n), 0), _iota2((n, n), 1)
    row = _col_to_row(col)
    c_col = jnp.sum(jnp.where(li >= si, row, 0.0), axis=-1, keepdims=True)
    c_row = jnp.sum(jnp.where(li <= si, col, 0.0), axis=-2, keepdims=True)
    return c_col, c_row


def _gdn_chunk(q, k, v, g, beta, state, t_known=None):
    cl = q.shape[-2]
    li, si = _iota2((cl, cl), 0), _iota2((cl, cl), 1)
    gc_col, gc_row = _cumsum_col(g)
    g_last = jnp.sum(g, axis=-2, keepdims=True)
    dec_strict = _safe_decay(gc_col, gc_row, li > si)
    dec_incl = _safe_decay(gc_col, gc_row, li >= si)
    e_gc = jnp.exp(gc_col)
    qs = q * (HEAD_DIM ** -0.5)
    kb = k * beta
    m = _mm(kb, k, "nt") * dec_strict
    t_inv = _inv_unit_lower(m) if t_known is None else _inv_known(m, t_known)
    u = _mm(t_inv, v * beta)
    w = _mm(t_inv, kb * e_gc)
    a_qk = _mm(qs, k, "nt") * dec_incl
    q_dec = qs * e_gc
    k_end = k * jnp.exp(g_last - gc_col)
    v_new = u - _mm(w, state)
    o = _mm(q_dec, state) + _mm(a_qk, v_new)
    new_state = state * jnp.exp(g_last) + _mm(k_end, v_new, "tn")
    return o, new_state, t_inv


def _gdn_operands(qkv_ref, gt):
    nh, width = N_HEADS, N_HEADS * HEAD_DIM
    heads = lambda off: jnp.stack([qkv_ref[:, off + h * HEAD_DIM:off + (h + 1) * HEAD_DIM] for h in range(nh)])
    cols = lambda off: jnp.stack([gt[:, off + h:off + h + 1] for h in range(nh)])
    return heads(0), heads(width), heads(2 * width), cols(nh), cols(0)


def gdn_fwd(qkv, gates):
    lp = qkv.shape[0]
    nh = N_HEADS
    nc = lp // CHUNK
    width = nh * HEAD_DIM

    def body(qkv_ref, gt_ref, o_ref, s_ref, t_ref, state):
        @pl.when(pl.program_id(0) == 0)
        def _():
            state[...] = jnp.zeros_like(state)

        s_in = state[...]
        s_ref[0] = s_in
        o, s_new, t_inv = _gdn_chunk(*_gdn_operands(qkv_ref, gt_ref[...]), s_in)
        t_ref[0] = t_inv
        for h in range(nh):
            o_ref[:, h * HEAD_DIM:(h + 1) * HEAD_DIM] = o[h]
        state[...] = s_new

    return pl.pallas_call(
        body, name="gdn_fwd", grid=(nc,),
        in_specs=[pl.BlockSpec((CHUNK, 3 * width), lambda c: (c, 0)), pl.BlockSpec((CHUNK, HEAD_DIM), lambda c: (c, 0))],
        out_specs=[pl.BlockSpec((CHUNK, width), lambda c: (c, 0)), pl.BlockSpec((1, nh, HEAD_DIM, HEAD_DIM), lambda c: (c, 0, 0, 0)),
                   pl.BlockSpec((1, nh, CHUNK, CHUNK), lambda c: (c, 0, 0, 0))],
        out_shape=[jax.ShapeDtypeStruct((lp, width), F32), jax.ShapeDtypeStruct((nc, nh, HEAD_DIM, HEAD_DIM), F32),
                   jax.ShapeDtypeStruct((nc, nh, CHUNK, CHUNK), F32)],
        scratch_shapes=[pltpu.VMEM((nh, HEAD_DIM, HEAD_DIM), F32)],
        compiler_params=pltpu.CompilerParams(dimension_semantics=("arbitrary",), vmem_limit_bytes=VMEM_LIMIT),
    )(qkv, gates)


def gdn_bwd(qkv, gates, states, t_invs, d_o):
    lp = qkv.shape[0]
    nh = N_HEADS
    nc = lp // CHUNK
    width = nh * HEAD_DIM

    def body(qkv_ref, gt_ref, s_ref, t_ref, do_ref, dqkv_ref, dgt_ref, d_state):
        @pl.when(pl.program_id(0) == 0)
        def _():
            d_state[...] = jnp.zeros_like(d_state)

        t_known = t_ref[0]
        _, pull = jax.vjp(lambda *xs: _gdn_chunk(*xs, t_known=t_known)[:2], *_gdn_operands(qkv_ref, gt_ref[...]), s_ref[0])
        d_o = jnp.stack([do_ref[:, h * HEAD_DIM:(h + 1) * HEAD_DIM] for h in range(nh)])
        dq, dk, dv, dg, db, ds = pull((d_o, d_state[...]))
        lane = _iota2((CHUNK, HEAD_DIM), 1)
        d_gt = jnp.zeros((CHUNK, HEAD_DIM), F32)
        for h in range(nh):
            for part, val in enumerate((dq, dk, dv)):
                dqkv_ref[:, part * width + h * HEAD_DIM:part * width + (h + 1) * HEAD_DIM] = val[h]
            d_gt = d_gt + jnp.where(lane == h, db[h], 0.0) + jnp.where(lane == nh + h, dg[h], 0.0)
        dgt_ref[...] = d_gt
        d_state[...] = ds

    rev = lambda c: (nc - 1 - c, 0)
    return pl.pallas_call(
        body, name="gdn_bwd", grid=(nc,),
        in_specs=[pl.BlockSpec((CHUNK, 3 * width), rev), pl.BlockSpec((CHUNK, HEAD_DIM), rev),
                  pl.BlockSpec((1, nh, HEAD_DIM, HEAD_DIM), lambda c: (nc - 1 - c, 0, 0, 0)),
                  pl.BlockSpec((1, nh, CHUNK, CHUNK), lambda c: (nc - 1 - c, 0, 0, 0)), pl.BlockSpec((CHUNK, width), rev)],
        out_specs=[pl.BlockSpec((CHUNK, 3 * width), rev), pl.BlockSpec((CHUNK, HEAD_DIM), rev)],
        out_shape=[jax.ShapeDtypeStruct((lp, 3 * width), F32), jax.ShapeDtypeStruct((lp, HEAD_DIM), F32)],
        scratch_shapes=[pltpu.VMEM((nh, HEAD_DIM, HEAD_DIM), F32)],
        compiler_params=pltpu.CompilerParams(dimension_semantics=("arbitrary",), vmem_limit_bytes=VMEM_LIMIT),
    )(qkv, gates, states, t_invs, d_o)


def _head_expand():
    width = SSM_HG * SSM_P
    return (_iota2((SSM_HG, width), 1) // SSM_P == _iota2((SSM_HG, width), 0)).astype(F32)


def _ssd_chunk(x, b, c, dt, la, state):
    cl = x.shape[0]
    li, si = _iota2((cl, cl), 0), _iota2((cl, cl), 1)
    causal = li >= si
    expand = _head_expand()
    tri = causal.astype(F32)
    xs = x * _mm(dt, expand, "nn", "rhs01")
    la_x = _mm(la, expand, "nn", "rhs01")
    cs_x = _mm(tri, la_x, "nn", "lhs01")
    last_x = jnp.sum(la_x, axis=0, keepdims=True)
    cs = _mm(tri, la, "nn", "lhs01")
    scores = _mm(c, b, "nt")
    head_id = _iota2((1, SSM_HG), 1)
    per_tile = HEAD_DIM // SSM_P
    tile_head = _iota2((1, HEAD_DIM), 1) // SSM_P
    within = []
    for t in range(SSM_HG // per_tile):
        xs_t = xs[:, t * HEAD_DIM:(t + 1) * HEAD_DIM]
        acc = jnp.zeros((cl, HEAD_DIM), F32)
        for hh in range(per_tile):
            cs_col = jnp.sum(jnp.where(head_id == t * per_tile + hh, cs, 0.0), axis=1, keepdims=True)
            decay = _safe_decay(cs_col, _col_to_row(cs_col), causal)
            acc = acc + _mm(scores * decay, jnp.where(tile_head == hh, xs_t, 0.0))
        within.append(acc)
    y = _mm(c, state) * jnp.exp(cs_x) + jnp.concatenate(within, axis=1)
    new_state = state * jnp.exp(last_x) + _mm(b, xs * jnp.exp(last_x - cs_x), "tn")
    return y, new_state


GATE_DT = 16


def _place_lanes(v, lo):
    n = v.shape[1]
    sel = (_iota2((n, HEAD_DIM), 1) == _iota2((n, HEAD_DIM), 0) + lo).astype(F32)
    return _mm_raw(v, sel, "nn", "rhs01")


def ssd_fwd(xbc, first, second):
    lp = xbc.shape[0]
    nc = lp // CHUNK
    width = SSM_HG * SSM_P
    b_off, c_off = SSM_INNER, SSM_INNER + SSM_GROUPS * SSM_N

    def body(x_ref, f_ref, s2_ref, y_ref, s_ref, state):
        @pl.when(pl.program_id(0) == 0)
        def _():
            state[...] = jnp.zeros_like(state)

        f, s2 = f_ref[...], s2_ref[...]
        for g in range(SSM_GROUPS):
            lo = GATE_DT + g * SSM_HG
            s_in = state[g]
            s_ref[0, g] = s_in
            y, s_new = _ssd_chunk(x_ref[:, g * width:(g + 1) * width], x_ref[:, b_off + g * SSM_N:b_off + (g + 1) * SSM_N],
                                  x_ref[:, c_off + g * SSM_N:c_off + (g + 1) * SSM_N], f[:, lo:lo + SSM_HG], s2[:, lo:lo + SSM_HG], s_in)
            y_ref[:, g * width:(g + 1) * width] = y
            state[g] = s_new

    row = lambda cols: pl.BlockSpec((CHUNK, cols), lambda k: (k, 0))
    return pl.pallas_call(
        body, name="ssd_fwd", grid=(nc,),
        in_specs=[row(xbc.shape[1]), row(HEAD_DIM), row(HEAD_DIM)],
        out_specs=[row(SSM_INNER), pl.BlockSpec((1, SSM_GROUPS, SSM_N, width), lambda k: (k, 0, 0, 0))],
        out_shape=[jax.ShapeDtypeStruct((lp, SSM_INNER), F32), jax.ShapeDtypeStruct((nc, SSM_GROUPS, SSM_N, width), F32)],
        scratch_shapes=[pltpu.VMEM((SSM_GROUPS, SSM_N, width), F32)],
        compiler_params=pltpu.CompilerParams(dimension_semantics=("arbitrary",), vmem_limit_bytes=VMEM_LIMIT),
    )(xbc, first, second)


def ssd_bwd(xbc, first, second, states, d_y, d_xh):
    lp = xbc.shape[0]
    nc = lp // CHUNK
    width = SSM_HG * SSM_P
    b_off, c_off = SSM_INNER, SSM_INNER + SSM_GROUPS * SSM_N

    def body(x_ref, f_ref, s2_ref, s_ref, dy_ref, dxh_ref, dx_ref, df_ref, ds2_ref, d_state):
        @pl.when(pl.program_id(0) == 0)
        def _():
            d_state[...] = jnp.zeros_like(d_state)

        f, s2 = f_ref[...], s2_ref[...]
        d_f = jnp.zeros((CHUNK, HEAD_DIM), F32)
        d_s2 = jnp.zeros((CHUNK, HEAD_DIM), F32)
        for g in range(SSM_GROUPS):
            lo = GATE_DT + g * SSM_HG
            x_l = slice(g * width, (g + 1) * width)
            b_l = slice(b_off + g * SSM_N, b_off + (g + 1) * SSM_N)
            c_l = slice(c_off + g * SSM_N, c_off + (g + 1) * SSM_N)
            _, pull = jax.vjp(_ssd_chunk, x_ref[:, x_l], x_ref[:, b_l], x_ref[:, c_l], f[:, lo:lo + SSM_HG], s2[:, lo:lo + SSM_HG],
                              s_ref[0, g])
            dx, db, dc, ddt, dla, ds = pull((dy_ref[:, x_l], d_state[g]))
            dx_ref[:, x_l] = dx + dxh_ref[:, x_l]
            dx_ref[:, b_l] = db
            dx_ref[:, c_l] = dc
            d_f = d_f + _place_lanes(ddt, lo)
            d_s2 = d_s2 + _place_lanes(dla, lo)
            d_state[g] = ds
        df_ref[...] = d_f
        ds2_ref[...] = d_s2

    row = lambda cols: pl.BlockSpec((CHUNK, cols), lambda k: (nc - 1 - k, 0))
    gate_shape = jax.ShapeDtypeStruct((lp, HEAD_DIM), F32)
    return pl.pallas_call(
        body, name="ssd_bwd", grid=(nc,),
        in_specs=[row(xbc.shape[1]), row(HEAD_DIM), row(HEAD_DIM),
                  pl.BlockSpec((1, SSM_GROUPS, SSM_N, width), lambda k: (nc - 1 - k, 0, 0, 0)), row(SSM_INNER), row(SSM_INNER)],
        out_specs=[row(xbc.shape[1]), row(HEAD_DIM), row(HEAD_DIM)],
        out_shape=[jax.ShapeDtypeStruct(xbc.shape, F32), gate_shape, gate_shape],
        scratch_shapes=[pltpu.VMEM((SSM_GROUPS, SSM_N, width), F32)],
        compiler_params=pltpu.CompilerParams(dimension_semantics=("arbitrary",), vmem_limit_bytes=VMEM_LIMIT),
    )(xbc, first, second, states, d_y, d_xh)


SB_QROWS = 544


def _mm_tri(a, tri):
    return _mm_raw(a, tri.astype(BF16), "nn", "rhs01")


def _sb_scores(q_scaled, kb, row0, j):
    shape = (q_scaled.shape[0], SB_BLOCK)
    z = _mm_raw(q_scaled, kb, "nt", "bf16")
    q_pos = row0 + _iota2(shape, 0)
    k_pos = j * SB_BLOCK + _iota2(shape, 1)
    valid = (k_pos < q_pos) & (k_pos >= PAD)
    sp = jnp.maximum(z, 0.0) + jnp.log(1.0 + jnp.exp(-jnp.abs(z)))
    lk = jnp.where(valid, -sp, 0.0)
    return z, sp, valid, lk


SB_DEAD = -110.0


def sb_fwd(src, offs, ride=None):
    lp = src.shape[0]
    nh = N_HEADS
    qb = _pick(lp, SB_QROWS, 8)
    scale = HEAD_DIM ** -0.5
    blk = SB_BLOCK

    n_in = len(ride.ins) if ride else 0
    n_out = len(ride.out_shapes) if ride else 0

    def body(*refs):
        q_ref, k_ref, v_ref = refs[:3]
        o_ref = refs[3 + n_in]
        ride_refs = (refs[3:3 + n_in], refs[4 + n_in:4 + n_in + n_out], refs[4 + n_in + n_out:])
        i = pl.program_id(1)
        if ride:
            @pl.when((pl.program_id(0) == 0) & (i == 0))
            def _():
                ride.start(*ride_refs)
        q_scaled = q_ref[...] * scale
        upper = _iota2((blk, blk), 0) > _iota2((blk, blk), 1)
        n_blocks = ((i + 1) * qb + blk - 1) // blk

        def live(state):
            it, _, c = state
            return (it < n_blocks) & (jnp.max(c) > SB_DEAD)

        def step(state):
            it, acc, c = state
            j = n_blocks - 1 - it
            rows = pl.ds(pl.multiple_of(j * blk, blk), blk)
            z, sp, valid, lk = _sb_scores(q_scaled, k_ref[rows, :], i * qb, j)
            later = _mm_tri(lk, upper) + c
            w = jnp.where(valid, jnp.exp(z - sp + later), 0.0)
            acc = acc + _mm_raw(w, v_ref[rows, :], "nn", "bf16")
            return it + 1, acc, c + jnp.sum(lk, axis=1, keepdims=True)

        _, acc, _ = lax.while_loop(live, step, (jnp.int32(0), jnp.zeros((qb, HEAD_DIM), F32), jnp.zeros((qb, 1), F32)))
        o_ref[...] = acc
        if ride:
            @pl.when((pl.program_id(0) == nh - 1) & (i == lp // qb - 1))
            def _():
                ride.finish(*ride_refs)

    qspec = pl.BlockSpec((qb, HEAD_DIM), lambda h, i: (i, offs[0] + h))
    kspec = pl.BlockSpec((lp, HEAD_DIM), lambda h, i: (0, offs[1] + h))
    vspec = pl.BlockSpec((lp, HEAD_DIM), lambda h, i: (0, offs[2] + h))
    ospec = pl.BlockSpec((qb, HEAD_DIM), lambda h, i: (i, h))
    return pl.pallas_call(
        body, name="sb_fwd", grid=(nh, lp // qb), in_specs=[qspec, kspec, vspec] + [ANY] * n_in, out_specs=[ospec] + [ANY] * n_out,
        out_shape=[jax.ShapeDtypeStruct((lp, nh * HEAD_DIM), F32)] + (ride.out_shapes if ride else []),
        scratch_shapes=ride.sems if ride else [],
        input_output_aliases={3 + k: 1 + k for k in range(n_in)} if ride and ride.alias else {},
        compiler_params=pltpu.CompilerParams(dimension_semantics=("arbitrary", "arbitrary"), vmem_limit_bytes=VMEM_LIMIT),
    )(src, src, src, *(ride.ins if ride else []))


def sb_bwd(src, offs, d_o, ride=None):
    lp = src.shape[0]
    nh = N_HEADS
    qb = _pick(lp, SB_QROWS, 8)
    scale = HEAD_DIM ** -0.5
    blk = SB_BLOCK

    n_in = len(ride.ins) if ride else 0
    n_out = len(ride.out_shapes) if ride else 0

    def body(*refs):
        q_ref, k_ref, v_ref, do_ref = refs[:4]
        dq_ref, dk_out, dv_out = refs[4 + n_in:7 + n_in]
        ride_refs = (refs[4:4 + n_in], refs[7 + n_in:7 + n_in + n_out], refs[9 + n_in + n_out:])
        dk_ref, dv_ref = refs[7 + n_in + n_out:9 + n_in + n_out]
        i = pl.program_id(1)
        if ride:
            @pl.when((pl.program_id(0) == 0) & (i == 0))
            def _():
                ride.start(*ride_refs)

        @pl.when(i == 0)
        def _():
            dk_ref[...] = jnp.zeros_like(dk_ref)
            dv_ref[...] = jnp.zeros_like(dv_ref)

        q_scaled = q_ref[...] * scale
        d_out = do_ref[...]
        lower_incl = _iota2((blk, blk), 0) <= _iota2((blk, blk), 1)
        lower = _iota2((blk, blk), 0) < _iota2((blk, blk), 1)
        n_blocks = ((i + 1) * qb + blk - 1) // blk

        def live(state):
            it, c = state
            return (it < n_blocks) & (jnp.max(c) > SB_DEAD)

        def count(state):
            it, c = state
            rows = pl.ds(pl.multiple_of((n_blocks - 1 - it) * blk, blk), blk)
            _, _, _, lk = _sb_scores(q_scaled, k_ref[rows, :], i * qb, n_blocks - 1 - it)
            return it + 1, c + jnp.sum(lk, axis=1, keepdims=True)

        n_live, total = lax.while_loop(live, count, (jnp.int32(0), jnp.zeros((qb, 1), F32)))

        def step(j, carry):
            acc, cp, ep = carry
            rows = pl.ds(pl.multiple_of(j * blk, blk), blk)
            kb = k_ref[rows, :]
            vb = v_ref[rows, :]
            z, sp, valid, lk = _sb_scores(q_scaled, kb, i * qb, j)
            later = total - cp - _mm_tri(lk, lower_incl)
            w = jnp.where(valid, jnp.exp(z - sp + later), 0.0)
            e = w * _mm_raw(d_out, vb, "nt", "bf16")
            before = ep + _mm_tri(e, lower)
            dz = jnp.where(valid, e * jnp.exp(-sp) - before * jnp.exp(z - sp), 0.0)
            dk_ref[rows, :] += _mm_raw(dz, q_scaled, "tn", "bf16")
            dv_ref[rows, :] += _mm_raw(w, d_out, "tn", "bf16")
            acc = acc + _mm_raw(dz, kb, "nn", "bf16")
            return acc, cp + jnp.sum(lk, axis=1, keepdims=True), ep + jnp.sum(e, axis=1, keepdims=True)

        zero_col = jnp.zeros((qb, 1), F32)
        acc, _, _ = lax.fori_loop(n_blocks - n_live, n_blocks, step, (jnp.zeros((qb, HEAD_DIM), F32), zero_col, zero_col))
        dq_ref[...] = (acc * scale).astype(dq_ref.dtype)

        @pl.when(i == lp // qb - 1)
        def _():
            dk_out[...] = dk_ref[...].astype(dk_out.dtype)
            dv_out[...] = dv_ref[...].astype(dv_out.dtype)

        if ride:
            @pl.when((pl.program_id(0) == nh - 1) & (i == lp // qb - 1))
            def _():
                ride.finish(*ride_refs)

    qspec = pl.BlockSpec((qb, HEAD_DIM), lambda h, i: (i, offs[0] + h))
    kspec = pl.BlockSpec((lp, HEAD_DIM), lambda h, i: (0, offs[1] + h))
    vspec = pl.BlockSpec((lp, HEAD_DIM), lambda h, i: (0, offs[2] + h))
    ospec = pl.BlockSpec((qb, HEAD_DIM), lambda h, i: (i, h))
    fullspec = pl.BlockSpec((lp, HEAD_DIM), lambda h, i: (0, h))
    return pl.pallas_call(
        body, name="sb_bwd", grid=(nh, lp // qb), in_specs=[qspec, kspec, vspec, ospec] + [ANY] * n_in,
        out_specs=[ospec, fullspec, fullspec] + [ANY] * n_out,
        out_shape=[jax.ShapeDtypeStruct((lp, nh * HEAD_DIM), BF16)] * 3 + (ride.out_shapes if ride else []),
        scratch_shapes=[pltpu.VMEM((lp, HEAD_DIM), F32)] * 2 + (ride.sems if ride else []),
        input_output_aliases={4 + k: 3 + k for k in range(n_in)} if ride and ride.alias else {},
        compiler_params=pltpu.CompilerParams(dimension_semantics=("arbitrary", "arbitrary"), vmem_limit_bytes=VMEM_LIMIT),
    )(src, src, src, d_o, *(ride.ins if ride else []))


def _pick(n, target, unit):
    if n <= target:
        return n
    best = None
    for d in range(unit, target + 1, unit):
        if n % d == 0:
            best = d
    assert best is not None, (n, target, unit)
    return best


def matmul(a, b, mode="nn", *, name, bm=1088, bn=640, bk=2176, residual=None, out_dtype=F32, b_koff=0, ride=None):
    if mode == "nn":
        (m, k), n = a.shape, b.shape[1]
    elif mode == "nt":
        (m, k), n = a.shape, b.shape[0]
    else:
        (k, m), n = a.shape, b.shape[1]
    assert b_koff == 0 or mode == "nt"
    bm = _pick(m, bm, 128 if mode == "tn" else 8)
    bn = _pick(n, bn, 128 if mode != "nt" else 8)
    bk = _pick(k, bk, 128 if mode != "tn" else 8)
    nk = k // bk
    n_plain = 2 if residual is None else 3
    n_in = len(ride.ins) if ride else 0
    n_out = len(ride.out_shapes) if ride else 0
    steps = (m // bm, n // bn, nk)

    def body(*refs):
        a_ref, b_ref = refs[:2]
        r_ref = None if residual is None else refs[2]
        o_ref = refs[n_plain + n_in]
        acc = refs[n_plain + n_in + 1 + n_out]
        ride_refs = (refs[n_plain:n_plain + n_in], refs[n_plain + n_in + 1:n_plain + n_in + 1 + n_out], refs[n_plain + n_in + 2 + n_out:])
        kk = pl.program_id(2)
        here = [pl.program_id(d) for d in range(3)]
        if ride:
            @pl.when((here[0] == 0) & (here[1] == 0) & (here[2] == 0))
            def _():
                ride.start(*ride_refs)
        part = _mm_raw(a_ref[...], b_ref[...], mode, "bf16")

        @pl.when(kk == 0)
        def _():
            acc[...] = part

        @pl.when(kk > 0)
        def _():
            acc[...] += part

        @pl.when(kk == nk - 1)
        def _():
            res = acc[...]
            if r_ref is not None:
                res = res + r_ref[...]
            o_ref[...] = res.astype(out_dtype)

        if ride:
            @pl.when((here[0] == steps[0] - 1) & (here[1] == steps[1] - 1) & (here[2] == steps[2] - 1))
            def _():
                ride.finish(*ride_refs)

    a_spec = pl.BlockSpec((bk, bm), lambda i, j, kk: (kk, i)) if mode == "tn" else pl.BlockSpec((bm, bk), lambda i, j, kk: (i, kk))
    b_spec = pl.BlockSpec((bn, bk), lambda i, j, kk: (j, b_koff + kk)) if mode == "nt" else pl.BlockSpec((bk, bn), lambda i, j, kk: (kk, j))
    o_spec = pl.BlockSpec((bm, bn), lambda i, j, kk: (i, j))
    ins, specs = [a, b], [a_spec, b_spec]
    if residual is not None:
        ins.append(residual)
        specs.append(o_spec)
    res = pl.pallas_call(
        body, name=name, grid=steps, in_specs=specs + [ANY] * n_in, out_specs=[o_spec] + [ANY] * n_out,
        out_shape=[jax.ShapeDtypeStruct((m, n), out_dtype)] + (ride.out_shapes if ride else []),
        scratch_shapes=[pltpu.VMEM((bm, bn), F32)] + (ride.sems if ride else []),
        compiler_params=pltpu.CompilerParams(dimension_semantics=("arbitrary", "arbitrary", "arbitrary"), vmem_limit_bytes=VMEM_LIMIT),
    )(*ins, *(ride.ins if ride else []))
    return res if ride else res[0]


def _row_specs(rows, params, width, bm):
    row_specs = [pl.BlockSpec((bm, width), (lambda j, i, off=off: (i, off + j))) for _, off in rows]
    par_specs = [pl.BlockSpec((p.shape[0], width) if per_col else p.shape, ((lambda j, i: (0, j)) if per_col else (lambda j, i: (0, 0))))
                 for p, per_col in params]
    return row_specs, par_specs


def rowmap_fwd(name, fn, rows, params, n_out, *, width, ncol, bm, lp, out_dtypes=None):
    out_dtypes = out_dtypes or [F32] * n_out
    row_specs, par_specs = _row_specs(rows, params, width, bm)
    nr = len(rows)

    def body(*refs):
        ins, outs = refs[:nr + len(params)], refs[nr + len(params):]
        row_ids = pl.program_id(1) * bm + _iota2((bm, 1), 0)
        res = fn(row_ids, *[r[...].astype(F32) for r in ins])
        for o_ref, val in zip(outs, res):
            o_ref[...] = val.astype(o_ref.dtype)

    o_spec = pl.BlockSpec((bm, width), lambda j, i: (i, j))
    return pl.pallas_call(
        body, name=name, grid=(ncol, lp // bm), in_specs=row_specs + par_specs, out_specs=[o_spec] * n_out,
        out_shape=[jax.ShapeDtypeStruct((lp, ncol * width), dt) for dt in out_dtypes],
        compiler_params=pltpu.CompilerParams(dimension_semantics=("arbitrary", "arbitrary"), vmem_limit_bytes=VMEM_LIMIT),
    )(*[a for a, _ in rows], *[p for p, _ in params])


def rowmap_bwd(name, fn, rows, params, d_outs, *, width, ncol, bm, lp, d_row_dtypes=None):
    d_row_dtypes = d_row_dtypes or [F32] * len(rows)
    row_specs, par_specs = _row_specs(rows, params, width, bm)
    nr, npar, nout = len(rows), len(params), len(d_outs)

    def body(*refs):
        ins = refs[:nr + npar]
        dos = refs[nr + npar:nr + npar + nout]
        d_rows = refs[nr + npar + nout:nr + npar + nout + nr]
        d_pars = refs[nr + npar + nout + nr:]
        j, i = pl.program_id(0), pl.program_id(1)
        row_ids = i * bm + _iota2((bm, 1), 0)
        _, pull = jax.vjp(lambda *xs: tuple(fn(row_ids, *xs)), *[r[...].astype(F32) for r in ins])
        grads = pull(tuple(d[...].astype(F32) for d in dos))
        for ref, val in zip(d_rows, grads[:nr]):
            ref[...] = val.astype(ref.dtype)
        for ref, val, (_, per_col) in zip(d_pars, grads[nr:], params):
            first = (i == 0) if per_col else ((i == 0) & (j == 0))

            @pl.when(first)
            def _(ref=ref, val=val):
                ref[...] = val

            @pl.when(jnp.logical_not(first))
            def _(ref=ref, val=val):
                ref[...] += val

    o_spec = pl.BlockSpec((bm, width), lambda j, i: (i, j))
    res = pl.pallas_call(
        body, name=name, grid=(ncol, lp // bm), in_specs=row_specs + par_specs + [o_spec] * nout,
        out_specs=[o_spec] * nr + par_specs,
        out_shape=[jax.ShapeDtypeStruct((lp, ncol * width), dt) for dt in d_row_dtypes]
        + [jax.ShapeDtypeStruct(p.shape, F32) for p, _ in params],
        compiler_params=pltpu.CompilerParams(dimension_semantics=("arbitrary", "arbitrary"), vmem_limit_bytes=VMEM_LIMIT),
    )(*[a for a, _ in rows], *[p for p, _ in params], *d_outs)
    return res[:nr], res[nr:]


def _silu(x):
    return x * jax.nn.sigmoid(x)


def _softplus(x):
    return jnp.maximum(x, 0.0) + jnp.log(1.0 + jnp.exp(-jnp.abs(x)))


def _real_rows(row_ids):
    return (row_ids >= PAD).astype(F32)


def _f_rmsnorm(row_ids, h, g):
    return (h * lax.rsqrt(jnp.mean(h * h, axis=-1, keepdims=True) + RMS_EPS) * g,)


def _f_small_gates(row_ids, small, bias, a_log):
    lane = _iota2(small.shape, 1)
    t = small + bias
    sp = _softplus(t)
    coef = -jnp.exp(a_log)
    keep = _real_rows(row_ids)
    first = jnp.where(lane < 8, jax.nn.sigmoid(t), jnp.where(lane < 16, coef * sp, jnp.where(lane < 48, sp, 0.0)))
    second = jnp.where((lane >= 8) & (lane < 48), coef * sp, 0.0)
    return first * keep, second * keep


def _f_gate_silu(row_ids, o, z):
    return (o * _silu(z),)


def _f_head_norm_gate(row_ids, o, z, g):
    out = []
    for h in range(o.shape[1] // HEAD_DIM):
        oh = o[:, h * HEAD_DIM:(h + 1) * HEAD_DIM]
        out.append(oh * lax.rsqrt(jnp.mean(oh * oh, axis=-1, keepdims=True) + RMS_EPS) * g)
    return (jnp.concatenate(out, axis=1) * _silu(z),)


def _f_ssm_out(row_ids, y, xh, z, d_skip, g):
    t = (y + d_skip * xh) * _silu(z)
    return (t * lax.rsqrt(jnp.mean(t * t, axis=-1, keepdims=True) + RMS_EPS) * g,)


def _f_merge(row_ids, pa, pb, pc, ga, gb, gc):
    return (jax.nn.sigmoid(ga) * pa + jax.nn.sigmoid(gb) * pb + jax.nn.sigmoid(gc) * pc,)


def _shift_rows(x, s):
    s = s % x.shape[0]
    return x if s == 0 else pltpu.roll(x, s, 0)


def _conv_pre(x, w, b):
    pre = b
    for kk in range(CONV_K):
        pre = pre + w[kk:kk + 1, :] * _shift_rows(x, CONV_K - 1 - kk)
    return pre


def _conv_post(pre, l2_flag, keep):
    act = _silu(pre)
    nrm = act * lax.rsqrt(jnp.sum(act * act, axis=-1, keepdims=True) + L2_EPS)
    return (l2_flag * nrm + (1.0 - l2_flag) * act) * keep


def conv_fwd(name, src, col_off, w, b, n_l2):
    lp, ch = src.shape[0], w.shape[1]

    def body(x_ref, w_ref, b_ref, o_ref):
        l2_flag = (pl.program_id(0) < n_l2).astype(F32)
        keep = _real_rows(_iota2((lp, 1), 0))
        o_ref[...] = _conv_post(_conv_pre(x_ref[...], w_ref[...], b_ref[...]), l2_flag, keep)

    return pl.pallas_call(
        body, name=name, grid=(ch // HEAD_DIM,),
        in_specs=[pl.BlockSpec((lp, HEAD_DIM), lambda j: (0, col_off + j)), pl.BlockSpec((CONV_K, HEAD_DIM), lambda j: (0, j)),
                  pl.BlockSpec((1, HEAD_DIM), lambda j: (0, j))],
        out_specs=pl.BlockSpec((lp, HEAD_DIM), lambda j: (0, j)),
        out_shape=jax.ShapeDtypeStruct((lp, ch), F32),
        compiler_params=pltpu.CompilerParams(dimension_semantics=("arbitrary",), vmem_limit_bytes=VMEM_LIMIT),
    )(src, w, b)


def conv_bwd(name, src, col_off, w, b, n_l2, d_out):
    lp, ch = src.shape[0], w.shape[1]

    def body(x_ref, w_ref, b_ref, do_ref, dx_ref, dw_ref, db_ref):
        l2_flag = (pl.program_id(0) < n_l2).astype(F32)
        keep = _real_rows(_iota2((lp, 1), 0))
        x, wv = x_ref[...], w_ref[...]
        pre = _conv_pre(x, wv, b_ref[...])
        _, pull = jax.vjp(lambda p: _conv_post(p, l2_flag, keep), pre)
        (d_pre,) = pull(do_ref[...])
        dx = jnp.zeros_like(x)
        for kk in range(CONV_K):
            s = CONV_K - 1 - kk
            dx = dx + wv[kk:kk + 1, :] * _shift_rows(d_pre, -s)
            dw_ref[kk:kk + 1, :] = jnp.sum(d_pre * _shift_rows(x, s), axis=0, keepdims=True)
        dx_ref[...] = (dx * keep).astype(dx_ref.dtype)
        db_ref[...] = jnp.sum(d_pre, axis=0, keepdims=True)

    seq = pl.BlockSpec((lp, HEAD_DIM), lambda j: (0, j))
    wspec = pl.BlockSpec((CONV_K, HEAD_DIM), lambda j: (0, j))
    bspec = pl.BlockSpec((1, HEAD_DIM), lambda j: (0, j))
    return pl.pallas_call(
        body, name=name, grid=(ch // HEAD_DIM,),
        in_specs=[pl.BlockSpec((lp, HEAD_DIM), lambda j: (0, col_off + j)), wspec, bspec, seq],
        out_specs=[seq, wspec, bspec],
        out_shape=[jax.ShapeDtypeStruct((lp, ch), BF16), jax.ShapeDtypeStruct(w.shape, F32), jax.ShapeDtypeStruct(b.shape, F32)],
        compiler_params=pltpu.CompilerParams(dimension_semantics=("arbitrary",), vmem_limit_bytes=VMEM_LIMIT),
    )(src, w, b, d_out)


def loss_head(h, target, g):
    lp, d = h.shape
    bm = SB_BLOCK
    first = (PAD + N_META) // bm

    def body(h_ref, t_ref, g_ref, loss_ref, dh_ref, dg_ref):
        i = pl.program_id(0)
        keep = (i >= first).astype(F32)

        def f(hv, gv):
            y = hv * lax.rsqrt(jnp.mean(hv * hv, axis=-1, keepdims=True) + RMS_EPS) * gv
            err = y - t_ref[...]
            return 0.5 * jnp.sum(jnp.mean(err * err, axis=-1, keepdims=True), axis=0, keepdims=True) * keep

        val, pull = jax.vjp(f, h_ref[...], g_ref[...])
        dh, dg = pull(jnp.ones((1, 1), F32))
        dh_ref[...] = dh

        @pl.when(i == 0)
        def _():
            loss_ref[...] = val
            dg_ref[...] = dg

        @pl.when(i > 0)
        def _():
            loss_ref[...] += val
            dg_ref[...] += dg

    row = pl.BlockSpec((bm, d), lambda i: (i, 0))
    return pl.pallas_call(
        body, name="loss_head", grid=(lp // bm,),
        in_specs=[row, pl.BlockSpec((bm, d), lambda i: (jnp.maximum(i - first, 0), 0)), pl.BlockSpec((1, d), lambda i: (0, 0))],
        out_specs=[pl.BlockSpec((1, 1), lambda i: (0, 0)), row, pl.BlockSpec((1, d), lambda i: (0, 0))],
        out_shape=[jax.ShapeDtypeStruct((1, 1), F32), jax.ShapeDtypeStruct((lp, d), F32), jax.ShapeDtypeStruct((1, d), F32)],
        compiler_params=pltpu.CompilerParams(dimension_semantics=("arbitrary",)),
    )(h, target, g)


ADAM_LR, ADAM_B1, ADAM_B2, ADAM_EPS, ADAM_WD, ADAM_STEP = 0.001, 0.9, 0.999, 1e-08, 0.01, 10


def adamw(name, w, g, m, v, echo_g=False):
    lead = w.shape[:-2]
    rows, cols = w.shape[-2:]
    br = _pick(rows, 128, 8)
    n_out = 4 if echo_g else 3

    def body(w_ref, g_ref, m_ref, v_ref, d_ref, nm_ref, nv_ref, *echo):
        gv = g_ref[...]
        nm = ADAM_B1 * m_ref[...] + (1.0 - ADAM_B1) * gv
        nv = ADAM_B2 * v_ref[...] + (1.0 - ADAM_B2) * (gv * gv)
        m_hat = nm / (1.0 - ADAM_B1 ** ADAM_STEP)
        v_hat = nv / (1.0 - ADAM_B2 ** ADAM_STEP)
        d_ref[...] = -ADAM_LR * (m_hat / (jnp.sqrt(v_hat) + ADAM_EPS) + ADAM_WD * w_ref[...])
        nm_ref[...] = nm
        nv_ref[...] = nv
        for e_ref in echo:
            e_ref[...] = gv

    if lead and rows <= 8:
        bl = _pick(lead[0], 32, 1)
        spec = pl.BlockSpec((bl, rows, cols), lambda s: (s, 0, 0))
        grid = (lead[0] // bl,)
    elif lead:
        spec = pl.BlockSpec((None, br, cols), lambda s, i: (s, i, 0))
        grid = (lead[0], rows // br)
    else:
        spec = pl.BlockSpec((br, cols), lambda i: (i, 0))
        grid = (rows // br,)
    return pl.pallas_call(
        body, name=name, grid=grid, in_specs=[spec] * 4, out_specs=[spec] * n_out,
        out_shape=[jax.ShapeDtypeStruct(w.shape, F32)] * n_out,
        compiler_params=pltpu.CompilerParams(dimension_semantics=("arbitrary",) * len(grid), vmem_limit_bytes=VMEM_LIMIT),
    )(w, g, m, v)


MESH = pl.DeviceIdType.MESH
ANY = pl.BlockSpec(memory_space=pl.ANY)
D2D_PIECES = 16
ICI_PIECES = 4


def _place():
    x, y, c = lax.axis_index("x"), lax.axis_index("y"), lax.axis_index("c")
    return x, y, c, [(1 - x, y), (x, 1 - y), (1 - x, 1 - y)]


def _pieces(rows, n, unit):
    per = -(-rows // (n * unit)) * unit
    return [(s, min(per, rows - s)) for s in range(0, rows, per)]


def _row_unit(dtype):
    return 16 if dtype == BF16 else 8


def _scalar(v):
    return jnp.reshape(v, (1,)).astype(jnp.int32)


def place_shard(name, pack):
    rows, cols = pack.shape
    br = _pick(rows, 256, 16)

    def body(m_ref, p_ref, o_ref):
        o_ref[...] = p_ref[...]

    return pl.pallas_call(
        body, name=name,
        grid_spec=pltpu.PrefetchScalarGridSpec(
            num_scalar_prefetch=1, grid=(rows // br,),
            in_specs=[pl.BlockSpec((br, cols), lambda i, m: (i, 0))],
            out_specs=pl.BlockSpec((None, br, cols), lambda i, m: (m[0], i, 0))),
        out_shape=jax.ShapeDtypeStruct((4, rows, cols), pack.dtype),
        compiler_params=pltpu.CompilerParams(dimension_semantics=("arbitrary",), vmem_limit_bytes=VMEM_LIMIT),
    )(_scalar(2 * lax.axis_index("x") + lax.axis_index("y")), pack)


class Ride:
    def __init__(self, ins, out_shapes, alias, sems, start, finish):
        self.ins, self.out_shapes, self.alias, self.sems, self.start, self.finish = list(ins), out_shapes, alias, sems, start, finish


def _gather_parts(o_refs, send_sems, recv_sems):
    x, y, c, chips = _place()
    mine = 2 * x + y

    def half_rows(b, which, start=0, size=None):
        half = o_refs[b].shape[1] // 2
        return pl.ds(pl.multiple_of(which * half + start, _row_unit(o_refs[b].dtype)), half if size is None else size)

    def remote(b, k, slot, rws, to):
        piece = o_refs[b].at[slot, rws, :]
        return pltpu.make_async_remote_copy(src_ref=piece, dst_ref=piece, send_sem=send_sems.at[b, k], recv_sem=recv_sems.at[b, k],
                                            device_id=to, device_id_type=MESH)

    return x, y, c, chips, mine, half_rows, remote


def _gather_start(o_refs, send_sems, recv_sems):
    x, y, c, chips, mine, half_rows, remote = _gather_parts(o_refs, send_sems, recv_sems)
    for b, o_ref in enumerate(o_refs):
        for j, (cx, cy) in enumerate(chips):
            for start, size in _pieces(o_ref.shape[1] // 2, ICI_PIECES, _row_unit(o_ref.dtype)):
                remote(b, j, mine, half_rows(b, c, start, size), (cx, cy, c)).start()


def _gather_finish(o_refs, send_sems, recv_sems):
    x, y, c, chips, mine, half_rows, remote = _gather_parts(o_refs, send_sems, recv_sems)
    sends = []
    for b, o_ref in enumerate(o_refs):
        for j, (cx, cy) in enumerate(chips):
            slot = 2 * cx + cy
            sends.append(remote(b, j, mine, half_rows(b, c), (cx, cy, c)))
            remote(b, j, slot, half_rows(b, c), (cx, cy, c)).wait_recv()
            for start, size in _pieces(o_ref.shape[1] // 2, D2D_PIECES, _row_unit(o_ref.dtype)):
                remote(b, 3 + j, slot, half_rows(b, c, start, size), (x, y, 1 - c)).start()
            sends.append(remote(b, 3 + j, slot, half_rows(b, c), (x, y, 1 - c)))
    for b in range(len(o_refs)):
        for j, (cx, cy) in enumerate(chips):
            remote(b, 3 + j, 2 * cx + cy, half_rows(b, 1 - c), (x, y, 1 - c)).wait_recv()
    for cp in sends:
        cp.wait_send()


def gather_ride(placed):
    n = len(placed)
    return Ride(placed, [jax.ShapeDtypeStruct(p.shape, p.dtype) for p in placed], True,
                [pltpu.SemaphoreType.DMA((n, 6)), pltpu.SemaphoreType.DMA((n, 6))],
                lambda ins, outs, sems: _gather_start(outs, *sems), lambda ins, outs, sems: _gather_finish(outs, *sems))


def gather_shards(name, placed):
    n = len(placed)

    def body(*refs):
        o_refs, sems = refs[n:2 * n], refs[2 * n:]
        _gather_start(o_refs, *sems)
        _gather_finish(o_refs, *sems)

    return pl.pallas_call(
        body, name=name, in_specs=[ANY] * n, out_specs=[ANY] * n,
        out_shape=[jax.ShapeDtypeStruct(p.shape, p.dtype) for p in placed],
        input_output_aliases={i: i for i in range(n)},
        scratch_shapes=[pltpu.SemaphoreType.DMA((n, 6)), pltpu.SemaphoreType.DMA((n, 6))],
    )(*placed)


def _split_copies(g_refs, t_refs, send_sems, recv_sems, start):
    x, y, c, _ = _place()
    waits = []
    for b, (g_ref, t_ref) in enumerate(zip(g_refs, t_refs)):
        half = g_ref.shape[2]

        def copy(slots, first, size):
            rws = pl.ds(first, size)
            return pltpu.make_async_remote_copy(src_ref=g_ref.at[slots, 1 - c, rws, :], dst_ref=t_ref.at[slots, rws, :],
                                                send_sem=send_sems.at[b], recv_sem=recv_sems.at[b], device_id=(x, y, 1 - c),
                                                device_id_type=MESH)

        if start:
            for s in range(4):
                for first, size in _pieces(half, D2D_PIECES // 4, _row_unit(g_ref.dtype)):
                    copy(s, first, size).start()
        else:
            waits.append(copy(slice(None), 0, half))
    return waits


def _split_finish(g_refs, t_refs, send_sems, recv_sems):
    for cp in _split_copies(g_refs, t_refs, send_sems, recv_sems, False):
        cp.wait()


def _split_shapes(bufs):
    return [jax.ShapeDtypeStruct((4,) + g.shape[2:], g.dtype) for g in bufs]


def split_ride(bufs):
    n = len(bufs)
    return Ride(bufs, _split_shapes(bufs), False, [pltpu.SemaphoreType.DMA((n,)), pltpu.SemaphoreType.DMA((n,))],
                lambda ins, outs, sems: _split_copies(ins, outs, *sems, True), lambda ins, outs, sems: _split_finish(ins, outs, *sems))


def pair_split(name, bufs):
    n = len(bufs)

    def body(*refs):
        g_refs, t_refs, sems = refs[:n], refs[n:2 * n], refs[2 * n:]
        _split_copies(g_refs, t_refs, *sems, True)
        _split_finish(g_refs, t_refs, *sems)

    return pl.pallas_call(
        body, name=name, in_specs=[ANY] * n, out_specs=[ANY] * n, out_shape=_split_shapes(bufs),
        scratch_shapes=[pltpu.SemaphoreType.DMA((n,)), pltpu.SemaphoreType.DMA((n,))],
    )(*bufs)


def pair_add(name, g, theirs, transit):
    _, _, half, cols = g.shape
    br = _pick(half, 128, 16)

    def body(c_ref, g_ref, t_ref, o_ref):
        o_ref[...] = (g_ref[...].astype(F32) + t_ref[...].astype(F32)).astype(transit)

    blk = (4, br, cols)
    return pl.pallas_call(
        body, name=name,
        grid_spec=pltpu.PrefetchScalarGridSpec(
            num_scalar_prefetch=1, grid=(half // br,),
            in_specs=[pl.BlockSpec((4, None, br, cols), lambda i, c: (0, c[0], i, 0)), pl.BlockSpec(blk, lambda i, c: (0, i, 0))],
            out_specs=pl.BlockSpec(blk, lambda i, c: (0, i, 0))),
        out_shape=jax.ShapeDtypeStruct((4, half, cols), transit),
        compiler_params=pltpu.CompilerParams(dimension_semantics=("arbitrary",), vmem_limit_bytes=VMEM_LIMIT),
    )(_scalar(lax.axis_index("c")), g, theirs)


def _exchange_copies(a_refs, o_refs, send_sems, recv_sems, start):
    x, y, c, chips = _place()
    mine = 2 * x + y
    waits = []
    for b, (a_ref, o_ref) in enumerate(zip(a_refs, o_refs)):
        rows = a_ref.shape[1]
        for j, (cx, cy) in enumerate(chips):
            def copy(first, size):
                rws = pl.ds(first, size)
                return pltpu.make_async_remote_copy(src_ref=a_ref.at[2 * cx + cy, rws, :], dst_ref=o_ref.at[mine, rws, :],
                                                    send_sem=send_sems.at[b, j], recv_sem=recv_sems.at[b, j],
                                                    device_id=(cx, cy, c), device_id_type=MESH)
            if start:
                for first, size in _pieces(rows, ICI_PIECES, _row_unit(a_ref.dtype)):
                    copy(first, size).start()
            else:
                waits.append(copy(0, rows))
    return waits


def _exchange_finish(a_refs, o_refs, send_sems, recv_sems):
    for cp in _exchange_copies(a_refs, o_refs, send_sems, recv_sems, False):
        cp.wait()


def exchange_ride(parts):
    n = len(parts)
    return Ride(parts, [jax.ShapeDtypeStruct(a.shape, a.dtype) for a in parts], False,
                [pltpu.SemaphoreType.DMA((n, 3)), pltpu.SemaphoreType.DMA((n, 3))],
                lambda ins, outs, sems: _exchange_copies(ins, outs, *sems, True), lambda ins, outs, sems: _exchange_finish(ins, outs, *sems))


def chip_exchange(name, parts):
    n = len(parts)

    def body(*refs):
        a_refs, o_refs, sems = refs[:n], refs[n:2 * n], refs[2 * n:]
        _exchange_copies(a_refs, o_refs, *sems, True)
        _exchange_finish(a_refs, o_refs, *sems)

    return pl.pallas_call(
        body, name=name, in_specs=[ANY] * n, out_specs=[ANY] * n,
        out_shape=[jax.ShapeDtypeStruct(a.shape, a.dtype) for a in parts],
        scratch_shapes=[pltpu.SemaphoreType.DMA((n, 3)), pltpu.SemaphoreType.DMA((n, 3))],
    )(*parts)


def chip_add(name, got, part):
    _, rows, cols = got.shape
    br = _pick(rows, 128, 16)
    nblk = rows // br

    def body(m_ref, c_ref, got_ref, part_ref, o_ref):
        mine = m_ref[0]
        for s in range(4):
            @pl.when(mine == s)
            def _(s=s):
                val = part_ref[...].astype(F32)
                o_ref[...] = val if s == 0 else o_ref[...] + val

            @pl.when(mine != s)
            def _(s=s):
                val = got_ref[s].astype(F32)
                o_ref[...] = val if s == 0 else o_ref[...] + val

    return pl.pallas_call(
        body, name=name,
        grid_spec=pltpu.PrefetchScalarGridSpec(
            num_scalar_prefetch=2, grid=(nblk,),
            in_specs=[pl.BlockSpec((4, br, cols), lambda i, m, c: (0, i, 0)),
                      pl.BlockSpec((None, br, cols), lambda i, m, c: (m[0], i, 0))],
            out_specs=pl.BlockSpec((br, cols), lambda i, m, c: (c[0] * nblk + i, 0))),
        out_shape=jax.ShapeDtypeStruct((2 * rows, cols), F32),
        compiler_params=pltpu.CompilerParams(dimension_semantics=("arbitrary",), vmem_limit_bytes=VMEM_LIMIT),
    )(_scalar(2 * lax.axis_index("x") + lax.axis_index("y")), _scalar(lax.axis_index("c")), got, part)


def pair_join(name, fulls):
    n = len(fulls)

    def body(*refs):
        o_refs = refs[n:2 * n]
        send_sems, recv_sems = refs[2 * n:]
        x, y, c, _ = _place()
        waits = []
        for b, o_ref in enumerate(o_refs):
            half = o_ref.shape[0] // 2
            unit = _row_unit(o_ref.dtype)

            def copy(start, size):
                piece = o_ref.at[pl.ds(pl.multiple_of(c * half + start, unit), size), :]
                return pltpu.make_async_remote_copy(src_ref=piece, dst_ref=piece, send_sem=send_sems.at[b], recv_sem=recv_sems.at[b],
                                                    device_id=(x, y, 1 - c), device_id_type=MESH)

            for start, size in _pieces(half, D2D_PIECES, unit):
                copy(start, size).start()
            waits.append(copy(0, half))
        for cp in waits:
            cp.wait()

    return pl.pallas_call(
        body, name=name, in_specs=[ANY] * n, out_specs=[ANY] * n,
        out_shape=[jax.ShapeDtypeStruct(f.shape, f.dtype) for f in fulls],
        input_output_aliases={i: i for i in range(n)},
        scratch_shapes=[pltpu.SemaphoreType.DMA((n,)), pltpu.SemaphoreType.DMA((n,))],
    )(*fulls)


D_IN = 15920
D_PROJ = 16000
_SEGMENTS = ((0, 8192), (8208, 12816), (12848, 15920), (8192, 8208), (12816, 12848))
OFF_SB_Z, OFF_GDN_QKV, OFF_GDN_Z, OFF_SSM_Z, OFF_SSM_XBC, OFF_GATES, OFF_SMALL = 3072, 4096, 7168, 8192, 10240, 12800, 15872
PACK_C = 1024
WEIGHTS = ("meta_tokens", "norm_g", "w_in", "gdn_conv_w", "gdn_a_log", "gdn_dt_bias", "gdn_norm_g", "ssm_conv_w", "ssm_conv_b",
           "ssm_a_log", "ssm_dt_bias", "ssm_d", "ssm_norm_g", "w_branch_a", "w_branch_b", "w_branch_c", "w_out", "final_norm_g")
SHARDED = ("w_in", "w_branch_a", "w_branch_b", "w_branch_c", "w_out", "gdn_conv_w", "ssm_conv_w", "meta_tokens")
SHARD_AXIS = {"w_in": 2, "w_branch_a": 1, "w_branch_b": 1, "w_branch_c": 1, "w_out": 1, "gdn_conv_w": 2, "ssm_conv_w": 2, "meta_tokens": 1}
BRANCH = ("w_branch_a", "w_branch_b", "w_branch_c", "w_out")
EXACT = ("gdn_conv_w", "ssm_conv_w", "meta_tokens")
REPLICATED = tuple(n for n in WEIGHTS if n not in SHARDED)


def _regrouped_from_shards(shard_cols):
    out = []
    for a, b in _SEGMENTS:
        while a < b:
            chip = a // shard_cols
            stop = min(b, (chip + 1) * shard_cols)
            out.append((chip, a - chip * shard_cols, stop - chip * shard_cols))
            a = stop
    return out


def _shard_from_regrouped(chip, shard_cols):
    lo, hi = chip * shard_cols, (chip + 1) * shard_cols
    out, pos = [], 0
    starts = {}
    for a, b in _SEGMENTS:
        starts[(a, b)] = pos
        pos += b - a
    for a, b in sorted(_SEGMENTS):
        s0, s1 = max(a, lo), min(b, hi)
        if s0 < s1:
            out.append((starts[(a, b)] + s0 - a, starts[(a, b)] + s1 - a))
    return out


def _pack(parts, row_unit=64):
    n = sum(p.shape[0] for p in parts)
    rows = -(-n // (PACK_C * row_unit)) * row_unit
    flat = jnp.concatenate(list(parts) + [jnp.zeros((rows * PACK_C - n,), parts[0].dtype)])
    return flat.reshape(rows, PACK_C)


def _unpack(buf, shapes):
    flat, out, pos = buf.reshape(-1), [], 0
    for shp in shapes:
        n = math.prod(shp)
        out.append(flat[pos:pos + n].reshape(shp))
        pos += n
    return out


def _as_bf16_words(a):
    return lax.bitcast_convert_type(a, BF16).reshape(-1)


BRANCH_ROWS = (D_MODEL // 4, D_MODEL // 4, SSM_INNER // 4, D_MODEL // 4)


def _place_weights(w):
    depth = w["w_in"].shape[0]
    layers = []
    for l in range(depth):
        a = w["w_in"][l].astype(BF16)
        b = jnp.concatenate([w[n][l] for n in BRANCH], axis=0).astype(BF16)
        layers.append([place_shard("place_w_in", a), place_shard("place_branch", b)])
    small = place_shard("place_exact", _pack([_as_bf16_words(w[n]) for n in EXACT]))
    return layers, small


def _exact_weights(w, got_s):
    per_chip = [_unpack(got_s[c], [w[n].shape + (2,) for n in EXACT]) for c in range(4)]
    return {n: jnp.concatenate([lax.bitcast_convert_type(per_chip[c][i], F32) for c in range(4)], axis=SHARD_AXIS[n])
            for i, n in enumerate(EXACT)}


def _layer_weights(got_a, got_b):
    d_model, shard_cols = got_a.shape[1:]
    pad = jnp.zeros((d_model, D_PROJ - D_IN), BF16)
    out = {"wp": jnp.concatenate([got_a[c, :, lo:hi] for c, lo, hi in _regrouped_from_shards(shard_cols)] + [pad], axis=1)}
    pos = 0
    for n, rows in zip(BRANCH, BRANCH_ROWS):
        out[n] = jnp.concatenate([got_b[c, pos:pos + rows] for c in range(4)], axis=0)
        pos += rows
    return out


def _shard(a, axis, s):
    size = a.shape[axis] // 4
    return lax.slice_in_dim(a, s * size, (s + 1) * size, axis=axis)


def _layer_grad_buffers(g, shard_cols):
    buf_a = jnp.stack([jnp.concatenate([g["w_in"][:, lo:hi] for lo, hi in _shard_from_regrouped(s, shard_cols)], axis=1)
                       for s in range(4)]).astype(BF16)
    buf_b = jnp.stack([jnp.concatenate([_shard(g[n], 0, s) for n in BRANCH], axis=0) for s in range(4)]).astype(BF16)
    return [buf_a, buf_b]


def _split_halves(buf):
    return buf.reshape(4, 2, buf.shape[1] // 2, buf.shape[2])


def _start_reduce(tag, bufs, transits):
    bufs = [_split_halves(g) for g in bufs]
    theirs = pair_split(tag + "_pair_split", bufs)
    return [pair_add(f"{tag}_pair_add_{i}", g, t, tr) for i, (g, t, tr) in enumerate(zip(bufs, theirs, transits))]


def _layer_params(w, exact, weights, l):
    lane = lambda v, lo: jnp.pad(v, (lo, HEAD_DIM - lo - v.shape[0]))[None]
    return dict(
        norm_g=w["norm_g"][l][None], wp=weights["wp"],
        gdn_conv_w=exact["gdn_conv_w"][l], gdn_conv_b=jnp.zeros((1, 3 * N_HEADS * HEAD_DIM), F32),
        ssm_conv_w=exact["ssm_conv_w"][l], ssm_conv_b=w["ssm_conv_b"][l][None],
        bias_vec=lane(w["gdn_dt_bias"][l], 8) + lane(w["ssm_dt_bias"][l], 16),
        alog_vec=lane(w["gdn_a_log"][l], 8) + lane(w["ssm_a_log"][l], 16),
        gdn_norm_g=w["gdn_norm_g"][l][None], d_skip=jnp.repeat(w["ssm_d"][l], SSM_P)[None], ssm_norm_g=w["ssm_norm_g"][l][None],
        wa=weights["w_branch_a"], wb=weights["w_branch_b"], wc=weights["w_branch_c"], wo=weights["w_out"])


def _layer_fwd(h, p, ride):
    lp = h.shape[0]
    bm = _pick(lp, 272, 8)
    kw = dict(bm=bm, lp=lp)
    (u,) = rowmap_fwd("rms_fwd", _f_rmsnorm, [(h, 0)], [(p["norm_g"], False)], 1, width=D_MODEL, ncol=1, out_dtypes=[BF16], **kw)
    proj = matmul(u, p["wp"], "nn", name="proj", bm=lp, bn=640)
    o_a_raw, *rode = sb_fwd(proj, (0, N_HEADS, 2 * N_HEADS), ride)
    qkv = conv_fwd("gdn_conv_fwd", proj, OFF_GDN_QKV // HEAD_DIM, p["gdn_conv_w"], p["gdn_conv_b"], 2 * N_HEADS)
    first, second = rowmap_fwd("gates_fwd", _f_small_gates, [(proj, OFF_SMALL // HEAD_DIM)],
                               [(p["bias_vec"], False), (p["alog_vec"], False)], 2, width=HEAD_DIM, ncol=1, **kw)
    o_b_raw, gdn_states, gdn_t = gdn_fwd(qkv, first)
    xbc = conv_fwd("ssm_conv_fwd", proj, OFF_SSM_XBC // HEAD_DIM, p["ssm_conv_w"], p["ssm_conv_b"], 0)
    y_raw, ssd_states = ssd_fwd(xbc, first, second)
    (o_a,) = rowmap_fwd("gate_a_fwd", _f_gate_silu, [(o_a_raw, 0), (proj, OFF_SB_Z // 1024)], [], 1, width=1024, ncol=1,
                        out_dtypes=[BF16], **kw)
    (o_b,) = rowmap_fwd("gate_b_fwd", _f_head_norm_gate, [(o_b_raw, 0), (proj, OFF_GDN_Z // 1024)], [(p["gdn_norm_g"], False)], 1,
                        width=1024, ncol=1, out_dtypes=[BF16], **kw)
    (o_c,) = rowmap_fwd("gate_c_fwd", _f_ssm_out, [(y_raw, 0), (xbc, 0), (proj, OFF_SSM_Z // 1024)],
                        [(p["d_skip"], True), (p["ssm_norm_g"], True)], 1, width=1024, ncol=SSM_GROUPS, out_dtypes=[BF16], **kw)
    pa = matmul(o_a, p["wa"], "nn", name="branch_a", bm=lp // 2, bn=512)
    pb = matmul(o_b, p["wb"], "nn", name="branch_b", bm=lp // 2, bn=512)
    pc = matmul(o_c, p["wc"], "nn", name="branch_c", bm=lp // 2, bn=512)
    merge_rows = [(pa, 0), (pb, 0), (pc, 0)] + [(proj, OFF_GATES // 512 + 2 * i) for i in range(3)]
    (merged,) = rowmap_fwd("merge_fwd", _f_merge, merge_rows, [], 1, width=512, ncol=2, out_dtypes=[BF16], **kw)
    h_out = matmul(merged, p["wo"], "nn", name="out_proj", bm=lp, bn=512, residual=h)
    saved = dict(h=h, u=u, proj=proj, qkv=qkv, first=first, second=second, o_a_raw=o_a_raw, o_b_raw=o_b_raw,
                 gdn_states=gdn_states, gdn_t=gdn_t, xbc=xbc, y_raw=y_raw, ssd_states=ssd_states, o_a=o_a, o_b=o_b, o_c=o_c, pa=pa, pb=pb, pc=pc,
                 merged=merged)
    return h_out, saved, rode


def _layer_bwd(d_h, p, s, ride):
    lp = d_h.shape[0]
    bm = _pick(lp, 272, 8)
    kw = dict(bm=bm, lp=lp)
    proj = s["proj"]
    g = {}
    d_merged = matmul(d_h, p["wo"], "nt", name="d_merged", bm=lp, bn=512)
    g["w_out"] = matmul(s["merged"], d_h, "tn", name="g_w_out", bm=512, bn=1024, bk=lp)
    merge_rows = [(s["pa"], 0), (s["pb"], 0), (s["pc"], 0)] + [(proj, OFF_GATES // 512 + 2 * i) for i in range(3)]
    (d_pa, d_pb, d_pc, d_ga, d_gb, d_gc), _ = rowmap_bwd("merge_bwd", _f_merge, merge_rows, [], [d_merged], width=512, ncol=2,
                                                         d_row_dtypes=[BF16] * 6, **kw)
    g["w_branch_a"] = matmul(s["o_a"], d_pa, "tn", name="g_w_a", bm=512, bn=1024, bk=lp)
    g["w_branch_b"] = matmul(s["o_b"], d_pb, "tn", name="g_w_b", bm=512, bn=1024, bk=lp)
    g["w_branch_c"] = matmul(s["o_c"], d_pc, "tn", name="g_w_c", bm=512, bn=1024, bk=lp)
    d_oa = matmul(d_pa, p["wa"], "nt", name="d_o_a", bm=lp, bn=512)
    d_ob = matmul(d_pb, p["wb"], "nt", name="d_o_b", bm=lp, bn=512)
    d_oc = matmul(d_pc, p["wc"], "nt", name="d_o_c", bm=lp, bn=512)
    (d_oa_raw, d_sbz), _ = rowmap_bwd("gate_a_bwd", _f_gate_silu, [(s["o_a_raw"], 0), (proj, OFF_SB_Z // 1024)], [], [d_oa],
                                      width=1024, ncol=1, d_row_dtypes=[F32, BF16], **kw)
    (d_ob_raw, d_gdz), (g["gdn_norm_g"],) = rowmap_bwd(
        "gate_b_bwd", _f_head_norm_gate, [(s["o_b_raw"], 0), (proj, OFF_GDN_Z // 1024)], [(p["gdn_norm_g"], False)], [d_ob],
        width=1024, ncol=1, d_row_dtypes=[F32, BF16], **kw)
    (d_y, d_xh, d_ssz), (g_dskip, g["ssm_norm_g"]) = rowmap_bwd(
        "gate_c_bwd", _f_ssm_out, [(s["y_raw"], 0), (s["xbc"], 0), (proj, OFF_SSM_Z // 1024)],
        [(p["d_skip"], True), (p["ssm_norm_g"], True)], [d_oc], width=1024, ncol=SSM_GROUPS, d_row_dtypes=[F32, F32, BF16], **kw)
    g["gdn_norm_g"], g["ssm_norm_g"] = g["gdn_norm_g"][0], g["ssm_norm_g"][0]
    g["ssm_d"] = g_dskip.reshape(SSM_HEADS, SSM_P).sum(axis=1)
    d_q, d_k, d_v, *rode = sb_bwd(proj, (0, N_HEADS, 2 * N_HEADS), d_oa_raw, ride)
    d_qkv, d_first_gdn = gdn_bwd(s["qkv"], s["first"], s["gdn_states"], s["gdn_t"], d_ob_raw)
    d_xbc_out, d_first_ssd, d_second = ssd_bwd(s["xbc"], s["first"], s["second"], s["ssd_states"], d_y, d_xh)
    d_gdqkv, g["gdn_conv_w"], _ = conv_bwd("gdn_conv_bwd", proj, OFF_GDN_QKV // HEAD_DIM, p["gdn_conv_w"], p["gdn_conv_b"], 2 * N_HEADS,
                                           d_qkv)
    d_xbc, g["ssm_conv_w"], g_cb = conv_bwd("ssm_conv_bwd", proj, OFF_SSM_XBC // HEAD_DIM, p["ssm_conv_w"], p["ssm_conv_b"], 0, d_xbc_out)
    g["ssm_conv_b"] = g_cb[0]
    d_first = d_first_gdn + d_first_ssd
    (d_small,), (g_bias, g_alog) = rowmap_bwd("gates_bwd", _f_small_gates, [(proj, OFF_SMALL // HEAD_DIM)],
                                              [(p["bias_vec"], False), (p["alog_vec"], False)], [d_first, d_second],
                                              width=HEAD_DIM, ncol=1, d_row_dtypes=[BF16], **kw)
    g["gdn_dt_bias"], g["ssm_dt_bias"] = g_bias[0, 8:16], g_bias[0, 16:48]
    g["gdn_a_log"], g["ssm_a_log"] = g_alog[0, 8:16], g_alog[0, 16:48]
    d_proj = jnp.concatenate([d_q, d_k, d_v, d_sbz, d_gdqkv, d_gdz, d_ssz, d_xbc, d_ga, d_gb, d_gc, d_small], axis=1)
    g["w_in"] = matmul(s["u"], d_proj, "tn", name="g_w_in", bm=1024, bn=640, bk=lp, out_dtype=BF16)
    bufs = [_split_halves(buf) for buf in _layer_grad_buffers(g, D_IN // 4)]
    d_u, *theirs = matmul(d_proj, p["wp"], "nt", name="d_u", bm=lp // 2, bn=1024, bk=1600, ride=split_ride(bufs))
    parts = [pair_add(f"grads_pair_add_{i}", buf, t, BF16) for i, (buf, t) in enumerate(zip(bufs, theirs))]
    (d_hn,), (g_norm,) = rowmap_bwd("rms_bwd", _f_rmsnorm, [(s["h"], 0)], [(p["norm_g"], False)], [d_u], width=D_MODEL, ncol=1, **kw)
    g["norm_g"] = g_norm[0]
    return d_h + d_hn, g, rode, parts


def kernel(x, meta_tokens, norm_g, w_in, gdn_conv_w, gdn_a_log, gdn_dt_bias, gdn_norm_g, ssm_conv_w, ssm_conv_b, ssm_a_log, ssm_dt_bias, ssm_d, ssm_norm_g, w_branch_a, w_branch_b, w_branch_c, w_out, final_norm_g, loss_target, m_meta_tokens, m_norm_g, m_w_in, m_gdn_conv_w, m_gdn_a_log, m_gdn_dt_bias, m_gdn_norm_g, m_ssm_conv_w, m_ssm_conv_b, m_ssm_a_log, m_ssm_dt_bias, m_ssm_d, m_ssm_norm_g, m_w_branch_a, m_w_branch_b, m_w_branch_c, m_w_out, m_final_norm_g, v_meta_tokens, v_norm_g, v_w_in, v_gdn_conv_w, v_gdn_a_log, v_gdn_dt_bias, v_gdn_norm_g, v_ssm_conv_w, v_ssm_conv_b, v_ssm_a_log, v_ssm_dt_bias, v_ssm_d, v_ssm_norm_g, v_w_branch_a, v_w_branch_b, v_w_branch_c, v_w_out, v_final_norm_g):
    w = dict(meta_tokens=meta_tokens, norm_g=norm_g, w_in=w_in, gdn_conv_w=gdn_conv_w, gdn_a_log=gdn_a_log, gdn_dt_bias=gdn_dt_bias,
             gdn_norm_g=gdn_norm_g, ssm_conv_w=ssm_conv_w, ssm_conv_b=ssm_conv_b, ssm_a_log=ssm_a_log, ssm_dt_bias=ssm_dt_bias,
             ssm_d=ssm_d, ssm_norm_g=ssm_norm_g, w_branch_a=w_branch_a, w_branch_b=w_branch_b, w_branch_c=w_branch_c, w_out=w_out,
             final_norm_g=final_norm_g)
    m = dict(meta_tokens=m_meta_tokens, norm_g=m_norm_g, w_in=m_w_in, gdn_conv_w=m_gdn_conv_w, gdn_a_log=m_gdn_a_log,
             gdn_dt_bias=m_gdn_dt_bias, gdn_norm_g=m_gdn_norm_g, ssm_conv_w=m_ssm_conv_w, ssm_conv_b=m_ssm_conv_b,
             ssm_a_log=m_ssm_a_log, ssm_dt_bias=m_ssm_dt_bias, ssm_d=m_ssm_d, ssm_norm_g=m_ssm_norm_g, w_branch_a=m_w_branch_a,
             w_branch_b=m_w_branch_b, w_branch_c=m_w_branch_c, w_out=m_w_out, final_norm_g=m_final_norm_g)
    v = dict(meta_tokens=v_meta_tokens, norm_g=v_norm_g, w_in=v_w_in, gdn_conv_w=v_gdn_conv_w, gdn_a_log=v_gdn_a_log,
             gdn_dt_bias=v_gdn_dt_bias, gdn_norm_g=v_gdn_norm_g, ssm_conv_w=v_ssm_conv_w, ssm_conv_b=v_ssm_conv_b,
             ssm_a_log=v_ssm_a_log, ssm_dt_bias=v_ssm_dt_bias, ssm_d=v_ssm_d, ssm_norm_g=v_ssm_norm_g, w_branch_a=v_w_branch_a,
             w_branch_b=v_w_branch_b, w_branch_c=v_w_branch_c, w_out=v_w_out, final_norm_g=v_final_norm_g)
    depth = norm_g.shape[0]
    placed, placed_small = _place_weights(w)
    got_a, got_b, got_s = gather_shards("gather_first", placed[0] + [placed_small])
    exact = _exact_weights(w, got_s)

    h = jnp.concatenate([jnp.zeros((PAD, D_MODEL), F32), exact["meta_tokens"], x[0]], axis=0)
    params, saved = [], []
    for l in range(depth):
        params.append(_layer_params(w, exact, _layer_weights(got_a, got_b), l))
        h, s, rode = _layer_fwd(h, params[l], gather_ride(placed[l + 1]) if l + 1 < depth else None)
        saved.append(s)
        if rode:
            got_a, got_b = rode
    loss, d_h, g_final = loss_head(h, loss_target[0], final_norm_g[None])

    layer_grads, reds, waiting = [None] * depth, [None] * depth, None
    for l in reversed(range(depth)):
        d_h, layer_grads[l], rode, parts = _layer_bwd(d_h, params[l], saved[l], exchange_ride(waiting) if waiting else None)
        if waiting:
            reds[l + 1] = [chip_add(f"grads_chip_add_{i}", gt, p) for i, (gt, p) in enumerate(zip(rode, waiting))]
        waiting = parts
    grads = {n: jnp.stack([layer_grads[l][n] for l in range(depth)]) for n in WEIGHTS
             if n not in ("meta_tokens", "final_norm_g", "w_in") + BRANCH}
    grads["meta_tokens"] = d_h[PAD:PAD + N_META]
    grads["final_norm_g"] = g_final[0]
    grad_x = d_h[PAD + N_META:][None]
    buf_s = jnp.stack([_pack([_shard(grads[n], SHARD_AXIS[n], s).astype(BF16).reshape(-1) for n in EXACT], row_unit=32) for s in range(4)])
    small = _pack([grads[n].reshape(-1) for n in REPLICATED], row_unit=32)
    last = waiting + _start_reduce("small_grads", [buf_s, jnp.broadcast_to(small[None], (4,) + small.shape)], [BF16, F32])
    got = chip_exchange("grads_chip_exchange", last)
    sums = [chip_add(f"grads_chip_add_{i}", gt, p) for i, (gt, p) in enumerate(zip(got, last))]
    reds[0] = sums[:2]
    joined = pair_join("grads_pair_join", [r for layer in reds for r in layer] + sums[2:])
    red = {"w_in": jnp.stack(joined[0:2 * depth:2])}
    pos = 0
    for n, rows in zip(BRANCH, BRANCH_ROWS):
        red[n] = jnp.stack([joined[2 * l + 1][pos:pos + rows] for l in range(depth)])
        pos += rows
    red.update(zip(EXACT, _unpack(joined[-2], [w[n].shape for n in EXACT])))
    small_red = joined[-1]
    delta, new_m, new_v = {}, {}, {}
    for n in SHARDED:
        if w[n].shape[-1] % HEAD_DIM:
            to_view, from_view = (lambda a: jnp.transpose(a, (2, 0, 1))), (lambda a: jnp.transpose(a, (1, 2, 0)))
            delta[n], new_m[n], new_v[n], red[n] = [from_view(a) for a in adamw("adamw_" + n, to_view(w[n]), to_view(red[n]),
                                                                                to_view(m[n]), to_view(v[n]), echo_g=True)]
        else:
            delta[n], new_m[n], new_v[n] = adamw("adamw_" + n, w[n], red[n], m[n], v[n])
    pack_small = lambda d: _pack([d[n].reshape(-1) for n in REPLICATED], row_unit=32)
    small = adamw("adamw_small", pack_small(w), small_red, pack_small(m), pack_small(v))
    shapes = [w[n].shape for n in REPLICATED]
    red.update(zip(REPLICATED, _unpack(small_red, shapes)))
    for d, buf in zip((delta, new_m, new_v), small):
        d.update(zip(REPLICATED, _unpack(buf, shapes)))
    total_loss = lax.psum(loss[0, 0], ("x", "y", "c"))
    return (total_loss, grad_x, *[red[n] for n in WEIGHTS], *[delta[n] for n in WEIGHTS], *[new_m[n] for n in WEIGHTS],
            *[new_v[n] for n in WEIGHTS])
```

```python
import functools
import math

import jax
import jax.numpy as jnp
from jax import lax
from jax.experimental import pallas as pl
from jax.experimental.pallas import tpu as pltpu

F32 = jnp.float32
BF16 = jnp.bfloat16

N_META = 16
RMS_EPS = 1e-6
L2_EPS = 1e-6
CONV_K = 4
D_MODEL = 1024
HEAD_DIM = 128
N_HEADS = 8
CHUNK = 64
SB_BLOCK = 128
PAD = SB_BLOCK - N_META
SSM_INNER = 2048
SSM_P = 64
SSM_HEADS = 32
SSM_GROUPS = 2
SSM_HG = SSM_HEADS // SSM_GROUPS
SSM_N = 128
VMEM_LIMIT = 56 * 1024 * 1024

def _dims(mode, ndim):
    lhs, rhs = {"nn": (1, 0), "nt": (1, 1), "tn": (0, 0)}[mode]
    off = ndim - 2
    return (((lhs + off,), (rhs + off,)), (tuple(range(off)), tuple(range(off))))


def _dot(a, b, mode):
    return lax.dot_general(a, b, _dims(mode, a.ndim), preferred_element_type=F32)


def _halves(a):
    hi = a.astype(BF16)
    return hi, (a - hi.astype(F32)).astype(BF16)


def _mm_raw(a, b, mode, kind):
    if kind == "bf16":
        return _dot(a.astype(BF16), b.astype(BF16), mode)
    if kind == "lhs01":
        hi, lo = _halves(b)
        a = a.astype(BF16)
        return _dot(a, hi, mode) + _dot(a, lo, mode)
    if kind == "rhs01":
        hi, lo = _halves(a)
        b = b.astype(BF16)
        return _dot(hi, b, mode) + _dot(lo, b, mode)
    a_hi, a_lo = _halves(a)
    b_hi, b_lo = _halves(b)
    return _dot(a_hi, b_hi, mode) + (_dot(a_hi, b_lo, mode) + _dot(a_lo, b_hi, mode))


@functools.partial(jax.custom_vjp, nondiff_argnums=(2, 3))
def _mm(a, b, mode="nn", kind="bf16"):
    return _mm_raw(a, b, mode, kind)


def _mm_fwd(a, b, mode, kind):
    return _mm_raw(a, b, mode, kind), (a, b)


def _mm_bwd(mode, kind, res, g):
    a, b = res
    if kind == "lhs01":
        return jnp.zeros_like(a), _mm_raw(a, g, {"nn": "tn", "tn": "nn"}[mode], "lhs01")
    if kind == "rhs01":
        return _mm_raw(g, b, {"nn": "nt", "nt": "nn"}[mode], "rhs01"), jnp.zeros_like(b)
    if mode == "nn":
        return _mm_raw(g, b, "nt", kind), _mm_raw(a, g, "tn", kind)
    if mode == "nt":
        return _mm_raw(g, b, "nn", kind), _mm_raw(g, a, "tn", kind)
    return _mm_raw(b, g, "nt", kind), _mm_raw(a, g, "nn", kind)


_mm.defvjp(_mm_fwd, _mm_bwd)


def _iota2(shape, axis):
    return lax.broadcasted_iota(jnp.int32, shape, axis)


def _inv_unit_lower_raw(m):
    size = m.shape[-1]
    eye = (_iota2((size, size), 0) == _iota2((size, size), 1)).astype(F32)
    n = -m
    t = eye + n
    p = n
    steps = int(math.log2(size)) - 1
    for _ in range(steps):
        p = _mm_raw(p, p, "nn", "x3")
        t = t + _mm_raw(t, p, "nn", "x3")
    return t


@jax.custom_vjp
def _inv_unit_lower(m):
    return _inv_unit_lower_raw(m)


def _inv_fwd(m):
    t = _inv_unit_lower_raw(m)
    return t, t


def _inv_bwd(t, g):
    return (-_mm_raw(_mm_raw(t, g, "tn", "x3"), t, "nt", "x3"),)


_inv_unit_lower.defvjp(_inv_fwd, _inv_bwd)


@jax.custom_vjp
def _inv_known(m, t):
    return t


_inv_known.defvjp(lambda m, t: (t, t), lambda t, g: (_inv_bwd(t, g)[0], jnp.zeros_like(t)))


def _safe_decay(col, row, keep):
    return jnp.where(keep, jnp.exp(jnp.where(keep, col - row, 0.0)), 0.0)


def _col_to_row(col):
    n = col.shape[-2]
    eye = _iota2((n, n), 0) == _iota2((n, n), 1)
    return jnp.sum(jnp.where(eye, col, 0.0), axis=-2, keepdims=True)


def _cumsum_col(col):
    n = col.shape[-2]
    li, si = _iota2((n, n), 0), _iota2((n, n), 1)
    row = _col_to_row(col)
    c_col = jnp.sum(jnp.where(li >= si, row, 0.0), axis=-1, keepdims=True)
    c_row = jnp.sum(jnp.where(li <= si, col, 0.0), axis=-2, keepdims=True)
    return c_col, c_row


def _gdn_chunk(q, k, v, g, beta, state, t_known=None):
    cl = q.shape[-2]
    li, si = _iota2((cl, cl), 0), _iota2((cl, cl), 1)
    gc_col, gc_row = _cumsum_col(g)
    g_last = jnp.sum(g, axis=-2, keepdims=True)
    dec_strict = _safe_decay(gc_col, gc_row, li > si)
    dec_incl = _safe_decay(gc_col, gc_row, li >= si)
    e_gc = jnp.exp(gc_col)
    qs = q * (HEAD_DIM ** -0.5)
    kb = k * beta
    m = _mm(kb, k, "nt") * dec_strict
    t_inv = _inv_unit_lower(m) if t_known is None else _inv_known(m, t_known)
    u = _mm(t_inv, v * beta)
    w = _mm(t_inv, kb * e_gc)
    a_qk = _mm(qs, k, "nt") * dec_incl
    q_dec = qs * e_gc
    k_end = k * jnp.exp(g_last - gc_col)
    v_new = u - _mm(w, state)
    o = _mm(q_dec, state) + _mm(a_qk, v_new)
    new_state = state * jnp.exp(g_last) + _mm(k_end, v_new, "tn")
    return o, new_state, t_inv


def _gdn_operands(qkv_ref, gt):
    nh, width = N_HEADS, N_HEADS * HEAD_DIM
    heads = lambda off: jnp.stack([qkv_ref[:, off + h * HEAD_DIM:off + (h + 1) * HEAD_DIM] for h in range(nh)])
    cols = lambda off: jnp.stack([gt[:, off + h:off + h + 1] for h in range(nh)])
    return heads(0), heads(width), heads(2 * width), cols(nh), cols(0)


def gdn_fwd(qkv, gates):
    lp = qkv.shape[0]
    nh = N_HEADS
    nc = lp // CHUNK
    width = nh * HEAD_DIM

    def body(qkv_ref, gt_ref, o_ref, s_ref, t_ref, state):
        @pl.when(pl.program_id(0) == 0)
        def _():
            state[...] = jnp.zeros_like(state)

        s_in = state[...]
        s_ref[0] = s_in
        o, s_new, t_inv = _gdn_chunk(*_gdn_operands(qkv_ref, gt_ref[...]), s_in)
        t_ref[0] = t_inv
        for h in range(nh):
            o_ref[:, h * HEAD_DIM:(h + 1) * HEAD_DIM] = o[h]
        state[...] = s_new

    return pl.pallas_call(
        body, name="gdn_fwd", grid=(nc,),
        in_specs=[pl.BlockSpec((CHUNK, 3 * width), lambda c: (c, 0)), pl.BlockSpec((CHUNK, HEAD_DIM), lambda c: (c, 0))],
        out_specs=[pl.BlockSpec((CHUNK, width), lambda c: (c, 0)), pl.BlockSpec((1, nh, HEAD_DIM, HEAD_DIM), lambda c: (c, 0, 0, 0)),
                   pl.BlockSpec((1, nh, CHUNK, CHUNK), lambda c: (c, 0, 0, 0))],
        out_shape=[jax.ShapeDtypeStruct((lp, width), F32), jax.ShapeDtypeStruct((nc, nh, HEAD_DIM, HEAD_DIM), F32),
                   jax.ShapeDtypeStruct((nc, nh, CHUNK, CHUNK), F32)],
        scratch_shapes=[pltpu.VMEM((nh, HEAD_DIM, HEAD_DIM), F32)],
        compiler_params=pltpu.CompilerParams(dimension_semantics=("arbitrary",), vmem_limit_bytes=VMEM_LIMIT),
    )(qkv, gates)


def gdn_bwd(qkv, gates, states, t_invs, d_o):
    lp = qkv.shape[0]
    nh = N_HEADS
    nc = lp // CHUNK
    width = nh * HEAD_DIM

    def body(qkv_ref, gt_ref, s_ref, t_ref, do_ref, dqkv_ref, dgt_ref, d_state):
        @pl.when(pl.program_id(0) == 0)
        def _():
            d_state[...] = jnp.zeros_like(d_state)

        t_known = t_ref[0]
        _, pull = jax.vjp(lambda *xs: _gdn_chunk(*xs, t_known=t_known)[:2], *_gdn_operands(qkv_ref, gt_ref[...]), s_ref[0])
        d_o = jnp.stack([do_ref[:, h * HEAD_DIM:(h + 1) * HEAD_DIM] for h in range(nh)])
        dq, dk, dv, dg, db, ds = pull((d_o, d_state[...]))
        lane = _iota2((CHUNK, HEAD_DIM), 1)
        d_gt = jnp.zeros((CHUNK, HEAD_DIM), F32)
        for h in range(nh):
            for part, val in enumerate((dq, dk, dv)):
                dqkv_ref[:, part * width + h * HEAD_DIM:part * width + (h + 1) * HEAD_DIM] = val[h]
            d_gt = d_gt + jnp.where(lane == h, db[h], 0.0) + jnp.where(lane == nh + h, dg[h], 0.0)
        dgt_ref[...] = d_gt
        d_state[...] = ds

    rev = lambda c: (nc - 1 - c, 0)
    return pl.pallas_call(
        body, name="gdn_bwd", grid=(nc,),
        in_specs=[pl.BlockSpec((CHUNK, 3 * width), rev), pl.BlockSpec((CHUNK, HEAD_DIM), rev),
                  pl.BlockSpec((1, nh, HEAD_DIM, HEAD_DIM), lambda c: (nc - 1 - c, 0, 0, 0)),
                  pl.BlockSpec((1, nh, CHUNK, CHUNK), lambda c: (nc - 1 - c, 0, 0, 0)), pl.BlockSpec((CHUNK, width), rev)],
        out_specs=[pl.BlockSpec((CHUNK, 3 * width), rev), pl.BlockSpec((CHUNK, HEAD_DIM), rev)],
        out_shape=[jax.ShapeDtypeStruct((lp, 3 * width), F32), jax.ShapeDtypeStruct((lp, HEAD_DIM), F32)],
        scratch_shapes=[pltpu.VMEM((nh, HEAD_DIM, HEAD_DIM), F32)],
        compiler_params=pltpu.CompilerParams(dimension_semantics=("arbitrary",), vmem_limit_bytes=VMEM_LIMIT),
    )(qkv, gates, states, t_invs, d_o)


def _head_expand():
    width = SSM_HG * SSM_P
    return (_iota2((SSM_HG, width), 1) // SSM_P == _iota2((SSM_HG, width), 0)).astype(F32)


def _ssd_chunk(x, b, c, dt, la, state):
    cl = x.shape[0]
    li, si = _iota2((cl, cl), 0), _iota2((cl, cl), 1)
    causal = li >= si
    expand = _head_expand()
    tri = causal.astype(F32)
    xs = x * _mm(dt, expand, "nn", "rhs01")
    la_x = _mm(la, expand, "nn", "rhs01")
    cs_x = _mm(tri, la_x, "nn", "lhs01")
    last_x = jnp.sum(la_x, axis=0, keepdims=True)
    cs = _mm(tri, la, "nn", "lhs01")
    scores = _mm(c, b, "nt")
    head_id = _iota2((1, SSM_HG), 1)
    per_tile = HEAD_DIM // SSM_P
    tile_head = _iota2((1, HEAD_DIM), 1) // SSM_P
    within = []
    for t in range(SSM_HG // per_tile):
        xs_t = xs[:, t * HEAD_DIM:(t + 1) * HEAD_DIM]
        acc = jnp.zeros((cl, HEAD_DIM), F32)
        for hh in range(per_tile):
            cs_col = jnp.sum(jnp.where(head_id == t * per_tile + hh, cs, 0.0), axis=1, keepdims=True)
            decay = _safe_decay(cs_col, _col_to_row(cs_col), causal)
            acc = acc + _mm(scores * decay, jnp.where(tile_head == hh, xs_t, 0.0))
        within.append(acc)
    y = _mm(c, state) * jnp.exp(cs_x) + jnp.concatenate(within, axis=1)
    new_state = state * jnp.exp(last_x) + _mm(b, xs * jnp.exp(last_x - cs_x), "tn")
    return y, new_state


GATE_DT = 16


def _place_lanes(v, lo):
    n = v.shape[1]
    sel = (_iota2((n, HEAD_DIM), 1) == _iota2((n, HEAD_DIM), 0) + lo).astype(F32)
    return _mm_raw(v, sel, "nn", "rhs01")


def ssd_fwd(xbc, first, second):
    lp = xbc.shape[0]
    nc = lp // CHUNK
    width = SSM_HG * SSM_P
    b_off, c_off = SSM_INNER, SSM_INNER + SSM_GROUPS * SSM_N

    def body(x_ref, f_ref, s2_ref, y_ref, s_ref, state):
        @pl.when(pl.program_id(0) == 0)
        def _():
            state[...] = jnp.zeros_like(state)

        f, s2 = f_ref[...], s2_ref[...]
        for g in range(SSM_GROUPS):
            lo = GATE_DT + g * SSM_HG
            s_in = state[g]
            s_ref[0, g] = s_in
            y, s_new = _ssd_chunk(x_ref[:, g * width:(g + 1) * width], x_ref[:, b_off + g * SSM_N:b_off + (g + 1) * SSM_N],
                                  x_ref[:, c_off + g * SSM_N:c_off + (g + 1) * SSM_N], f[:, lo:lo + SSM_HG], s2[:, lo:lo + SSM_HG], s_in)
            y_ref[:, g * width:(g + 1) * width] = y
            state[g] = s_new

    row = lambda cols: pl.BlockSpec((CHUNK, cols), lambda k: (k, 0))
    return pl.pallas_call(
        body, name="ssd_fwd", grid=(nc,),
        in_specs=[row(xbc.shape[1]), row(HEAD_DIM), row(HEAD_DIM)],
        out_specs=[row(SSM_INNER), pl.BlockSpec((1, SSM_GROUPS, SSM_N, width), lambda k: (k, 0, 0, 0))],
        out_shape=[jax.ShapeDtypeStruct((lp, SSM_INNER), F32), jax.ShapeDtypeStruct((nc, SSM_GROUPS, SSM_N, width), F32)],
        scratch_shapes=[pltpu.VMEM((SSM_GROUPS, SSM_N, width), F32)],
        compiler_params=pltpu.CompilerParams(dimension_semantics=("arbitrary",), vmem_limit_bytes=VMEM_LIMIT),
    )(xbc, first, second)


def ssd_bwd(xbc, first, second, states, d_y, d_xh):
    lp = xbc.shape[0]
    nc = lp // CHUNK
    width = SSM_HG * SSM_P
    b_off, c_off = SSM_INNER, SSM_INNER + SSM_GROUPS * SSM_N

    def body(x_ref, f_ref, s2_ref, s_ref, dy_ref, dxh_ref, dx_ref, df_ref, ds2_ref, d_state):
        @pl.when(pl.program_id(0) == 0)
        def _():
            d_state[...] = jnp.zeros_like(d_state)

        f, s2 = f_ref[...], s2_ref[...]
        d_f = jnp.zeros((CHUNK, HEAD_DIM), F32)
        d_s2 = jnp.zeros((CHUNK, HEAD_DIM), F32)
        for g in range(SSM_GROUPS):
            lo = GATE_DT + g * SSM_HG
            x_l = slice(g * width, (g + 1) * width)
            b_l = slice(b_off + g * SSM_N, b_off + (g + 1) * SSM_N)
            c_l = slice(c_off + g * SSM_N, c_off + (g + 1) * SSM_N)
            _, pull = jax.vjp(_ssd_chunk, x_ref[:, x_l], x_ref[:, b_l], x_ref[:, c_l], f[:, lo:lo + SSM_HG], s2[:, lo:lo + SSM_HG],
                              s_ref[0, g])
            dx, db, dc, ddt, dla, ds = pull((dy_ref[:, x_l], d_state[g]))
            dx_ref[:, x_l] = dx + dxh_ref[:, x_l]
            dx_ref[:, b_l] = db
            dx_ref[:, c_l] = dc
            d_f = d_f + _place_lanes(ddt, lo)
            d_s2 = d_s2 + _place_lanes(dla, lo)
            d_state[g] = ds
        df_ref[...] = d_f
        ds2_ref[...] = d_s2

    row = lambda cols: pl.BlockSpec((CHUNK, cols), lambda k: (nc - 1 - k, 0))
    gate_shape = jax.ShapeDtypeStruct((lp, HEAD_DIM), F32)
    return pl.pallas_call(
        body, name="ssd_bwd", grid=(nc,),
        in_specs=[row(xbc.shape[1]), row(HEAD_DIM), row(HEAD_DIM),
                  pl.BlockSpec((1, SSM_GROUPS, SSM_N, width), lambda k: (nc - 1 - k, 0, 0, 0)), row(SSM_INNER), row(SSM_INNER)],
        out_specs=[row(xbc.shape[1]), row(HEAD_DIM), row(HEAD_DIM)],
        out_shape=[jax.ShapeDtypeStruct(xbc.shape, F32), gate_shape, gate_shape],
        scratch_shapes=[pltpu.VMEM((SSM_GROUPS, SSM_N, width), F32)],
        compiler_params=pltpu.CompilerParams(dimension_semantics=("arbitrary",), vmem_limit_bytes=VMEM_LIMIT),
    )(xbc, first, second, states, d_y, d_xh)


SB_QROWS = 544


def _mm_tri(a, tri):
    return _mm_raw(a, tri.astype(BF16), "nn", "rhs01")


def _sb_scores(q_scaled, kb, row0, j):
    shape = (q_scaled.shape[0], SB_BLOCK)
    z = _mm_raw(q_scaled, kb, "nt", "bf16")
    q_pos = row0 + _iota2(shape, 0)
    k_pos = j * SB_BLOCK + _iota2(shape, 1)
    valid = (k_pos < q_pos) & (k_pos >= PAD)
    sp = jnp.maximum(z, 0.0) + jnp.log(1.0 + jnp.exp(-jnp.abs(z)))
    lk = jnp.where(valid, -sp, 0.0)
    return z, sp, valid, lk


SB_DEAD = -110.0


def sb_fwd(src, offs, ride=None):
    lp = src.shape[0]
    nh = N_HEADS
    qb = _pick(lp, SB_QROWS, 8)
    scale = HEAD_DIM ** -0.5
    blk = SB_BLOCK

    n_in = len(ride.ins) if ride else 0
    n_out = len(ride.out_shapes) if ride else 0

    def body(*refs):
        q_ref, k_ref, v_ref = refs[:3]
        o_ref = refs[3 + n_in]
        ride_refs = (refs[3:3 + n_in], refs[4 + n_in:4 + n_in + n_out], refs[4 + n_in + n_out:])
        i = pl.program_id(1)
        if ride:
            @pl.when((pl.program_id(0) == 0) & (i == 0))
            def _():
                ride.start(*ride_refs)
        q_scaled = q_ref[...] * scale
        upper = _iota2((blk, blk), 0) > _iota2((blk, blk), 1)
        n_blocks = ((i + 1) * qb + blk - 1) // blk

        def live(state):
            it, _, c = state
            return (it < n_blocks) & (jnp.max(c) > SB_DEAD)

        def step(state):
            it, acc, c = state
            j = n_blocks - 1 - it
            rows = pl.ds(pl.multiple_of(j * blk, blk), blk)
            z, sp, valid, lk = _sb_scores(q_scaled, k_ref[rows, :], i * qb, j)
            later = _mm_tri(lk, upper) + c
            w = jnp.where(valid, jnp.exp(z - sp + later), 0.0)
            acc = acc + _mm_raw(w, v_ref[rows, :], "nn", "bf16")
            return it + 1, acc, c + jnp.sum(lk, axis=1, keepdims=True)

        _, acc, _ = lax.while_loop(live, step, (jnp.int32(0), jnp.zeros((qb, HEAD_DIM), F32), jnp.zeros((qb, 1), F32)))
        o_ref[...] = acc
        if ride:
            @pl.when((pl.program_id(0) == nh - 1) & (i == lp // qb - 1))
            def _():
                ride.finish(*ride_refs)

    qspec = pl.BlockSpec((qb, HEAD_DIM), lambda h, i: (i, offs[0] + h))
    kspec = pl.BlockSpec((lp, HEAD_DIM), lambda h, i: (0, offs[1] + h))
    vspec = pl.BlockSpec((lp, HEAD_DIM), lambda h, i: (0, offs[2] + h))
    ospec = pl.BlockSpec((qb, HEAD_DIM), lambda h, i: (i, h))
    return pl.pallas_call(
        body, name="sb_fwd", grid=(nh, lp // qb), in_specs=[qspec, kspec, vspec] + [ANY] * n_in, out_specs=[ospec] + [ANY] * n_out,
        out_shape=[jax.ShapeDtypeStruct((lp, nh * HEAD_DIM), F32)] + (ride.out_shapes if ride else []),
        scratch_shapes=ride.sems if ride else [],
        input_output_aliases={3 + k: 1 + k for k in range(n_in)} if ride and ride.alias else {},
        compiler_params=pltpu.CompilerParams(dimension_semantics=("arbitrary", "arbitrary"), vmem_limit_bytes=VMEM_LIMIT),
    )(src, src, src, *(ride.ins if ride else []))


def sb_bwd(src, offs, d_o, ride=None):
    lp = src.shape[0]
    nh = N_HEADS
    qb = _pick(lp, SB_QROWS, 8)
    scale = HEAD_DIM ** -0.5
    blk = SB_BLOCK

    n_in = len(ride.ins) if ride else 0
    n_out = len(ride.out_shapes) if ride else 0

    def body(*refs):
        q_ref, k_ref, v_ref, do_ref = refs[:4]
        dq_ref, dk_out, dv_out = refs[4 + n_in:7 + n_in]
        ride_refs = (refs[4:4 + n_in], refs[7 + n_in:7 + n_in + n_out], refs[9 + n_in + n_out:])
        dk_ref, dv_ref = refs[7 + n_in + n_out:9 + n_in + n_out]
        i = pl.program_id(1)
        if ride:
            @pl.when((pl.program_id(0) == 0) & (i == 0))
            def _():
                ride.start(*ride_refs)

        @pl.when(i == 0)
        def _():
            dk_ref[...] = jnp.zeros_like(dk_ref)
            dv_ref[...] = jnp.zeros_like(dv_ref)

        q_scaled = q_ref[...] * scale
        d_out = do_ref[...]
        lower_incl = _iota2((blk, blk), 0) <= _iota2((blk, blk), 1)
        lower = _iota2((blk, blk), 0) < _iota2((blk, blk), 1)
        n_blocks = ((i + 1) * qb + blk - 1) // blk

        def live(state):
            it, c = state
            return (it < n_blocks) & (jnp.max(c) > SB_DEAD)

        def count(state):
            it, c = state
            rows = pl.ds(pl.multiple_of((n_blocks - 1 - it) * blk, blk), blk)
            _, _, _, lk = _sb_scores(q_scaled, k_ref[rows, :], i * qb, n_blocks - 1 - it)
            return it + 1, c + jnp.sum(lk, axis=1, keepdims=True)

        n_live, total = lax.while_loop(live, count, (jnp.int32(0), jnp.zeros((qb, 1), F32)))

        def step(j, carry):
            acc, cp, ep = carry
            rows = pl.ds(pl.multiple_of(j * blk, blk), blk)
            kb = k_ref[rows, :]
            vb = v_ref[rows, :]
            z, sp, valid, lk = _sb_scores(q_scaled, kb, i * qb, j)
            later = total - cp - _mm_tri(lk, lower_incl)
            w = jnp.where(valid, jnp.exp(z - sp + later), 0.0)
            e = w * _mm_raw(d_out, vb, "nt", "bf16")
            before = ep + _mm_tri(e, lower)
            dz = jnp.where(valid, e * jnp.exp(-sp) - before * jnp.exp(z - sp), 0.0)
            dk_ref[rows, :] += _mm_raw(dz, q_scaled, "tn", "bf16")
            dv_ref[rows, :] += _mm_raw(w, d_out, "tn", "bf16")
            acc = acc + _mm_raw(dz, kb, "nn", "bf16")
            return acc, cp + jnp.sum(lk, axis=1, keepdims=True), ep + jnp.sum(e, axis=1, keepdims=True)

        zero_col = jnp.zeros((qb, 1), F32)
        acc, _, _ = lax.fori_loop(n_blocks - n_live, n_blocks, step, (jnp.zeros((qb, HEAD_DIM), F32), zero_col, zero_col))
        dq_ref[...] = (acc * scale).astype(dq_ref.dtype)

        @pl.when(i == lp // qb - 1)
        def _():
            dk_out[...] = dk_ref[...].astype(dk_out.dtype)
            dv_out[...] = dv_ref[...].astype(dv_out.dtype)

        if ride:
            @pl.when((pl.program_id(0) == nh - 1) & (i == lp // qb - 1))
            def _():
                ride.finish(*ride_refs)

    qspec = pl.BlockSpec((qb, HEAD_DIM), lambda h, i: (i, offs[0] + h))
    kspec = pl.BlockSpec((lp, HEAD_DIM), lambda h, i: (0, offs[1] + h))
    vspec = pl.BlockSpec((lp, HEAD_DIM), lambda h, i: (0, offs[2] + h))
    ospec = pl.BlockSpec((qb, HEAD_DIM), lambda h, i: (i, h))
    fullspec = pl.BlockSpec((lp, HEAD_DIM), lambda h, i: (0, h))
    return pl.pallas_call(
        body, name="sb_bwd", grid=(nh, lp // qb), in_specs=[qspec, kspec, vspec, ospec] + [ANY] * n_in,
        out_specs=[ospec, fullspec, fullspec] + [ANY] * n_out,
        out_shape=[jax.ShapeDtypeStruct((lp, nh * HEAD_DIM), BF16)] * 3 + (ride.out_shapes if ride else []),
        scratch_shapes=[pltpu.VMEM((lp, HEAD_DIM), F32)] * 2 + (ride.sems if ride else []),
        input_output_aliases={4 + k: 3 + k for k in range(n_in)} if ride and ride.alias else {},
        compiler_params=pltpu.CompilerParams(dimension_semantics=("arbitrary", "arbitrary"), vmem_limit_bytes=VMEM_LIMIT),
    )(src, src, src, d_o, *(ride.ins if ride else []))


def _pick(n, target, unit):
    if n <= target:
        return n
    best = None
    for d in range(unit, target + 1, unit):
        if n % d == 0:
            best = d
    assert best is not None, (n, target, unit)
    return best


def matmul(a, b, mode="nn", *, name, bm=1088, bn=640, bk=2176, residual=None, out_dtype=F32, b_koff=0, ride=None):
    if mode == "nn":
        (m, k), n = a.shape, b.shape[1]
    elif mode == "nt":
        (m, k), n = a.shape, b.shape[0]
    else:
        (k, m), n = a.shape, b.shape[1]
    assert b_koff == 0 or mode == "nt"
    bm = _pick(m, bm, 128 if mode == "tn" else 8)
    bn = _pick(n, bn, 128 if mode != "nt" else 8)
    bk = _pick(k, bk, 128 if mode != "tn" else 8)
    nk = k // bk
    n_plain = 2 if residual is None else 3
    n_in = len(ride.ins) if ride else 0
    n_out = len(ride.out_shapes) if ride else 0
    steps = (m // bm, n // bn, nk)

    def body(*refs):
        a_ref, b_ref = refs[:2]
        r_ref = None if residual is None else refs[2]
        o_ref = refs[n_plain + n_in]
        acc = refs[n_plain + n_in + 1 + n_out]
        ride_refs = (refs[n_plain:n_plain + n_in], refs[n_plain + n_in + 1:n_plain + n_in + 1 + n_out], refs[n_plain + n_in + 2 + n_out:])
        kk = pl.program_id(2)
        here = [pl.program_id(d) for d in range(3)]
        if ride:
            @pl.when((here[0] == 0) & (here[1] == 0) & (here[2] == 0))
            def _():
                ride.start(*ride_refs)
        part = _mm_raw(a_ref[...], b_ref[...], mode, "bf16")

        @pl.when(kk == 0)
        def _():
            acc[...] = part

        @pl.when(kk > 0)
        def _():
            acc[...] += part

        @pl.when(kk == nk - 1)
        def _():
            res = acc[...]
            if r_ref is not None:
                res = res + r_ref[...]
            o_ref[...] = res.astype(out_dtype)

        if ride:
            @pl.when((here[0] == steps[0] - 1) & (here[1] == steps[1] - 1) & (here[2] == steps[2] - 1))
            def _():
                ride.finish(*ride_refs)

    a_spec = pl.BlockSpec((bk, bm), lambda i, j, kk: (kk, i)) if mode == "tn" else pl.BlockSpec((bm, bk), lambda i, j, kk: (i, kk))
    b_spec = pl.BlockSpec((bn, bk), lambda i, j, kk: (j, b_koff + kk)) if mode == "nt" else pl.BlockSpec((bk, bn), lambda i, j, kk: (kk, j))
    o_spec = pl.BlockSpec((bm, bn), lambda i, j, kk: (i, j))
    ins, specs = [a, b], [a_spec, b_spec]
    if residual is not None:
        ins.append(residual)
        specs.append(o_spec)
    res = pl.pallas_call(
        body, name=name, grid=steps, in_specs=specs + [ANY] * n_in, out_specs=[o_spec] + [ANY] * n_out,
        out_shape=[jax.ShapeDtypeStruct((m, n), out_dtype)] + (ride.out_shapes if ride else []),
        scratch_shapes=[pltpu.VMEM((bm, bn), F32)] + (ride.sems if ride else []),
        compiler_params=pltpu.CompilerParams(dimension_semantics=("arbitrary", "arbitrary", "arbitrary"), vmem_limit_bytes=VMEM_LIMIT),
    )(*ins, *(ride.ins if ride else []))
    return res if ride else res[0]


def _row_specs(rows, params, width, bm):
    row_specs = [pl.BlockSpec((bm, width), (lambda j, i, off=off: (i, off + j))) for _, off in rows]
    par_specs = [pl.BlockSpec((p.shape[0], width) if per_col else p.shape, ((lambda j, i: (0, j)) if per_col else (lambda j, i: (0, 0))))
                 for p, per_col in params]
    return row_specs, par_specs


def rowmap_fwd(name, fn, rows, params, n_out, *, width, ncol, bm, lp, out_dtypes=None):
    out_dtypes = out_dtypes or [F32] * n_out
    row_specs, par_specs = _row_specs(rows, params, width, bm)
    nr = len(rows)

    def body(*refs):
        ins, outs = refs[:nr + len(params)], refs[nr + len(params):]
        row_ids = pl.program_id(1) * bm + _iota2((bm, 1), 0)
        res = fn(row_ids, *[r[...].astype(F32) for r in ins])
        for o_ref, val in zip(outs, res):
            o_ref[...] = val.astype(o_ref.dtype)

    o_spec = pl.BlockSpec((bm, width), lambda j, i: (i, j))
    return pl.pallas_call(
        body, name=name, grid=(ncol, lp // bm), in_specs=row_specs + par_specs, out_specs=[o_spec] * n_out,
        out_shape=[jax.ShapeDtypeStruct((lp, ncol * width), dt) for dt in out_dtypes],
        compiler_params=pltpu.CompilerParams(dimension_semantics=("arbitrary", "arbitrary"), vmem_limit_bytes=VMEM_LIMIT),
    )(*[a for a, _ in rows], *[p for p, _ in params])


def rowmap_bwd(name, fn, rows, params, d_outs, *, width, ncol, bm, lp, d_row_dtypes=None):
    d_row_dtypes = d_row_dtypes or [F32] * len(rows)
    row_specs, par_specs = _row_specs(rows, params, width, bm)
    nr, npar, nout = len(rows), len(params), len(d_outs)

    def body(*refs):
        ins = refs[:nr + npar]
        dos = refs[nr + npar:nr + npar + nout]
        d_rows = refs[nr + npar + nout:nr + npar + nout + nr]
        d_pars = refs[nr + npar + nout + nr:]
        j, i = pl.program_id(0), pl.program_id(1)
        row_ids = i * bm + _iota2((bm, 1), 0)
        _, pull = jax.vjp(lambda *xs: tuple(fn(row_ids, *xs)), *[r[...].astype(F32) for r in ins])
        grads = pull(tuple(d[...].astype(F32) for d in dos))
        for ref, val in zip(d_rows, grads[:nr]):
            ref[...] = val.astype(ref.dtype)
        for ref, val, (_, per_col) in zip(d_pars, grads[nr:], params):
            first = (i == 0) if per_col else ((i == 0) & (j == 0))

            @pl.when(first)
            def _(ref=ref, val=val):
                ref[...] = val

            @pl.when(jnp.logical_not(first))
            def _(ref=ref, val=val):
                ref[...] += val

    o_spec = pl.BlockSpec((bm, width), lambda j, i: (i, j))
    res = pl.pallas_call(
        body, name=name, grid=(ncol, lp // bm), in_specs=row_specs + par_specs + [o_spec] * nout,
        out_specs=[o_spec] * nr + par_specs,
        out_shape=[jax.ShapeDtypeStruct((lp, ncol * width), dt) for dt in d_row_dtypes]
        + [jax.ShapeDtypeStruct(p.shape, F32) for p, _ in params],
        compiler_params=pltpu.CompilerParams(dimension_semantics=("arbitrary", "arbitrary"), vmem_limit_bytes=VMEM_LIMIT),
    )(*[a for a, _ in rows], *[p for p, _ in params], *d_outs)
    return res[:nr], res[nr:]


def _silu(x):
    return x * jax.nn.sigmoid(x)


def _softplus(x):
    return jnp.maximum(x, 0.0) + jnp.log(1.0 + jnp.exp(-jnp.abs(x)))


def _real_rows(row_ids):
    return (row_ids >= PAD).astype(F32)


def _f_rmsnorm(row_ids, h, g):
    return (h * lax.rsqrt(jnp.mean(h * h, axis=-1, keepdims=True) + RMS_EPS) * g,)


def _f_small_gates(row_ids, small, bias, a_log):
    lane = _iota2(small.shape, 1)
    t = small + bias
    sp = _softplus(t)
    coef = -jnp.exp(a_log)
    keep = _real_rows(row_ids)
    first = jnp.where(lane < 8, jax.nn.sigmoid(t), jnp.where(lane < 16, coef * sp, jnp.where(lane < 48, sp, 0.0)))
    second = jnp.where((lane >= 8) & (lane < 48), coef * sp, 0.0)
    return first * keep, second * keep


def _f_gate_silu(row_ids, o, z):
    return (o * _silu(z),)


def _f_head_norm_gate(row_ids, o, z, g):
    out = []
    for h in range(o.shape[1] // HEAD_DIM):
        oh = o[:, h * HEAD_DIM:(h + 1) * HEAD_DIM]
        out.append(oh * lax.rsqrt(jnp.mean(oh * oh, axis=-1, keepdims=True) + RMS_EPS) * g)
    return (jnp.concatenate(out, axis=1) * _silu(z),)


def _f_ssm_out(row_ids, y, xh, z, d_skip, g):
    t = (y + d_skip * xh) * _silu(z)
    return (t * lax.rsqrt(jnp.mean(t * t, axis=-1, keepdims=True) + RMS_EPS) * g,)


def _f_merge(row_ids, pa, pb, pc, ga, gb, gc):
    return (jax.nn.sigmoid(ga) * pa + jax.nn.sigmoid(gb) * pb + jax.nn.sigmoid(gc) * pc,)


def _shift_rows(x, s):
    s = s % x.shape[0]
    return x if s == 0 else pltpu.roll(x, s, 0)


def _conv_pre(x, w, b):
    pre = b
    for kk in range(CONV_K):
        pre = pre + w[kk:kk + 1, :] * _shift_rows(x, CONV_K - 1 - kk)
    return pre


def _conv_post(pre, l2_flag, keep):
    act = _silu(pre)
    nrm = act * lax.rsqrt(jnp.sum(act * act, axis=-1, keepdims=True) + L2_EPS)
    return (l2_flag * nrm + (1.0 - l2_flag) * act) * keep


def conv_fwd(name, src, col_off, w, b, n_l2):
    lp, ch = src.shape[0], w.shape[1]

    def body(x_ref, w_ref, b_ref, o_ref):
        l2_flag = (pl.program_id(0) < n_l2).astype(F32)
        keep = _real_rows(_iota2((lp, 1), 0))
        o_ref[...] = _conv_post(_conv_pre(x_ref[...], w_ref[...], b_ref[...]), l2_flag, keep)

    return pl.pallas_call(
        body, name=name, grid=(ch // HEAD_DIM,),
        in_specs=[pl.BlockSpec((lp, HEAD_DIM), lambda j: (0, col_off + j)), pl.BlockSpec((CONV_K, HEAD_DIM), lambda j: (0, j)),
                  pl.BlockSpec((1, HEAD_DIM), lambda j: (0, j))],
        out_specs=pl.BlockSpec((lp, HEAD_DIM), lambda j: (0, j)),
        out_shape=jax.ShapeDtypeStruct((lp, ch), F32),
        compiler_params=pltpu.CompilerParams(dimension_semantics=("arbitrary",), vmem_limit_bytes=VMEM_LIMIT),
    )(src, w, b)


def conv_bwd(name, src, col_off, w, b, n_l2, d_out):
    lp, ch = src.shape[0], w.shape[1]

    def body(x_ref, w_ref, b_ref, do_ref, dx_ref, dw_ref, db_ref):
        l2_flag = (pl.program_id(0) < n_l2).astype(F32)
        keep = _real_rows(_iota2((lp, 1), 0))
        x, wv = x_ref[...], w_ref[...]
        pre = _conv_pre(x, wv, b_ref[...])
        _, pull = jax.vjp(lambda p: _conv_post(p, l2_flag, keep), pre)
        (d_pre,) = pull(do_ref[...])
        dx = jnp.zeros_like(x)
        for kk in range(CONV_K):
            s = CONV_K - 1 - kk
            dx = dx + wv[kk:kk + 1, :] * _shift_rows(d_pre, -s)
            dw_ref[kk:kk + 1, :] = jnp.sum(d_pre * _shift_rows(x, s), axis=0, keepdims=True)
        dx_ref[...] = (dx * keep).astype(dx_ref.dtype)
        db_ref[...] = jnp.sum(d_pre, axis=0, keepdims=True)

    seq = pl.BlockSpec((lp, HEAD_DIM), lambda j: (0, j))
    wspec = pl.BlockSpec((CONV_K, HEAD_DIM), lambda j: (0, j))
    bspec = pl.BlockSpec((1, HEAD_DIM), lambda j: (0, j))
    return pl.pallas_call(
        body, name=name, grid=(ch // HEAD_DIM,),
        in_specs=[pl.BlockSpec((lp, HEAD_DIM), lambda j: (0, col_off + j)), wspec, bspec, seq],
        out_specs=[seq, wspec, bspec],
        out_shape=[jax.ShapeDtypeStruct((lp, ch), BF16), jax.ShapeDtypeStruct(w.shape, F32), jax.ShapeDtypeStruct(b.shape, F32)],
        compiler_params=pltpu.CompilerParams(dimension_semantics=("arbitrary",), vmem_limit_bytes=VMEM_LIMIT),
    )(src, w, b, d_out)


def loss_head(h, target, g):
    lp, d = h.shape
    bm = SB_BLOCK
    first = (PAD + N_META) // bm

    def body(h_ref, t_ref, g_ref, loss_ref, dh_ref, dg_ref):
        i = pl.program_id(0)
        keep = (i >= first).astype(F32)

        def f(hv, gv):
            y = hv * lax.rsqrt(jnp.mean(hv * hv, axis=-1, keepdims=True) + RMS_EPS) * gv
            err = y - t_ref[...]
            return 0.5 * jnp.sum(jnp.mean(err * err, axis=-1, keepdims=True), axis=0, keepdims=True) * keep

        val, pull = jax.vjp(f, h_ref[...], g_ref[...])
        dh, dg = pull(jnp.ones((1, 1), F32))
        dh_ref[...] = dh

        @pl.when(i == 0)
        def _():
            loss_ref[...] = val
            dg_ref[...] = dg

        @pl.when(i > 0)
        def _():
            loss_ref[...] += val
            dg_ref[...] += dg

    row = pl.BlockSpec((bm, d), lambda i: (i, 0))
    return pl.pallas_call(
        body, name="loss_head", grid=(lp // bm,),
        in_specs=[row, pl.BlockSpec((bm, d), lambda i: (jnp.maximum(i - first, 0), 0)), pl.BlockSpec((1, d), lambda i: (0, 0))],
        out_specs=[pl.BlockSpec((1, 1), lambda i: (0, 0)), row, pl.BlockSpec((1, d), lambda i: (0, 0))],
        out_shape=[jax.ShapeDtypeStruct((1, 1), F32), jax.ShapeDtypeStruct((lp, d), F32), jax.ShapeDtypeStruct((1, d), F32)],
        compiler_params=pltpu.CompilerParams(dimension_semantics=("arbitrary",)),
    )(h, target, g)


ADAM_LR, ADAM_B1, ADAM_B2, ADAM_EPS, ADAM_WD, ADAM_STEP = 0.001, 0.9, 0.999, 1e-08, 0.01, 10


def adamw(name, w, g, m, v, echo_g=False):
    lead = w.shape[:-2]
    rows, cols = w.shape[-2:]
    br = _pick(rows, 128, 8)
    n_out = 4 if echo_g else 3

    def body(w_ref, g_ref, m_ref, v_ref, d_ref, nm_ref, nv_ref, *echo):
        gv = g_ref[...]
        nm = ADAM_B1 * m_ref[...] + (1.0 - ADAM_B1) * gv
        nv = ADAM_B2 * v_ref[...] + (1.0 - ADAM_B2) * (gv * gv)
        m_hat = nm / (1.0 - ADAM_B1 ** ADAM_STEP)
        v_hat = nv / (1.0 - ADAM_B2 ** ADAM_STEP)
        d_ref[...] = -ADAM_LR * (m_hat / (jnp.sqrt(v_hat) + ADAM_EPS) + ADAM_WD * w_ref[...])
        nm_ref[...] = nm
        nv_ref[...] = nv
        for e_ref in echo:
            e_ref[...] = gv

    if lead and rows <= 8:
        bl = _pick(lead[0], 32, 1)
        spec = pl.BlockSpec((bl, rows, cols), lambda s: (s, 0, 0))
        grid = (lead[0] // bl,)
    elif lead:
        spec = pl.BlockSpec((None, br, cols), lambda s, i: (s, i, 0))
        grid = (lead[0], rows // br)
    else:
        spec = pl.BlockSpec((br, cols), lambda i: (i, 0))
        grid = (rows // br,)
    return pl.pallas_call(
        body, name=name, grid=grid, in_specs=[spec] * 4, out_specs=[spec] * n_out,
        out_shape=[jax.ShapeDtypeStruct(w.shape, F32)] * n_out,
        compiler_params=pltpu.CompilerParams(dimension_semantics=("arbitrary",) * len(grid), vmem_limit_bytes=VMEM_LIMIT),
    )(w, g, m, v)


MESH = pl.DeviceIdType.MESH
ANY = pl.BlockSpec(memory_space=pl.ANY)
D2D_PIECES = 16
ICI_PIECES = 4


def _place():
    x, y, c = lax.axis_index("x"), lax.axis_index("y"), lax.axis_index("c")
    return x, y, c, [(1 - x, y), (x, 1 - y), (1 - x, 1 - y)]


def _pieces(rows, n, unit):
    per = -(-rows // (n * unit)) * unit
    return [(s, min(per, rows - s)) for s in range(0, rows, per)]


def _row_unit(dtype):
    return 16 if dtype == BF16 else 8


def _scalar(v):
    return jnp.reshape(v, (1,)).astype(jnp.int32)


def place_shard(name, pack):
    rows, cols = pack.shape
    br = _pick(rows, 256, 16)

    def body(m_ref, p_ref, o_ref):
        o_ref[...] = p_ref[...]

    return pl.pallas_call(
        body, name=name,
        grid_spec=pltpu.PrefetchScalarGridSpec(
            num_scalar_prefetch=1, grid=(rows // br,),
            in_specs=[pl.BlockSpec((br, cols), lambda i, m: (i, 0))],
            out_specs=pl.BlockSpec((None, br, cols), lambda i, m: (m[0], i, 0))),
        out_shape=jax.ShapeDtypeStruct((4, rows, cols), pack.dtype),
        compiler_params=pltpu.CompilerParams(dimension_semantics=("arbitrary",), vmem_limit_bytes=VMEM_LIMIT),
    )(_scalar(2 * lax.axis_index("x") + lax.axis_index("y")), pack)


class Ride:
    def __init__(self, ins, out_shapes, alias, sems, start, finish):
        self.ins, self.out_shapes, self.alias, self.sems, self.start, self.finish = list(ins), out_shapes, alias, sems, start, finish


def _gather_parts(o_refs, send_sems, recv_sems):
    x, y, c, chips = _place()
    mine = 2 * x + y

    def half_rows(b, which, start=0, size=None):
        half = o_refs[b].shape[1] // 2
        return pl.ds(pl.multiple_of(which * half + start, _row_unit(o_refs[b].dtype)), half if size is None else size)

    def remote(b, k, slot, rws, to):
        piece = o_refs[b].at[slot, rws, :]
        return pltpu.make_async_remote_copy(src_ref=piece, dst_ref=piece, send_sem=send_sems.at[b, k], recv_sem=recv_sems.at[b, k],
                                            device_id=to, device_id_type=MESH)

    return x, y, c, chips, mine, half_rows, remote


def _gather_start(o_refs, send_sems, recv_sems):
    x, y, c, chips, mine, half_rows, remote = _gather_parts(o_refs, send_sems, recv_sems)
    for b, o_ref in enumerate(o_refs):
        for j, (cx, cy) in enumerate(chips):
            for start, size in _pieces(o_ref.shape[1] // 2, ICI_PIECES, _row_unit(o_ref.dtype)):
                remote(b, j, mine, half_rows(b, c, start, size), (cx, cy, c)).start()


def _gather_finish(o_refs, send_sems, recv_sems):
    x, y, c, chips, mine, half_rows, remote = _gather_parts(o_refs, send_sems, recv_sems)
    sends = []
    for b, o_ref in enumerate(o_refs):
        for j, (cx, cy) in enumerate(chips):
            slot = 2 * cx + cy
            sends.append(remote(b, j, mine, half_rows(b, c), (cx, cy, c)))
            remote(b, j, slot, half_rows(b, c), (cx, cy, c)).wait_recv()
            for start, size in _pieces(o_ref.shape[1] // 2, D2D_PIECES, _row_unit(o_ref.dtype)):
                remote(b, 3 + j, slot, half_rows(b, c, start, size), (x, y, 1 - c)).start()
            sends.append(remote(b, 3 + j, slot, half_rows(b, c), (x, y, 1 - c)))
    for b in range(len(o_refs)):
        for j, (cx, cy) in enumerate(chips):
            remote(b, 3 + j, 2 * cx + cy, half_rows(b, 1 - c), (x, y, 1 - c)).wait_recv()
    for cp in sends:
        cp.wait_send()


def gather_ride(placed):
    n = len(placed)
    return Ride(placed, [jax.ShapeDtypeStruct(p.shape, p.dtype) for p in placed], True,
                [pltpu.SemaphoreType.DMA((n, 6)), pltpu.SemaphoreType.DMA((n, 6))],
                lambda ins, outs, sems: _gather_start(outs, *sems), lambda ins, outs, sems: _gather_finish(outs, *sems))


def gather_shards(name, placed):
    n = len(placed)

    def body(*refs):
        o_refs, sems = refs[n:2 * n], refs[2 * n:]
        _gather_start(o_refs, *sems)
        _gather_finish(o_refs, *sems)

    return pl.pallas_call(
        body, name=name, in_specs=[ANY] * n, out_specs=[ANY] * n,
        out_shape=[jax.ShapeDtypeStruct(p.shape, p.dtype) for p in placed],
        input_output_aliases={i: i for i in range(n)},
        scratch_shapes=[pltpu.SemaphoreType.DMA((n, 6)), pltpu.SemaphoreType.DMA((n, 6))],
    )(*placed)


def _split_copies(g_refs, t_refs, send_sems, recv_sems, start):
    x, y, c, _ = _place()
    waits = []
    for b, (g_ref, t_ref) in enumerate(zip(g_refs, t_refs)):
        half = g_ref.shape[2]

        def copy(slots, first, size):
            rws = pl.ds(first, size)
            return pltpu.make_async_remote_copy(src_ref=g_ref.at[slots, 1 - c, rws, :], dst_ref=t_ref.at[slots, rws, :],
                                                send_sem=send_sems.at[b], recv_sem=recv_sems.at[b], device_id=(x, y, 1 - c),
                                                device_id_type=MESH)

        if start:
            for s in range(4):
                for first, size in _pieces(half, D2D_PIECES // 4, _row_unit(g_ref.dtype)):
                    copy(s, first, size).start()
        else:
            waits.append(copy(slice(None), 0, half))
    return waits


def _split_finish(g_refs, t_refs, send_sems, recv_sems):
    for cp in _split_copies(g_refs, t_refs, send_sems, recv_sems, False):
        cp.wait()


def _split_shapes(bufs):
    return [jax.ShapeDtypeStruct((4,) + g.shape[2:], g.dtype) for g in bufs]


def split_ride(bufs):
    n = len(bufs)
    return Ride(bufs, _split_shapes(bufs), False, [pltpu.SemaphoreType.DMA((n,)), pltpu.SemaphoreType.DMA((n,))],
                lambda ins, outs, sems: _split_copies(ins, outs, *sems, True), lambda ins, outs, sems: _split_finish(ins, outs, *sems))


def pair_split(name, bufs):
    n = len(bufs)

    def body(*refs):
        g_refs, t_refs, sems = refs[:n], refs[n:2 * n], refs[2 * n:]
        _split_copies(g_refs, t_refs, *sems, True)
        _split_finish(g_refs, t_refs, *sems)

    return pl.pallas_call(
        body, name=name, in_specs=[ANY] * n, out_specs=[ANY] * n, out_shape=_split_shapes(bufs),
        scratch_shapes=[pltpu.SemaphoreType.DMA((n,)), pltpu.SemaphoreType.DMA((n,))],
    )(*bufs)


def pair_add(name, g, theirs, transit):
    _, _, half, cols = g.shape
    br = _pick(half, 128, 16)

    def body(c_ref, g_ref, t_ref, o_ref):
        o_ref[...] = (g_ref[...].astype(F32) + t_ref[...].astype(F32)).astype(transit)

    blk = (4, br, cols)
    return pl.pallas_call(
        body, name=name,
        grid_spec=pltpu.PrefetchScalarGridSpec(
            num_scalar_prefetch=1, grid=(half // br,),
            in_specs=[pl.BlockSpec((4, None, br, cols), lambda i, c: (0, c[0], i, 0)), pl.BlockSpec(blk, lambda i, c: (0, i, 0))],
            out_specs=pl.BlockSpec(blk, lambda i, c: (0, i, 0))),
        out_shape=jax.ShapeDtypeStruct((4, half, cols), transit),
        compiler_params=pltpu.CompilerParams(dimension_semantics=("arbitrary",), vmem_limit_bytes=VMEM_LIMIT),
    )(_scalar(lax.axis_index("c")), g, theirs)


def _exchange_copies(a_refs, o_refs, send_sems, recv_sems, start):
    x, y, c, chips = _place()
    mine = 2 * x + y
    waits = []
    for b, (a_ref, o_ref) in enumerate(zip(a_refs, o_refs)):
        rows = a_ref.shape[1]
        for j, (cx, cy) in enumerate(chips):
            def copy(first, size):
                rws = pl.ds(first, size)
                return pltpu.make_async_remote_copy(src_ref=a_ref.at[2 * cx + cy, rws, :], dst_ref=o_ref.at[mine, rws, :],
                                                    send_sem=send_sems.at[b, j], recv_sem=recv_sems.at[b, j],
                                                    device_id=(cx, cy, c), device_id_type=MESH)
            if start:
                for first, size in _pieces(rows, ICI_PIECES, _row_unit(a_ref.dtype)):
                    copy(first, size).start()
            else:
                waits.append(copy(0, rows))
    return waits


def _exchange_finish(a_refs, o_refs, send_sems, recv_sems):
    for cp in _exchange_copies(a_refs, o_refs, send_sems, recv_sems, False):
        cp.wait()


def exchange_ride(parts):
    n = len(parts)
    return Ride(parts, [jax.ShapeDtypeStruct(a.shape, a.dtype) for a in parts], False,
                [pltpu.SemaphoreType.DMA((n, 3)), pltpu.SemaphoreType.DMA((n, 3))],
                lambda ins, outs, sems: _exchange_copies(ins, outs, *sems, True), lambda ins, outs, sems: _exchange_finish(ins, outs, *sems))


def chip_exchange(name, parts):
    n = len(parts)

    def body(*refs):
        a_refs, o_refs, sems = refs[:n], refs[n:2 * n], refs[2 * n:]
        _exchange_copies(a_refs, o_refs, *sems, True)
        _exchange_finish(a_refs, o_refs, *sems)

    return pl.pallas_call(
        body, name=name, in_specs=[ANY] * n, out_specs=[ANY] * n,
        out_shape=[jax.ShapeDtypeStruct(a.shape, a.dtype) for a in parts],
        scratch_shapes=[pltpu.SemaphoreType.DMA((n, 3)), pltpu.SemaphoreType.DMA((n, 3))],
    )(*parts)


def chip_add(name, got, part):
    _, rows, cols = got.shape
    br = _pick(rows, 128, 16)
    nblk = rows // br

    def body(m_ref, c_ref, got_ref, part_ref, o_ref):
        mine = m_ref[0]
        for s in range(4):
            @pl.when(mine == s)
            def _(s=s):
                val = part_ref[...].astype(F32)
                o_ref[...] = val if s == 0 else o_ref[...] + val

            @pl.when(mine != s)
            def _(s=s):
                val = got_ref[s].astype(F32)
                o_ref[...] = val if s == 0 else o_ref[...] + val

    return pl.pallas_call(
        body, name=name,
        grid_spec=pltpu.PrefetchScalarGridSpec(
            num_scalar_prefetch=2, grid=(nblk,),
            in_specs=[pl.BlockSpec((4, br, cols), lambda i, m, c: (0, i, 0)),
                      pl.BlockSpec((None, br, cols), lambda i, m, c: (m[0], i, 0))],
            out_specs=pl.BlockSpec((br, cols), lambda i, m, c: (c[0] * nblk + i, 0))),
        out_shape=jax.ShapeDtypeStruct((2 * rows, cols), F32),
        compiler_params=pltpu.CompilerParams(dimension_semantics=("arbitrary",), vmem_limit_bytes=VMEM_LIMIT),
    )(_scalar(2 * lax.axis_index("x") + lax.axis_index("y")), _scalar(lax.axis_index("c")), got, part)


def pair_join(name, fulls):
    n = len(fulls)

    def body(*refs):
        o_refs = refs[n:2 * n]
        send_sems, recv_sems = refs[2 * n:]
        x, y, c, _ = _place()
        waits = []
        for b, o_ref in enumerate(o_refs):
            half = o_ref.shape[0] // 2
            unit = _row_unit(o_ref.dtype)

            def copy(start, size):
                piece = o_ref.at[pl.ds(pl.multiple_of(c * half + start, unit), size), :]
                return pltpu.make_async_remote_copy(src_ref=piece, dst_ref=piece, send_sem=send_sems.at[b], recv_sem=recv_sems.at[b],
                                                    device_id=(x, y, 1 - c), device_id_type=MESH)

            for start, size in _pieces(half, D2D_PIECES, unit):
                copy(start, size).start()
            waits.append(copy(0, half))
        for cp in waits:
            cp.wait()

    return pl.pallas_call(
        body, name=name, in_specs=[ANY] * n, out_specs=[ANY] * n,
        out_shape=[jax.ShapeDtypeStruct(f.shape, f.dtype) for f in fulls],
        input_output_aliases={i: i for i in range(n)},
        scratch_shapes=[pltpu.SemaphoreType.DMA((n,)), pltpu.SemaphoreType.DMA((n,))],
    )(*fulls)


D_IN = 15920
D_PROJ = 16000
_SEGMENTS = ((0, 8192), (8208, 12816), (12848, 15920), (8192, 8208), (12816, 12848))
OFF_SB_Z, OFF_GDN_QKV, OFF_GDN_Z, OFF_SSM_Z, OFF_SSM_XBC, OFF_GATES, OFF_SMALL = 3072, 4096, 7168, 8192, 10240, 12800, 15872
PACK_C = 1024
WEIGHTS = ("meta_tokens", "norm_g", "w_in", "gdn_conv_w", "gdn_a_log", "gdn_dt_bias", "gdn_norm_g", "ssm_conv_w", "ssm_conv_b",
           "ssm_a_log", "ssm_dt_bias", "ssm_d", "ssm_norm_g", "w_branch_a", "w_branch_b", "w_branch_c", "w_out", "final_norm_g")
SHARDED = ("w_in", "w_branch_a", "w_branch_b", "w_branch_c", "w_out", "gdn_conv_w", "ssm_conv_w", "meta_tokens")
SHARD_AXIS = {"w_in": 2, "w_branch_a": 1, "w_branch_b": 1, "w_branch_c": 1, "w_out": 1, "gdn_conv_w": 2, "ssm_conv_w": 2, "meta_tokens": 1}
BRANCH = ("w_branch_a", "w_branch_b", "w_branch_c", "w_out")
EXACT = ("gdn_conv_w", "ssm_conv_w", "meta_tokens")
REPLICATED = tuple(n for n in WEIGHTS if n not in SHARDED)


def _regrouped_from_shards(shard_cols):
    out = []
    for a, b in _SEGMENTS:
        while a < b:
            chip = a // shard_cols
            stop = min(b, (chip + 1) * shard_cols)
            out.append((chip, a - chip * shard_cols, stop - chip * shard_cols))
            a = stop
    return out


def _shard_from_regrouped(chip, shard_cols):
    lo, hi = chip * shard_cols, (chip + 1) * shard_cols
    out, pos = [], 0
    starts = {}
    for a, b in _SEGMENTS:
        starts[(a, b)] = pos
        pos += b - a
    for a, b in sorted(_SEGMENTS):
        s0, s1 = max(a, lo), min(b, hi)
        if s0 < s1:
            out.append((starts[(a, b)] + s0 - a, starts[(a, b)] + s1 - a))
    return out


def _pack(parts, row_unit=64):
    n = sum(p.shape[0] for p in parts)
    rows = -(-n // (PACK_C * row_unit)) * row_unit
    flat = jnp.concatenate(list(parts) + [jnp.zeros((rows * PACK_C - n,), parts[0].dtype)])
    return flat.reshape(rows, PACK_C)


def _unpack(buf, shapes):
    flat, out, pos = buf.reshape(-1), [], 0
    for shp in shapes:
        n = math.prod(shp)
        out.append(flat[pos:pos + n].reshape(shp))
        pos += n
    return out


def _as_bf16_words(a):
    return lax.bitcast_convert_type(a, BF16).reshape(-1)


BRANCH_ROWS = (D_MODEL // 4, D_MODEL // 4, SSM_INNER // 4, D_MODEL // 4)


def _place_weights(w):
    depth = w["w_in"].shape[0]
    layers = []
    for l in range(depth):
        a = w["w_in"][l].astype(BF16)
        b = jnp.concatenate([w[n][l] for n in BRANCH], axis=0).astype(BF16)
        layers.append([place_shard("place_w_in", a), place_shard("place_branch", b)])
    small = place_shard("place_exact", _pack([_as_bf16_words(w[n]) for n in EXACT]))
    return layers, small


def _exact_weights(w, got_s):
    per_chip = [_unpack(got_s[c], [w[n].shape + (2,) for n in EXACT]) for c in range(4)]
    return {n: jnp.concatenate([lax.bitcast_convert_type(per_chip[c][i], F32) for c in range(4)], axis=SHARD_AXIS[n])
            for i, n in enumerate(EXACT)}


def _proj_weight(got_a):
    d_model, shard_cols = got_a.shape[1:]
    pad = jnp.zeros((d_model, D_PROJ - D_IN), BF16)
    return jnp.concatenate([got_a[c, :, lo:hi] for c, lo, hi in _regrouped_from_shards(shard_cols)] + [pad], axis=1)


def _branch_weights(got_b):
    out, pos = {}, 0
    for n, rows in zip(("wa", "wb", "wc", "wo"), BRANCH_ROWS):
        out[n] = jnp.concatenate([got_b[c, pos:pos + rows] for c in range(4)], axis=0)
        pos += rows
    return out


def _shard(a, axis, s):
    size = a.shape[axis] // 4
    return lax.slice_in_dim(a, s * size, (s + 1) * size, axis=axis)


def _layer_grad_buffers(g, shard_cols):
    buf_a = jnp.stack([jnp.concatenate([g["w_in"][:, lo:hi] for lo, hi in _shard_from_regrouped(s, shard_cols)], axis=1)
                       for s in range(4)]).astype(BF16)
    buf_b = jnp.stack([jnp.concatenate([_shard(g[n], 0, s) for n in BRANCH], axis=0) for s in range(4)]).astype(BF16)
    return [buf_a, buf_b]


def _split_halves(buf):
    return buf.reshape(4, 2, buf.shape[1] // 2, buf.shape[2])


def _start_reduce(tag, bufs, transits):
    bufs = [_split_halves(g) for g in bufs]
    theirs = pair_split(tag + "_pair_split", bufs)
    return [pair_add(f"{tag}_pair_add_{i}", g, t, tr) for i, (g, t, tr) in enumerate(zip(bufs, theirs, transits))]


def _layer_params(w, exact, wp, l):
    lane = lambda v, lo: jnp.pad(v, (lo, HEAD_DIM - lo - v.shape[0]))[None]
    return dict(
        norm_g=w["norm_g"][l][None], wp=wp,
        gdn_conv_w=exact["gdn_conv_w"][l], gdn_conv_b=jnp.zeros((1, 3 * N_HEADS * HEAD_DIM), F32),
        ssm_conv_w=exact["ssm_conv_w"][l], ssm_conv_b=w["ssm_conv_b"][l][None],
        bias_vec=lane(w["gdn_dt_bias"][l], 8) + lane(w["ssm_dt_bias"][l], 16),
        alog_vec=lane(w["gdn_a_log"][l], 8) + lane(w["ssm_a_log"][l], 16),
        gdn_norm_g=w["gdn_norm_g"][l][None], d_skip=jnp.repeat(w["ssm_d"][l], SSM_P)[None], ssm_norm_g=w["ssm_norm_g"][l][None])


def _layer_fwd(h, p, ride):
    lp = h.shape[0]
    bm = _pick(lp, 272, 8)
    kw = dict(bm=bm, lp=lp)
    (u,) = rowmap_fwd("rms_fwd", _f_rmsnorm, [(h, 0)], [(p["norm_g"], False)], 1, width=D_MODEL, ncol=1, out_dtypes=[BF16], **kw)
    proj = matmul(u, p["wp"], "nn", name="proj", bm=lp, bn=640)
    o_a_raw, *rode = sb_fwd(proj, (0, N_HEADS, 2 * N_HEADS), ride)
    p = dict(p, **_branch_weights(rode[0]))
    qkv = conv_fwd("gdn_conv_fwd", proj, OFF_GDN_QKV // HEAD_DIM, p["gdn_conv_w"], p["gdn_conv_b"], 2 * N_HEADS)
    first, second = rowmap_fwd("gates_fwd", _f_small_gates, [(proj, OFF_SMALL // HEAD_DIM)],
                               [(p["bias_vec"], False), (p["alog_vec"], False)], 2, width=HEAD_DIM, ncol=1, **kw)
    o_b_raw, gdn_states, gdn_t = gdn_fwd(qkv, first)
    xbc = conv_fwd("ssm_conv_fwd", proj, OFF_SSM_XBC // HEAD_DIM, p["ssm_conv_w"], p["ssm_conv_b"], 0)
    y_raw, ssd_states = ssd_fwd(xbc, first, second)
    (o_a,) = rowmap_fwd("gate_a_fwd", _f_gate_silu, [(o_a_raw, 0), (proj, OFF_SB_Z // 1024)], [], 1, width=1024, ncol=1,
                        out_dtypes=[BF16], **kw)
    (o_b,) = rowmap_fwd("gate_b_fwd", _f_head_norm_gate, [(o_b_raw, 0), (proj, OFF_GDN_Z // 1024)], [(p["gdn_norm_g"], False)], 1,
                        width=1024, ncol=1, out_dtypes=[BF16], **kw)
    (o_c,) = rowmap_fwd("gate_c_fwd", _f_ssm_out, [(y_raw, 0), (xbc, 0), (proj, OFF_SSM_Z // 1024)],
                        [(p["d_skip"], True), (p["ssm_norm_g"], True)], 1, width=1024, ncol=SSM_GROUPS, out_dtypes=[BF16], **kw)
    pa = matmul(o_a, p["wa"], "nn", name="branch_a", bm=lp // 2, bn=512)
    pb = matmul(o_b, p["wb"], "nn", name="branch_b", bm=lp // 2, bn=512)
    pc = matmul(o_c, p["wc"], "nn", name="branch_c", bm=lp // 2, bn=512)
    merge_rows = [(pa, 0), (pb, 0), (pc, 0)] + [(proj, OFF_GATES // 512 + 2 * i) for i in range(3)]
    (merged,) = rowmap_fwd("merge_fwd", _f_merge, merge_rows, [], 1, width=512, ncol=2, out_dtypes=[BF16], **kw)
    h_out = matmul(merged, p["wo"], "nn", name="out_proj", bm=lp, bn=512, residual=h)
    saved = dict(h=h, u=u, proj=proj, qkv=qkv, first=first, second=second, o_a_raw=o_a_raw, o_b_raw=o_b_raw,
                 gdn_states=gdn_states, gdn_t=gdn_t, xbc=xbc, y_raw=y_raw, ssd_states=ssd_states, o_a=o_a, o_b=o_b, o_c=o_c, pa=pa, pb=pb, pc=pc,
                 merged=merged, params=p)
    return h_out, saved, rode[1:]


def _layer_bwd(d_h, p, s, ride):
    lp = d_h.shape[0]
    bm = _pick(lp, 272, 8)
    kw = dict(bm=bm, lp=lp)
    proj = s["proj"]
    g = {}
    d_merged = matmul(d_h, p["wo"], "nt", name="d_merged", bm=lp, bn=512)
    g["w_out"] = matmul(s["merged"], d_h, "tn", name="g_w_out", bm=512, bn=1024, bk=lp)
    merge_rows = [(s["pa"], 0), (s["pb"], 0), (s["pc"], 0)] + [(proj, OFF_GATES // 512 + 2 * i) for i in range(3)]
    (d_pa, d_pb, d_pc, d_ga, d_gb, d_gc), _ = rowmap_bwd("merge_bwd", _f_merge, merge_rows, [], [d_merged], width=512, ncol=2,
                                                         d_row_dtypes=[BF16] * 6, **kw)
    g["w_branch_a"] = matmul(s["o_a"], d_pa, "tn", name="g_w_a", bm=512, bn=1024, bk=lp)
    g["w_branch_b"] = matmul(s["o_b"], d_pb, "tn", name="g_w_b", bm=512, bn=1024, bk=lp)
    g["w_branch_c"] = matmul(s["o_c"], d_pc, "tn", name="g_w_c", bm=512, bn=1024, bk=lp)
    d_oa = matmul(d_pa, p["wa"], "nt", name="d_o_a", bm=lp, bn=512)
    d_ob = matmul(d_pb, p["wb"], "nt", name="d_o_b", bm=lp, bn=512)
    d_oc = matmul(d_pc, p["wc"], "nt", name="d_o_c", bm=lp, bn=512)
    (d_oa_raw, d_sbz), _ = rowmap_bwd("gate_a_bwd", _f_gate_silu, [(s["o_a_raw"], 0), (proj, OFF_SB_Z // 1024)], [], [d_oa],
                                      width=1024, ncol=1, d_row_dtypes=[F32, BF16], **kw)
    (d_ob_raw, d_gdz), (g["gdn_norm_g"],) = rowmap_bwd(
        "gate_b_bwd", _f_head_norm_gate, [(s["o_b_raw"], 0), (proj, OFF_GDN_Z // 1024)], [(p["gdn_norm_g"], False)], [d_ob],
        width=1024, ncol=1, d_row_dtypes=[F32, BF16], **kw)
    (d_y, d_xh, d_ssz), (g_dskip, g["ssm_norm_g"]) = rowmap_bwd(
        "gate_c_bwd", _f_ssm_out, [(s["y_raw"], 0), (s["xbc"], 0), (proj, OFF_SSM_Z // 1024)],
        [(p["d_skip"], True), (p["ssm_norm_g"], True)], [d_oc], width=1024, ncol=SSM_GROUPS, d_row_dtypes=[F32, F32, BF16], **kw)
    g["gdn_norm_g"], g["ssm_norm_g"] = g["gdn_norm_g"][0], g["ssm_norm_g"][0]
    g["ssm_d"] = g_dskip.reshape(SSM_HEADS, SSM_P).sum(axis=1)
    d_q, d_k, d_v, *rode = sb_bwd(proj, (0, N_HEADS, 2 * N_HEADS), d_oa_raw, ride)
    d_qkv, d_first_gdn = gdn_bwd(s["qkv"], s["first"], s["gdn_states"], s["gdn_t"], d_ob_raw)
    d_xbc_out, d_first_ssd, d_second = ssd_bwd(s["xbc"], s["first"], s["second"], s["ssd_states"], d_y, d_xh)
    d_gdqkv, g["gdn_conv_w"], _ = conv_bwd("gdn_conv_bwd", proj, OFF_GDN_QKV // HEAD_DIM, p["gdn_conv_w"], p["gdn_conv_b"], 2 * N_HEADS,
                                           d_qkv)
    d_xbc, g["ssm_conv_w"], g_cb = conv_bwd("ssm_conv_bwd", proj, OFF_SSM_XBC // HEAD_DIM, p["ssm_conv_w"], p["ssm_conv_b"], 0, d_xbc_out)
    g["ssm_conv_b"] = g_cb[0]
    d_first = d_first_gdn + d_first_ssd
    (d_small,), (g_bias, g_alog) = rowmap_bwd("gates_bwd", _f_small_gates, [(proj, OFF_SMALL // HEAD_DIM)],
                                              [(p["bias_vec"], False), (p["alog_vec"], False)], [d_first, d_second],
                                              width=HEAD_DIM, ncol=1, d_row_dtypes=[BF16], **kw)
    g["gdn_dt_bias"], g["ssm_dt_bias"] = g_bias[0, 8:16], g_bias[0, 16:48]
    g["gdn_a_log"], g["ssm_a_log"] = g_alog[0, 8:16], g_alog[0, 16:48]
    d_proj = jnp.concatenate([d_q, d_k, d_v, d_sbz, d_gdqkv, d_gdz, d_ssz, d_xbc, d_ga, d_gb, d_gc, d_small], axis=1)
    g["w_in"] = matmul(s["u"], d_proj, "tn", name="g_w_in", bm=1024, bn=640, bk=lp, out_dtype=BF16)
    bufs = [_split_halves(buf) for buf in _layer_grad_buffers(g, D_IN // 4)]
    d_u, *theirs = matmul(d_proj, p["wp"], "nt", name="d_u", bm=lp, bn=1024, bk=800, ride=split_ride(bufs))
    parts = [pair_add(f"grads_pair_add_{i}", buf, t, BF16) for i, (buf, t) in enumerate(zip(bufs, theirs))]
    (d_hn,), (g_norm,) = rowmap_bwd("rms_bwd", _f_rmsnorm, [(s["h"], 0)], [(p["norm_g"], False)], [d_u], width=D_MODEL, ncol=1, **kw)
    g["norm_g"] = g_norm[0]
    return d_h + d_hn, g, rode, parts


def kernel(x, meta_tokens, norm_g, w_in, gdn_conv_w, gdn_a_log, gdn_dt_bias, gdn_norm_g, ssm_conv_w, ssm_conv_b, ssm_a_log, ssm_dt_bias, ssm_d, ssm_norm_g, w_branch_a, w_branch_b, w_branch_c, w_out, final_norm_g, loss_target, m_meta_tokens, m_norm_g, m_w_in, m_gdn_conv_w, m_gdn_a_log, m_gdn_dt_bias, m_gdn_norm_g, m_ssm_conv_w, m_ssm_conv_b, m_ssm_a_log, m_ssm_dt_bias, m_ssm_d, m_ssm_norm_g, m_w_branch_a, m_w_branch_b, m_w_branch_c, m_w_out, m_final_norm_g, v_meta_tokens, v_norm_g, v_w_in, v_gdn_conv_w, v_gdn_a_log, v_gdn_dt_bias, v_gdn_norm_g, v_ssm_conv_w, v_ssm_conv_b, v_ssm_a_log, v_ssm_dt_bias, v_ssm_d, v_ssm_norm_g, v_w_branch_a, v_w_branch_b, v_w_branch_c, v_w_out, v_final_norm_g):
    w = dict(meta_tokens=meta_tokens, norm_g=norm_g, w_in=w_in, gdn_conv_w=gdn_conv_w, gdn_a_log=gdn_a_log, gdn_dt_bias=gdn_dt_bias,
             gdn_norm_g=gdn_norm_g, ssm_conv_w=ssm_conv_w, ssm_conv_b=ssm_conv_b, ssm_a_log=ssm_a_log, ssm_dt_bias=ssm_dt_bias,
             ssm_d=ssm_d, ssm_norm_g=ssm_norm_g, w_branch_a=w_branch_a, w_branch_b=w_branch_b, w_branch_c=w_branch_c, w_out=w_out,
             final_norm_g=final_norm_g)
    m = dict(meta_tokens=m_meta_tokens, norm_g=m_norm_g, w_in=m_w_in, gdn_conv_w=m_gdn_conv_w, gdn_a_log=m_gdn_a_log,
             gdn_dt_bias=m_gdn_dt_bias, gdn_norm_g=m_gdn_norm_g, ssm_conv_w=m_ssm_conv_w, ssm_conv_b=m_ssm_conv_b,
             ssm_a_log=m_ssm_a_log, ssm_dt_bias=m_ssm_dt_bias, ssm_d=m_ssm_d, ssm_norm_g=m_ssm_norm_g, w_branch_a=m_w_branch_a,
             w_branch_b=m_w_branch_b, w_branch_c=m_w_branch_c, w_out=m_w_out, final_norm_g=m_final_norm_g)
    v = dict(meta_tokens=v_meta_tokens, norm_g=v_norm_g, w_in=v_w_in, gdn_conv_w=v_gdn_conv_w, gdn_a_log=v_gdn_a_log,
             gdn_dt_bias=v_gdn_dt_bias, gdn_norm_g=v_gdn_norm_g, ssm_conv_w=v_ssm_conv_w, ssm_conv_b=v_ssm_conv_b,
             ssm_a_log=v_ssm_a_log, ssm_dt_bias=v_ssm_dt_bias, ssm_d=v_ssm_d, ssm_norm_g=v_ssm_norm_g, w_branch_a=v_w_branch_a,
             w_branch_b=v_w_branch_b, w_branch_c=v_w_branch_c, w_out=v_w_out, final_norm_g=v_final_norm_g)
    depth = norm_g.shape[0]
    placed, placed_small = _place_weights(w)
    got_a, got_s = gather_shards("gather_first", [placed[0][0], placed_small])
    exact = _exact_weights(w, got_s)

    h = jnp.concatenate([jnp.zeros((PAD, D_MODEL), F32), exact["meta_tokens"], x[0]], axis=0)
    params, saved = [], []
    for l in range(depth):
        ride = gather_ride([placed[l][1]] + ([placed[l + 1][0]] if l + 1 < depth else []))
        h, s, rode = _layer_fwd(h, _layer_params(w, exact, _proj_weight(got_a), l), ride)
        params.append(s.pop("params"))
        saved.append(s)
        if rode:
            (got_a,) = rode
    loss, d_h, g_final = loss_head(h, loss_target[0], final_norm_g[None])

    layer_grads, reds, waiting = [None] * depth, [None] * depth, None
    for l in reversed(range(depth)):
        d_h, layer_grads[l], rode, parts = _layer_bwd(d_h, params[l], saved[l], exchange_ride(waiting) if waiting else None)
        if waiting:
            reds[l + 1] = [chip_add(f"grads_chip_add_{i}", gt, p) for i, (gt, p) in enumerate(zip(rode, waiting))]
        waiting = parts
    grads = {n: jnp.stack([layer_grads[l][n] for l in range(depth)]) for n in WEIGHTS
             if n not in ("meta_tokens", "final_norm_g", "w_in") + BRANCH}
    grads["meta_tokens"] = d_h[PAD:PAD + N_META]
    grads["final_norm_g"] = g_final[0]
    grad_x = d_h[PAD + N_META:][None]
    buf_s = jnp.stack([_pack([_shard(grads[n], SHARD_AXIS[n], s).astype(BF16).reshape(-1) for n in EXACT], row_unit=32) for s in range(4)])
    small = _pack([grads[n].reshape(-1) for n in REPLICATED], row_unit=32)
    last = waiting + _start_reduce("small_grads", [buf_s, jnp.broadcast_to(small[None], (4,) + small.shape)], [BF16, F32])
    got = chip_exchange("grads_chip_exchange", last)
    sums = [chip_add(f"grads_chip_add_{i}", gt, p) for i, (gt, p) in enumerate(zip(got, last))]
    reds[0] = sums[:2]
    joined = pair_join("grads_pair_join", [r for layer in reds for r in layer] + sums[2:])
    red = {"w_in": jnp.stack(joined[0:2 * depth:2])}
    pos = 0
    for n, rows in zip(BRANCH, BRANCH_ROWS):
        red[n] = jnp.stack([joined[2 * l + 1][pos:pos + rows] for l in range(depth)])
        pos += rows
    red.update(zip(EXACT, _unpack(joined[-2], [w[n].shape for n in EXACT])))
    small_red = joined[-1]
    delta, new_m, new_v = {}, {}, {}
    for n in SHARDED:
        if w[n].shape[-1] % HEAD_DIM:
            to_view, from_view = (lambda a: jnp.transpose(a, (2, 0, 1))), (lambda a: jnp.transpose(a, (1, 2, 0)))
            delta[n], new_m[n], new_v[n], red[n] = [from_view(a) for a in adamw("adamw_" + n, to_view(w[n]), to_view(red[n]),
                                                                                to_view(m[n]), to_view(v[n]), echo_g=True)]
        else:
            delta[n], new_m[n], new_v[n] = adamw("adamw_" + n, w[n], red[n], m[n], v[n])
    pack_small = lambda d: _pack([d[n].reshape(-1) for n in REPLICATED], row_unit=32)
    small = adamw("adamw_small", pack_small(w), small_red, pack_small(m), pack_small(v))
    shapes = [w[n].shape for n in REPLICATED]
    red.update(zip(REPLICATED, _unpack(small_red, shapes)))
    for d, buf in zip((delta, new_m, new_v), small):
        d.update(zip(REPLICATED, _unpack(buf, shapes)))
    total_loss = lax.psum(loss[0, 0], ("x", "y", "c"))
    return (total_loss, grad_x, *[red[n] for n in WEIGHTS], *[delta[n] for n in WEIGHTS], *[new_m[n] for n in WEIGHTS],
            *[new_v[n] for n in WEIGHTS])
```

```python
import functools
import math

import jax
import jax.numpy as jnp
from jax import lax
from jax.experimental import pallas as pl
from jax.experimental.pallas import tpu as pltpu

F32 = jnp.float32
BF16 = jnp.bfloat16

N_META = 16
RMS_EPS = 1e-6
L2_EPS = 1e-6
CONV_K = 4
D_MODEL = 1024
HEAD_DIM = 128
N_HEADS = 8
CHUNK = 64
SB_BLOCK = 128
PAD = SB_BLOCK - N_META
SSM_INNER = 2048
SSM_P = 64
SSM_HEADS = 32
SSM_GROUPS = 2
SSM_HG = SSM_HEADS // SSM_GROUPS
SSM_N = 128
VMEM_LIMIT = 56 * 1024 * 1024

def _dims(mode, ndim):
    lhs, rhs = {"nn": (1, 0), "nt": (1, 1), "tn": (0, 0)}[mode]
    off = ndim - 2
    return (((lhs + off,), (rhs + off,)), (tuple(range(off)), tuple(range(off))))


def _dot(a, b, mode):
    return lax.dot_general(a, b, _dims(mode, a.ndim), preferred_element_type=F32)


def _halves(a):
    hi = a.astype(BF16)
    return hi, (a - hi.astype(F32)).astype(BF16)


def _mm_raw(a, b, mode, kind):
    if kind == "bf16":
        return _dot(a.astype(BF16), b.astype(BF16), mode)
    if kind == "lhs01":
        hi, lo = _halves(b)
        a = a.astype(BF16)
        return _dot(a, hi, mode) + _dot(a, lo, mode)
    if kind == "rhs01":
        hi, lo = _halves(a)
        b = b.astype(BF16)
        return _dot(hi, b, mode) + _dot(lo, b, mode)
    a_hi, a_lo = _halves(a)
    b_hi, b_lo = _halves(b)
    return _dot(a_hi, b_hi, mode) + (_dot(a_hi, b_lo, mode) + _dot(a_lo, b_hi, mode))


@functools.partial(jax.custom_vjp, nondiff_argnums=(2, 3))
def _mm(a, b, mode="nn", kind="bf16"):
    return _mm_raw(a, b, mode, kind)


def _mm_fwd(a, b, mode, kind):
    return _mm_raw(a, b, mode, kind), (a, b)


def _mm_bwd(mode, kind, res, g):
    a, b = res
    if kind == "lhs01":
        return jnp.zeros_like(a), _mm_raw(a, g, {"nn": "tn", "tn": "nn"}[mode], "lhs01")
    if kind == "rhs01":
        return _mm_raw(g, b, {"nn": "nt", "nt": "nn"}[mode], "rhs01"), jnp.zeros_like(b)
    if mode == "nn":
        return _mm_raw(g, b, "nt", kind), _mm_raw(a, g, "tn", kind)
    if mode == "nt":
        return _mm_raw(g, b, "nn", kind), _mm_raw(g, a, "tn", kind)
    return _mm_raw(b, g, "nt", kind), _mm_raw(a, g, "nn", kind)


_mm.defvjp(_mm_fwd, _mm_bwd)


def _iota2(shape, axis):
    return lax.broadcasted_iota(jnp.int32, shape, axis)


def _inv_unit_lower_raw(m):
    size = m.shape[-1]
    eye = (_iota2((size, size), 0) == _iota2((size, size), 1)).astype(F32)
    n = -m
    t = eye + n
    p = n
    steps = int(math.log2(size)) - 1
    for _ in range(steps):
        p = _mm_raw(p, p, "nn", "x3")
        t = t + _mm_raw(t, p, "nn", "x3")
    return t


@jax.custom_vjp
def _inv_unit_lower(m):
    return _inv_unit_lower_raw(m)


def _inv_fwd(m):
    t = _inv_unit_lower_raw(m)
    return t, t


def _inv_bwd(t, g):
    return (-_mm_raw(_mm_raw(t, g, "tn", "x3"), t, "nt", "x3"),)


_inv_unit_lower.defvjp(_inv_fwd, _inv_bwd)


@jax.custom_vjp
def _inv_known(m, t):
    return t


_inv_known.defvjp(lambda m, t: (t, t), lambda t, g: (_inv_bwd(t, g)[0], jnp.zeros_like(t)))


def _safe_decay(col, row, keep):
    return jnp.where(keep, jnp.exp(jnp.where(keep, col - row, 0.0)), 0.0)


def _col_to_row(col):
    n = col.shape[-2]
    eye = _iota2((n, n), 0) == _iota2((n, n), 1)
    return jnp.sum(jnp.where(eye, col, 0.0), axis=-2, keepdims=True)


def _cumsum_col(col):
    n = col.shape[-2]
    li, si = _iota2((n, n), 0), _iota2((n, n), 1)
    row = _col_to_row(col)
    c_col = jnp.sum(jnp.where(li >= si, row, 0.0), axis=-1, keepdims=True)
    c_row = jnp.sum(jnp.where(li <= si, col, 0.0), axis=-2, keepdims=True)
    return c_col, c_row


def _gdn_chunk(q, k, v, g, beta, state, t_known=None):
    cl = q.shape[-2]
    li, si = _iota2((cl, cl), 0), _iota2((cl, cl), 1)
    gc_col, gc_row = _cumsum_col(g)
    g_last = jnp.sum(g, axis=-2, keepdims=True)
    dec_strict = _safe_decay(gc_col, gc_row, li > si)
    dec_incl = _safe_decay(gc_col, gc_row, li >= si)
    e_gc = jnp.exp(gc_col)
    qs = q * (HEAD_DIM ** -0.5)
    kb = k * beta
    m = _mm(kb, k, "nt") * dec_strict
    t_inv = _inv_unit_lower(m) if t_known is None else _inv_known(m, t_known)
    u = _mm(t_inv, v * beta)
    w = _mm(t_inv, kb * e_gc)
    a_qk = _mm(qs, k, "nt") * dec_incl
    q_dec = qs * e_gc
    k_end = k * jnp.exp(g_last - gc_col)
    v_new = u - _mm(w, state)
    o = _mm(q_dec, state) + _mm(a_qk, v_new)
    new_state = state * jnp.exp(g_last) + _mm(k_end, v_new, "tn")
    return o, new_state, t_inv


def _gdn_operands(qkv_ref, gt):
    nh, width = N_HEADS, N_HEADS * HEAD_DIM
    heads = lambda off: jnp.stack([qkv_ref[:, off + h * HEAD_DIM:off + (h + 1) * HEAD_DIM] for h in range(nh)])
    cols = lambda off: jnp.stack([gt[:, off + h:off + h + 1] for h in range(nh)])
    return heads(0), heads(width), heads(2 * width), cols(nh), cols(0)


def gdn_fwd(qkv, gates):
    lp = qkv.shape[0]
    nh = N_HEADS
    nc = lp // CHUNK
    width = nh * HEAD_DIM

    def body(qkv_ref, gt_ref, o_ref, s_ref, t_ref, state):
        @pl.when(pl.program_id(0) == 0)
        def _():
            state[...] = jnp.zeros_like(state)

        s_in = state[...]
        s_ref[0] = s_in
        o, s_new, t_inv = _gdn_chunk(*_gdn_operands(qkv_ref, gt_ref[...]), s_in)
        t_ref[0] = t_inv
        for h in range(nh):
            o_ref[:, h * HEAD_DIM:(h + 1) * HEAD_DIM] = o[h]
        state[...] = s_new

    return pl.pallas_call(
        body, name="gdn_fwd", grid=(nc,),
        in_specs=[pl.BlockSpec((CHUNK, 3 * width), lambda c: (c, 0)), pl.BlockSpec((CHUNK, HEAD_DIM), lambda c: (c, 0))],
        out_specs=[pl.BlockSpec((CHUNK, width), lambda c: (c, 0)), pl.BlockSpec((1, nh, HEAD_DIM, HEAD_DIM), lambda c: (c, 0, 0, 0)),
                   pl.BlockSpec((1, nh, CHUNK, CHUNK), lambda c: (c, 0, 0, 0))],
        out_shape=[jax.ShapeDtypeStruct((lp, width), F32), jax.ShapeDtypeStruct((nc, nh, HEAD_DIM, HEAD_DIM), F32),
                   jax.ShapeDtypeStruct((nc, nh, CHUNK, CHUNK), F32)],
        scratch_shapes=[pltpu.VMEM((nh, HEAD_DIM, HEAD_DIM), F32)],
        compiler_params=pltpu.CompilerParams(dimension_semantics=("arbitrary",), vmem_limit_bytes=VMEM_LIMIT),
    )(qkv, gates)


def gdn_bwd(qkv, gates, states, t_invs, d_o):
    lp = qkv.shape[0]
    nh = N_HEADS
    nc = lp // CHUNK
    width = nh * HEAD_DIM

    def body(qkv_ref, gt_ref, s_ref, t_ref, do_ref, dqkv_ref, dgt_ref, d_state):
        @pl.when(pl.program_id(0) == 0)
        def _():
            d_state[...] = jnp.zeros_like(d_state)

        t_known = t_ref[0]
        _, pull = jax.vjp(lambda *xs: _gdn_chunk(*xs, t_known=t_known)[:2], *_gdn_operands(qkv_ref, gt_ref[...]), s_ref[0])
        d_o = jnp.stack([do_ref[:, h * HEAD_DIM:(h + 1) * HEAD_DIM] for h in range(nh)])
        dq, dk, dv, dg, db, ds = pull((d_o, d_state[...]))
        lane = _iota2((CHUNK, HEAD_DIM), 1)
        d_gt = jnp.zeros((CHUNK, HEAD_DIM), F32)
        for h in range(nh):
            for part, val in enumerate((dq, dk, dv)):
                dqkv_ref[:, part * width + h * HEAD_DIM:part * width + (h + 1) * HEAD_DIM] = val[h]
            d_gt = d_gt + jnp.where(lane == h, db[h], 0.0) + jnp.where(lane == nh + h, dg[h], 0.0)
        dgt_ref[...] = d_gt
        d_state[...] = ds

    rev = lambda c: (nc - 1 - c, 0)
    return pl.pallas_call(
        body, name="gdn_bwd", grid=(nc,),
        in_specs=[pl.BlockSpec((CHUNK, 3 * width), rev), pl.BlockSpec((CHUNK, HEAD_DIM), rev),
                  pl.BlockSpec((1, nh, HEAD_DIM, HEAD_DIM), lambda c: (nc - 1 - c, 0, 0, 0)),
                  pl.BlockSpec((1, nh, CHUNK, CHUNK), lambda c: (nc - 1 - c, 0, 0, 0)), pl.BlockSpec((CHUNK, width), rev)],
        out_specs=[pl.BlockSpec((CHUNK, 3 * width), rev), pl.BlockSpec((CHUNK, HEAD_DIM), rev)],
        out_shape=[jax.ShapeDtypeStruct((lp, 3 * width), F32), jax.ShapeDtypeStruct((lp, HEAD_DIM), F32)],
        scratch_shapes=[pltpu.VMEM((nh, HEAD_DIM, HEAD_DIM), F32)],
        compiler_params=pltpu.CompilerParams(dimension_semantics=("arbitrary",), vmem_limit_bytes=VMEM_LIMIT),
    )(qkv, gates, states, t_invs, d_o)


def _head_expand():
    width = SSM_HG * SSM_P
    return (_iota2((SSM_HG, width), 1) // SSM_P == _iota2((SSM_HG, width), 0)).astype(F32)


def _ssd_chunk(x, b, c, dt, la, state):
    cl = x.shape[0]
    li, si = _iota2((cl, cl), 0), _iota2((cl, cl), 1)
    causal = li >= si
    expand = _head_expand()
    tri = causal.astype(F32)
    xs = x * _mm(dt, expand, "nn", "rhs01")
    la_x = _mm(la, expand, "nn", "rhs01")
    cs_x = _mm(tri, la_x, "nn", "lhs01")
    last_x = jnp.sum(la_x, axis=0, keepdims=True)
    cs = _mm(tri, la, "nn", "lhs01")
    scores = _mm(c, b, "nt")
    head_id = _iota2((1, SSM_HG), 1)
    per_tile = HEAD_DIM // SSM_P
    tile_head = _iota2((1, HEAD_DIM), 1) // SSM_P
    within = []
    for t in range(SSM_HG // per_tile):
        xs_t = xs[:, t * HEAD_DIM:(t + 1) * HEAD_DIM]
        acc = jnp.zeros((cl, HEAD_DIM), F32)
        for hh in range(per_tile):
            cs_col = jnp.sum(jnp.where(head_id == t * per_tile + hh, cs, 0.0), axis=1, keepdims=True)
            decay = _safe_decay(cs_col, _col_to_row(cs_col), causal)
            acc = acc + _mm(scores * decay, jnp.where(tile_head == hh, xs_t, 0.0))
        within.append(acc)
    y = _mm(c, state) * jnp.exp(cs_x) + jnp.concatenate(within, axis=1)
    new_state = state * jnp.exp(last_x) + _mm(b, xs * jnp.exp(last_x - cs_x), "tn")
    return y, new_state


GATE_DT = 16


def _place_lanes(v, lo):
    n = v.shape[1]
    sel = (_iota2((n, HEAD_DIM), 1) == _iota2((n, HEAD_DIM), 0) + lo).astype(F32)
    return _mm_raw(v, sel, "nn", "rhs01")


def ssd_fwd(xbc, first, second):
    lp = xbc.shape[0]
    nc = lp // CHUNK
    width = SSM_HG * SSM_P
    b_off, c_off = SSM_INNER, SSM_INNER + SSM_GROUPS * SSM_N

    def body(x_ref, f_ref, s2_ref, y_ref, s_ref, state):
        @pl.when(pl.program_id(0) == 0)
        def _():
            state[...] = jnp.zeros_like(state)

        f, s2 = f_ref[...], s2_ref[...]
        for g in range(SSM_GROUPS):
            lo = GATE_DT + g * SSM_HG
            s_in = state[g]
            s_ref[0, g] = s_in
            y, s_new = _ssd_chunk(x_ref[:, g * width:(g + 1) * width], x_ref[:, b_off + g * SSM_N:b_off + (g + 1) * SSM_N],
                                  x_ref[:, c_off + g * SSM_N:c_off + (g + 1) * SSM_N], f[:, lo:lo + SSM_HG], s2[:, lo:lo + SSM_HG], s_in)
            y_ref[:, g * width:(g + 1) * width] = y
            state[g] = s_new

    row = lambda cols: pl.BlockSpec((CHUNK, cols), lambda k: (k, 0))
    return pl.pallas_call(
        body, name="ssd_fwd", grid=(nc,),
        in_specs=[row(xbc.shape[1]), row(HEAD_DIM), row(HEAD_DIM)],
        out_specs=[row(SSM_INNER), pl.BlockSpec((1, SSM_GROUPS, SSM_N, width), lambda k: (k, 0, 0, 0))],
        out_shape=[jax.ShapeDtypeStruct((lp, SSM_INNER), F32), jax.ShapeDtypeStruct((nc, SSM_GROUPS, SSM_N, width), F32)],
        scratch_shapes=[pltpu.VMEM((SSM_GROUPS, SSM_N, width), F32)],
        compiler_params=pltpu.CompilerParams(dimension_semantics=("arbitrary",), vmem_limit_bytes=VMEM_LIMIT),
    )(xbc, first, second)


def ssd_bwd(xbc, first, second, states, d_y, d_xh):
    lp = xbc.shape[0]
    nc = lp // CHUNK
    width = SSM_HG * SSM_P
    b_off, c_off = SSM_INNER, SSM_INNER + SSM_GROUPS * SSM_N

    def body(x_ref, f_ref, s2_ref, s_ref, dy_ref, dxh_ref, dx_ref, df_ref, ds2_ref, d_state):
        @pl.when(pl.program_id(0) == 0)
        def _():
            d_state[...] = jnp.zeros_like(d_state)

        f, s2 = f_ref[...], s2_ref[...]
        d_f = jnp.zeros((CHUNK, HEAD_DIM), F32)
        d_s2 = jnp.zeros((CHUNK, HEAD_DIM), F32)
        for g in range(SSM_GROUPS):
            lo = GATE_DT + g * SSM_HG
            x_l = slice(g * width, (g + 1) * width)
            b_l = slice(b_off + g * SSM_N, b_off + (g + 1) * SSM_N)
            c_l = slice(c_off + g * SSM_N, c_off + (g + 1) * SSM_N)
            _, pull = jax.vjp(_ssd_chunk, x_ref[:, x_l], x_ref[:, b_l], x_ref[:, c_l], f[:, lo:lo + SSM_HG], s2[:, lo:lo + SSM_HG],
                              s_ref[0, g])
            dx, db, dc, ddt, dla, ds = pull((dy_ref[:, x_l], d_state[g]))
            dx_ref[:, x_l] = dx + dxh_ref[:, x_l]
            dx_ref[:, b_l] = db
            dx_ref[:, c_l] = dc
            d_f = d_f + _place_lanes(ddt, lo)
            d_s2 = d_s2 + _place_lanes(dla, lo)
            d_state[g] = ds
        df_ref[...] = d_f
        ds2_ref[...] = d_s2

    row = lambda cols: pl.BlockSpec((CHUNK, cols), lambda k: (nc - 1 - k, 0))
    gate_shape = jax.ShapeDtypeStruct((lp, HEAD_DIM), F32)
    return pl.pallas_call(
        body, name="ssd_bwd", grid=(nc,),
        in_specs=[row(xbc.shape[1]), row(HEAD_DIM), row(HEAD_DIM),
                  pl.BlockSpec((1, SSM_GROUPS, SSM_N, width), lambda k: (nc - 1 - k, 0, 0, 0)), row(SSM_INNER), row(SSM_INNER)],
        out_specs=[row(xbc.shape[1]), row(HEAD_DIM), row(HEAD_DIM)],
        out_shape=[jax.ShapeDtypeStruct(xbc.shape, F32), gate_shape, gate_shape],
        scratch_shapes=[pltpu.VMEM((SSM_GROUPS, SSM_N, width), F32)],
        compiler_params=pltpu.CompilerParams(dimension_semantics=("arbitrary",), vmem_limit_bytes=VMEM_LIMIT),
    )(xbc, first, second, states, d_y, d_xh)


SB_QROWS = 544


def _mm_tri(a, tri):
    return _mm_raw(a, tri.astype(BF16), "nn", "rhs01")


def _sb_scores(q_scaled, kb, row0, j):
    shape = (q_scaled.shape[0], SB_BLOCK)
    z = _mm_raw(q_scaled, kb, "nt", "bf16")
    q_pos = row0 + _iota2(shape, 0)
    k_pos = j * SB_BLOCK + _iota2(shape, 1)
    valid = (k_pos < q_pos) & (k_pos >= PAD)
    sp = jnp.maximum(z, 0.0) + jnp.log(1.0 + jnp.exp(-jnp.abs(z)))
    lk = jnp.where(valid, -sp, 0.0)
    return z, sp, valid, lk


SB_DEAD = -110.0


def sb_fwd(src, offs, ride=None):
    lp = src.shape[0]
    nh = N_HEADS
    qb = _pick(lp, SB_QROWS, 8)
    scale = HEAD_DIM ** -0.5
    blk = SB_BLOCK

    n_in = len(ride.ins) if ride else 0
    n_out = len(ride.out_shapes) if ride else 0

    def body(*refs):
        q_ref, k_ref, v_ref = refs[:3]
        o_ref = refs[3 + n_in]
        ride_refs = (refs[3:3 + n_in], refs[4 + n_in:4 + n_in + n_out], refs[4 + n_in + n_out:])
        i = pl.program_id(1)
        if ride:
            @pl.when((pl.program_id(0) == 0) & (i == 0))
            def _():
                ride.start(*ride_refs)
        q_scaled = q_ref[...] * scale
        upper = _iota2((blk, blk), 0) > _iota2((blk, blk), 1)
        n_blocks = ((i + 1) * qb + blk - 1) // blk

        def live(state):
            it, _, c = state
            return (it < n_blocks) & (jnp.max(c) > SB_DEAD)

        def step(state):
            it, acc, c = state
            j = n_blocks - 1 - it
            rows = pl.ds(pl.multiple_of(j * blk, blk), blk)
            z, sp, valid, lk = _sb_scores(q_scaled, k_ref[rows, :], i * qb, j)
            later = _mm_tri(lk, upper) + c
            w = jnp.where(valid, jnp.exp(z - sp + later), 0.0)
            acc = acc + _mm_raw(w, v_ref[rows, :], "nn", "bf16")
            return it + 1, acc, c + jnp.sum(lk, axis=1, keepdims=True)

        _, acc, _ = lax.while_loop(live, step, (jnp.int32(0), jnp.zeros((qb, HEAD_DIM), F32), jnp.zeros((qb, 1), F32)))
        o_ref[...] = acc
        if ride:
            @pl.when((pl.program_id(0) == nh - 1) & (i == lp // qb - 1))
            def _():
                ride.finish(*ride_refs)

    qspec = pl.BlockSpec((qb, HEAD_DIM), lambda h, i: (i, offs[0] + h))
    kspec = pl.BlockSpec((lp, HEAD_DIM), lambda h, i: (0, offs[1] + h))
    vspec = pl.BlockSpec((lp, HEAD_DIM), lambda h, i: (0, offs[2] + h))
    ospec = pl.BlockSpec((qb, HEAD_DIM), lambda h, i: (i, h))
    return pl.pallas_call(
        body, name="sb_fwd", grid=(nh, lp // qb), in_specs=[qspec, kspec, vspec] + [ANY] * n_in, out_specs=[ospec] + [ANY] * n_out,
        out_shape=[jax.ShapeDtypeStruct((lp, nh * HEAD_DIM), F32)] + (ride.out_shapes if ride else []),
        scratch_shapes=ride.sems if ride else [],
        input_output_aliases={3 + k: 1 + k for k in range(n_in)} if ride and ride.alias else {},
        compiler_params=pltpu.CompilerParams(dimension_semantics=("arbitrary", "arbitrary"), vmem_limit_bytes=VMEM_LIMIT),
    )(src, src, src, *(ride.ins if ride else []))


def sb_bwd(src, offs, d_o, ride=None):
    lp = src.shape[0]
    nh = N_HEADS
    qb = _pick(lp, SB_QROWS, 8)
    scale = HEAD_DIM ** -0.5
    blk = SB_BLOCK

    n_in = len(ride.ins) if ride else 0
    n_out = len(ride.out_shapes) if ride else 0

    def body(*refs):
        q_ref, k_ref, v_ref, do_ref = refs[:4]
        dq_ref, dk_out, dv_out = refs[4 + n_in:7 + n_in]
        ride_refs = (refs[4:4 + n_in], refs[7 + n_in:7 + n_in + n_out], refs[9 + n_in + n_out:])
        dk_ref, dv_ref = refs[7 + n_in + n_out:9 + n_in + n_out]
        i = pl.program_id(1)
        if ride:
            @pl.when((pl.program_id(0) == 0) & (i == 0))
            def _():
                ride.start(*ride_refs)

        @pl.when(i == 0)
        def _():
            dk_ref[...] = jnp.zeros_like(dk_ref)
            dv_ref[...] = jnp.zeros_like(dv_ref)

        q_scaled = q_ref[...] * scale
        d_out = do_ref[...]
        lower_incl = _iota2((blk, blk), 0) <= _iota2((blk, blk), 1)
        lower = _iota2((blk, blk), 0) < _iota2((blk, blk), 1)
        n_blocks = ((i + 1) * qb + blk - 1) // blk

        def live(state):
            it, c = state
            return (it < n_blocks) & (jnp.max(c) > SB_DEAD)

        def count(state):
            it, c = state
            rows = pl.ds(pl.multiple_of((n_blocks - 1 - it) * blk, blk), blk)
            _, _, _, lk = _sb_scores(q_scaled, k_ref[rows, :], i * qb, n_blocks - 1 - it)
            return it + 1, c + jnp.sum(lk, axis=1, keepdims=True)

        n_live, total = lax.while_loop(live, count, (jnp.int32(0), jnp.zeros((qb, 1), F32)))

        def step(j, carry):
            acc, cp, ep = carry
            rows = pl.ds(pl.multiple_of(j * blk, blk), blk)
            kb = k_ref[rows, :]
            vb = v_ref[rows, :]
            z, sp, valid, lk = _sb_scores(q_scaled, kb, i * qb, j)
            later = total - cp - _mm_tri(lk, lower_incl)
            w = jnp.where(valid, jnp.exp(z - sp + later), 0.0)
            e = w * _mm_raw(d_out, vb, "nt", "bf16")
            before = ep + _mm_tri(e, lower)
            dz = jnp.where(valid, e * jnp.exp(-sp) - before * jnp.exp(z - sp), 0.0)
            dk_ref[rows, :] += _mm_raw(dz, q_scaled, "tn", "bf16")
            dv_ref[rows, :] += _mm_raw(w, d_out, "tn", "bf16")
            acc = acc + _mm_raw(dz, kb, "nn", "bf16")
            return acc, cp + jnp.sum(lk, axis=1, keepdims=True), ep + jnp.sum(e, axis=1, keepdims=True)

        zero_col = jnp.zeros((qb, 1), F32)
        acc, _, _ = lax.fori_loop(n_blocks - n_live, n_blocks, step, (jnp.zeros((qb, HEAD_DIM), F32), zero_col, zero_col))
        dq_ref[...] = (acc * scale).astype(dq_ref.dtype)

        @pl.when(i == lp // qb - 1)
        def _():
            dk_out[...] = dk_ref[...].astype(dk_out.dtype)
            dv_out[...] = dv_ref[...].astype(dv_out.dtype)

        if ride:
            @pl.when((pl.program_id(0) == nh - 1) & (i == lp // qb - 1))
            def _():
                ride.finish(*ride_refs)

    qspec = pl.BlockSpec((qb, HEAD_DIM), lambda h, i: (i, offs[0] + h))
    kspec = pl.BlockSpec((lp, HEAD_DIM), lambda h, i: (0, offs[1] + h))
    vspec = pl.BlockSpec((lp, HEAD_DIM), lambda h, i: (0, offs[2] + h))
    ospec = pl.BlockSpec((qb, HEAD_DIM), lambda h, i: (i, h))
    fullspec = pl.BlockSpec((lp, HEAD_DIM), lambda h, i: (0, h))
    return pl.pallas_call(
        body, name="sb_bwd", grid=(nh, lp // qb), in_specs=[qspec, kspec, vspec, ospec] + [ANY] * n_in,
        out_specs=[ospec, fullspec, fullspec] + [ANY] * n_out,
        out_shape=[jax.ShapeDtypeStruct((lp, nh * HEAD_DIM), BF16)] * 3 + (ride.out_shapes if ride else []),
        scratch_shapes=[pltpu.VMEM((lp, HEAD_DIM), F32)] * 2 + (ride.sems if ride else []),
        input_output_aliases={4 + k: 3 + k for k in range(n_in)} if ride and ride.alias else {},
        compiler_params=pltpu.CompilerParams(dimension_semantics=("arbitrary", "arbitrary"), vmem_limit_bytes=VMEM_LIMIT),
    )(src, src, src, d_o, *(ride.ins if ride else []))


def _pick(n, target, unit):
    if n <= target:
        return n
    best = None
    for d in range(unit, target + 1, unit):
        if n % d == 0:
            best = d
    assert best is not None, (n, target, unit)
    return best


def matmul(a, b, mode="nn", *, name, bm=1088, bn=640, bk=2176, residual=None, out_dtype=F32, b_koff=0, ride=None, b_view=None):
    if mode == "nn":
        (m, k), n = a.shape, b.shape[-1]
    elif mode == "nt":
        (m, k), n = a.shape, b.shape[0]
    else:
        (k, m), n = a.shape, b.shape[1]
    if b_view is not None:
        rows, first, per_chip = b_view
        if mode == "nn":
            k, bk = 4 * per_chip * rows, rows
        else:
            assert mode == "nt"
            n, bn = 4 * per_chip * rows, rows
    assert b_koff == 0 or mode == "nt"
    bm = _pick(m, bm, 128 if mode == "tn" else 8)
    bn = _pick(n, bn, 128 if mode != "nt" else 8)
    bk = _pick(k, bk, 128 if mode != "tn" else 8)
    nk = k // bk
    n_plain = 2 if residual is None else 3
    n_in = len(ride.ins) if ride else 0
    n_out = len(ride.out_shapes) if ride else 0
    steps = (m // bm, n // bn, nk)

    def body(*refs):
        a_ref, b_ref = refs[:2]
        r_ref = None if residual is None else refs[2]
        o_ref = refs[n_plain + n_in]
        acc = refs[n_plain + n_in + 1 + n_out]
        ride_refs = (refs[n_plain:n_plain + n_in], refs[n_plain + n_in + 1:n_plain + n_in + 1 + n_out], refs[n_plain + n_in + 2 + n_out:])
        kk = pl.program_id(2)
        here = [pl.program_id(d) for d in range(3)]
        if ride:
            @pl.when((here[0] == 0) & (here[1] == 0) & (here[2] == 0))
            def _():
                ride.start(*ride_refs)
        part = _mm_raw(a_ref[...], b_ref[...], mode, "bf16")

        @pl.when(kk == 0)
        def _():
            acc[...] = part

        @pl.when(kk > 0)
        def _():
            acc[...] += part

        @pl.when(kk == nk - 1)
        def _():
            res = acc[...]
            if r_ref is not None:
                res = res + r_ref[...]
            o_ref[...] = res.astype(out_dtype)

        if ride:
            @pl.when((here[0] == steps[0] - 1) & (here[1] == steps[1] - 1) & (here[2] == steps[2] - 1))
            def _():
                ride.finish(*ride_refs)

    a_spec = pl.BlockSpec((bk, bm), lambda i, j, kk: (kk, i)) if mode == "tn" else pl.BlockSpec((bm, bk), lambda i, j, kk: (i, kk))
    b_spec = pl.BlockSpec((bn, bk), lambda i, j, kk: (j, b_koff + kk)) if mode == "nt" else pl.BlockSpec((bk, bn), lambda i, j, kk: (kk, j))
    if b_view is not None and mode == "nn":
        b_spec = pl.BlockSpec((None, bk, bn), lambda i, j, kk: (kk // per_chip, first + kk % per_chip, j))
    elif b_view is not None:
        b_spec = pl.BlockSpec((None, bn, bk), lambda i, j, kk: (j // per_chip, first + j % per_chip, kk))
    o_spec = pl.BlockSpec((bm, bn), lambda i, j, kk: (i, j))
    ins, specs = [a, b], [a_spec, b_spec]
    if residual is not None:
        ins.append(residual)
        specs.append(o_spec)
    res = pl.pallas_call(
        body, name=name, grid=steps, in_specs=specs + [ANY] * n_in, out_specs=[o_spec] + [ANY] * n_out,
        out_shape=[jax.ShapeDtypeStruct((m, n), out_dtype)] + (ride.out_shapes if ride else []),
        scratch_shapes=[pltpu.VMEM((bm, bn), F32)] + (ride.sems if ride else []),
        compiler_params=pltpu.CompilerParams(dimension_semantics=("arbitrary", "arbitrary", "arbitrary"), vmem_limit_bytes=VMEM_LIMIT),
    )(*ins, *(ride.ins if ride else []))
    return res if ride else res[0]


def _row_specs(rows, params, width, bm):
    row_specs = [pl.BlockSpec((bm, width), (lambda j, i, off=off: (i, off + j))) for _, off in rows]
    par_specs = [pl.BlockSpec((p.shape[0], width) if per_col else p.shape, ((lambda j, i: (0, j)) if per_col else (lambda j, i: (0, 0))))
                 for p, per_col in params]
    return row_specs, par_specs


def rowmap_fwd(name, fn, rows, params, n_out, *, width, ncol, bm, lp, out_dtypes=None):
    out_dtypes = out_dtypes or [F32] * n_out
    row_specs, par_specs = _row_specs(rows, params, width, bm)
    nr = len(rows)

    def body(*refs):
        ins, outs = refs[:nr + len(params)], refs[nr + len(params):]
        row_ids = pl.program_id(1) * bm + _iota2((bm, 1), 0)
        res = fn(row_ids, *[r[...].astype(F32) for r in ins])
        for o_ref, val in zip(outs, res):
            o_ref[...] = val.astype(o_ref.dtype)

    o_spec = pl.BlockSpec((bm, width), lambda j, i: (i, j))
    return pl.pallas_call(
        body, name=name, grid=(ncol, lp // bm), in_specs=row_specs + par_specs, out_specs=[o_spec] * n_out,
        out_shape=[jax.ShapeDtypeStruct((lp, ncol * width), dt) for dt in out_dtypes],
        compiler_params=pltpu.CompilerParams(dimension_semantics=("arbitrary", "arbitrary"), vmem_limit_bytes=VMEM_LIMIT),
    )(*[a for a, _ in rows], *[p for p, _ in params])


def rowmap_bwd(name, fn, rows, params, d_outs, *, width, ncol, bm, lp, d_row_dtypes=None):
    d_row_dtypes = d_row_dtypes or [F32] * len(rows)
    row_specs, par_specs = _row_specs(rows, params, width, bm)
    nr, npar, nout = len(rows), len(params), len(d_outs)

    def body(*refs):
        ins = refs[:nr + npar]
        dos = refs[nr + npar:nr + npar + nout]
        d_rows = refs[nr + npar + nout:nr + npar + nout + nr]
        d_pars = refs[nr + npar + nout + nr:]
        j, i = pl.program_id(0), pl.program_id(1)
        row_ids = i * bm + _iota2((bm, 1), 0)
        _, pull = jax.vjp(lambda *xs: tuple(fn(row_ids, *xs)), *[r[...].astype(F32) for r in ins])
        grads = pull(tuple(d[...].astype(F32) for d in dos))
        for ref, val in zip(d_rows, grads[:nr]):
            ref[...] = val.astype(ref.dtype)
        for ref, val, (_, per_col) in zip(d_pars, grads[nr:], params):
            first = (i == 0) if per_col else ((i == 0) & (j == 0))

            @pl.when(first)
            def _(ref=ref, val=val):
                ref[...] = val

            @pl.when(jnp.logical_not(first))
            def _(ref=ref, val=val):
                ref[...] += val

    o_spec = pl.BlockSpec((bm, width), lambda j, i: (i, j))
    res = pl.pallas_call(
        body, name=name, grid=(ncol, lp // bm), in_specs=row_specs + par_specs + [o_spec] * nout,
        out_specs=[o_spec] * nr + par_specs,
        out_shape=[jax.ShapeDtypeStruct((lp, ncol * width), dt) for dt in d_row_dtypes]
        + [jax.ShapeDtypeStruct(p.shape, F32) for p, _ in params],
        compiler_params=pltpu.CompilerParams(dimension_semantics=("arbitrary", "arbitrary"), vmem_limit_bytes=VMEM_LIMIT),
    )(*[a for a, _ in rows], *[p for p, _ in params], *d_outs)
    return res[:nr], res[nr:]


def _silu(x):
    return x * jax.nn.sigmoid(x)


def _softplus(x):
    return jnp.maximum(x, 0.0) + jnp.log(1.0 + jnp.exp(-jnp.abs(x)))


def _real_rows(row_ids):
    return (row_ids >= PAD).astype(F32)


def _f_rmsnorm(row_ids, h, g):
    return (h * lax.rsqrt(jnp.mean(h * h, axis=-1, keepdims=True) + RMS_EPS) * g,)


def _f_small_gates(row_ids, small, bias, a_log):
    lane = _iota2(small.shape, 1)
    t = small + bias
    sp = _softplus(t)
    coef = -jnp.exp(a_log)
    keep = _real_rows(row_ids)
    first = jnp.where(lane < 8, jax.nn.sigmoid(t), jnp.where(lane < 16, coef * sp, jnp.where(lane < 48, sp, 0.0)))
    second = jnp.where((lane >= 8) & (lane < 48), coef * sp, 0.0)
    return first * keep, second * keep


def _f_gate_silu(row_ids, o, z):
    return (o * _silu(z),)


def _f_head_norm_gate(row_ids, o, z, g):
    out = []
    for h in range(o.shape[1] // HEAD_DIM):
        oh = o[:, h * HEAD_DIM:(h + 1) * HEAD_DIM]
        out.append(oh * lax.rsqrt(jnp.mean(oh * oh, axis=-1, keepdims=True) + RMS_EPS) * g)
    return (jnp.concatenate(out, axis=1) * _silu(z),)


def _f_ssm_out(row_ids, y, xh, z, d_skip, g):
    t = (y + d_skip * xh) * _silu(z)
    return (t * lax.rsqrt(jnp.mean(t * t, axis=-1, keepdims=True) + RMS_EPS) * g,)


def _f_merge(row_ids, pa, pb, pc, ga, gb, gc):
    return (jax.nn.sigmoid(ga) * pa + jax.nn.sigmoid(gb) * pb + jax.nn.sigmoid(gc) * pc,)


def _shift_rows(x, s):
    s = s % x.shape[0]
    return x if s == 0 else pltpu.roll(x, s, 0)


def _conv_pre(x, w, b):
    pre = b
    for kk in range(CONV_K):
        pre = pre + w[kk:kk + 1, :] * _shift_rows(x, CONV_K - 1 - kk)
    return pre


def _conv_post(pre, l2_flag, keep):
    act = _silu(pre)
    nrm = act * lax.rsqrt(jnp.sum(act * act, axis=-1, keepdims=True) + L2_EPS)
    return (l2_flag * nrm + (1.0 - l2_flag) * act) * keep


def conv_fwd(name, src, col_off, w, b, n_l2):
    lp, ch = src.shape[0], w.shape[1]

    def body(x_ref, w_ref, b_ref, o_ref):
        l2_flag = (pl.program_id(0) < n_l2).astype(F32)
        keep = _real_rows(_iota2((lp, 1), 0))
        o_ref[...] = _conv_post(_conv_pre(x_ref[...], w_ref[...], b_ref[...]), l2_flag, keep)

    return pl.pallas_call(
        body, name=name, grid=(ch // HEAD_DIM,),
        in_specs=[pl.BlockSpec((lp, HEAD_DIM), lambda j: (0, col_off + j)), pl.BlockSpec((CONV_K, HEAD_DIM), lambda j: (0, j)),
                  pl.BlockSpec((1, HEAD_DIM), lambda j: (0, j))],
        out_specs=pl.BlockSpec((lp, HEAD_DIM), lambda j: (0, j)),
        out_shape=jax.ShapeDtypeStruct((lp, ch), F32),
        compiler_params=pltpu.CompilerParams(dimension_semantics=("arbitrary",), vmem_limit_bytes=VMEM_LIMIT),
    )(src, w, b)


def conv_bwd(name, src, col_off, w, b, n_l2, d_out):
    lp, ch = src.shape[0], w.shape[1]

    def body(x_ref, w_ref, b_ref, do_ref, dx_ref, dw_ref, db_ref):
        l2_flag = (pl.program_id(0) < n_l2).astype(F32)
        keep = _real_rows(_iota2((lp, 1), 0))
        x, wv = x_ref[...], w_ref[...]
        pre = _conv_pre(x, wv, b_ref[...])
        _, pull = jax.vjp(lambda p: _conv_post(p, l2_flag, keep), pre)
        (d_pre,) = pull(do_ref[...])
        dx = jnp.zeros_like(x)
        for kk in range(CONV_K):
            s = CONV_K - 1 - kk
            dx = dx + wv[kk:kk + 1, :] * _shift_rows(d_pre, -s)
            dw_ref[kk:kk + 1, :] = jnp.sum(d_pre * _shift_rows(x, s), axis=0, keepdims=True)
        dx_ref[...] = (dx * keep).astype(dx_ref.dtype)
        db_ref[...] = jnp.sum(d_pre, axis=0, keepdims=True)

    seq = pl.BlockSpec((lp, HEAD_DIM), lambda j: (0, j))
    wspec = pl.BlockSpec((CONV_K, HEAD_DIM), lambda j: (0, j))
    bspec = pl.BlockSpec((1, HEAD_DIM), lambda j: (0, j))
    return pl.pallas_call(
        body, name=name, grid=(ch // HEAD_DIM,),
        in_specs=[pl.BlockSpec((lp, HEAD_DIM), lambda j: (0, col_off + j)), wspec, bspec, seq],
        out_specs=[seq, wspec, bspec],
        out_shape=[jax.ShapeDtypeStruct((lp, ch), BF16), jax.ShapeDtypeStruct(w.shape, F32), jax.ShapeDtypeStruct(b.shape, F32)],
        compiler_params=pltpu.CompilerParams(dimension_semantics=("arbitrary",), vmem_limit_bytes=VMEM_LIMIT),
    )(src, w, b, d_out)


def loss_head(h, target, g):
    lp, d = h.shape
    bm = SB_BLOCK
    first = (PAD + N_META) // bm

    def body(h_ref, t_ref, g_ref, loss_ref, dh_ref, dg_ref):
        i = pl.program_id(0)
        keep = (i >= first).astype(F32)

        def f(hv, gv):
            y = hv * lax.rsqrt(jnp.mean(hv * hv, axis=-1, keepdims=True) + RMS_EPS) * gv
            err = y - t_ref[...]
            return 0.5 * jnp.sum(jnp.mean(err * err, axis=-1, keepdims=True), axis=0, keepdims=True) * keep

        val, pull = jax.vjp(f, h_ref[...], g_ref[...])
        dh, dg = pull(jnp.ones((1, 1), F32))
        dh_ref[...] = dh

        @pl.when(i == 0)
        def _():
            loss_ref[...] = val
            dg_ref[...] = dg

        @pl.when(i > 0)
        def _():
            loss_ref[...] += val
            dg_ref[...] += dg

    row = pl.BlockSpec((bm, d), lambda i: (i, 0))
    return pl.pallas_call(
        body, name="loss_head", grid=(lp // bm,),
        in_specs=[row, pl.BlockSpec((bm, d), lambda i: (jnp.maximum(i - first, 0), 0)), pl.BlockSpec((1, d), lambda i: (0, 0))],
        out_specs=[pl.BlockSpec((1, 1), lambda i: (0, 0)), row, pl.BlockSpec((1, d), lambda i: (0, 0))],
        out_shape=[jax.ShapeDtypeStruct((1, 1), F32), jax.ShapeDtypeStruct((lp, d), F32), jax.ShapeDtypeStruct((1, d), F32)],
        compiler_params=pltpu.CompilerParams(dimension_semantics=("arbitrary",)),
    )(h, target, g)


ADAM_LR, ADAM_B1, ADAM_B2, ADAM_EPS, ADAM_WD, ADAM_STEP = 0.001, 0.9, 0.999, 1e-08, 0.01, 10


def adamw(name, w, g, m, v, echo_g=False):
    lead = w.shape[:-2]
    rows, cols = w.shape[-2:]
    br = _pick(rows, 128, 8)
    n_out = 4 if echo_g else 3

    def body(w_ref, g_ref, m_ref, v_ref, d_ref, nm_ref, nv_ref, *echo):
        gv = g_ref[...]
        nm = ADAM_B1 * m_ref[...] + (1.0 - ADAM_B1) * gv
        nv = ADAM_B2 * v_ref[...] + (1.0 - ADAM_B2) * (gv * gv)
        m_hat = nm / (1.0 - ADAM_B1 ** ADAM_STEP)
        v_hat = nv / (1.0 - ADAM_B2 ** ADAM_STEP)
        d_ref[...] = -ADAM_LR * (m_hat / (jnp.sqrt(v_hat) + ADAM_EPS) + ADAM_WD * w_ref[...])
        nm_ref[...] = nm
        nv_ref[...] = nv
        for e_ref in echo:
            e_ref[...] = gv

    if lead and rows <= 8:
        bl = _pick(lead[0], 32, 1)
        spec = pl.BlockSpec((bl, rows, cols), lambda s: (s, 0, 0))
        grid = (lead[0] // bl,)
    elif lead:
        spec = pl.BlockSpec((None, br, cols), lambda s, i: (s, i, 0))
        grid = (lead[0], rows // br)
    else:
        spec = pl.BlockSpec((br, cols), lambda i: (i, 0))
        grid = (rows // br,)
    return pl.pallas_call(
        body, name=name, grid=grid, in_specs=[spec] * 4, out_specs=[spec] * n_out,
        out_shape=[jax.ShapeDtypeStruct(w.shape, F32)] * n_out,
        compiler_params=pltpu.CompilerParams(dimension_semantics=("arbitrary",) * len(grid), vmem_limit_bytes=VMEM_LIMIT),
    )(w, g, m, v)


MESH = pl.DeviceIdType.MESH
ANY = pl.BlockSpec(memory_space=pl.ANY)
D2D_PIECES = 16
ICI_PIECES = 4


def _place():
    x, y, c = lax.axis_index("x"), lax.axis_index("y"), lax.axis_index("c")
    return x, y, c, [(1 - x, y), (x, 1 - y), (1 - x, 1 - y)]


def _pieces(rows, n, unit):
    per = -(-rows // (n * unit)) * unit
    return [(s, min(per, rows - s)) for s in range(0, rows, per)]


def _row_unit(dtype):
    return 16 if dtype == BF16 else 8


def _scalar(v):
    return jnp.reshape(v, (1,)).astype(jnp.int32)


def place_shard(name, pack):
    rows, cols = pack.shape
    br = _pick(rows, 256, 16)

    def body(m_ref, p_ref, o_ref):
        o_ref[...] = p_ref[...]

    return pl.pallas_call(
        body, name=name,
        grid_spec=pltpu.PrefetchScalarGridSpec(
            num_scalar_prefetch=1, grid=(rows // br,),
            in_specs=[pl.BlockSpec((br, cols), lambda i, m: (i, 0))],
            out_specs=pl.BlockSpec((None, br, cols), lambda i, m: (m[0], i, 0))),
        out_shape=jax.ShapeDtypeStruct((4, rows, cols), pack.dtype),
        compiler_params=pltpu.CompilerParams(dimension_semantics=("arbitrary",), vmem_limit_bytes=VMEM_LIMIT),
    )(_scalar(2 * lax.axis_index("x") + lax.axis_index("y")), pack)


class Ride:
    def __init__(self, ins, out_shapes, alias, sems, start, finish):
        self.ins, self.out_shapes, self.alias, self.sems, self.start, self.finish = list(ins), out_shapes, alias, sems, start, finish


def _gather_parts(o_refs, send_sems, recv_sems):
    x, y, c, chips = _place()
    mine = 2 * x + y

    def half_rows(b, which, start=0, size=None):
        half = o_refs[b].shape[1] // 2
        return pl.ds(pl.multiple_of(which * half + start, _row_unit(o_refs[b].dtype)), half if size is None else size)

    def remote(b, k, slot, rws, to):
        piece = o_refs[b].at[slot, rws, :]
        return pltpu.make_async_remote_copy(src_ref=piece, dst_ref=piece, send_sem=send_sems.at[b, k], recv_sem=recv_sems.at[b, k],
                                            device_id=to, device_id_type=MESH)

    return x, y, c, chips, mine, half_rows, remote


def _gather_start(o_refs, send_sems, recv_sems):
    x, y, c, chips, mine, half_rows, remote = _gather_parts(o_refs, send_sems, recv_sems)
    for b, o_ref in enumerate(o_refs):
        for j, (cx, cy) in enumerate(chips):
            for start, size in _pieces(o_ref.shape[1] // 2, ICI_PIECES, _row_unit(o_ref.dtype)):
                remote(b, j, mine, half_rows(b, c, start, size), (cx, cy, c)).start()


def _gather_finish(o_refs, send_sems, recv_sems):
    x, y, c, chips, mine, half_rows, remote = _gather_parts(o_refs, send_sems, recv_sems)
    sends = []
    for b, o_ref in enumerate(o_refs):
        for j, (cx, cy) in enumerate(chips):
            slot = 2 * cx + cy
            sends.append(remote(b, j, mine, half_rows(b, c), (cx, cy, c)))
            remote(b, j, slot, half_rows(b, c), (cx, cy, c)).wait_recv()
            for start, size in _pieces(o_ref.shape[1] // 2, D2D_PIECES, _row_unit(o_ref.dtype)):
                remote(b, 3 + j, slot, half_rows(b, c, start, size), (x, y, 1 - c)).start()
            sends.append(remote(b, 3 + j, slot, half_rows(b, c), (x, y, 1 - c)))
    for b in range(len(o_refs)):
        for j, (cx, cy) in enumerate(chips):
            remote(b, 3 + j, 2 * cx + cy, half_rows(b, 1 - c), (x, y, 1 - c)).wait_recv()
    for cp in sends:
        cp.wait_send()


def gather_ride(placed):
    n = len(placed)
    return Ride(placed, [jax.ShapeDtypeStruct(p.shape, p.dtype) for p in placed], True,
                [pltpu.SemaphoreType.DMA((n, 6)), pltpu.SemaphoreType.DMA((n, 6))],
                lambda ins, outs, sems: _gather_start(outs, *sems), lambda ins, outs, sems: _gather_finish(outs, *sems))


def gather_shards(name, placed):
    n = len(placed)

    def body(*refs):
        o_refs, sems = refs[n:2 * n], refs[2 * n:]
        _gather_start(o_refs, *sems)
        _gather_finish(o_refs, *sems)

    return pl.pallas_call(
        body, name=name, in_specs=[ANY] * n, out_specs=[ANY] * n,
        out_shape=[jax.ShapeDtypeStruct(p.shape, p.dtype) for p in placed],
        input_output_aliases={i: i for i in range(n)},
        scratch_shapes=[pltpu.SemaphoreType.DMA((n, 6)), pltpu.SemaphoreType.DMA((n, 6))],
    )(*placed)


def _split_copies(g_refs, t_refs, send_sems, recv_sems, start):
    x, y, c, _ = _place()
    waits = []
    for b, (g_ref, t_ref) in enumerate(zip(g_refs, t_refs)):
        half = g_ref.shape[2]

        def copy(slots, first, size):
            rws = pl.ds(first, size)
            return pltpu.make_async_remote_copy(src_ref=g_ref.at[slots, 1 - c, rws, :], dst_ref=t_ref.at[slots, rws, :],
                                                send_sem=send_sems.at[b], recv_sem=recv_sems.at[b], device_id=(x, y, 1 - c),
                                                device_id_type=MESH)

        if start:
            for s in range(4):
                for first, size in _pieces(half, D2D_PIECES // 4, _row_unit(g_ref.dtype)):
                    copy(s, first, size).start()
        else:
            waits.append(copy(slice(None), 0, half))
    return waits


def _split_finish(g_refs, t_refs, send_sems, recv_sems):
    for cp in _split_copies(g_refs, t_refs, send_sems, recv_sems, False):
        cp.wait()


def _split_shapes(bufs):
    return [jax.ShapeDtypeStruct((4,) + g.shape[2:], g.dtype) for g in bufs]


def split_ride(bufs):
    n = len(bufs)
    return Ride(bufs, _split_shapes(bufs), False, [pltpu.SemaphoreType.DMA((n,)), pltpu.SemaphoreType.DMA((n,))],
                lambda ins, outs, sems: _split_copies(ins, outs, *sems, True), lambda ins, outs, sems: _split_finish(ins, outs, *sems))


def pair_split(name, bufs):
    n = len(bufs)

    def body(*refs):
        g_refs, t_refs, sems = refs[:n], refs[n:2 * n], refs[2 * n:]
        _split_copies(g_refs, t_refs, *sems, True)
        _split_finish(g_refs, t_refs, *sems)

    return pl.pallas_call(
        body, name=name, in_specs=[ANY] * n, out_specs=[ANY] * n, out_shape=_split_shapes(bufs),
        scratch_shapes=[pltpu.SemaphoreType.DMA((n,)), pltpu.SemaphoreType.DMA((n,))],
    )(*bufs)


def pair_add(name, g, theirs, transit):
    _, _, half, cols = g.shape
    br = _pick(half, 128, 16)

    def body(c_ref, g_ref, t_ref, o_ref):
        o_ref[...] = (g_ref[...].astype(F32) + t_ref[...].astype(F32)).astype(transit)

    blk = (4, br, cols)
    return pl.pallas_call(
        body, name=name,
        grid_spec=pltpu.PrefetchScalarGridSpec(
            num_scalar_prefetch=1, grid=(half // br,),
            in_specs=[pl.BlockSpec((4, None, br, cols), lambda i, c: (0, c[0], i, 0)), pl.BlockSpec(blk, lambda i, c: (0, i, 0))],
            out_specs=pl.BlockSpec(blk, lambda i, c: (0, i, 0))),
        out_shape=jax.ShapeDtypeStruct((4, half, cols), transit),
        compiler_params=pltpu.CompilerParams(dimension_semantics=("arbitrary",), vmem_limit_bytes=VMEM_LIMIT),
    )(_scalar(lax.axis_index("c")), g, theirs)


def _exchange_copies(a_refs, o_refs, send_sems, recv_sems, start):
    x, y, c, chips = _place()
    mine = 2 * x + y
    waits = []
    for b, (a_ref, o_ref) in enumerate(zip(a_refs, o_refs)):
        rows = a_ref.shape[1]
        for j, (cx, cy) in enumerate(chips):
            def copy(first, size):
                rws = pl.ds(first, size)
                return pltpu.make_async_remote_copy(src_ref=a_ref.at[2 * cx + cy, rws, :], dst_ref=o_ref.at[mine, rws, :],
                                                    send_sem=send_sems.at[b, j], recv_sem=recv_sems.at[b, j],
                                                    device_id=(cx, cy, c), device_id_type=MESH)
            if start:
                for first, size in _pieces(rows, ICI_PIECES, _row_unit(a_ref.dtype)):
                    copy(first, size).start()
            else:
                waits.append(copy(0, rows))
    return waits


def _exchange_finish(a_refs, o_refs, send_sems, recv_sems):
    for cp in _exchange_copies(a_refs, o_refs, send_sems, recv_sems, False):
        cp.wait()


def exchange_ride(parts):
    n = len(parts)
    return Ride(parts, [jax.ShapeDtypeStruct(a.shape, a.dtype) for a in parts], False,
                [pltpu.SemaphoreType.DMA((n, 3)), pltpu.SemaphoreType.DMA((n, 3))],
                lambda ins, outs, sems: _exchange_copies(ins, outs, *sems, True), lambda ins, outs, sems: _exchange_finish(ins, outs, *sems))


def chip_exchange(name, parts):
    n = len(parts)

    def body(*refs):
        a_refs, o_refs, sems = refs[:n], refs[n:2 * n], refs[2 * n:]
        _exchange_copies(a_refs, o_refs, *sems, True)
        _exchange_finish(a_refs, o_refs, *sems)

    return pl.pallas_call(
        body, name=name, in_specs=[ANY] * n, out_specs=[ANY] * n,
        out_shape=[jax.ShapeDtypeStruct(a.shape, a.dtype) for a in parts],
        scratch_shapes=[pltpu.SemaphoreType.DMA((n, 3)), pltpu.SemaphoreType.DMA((n, 3))],
    )(*parts)


def chip_add(name, got, part):
    _, rows, cols = got.shape
    br = _pick(rows, 128, 16)
    nblk = rows // br

    def body(m_ref, c_ref, got_ref, part_ref, o_ref):
        mine = m_ref[0]
        for s in range(4):
            @pl.when(mine == s)
            def _(s=s):
                val = part_ref[...].astype(F32)
                o_ref[...] = val if s == 0 else o_ref[...] + val

            @pl.when(mine != s)
            def _(s=s):
                val = got_ref[s].astype(F32)
                o_ref[...] = val if s == 0 else o_ref[...] + val

    return pl.pallas_call(
        body, name=name,
        grid_spec=pltpu.PrefetchScalarGridSpec(
            num_scalar_prefetch=2, grid=(nblk,),
            in_specs=[pl.BlockSpec((4, br, cols), lambda i, m, c: (0, i, 0)),
                      pl.BlockSpec((None, br, cols), lambda i, m, c: (m[0], i, 0))],
            out_specs=pl.BlockSpec((br, cols), lambda i, m, c: (c[0] * nblk + i, 0))),
        out_shape=jax.ShapeDtypeStruct((2 * rows, cols), F32),
        compiler_params=pltpu.CompilerParams(dimension_semantics=("arbitrary",), vmem_limit_bytes=VMEM_LIMIT),
    )(_scalar(2 * lax.axis_index("x") + lax.axis_index("y")), _scalar(lax.axis_index("c")), got, part)


def pair_join(name, fulls):
    n = len(fulls)

    def body(*refs):
        o_refs = refs[n:2 * n]
        send_sems, recv_sems = refs[2 * n:]
        x, y, c, _ = _place()
        waits = []
        for b, o_ref in enumerate(o_refs):
            half = o_ref.shape[0] // 2
            unit = _row_unit(o_ref.dtype)

            def copy(start, size):
                piece = o_ref.at[pl.ds(pl.multiple_of(c * half + start, unit), size), :]
                return pltpu.make_async_remote_copy(src_ref=piece, dst_ref=piece, send_sem=send_sems.at[b], recv_sem=recv_sems.at[b],
                                                    device_id=(x, y, 1 - c), device_id_type=MESH)

            for start, size in _pieces(half, D2D_PIECES, unit):
                copy(start, size).start()
            waits.append(copy(0, half))
        for cp in waits:
            cp.wait()

    return pl.pallas_call(
        body, name=name, in_specs=[ANY] * n, out_specs=[ANY] * n,
        out_shape=[jax.ShapeDtypeStruct(f.shape, f.dtype) for f in fulls],
        input_output_aliases={i: i for i in range(n)},
        scratch_shapes=[pltpu.SemaphoreType.DMA((n,)), pltpu.SemaphoreType.DMA((n,))],
    )(*fulls)


D_IN = 15920
D_PROJ = 16000
_SEGMENTS = ((0, 8192), (8208, 12816), (12848, 15920), (8192, 8208), (12816, 12848))
OFF_SB_Z, OFF_GDN_QKV, OFF_GDN_Z, OFF_SSM_Z, OFF_SSM_XBC, OFF_GATES, OFF_SMALL = 3072, 4096, 7168, 8192, 10240, 12800, 15872
PACK_C = 1024
WEIGHTS = ("meta_tokens", "norm_g", "w_in", "gdn_conv_w", "gdn_a_log", "gdn_dt_bias", "gdn_norm_g", "ssm_conv_w", "ssm_conv_b",
           "ssm_a_log", "ssm_dt_bias", "ssm_d", "ssm_norm_g", "w_branch_a", "w_branch_b", "w_branch_c", "w_out", "final_norm_g")
SHARDED = ("w_in", "w_branch_a", "w_branch_b", "w_branch_c", "w_out", "gdn_conv_w", "ssm_conv_w", "meta_tokens")
SHARD_AXIS = {"w_in": 2, "w_branch_a": 1, "w_branch_b": 1, "w_branch_c": 1, "w_out": 1, "gdn_conv_w": 2, "ssm_conv_w": 2, "meta_tokens": 1}
BRANCH = ("w_branch_a", "w_branch_b", "w_branch_c", "w_out")
EXACT = ("gdn_conv_w", "ssm_conv_w", "meta_tokens")
REPLICATED = tuple(n for n in WEIGHTS if n not in SHARDED)


def _regrouped_from_shards(shard_cols):
    out = []
    for a, b in _SEGMENTS:
        while a < b:
            chip = a // shard_cols
            stop = min(b, (chip + 1) * shard_cols)
            out.append((chip, a - chip * shard_cols, stop - chip * shard_cols))
            a = stop
    return out


def _shard_from_regrouped(chip, shard_cols):
    lo, hi = chip * shard_cols, (chip + 1) * shard_cols
    out, pos = [], 0
    starts = {}
    for a, b in _SEGMENTS:
        starts[(a, b)] = pos
        pos += b - a
    for a, b in sorted(_SEGMENTS):
        s0, s1 = max(a, lo), min(b, hi)
        if s0 < s1:
            out.append((starts[(a, b)] + s0 - a, starts[(a, b)] + s1 - a))
    return out


def _pack(parts, row_unit=64):
    n = sum(p.shape[0] for p in parts)
    rows = -(-n // (PACK_C * row_unit)) * row_unit
    flat = jnp.concatenate(list(parts) + [jnp.zeros((rows * PACK_C - n,), parts[0].dtype)])
    return flat.reshape(rows, PACK_C)


def _unpack(buf, shapes):
    flat, out, pos = buf.reshape(-1), [], 0
    for shp in shapes:
        n = math.prod(shp)
        out.append(flat[pos:pos + n].reshape(shp))
        pos += n
    return out


def _as_bf16_words(a):
    return lax.bitcast_convert_type(a, BF16).reshape(-1)


BRANCH_ROWS = (D_MODEL // 4, D_MODEL // 4, SSM_INNER // 4, D_MODEL // 4)


def _place_weights(w):
    depth = w["w_in"].shape[0]
    layers = []
    for l in range(depth):
        a = w["w_in"][l].astype(BF16)
        b = jnp.concatenate([w[n][l] for n in BRANCH], axis=0).astype(BF16)
        layers.append([place_shard("place_w_in", a), place_shard("place_branch", b)])
    small = place_shard("place_exact", _pack([_as_bf16_words(w[n]) for n in EXACT]))
    return layers, small


def _exact_weights(w, got_s):
    per_chip = [_unpack(got_s[c], [w[n].shape + (2,) for n in EXACT]) for c in range(4)]
    return {n: jnp.concatenate([lax.bitcast_convert_type(per_chip[c][i], F32) for c in range(4)], axis=SHARD_AXIS[n])
            for i, n in enumerate(EXACT)}


def _proj_weight(got_a):
    d_model, shard_cols = got_a.shape[1:]
    pad = jnp.zeros((d_model, D_PROJ - D_IN), BF16)
    return jnp.concatenate([got_a[c, :, lo:hi] for c, lo, hi in _regrouped_from_shards(shard_cols)] + [pad], axis=1)


def _branch_weights(got_b):
    unit = min(BRANCH_ROWS)
    out, pos = {}, 0
    for n, rows in zip(("wa", "wb", "wc", "wo"), BRANCH_ROWS):
        out[n] = (got_b, (unit, pos // unit, rows // unit))
        pos += rows
    return out


def _shard(a, axis, s):
    size = a.shape[axis] // 4
    return lax.slice_in_dim(a, s * size, (s + 1) * size, axis=axis)


def _layer_grad_buffers(g, shard_cols):
    buf_a = jnp.stack([jnp.concatenate([g["w_in"][:, lo:hi] for lo, hi in _shard_from_regrouped(s, shard_cols)], axis=1)
                       for s in range(4)]).astype(BF16)
    buf_b = jnp.stack([jnp.concatenate([_shard(g[n], 0, s) for n in BRANCH], axis=0) for s in range(4)]).astype(BF16)
    return [buf_a, buf_b]


def _split_halves(buf):
    return buf.reshape(4, 2, buf.shape[1] // 2, buf.shape[2])


def _start_reduce(tag, bufs, transits):
    bufs = [_split_halves(g) for g in bufs]
    theirs = pair_split(tag + "_pair_split", bufs)
    return [pair_add(f"{tag}_pair_add_{i}", g, t, tr) for i, (g, t, tr) in enumerate(zip(bufs, theirs, transits))]


def _layer_params(w, exact, wp, l):
    lane = lambda v, lo: jnp.pad(v, (lo, HEAD_DIM - lo - v.shape[0]))[None]
    return dict(
        norm_g=w["norm_g"][l][None], wp=wp,
        gdn_conv_w=exact["gdn_conv_w"][l], gdn_conv_b=jnp.zeros((1, 3 * N_HEADS * HEAD_DIM), F32),
        ssm_conv_w=exact["ssm_conv_w"][l], ssm_conv_b=w["ssm_conv_b"][l][None],
        bias_vec=lane(w["gdn_dt_bias"][l], 8) + lane(w["ssm_dt_bias"][l], 16),
        alog_vec=lane(w["gdn_a_log"][l], 8) + lane(w["ssm_a_log"][l], 16),
        gdn_norm_g=w["gdn_norm_g"][l][None], d_skip=jnp.repeat(w["ssm_d"][l], SSM_P)[None], ssm_norm_g=w["ssm_norm_g"][l][None])


def _layer_fwd(h, p, ride):
    lp = h.shape[0]
    bm = _pick(lp, 272, 8)
    kw = dict(bm=bm, lp=lp)
    (u,) = rowmap_fwd("rms_fwd", _f_rmsnorm, [(h, 0)], [(p["norm_g"], False)], 1, width=D_MODEL, ncol=1, out_dtypes=[BF16], **kw)
    proj = matmul(u, p["wp"], "nn", name="proj", bm=lp, bn=640)
    o_a_raw, *rode = sb_fwd(proj, (0, N_HEADS, 2 * N_HEADS), ride)
    p = dict(p, **_branch_weights(rode[0]))
    qkv = conv_fwd("gdn_conv_fwd", proj, OFF_GDN_QKV // HEAD_DIM, p["gdn_conv_w"], p["gdn_conv_b"], 2 * N_HEADS)
    first, second = rowmap_fwd("gates_fwd", _f_small_gates, [(proj, OFF_SMALL // HEAD_DIM)],
                               [(p["bias_vec"], False), (p["alog_vec"], False)], 2, width=HEAD_DIM, ncol=1, **kw)
    o_b_raw, gdn_states, gdn_t = gdn_fwd(qkv, first)
    xbc = conv_fwd("ssm_conv_fwd", proj, OFF_SSM_XBC // HEAD_DIM, p["ssm_conv_w"], p["ssm_conv_b"], 0)
    y_raw, ssd_states = ssd_fwd(xbc, first, second)
    (o_a,) = rowmap_fwd("gate_a_fwd", _f_gate_silu, [(o_a_raw, 0), (proj, OFF_SB_Z // 1024)], [], 1, width=1024, ncol=1,
                        out_dtypes=[BF16], **kw)
    (o_b,) = rowmap_fwd("gate_b_fwd", _f_head_norm_gate, [(o_b_raw, 0), (proj, OFF_GDN_Z // 1024)], [(p["gdn_norm_g"], False)], 1,
                        width=1024, ncol=1, out_dtypes=[BF16], **kw)
    (o_c,) = rowmap_fwd("gate_c_fwd", _f_ssm_out, [(y_raw, 0), (xbc, 0), (proj, OFF_SSM_Z // 1024)],
                        [(p["d_skip"], True), (p["ssm_norm_g"], True)], 1, width=1024, ncol=SSM_GROUPS, out_dtypes=[BF16], **kw)
    pa = matmul(o_a, p["wa"][0], "nn", name="branch_a", bm=lp // 2, bn=512, b_view=p["wa"][1])
    pb = matmul(o_b, p["wb"][0], "nn", name="branch_b", bm=lp // 2, bn=512, b_view=p["wb"][1])
    pc = matmul(o_c, p["wc"][0], "nn", name="branch_c", bm=lp // 2, bn=512, b_view=p["wc"][1])
    merge_rows = [(pa, 0), (pb, 0), (pc, 0)] + [(proj, OFF_GATES // 512 + 2 * i) for i in range(3)]
    (merged,) = rowmap_fwd("merge_fwd", _f_merge, merge_rows, [], 1, width=512, ncol=2, out_dtypes=[BF16], **kw)
    h_out = matmul(merged, p["wo"][0], "nn", name="out_proj", bm=lp, bn=512, residual=h, b_view=p["wo"][1])
    saved = dict(h=h, u=u, proj=proj, qkv=qkv, first=first, second=second, o_a_raw=o_a_raw, o_b_raw=o_b_raw,
                 gdn_states=gdn_states, gdn_t=gdn_t, xbc=xbc, y_raw=y_raw, ssd_states=ssd_states, o_a=o_a, o_b=o_b, o_c=o_c, pa=pa, pb=pb, pc=pc,
                 merged=merged, params=p)
    return h_out, saved, rode[1:]


def _layer_bwd(d_h, p, s, ride):
    lp = d_h.shape[0]
    bm = _pick(lp, 272, 8)
    kw = dict(bm=bm, lp=lp)
    proj = s["proj"]
    g = {}
    d_merged = matmul(d_h, p["wo"][0], "nt", name="d_merged", bm=lp, bk=1024, b_view=p["wo"][1])
    g["w_out"] = matmul(s["merged"], d_h, "tn", name="g_w_out", bm=512, bn=1024, bk=lp)
    merge_rows = [(s["pa"], 0), (s["pb"], 0), (s["pc"], 0)] + [(proj, OFF_GATES // 512 + 2 * i) for i in range(3)]
    (d_pa, d_pb, d_pc, d_ga, d_gb, d_gc), _ = rowmap_bwd("merge_bwd", _f_merge, merge_rows, [], [d_merged], width=512, ncol=2,
                                                         d_row_dtypes=[BF16] * 6, **kw)
    g["w_branch_a"] = matmul(s["o_a"], d_pa, "tn", name="g_w_a", bm=512, bn=1024, bk=lp)
    g["w_branch_b"] = matmul(s["o_b"], d_pb, "tn", name="g_w_b", bm=512, bn=1024, bk=lp)
    g["w_branch_c"] = matmul(s["o_c"], d_pc, "tn", name="g_w_c", bm=512, bn=1024, bk=lp)
    d_oa = matmul(d_pa, p["wa"][0], "nt", name="d_o_a", bm=lp, bk=1024, b_view=p["wa"][1])
    d_ob = matmul(d_pb, p["wb"][0], "nt", name="d_o_b", bm=lp, bk=1024, b_view=p["wb"][1])
    d_oc = matmul(d_pc, p["wc"][0], "nt", name="d_o_c", bm=lp, bk=1024, b_view=p["wc"][1])
    (d_oa_raw, d_sbz), _ = rowmap_bwd("gate_a_bwd", _f_gate_silu, [(s["o_a_raw"], 0), (proj, OFF_SB_Z // 1024)], [], [d_oa],
                                      width=1024, ncol=1, d_row_dtypes=[F32, BF16], **kw)
    (d_ob_raw, d_gdz), (g["gdn_norm_g"],) = rowmap_bwd(
        "gate_b_bwd", _f_head_norm_gate, [(s["o_b_raw"], 0), (proj, OFF_GDN_Z // 1024)], [(p["gdn_norm_g"], False)], [d_ob],
        width=1024, ncol=1, d_row_dtypes=[F32, BF16], **kw)
    (d_y, d_xh, d_ssz), (g_dskip, g["ssm_norm_g"]) = rowmap_bwd(
        "gate_c_bwd", _f_ssm_out, [(s["y_raw"], 0), (s["xbc"], 0), (proj, OFF_SSM_Z // 1024)],
        [(p["d_skip"], True), (p["ssm_norm_g"], True)], [d_oc], width=1024, ncol=SSM_GROUPS, d_row_dtypes=[F32, F32, BF16], **kw)
    g["gdn_norm_g"], g["ssm_norm_g"] = g["gdn_norm_g"][0], g["ssm_norm_g"][0]
    g["ssm_d"] = g_dskip.reshape(SSM_HEADS, SSM_P).sum(axis=1)
    d_q, d_k, d_v, *rode = sb_bwd(proj, (0, N_HEADS, 2 * N_HEADS), d_oa_raw, ride)
    d_qkv, d_first_gdn = gdn_bwd(s["qkv"], s["first"], s["gdn_states"], s["gdn_t"], d_ob_raw)
    d_xbc_out, d_first_ssd, d_second = ssd_bwd(s["xbc"], s["first"], s["second"], s["ssd_states"], d_y, d_xh)
    d_gdqkv, g["gdn_conv_w"], _ = conv_bwd("gdn_conv_bwd", proj, OFF_GDN_QKV // HEAD_DIM, p["gdn_conv_w"], p["gdn_conv_b"], 2 * N_HEADS,
                                           d_qkv)
    d_xbc, g["ssm_conv_w"], g_cb = conv_bwd("ssm_conv_bwd", proj, OFF_SSM_XBC // HEAD_DIM, p["ssm_conv_w"], p["ssm_conv_b"], 0, d_xbc_out)
    g["ssm_conv_b"] = g_cb[0]
    d_first = d_first_gdn + d_first_ssd
    (d_small,), (g_bias, g_alog) = rowmap_bwd("gates_bwd", _f_small_gates, [(proj, OFF_SMALL // HEAD_DIM)],
                                              [(p["bias_vec"], False), (p["alog_vec"], False)], [d_first, d_second],
                                              width=HEAD_DIM, ncol=1, d_row_dtypes=[BF16], **kw)
    g["gdn_dt_bias"], g["ssm_dt_bias"] = g_bias[0, 8:16], g_bias[0, 16:48]
    g["gdn_a_log"], g["ssm_a_log"] = g_alog[0, 8:16], g_alog[0, 16:48]
    d_proj = jnp.concatenate([d_q, d_k, d_v, d_sbz, d_gdqkv, d_gdz, d_ssz, d_xbc, d_ga, d_gb, d_gc, d_small], axis=1)
    g["w_in"] = matmul(s["u"], d_proj, "tn", name="g_w_in", bm=1024, bn=640, bk=lp, out_dtype=BF16)
    bufs = [_split_halves(buf) for buf in _layer_grad_buffers(g, D_IN // 4)]
    d_u, *theirs = matmul(d_proj, p["wp"], "nt", name="d_u", bm=lp, bn=1024, bk=800, ride=split_ride(bufs))
    parts = [pair_add(f"grads_pair_add_{i}", buf, t, BF16) for i, (buf, t) in enumerate(zip(bufs, theirs))]
    (d_hn,), (g_norm,) = rowmap_bwd("rms_bwd", _f_rmsnorm, [(s["h"], 0)], [(p["norm_g"], False)], [d_u], width=D_MODEL, ncol=1, **kw)
    g["norm_g"] = g_norm[0]
    return d_h + d_hn, g, rode, parts


def kernel(x, meta_tokens, norm_g, w_in, gdn_conv_w, gdn_a_log, gdn_dt_bias, gdn_norm_g, ssm_conv_w, ssm_conv_b, ssm_a_log, ssm_dt_bias, ssm_d, ssm_norm_g, w_branch_a, w_branch_b, w_branch_c, w_out, final_norm_g, loss_target, m_meta_tokens, m_norm_g, m_w_in, m_gdn_conv_w, m_gdn_a_log, m_gdn_dt_bias, m_gdn_norm_g, m_ssm_conv_w, m_ssm_conv_b, m_ssm_a_log, m_ssm_dt_bias, m_ssm_d, m_ssm_norm_g, m_w_branch_a, m_w_branch_b, m_w_branch_c, m_w_out, m_final_norm_g, v_meta_tokens, v_norm_g, v_w_in, v_gdn_conv_w, v_gdn_a_log, v_gdn_dt_bias, v_gdn_norm_g, v_ssm_conv_w, v_ssm_conv_b, v_ssm_a_log, v_ssm_dt_bias, v_ssm_d, v_ssm_norm_g, v_w_branch_a, v_w_branch_b, v_w_branch_c, v_w_out, v_final_norm_g):
    w = dict(meta_tokens=meta_tokens, norm_g=norm_g, w_in=w_in, gdn_conv_w=gdn_conv_w, gdn_a_log=gdn_a_log, gdn_dt_bias=gdn_dt_bias,
             gdn_norm_g=gdn_norm_g, ssm_conv_w=ssm_conv_w, ssm_conv_b=ssm_conv_b, ssm_a_log=ssm_a_log, ssm_dt_bias=ssm_dt_bias,
             ssm_d=ssm_d, ssm_norm_g=ssm_norm_g, w_branch_a=w_branch_a, w_branch_b=w_branch_b, w_branch_c=w_branch_c, w_out=w_out,
             final_norm_g=final_norm_g)
    m = dict(meta_tokens=m_meta_tokens, norm_g=m_norm_g, w_in=m_w_in, gdn_conv_w=m_gdn_conv_w, gdn_a_log=m_gdn_a_log,
             gdn_dt_bias=m_gdn_dt_bias, gdn_norm_g=m_gdn_norm_g, ssm_conv_w=m_ssm_conv_w, ssm_conv_b=m_ssm_conv_b,
             ssm_a_log=m_ssm_a_log, ssm_dt_bias=m_ssm_dt_bias, ssm_d=m_ssm_d, ssm_norm_g=m_ssm_norm_g, w_branch_a=m_w_branch_a,
             w_branch_b=m_w_branch_b, w_branch_c=m_w_branch_c, w_out=m_w_out, final_norm_g=m_final_norm_g)
    v = dict(meta_tokens=v_meta_tokens, norm_g=v_norm_g, w_in=v_w_in, gdn_conv_w=v_gdn_conv_w, gdn_a_log=v_gdn_a_log,
             gdn_dt_bias=v_gdn_dt_bias, gdn_norm_g=v_gdn_norm_g, ssm_conv_w=v_ssm_conv_w, ssm_conv_b=v_ssm_conv_b,
             ssm_a_log=v_ssm_a_log, ssm_dt_bias=v_ssm_dt_bias, ssm_d=v_ssm_d, ssm_norm_g=v_ssm_norm_g, w_branch_a=v_w_branch_a,
             w_branch_b=v_w_branch_b, w_branch_c=v_w_branch_c, w_out=v_w_out, final_norm_g=v_final_norm_g)
    depth = norm_g.shape[0]
    placed, placed_small = _place_weights(w)
    got_a, got_s = gather_shards("gather_first", [placed[0][0], placed_small])
    exact = _exact_weights(w, got_s)

    h = jnp.concatenate([jnp.zeros((PAD, D_MODEL), F32), exact["meta_tokens"], x[0]], axis=0)
    params, saved = [], []
    for l in range(depth):
        ride = gather_ride([placed[l][1]] + ([placed[l + 1][0]] if l + 1 < depth else []))
        h, s, rode = _layer_fwd(h, _layer_params(w, exact, _proj_weight(got_a), l), ride)
        params.append(s.pop("params"))
        saved.append(s)
        if rode:
            (got_a,) = rode
    loss, d_h, g_final = loss_head(h, loss_target[0], final_norm_g[None])

    layer_grads, reds, waiting = [None] * depth, [None] * depth, None
    for l in reversed(range(depth)):
        d_h, layer_grads[l], rode, parts = _layer_bwd(d_h, params[l], saved[l], exchange_ride(waiting) if waiting else None)
        if waiting:
            reds[l + 1] = [chip_add(f"grads_chip_add_{i}", gt, p) for i, (gt, p) in enumerate(zip(rode, waiting))]
        waiting = parts
    grads = {n: jnp.stack([layer_grads[l][n] for l in range(depth)]) for n in WEIGHTS
             if n not in ("meta_tokens", "final_norm_g", "w_in") + BRANCH}
    grads["meta_tokens"] = d_h[PAD:PAD + N_META]
    grads["final_norm_g"] = g_final[0]
    grad_x = d_h[PAD + N_META:][None]
    buf_s = jnp.stack([_pack([_shard(grads[n], SHARD_AXIS[n], s).astype(BF16).reshape(-1) for n in EXACT], row_unit=32) for s in range(4)])
    small = _pack([grads[n].reshape(-1) for n in REPLICATED], row_unit=32)
    last = waiting + _start_reduce("small_grads", [buf_s, jnp.broadcast_to(small[None], (4,) + small.shape)], [BF16, F32])
    got = chip_exchange("grads_chip_exchange", last)
    sums = [chip_add(f"grads_chip_add_{i}", gt, p) for i, (gt, p) in enumerate(zip(got, last))]
    reds[0] = sums[:2]
    joined = pair_join("grads_pair_join", [r for layer in reds for r in layer] + sums[2:])
    red = {"w_in": jnp.stack(joined[0:2 * depth:2])}
    pos = 0
    for n, rows in zip(BRANCH, BRANCH_ROWS):
        red[n] = jnp.stack([joined[2 * l + 1][pos:pos + rows] for l in range(depth)])
        pos += rows
    red.update(zip(EXACT, _unpack(joined[-2], [w[n].shape for n in EXACT])))
    small_red = joined[-1]
    delta, new_m, new_v = {}, {}, {}
    for n in SHARDED:
        if w[n].shape[-1] % HEAD_DIM:
            to_view, from_view = (lambda a: jnp.transpose(a, (2, 0, 1))), (lambda a: jnp.transpose(a, (1, 2, 0)))
            delta[n], new_m[n], new_v[n], red[n] = [from_view(a) for a in adamw("adamw_" + n, to_view(w[n]), to_view(red[n]),
                                                                                to_view(m[n]), to_view(v[n]), echo_g=True)]
        else:
            delta[n], new_m[n], new_v[n] = adamw("adamw_" + n, w[n], red[n], m[n], v[n])
    pack_small = lambda d: _pack([d[n].reshape(-1) for n in REPLICATED], row_unit=32)
    small = adamw("adamw_small", pack_small(w), small_red, pack_small(m), pack_small(v))
    shapes = [w[n].shape for n in REPLICATED]
    red.update(zip(REPLICATED, _unpack(small_red, shapes)))
    for d, buf in zip((delta, new_m, new_v), small):
        d.update(zip(REPLICATED, _unpack(buf, shapes)))
    total_loss = lax.psum(loss[0, 0], ("x", "y", "c"))
    return (total_loss, grad_x, *[red[n] for n in WEIGHTS], *[delta[n] for n in WEIGHTS], *[new_m[n] for n in WEIGHTS],
            *[new_v[n] for n in WEIGHTS])
```

```python
import functools
import math

import jax
import jax.numpy as jnp
from jax import lax
from jax.experimental import pallas as pl
from jax.experimental.pallas import tpu as pltpu

F32 = jnp.float32
BF16 = jnp.bfloat16

N_META = 16
RMS_EPS = 1e-6
L2_EPS = 1e-6
CONV_K = 4
D_MODEL = 1024
HEAD_DIM = 128
N_HEADS = 8
CHUNK = 64
SB_BLOCK = 128
PAD = SB_BLOCK - N_META
SSM_INNER = 2048
SSM_P = 64
SSM_HEADS = 32
SSM_GROUPS = 2
SSM_HG = SSM_HEADS // SSM_GROUPS
SSM_N = 128
VMEM_LIMIT = 56 * 1024 * 1024

def _dims(mode, ndim):
    lhs, rhs = {"nn": (1, 0), "nt": (1, 1), "tn": (0, 0)}[mode]
    off = ndim - 2
    return (((lhs + off,), (rhs + off,)), (tuple(range(off)), tuple(range(off))))


def _dot(a, b, mode):
    return lax.dot_general(a, b, _dims(mode, a.ndim), preferred_element_type=F32)


def _halves(a):
    hi = a.astype(BF16)
    return hi, (a - hi.astype(F32)).astype(BF16)


def _mm_raw(a, b, mode, kind):
    if kind == "bf16":
        return _dot(a.astype(BF16), b.astype(BF16), mode)
    if kind == "lhs01":
        hi, lo = _halves(b)
        a = a.astype(BF16)
        return _dot(a, hi, mode) + _dot(a, lo, mode)
    if kind == "rhs01":
        hi, lo = _halves(a)
        b = b.astype(BF16)
        return _dot(hi, b, mode) + _dot(lo, b, mode)
    a_hi, a_lo = _halves(a)
    b_hi, b_lo = _halves(b)
    return _dot(a_hi, b_hi, mode) + (_dot(a_hi, b_lo, mode) + _dot(a_lo, b_hi, mode))


@functools.partial(jax.custom_vjp, nondiff_argnums=(2, 3))
def _mm(a, b, mode="nn", kind="bf16"):
    return _mm_raw(a, b, mode, kind)


def _mm_fwd(a, b, mode, kind):
    return _mm_raw(a, b, mode, kind), (a, b)


def _mm_bwd(mode, kind, res, g):
    a, b = res
    if kind == "lhs01":
        return jnp.zeros_like(a), _mm_raw(a, g, {"nn": "tn", "tn": "nn"}[mode], "lhs01")
    if kind == "rhs01":
        return _mm_raw(g, b, {"nn": "nt", "nt": "nn"}[mode], "rhs01"), jnp.zeros_like(b)
    if mode == "nn":
        return _mm_raw(g, b, "nt", kind), _mm_raw(a, g, "tn", kind)
    if mode == "nt":
        return _mm_raw(g, b, "nn", kind), _mm_raw(g, a, "tn", kind)
    return _mm_raw(b, g, "nt", kind), _mm_raw(a, g, "nn", kind)


_mm.defvjp(_mm_fwd, _mm_bwd)


def _iota2(shape, axis):
    return lax.broadcasted_iota(jnp.int32, shape, axis)


def _inv_unit_lower_raw(m):
    size = m.shape[-1]
    eye = (_iota2((size, size), 0) == _iota2((size, size), 1)).astype(F32)
    n = -m
    t = eye + n
    p = n
    steps = int(math.log2(size)) - 1
    for _ in range(steps):
        p = _mm_raw(p, p, "nn", "x3")
        t = t + _mm_raw(t, p, "nn", "x3")
    return t


@jax.custom_vjp
def _inv_unit_lower(m):
    return _inv_unit_lower_raw(m)


def _inv_fwd(m):
    t = _inv_unit_lower_raw(m)
    return t, t


def _inv_bwd(t, g):
    return (-_mm_raw(_mm_raw(t, g, "tn", "x3"), t, "nt", "x3"),)


_inv_unit_lower.defvjp(_inv_fwd, _inv_bwd)


@jax.custom_vjp
def _inv_known(m, t):
    return t


_inv_known.defvjp(lambda m, t: (t, t), lambda t, g: (_inv_bwd(t, g)[0], jnp.zeros_like(t)))


def _safe_decay(col, row, keep):
    return jnp.where(keep, jnp.exp(jnp.where(keep, col - row, 0.0)), 0.0)


def _col_to_row(col):
    n = col.shape[-2]
    eye = _iota2((n, n), 0) == _iota2((n, n), 1)
    return jnp.sum(jnp.where(eye, col, 0.0), axis=-2, keepdims=True)


def _cumsum_col(col):
    n = col.shape[-2]
    li, si = _iota2((n, n), 0), _iota2((n, n), 1)
    row = _col_to_row(col)
    c_col = jnp.sum(jnp.where(li >= si, row, 0.0), axis=-1, keepdims=True)
    c_row = jnp.sum(jnp.where(li <= si, col, 0.0), axis=-2, keepdims=True)
    return c_col, c_row


def _gdn_chunk(q, k, v, g, beta, state, t_known=None):
    cl = q.shape[-2]
    li, si = _iota2((cl, cl), 0), _iota2((cl, cl), 1)
    gc_col, gc_row = _cumsum_col(g)
    g_last = jnp.sum(g, axis=-2, keepdims=True)
    dec_strict = _safe_decay(gc_col, gc_row, li > si)
    dec_incl = _safe_decay(gc_col, gc_row, li >= si)
    e_gc = jnp.exp(gc_col)
    qs = q * (HEAD_DIM ** -0.5)
    kb = k * beta
    m = _mm(kb, k, "nt") * dec_strict
    t_inv = _inv_unit_lower(m) if t_known is None else _inv_known(m, t_known)
    u = _mm(t_inv, v * beta)
    w = _mm(t_inv, kb * e_gc)
    a_qk = _mm(qs, k, "nt") * dec_incl
    q_dec = qs * e_gc
    k_end = k * jnp.exp(g_last - gc_col)
    v_new = u - _mm(w, state)
    o = _mm(q_dec, state) + _mm(a_qk, v_new)
    new_state = state * jnp.exp(g_last) + _mm(k_end, v_new, "tn")
    return o, new_state, t_inv


def _gdn_operands(qkv_ref, gt):
    nh, width = N_HEADS, N_HEADS * HEAD_DIM
    heads = lambda off: jnp.stack([qkv_ref[:, off + h * HEAD_DIM:off + (h + 1) * HEAD_DIM] for h in range(nh)])
    cols = lambda off: jnp.stack([gt[:, off + h:off + h + 1] for h in range(nh)])
    return heads(0), heads(width), heads(2 * width), cols(nh), cols(0)


def gdn_fwd(qkv, gates):
    lp = qkv.shape[0]
    nh = N_HEADS
    nc = lp // CHUNK
    width = nh * HEAD_DIM

    def body(qkv_ref, gt_ref, o_ref, s_ref, t_ref, state):
        @pl.when(pl.program_id(0) == 0)
        def _():
            state[...] = jnp.zeros_like(state)

        s_in = state[...]
        s_ref[0] = s_in
        o, s_new, t_inv = _gdn_chunk(*_gdn_operands(qkv_ref, gt_ref[...]), s_in)
        t_ref[0] = t_inv
        for h in range(nh):
            o_ref[:, h * HEAD_DIM:(h + 1) * HEAD_DIM] = o[h]
        state[...] = s_new

    return pl.pallas_call(
        body, name="gdn_fwd", grid=(nc,),
        in_specs=[pl.BlockSpec((CHUNK, 3 * width), lambda c: (c, 0)), pl.BlockSpec((CHUNK, HEAD_DIM), lambda c: (c, 0))],
        out_specs=[pl.BlockSpec((CHUNK, width), lambda c: (c, 0)), pl.BlockSpec((1, nh, HEAD_DIM, HEAD_DIM), lambda c: (c, 0, 0, 0)),
                   pl.BlockSpec((1, nh, CHUNK, CHUNK), lambda c: (c, 0, 0, 0))],
        out_shape=[jax.ShapeDtypeStruct((lp, width), F32), jax.ShapeDtypeStruct((nc, nh, HEAD_DIM, HEAD_DIM), F32),
                   jax.ShapeDtypeStruct((nc, nh, CHUNK, CHUNK), F32)],
        scratch_shapes=[pltpu.VMEM((nh, HEAD_DIM, HEAD_DIM), F32)],
        compiler_params=pltpu.CompilerParams(dimension_semantics=("arbitrary",), vmem_limit_bytes=VMEM_LIMIT),
    )(qkv, gates)


def gdn_bwd(qkv, gates, states, t_invs, d_o):
    lp = qkv.shape[0]
    nh = N_HEADS
    nc = lp // CHUNK
    width = nh * HEAD_DIM

    def body(qkv_ref, gt_ref, s_ref, t_ref, do_ref, dqkv_ref, dgt_ref, d_state):
        @pl.when(pl.program_id(0) == 0)
        def _():
            d_state[...] = jnp.zeros_like(d_state)

        t_known = t_ref[0]
        _, pull = jax.vjp(lambda *xs: _gdn_chunk(*xs, t_known=t_known)[:2], *_gdn_operands(qkv_ref, gt_ref[...]), s_ref[0])
        d_o = jnp.stack([do_ref[:, h * HEAD_DIM:(h + 1) * HEAD_DIM] for h in range(nh)])
        dq, dk, dv, dg, db, ds = pull((d_o, d_state[...]))
        lane = _iota2((CHUNK, HEAD_DIM), 1)
        d_gt = jnp.zeros((CHUNK, HEAD_DIM), F32)
        for h in range(nh):
            for part, val in enumerate((dq, dk, dv)):
                dqkv_ref[:, part * width + h * HEAD_DIM:part * width + (h + 1) * HEAD_DIM] = val[h]
            d_gt = d_gt + jnp.where(lane == h, db[h], 0.0) + jnp.where(lane == nh + h, dg[h], 0.0)
        dgt_ref[...] = d_gt
        d_state[...] = ds

    rev = lambda c: (nc - 1 - c, 0)
    return pl.pallas_call(
        body, name="gdn_bwd", grid=(nc,),
        in_specs=[pl.BlockSpec((CHUNK, 3 * width), rev), pl.BlockSpec((CHUNK, HEAD_DIM), rev),
                  pl.BlockSpec((1, nh, HEAD_DIM, HEAD_DIM), lambda c: (nc - 1 - c, 0, 0, 0)),
                  pl.BlockSpec((1, nh, CHUNK, CHUNK), lambda c: (nc - 1 - c, 0, 0, 0)), pl.BlockSpec((CHUNK, width), rev)],
        out_specs=[pl.BlockSpec((CHUNK, 3 * width), rev), pl.BlockSpec((CHUNK, HEAD_DIM), rev)],
        out_shape=[jax.ShapeDtypeStruct((lp, 3 * width), F32), jax.ShapeDtypeStruct((lp, HEAD_DIM), F32)],
        scratch_shapes=[pltpu.VMEM((nh, HEAD_DIM, HEAD_DIM), F32)],
        compiler_params=pltpu.CompilerParams(dimension_semantics=("arbitrary",), vmem_limit_bytes=VMEM_LIMIT),
    )(qkv, gates, states, t_invs, d_o)


def _head_expand():
    width = SSM_HG * SSM_P
    return (_iota2((SSM_HG, width), 1) // SSM_P == _iota2((SSM_HG, width), 0)).astype(F32)


def _ssd_chunk(x, b, c, dt, la, state):
    cl = x.shape[0]
    li, si = _iota2((cl, cl), 0), _iota2((cl, cl), 1)
    causal = li >= si
    expand = _head_expand()
    tri = causal.astype(F32)
    xs = x * _mm(dt, expand, "nn", "rhs01")
    la_x = _mm(la, expand, "nn", "rhs01")
    cs_x = _mm(tri, la_x, "nn", "lhs01")
    last_x = jnp.sum(la_x, axis=0, keepdims=True)
    cs = _mm(tri, la, "nn", "lhs01")
    scores = _mm(c, b, "nt")
    head_id = _iota2((1, SSM_HG), 1)
    per_tile = HEAD_DIM // SSM_P
    tile_head = _iota2((1, HEAD_DIM), 1) // SSM_P
    within = []
    for t in range(SSM_HG // per_tile):
        xs_t = xs[:, t * HEAD_DIM:(t + 1) * HEAD_DIM]
        acc = jnp.zeros((cl, HEAD_DIM), F32)
        for hh in range(per_tile):
            cs_col = jnp.sum(jnp.where(head_id == t * per_tile + hh, cs, 0.0), axis=1, keepdims=True)
            decay = _safe_decay(cs_col, _col_to_row(cs_col), causal)
            acc = acc + _mm(scores * decay, jnp.where(tile_head == hh, xs_t, 0.0))
        within.append(acc)
    y = _mm(c, state) * jnp.exp(cs_x) + jnp.concatenate(within, axis=1)
    new_state = state * jnp.exp(last_x) + _mm(b, xs * jnp.exp(last_x - cs_x), "tn")
    return y, new_state


GATE_DT = 16


def _place_lanes(v, lo):
    n = v.shape[1]
    sel = (_iota2((n, HEAD_DIM), 1) == _iota2((n, HEAD_DIM), 0) + lo).astype(F32)
    return _mm_raw(v, sel, "nn", "rhs01")


def ssd_fwd(xbc, first, second):
    lp = xbc.shape[0]
    nc = lp // CHUNK
    width = SSM_HG * SSM_P
    b_off, c_off = SSM_INNER, SSM_INNER + SSM_GROUPS * SSM_N

    def body(x_ref, f_ref, s2_ref, y_ref, s_ref, state):
        @pl.when(pl.program_id(0) == 0)
        def _():
            state[...] = jnp.zeros_like(state)

        f, s2 = f_ref[...], s2_ref[...]
        for g in range(SSM_GROUPS):
            lo = GATE_DT + g * SSM_HG
            s_in = state[g]
            s_ref[0, g] = s_in
            y, s_new = _ssd_chunk(x_ref[:, g * width:(g + 1) * width], x_ref[:, b_off + g * SSM_N:b_off + (g + 1) * SSM_N],
                                  x_ref[:, c_off + g * SSM_N:c_off + (g + 1) * SSM_N], f[:, lo:lo + SSM_HG], s2[:, lo:lo + SSM_HG], s_in)
            y_ref[:, g * width:(g + 1) * width] = y
            state[g] = s_new

    row = lambda cols: pl.BlockSpec((CHUNK, cols), lambda k: (k, 0))
    return pl.pallas_call(
        body, name="ssd_fwd", grid=(nc,),
        in_specs=[row(xbc.shape[1]), row(HEAD_DIM), row(HEAD_DIM)],
        out_specs=[row(SSM_INNER), pl.BlockSpec((1, SSM_GROUPS, SSM_N, width), lambda k: (k, 0, 0, 0))],
        out_shape=[jax.ShapeDtypeStruct((lp, SSM_INNER), F32), jax.ShapeDtypeStruct((nc, SSM_GROUPS, SSM_N, width), F32)],
        scratch_shapes=[pltpu.VMEM((SSM_GROUPS, SSM_N, width), F32)],
        compiler_params=pltpu.CompilerParams(dimension_semantics=("arbitrary",), vmem_limit_bytes=VMEM_LIMIT),
    )(xbc, first, second)


def ssd_bwd(xbc, first, second, states, d_y, d_xh):
    lp = xbc.shape[0]
    nc = lp // CHUNK
    width = SSM_HG * SSM_P
    b_off, c_off = SSM_INNER, SSM_INNER + SSM_GROUPS * SSM_N

    def body(x_ref, f_ref, s2_ref, s_ref, dy_ref, dxh_ref, dx_ref, df_ref, ds2_ref, d_state):
        @pl.when(pl.program_id(0) == 0)
        def _():
            d_state[...] = jnp.zeros_like(d_state)

        f, s2 = f_ref[...], s2_ref[...]
        d_f = jnp.zeros((CHUNK, HEAD_DIM), F32)
        d_s2 = jnp.zeros((CHUNK, HEAD_DIM), F32)
        for g in range(SSM_GROUPS):
            lo = GATE_DT + g * SSM_HG
            x_l = slice(g * width, (g + 1) * width)
            b_l = slice(b_off + g * SSM_N, b_off + (g + 1) * SSM_N)
            c_l = slice(c_off + g * SSM_N, c_off + (g + 1) * SSM_N)
            _, pull = jax.vjp(_ssd_chunk, x_ref[:, x_l], x_ref[:, b_l], x_ref[:, c_l], f[:, lo:lo + SSM_HG], s2[:, lo:lo + SSM_HG],
                              s_ref[0, g])
            dx, db, dc, ddt, dla, ds = pull((dy_ref[:, x_l], d_state[g]))
            dx_ref[:, x_l] = dx + dxh_ref[:, x_l]
            dx_ref[:, b_l] = db
            dx_ref[:, c_l] = dc
            d_f = d_f + _place_lanes(ddt, lo)
            d_s2 = d_s2 + _place_lanes(dla, lo)
            d_state[g] = ds
        df_ref[...] = d_f
        ds2_ref[...] = d_s2

    row = lambda cols: pl.BlockSpec((CHUNK, cols), lambda k: (nc - 1 - k, 0))
    gate_shape = jax.ShapeDtypeStruct((lp, HEAD_DIM), F32)
    return pl.pallas_call(
        body, name="ssd_bwd", grid=(nc,),
        in_specs=[row(xbc.shape[1]), row(HEAD_DIM), row(HEAD_DIM),
                  pl.BlockSpec((1, SSM_GROUPS, SSM_N, width), lambda k: (nc - 1 - k, 0, 0, 0)), row(SSM_INNER), row(SSM_INNER)],
        out_specs=[row(xbc.shape[1]), row(HEAD_DIM), row(HEAD_DIM)],
        out_shape=[jax.ShapeDtypeStruct(xbc.shape, F32), gate_shape, gate_shape],
        scratch_shapes=[pltpu.VMEM((SSM_GROUPS, SSM_N, width), F32)],
        compiler_params=pltpu.CompilerParams(dimension_semantics=("arbitrary",), vmem_limit_bytes=VMEM_LIMIT),
    )(xbc, first, second, states, d_y, d_xh)


SB_QROWS = 544


def _mm_tri(a, tri):
    return _mm_raw(a, tri.astype(BF16), "nn", "rhs01")


def _sb_scores(q_scaled, kb, row0, j):
    shape = (q_scaled.shape[0], SB_BLOCK)
    z = _mm_raw(q_scaled, kb, "nt", "bf16")
    q_pos = row0 + _iota2(shape, 0)
    k_pos = j * SB_BLOCK + _iota2(shape, 1)
    valid = (k_pos < q_pos) & (k_pos >= PAD)
    sp = jnp.maximum(z, 0.0) + jnp.log(1.0 + jnp.exp(-jnp.abs(z)))
    lk = jnp.where(valid, -sp, 0.0)
    return z, sp, valid, lk


SB_DEAD = -110.0


def sb_fwd(src, offs, ride=None):
    lp = src.shape[0]
    nh = N_HEADS
    qb = _pick(lp, SB_QROWS, 8)
    scale = HEAD_DIM ** -0.5
    blk = SB_BLOCK

    n_in = len(ride.ins) if ride else 0
    n_out = len(ride.out_shapes) if ride else 0

    def body(*refs):
        q_ref, k_ref, v_ref = refs[:3]
        o_ref = refs[3 + n_in]
        ride_refs = (refs[3:3 + n_in], refs[4 + n_in:4 + n_in + n_out], refs[4 + n_in + n_out:])
        i = pl.program_id(1)
        if ride:
            @pl.when((pl.program_id(0) == 0) & (i == 0))
            def _():
                ride.start(*ride_refs)
        q_scaled = q_ref[...] * scale
        upper = _iota2((blk, blk), 0) > _iota2((blk, blk), 1)
        n_blocks = ((i + 1) * qb + blk - 1) // blk

        def live(state):
            it, _, c = state
            return (it < n_blocks) & (jnp.max(c) > SB_DEAD)

        def step(state):
            it, acc, c = state
            j = n_blocks - 1 - it
            rows = pl.ds(pl.multiple_of(j * blk, blk), blk)
            z, sp, valid, lk = _sb_scores(q_scaled, k_ref[rows, :], i * qb, j)
            later = _mm_tri(lk, upper) + c
            w = jnp.where(valid, jnp.exp(z - sp + later), 0.0)
            acc = acc + _mm_raw(w, v_ref[rows, :], "nn", "bf16")
            return it + 1, acc, c + jnp.sum(lk, axis=1, keepdims=True)

        _, acc, _ = lax.while_loop(live, step, (jnp.int32(0), jnp.zeros((qb, HEAD_DIM), F32), jnp.zeros((qb, 1), F32)))
        o_ref[...] = acc
        if ride:
            @pl.when((pl.program_id(0) == nh - 1) & (i == lp // qb - 1))
            def _():
                ride.finish(*ride_refs)

    qspec = pl.BlockSpec((qb, HEAD_DIM), lambda h, i: (i, offs[0] + h))
    kspec = pl.BlockSpec((lp, HEAD_DIM), lambda h, i: (0, offs[1] + h))
    vspec = pl.BlockSpec((lp, HEAD_DIM), lambda h, i: (0, offs[2] + h))
    ospec = pl.BlockSpec((qb, HEAD_DIM), lambda h, i: (i, h))
    return pl.pallas_call(
        body, name="sb_fwd", grid=(nh, lp // qb), in_specs=[qspec, kspec, vspec] + [ANY] * n_in, out_specs=[ospec] + [ANY] * n_out,
        out_shape=[jax.ShapeDtypeStruct((lp, nh * HEAD_DIM), F32)] + (ride.out_shapes if ride else []),
        scratch_shapes=ride.sems if ride else [],
        input_output_aliases={3 + k: 1 + k for k in range(n_in)} if ride and ride.alias else {},
        compiler_params=pltpu.CompilerParams(dimension_semantics=("arbitrary", "arbitrary"), vmem_limit_bytes=VMEM_LIMIT),
    )(src, src, src, *(ride.ins if ride else []))


def sb_bwd(src, offs, d_o, ride=None):
    lp = src.shape[0]
    nh = N_HEADS
    qb = _pick(lp, SB_QROWS, 8)
    scale = HEAD_DIM ** -0.5
    blk = SB_BLOCK

    n_in = len(ride.ins) if ride else 0
    n_out = len(ride.out_shapes) if ride else 0

    def body(*refs):
        q_ref, k_ref, v_ref, do_ref = refs[:4]
        dq_ref, dk_out, dv_out = refs[4 + n_in:7 + n_in]
        ride_refs = (refs[4:4 + n_in], refs[7 + n_in:7 + n_in + n_out], refs[9 + n_in + n_out:])
        dk_ref, dv_ref = refs[7 + n_in + n_out:9 + n_in + n_out]
        i = pl.program_id(1)
        if ride:
            @pl.when((pl.program_id(0) == 0) & (i == 0))
            def _():
                ride.start(*ride_refs)

        @pl.when(i == 0)
        def _():
            dk_ref[...] = jnp.zeros_like(dk_ref)
            dv_ref[...] = jnp.zeros_like(dv_ref)

        q_scaled = q_ref[...] * scale
        d_out = do_ref[...]
        lower_incl = _iota2((blk, blk), 0) <= _iota2((blk, blk), 1)
        lower = _iota2((blk, blk), 0) < _iota2((blk, blk), 1)
        n_blocks = ((i + 1) * qb + blk - 1) // blk

        def live(state):
            it, c = state
            return (it < n_blocks) & (jnp.max(c) > SB_DEAD)

        def count(state):
            it, c = state
            rows = pl.ds(pl.multiple_of((n_blocks - 1 - it) * blk, blk), blk)
            _, _, _, lk = _sb_scores(q_scaled, k_ref[rows, :], i * qb, n_blocks - 1 - it)
            return it + 1, c + jnp.sum(lk, axis=1, keepdims=True)

        n_live, total = lax.while_loop(live, count, (jnp.int32(0), jnp.zeros((qb, 1), F32)))

        def step(j, carry):
            acc, cp, ep = carry
            rows = pl.ds(pl.multiple_of(j * blk, blk), blk)
            kb = k_ref[rows, :]
            vb = v_ref[rows, :]
            z, sp, valid, lk = _sb_scores(q_scaled, kb, i * qb, j)
            later = total - cp - _mm_tri(lk, lower_incl)
            w = jnp.where(valid, jnp.exp(z - sp + later), 0.0)
            e = w * _mm_raw(d_out, vb, "nt", "bf16")
            before = ep + _mm_tri(e, lower)
            dz = jnp.where(valid, e * jnp.exp(-sp) - before * jnp.exp(z - sp), 0.0)
            dk_ref[rows, :] += _mm_raw(dz, q_scaled, "tn", "bf16")
            dv_ref[rows, :] += _mm_raw(w, d_out, "tn", "bf16")
            acc = acc + _mm_raw(dz, kb, "nn", "bf16")
            return acc, cp + jnp.sum(lk, axis=1, keepdims=True), ep + jnp.sum(e, axis=1, keepdims=True)

        zero_col = jnp.zeros((qb, 1), F32)
        acc, _, _ = lax.fori_loop(n_blocks - n_live, n_blocks, step, (jnp.zeros((qb, HEAD_DIM), F32), zero_col, zero_col))
        dq_ref[...] = (acc * scale).astype(dq_ref.dtype)

        @pl.when(i == lp // qb - 1)
        def _():
            dk_out[...] = dk_ref[...].astype(dk_out.dtype)
            dv_out[...] = dv_ref[...].astype(dv_out.dtype)

        if ride:
            @pl.when((pl.program_id(0) == nh - 1) & (i == lp // qb - 1))
            def _():
                ride.finish(*ride_refs)

    qspec = pl.BlockSpec((qb, HEAD_DIM), lambda h, i: (i, offs[0] + h))
    kspec = pl.BlockSpec((lp, HEAD_DIM), lambda h, i: (0, offs[1] + h))
    vspec = pl.BlockSpec((lp, HEAD_DIM), lambda h, i: (0, offs[2] + h))
    ospec = pl.BlockSpec((qb, HEAD_DIM), lambda h, i: (i, h))
    fullspec = pl.BlockSpec((lp, HEAD_DIM), lambda h, i: (0, h))
    return pl.pallas_call(
        body, name="sb_bwd", grid=(nh, lp // qb), in_specs=[qspec, kspec, vspec, ospec] + [ANY] * n_in,
        out_specs=[ospec, fullspec, fullspec] + [ANY] * n_out,
        out_shape=[jax.ShapeDtypeStruct((lp, nh * HEAD_DIM), BF16)] * 3 + (ride.out_shapes if ride else []),
        scratch_shapes=[pltpu.VMEM((lp, HEAD_DIM), F32)] * 2 + (ride.sems if ride else []),
        input_output_aliases={4 + k: 3 + k for k in range(n_in)} if ride and ride.alias else {},
        compiler_params=pltpu.CompilerParams(dimension_semantics=("arbitrary", "arbitrary"), vmem_limit_bytes=VMEM_LIMIT),
    )(src, src, src, d_o, *(ride.ins if ride else []))


def _pick(n, target, unit):
    if n <= target:
        return n
    best = None
    for d in range(unit, target + 1, unit):
        if n % d == 0:
            best = d
    assert best is not None, (n, target, unit)
    return best


def matmul(a, b, mode="nn", *, name, bm=1088, bn=640, bk=2176, residual=None, out_dtype=F32, b_koff=0, ride=None):
    if mode == "nn":
        (m, k), n = a.shape, b.shape[1]
    elif mode == "nt":
        (m, k), n = a.shape, b.shape[0]
    else:
        (k, m), n = a.shape, b.shape[1]
    assert b_koff == 0 or mode == "nt"
    bm = _pick(m, bm, 128 if mode == "tn" else 8)
    bn = _pick(n, bn, 128 if mode != "nt" else 8)
    bk = _pick(k, bk, 128 if mode != "tn" else 8)
    nk = k // bk
    n_plain = 2 if residual is None else 3
    n_in = len(ride.ins) if ride else 0
    n_out = len(ride.out_shapes) if ride else 0
    steps = (m // bm, n // bn, nk)

    def body(*refs):
        a_ref, b_ref = refs[:2]
        r_ref = None if residual is None else refs[2]
        o_ref = refs[n_plain + n_in]
        acc = refs[n_plain + n_in + 1 + n_out]
        ride_refs = (refs[n_plain:n_plain + n_in], refs[n_plain + n_in + 1:n_plain + n_in + 1 + n_out], refs[n_plain + n_in + 2 + n_out:])
        kk = pl.program_id(2)
        here = [pl.program_id(d) for d in range(3)]
        if ride:
            @pl.when((here[0] == 0) & (here[1] == 0) & (here[2] == 0))
            def _():
                ride.start(*ride_refs)
        part = _mm_raw(a_ref[...], b_ref[...], mode, "bf16")

        @pl.when(kk == 0)
        def _():
            acc[...] = part

        @pl.when(kk > 0)
        def _():
            acc[...] += part

        @pl.when(kk == nk - 1)
        def _():
            res = acc[...]
            if r_ref is not None:
                res = res + r_ref[...]
            o_ref[...] = res.astype(out_dtype)

        if ride:
            @pl.when((here[0] == steps[0] - 1) & (here[1] == steps[1] - 1) & (here[2] == steps[2] - 1))
            def _():
                ride.finish(*ride_refs)

    a_spec = pl.BlockSpec((bk, bm), lambda i, j, kk: (kk, i)) if mode == "tn" else pl.BlockSpec((bm, bk), lambda i, j, kk: (i, kk))
    b_spec = pl.BlockSpec((bn, bk), lambda i, j, kk: (j, b_koff + kk)) if mode == "nt" else pl.BlockSpec((bk, bn), lambda i, j, kk: (kk, j))
    o_spec = pl.BlockSpec((bm, bn), lambda i, j, kk: (i, j))
    ins, specs = [a, b], [a_spec, b_spec]
    if residual is not None:
        ins.append(residual)
        specs.append(o_spec)
    res = pl.pallas_call(
        body, name=name, grid=steps, in_specs=specs + [ANY] * n_in, out_specs=[o_spec] + [ANY] * n_out,
        out_shape=[jax.ShapeDtypeStruct((m, n), out_dtype)] + (ride.out_shapes if ride else []),
        scratch_shapes=[pltpu.VMEM((bm, bn), F32)] + (ride.sems if ride else []),
        compiler_params=pltpu.CompilerParams(dimension_semantics=("arbitrary", "arbitrary", "arbitrary"), vmem_limit_bytes=VMEM_LIMIT),
    )(*ins, *(ride.ins if ride else []))
    return res if ride else res[0]


def _row_specs(rows, params, width, bm):
    row_specs = [pl.BlockSpec((bm, width), (lambda j, i, off=off: (i, off + j))) for _, off in rows]
    par_specs = [pl.BlockSpec((p.shape[0], width) if per_col else p.shape, ((lambda j, i: (0, j)) if per_col else (lambda j, i: (0, 0))))
                 for p, per_col in params]
    return row_specs, par_specs


def rowmap_fwd(name, fn, rows, params, n_out, *, width, ncol, bm, lp, out_dtypes=None):
    out_dtypes = out_dtypes or [F32] * n_out
    row_specs, par_specs = _row_specs(rows, params, width, bm)
    nr = len(rows)

    def body(*refs):
        ins, outs = refs[:nr + len(params)], refs[nr + len(params):]
        row_ids = pl.program_id(1) * bm + _iota2((bm, 1), 0)
        res = fn(row_ids, *[r[...].astype(F32) for r in ins])
        for o_ref, val in zip(outs, res):
            o_ref[...] = val.astype(o_ref.dtype)

    o_spec = pl.BlockSpec((bm, width), lambda j, i: (i, j))
    return pl.pallas_call(
        body, name=name, grid=(ncol, lp // bm), in_specs=row_specs + par_specs, out_specs=[o_spec] * n_out,
        out_shape=[jax.ShapeDtypeStruct((lp, ncol * width), dt) for dt in out_dtypes],
        compiler_params=pltpu.CompilerParams(dimension_semantics=("arbitrary", "arbitrary"), vmem_limit_bytes=VMEM_LIMIT),
    )(*[a for a, _ in rows], *[p for p, _ in params])


def rowmap_bwd(name, fn, rows, params, d_outs, *, width, ncol, bm, lp, d_row_dtypes=None):
    d_row_dtypes = d_row_dtypes or [F32] * len(rows)
    row_specs, par_specs = _row_specs(rows, params, width, bm)
    nr, npar, nout = len(rows), len(params), len(d_outs)

    def body(*refs):
        ins = refs[:nr + npar]
        dos = refs[nr + npar:nr + npar + nout]
        d_rows = refs[nr + npar + nout:nr + npar + nout + nr]
        d_pars = refs[nr + npar + nout + nr:]
        j, i = pl.program_id(0), pl.program_id(1)
        row_ids = i * bm + _iota2((bm, 1), 0)
        _, pull = jax.vjp(lambda *xs: tuple(fn(row_ids, *xs)), *[r[...].astype(F32) for r in ins])
        grads = pull(tuple(d[...].astype(F32) for d in dos))
        for ref, val in zip(d_rows, grads[:nr]):
            ref[...] = val.astype(ref.dtype)
        for ref, val, (_, per_col) in zip(d_pars, grads[nr:], params):
            first = (i == 0) if per_col else ((i == 0) & (j == 0))

            @pl.when(first)
            def _(ref=ref, val=val):
                ref[...] = val

            @pl.when(jnp.logical_not(first))
            def _(ref=ref, val=val):
                ref[...] += val

    o_spec = pl.BlockSpec((bm, width), lambda j, i: (i, j))
    res = pl.pallas_call(
        body, name=name, grid=(ncol, lp // bm), in_specs=row_specs + par_specs + [o_spec] * nout,
        out_specs=[o_spec] * nr + par_specs,
        out_shape=[jax.ShapeDtypeStruct((lp, ncol * width), dt) for dt in d_row_dtypes]
        + [jax.ShapeDtypeStruct(p.shape, F32) for p, _ in params],
        compiler_params=pltpu.CompilerParams(dimension_semantics=("arbitrary", "arbitrary"), vmem_limit_bytes=VMEM_LIMIT),
    )(*[a for a, _ in rows], *[p for p, _ in params], *d_outs)
    return res[:nr], res[nr:]


def _silu(x):
    return x * jax.nn.sigmoid(x)


def _softplus(x):
    return jnp.maximum(x, 0.0) + jnp.log(1.0 + jnp.exp(-jnp.abs(x)))


def _real_rows(row_ids):
    return (row_ids >= PAD).astype(F32)


def _f_rmsnorm(row_ids, h, g):
    return (h * lax.rsqrt(jnp.mean(h * h, axis=-1, keepdims=True) + RMS_EPS) * g,)


def _f_small_gates(row_ids, small, bias, a_log):
    lane = _iota2(small.shape, 1)
    t = small + bias
    sp = _softplus(t)
    coef = -jnp.exp(a_log)
    keep = _real_rows(row_ids)
    first = jnp.where(lane < 8, jax.nn.sigmoid(t), jnp.where(lane < 16, coef * sp, jnp.where(lane < 48, sp, 0.0)))
    second = jnp.where((lane >= 8) & (lane < 48), coef * sp, 0.0)
    return first * keep, second * keep


def _f_gate_silu(row_ids, o, z):
    return (o * _silu(z),)


def _f_head_norm_gate(row_ids, o, z, g):
    out = []
    for h in range(o.shape[1] // HEAD_DIM):
        oh = o[:, h * HEAD_DIM:(h + 1) * HEAD_DIM]
        out.append(oh * lax.rsqrt(jnp.mean(oh * oh, axis=-1, keepdims=True) + RMS_EPS) * g)
    return (jnp.concatenate(out, axis=1) * _silu(z),)


def _f_ssm_out(row_ids, y, xh, z, d_skip, g):
    t = (y + d_skip * xh) * _silu(z)
    return (t * lax.rsqrt(jnp.mean(t * t, axis=-1, keepdims=True) + RMS_EPS) * g,)


def _f_merge(row_ids, pa, pb, pc, ga, gb, gc):
    return (jax.nn.sigmoid(ga) * pa + jax.nn.sigmoid(gb) * pb + jax.nn.sigmoid(gc) * pc,)


def _shift_rows(x, s):
    s = s % x.shape[0]
    return x if s == 0 else pltpu.roll(x, s, 0)


def _conv_pre(x, w, b):
    pre = b
    for kk in range(CONV_K):
        pre = pre + w[kk:kk + 1, :] * _shift_rows(x, CONV_K - 1 - kk)
    return pre


def _conv_post(pre, l2_flag, keep):
    act = _silu(pre)
    nrm = act * lax.rsqrt(jnp.sum(act * act, axis=-1, keepdims=True) + L2_EPS)
    return (l2_flag * nrm + (1.0 - l2_flag) * act) * keep


def conv_fwd(name, src, col_off, w, b, n_l2):
    lp, ch = src.shape[0], w.shape[1]

    def body(x_ref, w_ref, b_ref, o_ref):
        l2_flag = (pl.program_id(0) < n_l2).astype(F32)
        keep = _real_rows(_iota2((lp, 1), 0))
        o_ref[...] = _conv_post(_conv_pre(x_ref[...], w_ref[...], b_ref[...]), l2_flag, keep)

    return pl.pallas_call(
        body, name=name, grid=(ch // HEAD_DIM,),
        in_specs=[pl.BlockSpec((lp, HEAD_DIM), lambda j: (0, col_off + j)), pl.BlockSpec((CONV_K, HEAD_DIM), lambda j: (0, j)),
                  pl.BlockSpec((1, HEAD_DIM), lambda j: (0, j))],
        out_specs=pl.BlockSpec((lp, HEAD_DIM), lambda j: (0, j)),
        out_shape=jax.ShapeDtypeStruct((lp, ch), F32),
        compiler_params=pltpu.CompilerParams(dimension_semantics=("arbitrary",), vmem_limit_bytes=VMEM_LIMIT),
    )(src, w, b)


def conv_bwd(name, src, col_off, w, b, n_l2, d_out):
    lp, ch = src.shape[0], w.shape[1]

    def body(x_ref, w_ref, b_ref, do_ref, dx_ref, dw_ref, db_ref):
        l2_flag = (pl.program_id(0) < n_l2).astype(F32)
        keep = _real_rows(_iota2((lp, 1), 0))
        x, wv = x_ref[...], w_ref[...]
        pre = _conv_pre(x, wv, b_ref[...])
        _, pull = jax.vjp(lambda p: _conv_post(p, l2_flag, keep), pre)
        (d_pre,) = pull(do_ref[...])
        dx = jnp.zeros_like(x)
        for kk in range(CONV_K):
            s = CONV_K - 1 - kk
            dx = dx + wv[kk:kk + 1, :] * _shift_rows(d_pre, -s)
            dw_ref[kk:kk + 1, :] = jnp.sum(d_pre * _shift_rows(x, s), axis=0, keepdims=True)
        dx_ref[...] = (dx * keep).astype(dx_ref.dtype)
        db_ref[...] = jnp.sum(d_pre, axis=0, keepdims=True)

    seq = pl.BlockSpec((lp, HEAD_DIM), lambda j: (0, j))
    wspec = pl.BlockSpec((CONV_K, HEAD_DIM), lambda j: (0, j))
    bspec = pl.BlockSpec((1, HEAD_DIM), lambda j: (0, j))
    return pl.pallas_call(
        body, name=name, grid=(ch // HEAD_DIM,),
        in_specs=[pl.BlockSpec((lp, HEAD_DIM), lambda j: (0, col_off + j)), wspec, bspec, seq],
        out_specs=[seq, wspec, bspec],
        out_shape=[jax.ShapeDtypeStruct((lp, ch), BF16), jax.ShapeDtypeStruct(w.shape, F32), jax.ShapeDtypeStruct(b.shape, F32)],
        compiler_params=pltpu.CompilerParams(dimension_semantics=("arbitrary",), vmem_limit_bytes=VMEM_LIMIT),
    )(src, w, b, d_out)


def loss_head(h, target, g):
    lp, d = h.shape
    bm = SB_BLOCK
    first = (PAD + N_META) // bm

    def body(h_ref, t_ref, g_ref, loss_ref, dh_ref, dg_ref):
        i = pl.program_id(0)
        keep = (i >= first).astype(F32)

        def f(hv, gv):
            y = hv * lax.rsqrt(jnp.mean(hv * hv, axis=-1, keepdims=True) + RMS_EPS) * gv
            err = y - t_ref[...]
            return 0.5 * jnp.sum(jnp.mean(err * err, axis=-1, keepdims=True), axis=0, keepdims=True) * keep

        val, pull = jax.vjp(f, h_ref[...], g_ref[...])
        dh, dg = pull(jnp.ones((1, 1), F32))
        dh_ref[...] = dh

        @pl.when(i == 0)
        def _():
            loss_ref[...] = val
            dg_ref[...] = dg

        @pl.when(i > 0)
        def _():
            loss_ref[...] += val
            dg_ref[...] += dg

    row = pl.BlockSpec((bm, d), lambda i: (i, 0))
    return pl.pallas_call(
        body, name="loss_head", grid=(lp // bm,),
        in_specs=[row, pl.BlockSpec((bm, d), lambda i: (jnp.maximum(i - first, 0), 0)), pl.BlockSpec((1, d), lambda i: (0, 0))],
        out_specs=[pl.BlockSpec((1, 1), lambda i: (0, 0)), row, pl.BlockSpec((1, d), lambda i: (0, 0))],
        out_shape=[jax.ShapeDtypeStruct((1, 1), F32), jax.ShapeDtypeStruct((lp, d), F32), jax.ShapeDtypeStruct((1, d), F32)],
        compiler_params=pltpu.CompilerParams(dimension_semantics=("arbitrary",)),
    )(h, target, g)


ADAM_LR, ADAM_B1, ADAM_B2, ADAM_EPS, ADAM_WD, ADAM_STEP = 0.001, 0.9, 0.999, 1e-08, 0.01, 10


def adamw(name, w, g, m, v, echo_g=False):
    lead = w.shape[:-2]
    rows, cols = w.shape[-2:]
    br = _pick(rows, 128, 8)
    n_out = 4 if echo_g else 3

    def body(w_ref, g_ref, m_ref, v_ref, d_ref, nm_ref, nv_ref, *echo):
        gv = g_ref[...]
        nm = ADAM_B1 * m_ref[...] + (1.0 - ADAM_B1) * gv
        nv = ADAM_B2 * v_ref[...] + (1.0 - ADAM_B2) * (gv * gv)
        m_hat = nm / (1.0 - ADAM_B1 ** ADAM_STEP)
        v_hat = nv / (1.0 - ADAM_B2 ** ADAM_STEP)
        d_ref[...] = -ADAM_LR * (m_hat / (jnp.sqrt(v_hat) + ADAM_EPS) + ADAM_WD * w_ref[...])
        nm_ref[...] = nm
        nv_ref[...] = nv
        for e_ref in echo:
            e_ref[...] = gv

    if lead and rows <= 8:
        bl = _pick(lead[0], 32, 1)
        spec = pl.BlockSpec((bl, rows, cols), lambda s: (s, 0, 0))
        grid = (lead[0] // bl,)
    elif lead:
        spec = pl.BlockSpec((None, br, cols), lambda s, i: (s, i, 0))
        grid = (lead[0], rows // br)
    else:
        spec = pl.BlockSpec((br, cols), lambda i: (i, 0))
        grid = (rows // br,)
    return pl.pallas_call(
        body, name=name, grid=grid, in_specs=[spec] * 4, out_specs=[spec] * n_out,
        out_shape=[jax.ShapeDtypeStruct(w.shape, F32)] * n_out,
        compiler_params=pltpu.CompilerParams(dimension_semantics=("arbitrary",) * len(grid), vmem_limit_bytes=VMEM_LIMIT),
    )(w, g, m, v)


MESH = pl.DeviceIdType.MESH
ANY = pl.BlockSpec(memory_space=pl.ANY)
D2D_PIECES = 16
ICI_PIECES = 4


def _place():
    x, y, c = lax.axis_index("x"), lax.axis_index("y"), lax.axis_index("c")
    return x, y, c, [(1 - x, y), (x, 1 - y), (1 - x, 1 - y)]


def _pieces(rows, n, unit):
    per = -(-rows // (n * unit)) * unit
    return [(s, min(per, rows - s)) for s in range(0, rows, per)]


def _row_unit(dtype):
    return 16 if dtype == BF16 else 8


def _scalar(v):
    return jnp.reshape(v, (1,)).astype(jnp.int32)


def place_shard(name, pack):
    rows, cols = pack.shape
    br = _pick(rows, 256, 16)

    def body(m_ref, p_ref, o_ref):
        o_ref[...] = p_ref[...]

    return pl.pallas_call(
        body, name=name,
        grid_spec=pltpu.PrefetchScalarGridSpec(
            num_scalar_prefetch=1, grid=(rows // br,),
            in_specs=[pl.BlockSpec((br, cols), lambda i, m: (i, 0))],
            out_specs=pl.BlockSpec((None, br, cols), lambda i, m: (m[0], i, 0))),
        out_shape=jax.ShapeDtypeStruct((4, rows, cols), pack.dtype),
        compiler_params=pltpu.CompilerParams(dimension_semantics=("arbitrary",), vmem_limit_bytes=VMEM_LIMIT),
    )(_scalar(2 * lax.axis_index("x") + lax.axis_index("y")), pack)


class Ride:
    def __init__(self, ins, out_shapes, alias, sems, start, finish):
        self.ins, self.out_shapes, self.alias, self.sems, self.start, self.finish = list(ins), out_shapes, alias, sems, start, finish


def _gather_parts(o_refs, send_sems, recv_sems):
    x, y, c, chips = _place()
    mine = 2 * x + y

    def half_rows(b, which, start=0, size=None):
        half = o_refs[b].shape[1] // 2
        return pl.ds(pl.multiple_of(which * half + start, _row_unit(o_refs[b].dtype)), half if size is None else size)

    def remote(b, k, slot, rws, to):
        piece = o_refs[b].at[slot, rws, :]
        return pltpu.make_async_remote_copy(src_ref=piece, dst_ref=piece, send_sem=send_sems.at[b, k], recv_sem=recv_sems.at[b, k],
                                            device_id=to, device_id_type=MESH)

    return x, y, c, chips, mine, half_rows, remote


def _gather_start(o_refs, send_sems, recv_sems):
    x, y, c, chips, mine, half_rows, remote = _gather_parts(o_refs, send_sems, recv_sems)
    for b, o_ref in enumerate(o_refs):
        for j, (cx, cy) in enumerate(chips):
            for start, size in _pieces(o_ref.shape[1] // 2, ICI_PIECES, _row_unit(o_ref.dtype)):
                remote(b, j, mine, half_rows(b, c, start, size), (cx, cy, c)).start()


def _gather_finish(o_refs, send_sems, recv_sems):
    x, y, c, chips, mine, half_rows, remote = _gather_parts(o_refs, send_sems, recv_sems)
    sends = []
    for b, o_ref in enumerate(o_refs):
        for j, (cx, cy) in enumerate(chips):
            slot = 2 * cx + cy
            sends.append(remote(b, j, mine, half_rows(b, c), (cx, cy, c)))
            remote(b, j, slot, half_rows(b, c), (cx, cy, c)).wait_recv()
            for start, size in _pieces(o_ref.shape[1] // 2, D2D_PIECES, _row_unit(o_ref.dtype)):
                remote(b, 3 + j, slot, half_rows(b, c, start, size), (x, y, 1 - c)).start()
            sends.append(remote(b, 3 + j, slot, half_rows(b, c), (x, y, 1 - c)))
    for b in range(len(o_refs)):
        for j, (cx, cy) in enumerate(chips):
            remote(b, 3 + j, 2 * cx + cy, half_rows(b, 1 - c), (x, y, 1 - c)).wait_recv()
    for cp in sends:
        cp.wait_send()


def gather_ride(placed):
    n = len(placed)
    return Ride(placed, [jax.ShapeDtypeStruct(p.shape, p.dtype) for p in placed], True,
                [pltpu.SemaphoreType.DMA((n, 6)), pltpu.SemaphoreType.DMA((n, 6))],
                lambda ins, outs, sems: _gather_start(outs, *sems), lambda ins, outs, sems: _gather_finish(outs, *sems))


def gather_shards(name, placed):
    n = len(placed)

    def body(*refs):
        o_refs, sems = refs[n:2 * n], refs[2 * n:]
        _gather_start(o_refs, *sems)
        _gather_finish(o_refs, *sems)

    return pl.pallas_call(
        body, name=name, in_specs=[ANY] * n, out_specs=[ANY] * n,
        out_shape=[jax.ShapeDtypeStruct(p.shape, p.dtype) for p in placed],
        input_output_aliases={i: i for i in range(n)},
        scratch_shapes=[pltpu.SemaphoreType.DMA((n, 6)), pltpu.SemaphoreType.DMA((n, 6))],
    )(*placed)


def _split_copies(g_refs, t_refs, send_sems, recv_sems, start):
    x, y, c, _ = _place()
    waits = []
    for b, (g_ref, t_ref) in enumerate(zip(g_refs, t_refs)):
        half = g_ref.shape[2]

        def copy(slots, first, size):
            rws = pl.ds(first, size)
            return pltpu.make_async_remote_copy(src_ref=g_ref.at[slots, 1 - c, rws, :], dst_ref=t_ref.at[slots, rws, :],
                                                send_sem=send_sems.at[b], recv_sem=recv_sems.at[b], device_id=(x, y, 1 - c),
                                                device_id_type=MESH)

        if start:
            for s in range(4):
                for first, size in _pieces(half, D2D_PIECES // 4, _row_unit(g_ref.dtype)):
                    copy(s, first, size).start()
        else:
            waits.append(copy(slice(None), 0, half))
    return waits


def _split_finish(g_refs, t_refs, send_sems, recv_sems):
    for cp in _split_copies(g_refs, t_refs, send_sems, recv_sems, False):
        cp.wait()


def _split_shapes(bufs):
    return [jax.ShapeDtypeStruct((4,) + g.shape[2:], g.dtype) for g in bufs]


def split_ride(bufs):
    n = len(bufs)
    return Ride(bufs, _split_shapes(bufs), False, [pltpu.SemaphoreType.DMA((n,)), pltpu.SemaphoreType.DMA((n,))],
                lambda ins, outs, sems: _split_copies(ins, outs, *sems, True), lambda ins, outs, sems: _split_finish(ins, outs, *sems))


def pair_split(name, bufs):
    n = len(bufs)

    def body(*refs):
        g_refs, t_refs, sems = refs[:n], refs[n:2 * n], refs[2 * n:]
        _split_copies(g_refs, t_refs, *sems, True)
        _split_finish(g_refs, t_refs, *sems)

    return pl.pallas_call(
        body, name=name, in_specs=[ANY] * n, out_specs=[ANY] * n, out_shape=_split_shapes(bufs),
        scratch_shapes=[pltpu.SemaphoreType.DMA((n,)), pltpu.SemaphoreType.DMA((n,))],
    )(*bufs)


def pair_add(name, g, theirs, transit):
    _, _, half, cols = g.shape
    br = _pick(half, 128, 16)

    def body(c_ref, g_ref, t_ref, o_ref):
        o_ref[...] = (g_ref[...].astype(F32) + t_ref[...].astype(F32)).astype(transit)

    blk = (4, br, cols)
    return pl.pallas_call(
        body, name=name,
        grid_spec=pltpu.PrefetchScalarGridSpec(
            num_scalar_prefetch=1, grid=(half // br,),
            in_specs=[pl.BlockSpec((4, None, br, cols), lambda i, c: (0, c[0], i, 0)), pl.BlockSpec(blk, lambda i, c: (0, i, 0))],
            out_specs=pl.BlockSpec(blk, lambda i, c: (0, i, 0))),
        out_shape=jax.ShapeDtypeStruct((4, half, cols), transit),
        compiler_params=pltpu.CompilerParams(dimension_semantics=("arbitrary",), vmem_limit_bytes=VMEM_LIMIT),
    )(_scalar(lax.axis_index("c")), g, theirs)


def _exchange_copies(a_refs, o_refs, send_sems, recv_sems, start):
    x, y, c, chips = _place()
    mine = 2 * x + y
    waits = []
    for b, (a_ref, o_ref) in enumerate(zip(a_refs, o_refs)):
        rows = a_ref.shape[1]
        for j, (cx, cy) in enumerate(chips):
            def copy(first, size):
                rws = pl.ds(first, size)
                return pltpu.make_async_remote_copy(src_ref=a_ref.at[2 * cx + cy, rws, :], dst_ref=o_ref.at[mine, rws, :],
                                                    send_sem=send_sems.at[b, j], recv_sem=recv_sems.at[b, j],
                                                    device_id=(cx, cy, c), device_id_type=MESH)
            if start:
                for first, size in _pieces(rows, ICI_PIECES, _row_unit(a_ref.dtype)):
                    copy(first, size).start()
            else:
                waits.append(copy(0, rows))
    return waits


def _exchange_finish(a_refs, o_refs, send_sems, recv_sems):
    for cp in _exchange_copies(a_refs, o_refs, send_sems, recv_sems, False):
        cp.wait()


def exchange_ride(parts):
    n = len(parts)
    return Ride(parts, [jax.ShapeDtypeStruct(a.shape, a.dtype) for a in parts], False,
                [pltpu.SemaphoreType.DMA((n, 3)), pltpu.SemaphoreType.DMA((n, 3))],
                lambda ins, outs, sems: _exchange_copies(ins, outs, *sems, True), lambda ins, outs, sems: _exchange_finish(ins, outs, *sems))


def chip_exchange(name, parts):
    n = len(parts)

    def body(*refs):
        a_refs, o_refs, sems = refs[:n], refs[n:2 * n], refs[2 * n:]
        _exchange_copies(a_refs, o_refs, *sems, True)
        _exchange_finish(a_refs, o_refs, *sems)

    return pl.pallas_call(
        body, name=name, in_specs=[ANY] * n, out_specs=[ANY] * n,
        out_shape=[jax.ShapeDtypeStruct(a.shape, a.dtype) for a in parts],
        scratch_shapes=[pltpu.SemaphoreType.DMA((n, 3)), pltpu.SemaphoreType.DMA((n, 3))],
    )(*parts)


def chip_add(name, got, part):
    _, rows, cols = got.shape
    br = _pick(rows, 128, 16)
    nblk = rows // br

    def body(m_ref, c_ref, got_ref, part_ref, o_ref):
        mine = m_ref[0]
        for s in range(4):
            @pl.when(mine == s)
            def _(s=s):
                val = part_ref[...].astype(F32)
                o_ref[...] = val if s == 0 else o_ref[...] + val

            @pl.when(mine != s)
            def _(s=s):
                val = got_ref[s].astype(F32)
                o_ref[...] = val if s == 0 else o_ref[...] + val

    return pl.pallas_call(
        body, name=name,
        grid_spec=pltpu.PrefetchScalarGridSpec(
            num_scalar_prefetch=2, grid=(nblk,),
            in_specs=[pl.BlockSpec((4, br, cols), lambda i, m, c: (0, i, 0)),
                      pl.BlockSpec((None, br, cols), lambda i, m, c: (m[0], i, 0))],
            out_specs=pl.BlockSpec((br, cols), lambda i, m, c: (c[0] * nblk + i, 0))),
        out_shape=jax.ShapeDtypeStruct((2 * rows, cols), F32),
        compiler_params=pltpu.CompilerParams(dimension_semantics=("arbitrary",), vmem_limit_bytes=VMEM_LIMIT),
    )(_scalar(2 * lax.axis_index("x") + lax.axis_index("y")), _scalar(lax.axis_index("c")), got, part)


def pair_join(name, fulls):
    n = len(fulls)

    def body(*refs):
        o_refs = refs[n:2 * n]
        send_sems, recv_sems = refs[2 * n:]
        x, y, c, _ = _place()
        waits = []
        for b, o_ref in enumerate(o_refs):
            half = o_ref.shape[0] // 2
            unit = _row_unit(o_ref.dtype)

            def copy(start, size):
                piece = o_ref.at[pl.ds(pl.multiple_of(c * half + start, unit), size), :]
                return pltpu.make_async_remote_copy(src_ref=piece, dst_ref=piece, send_sem=send_sems.at[b], recv_sem=recv_sems.at[b],
                                                    device_id=(x, y, 1 - c), device_id_type=MESH)

            for start, size in _pieces(half, D2D_PIECES, unit):
                copy(start, size).start()
            waits.append(copy(0, half))
        for cp in waits:
            cp.wait()

    return pl.pallas_call(
        body, name=name, in_specs=[ANY] * n, out_specs=[ANY] * n,
        out_shape=[jax.ShapeDtypeStruct(f.shape, f.dtype) for f in fulls],
        input_output_aliases={i: i for i in range(n)},
        scratch_shapes=[pltpu.SemaphoreType.DMA((n,)), pltpu.SemaphoreType.DMA((n,))],
    )(*fulls)


D_IN = 15920
D_PROJ = 16000
_SEGMENTS = ((0, 8192), (8208, 12816), (12848, 15920), (8192, 8208), (12816, 12848))
OFF_SB_Z, OFF_GDN_QKV, OFF_GDN_Z, OFF_SSM_Z, OFF_SSM_XBC, OFF_GATES, OFF_SMALL = 3072, 4096, 7168, 8192, 10240, 12800, 15872
PACK_C = 1024
WEIGHTS = ("meta_tokens", "norm_g", "w_in", "gdn_conv_w", "gdn_a_log", "gdn_dt_bias", "gdn_norm_g", "ssm_conv_w", "ssm_conv_b",
           "ssm_a_log", "ssm_dt_bias", "ssm_d", "ssm_norm_g", "w_branch_a", "w_branch_b", "w_branch_c", "w_out", "final_norm_g")
SHARDED = ("w_in", "w_branch_a", "w_branch_b", "w_branch_c", "w_out", "gdn_conv_w", "ssm_conv_w", "meta_tokens")
SHARD_AXIS = {"w_in": 2, "w_branch_a": 1, "w_branch_b": 1, "w_branch_c": 1, "w_out": 1, "gdn_conv_w": 2, "ssm_conv_w": 2, "meta_tokens": 1}
BRANCH = ("w_branch_a", "w_branch_b", "w_branch_c", "w_out")
EXACT = ("gdn_conv_w", "ssm_conv_w", "meta_tokens")
REPLICATED = tuple(n for n in WEIGHTS if n not in SHARDED)


def _regrouped_from_shards(shard_cols):
    out = []
    for a, b in _SEGMENTS:
        while a < b:
            chip = a // shard_cols
            stop = min(b, (chip + 1) * shard_cols)
            out.append((chip, a - chip * shard_cols, stop - chip * shard_cols))
            a = stop
    return out


def _shard_from_regrouped(chip, shard_cols):
    lo, hi = chip * shard_cols, (chip + 1) * shard_cols
    out, pos = [], 0
    starts = {}
    for a, b in _SEGMENTS:
        starts[(a, b)] = pos
        pos += b - a
    for a, b in sorted(_SEGMENTS):
        s0, s1 = max(a, lo), min(b, hi)
        if s0 < s1:
            out.append((starts[(a, b)] + s0 - a, starts[(a, b)] + s1 - a))
    return out


def _pack(parts, row_unit=64):
    n = sum(p.shape[0] for p in parts)
    rows = -(-n // (PACK_C * row_unit)) * row_unit
    flat = jnp.concatenate(list(parts) + [jnp.zeros((rows * PACK_C - n,), parts[0].dtype)])
    return flat.reshape(rows, PACK_C)


def _unpack(buf, shapes):
    flat, out, pos = buf.reshape(-1), [], 0
    for shp in shapes:
        n = math.prod(shp)
        out.append(flat[pos:pos + n].reshape(shp))
        pos += n
    return out


def _as_bf16_words(a):
    return lax.bitcast_convert_type(a, BF16).reshape(-1)


BRANCH_ROWS = (D_MODEL // 4, D_MODEL // 4, SSM_INNER // 4, D_MODEL // 4)


def _place_weights(w):
    depth = w["w_in"].shape[0]
    layers = []
    for l in range(depth):
        a = w["w_in"][l].astype(BF16)
        b = jnp.concatenate([w[n][l] for n in BRANCH], axis=0).astype(BF16)
        layers.append([place_shard("place_w_in", a), place_shard("place_branch", b)])
    small = place_shard("place_exact", _pack([_as_bf16_words(w[n]) for n in EXACT]))
    return layers, small


def _exact_weights(w, got_s):
    per_chip = [_unpack(got_s[c], [w[n].shape + (2,) for n in EXACT]) for c in range(4)]
    return {n: jnp.concatenate([lax.bitcast_convert_type(per_chip[c][i], F32) for c in range(4)], axis=SHARD_AXIS[n])
            for i, n in enumerate(EXACT)}


def _proj_weight(got_a):
    d_model, shard_cols = got_a.shape[1:]
    pad = jnp.zeros((d_model, D_PROJ - D_IN), BF16)
    return jnp.concatenate([got_a[c, :, lo:hi] for c, lo, hi in _regrouped_from_shards(shard_cols)] + [pad], axis=1)


def _branch_weights(got_b):
    out, pos = {}, 0
    for n, rows in zip(("wa", "wb", "wc", "wo"), BRANCH_ROWS):
        out[n] = jnp.concatenate([got_b[c, pos:pos + rows] for c in range(4)], axis=0)
        pos += rows
    return out


def _shard(a, axis, s):
    size = a.shape[axis] // 4
    return lax.slice_in_dim(a, s * size, (s + 1) * size, axis=axis)


def _layer_grad_buffers(g, shard_cols):
    buf_a = jnp.stack([jnp.concatenate([g["w_in"][:, lo:hi] for lo, hi in _shard_from_regrouped(s, shard_cols)], axis=1)
                       for s in range(4)]).astype(BF16)
    buf_b = jnp.stack([jnp.concatenate([_shard(g[n], 0, s) for n in BRANCH], axis=0) for s in range(4)]).astype(BF16)
    return [buf_a, buf_b]


def _split_halves(buf):
    return buf.reshape(4, 2, buf.shape[1] // 2, buf.shape[2])


def _start_reduce(tag, bufs, transits):
    bufs = [_split_halves(g) for g in bufs]
    theirs = pair_split(tag + "_pair_split", bufs)
    return [pair_add(f"{tag}_pair_add_{i}", g, t, tr) for i, (g, t, tr) in enumerate(zip(bufs, theirs, transits))]


def _layer_params(w, exact, wp, l):
    lane = lambda v, lo: jnp.pad(v, (lo, HEAD_DIM - lo - v.shape[0]))[None]
    return dict(
        norm_g=w["norm_g"][l][None], wp=wp,
        gdn_conv_w=exact["gdn_conv_w"][l], gdn_conv_b=jnp.zeros((1, 3 * N_HEADS * HEAD_DIM), F32),
        ssm_conv_w=exact["ssm_conv_w"][l], ssm_conv_b=w["ssm_conv_b"][l][None],
        bias_vec=lane(w["gdn_dt_bias"][l], 8) + lane(w["ssm_dt_bias"][l], 16),
        alog_vec=lane(w["gdn_a_log"][l], 8) + lane(w["ssm_a_log"][l], 16),
        gdn_norm_g=w["gdn_norm_g"][l][None], d_skip=jnp.repeat(w["ssm_d"][l], SSM_P)[None], ssm_norm_g=w["ssm_norm_g"][l][None])


def _layer_fwd(h, p, ride):
    lp = h.shape[0]
    bm = _pick(lp, 544, 16)
    kw = dict(bm=bm, lp=lp)
    (u,) = rowmap_fwd("rms_fwd", _f_rmsnorm, [(h, 0)], [(p["norm_g"], False)], 1, width=D_MODEL, ncol=1, out_dtypes=[BF16], **kw)
    proj = matmul(u, p["wp"], "nn", name="proj", bm=lp, bn=640)
    o_a_raw, *rode = sb_fwd(proj, (0, N_HEADS, 2 * N_HEADS), ride)
    p = dict(p, **_branch_weights(rode[0]))
    qkv = conv_fwd("gdn_conv_fwd", proj, OFF_GDN_QKV // HEAD_DIM, p["gdn_conv_w"], p["gdn_conv_b"], 2 * N_HEADS)
    first, second = rowmap_fwd("gates_fwd", _f_small_gates, [(proj, OFF_SMALL // HEAD_DIM)],
                               [(p["bias_vec"], False), (p["alog_vec"], False)], 2, width=HEAD_DIM, ncol=1, **kw)
    o_b_raw, gdn_states, gdn_t = gdn_fwd(qkv, first)
    xbc = conv_fwd("ssm_conv_fwd", proj, OFF_SSM_XBC // HEAD_DIM, p["ssm_conv_w"], p["ssm_conv_b"], 0)
    y_raw, ssd_states = ssd_fwd(xbc, first, second)
    (o_a,) = rowmap_fwd("gate_a_fwd", _f_gate_silu, [(o_a_raw, 0), (proj, OFF_SB_Z // 1024)], [], 1, width=1024, ncol=1,
                        out_dtypes=[BF16], **kw)
    (o_b,) = rowmap_fwd("gate_b_fwd", _f_head_norm_gate, [(o_b_raw, 0), (proj, OFF_GDN_Z // 1024)], [(p["gdn_norm_g"], False)], 1,
                        width=1024, ncol=1, out_dtypes=[BF16], **kw)
    (o_c,) = rowmap_fwd("gate_c_fwd", _f_ssm_out, [(y_raw, 0), (xbc, 0), (proj, OFF_SSM_Z // 1024)],
                        [(p["d_skip"], True), (p["ssm_norm_g"], True)], 1, width=1024, ncol=SSM_GROUPS, out_dtypes=[BF16], **kw)
    pa = matmul(o_a, p["wa"], "nn", name="branch_a", bm=lp // 2, bn=512)
    pb = matmul(o_b, p["wb"], "nn", name="branch_b", bm=lp // 2, bn=512)
    pc = matmul(o_c, p["wc"], "nn", name="branch_c", bm=lp // 2, bn=512)
    merge_rows = [(pa, 0), (pb, 0), (pc, 0)] + [(proj, OFF_GATES // 512 + 2 * i) for i in range(3)]
    (merged,) = rowmap_fwd("merge_fwd", _f_merge, merge_rows, [], 1, width=512, ncol=2, out_dtypes=[BF16], **kw)
    h_out = matmul(merged, p["wo"], "nn", name="out_proj", bm=lp, bn=512, residual=h)
    saved = dict(h=h, u=u, proj=proj, qkv=qkv, first=first, second=second, o_a_raw=o_a_raw, o_b_raw=o_b_raw,
                 gdn_states=gdn_states, gdn_t=gdn_t, xbc=xbc, y_raw=y_raw, ssd_states=ssd_states, o_a=o_a, o_b=o_b, o_c=o_c, pa=pa, pb=pb, pc=pc,
                 merged=merged, params=p)
    return h_out, saved, rode[1:]


def _layer_bwd(d_h, p, s, ride):
    lp = d_h.shape[0]
    bm = _pick(lp, 544, 16)
    kw = dict(bm=bm, lp=lp)
    proj = s["proj"]
    g = {}
    d_merged = matmul(d_h, p["wo"], "nt", name="d_merged", bm=lp, bn=512)
    g["w_out"] = matmul(s["merged"], d_h, "tn", name="g_w_out", bm=512, bn=1024, bk=lp)
    merge_rows = [(s["pa"], 0), (s["pb"], 0), (s["pc"], 0)] + [(proj, OFF_GATES // 512 + 2 * i) for i in range(3)]
    (d_pa, d_pb, d_pc, d_ga, d_gb, d_gc), _ = rowmap_bwd("merge_bwd", _f_merge, merge_rows, [], [d_merged], width=512, ncol=2,
                                                         d_row_dtypes=[BF16] * 6, **kw)
    g["w_branch_a"] = matmul(s["o_a"], d_pa, "tn", name="g_w_a", bm=512, bn=1024, bk=lp)
    g["w_branch_b"] = matmul(s["o_b"], d_pb, "tn", name="g_w_b", bm=512, bn=1024, bk=lp)
    g["w_branch_c"] = matmul(s["o_c"], d_pc, "tn", name="g_w_c", bm=512, bn=1024, bk=lp)
    d_oa = matmul(d_pa, p["wa"], "nt", name="d_o_a", bm=lp, bn=512)
    d_ob = matmul(d_pb, p["wb"], "nt", name="d_o_b", bm=lp, bn=512)
    d_oc = matmul(d_pc, p["wc"], "nt", name="d_o_c", bm=lp, bn=512)
    (d_oa_raw, d_sbz), _ = rowmap_bwd("gate_a_bwd", _f_gate_silu, [(s["o_a_raw"], 0), (proj, OFF_SB_Z // 1024)], [], [d_oa],
                                      width=1024, ncol=1, d_row_dtypes=[F32, BF16], **kw)
    (d_ob_raw, d_gdz), (g["gdn_norm_g"],) = rowmap_bwd(
        "gate_b_bwd", _f_head_norm_gate, [(s["o_b_raw"], 0), (proj, OFF_GDN_Z // 1024)], [(p["gdn_norm_g"], False)], [d_ob],
        width=1024, ncol=1, d_row_dtypes=[F32, BF16], **kw)
    (d_y, d_xh, d_ssz), (g_dskip, g["ssm_norm_g"]) = rowmap_bwd(
        "gate_c_bwd", _f_ssm_out, [(s["y_raw"], 0), (s["xbc"], 0), (proj, OFF_SSM_Z // 1024)],
        [(p["d_skip"], True), (p["ssm_norm_g"], True)], [d_oc], width=1024, ncol=SSM_GROUPS, d_row_dtypes=[F32, F32, BF16], **kw)
    g["gdn_norm_g"], g["ssm_norm_g"] = g["gdn_norm_g"][0], g["ssm_norm_g"][0]
    g["ssm_d"] = g_dskip.reshape(SSM_HEADS, SSM_P).sum(axis=1)
    d_q, d_k, d_v, *rode = sb_bwd(proj, (0, N_HEADS, 2 * N_HEADS), d_oa_raw, ride)
    d_qkv, d_first_gdn = gdn_bwd(s["qkv"], s["first"], s["gdn_states"], s["gdn_t"], d_ob_raw)
    d_xbc_out, d_first_ssd, d_second = ssd_bwd(s["xbc"], s["first"], s["second"], s["ssd_states"], d_y, d_xh)
    d_gdqkv, g["gdn_conv_w"], _ = conv_bwd("gdn_conv_bwd", proj, OFF_GDN_QKV // HEAD_DIM, p["gdn_conv_w"], p["gdn_conv_b"], 2 * N_HEADS,
                                           d_qkv)
    d_xbc, g["ssm_conv_w"], g_cb = conv_bwd("ssm_conv_bwd", proj, OFF_SSM_XBC // HEAD_DIM, p["ssm_conv_w"], p["ssm_conv_b"], 0, d_xbc_out)
    g["ssm_conv_b"] = g_cb[0]
    d_first = d_first_gdn + d_first_ssd
    (d_small,), (g_bias, g_alog) = rowmap_bwd("gates_bwd", _f_small_gates, [(proj, OFF_SMALL // HEAD_DIM)],
                                              [(p["bias_vec"], False), (p["alog_vec"], False)], [d_first, d_second],
                                              width=HEAD_DIM, ncol=1, d_row_dtypes=[BF16], **kw)
    g["gdn_dt_bias"], g["ssm_dt_bias"] = g_bias[0, 8:16], g_bias[0, 16:48]
    g["gdn_a_log"], g["ssm_a_log"] = g_alog[0, 8:16], g_alog[0, 16:48]
    d_proj = jnp.concatenate([d_q, d_k, d_v, d_sbz, d_gdqkv, d_gdz, d_ssz, d_xbc, d_ga, d_gb, d_gc, d_small], axis=1)
    g["w_in"] = matmul(s["u"], d_proj, "tn", name="g_w_in", bm=1024, bn=640, bk=lp, out_dtype=BF16)
    bufs = [_split_halves(buf) for buf in _layer_grad_buffers(g, D_IN // 4)]
    d_u, *theirs = matmul(d_proj, p["wp"], "nt", name="d_u", bm=lp, bn=1024, bk=1600, ride=split_ride(bufs))
    parts = [pair_add(f"grads_pair_add_{i}", buf, t, BF16) for i, (buf, t) in enumerate(zip(bufs, theirs))]
    (d_hn,), (g_norm,) = rowmap_bwd("rms_bwd", _f_rmsnorm, [(s["h"], 0)], [(p["norm_g"], False)], [d_u], width=D_MODEL, ncol=1, **kw)
    g["norm_g"] = g_norm[0]
    return d_h + d_hn, g, rode, parts


def kernel(x, meta_tokens, norm_g, w_in, gdn_conv_w, gdn_a_log, gdn_dt_bias, gdn_norm_g, ssm_conv_w, ssm_conv_b, ssm_a_log, ssm_dt_bias, ssm_d, ssm_norm_g, w_branch_a, w_branch_b, w_branch_c, w_out, final_norm_g, loss_target, m_meta_tokens, m_norm_g, m_w_in, m_gdn_conv_w, m_gdn_a_log, m_gdn_dt_bias, m_gdn_norm_g, m_ssm_conv_w, m_ssm_conv_b, m_ssm_a_log, m_ssm_dt_bias, m_ssm_d, m_ssm_norm_g, m_w_branch_a, m_w_branch_b, m_w_branch_c, m_w_out, m_final_norm_g, v_meta_tokens, v_norm_g, v_w_in, v_gdn_conv_w, v_gdn_a_log, v_gdn_dt_bias, v_gdn_norm_g, v_ssm_conv_w, v_ssm_conv_b, v_ssm_a_log, v_ssm_dt_bias, v_ssm_d, v_ssm_norm_g, v_w_branch_a, v_w_branch_b, v_w_branch_c, v_w_out, v_final_norm_g):
    w = dict(meta_tokens=meta_tokens, norm_g=norm_g, w_in=w_in, gdn_conv_w=gdn_conv_w, gdn_a_log=gdn_a_log, gdn_dt_bias=gdn_dt_bias,
             gdn_norm_g=gdn_norm_g, ssm_conv_w=ssm_conv_w, ssm_conv_b=ssm_conv_b, ssm_a_log=ssm_a_log, ssm_dt_bias=ssm_dt_bias,
             ssm_d=ssm_d, ssm_norm_g=ssm_norm_g, w_branch_a=w_branch_a, w_branch_b=w_branch_b, w_branch_c=w_branch_c, w_out=w_out,
             final_norm_g=final_norm_g)
    m = dict(meta_tokens=m_meta_tokens, norm_g=m_norm_g, w_in=m_w_in, gdn_conv_w=m_gdn_conv_w, gdn_a_log=m_gdn_a_log,
             gdn_dt_bias=m_gdn_dt_bias, gdn_norm_g=m_gdn_norm_g, ssm_conv_w=m_ssm_conv_w, ssm_conv_b=m_ssm_conv_b,
             ssm_a_log=m_ssm_a_log, ssm_dt_bias=m_ssm_dt_bias, ssm_d=m_ssm_d, ssm_norm_g=m_ssm_norm_g, w_branch_a=m_w_branch_a,
             w_branch_b=m_w_branch_b, w_branch_c=m_w_branch_c, w_out=m_w_out, final_norm_g=m_final_norm_g)
    v = dict(meta_tokens=v_meta_tokens, norm_g=v_norm_g, w_in=v_w_in, gdn_conv_w=v_gdn_conv_w, gdn_a_log=v_gdn_a_log,
             gdn_dt_bias=v_gdn_dt_bias, gdn_norm_g=v_gdn_norm_g, ssm_conv_w=v_ssm_conv_w, ssm_conv_b=v_ssm_conv_b,
             ssm_a_log=v_ssm_a_log, ssm_dt_bias=v_ssm_dt_bias, ssm_d=v_ssm_d, ssm_norm_g=v_ssm_norm_g, w_branch_a=v_w_branch_a,
             w_branch_b=v_w_branch_b, w_branch_c=v_w_branch_c, w_out=v_w_out, final_norm_g=v_final_norm_g)
    depth = norm_g.shape[0]
    placed, placed_small = _place_weights(w)
    got_a, got_s = gather_shards("gather_first", [placed[0][0], placed_small])
    exact = _exact_weights(w, got_s)

    h = jnp.concatenate([jnp.zeros((PAD, D_MODEL), F32), exact["meta_tokens"], x[0]], axis=0)
    params, saved = [], []
    for l in range(depth):
        ride = gather_ride([placed[l][1]] + ([placed[l + 1][0]] if l + 1 < depth else []))
        h, s, rode = _layer_fwd(h, _layer_params(w, exact, _proj_weight(got_a), l), ride)
        params.append(s.pop("params"))
        saved.append(s)
        if rode:
            (got_a,) = rode
    loss, d_h, g_final = loss_head(h, loss_target[0], final_norm_g[None])

    layer_grads, reds, waiting = [None] * depth, [None] * depth, None
    for l in reversed(range(depth)):
        d_h, layer_grads[l], rode, parts = _layer_bwd(d_h, params[l], saved[l], exchange_ride(waiting) if waiting else None)
        if waiting:
            reds[l + 1] = [chip_add(f"grads_chip_add_{i}", gt, p) for i, (gt, p) in enumerate(zip(rode, waiting))]
        waiting = parts
    grads = {n: jnp.stack([layer_grads[l][n] for l in range(depth)]) for n in WEIGHTS
             if n not in ("meta_tokens", "final_norm_g", "w_in") + BRANCH}
    grads["meta_tokens"] = d_h[PAD:PAD + N_META]
    grads["final_norm_g"] = g_final[0]
    grad_x = d_h[PAD + N_META:][None]
    buf_s = jnp.stack([_pack([_shard(grads[n], SHARD_AXIS[n], s).astype(BF16).reshape(-1) for n in EXACT], row_unit=32) for s in range(4)])
    small = _pack([grads[n].reshape(-1) for n in REPLICATED], row_unit=32)
    last = waiting + _start_reduce("small_grads", [buf_s, jnp.broadcast_to(small[None], (4,) + small.shape)], [BF16, F32])
    got = chip_exchange("grads_chip_exchange", last)
    sums = [chip_add(f"grads_chip_add_{i}", gt, p) for i, (gt, p) in enumerate(zip(got, last))]
    reds[0] = sums[:2]
    joined = pair_join("grads_pair_join", [r for layer in reds for r in layer] + sums[2:])
    red = {"w_in": jnp.stack(joined[0:2 * depth:2])}
    pos = 0
    for n, rows in zip(BRANCH, BRANCH_ROWS):
        red[n] = jnp.stack([joined[2 * l + 1][pos:pos + rows] for l in range(depth)])
        pos += rows
    red.update(zip(EXACT, _unpack(joined[-2], [w[n].shape for n in EXACT])))
    small_red = joined[-1]
    delta, new_m, new_v = {}, {}, {}
    for n in SHARDED:
        if w[n].shape[-1] % HEAD_DIM:
            to_view, from_view = (lambda a: jnp.transpose(a, (2, 0, 1))), (lambda a: jnp.transpose(a, (1, 2, 0)))
            delta[n], new_m[n], new_v[n], red[n] = [from_view(a) for a in adamw("adamw_" + n, to_view(w[n]), to_view(red[n]),
                                                                                to_view(m[n]), to_view(v[n]), echo_g=True)]
        else:
            delta[n], new_m[n], new_v[n] = adamw("adamw_" + n, w[n], red[n], m[n], v[n])
    pack_small = lambda d: _pack([d[n].reshape(-1) for n in REPLICATED], row_unit=32)
    small = adamw("adamw_small", pack_small(w), small_red, pack_small(m), pack_small(v))
    shapes = [w[n].shape for n in REPLICATED]
    red.update(zip(REPLICATED, _unpack(small_red, shapes)))
    for d, buf in zip((delta, new_m, new_v), small):
        d.update(zip(REPLICATED, _unpack(buf, shapes)))
    total_loss = lax.psum(loss[0, 0], ("x", "y", "c"))
    return (total_loss, grad_x, *[red[n] for n in WEIGHTS], *[delta[n] for n in WEIGHTS], *[new_m[n] for n in WEIGHTS],
            *[new_v[n] for n in WEIGHTS])
```

```python
import functools
import math

import jax
import jax.numpy as jnp
from jax import lax
from jax.experimental import pallas as pl
from jax.experimental.pallas import tpu as pltpu

F32 = jnp.float32
BF16 = jnp.bfloat16

N_META = 16
RMS_EPS = 1e-6
L2_EPS = 1e-6
CONV_K = 4
D_MODEL = 1024
HEAD_DIM = 128
N_HEADS = 8
CHUNK = 64
SB_BLOCK = 128
PAD = SB_BLOCK - N_META
SSM_INNER = 2048
SSM_P = 64
SSM_HEADS = 32
SSM_GROUPS = 2
SSM_HG = SSM_HEADS // SSM_GROUPS
SSM_N = 128
VMEM_LIMIT = 56 * 1024 * 1024

def _dims(mode, ndim):
    lhs, rhs = {"nn": (1, 0), "nt": (1, 1), "tn": (0, 0)}[mode]
    off = ndim - 2
    return (((lhs + off,), (rhs + off,)), (tuple(range(off)), tuple(range(off))))


def _dot(a, b, mode):
    return lax.dot_general(a, b, _dims(mode, a.ndim), preferred_element_type=F32)


def _halves(a):
    hi = a.astype(BF16)
    return hi, (a - hi.astype(F32)).astype(BF16)


def _mm_raw(a, b, mode, kind):
    if kind == "bf16":
        return _dot(a.astype(BF16), b.astype(BF16), mode)
    if kind == "lhs01":
        hi, lo = _halves(b)
        a = a.astype(BF16)
        return _dot(a, hi, mode) + _dot(a, lo, mode)
    if kind == "rhs01":
        hi, lo = _halves(a)
        b = b.astype(BF16)
        return _dot(hi, b, mode) + _dot(lo, b, mode)
    a_hi, a_lo = _halves(a)
    b_hi, b_lo = _halves(b)
    return _dot(a_hi, b_hi, mode) + (_dot(a_hi, b_lo, mode) + _dot(a_lo, b_hi, mode))


@functools.partial(jax.custom_vjp, nondiff_argnums=(2, 3))
def _mm(a, b, mode="nn", kind="bf16"):
    return _mm_raw(a, b, mode, kind)


def _mm_fwd(a, b, mode, kind):
    return _mm_raw(a, b, mode, kind), (a, b)


def _mm_bwd(mode, kind, res, g):
    a, b = res
    if kind == "lhs01":
        return jnp.zeros_like(a), _mm_raw(a, g, {"nn": "tn", "tn": "nn"}[mode], "lhs01")
    if kind == "rhs01":
        return _mm_raw(g, b, {"nn": "nt", "nt": "nn"}[mode], "rhs01"), jnp.zeros_like(b)
    if mode == "nn":
        return _mm_raw(g, b, "nt", kind), _mm_raw(a, g, "tn", kind)
    if mode == "nt":
        return _mm_raw(g, b, "nn", kind), _mm_raw(g, a, "tn", kind)
    return _mm_raw(b, g, "nt", kind), _mm_raw(a, g, "nn", kind)


_mm.defvjp(_mm_fwd, _mm_bwd)


def _iota2(shape, axis):
    return lax.broadcasted_iota(jnp.int32, shape, axis)


def _inv_unit_lower_raw(m):
    size = m.shape[-1]
    eye = (_iota2((size, size), 0) == _iota2((size, size), 1)).astype(F32)
    n = -m
    t = eye + n
    p = n
    steps = int(math.log2(size)) - 1
    for _ in range(steps):
        p = _mm_raw(p, p, "nn", "x3")
        t = t + _mm_raw(t, p, "nn", "x3")
    return t


@jax.custom_vjp
def _inv_unit_lower(m):
    return _inv_unit_lower_raw(m)


def _inv_fwd(m):
    t = _inv_unit_lower_raw(m)
    return t, t


def _inv_bwd(t, g):
    return (-_mm_raw(_mm_raw(t, g, "tn", "x3"), t, "nt", "x3"),)


_inv_unit_lower.defvjp(_inv_fwd, _inv_bwd)


@jax.custom_vjp
def _inv_known(m, t):
    return t


_inv_known.defvjp(lambda m, t: (t, t), lambda t, g: (_inv_bwd(t, g)[0], jnp.zeros_like(t)))


def _safe_decay(col, row, keep):
    return jnp.where(keep, jnp.exp(jnp.where(keep, col - row, 0.0)), 0.0)


def _col_to_row(col):
    n = col.shape[-2]
    eye = _iota2((n, n), 0) == _iota2((n, n), 1)
    return jnp.sum(jnp.where(eye, col, 0.0), axis=-2, keepdims=True)


def _cumsum_col(col):
    n = col.shape[-2]
    li, si = _iota2((n, n), 0), _iota2((n, n), 1)
    row = _col_to_row(col)
    c_col = jnp.sum(jnp.where(li >= si, row, 0.0), axis=-1, keepdims=True)
    c_row = jnp.sum(jnp.where(li <= si, col, 0.0), axis=-2, keepdims=True)
    return c_col, c_row


def _gdn_chunk(q, k, v, g, beta, state, t_known=None):
    cl = q.shape[-2]
    li, si = _iota2((cl, cl), 0), _iota2((cl, cl), 1)
    gc_col, gc_row = _cumsum_col(g)
    g_last = jnp.sum(g, axis=-2, keepdims=True)
    dec_strict = _safe_decay(gc_col, gc_row, li > si)
    dec_incl = _safe_decay(gc_col, gc_row, li >= si)
    e_gc = jnp.exp(gc_col)
    qs = q * (HEAD_DIM ** -0.5)
    kb = k * beta
    m = _mm(kb, k, "nt") * dec_strict
    t_inv = _inv_unit_lower(m) if t_known is None else _inv_known(m, t_known)
    u = _mm(t_inv, v * beta)
    w = _mm(t_inv, kb * e_gc)
    a_qk = _mm(qs, k, "nt") * dec_incl
    q_dec = qs * e_gc
    k_end = k * jnp.exp(g_last - gc_col)
    v_new = u - _mm(w, state)
    o = _mm(q_dec, state) + _mm(a_qk, v_new)
    new_state = state * jnp.exp(g_last) + _mm(k_end, v_new, "tn")
    return o, new_state, t_inv


def _gdn_operands(qkv_ref, gt):
    nh, width = N_HEADS, N_HEADS * HEAD_DIM
    heads = lambda off: jnp.stack([qkv_ref[:, off + h * HEAD_DIM:off + (h + 1) * HEAD_DIM] for h in range(nh)])
    cols = lambda off: jnp.stack([gt[:, off + h:off + h + 1] for h in range(nh)])
    return heads(0), heads(width), heads(2 * width), cols(nh), cols(0)


def gdn_fwd(qkv, gates):
    lp = qkv.shape[0]
    nh = N_HEADS
    nc = lp // CHUNK
    width = nh * HEAD_DIM

    def body(qkv_ref, gt_ref, o_ref, s_ref, t_ref, state):
        @pl.when(pl.program_id(0) == 0)
        def _():
            state[...] = jnp.zeros_like(state)

        s_in = state[...]
        s_ref[0] = s_in
        o, s_new, t_inv = _gdn_chunk(*_gdn_operands(qkv_ref, gt_ref[...]), s_in)
        t_ref[0] = t_inv
        for h in range(nh):
            o_ref[:, h * HEAD_DIM:(h + 1) * HEAD_DIM] = o[h]
        state[...] = s_new

    return pl.pallas_call(
        body, name="gdn_fwd", grid=(nc,),
        in_specs=[pl.BlockSpec((CHUNK, 3 * width), lambda c: (c, 0)), pl.BlockSpec((CHUNK, HEAD_DIM), lambda c: (c, 0))],
        out_specs=[pl.BlockSpec((CHUNK, width), lambda c: (c, 0)), pl.BlockSpec((1, nh, HEAD_DIM, HEAD_DIM), lambda c: (c, 0, 0, 0)),
                   pl.BlockSpec((1, nh, CHUNK, CHUNK), lambda c: (c, 0, 0, 0))],
        out_shape=[jax.ShapeDtypeStruct((lp, width), F32), jax.ShapeDtypeStruct((nc, nh, HEAD_DIM, HEAD_DIM), F32),
                   jax.ShapeDtypeStruct((nc, nh, CHUNK, CHUNK), F32)],
        scratch_shapes=[pltpu.VMEM((nh, HEAD_DIM, HEAD_DIM), F32)],
        compiler_params=pltpu.CompilerParams(dimension_semantics=("arbitrary",), vmem_limit_bytes=VMEM_LIMIT),
    )(qkv, gates)


def gdn_bwd(qkv, gates, states, t_invs, d_o):
    lp = qkv.shape[0]
    nh = N_HEADS
    nc = lp // CHUNK
    width = nh * HEAD_DIM

    def body(qkv_ref, gt_ref, s_ref, t_ref, do_ref, dqkv_ref, dgt_ref, d_state):
        @pl.when(pl.program_id(0) == 0)
        def _():
            d_state[...] = jnp.zeros_like(d_state)

        t_known = t_ref[0]
        _, pull = jax.vjp(lambda *xs: _gdn_chunk(*xs, t_known=t_known)[:2], *_gdn_operands(qkv_ref, gt_ref[...]), s_ref[0])
        d_o = jnp.stack([do_ref[:, h * HEAD_DIM:(h + 1) * HEAD_DIM] for h in range(nh)])
        dq, dk, dv, dg, db, ds = pull((d_o, d_state[...]))
        lane = _iota2((CHUNK, HEAD_DIM), 1)
        d_gt = jnp.zeros((CHUNK, HEAD_DIM), F32)
        for h in range(nh):
            for part, val in enumerate((dq, dk, dv)):
                dqkv_ref[:, part * width + h * HEAD_DIM:part * width + (h + 1) * HEAD_DIM] = val[h]
            d_gt = d_gt + jnp.where(lane == h, db[h], 0.0) + jnp.where(lane == nh + h, dg[h], 0.0)
        dgt_ref[...] = d_gt
        d_state[...] = ds

    rev = lambda c: (nc - 1 - c, 0)
    return pl.pallas_call(
        body, name="gdn_bwd", grid=(nc,),
        in_specs=[pl.BlockSpec((CHUNK, 3 * width), rev), pl.BlockSpec((CHUNK, HEAD_DIM), rev),
                  pl.BlockSpec((1, nh, HEAD_DIM, HEAD_DIM), lambda c: (nc - 1 - c, 0, 0, 0)),
                  pl.BlockSpec((1, nh, CHUNK, CHUNK), lambda c: (nc - 1 - c, 0, 0, 0)), pl.BlockSpec((CHUNK, width), rev)],
        out_specs=[pl.BlockSpec((CHUNK, 3 * width), rev), pl.BlockSpec((CHUNK, HEAD_DIM), rev)],
        out_shape=[jax.ShapeDtypeStruct((lp, 3 * width), F32), jax.ShapeDtypeStruct((lp, HEAD_DIM), F32)],
        scratch_shapes=[pltpu.VMEM((nh, HEAD_DIM, HEAD_DIM), F32)],
        compiler_params=pltpu.CompilerParams(dimension_semantics=("arbitrary",), vmem_limit_bytes=VMEM_LIMIT),
    )(qkv, gates, states, t_invs, d_o)


def _head_expand():
    width = SSM_HG * SSM_P
    return (_iota2((SSM_HG, width), 1) // SSM_P == _iota2((SSM_HG, width), 0)).astype(F32)


def _ssd_chunk(x, b, c, dt, la, state):
    cl = x.shape[0]
    li, si = _iota2((cl, cl), 0), _iota2((cl, cl), 1)
    causal = li >= si
    expand = _head_expand()
    tri = causal.astype(F32)
    xs = x * _mm(dt, expand, "nn", "rhs01")
    la_x = _mm(la, expand, "nn", "rhs01")
    cs_x = _mm(tri, la_x, "nn", "lhs01")
    last_x = jnp.sum(la_x, axis=0, keepdims=True)
    cs = _mm(tri, la, "nn", "lhs01")
    scores = _mm(c, b, "nt")
    head_id = _iota2((1, SSM_HG), 1)
    per_tile = HEAD_DIM // SSM_P
    tile_head = _iota2((1, HEAD_DIM), 1) // SSM_P
    within = []
    for t in range(SSM_HG // per_tile):
        xs_t = xs[:, t * HEAD_DIM:(t + 1) * HEAD_DIM]
        acc = jnp.zeros((cl, HEAD_DIM), F32)
        for hh in range(per_tile):
            cs_col = jnp.sum(jnp.where(head_id == t * per_tile + hh, cs, 0.0), axis=1, keepdims=True)
            decay = _safe_decay(cs_col, _col_to_row(cs_col), causal)
            acc = acc + _mm(scores * decay, jnp.where(tile_head == hh, xs_t, 0.0))
        within.append(acc)
    y = _mm(c, state) * jnp.exp(cs_x) + jnp.concatenate(within, axis=1)
    new_state = state * jnp.exp(last_x) + _mm(b, xs * jnp.exp(last_x - cs_x), "tn")
    return y, new_state


GATE_DT = 16


def _place_lanes(v, lo):
    n = v.shape[1]
    sel = (_iota2((n, HEAD_DIM), 1) == _iota2((n, HEAD_DIM), 0) + lo).astype(F32)
    return _mm_raw(v, sel, "nn", "rhs01")


def ssd_fwd(xbc, first, second):
    lp = xbc.shape[0]
    nc = lp // CHUNK
    width = SSM_HG * SSM_P
    b_off, c_off = SSM_INNER, SSM_INNER + SSM_GROUPS * SSM_N

    def body(x_ref, f_ref, s2_ref, y_ref, s_ref, state):
        @pl.when(pl.program_id(0) == 0)
        def _():
            state[...] = jnp.zeros_like(state)

        f, s2 = f_ref[...], s2_ref[...]
        for g in range(SSM_GROUPS):
            lo = GATE_DT + g * SSM_HG
            s_in = state[g]
            s_ref[0, g] = s_in
            y, s_new = _ssd_chunk(x_ref[:, g * width:(g + 1) * width], x_ref[:, b_off + g * SSM_N:b_off + (g + 1) * SSM_N],
                                  x_ref[:, c_off + g * SSM_N:c_off + (g + 1) * SSM_N], f[:, lo:lo + SSM_HG], s2[:, lo:lo + SSM_HG], s_in)
            y_ref[:, g * width:(g + 1) * width] = y
            state[g] = s_new

    row = lambda cols: pl.BlockSpec((CHUNK, cols), lambda k: (k, 0))
    return pl.pallas_call(
        body, name="ssd_fwd", grid=(nc,),
        in_specs=[row(xbc.shape[1]), row(HEAD_DIM), row(HEAD_DIM)],
        out_specs=[row(SSM_INNER), pl.BlockSpec((1, SSM_GROUPS, SSM_N, width), lambda k: (k, 0, 0, 0))],
        out_shape=[jax.ShapeDtypeStruct((lp, SSM_INNER), F32), jax.ShapeDtypeStruct((nc, SSM_GROUPS, SSM_N, width), F32)],
        scratch_shapes=[pltpu.VMEM((SSM_GROUPS, SSM_N, width), F32)],
        compiler_params=pltpu.CompilerParams(dimension_semantics=("arbitrary",), vmem_limit_bytes=VMEM_LIMIT),
    )(xbc, first, second)


def ssd_bwd(xbc, first, second, states, d_y, d_xh):
    lp = xbc.shape[0]
    nc = lp // CHUNK
    width = SSM_HG * SSM_P
    b_off, c_off = SSM_INNER, SSM_INNER + SSM_GROUPS * SSM_N

    def body(x_ref, f_ref, s2_ref, s_ref, dy_ref, dxh_ref, dx_ref, df_ref, ds2_ref, d_state):
        @pl.when(pl.program_id(0) == 0)
        def _():
            d_state[...] = jnp.zeros_like(d_state)

        f, s2 = f_ref[...], s2_ref[...]
        d_f = jnp.zeros((CHUNK, HEAD_DIM), F32)
        d_s2 = jnp.zeros((CHUNK, HEAD_DIM), F32)
        for g in range(SSM_GROUPS):
            lo = GATE_DT + g * SSM_HG
            x_l = slice(g * width, (g + 1) * width)
            b_l = slice(b_off + g * SSM_N, b_off + (g + 1) * SSM_N)
            c_l = slice(c_off + g * SSM_N, c_off + (g + 1) * SSM_N)
            _, pull = jax.vjp(_ssd_chunk, x_ref[:, x_l], x_ref[:, b_l], x_ref[:, c_l], f[:, lo:lo + SSM_HG], s2[:, lo:lo + SSM_HG],
                              s_ref[0, g])
            dx, db, dc, ddt, dla, ds = pull((dy_ref[:, x_l], d_state[g]))
            dx_ref[:, x_l] = dx + dxh_ref[:, x_l]
            dx_ref[:, b_l] = db
            dx_ref[:, c_l] = dc
            d_f = d_f + _place_lanes(ddt, lo)
            d_s2 = d_s2 + _place_lanes(dla, lo)
            d_state[g] = ds
        df_ref[...] = d_f
        ds2_ref[...] = d_s2

    row = lambda cols: pl.BlockSpec((CHUNK, cols), lambda k: (nc - 1 - k, 0))
    gate_shape = jax.ShapeDtypeStruct((lp, HEAD_DIM), F32)
    return pl.pallas_call(
        body, name="ssd_bwd", grid=(nc,),
        in_specs=[row(xbc.shape[1]), row(HEAD_DIM), row(HEAD_DIM),
                  pl.BlockSpec((1, SSM_GROUPS, SSM_N, width), lambda k: (nc - 1 - k, 0, 0, 0)), row(SSM_INNER), row(SSM_INNER)],
        out_specs=[row(xbc.shape[1]), row(HEAD_DIM), row(HEAD_DIM)],
        out_shape=[jax.ShapeDtypeStruct(xbc.shape, F32), gate_shape, gate_shape],
        scratch_shapes=[pltpu.VMEM((SSM_GROUPS, SSM_N, width), F32)],
        compiler_params=pltpu.CompilerParams(dimension_semantics=("arbitrary",), vmem_limit_bytes=VMEM_LIMIT),
    )(xbc, first, second, states, d_y, d_xh)


SB_QROWS = 272


def _mm_tri(a, tri):
    return _mm_raw(a, tri.astype(BF16), "nn", "rhs01")


def _sb_scores(q_scaled, kb, row0, j):
    shape = (q_scaled.shape[0], SB_BLOCK)
    z = _mm_raw(q_scaled, kb, "nt", "bf16")
    q_pos = row0 + _iota2(shape, 0)
    k_pos = j * SB_BLOCK + _iota2(shape, 1)
    valid = (k_pos < q_pos) & (k_pos >= PAD)
    sp = jnp.maximum(z, 0.0) + jnp.log(1.0 + jnp.exp(-jnp.abs(z)))
    lk = jnp.where(valid, -sp, 0.0)
    return z, sp, valid, lk


SB_DEAD = -110.0


def sb_fwd(src, offs, ride=None):
    lp = src.shape[0]
    nh = N_HEADS
    qb = _pick(lp, SB_QROWS, 8)
    scale = HEAD_DIM ** -0.5
    blk = SB_BLOCK

    n_in = len(ride.ins) if ride else 0
    n_out = len(ride.out_shapes) if ride else 0

    def body(*refs):
        q_ref, k_ref, v_ref = refs[:3]
        o_ref = refs[3 + n_in]
        ride_refs = (refs[3:3 + n_in], refs[4 + n_in:4 + n_in + n_out], refs[4 + n_in + n_out:])
        i = pl.program_id(1)
        if ride:
            @pl.when((pl.program_id(0) == 0) & (i == 0))
            def _():
                ride.start(*ride_refs)
        q_scaled = q_ref[...] * scale
        upper = _iota2((blk, blk), 0) > _iota2((blk, blk), 1)
        n_blocks = ((i + 1) * qb + blk - 1) // blk

        def live(state):
            it, _, c = state
            return (it < n_blocks) & (jnp.max(c) > SB_DEAD)

        def step(state):
            it, acc, c = state
            j = n_blocks - 1 - it
            rows = pl.ds(pl.multiple_of(j * blk, blk), blk)
            z, sp, valid, lk = _sb_scores(q_scaled, k_ref[rows, :], i * qb, j)
            later = _mm_tri(lk, upper) + c
            w = jnp.where(valid, jnp.exp(z - sp + later), 0.0)
            acc = acc + _mm_raw(w, v_ref[rows, :], "nn", "bf16")
            return it + 1, acc, c + jnp.sum(lk, axis=1, keepdims=True)

        _, acc, _ = lax.while_loop(live, step, (jnp.int32(0), jnp.zeros((qb, HEAD_DIM), F32), jnp.zeros((qb, 1), F32)))
        o_ref[...] = acc
        if ride:
            @pl.when((pl.program_id(0) == nh - 1) & (i == lp // qb - 1))
            def _():
                ride.finish(*ride_refs)

    qspec = pl.BlockSpec((qb, HEAD_DIM), lambda h, i: (i, offs[0] + h))
    kspec = pl.BlockSpec((lp, HEAD_DIM), lambda h, i: (0, offs[1] + h))
    vspec = pl.BlockSpec((lp, HEAD_DIM), lambda h, i: (0, offs[2] + h))
    ospec = pl.BlockSpec((qb, HEAD_DIM), lambda h, i: (i, h))
    return pl.pallas_call(
        body, name="sb_fwd", grid=(nh, lp // qb), in_specs=[qspec, kspec, vspec] + [ANY] * n_in, out_specs=[ospec] + [ANY] * n_out,
        out_shape=[jax.ShapeDtypeStruct((lp, nh * HEAD_DIM), F32)] + (ride.out_shapes if ride else []),
        scratch_shapes=ride.sems if ride else [],
        input_output_aliases={3 + k: 1 + k for k in range(n_in)} if ride and ride.alias else {},
        compiler_params=pltpu.CompilerParams(dimension_semantics=("arbitrary", "arbitrary"), vmem_limit_bytes=VMEM_LIMIT),
    )(src, src, src, *(ride.ins if ride else []))


def sb_bwd(src, offs, d_o, ride=None):
    lp = src.shape[0]
    nh = N_HEADS
    qb = _pick(lp, SB_QROWS, 8)
    scale = HEAD_DIM ** -0.5
    blk = SB_BLOCK

    n_in = len(ride.ins) if ride else 0
    n_out = len(ride.out_shapes) if ride else 0

    def body(*refs):
        q_ref, k_ref, v_ref, do_ref = refs[:4]
        dq_ref, dk_out, dv_out = refs[4 + n_in:7 + n_in]
        ride_refs = (refs[4:4 + n_in], refs[7 + n_in:7 + n_in + n_out], refs[9 + n_in + n_out:])
        dk_ref, dv_ref = refs[7 + n_in + n_out:9 + n_in + n_out]
        i = pl.program_id(1)
        if ride:
            @pl.when((pl.program_id(0) == 0) & (i == 0))
            def _():
                ride.start(*ride_refs)

        @pl.when(i == 0)
        def _():
            dk_ref[...] = jnp.zeros_like(dk_ref)
            dv_ref[...] = jnp.zeros_like(dv_ref)

        q_scaled = q_ref[...] * scale
        d_out = do_ref[...]
        lower_incl = _iota2((blk, blk), 0) <= _iota2((blk, blk), 1)
        lower = _iota2((blk, blk), 0) < _iota2((blk, blk), 1)
        n_blocks = ((i + 1) * qb + blk - 1) // blk

        def live(state):
            it, c = state
            return (it < n_blocks) & (jnp.max(c) > SB_DEAD)

        def count(state):
            it, c = state
            rows = pl.ds(pl.multiple_of((n_blocks - 1 - it) * blk, blk), blk)
            _, _, _, lk = _sb_scores(q_scaled, k_ref[rows, :], i * qb, n_blocks - 1 - it)
            return it + 1, c + jnp.sum(lk, axis=1, keepdims=True)

        n_live, total = lax.while_loop(live, count, (jnp.int32(0), jnp.zeros((qb, 1), F32)))

        def step(j, carry):
            acc, cp, ep = carry
            rows = pl.ds(pl.multiple_of(j * blk, blk), blk)
            kb = k_ref[rows, :]
            vb = v_ref[rows, :]
            z, sp, valid, lk = _sb_scores(q_scaled, kb, i * qb, j)
            later = total - cp - _mm_tri(lk, lower_incl)
            w = jnp.where(valid, jnp.exp(z - sp + later), 0.0)
            e = w * _mm_raw(d_out, vb, "nt", "bf16")
            before = ep + _mm_tri(e, lower)
            dz = jnp.where(valid, e * jnp.exp(-sp) - before * jnp.exp(z - sp), 0.0)
            dk_ref[rows, :] += _mm_raw(dz, q_scaled, "tn", "bf16")
            dv_ref[rows, :] += _mm_raw(w, d_out, "tn", "bf16")
            acc = acc + _mm_raw(dz, kb, "nn", "bf16")
            return acc, cp + jnp.sum(lk, axis=1, keepdims=True), ep + jnp.sum(e, axis=1, keepdims=True)

        zero_col = jnp.zeros((qb, 1), F32)
        acc, _, _ = lax.fori_loop(n_blocks - n_live, n_blocks, step, (jnp.zeros((qb, HEAD_DIM), F32), zero_col, zero_col))
        dq_ref[...] = (acc * scale).astype(dq_ref.dtype)

        @pl.when(i == lp // qb - 1)
        def _():
            dk_out[...] = dk_ref[...].astype(dk_out.dtype)
            dv_out[...] = dv_ref[...].astype(dv_out.dtype)

        if ride:
            @pl.when((pl.program_id(0) == nh - 1) & (i == lp // qb - 1))
            def _():
                ride.finish(*ride_refs)

    qspec = pl.BlockSpec((qb, HEAD_DIM), lambda h, i: (i, offs[0] + h))
    kspec = pl.BlockSpec((lp, HEAD_DIM), lambda h, i: (0, offs[1] + h))
    vspec = pl.BlockSpec((lp, HEAD_DIM), lambda h, i: (0, offs[2] + h))
    ospec = pl.BlockSpec((qb, HEAD_DIM), lambda h, i: (i, h))
    fullspec = pl.BlockSpec((lp, HEAD_DIM), lambda h, i: (0, h))
    return pl.pallas_call(
        body, name="sb_bwd", grid=(nh, lp // qb), in_specs=[qspec, kspec, vspec, ospec] + [ANY] * n_in,
        out_specs=[ospec, fullspec, fullspec] + [ANY] * n_out,
        out_shape=[jax.ShapeDtypeStruct((lp, nh * HEAD_DIM), BF16)] * 3 + (ride.out_shapes if ride else []),
        scratch_shapes=[pltpu.VMEM((lp, HEAD_DIM), F32)] * 2 + (ride.sems if ride else []),
        input_output_aliases={4 + k: 3 + k for k in range(n_in)} if ride and ride.alias else {},
        compiler_params=pltpu.CompilerParams(dimension_semantics=("arbitrary", "arbitrary"), vmem_limit_bytes=VMEM_LIMIT),
    )(src, src, src, d_o, *(ride.ins if ride else []))


def _pick(n, target, unit):
    if n <= target:
        return n
    best = None
    for d in range(unit, target + 1, unit):
        if n % d == 0:
            best = d
    assert best is not None, (n, target, unit)
    return best


def matmul(a, b, mode="nn", *, name, bm=1088, bn=640, bk=2176, residual=None, out_dtype=F32, b_koff=0, ride=None):
    if mode == "nn":
        (m, k), n = a.shape, b.shape[1]
    elif mode == "nt":
        (m, k), n = a.shape, b.shape[0]
    else:
        (k, m), n = a.shape, b.shape[1]
    assert b_koff == 0 or mode == "nt"
    bm = _pick(m, bm, 128 if mode == "tn" else 8)
    bn = _pick(n, bn, 128 if mode != "nt" else 8)
    bk = _pick(k, bk, 128 if mode != "tn" else 8)
    nk = k // bk
    n_plain = 2 if residual is None else 3
    n_in = len(ride.ins) if ride else 0
    n_out = len(ride.out_shapes) if ride else 0
    steps = (m // bm, n // bn, nk)

    def body(*refs):
        a_ref, b_ref = refs[:2]
        r_ref = None if residual is None else refs[2]
        o_ref = refs[n_plain + n_in]
        acc = refs[n_plain + n_in + 1 + n_out]
        ride_refs = (refs[n_plain:n_plain + n_in], refs[n_plain + n_in + 1:n_plain + n_in + 1 + n_out], refs[n_plain + n_in + 2 + n_out:])
        kk = pl.program_id(2)
        here = [pl.program_id(d) for d in range(3)]
        if ride:
            @pl.when((here[0] == 0) & (here[1] == 0) & (here[2] == 0))
            def _():
                ride.start(*ride_refs)
        part = _mm_raw(a_ref[...], b_ref[...], mode, "bf16")

        @pl.when(kk == 0)
        def _():
            acc[...] = part

        @pl.when(kk > 0)
        def _():
            acc[...] += part

        @pl.when(kk == nk - 1)
        def _():
            res = acc[...]
            if r_ref is not None:
                res = res + r_ref[...]
            o_ref[...] = res.astype(out_dtype)

        if ride:
            @pl.when((here[0] == steps[0] - 1) & (here[1] == steps[1] - 1) & (here[2] == steps[2] - 1))
            def _():
                ride.finish(*ride_refs)

    a_spec = pl.BlockSpec((bk, bm), lambda i, j, kk: (kk, i)) if mode == "tn" else pl.BlockSpec((bm, bk), lambda i, j, kk: (i, kk))
    b_spec = pl.BlockSpec((bn, bk), lambda i, j, kk: (j, b_koff + kk)) if mode == "nt" else pl.BlockSpec((bk, bn), lambda i, j, kk: (kk, j))
    o_spec = pl.BlockSpec((bm, bn), lambda i, j, kk: (i, j))
    ins, specs = [a, b], [a_spec, b_spec]
    if residual is not None:
        ins.append(residual)
        specs.append(o_spec)
    res = pl.pallas_call(
        body, name=name, grid=steps, in_specs=specs + [ANY] * n_in, out_specs=[o_spec] + [ANY] * n_out,
        out_shape=[jax.ShapeDtypeStruct((m, n), out_dtype)] + (ride.out_shapes if ride else []),
        scratch_shapes=[pltpu.VMEM((bm, bn), F32)] + (ride.sems if ride else []),
        compiler_params=pltpu.CompilerParams(dimension_semantics=("arbitrary", "arbitrary", "arbitrary"), vmem_limit_bytes=VMEM_LIMIT),
    )(*ins, *(ride.ins if ride else []))
    return res if ride else res[0]


def _row_specs(rows, params, width, bm):
    row_specs = [pl.BlockSpec((bm, width), (lambda j, i, off=off: (i, off + j))) for _, off in rows]
    par_specs = [pl.BlockSpec((p.shape[0], width) if per_col else p.shape, ((lambda j, i: (0, j)) if per_col else (lambda j, i: (0, 0))))
                 for p, per_col in params]
    return row_specs, par_specs


def rowmap_fwd(name, fn, rows, params, n_out, *, width, ncol, bm, lp, out_dtypes=None):
    out_dtypes = out_dtypes or [F32] * n_out
    row_specs, par_specs = _row_specs(rows, params, width, bm)
    nr = len(rows)

    def body(*refs):
        ins, outs = refs[:nr + len(params)], refs[nr + len(params):]
        row_ids = pl.program_id(1) * bm + _iota2((bm, 1), 0)
        res = fn(row_ids, *[r[...].astype(F32) for r in ins])
        for o_ref, val in zip(outs, res):
            o_ref[...] = val.astype(o_ref.dtype)

    o_spec = pl.BlockSpec((bm, width), lambda j, i: (i, j))
    return pl.pallas_call(
        body, name=name, grid=(ncol, lp // bm), in_specs=row_specs + par_specs, out_specs=[o_spec] * n_out,
        out_shape=[jax.ShapeDtypeStruct((lp, ncol * width), dt) for dt in out_dtypes],
        compiler_params=pltpu.CompilerParams(dimension_semantics=("arbitrary", "arbitrary"), vmem_limit_bytes=VMEM_LIMIT),
    )(*[a for a, _ in rows], *[p for p, _ in params])


def rowmap_bwd(name, fn, rows, params, d_outs, *, width, ncol, bm, lp, d_row_dtypes=None):
    d_row_dtypes = d_row_dtypes or [F32] * len(rows)
    row_specs, par_specs = _row_specs(rows, params, width, bm)
    nr, npar, nout = len(rows), len(params), len(d_outs)

    def body(*refs):
        ins = refs[:nr + npar]
        dos = refs[nr + npar:nr + npar + nout]
        d_rows = refs[nr + npar + nout:nr + npar + nout + nr]
        d_pars = refs[nr + npar + nout + nr:]
        j, i = pl.program_id(0), pl.program_id(1)
        row_ids = i * bm + _iota2((bm, 1), 0)
        _, pull = jax.vjp(lambda *xs: tuple(fn(row_ids, *xs)), *[r[...].astype(F32) for r in ins])
        grads = pull(tuple(d[...].astype(F32) for d in dos))
        for ref, val in zip(d_rows, grads[:nr]):
            ref[...] = val.astype(ref.dtype)
        for ref, val, (_, per_col) in zip(d_pars, grads[nr:], params):
            first = (i == 0) if per_col else ((i == 0) & (j == 0))

            @pl.when(first)
            def _(ref=ref, val=val):
                ref[...] = val

            @pl.when(jnp.logical_not(first))
            def _(ref=ref, val=val):
                ref[...] += val

    o_spec = pl.BlockSpec((bm, width), lambda j, i: (i, j))
    res = pl.pallas_call(
        body, name=name, grid=(ncol, lp // bm), in_specs=row_specs + par_specs + [o_spec] * nout,
        out_specs=[o_spec] * nr + par_specs,
        out_shape=[jax.ShapeDtypeStruct((lp, ncol * width), dt) for dt in d_row_dtypes]
        + [jax.ShapeDtypeStruct(p.shape, F32) for p, _ in params],
        compiler_params=pltpu.CompilerParams(dimension_semantics=("arbitrary", "arbitrary"), vmem_limit_bytes=VMEM_LIMIT),
    )(*[a for a, _ in rows], *[p for p, _ in params], *d_outs)
    return res[:nr], res[nr:]


def _silu(x):
    return x * jax.nn.sigmoid(x)


def _softplus(x):
    return jnp.maximum(x, 0.0) + jnp.log(1.0 + jnp.exp(-jnp.abs(x)))


def _real_rows(row_ids):
    return (row_ids >= PAD).astype(F32)


def _f_rmsnorm(row_ids, h, g):
    return (h * lax.rsqrt(jnp.mean(h * h, axis=-1, keepdims=True) + RMS_EPS) * g,)


def _f_small_gates(row_ids, small, bias, a_log):
    lane = _iota2(small.shape, 1)
    t = small + bias
    sp = _softplus(t)
    coef = -jnp.exp(a_log)
    keep = _real_rows(row_ids)
    first = jnp.where(lane < 8, jax.nn.sigmoid(t), jnp.where(lane < 16, coef * sp, jnp.where(lane < 48, sp, 0.0)))
    second = jnp.where((lane >= 8) & (lane < 48), coef * sp, 0.0)
    return first * keep, second * keep


def _f_gate_silu(row_ids, o, z):
    return (o * _silu(z),)


def _f_head_norm_gate(row_ids, o, z, g):
    out = []
    for h in range(o.shape[1] // HEAD_DIM):
        oh = o[:, h * HEAD_DIM:(h + 1) * HEAD_DIM]
        out.append(oh * lax.rsqrt(jnp.mean(oh * oh, axis=-1, keepdims=True) + RMS_EPS) * g)
    return (jnp.concatenate(out, axis=1) * _silu(z),)


def _f_ssm_out(row_ids, y, xh, z, d_skip, g):
    t = (y + d_skip * xh) * _silu(z)
    return (t * lax.rsqrt(jnp.mean(t * t, axis=-1, keepdims=True) + RMS_EPS) * g,)


def _f_merge(row_ids, pa, pb, pc, ga, gb, gc):
    return (jax.nn.sigmoid(ga) * pa + jax.nn.sigmoid(gb) * pb + jax.nn.sigmoid(gc) * pc,)


def _shift_rows(x, s):
    s = s % x.shape[0]
    return x if s == 0 else pltpu.roll(x, s, 0)


def _conv_pre(x, w, b):
    pre = b
    for kk in range(CONV_K):
        pre = pre + w[kk:kk + 1, :] * _shift_rows(x, CONV_K - 1 - kk)
    return pre


def _conv_post(pre, l2_flag, keep):
    act = _silu(pre)
    nrm = act * lax.rsqrt(jnp.sum(act * act, axis=-1, keepdims=True) + L2_EPS)
    return (l2_flag * nrm + (1.0 - l2_flag) * act) * keep


def conv_fwd(name, src, col_off, w, b, n_l2):
    lp, ch = src.shape[0], w.shape[1]

    def body(x_ref, w_ref, b_ref, o_ref):
        l2_flag = (pl.program_id(0) < n_l2).astype(F32)
        keep = _real_rows(_iota2((lp, 1), 0))
        o_ref[...] = _conv_post(_conv_pre(x_ref[...], w_ref[...], b_ref[...]), l2_flag, keep)

    return pl.pallas_call(
        body, name=name, grid=(ch // HEAD_DIM,),
        in_specs=[pl.BlockSpec((lp, HEAD_DIM), lambda j: (0, col_off + j)), pl.BlockSpec((CONV_K, HEAD_DIM), lambda j: (0, j)),
                  pl.BlockSpec((1, HEAD_DIM), lambda j: (0, j))],
        out_specs=pl.BlockSpec((lp, HEAD_DIM), lambda j: (0, j)),
        out_shape=jax.ShapeDtypeStruct((lp, ch), F32),
        compiler_params=pltpu.CompilerParams(dimension_semantics=("arbitrary",), vmem_limit_bytes=VMEM_LIMIT),
    )(src, w, b)


def conv_bwd(name, src, col_off, w, b, n_l2, d_out):
    lp, ch = src.shape[0], w.shape[1]

    def body(x_ref, w_ref, b_ref, do_ref, dx_ref, dw_ref, db_ref):
        l2_flag = (pl.program_id(0) < n_l2).astype(F32)
        keep = _real_rows(_iota2((lp, 1), 0))
        x, wv = x_ref[...], w_ref[...]
        pre = _conv_pre(x, wv, b_ref[...])
        _, pull = jax.vjp(lambda p: _conv_post(p, l2_flag, keep), pre)
        (d_pre,) = pull(do_ref[...])
        dx = jnp.zeros_like(x)
        for kk in range(CONV_K):
            s = CONV_K - 1 - kk
            dx = dx + wv[kk:kk + 1, :] * _shift_rows(d_pre, -s)
            dw_ref[kk:kk + 1, :] = jnp.sum(d_pre * _shift_rows(x, s), axis=0, keepdims=True)
        dx_ref[...] = (dx * keep).astype(dx_ref.dtype)
        db_ref[...] = jnp.sum(d_pre, axis=0, keepdims=True)

    seq = pl.BlockSpec((lp, HEAD_DIM), lambda j: (0, j))
    wspec = pl.BlockSpec((CONV_K, HEAD_DIM), lambda j: (0, j))
    bspec = pl.BlockSpec((1, HEAD_DIM), lambda j: (0, j))
    return pl.pallas_call(
        body, name=name, grid=(ch // HEAD_DIM,),
        in_specs=[pl.BlockSpec((lp, HEAD_DIM), lambda j: (0, col_off + j)), wspec, bspec, seq],
        out_specs=[seq, wspec, bspec],
        out_shape=[jax.ShapeDtypeStruct((lp, ch), BF16), jax.ShapeDtypeStruct(w.shape, F32), jax.ShapeDtypeStruct(b.shape, F32)],
        compiler_params=pltpu.CompilerParams(dimension_semantics=("arbitrary",), vmem_limit_bytes=VMEM_LIMIT),
    )(src, w, b, d_out)


def loss_head(h, target, g):
    lp, d = h.shape
    bm = SB_BLOCK
    first = (PAD + N_META) // bm

    def body(h_ref, t_ref, g_ref, loss_ref, dh_ref, dg_ref):
        i = pl.program_id(0)
        keep = (i >= first).astype(F32)

        def f(hv, gv):
            y = hv * lax.rsqrt(jnp.mean(hv * hv, axis=-1, keepdims=True) + RMS_EPS) * gv
            err = y - t_ref[...]
            return 0.5 * jnp.sum(jnp.mean(err * err, axis=-1, keepdims=True), axis=0, keepdims=True) * keep

        val, pull = jax.vjp(f, h_ref[...], g_ref[...])
        dh, dg = pull(jnp.ones((1, 1), F32))
        dh_ref[...] = dh

        @pl.when(i == 0)
        def _():
            loss_ref[...] = val
            dg_ref[...] = dg

        @pl.when(i > 0)
        def _():
            loss_ref[...] += val
            dg_ref[...] += dg

    row = pl.BlockSpec((bm, d), lambda i: (i, 0))
    return pl.pallas_call(
        body, name="loss_head", grid=(lp // bm,),
        in_specs=[row, pl.BlockSpec((bm, d), lambda i: (jnp.maximum(i - first, 0), 0)), pl.BlockSpec((1, d), lambda i: (0, 0))],
        out_specs=[pl.BlockSpec((1, 1), lambda i: (0, 0)), row, pl.BlockSpec((1, d), lambda i: (0, 0))],
        out_shape=[jax.ShapeDtypeStruct((1, 1), F32), jax.ShapeDtypeStruct((lp, d), F32), jax.ShapeDtypeStruct((1, d), F32)],
        compiler_params=pltpu.CompilerParams(dimension_semantics=("arbitrary",)),
    )(h, target, g)


ADAM_LR, ADAM_B1, ADAM_B2, ADAM_EPS, ADAM_WD, ADAM_STEP = 0.001, 0.9, 0.999, 1e-08, 0.01, 10


def adamw(name, w, g, m, v, echo_g=False):
    lead = w.shape[:-2]
    rows, cols = w.shape[-2:]
    br = _pick(rows, 128, 8)
    n_out = 4 if echo_g else 3

    def body(w_ref, g_ref, m_ref, v_ref, d_ref, nm_ref, nv_ref, *echo):
        gv = g_ref[...]
        nm = ADAM_B1 * m_ref[...] + (1.0 - ADAM_B1) * gv
        nv = ADAM_B2 * v_ref[...] + (1.0 - ADAM_B2) * (gv * gv)
        m_hat = nm / (1.0 - ADAM_B1 ** ADAM_STEP)
        v_hat = nv / (1.0 - ADAM_B2 ** ADAM_STEP)
        d_ref[...] = -ADAM_LR * (m_hat / (jnp.sqrt(v_hat) + ADAM_EPS) + ADAM_WD * w_ref[...])
        nm_ref[...] = nm
        nv_ref[...] = nv
        for e_ref in echo:
            e_ref[...] = gv

    if lead and rows <= 8:
        bl = _pick(lead[0], 32, 1)
        spec = pl.BlockSpec((bl, rows, cols), lambda s: (s, 0, 0))
        grid = (lead[0] // bl,)
    elif lead:
        spec = pl.BlockSpec((None, br, cols), lambda s, i: (s, i, 0))
        grid = (lead[0], rows // br)
    else:
        spec = pl.BlockSpec((br, cols), lambda i: (i, 0))
        grid = (rows // br,)
    return pl.pallas_call(
        body, name=name, grid=grid, in_specs=[spec] * 4, out_specs=[spec] * n_out,
        out_shape=[jax.ShapeDtypeStruct(w.shape, F32)] * n_out,
        compiler_params=pltpu.CompilerParams(dimension_semantics=("arbitrary",) * len(grid), vmem_limit_bytes=VMEM_LIMIT),
    )(w, g, m, v)


MESH = pl.DeviceIdType.MESH
ANY = pl.BlockSpec(memory_space=pl.ANY)
D2D_PIECES = 16
ICI_PIECES = 4


def _place():
    x, y, c = lax.axis_index("x"), lax.axis_index("y"), lax.axis_index("c")
    return x, y, c, [(1 - x, y), (x, 1 - y), (1 - x, 1 - y)]


def _pieces(rows, n, unit):
    per = -(-rows // (n * unit)) * unit
    return [(s, min(per, rows - s)) for s in range(0, rows, per)]


def _row_unit(dtype):
    return 16 if dtype == BF16 else 8


def _scalar(v):
    return jnp.reshape(v, (1,)).astype(jnp.int32)


def place_shard(name, pack):
    rows, cols = pack.shape
    br = _pick(rows, 256, 16)

    def body(m_ref, p_ref, o_ref):
        o_ref[...] = p_ref[...]

    return pl.pallas_call(
        body, name=name,
        grid_spec=pltpu.PrefetchScalarGridSpec(
            num_scalar_prefetch=1, grid=(rows // br,),
            in_specs=[pl.BlockSpec((br, cols), lambda i, m: (i, 0))],
            out_specs=pl.BlockSpec((None, br, cols), lambda i, m: (m[0], i, 0))),
        out_shape=jax.ShapeDtypeStruct((4, rows, cols), pack.dtype),
        compiler_params=pltpu.CompilerParams(dimension_semantics=("arbitrary",), vmem_limit_bytes=VMEM_LIMIT),
    )(_scalar(2 * lax.axis_index("x") + lax.axis_index("y")), pack)


class Ride:
    def __init__(self, ins, out_shapes, alias, sems, start, finish):
        self.ins, self.out_shapes, self.alias, self.sems, self.start, self.finish = list(ins), out_shapes, alias, sems, start, finish


def _gather_parts(o_refs, send_sems, recv_sems):
    x, y, c, chips = _place()
    mine = 2 * x + y

    def half_rows(b, which, start=0, size=None):
        half = o_refs[b].shape[1] // 2
        return pl.ds(pl.multiple_of(which * half + start, _row_unit(o_refs[b].dtype)), half if size is None else size)

    def remote(b, k, slot, rws, to):
        piece = o_refs[b].at[slot, rws, :]
        return pltpu.make_async_remote_copy(src_ref=piece, dst_ref=piece, send_sem=send_sems.at[b, k], recv_sem=recv_sems.at[b, k],
                                            device_id=to, device_id_type=MESH)

    return x, y, c, chips, mine, half_rows, remote


def _gather_start(o_refs, send_sems, recv_sems):
    x, y, c, chips, mine, half_rows, remote = _gather_parts(o_refs, send_sems, recv_sems)
    for b, o_ref in enumerate(o_refs):
        for j, (cx, cy) in enumerate(chips):
            for start, size in _pieces(o_ref.shape[1] // 2, ICI_PIECES, _row_unit(o_ref.dtype)):
                remote(b, j, mine, half_rows(b, c, start, size), (cx, cy, c)).start()


def _gather_finish(o_refs, send_sems, recv_sems):
    x, y, c, chips, mine, half_rows, remote = _gather_parts(o_refs, send_sems, recv_sems)
    sends = []
    for b, o_ref in enumerate(o_refs):
        for j, (cx, cy) in enumerate(chips):
            slot = 2 * cx + cy
            sends.append(remote(b, j, mine, half_rows(b, c), (cx, cy, c)))
            remote(b, j, slot, half_rows(b, c), (cx, cy, c)).wait_recv()
            for start, size in _pieces(o_ref.shape[1] // 2, D2D_PIECES, _row_unit(o_ref.dtype)):
                remote(b, 3 + j, slot, half_rows(b, c, start, size), (x, y, 1 - c)).start()
            sends.append(remote(b, 3 + j, slot, half_rows(b, c), (x, y, 1 - c)))
    for b in range(len(o_refs)):
        for j, (cx, cy) in enumerate(chips):
            remote(b, 3 + j, 2 * cx + cy, half_rows(b, 1 - c), (x, y, 1 - c)).wait_recv()
    for cp in sends:
        cp.wait_send()


def gather_ride(placed):
    n = len(placed)
    return Ride(placed, [jax.ShapeDtypeStruct(p.shape, p.dtype) for p in placed], True,
                [pltpu.SemaphoreType.DMA((n, 6)), pltpu.SemaphoreType.DMA((n, 6))],
                lambda ins, outs, sems: _gather_start(outs, *sems), lambda ins, outs, sems: _gather_finish(outs, *sems))


def gather_shards(name, placed):
    n = len(placed)

    def body(*refs):
        o_refs, sems = refs[n:2 * n], refs[2 * n:]
        _gather_start(o_refs, *sems)
        _gather_finish(o_refs, *sems)

    return pl.pallas_call(
        body, name=name, in_specs=[ANY] * n, out_specs=[ANY] * n,
        out_shape=[jax.ShapeDtypeStruct(p.shape, p.dtype) for p in placed],
        input_output_aliases={i: i for i in range(n)},
        scratch_shapes=[pltpu.SemaphoreType.DMA((n, 6)), pltpu.SemaphoreType.DMA((n, 6))],
    )(*placed)


def _split_copies(g_refs, t_refs, send_sems, recv_sems, start):
    x, y, c, _ = _place()
    waits = []
    for b, (g_ref, t_ref) in enumerate(zip(g_refs, t_refs)):
        half = g_ref.shape[2]

        def copy(slots, first, size):
            rws = pl.ds(first, size)
            return pltpu.make_async_remote_copy(src_ref=g_ref.at[slots, 1 - c, rws, :], dst_ref=t_ref.at[slots, rws, :],
                                                send_sem=send_sems.at[b], recv_sem=recv_sems.at[b], device_id=(x, y, 1 - c),
                                                device_id_type=MESH)

        if start:
            for s in range(4):
                for first, size in _pieces(half, D2D_PIECES // 4, _row_unit(g_ref.dtype)):
                    copy(s, first, size).start()
        else:
            waits.append(copy(slice(None), 0, half))
    return waits


def _split_finish(g_refs, t_refs, send_sems, recv_sems):
    for cp in _split_copies(g_refs, t_refs, send_sems, recv_sems, False):
        cp.wait()


def _split_shapes(bufs):
    return [jax.ShapeDtypeStruct((4,) + g.shape[2:], g.dtype) for g in bufs]


def split_ride(bufs):
    n = len(bufs)
    return Ride(bufs, _split_shapes(bufs), False, [pltpu.SemaphoreType.DMA((n,)), pltpu.SemaphoreType.DMA((n,))],
                lambda ins, outs, sems: _split_copies(ins, outs, *sems, True), lambda ins, outs, sems: _split_finish(ins, outs, *sems))


def pair_split(name, bufs):
    n = len(bufs)

    def body(*refs):
        g_refs, t_refs, sems = refs[:n], refs[n:2 * n], refs[2 * n:]
        _split_copies(g_refs, t_refs, *sems, True)
        _split_finish(g_refs, t_refs, *sems)

    return pl.pallas_call(
        body, name=name, in_specs=[ANY] * n, out_specs=[ANY] * n, out_shape=_split_shapes(bufs),
        scratch_shapes=[pltpu.SemaphoreType.DMA((n,)), pltpu.SemaphoreType.DMA((n,))],
    )(*bufs)


def pair_add(name, g, theirs, transit):
    _, _, half, cols = g.shape
    br = _pick(half, 128, 16)

    def body(c_ref, g_ref, t_ref, o_ref):
        o_ref[...] = (g_ref[...].astype(F32) + t_ref[...].astype(F32)).astype(transit)

    blk = (4, br, cols)
    return pl.pallas_call(
        body, name=name,
        grid_spec=pltpu.PrefetchScalarGridSpec(
            num_scalar_prefetch=1, grid=(half // br,),
            in_specs=[pl.BlockSpec((4, None, br, cols), lambda i, c: (0, c[0], i, 0)), pl.BlockSpec(blk, lambda i, c: (0, i, 0))],
            out_specs=pl.BlockSpec(blk, lambda i, c: (0, i, 0))),
        out_shape=jax.ShapeDtypeStruct((4, half, cols), transit),
        compiler_params=pltpu.CompilerParams(dimension_semantics=("arbitrary",), vmem_limit_bytes=VMEM_LIMIT),
    )(_scalar(lax.axis_index("c")), g, theirs)


def _exchange_copies(a_refs, o_refs, send_sems, recv_sems, start):
    x, y, c, chips = _place()
    mine = 2 * x + y
    waits = []
    for b, (a_ref, o_ref) in enumerate(zip(a_refs, o_refs)):
        rows = a_ref.shape[1]
        for j, (cx, cy) in enumerate(chips):
            def copy(first, size):
                rws = pl.ds(first, size)
                return pltpu.make_async_remote_copy(src_ref=a_ref.at[2 * cx + cy, rws, :], dst_ref=o_ref.at[mine, rws, :],
                                                    send_sem=send_sems.at[b, j], recv_sem=recv_sems.at[b, j],
                                                    device_id=(cx, cy, c), device_id_type=MESH)
            if start:
                for first, size in _pieces(rows, ICI_PIECES, _row_unit(a_ref.dtype)):
                    copy(first, size).start()
            else:
                waits.append(copy(0, rows))
    return waits


def _exchange_finish(a_refs, o_refs, send_sems, recv_sems):
    for cp in _exchange_copies(a_refs, o_refs, send_sems, recv_sems, False):
        cp.wait()


def exchange_ride(parts):
    n = len(parts)
    return Ride(parts, [jax.ShapeDtypeStruct(a.shape, a.dtype) for a in parts], False,
                [pltpu.SemaphoreType.DMA((n, 3)), pltpu.SemaphoreType.DMA((n, 3))],
                lambda ins, outs, sems: _exchange_copies(ins, outs, *sems, True), lambda ins, outs, sems: _exchange_finish(ins, outs, *sems))


def chip_exchange(name, parts):
    n = len(parts)

    def body(*refs):
        a_refs, o_refs, sems = refs[:n], refs[n:2 * n], refs[2 * n:]
        _exchange_copies(a_refs, o_refs, *sems, True)
        _exchange_finish(a_refs, o_refs, *sems)

    return pl.pallas_call(
        body, name=name, in_specs=[ANY] * n, out_specs=[ANY] * n,
        out_shape=[jax.ShapeDtypeStruct(a.shape, a.dtype) for a in parts],
        scratch_shapes=[pltpu.SemaphoreType.DMA((n, 3)), pltpu.SemaphoreType.DMA((n, 3))],
    )(*parts)


def chip_add(name, got, part):
    _, rows, cols = got.shape
    br = _pick(rows, 128, 16)
    nblk = rows // br

    def body(m_ref, c_ref, got_ref, part_ref, o_ref):
        mine = m_ref[0]
        for s in range(4):
            @pl.when(mine == s)
            def _(s=s):
                val = part_ref[...].astype(F32)
                o_ref[...] = val if s == 0 else o_ref[...] + val

            @pl.when(mine != s)
            def _(s=s):
                val = got_ref[s].astype(F32)
                o_ref[...] = val if s == 0 else o_ref[...] + val

    return pl.pallas_call(
        body, name=name,
        grid_spec=pltpu.PrefetchScalarGridSpec(
            num_scalar_prefetch=2, grid=(nblk,),
            in_specs=[pl.BlockSpec((4, br, cols), lambda i, m, c: (0, i, 0)),
                      pl.BlockSpec((None, br, cols), lambda i, m, c: (m[0], i, 0))],
            out_specs=pl.BlockSpec((br, cols), lambda i, m, c: (c[0] * nblk + i, 0))),
        out_shape=jax.ShapeDtypeStruct((2 * rows, cols), F32),
        compiler_params=pltpu.CompilerParams(dimension_semantics=("arbitrary",), vmem_limit_bytes=VMEM_LIMIT),
    )(_scalar(2 * lax.axis_index("x") + lax.axis_index("y")), _scalar(lax.axis_index("c")), got, part)


def pair_join(name, fulls):
    n = len(fulls)

    def body(*refs):
        o_refs = refs[n:2 * n]
        send_sems, recv_sems = refs[2 * n:]
        x, y, c, _ = _place()
        waits = []
        for b, o_ref in enumerate(o_refs):
            half = o_ref.shape[0] // 2
            unit = _row_unit(o_ref.dtype)

            def copy(start, size):
                piece = o_ref.at[pl.ds(pl.multiple_of(c * half + start, unit), size), :]
                return pltpu.make_async_remote_copy(src_ref=piece, dst_ref=piece, send_sem=send_sems.at[b], recv_sem=recv_sems.at[b],
                                                    device_id=(x, y, 1 - c), device_id_type=MESH)

            for start, size in _pieces(half, D2D_PIECES, unit):
                copy(start, size).start()
            waits.append(copy(0, half))
        for cp in waits:
            cp.wait()

    return pl.pallas_call(
        body, name=name, in_specs=[ANY] * n, out_specs=[ANY] * n,
        out_shape=[jax.ShapeDtypeStruct(f.shape, f.dtype) for f in fulls],
        input_output_aliases={i: i for i in range(n)},
        scratch_shapes=[pltpu.SemaphoreType.DMA((n,)), pltpu.SemaphoreType.DMA((n,))],
    )(*fulls)


D_IN = 15920
D_PROJ = 16000
_SEGMENTS = ((0, 8192), (8208, 12816), (12848, 15920), (8192, 8208), (12816, 12848))
OFF_SB_Z, OFF_GDN_QKV, OFF_GDN_Z, OFF_SSM_Z, OFF_SSM_XBC, OFF_GATES, OFF_SMALL = 3072, 4096, 7168, 8192, 10240, 12800, 15872
PACK_C = 1024
WEIGHTS = ("meta_tokens", "norm_g", "w_in", "gdn_conv_w", "gdn_a_log", "gdn_dt_bias", "gdn_norm_g", "ssm_conv_w", "ssm_conv_b",
           "ssm_a_log", "ssm_dt_bias", "ssm_d", "ssm_norm_g", "w_branch_a", "w_branch_b", "w_branch_c", "w_out", "final_norm_g")
SHARDED = ("w_in", "w_branch_a", "w_branch_b", "w_branch_c", "w_out", "gdn_conv_w", "ssm_conv_w", "meta_tokens")
SHARD_AXIS = {"w_in": 2, "w_branch_a": 1, "w_branch_b": 1, "w_branch_c": 1, "w_out": 1, "gdn_conv_w": 2, "ssm_conv_w": 2, "meta_tokens": 1}
BRANCH = ("w_branch_a", "w_branch_b", "w_branch_c", "w_out")
EXACT = ("gdn_conv_w", "ssm_conv_w", "meta_tokens")
REPLICATED = tuple(n for n in WEIGHTS if n not in SHARDED)


def _regrouped_from_shards(shard_cols):
    out = []
    for a, b in _SEGMENTS:
        while a < b:
            chip = a // shard_cols
            stop = min(b, (chip + 1) * shard_cols)
            out.append((chip, a - chip * shard_cols, stop - chip * shard_cols))
            a = stop
    return out


def _shard_from_regrouped(chip, shard_cols):
    lo, hi = chip * shard_cols, (chip + 1) * shard_cols
    out, pos = [], 0
    starts = {}
    for a, b in _SEGMENTS:
        starts[(a, b)] = pos
        pos += b - a
    for a, b in sorted(_SEGMENTS):
        s0, s1 = max(a, lo), min(b, hi)
        if s0 < s1:
            out.append((starts[(a, b)] + s0 - a, starts[(a, b)] + s1 - a))
    return out


def _pack(parts, row_unit=64):
    n = sum(p.shape[0] for p in parts)
    rows = -(-n // (PACK_C * row_unit)) * row_unit
    flat = jnp.concatenate(list(parts) + [jnp.zeros((rows * PACK_C - n,), parts[0].dtype)])
    return flat.reshape(rows, PACK_C)


def _unpack(buf, shapes):
    flat, out, pos = buf.reshape(-1), [], 0
    for shp in shapes:
        n = math.prod(shp)
        out.append(flat[pos:pos + n].reshape(shp))
        pos += n
    return out


def _as_bf16_words(a):
    return lax.bitcast_convert_type(a, BF16).reshape(-1)


BRANCH_ROWS = (D_MODEL // 4, D_MODEL // 4, SSM_INNER // 4, D_MODEL // 4)


def _place_weights(w):
    depth = w["w_in"].shape[0]
    layers = []
    for l in range(depth):
        a = w["w_in"][l].astype(BF16)
        b = jnp.concatenate([w[n][l] for n in BRANCH], axis=0).astype(BF16)
        layers.append([place_shard("place_w_in", a), place_shard("place_branch", b)])
    small = place_shard("place_exact", _pack([_as_bf16_words(w[n]) for n in EXACT]))
    return layers, small


def _exact_weights(w, got_s):
    per_chip = [_unpack(got_s[c], [w[n].shape + (2,) for n in EXACT]) for c in range(4)]
    return {n: jnp.concatenate([lax.bitcast_convert_type(per_chip[c][i], F32) for c in range(4)], axis=SHARD_AXIS[n])
            for i, n in enumerate(EXACT)}


def _proj_weight(got_a):
    d_model, shard_cols = got_a.shape[1:]
    pad = jnp.zeros((d_model, D_PROJ - D_IN), BF16)
    return jnp.concatenate([got_a[c, :, lo:hi] for c, lo, hi in _regrouped_from_shards(shard_cols)] + [pad], axis=1)


def _branch_weights(got_b):
    out, pos = {}, 0
    for n, rows in zip(("wa", "wb", "wc", "wo"), BRANCH_ROWS):
        out[n] = jnp.concatenate([got_b[c, pos:pos + rows] for c in range(4)], axis=0)
        pos += rows
    return out


def _shard(a, axis, s):
    size = a.shape[axis] // 4
    return lax.slice_in_dim(a, s * size, (s + 1) * size, axis=axis)


def _layer_grad_buffers(g, shard_cols):
    buf_a = jnp.stack([jnp.concatenate([g["w_in"][:, lo:hi] for lo, hi in _shard_from_regrouped(s, shard_cols)], axis=1)
                       for s in range(4)]).astype(BF16)
    buf_b = jnp.stack([jnp.concatenate([_shard(g[n], 0, s) for n in BRANCH], axis=0) for s in range(4)]).astype(BF16)
    return [buf_a, buf_b]


def _split_halves(buf):
    return buf.reshape(4, 2, buf.shape[1] // 2, buf.shape[2])


def _start_reduce(tag, bufs, transits):
    bufs = [_split_halves(g) for g in bufs]
    theirs = pair_split(tag + "_pair_split", bufs)
    return [pair_add(f"{tag}_pair_add_{i}", g, t, tr) for i, (g, t, tr) in enumerate(zip(bufs, theirs, transits))]


def _layer_params(w, exact, wp, l):
    lane = lambda v, lo: jnp.pad(v, (lo, HEAD_DIM - lo - v.shape[0]))[None]
    return dict(
        norm_g=w["norm_g"][l][None], wp=wp,
        gdn_conv_w=exact["gdn_conv_w"][l], gdn_conv_b=jnp.zeros((1, 3 * N_HEADS * HEAD_DIM), F32),
        ssm_conv_w=exact["ssm_conv_w"][l], ssm_conv_b=w["ssm_conv_b"][l][None],
        bias_vec=lane(w["gdn_dt_bias"][l], 8) + lane(w["ssm_dt_bias"][l], 16),
        alog_vec=lane(w["gdn_a_log"][l], 8) + lane(w["ssm_a_log"][l], 16),
        gdn_norm_g=w["gdn_norm_g"][l][None], d_skip=jnp.repeat(w["ssm_d"][l], SSM_P)[None], ssm_norm_g=w["ssm_norm_g"][l][None])


def _layer_fwd(h, p, ride):
    lp = h.shape[0]
    bm = _pick(lp, 544, 16)
    kw = dict(bm=bm, lp=lp)
    (u,) = rowmap_fwd("rms_fwd", _f_rmsnorm, [(h, 0)], [(p["norm_g"], False)], 1, width=D_MODEL, ncol=1, out_dtypes=[BF16], **kw)
    proj = matmul(u, p["wp"], "nn", name="proj", bm=lp, bn=640)
    o_a_raw, *rode = sb_fwd(proj, (0, N_HEADS, 2 * N_HEADS), ride)
    p = dict(p, **_branch_weights(rode[0]))
    qkv = conv_fwd("gdn_conv_fwd", proj, OFF_GDN_QKV // HEAD_DIM, p["gdn_conv_w"], p["gdn_conv_b"], 2 * N_HEADS)
    first, second = rowmap_fwd("gates_fwd", _f_small_gates, [(proj, OFF_SMALL // HEAD_DIM)],
                               [(p["bias_vec"], False), (p["alog_vec"], False)], 2, width=HEAD_DIM, ncol=1, **kw)
    o_b_raw, gdn_states, gdn_t = gdn_fwd(qkv, first)
    xbc = conv_fwd("ssm_conv_fwd", proj, OFF_SSM_XBC // HEAD_DIM, p["ssm_conv_w"], p["ssm_conv_b"], 0)
    y_raw, ssd_states = ssd_fwd(xbc, first, second)
    (o_a,) = rowmap_fwd("gate_a_fwd", _f_gate_silu, [(o_a_raw, 0), (proj, OFF_SB_Z // 1024)], [], 1, width=1024, ncol=1,
                        out_dtypes=[BF16], **kw)
    (o_b,) = rowmap_fwd("gate_b_fwd", _f_head_norm_gate, [(o_b_raw, 0), (proj, OFF_GDN_Z // 1024)], [(p["gdn_norm_g"], False)], 1,
                        width=1024, ncol=1, out_dtypes=[BF16], **kw)
    (o_c,) = rowmap_fwd("gate_c_fwd", _f_ssm_out, [(y_raw, 0), (xbc, 0), (proj, OFF_SSM_Z // 1024)],
                        [(p["d_skip"], True), (p["ssm_norm_g"], True)], 1, width=1024, ncol=SSM_GROUPS, out_dtypes=[BF16], **kw)
    pa = matmul(o_a, p["wa"], "nn", name="branch_a", bm=lp // 2, bn=512)
    pb = matmul(o_b, p["wb"], "nn", name="branch_b", bm=lp // 2, bn=512)
    pc = matmul(o_c, p["wc"], "nn", name="branch_c", bm=lp // 2, bn=512)
    merge_rows = [(pa, 0), (pb, 0), (pc, 0)] + [(proj, OFF_GATES // 512 + 2 * i) for i in range(3)]
    (merged,) = rowmap_fwd("merge_fwd", _f_merge, merge_rows, [], 1, width=512, ncol=2, out_dtypes=[BF16], **kw)
    h_out = matmul(merged, p["wo"], "nn", name="out_proj", bm=lp, bn=512, residual=h)
    saved = dict(h=h, u=u, proj=proj, qkv=qkv, first=first, second=second, o_a_raw=o_a_raw, o_b_raw=o_b_raw,
                 gdn_states=gdn_states, gdn_t=gdn_t, xbc=xbc, y_raw=y_raw, ssd_states=ssd_states, o_a=o_a, o_b=o_b, o_c=o_c, pa=pa, pb=pb, pc=pc,
                 merged=merged, params=p)
    return h_out, saved, rode[1:]


def _layer_bwd(d_h, p, s, ride):
    lp = d_h.shape[0]
    bm = _pick(lp, 544, 16)
    kw = dict(bm=bm, lp=lp)
    proj = s["proj"]
    g = {}
    d_merged = matmul(d_h, p["wo"], "nt", name="d_merged", bm=lp, bn=512)
    g["w_out"] = matmul(s["merged"], d_h, "tn", name="g_w_out", bm=512, bn=1024, bk=lp)
    merge_rows = [(s["pa"], 0), (s["pb"], 0), (s["pc"], 0)] + [(proj, OFF_GATES // 512 + 2 * i) for i in range(3)]
    (d_pa, d_pb, d_pc, d_ga, d_gb, d_gc), _ = rowmap_bwd("merge_bwd", _f_merge, merge_rows, [], [d_merged], width=512, ncol=2,
                                                         d_row_dtypes=[BF16] * 6, **kw)
    g["w_branch_a"] = matmul(s["o_a"], d_pa, "tn", name="g_w_a", bm=512, bn=1024, bk=lp)
    g["w_branch_b"] = matmul(s["o_b"], d_pb, "tn", name="g_w_b", bm=512, bn=1024, bk=lp)
    g["w_branch_c"] = matmul(s["o_c"], d_pc, "tn", name="g_w_c", bm=512, bn=1024, bk=lp)
    d_oa = matmul(d_pa, p["wa"], "nt", name="d_o_a", bm=lp, bn=512)
    d_ob = matmul(d_pb, p["wb"], "nt", name="d_o_b", bm=lp, bn=512)
    d_oc = matmul(d_pc, p["wc"], "nt", name="d_o_c", bm=lp, bn=512)
    (d_oa_raw, d_sbz), _ = rowmap_bwd("gate_a_bwd", _f_gate_silu, [(s["o_a_raw"], 0), (proj, OFF_SB_Z // 1024)], [], [d_oa],
                                      width=1024, ncol=1, d_row_dtypes=[F32, BF16], **kw)
    (d_ob_raw, d_gdz), (g["gdn_norm_g"],) = rowmap_bwd(
        "gate_b_bwd", _f_head_norm_gate, [(s["o_b_raw"], 0), (proj, OFF_GDN_Z // 1024)], [(p["gdn_norm_g"], False)], [d_ob],
        width=1024, ncol=1, d_row_dtypes=[F32, BF16], **kw)
    (d_y, d_xh, d_ssz), (g_dskip, g["ssm_norm_g"]) = rowmap_bwd(
        "gate_c_bwd", _f_ssm_out, [(s["y_raw"], 0), (s["xbc"], 0), (proj, OFF_SSM_Z // 1024)],
        [(p["d_skip"], True), (p["ssm_norm_g"], True)], [d_oc], width=1024, ncol=SSM_GROUPS, d_row_dtypes=[F32, F32, BF16], **kw)
    g["gdn_norm_g"], g["ssm_norm_g"] = g["gdn_norm_g"][0], g["ssm_norm_g"][0]
    g["ssm_d"] = g_dskip.reshape(SSM_HEADS, SSM_P).sum(axis=1)
    d_q, d_k, d_v, *rode = sb_bwd(proj, (0, N_HEADS, 2 * N_HEADS), d_oa_raw, ride)
    d_qkv, d_first_gdn = gdn_bwd(s["qkv"], s["first"], s["gdn_states"], s["gdn_t"], d_ob_raw)
    d_xbc_out, d_first_ssd, d_second = ssd_bwd(s["xbc"], s["first"], s["second"], s["ssd_states"], d_y, d_xh)
    d_gdqkv, g["gdn_conv_w"], _ = conv_bwd("gdn_conv_bwd", proj, OFF_GDN_QKV // HEAD_DIM, p["gdn_conv_w"], p["gdn_conv_b"], 2 * N_HEADS,
                                           d_qkv)
    d_xbc, g["ssm_conv_w"], g_cb = conv_bwd("ssm_conv_bwd", proj, OFF_SSM_XBC // HEAD_DIM, p["ssm_conv_w"], p["ssm_conv_b"], 0, d_xbc_out)
    g["ssm_conv_b"] = g_cb[0]
    d_first = d_first_gdn + d_first_ssd
    (d_small,), (g_bias, g_alog) = rowmap_bwd("gates_bwd", _f_small_gates, [(proj, OFF_SMALL // HEAD_DIM)],
                                              [(p["bias_vec"], False), (p["alog_vec"], False)], [d_first, d_second],
                                              width=HEAD_DIM, ncol=1, d_row_dtypes=[BF16], **kw)
    g["gdn_dt_bias"], g["ssm_dt_bias"] = g_bias[0, 8:16], g_bias[0, 16:48]
    g["gdn_a_log"], g["ssm_a_log"] = g_alog[0, 8:16], g_alog[0, 16:48]
    d_proj = jnp.concatenate([d_q, d_k, d_v, d_sbz, d_gdqkv, d_gdz, d_ssz, d_xbc, d_ga, d_gb, d_gc, d_small], axis=1)
    g["w_in"] = matmul(s["u"], d_proj, "tn", name="g_w_in", bm=1024, bn=640, bk=lp, out_dtype=BF16)
    bufs = [_split_halves(buf) for buf in _layer_grad_buffers(g, D_IN // 4)]
    d_u, *theirs = matmul(d_proj, p["wp"], "nt", name="d_u", bm=lp, bn=1024, bk=1600, ride=split_ride(bufs))
    parts = [pair_add(f"grads_pair_add_{i}", buf, t, BF16) for i, (buf, t) in enumerate(zip(bufs, theirs))]
    (d_hn,), (g_norm,) = rowmap_bwd("rms_bwd", _f_rmsnorm, [(s["h"], 0)], [(p["norm_g"], False)], [d_u], width=D_MODEL, ncol=1, **kw)
    g["norm_g"] = g_norm[0]
    return d_h + d_hn, g, rode, parts


def kernel(x, meta_tokens, norm_g, w_in, gdn_conv_w, gdn_a_log, gdn_dt_bias, gdn_norm_g, ssm_conv_w, ssm_conv_b, ssm_a_log, ssm_dt_bias, ssm_d, ssm_norm_g, w_branch_a, w_branch_b, w_branch_c, w_out, final_norm_g, loss_target, m_meta_tokens, m_norm_g, m_w_in, m_gdn_conv_w, m_gdn_a_log, m_gdn_dt_bias, m_gdn_norm_g, m_ssm_conv_w, m_ssm_conv_b, m_ssm_a_log, m_ssm_dt_bias, m_ssm_d, m_ssm_norm_g, m_w_branch_a, m_w_branch_b, m_w_branch_c, m_w_out, m_final_norm_g, v_meta_tokens, v_norm_g, v_w_in, v_gdn_conv_w, v_gdn_a_log, v_gdn_dt_bias, v_gdn_norm_g, v_ssm_conv_w, v_ssm_conv_b, v_ssm_a_log, v_ssm_dt_bias, v_ssm_d, v_ssm_norm_g, v_w_branch_a, v_w_branch_b, v_w_branch_c, v_w_out, v_final_norm_g):
    w = dict(meta_tokens=meta_tokens, norm_g=norm_g, w_in=w_in, gdn_conv_w=gdn_conv_w, gdn_a_log=gdn_a_log, gdn_dt_bias=gdn_dt_bias,
             gdn_norm_g=gdn_norm_g, ssm_conv_w=ssm_conv_w, ssm_conv_b=ssm_conv_b, ssm_a_log=ssm_a_log, ssm_dt_bias=ssm_dt_bias,
             ssm_d=ssm_d, ssm_norm_g=ssm_norm_g, w_branch_a=w_branch_a, w_branch_b=w_branch_b, w_branch_c=w_branch_c, w_out=w_out,
             final_norm_g=final_norm_g)
    m = dict(meta_tokens=m_meta_tokens, norm_g=m_norm_g, w_in=m_w_in, gdn_conv_w=m_gdn_conv_w, gdn_a_log=m_gdn_a_log,
             gdn_dt_bias=m_gdn_dt_bias, gdn_norm_g=m_gdn_norm_g, ssm_conv_w=m_ssm_conv_w, ssm_conv_b=m_ssm_conv_b,
             ssm_a_log=m_ssm_a_log, ssm_dt_bias=m_ssm_dt_bias, ssm_d=m_ssm_d, ssm_norm_g=m_ssm_norm_g, w_branch_a=m_w_branch_a,
             w_branch_b=m_w_branch_b, w_branch_c=m_w_branch_c, w_out=m_w_out, final_norm_g=m_final_norm_g)
    v = dict(meta_tokens=v_meta_tokens, norm_g=v_norm_g, w_in=v_w_in, gdn_conv_w=v_gdn_conv_w, gdn_a_log=v_gdn_a_log,
             gdn_dt_bias=v_gdn_dt_bias, gdn_norm_g=v_gdn_norm_g, ssm_conv_w=v_ssm_conv_w, ssm_conv_b=v_ssm_conv_b,
             ssm_a_log=v_ssm_a_log, ssm_dt_bias=v_ssm_dt_bias, ssm_d=v_ssm_d, ssm_norm_g=v_ssm_norm_g, w_branch_a=v_w_branch_a,
             w_branch_b=v_w_branch_b, w_branch_c=v_w_branch_c, w_out=v_w_out, final_norm_g=v_final_norm_g)
    depth = norm_g.shape[0]
    placed, placed_small = _place_weights(w)
    got_a, got_s = gather_shards("gather_first", [placed[0][0], placed_small])
    exact = _exact_weights(w, got_s)

    h = jnp.concatenate([jnp.zeros((PAD, D_MODEL), F32), exact["meta_tokens"], x[0]], axis=0)
    params, saved = [], []
    for l in range(depth):
        ride = gather_ride([placed[l][1]] + ([placed[l + 1][0]] if l + 1 < depth else []))
        h, s, rode = _layer_fwd(h, _layer_params(w, exact, _proj_weight(got_a), l), ride)
        params.append(s.pop("params"))
        saved.append(s)
        if rode:
            (got_a,) = rode
    loss, d_h, g_final = loss_head(h, loss_target[0], final_norm_g[None])

    layer_grads, reds, waiting = [None] * depth, [None] * depth, None
    for l in reversed(range(depth)):
        d_h, layer_grads[l], rode, parts = _layer_bwd(d_h, params[l], saved[l], exchange_ride(waiting) if waiting else None)
        if waiting:
            reds[l + 1] = [chip_add(f"grads_chip_add_{i}", gt, p) for i, (gt, p) in enumerate(zip(rode, waiting))]
        waiting = parts
    grads = {n: jnp.stack([layer_grads[l][n] for l in range(depth)]) for n in WEIGHTS
             if n not in ("meta_tokens", "final_norm_g", "w_in") + BRANCH}
    grads["meta_tokens"] = d_h[PAD:PAD + N_META]
    grads["final_norm_g"] = g_final[0]
    grad_x = d_h[PAD + N_META:][None]
    buf_s = jnp.stack([_pack([_shard(grads[n], SHARD_AXIS[n], s).astype(BF16).reshape(-1) for n in EXACT], row_unit=32) for s in range(4)])
    small = _pack([grads[n].reshape(-1) for n in REPLICATED], row_unit=32)
    last = waiting + _start_reduce("small_grads", [buf_s, jnp.broadcast_to(small[None], (4,) + small.shape)], [BF16, F32])
    got = chip_exchange("grads_chip_exchange", last)
    sums = [chip_add(f"grads_chip_add_{i}", gt, p) for i, (gt, p) in enumerate(zip(got, last))]
    reds[0] = sums[:2]
    joined = pair_join("grads_pair_join", [r for layer in reds for r in layer] + sums[2:])
    red = {"w_in": jnp.stack(joined[0:2 * depth:2])}
    pos = 0
    for n, rows in zip(BRANCH, BRANCH_ROWS):
        red[n] = jnp.stack([joined[2 * l + 1][pos:pos + rows] for l in range(depth)])
        pos += rows
    red.update(zip(EXACT, _unpack(joined[-2], [w[n].shape for n in EXACT])))
    small_red = joined[-1]
    delta, new_m, new_v = {}, {}, {}
    for n in SHARDED:
        if w[n].shape[-1] % HEAD_DIM:
            to_view, from_view = (lambda a: jnp.transpose(a, (2, 0, 1))), (lambda a: jnp.transpose(a, (1, 2, 0)))
            delta[n], new_m[n], new_v[n], red[n] = [from_view(a) for a in adamw("adamw_" + n, to_view(w[n]), to_view(red[n]),
                                                                                to_view(m[n]), to_view(v[n]), echo_g=True)]
        else:
            delta[n], new_m[n], new_v[n] = adamw("adamw_" + n, w[n], red[n], m[n], v[n])
    pack_small = lambda d: _pack([d[n].reshape(-1) for n in REPLICATED], row_unit=32)
    small = adamw("adamw_small", pack_small(w), small_red, pack_small(m), pack_small(v))
    shapes = [w[n].shape for n in REPLICATED]
    red.update(zip(REPLICATED, _unpack(small_red, shapes)))
    for d, buf in zip((delta, new_m, new_v), small):
        d.update(zip(REPLICATED, _unpack(buf, shapes)))
    total_loss = lax.psum(loss[0, 0], ("x", "y", "c"))
    return (total_loss, grad_x, *[red[n] for n in WEIGHTS], *[delta[n] for n in WEIGHTS], *[new_m[n] for n in WEIGHTS],
            *[new_v[n] for n in WEIGHTS])
```

```python
import functools
import math

import jax
import jax.numpy as jnp
from jax import lax
from jax.experimental import pallas as pl
from jax.experimental.pallas import tpu as pltpu

F32 = jnp.float32
BF16 = jnp.bfloat16

N_META = 16
RMS_EPS = 1e-6
L2_EPS = 1e-6
CONV_K = 4
D_MODEL = 1024
HEAD_DIM = 128
N_HEADS = 8
CHUNK = 64
SB_BLOCK = 128
PAD = SB_BLOCK - N_META
SSM_INNER = 2048
SSM_P = 64
SSM_HEADS = 32
SSM_GROUPS = 2
SSM_HG = SSM_HEADS // SSM_GROUPS
SSM_N = 128
VMEM_LIMIT = 56 * 1024 * 1024

def _dims(mode, ndim):
    lhs, rhs = {"nn": (1, 0), "nt": (1, 1), "tn": (0, 0)}[mode]
    off = ndim - 2
    return (((lhs + off,), (rhs + off,)), (tuple(range(off)), tuple(range(off))))


def _dot(a, b, mode):
    return lax.dot_general(a, b, _dims(mode, a.ndim), preferred_element_type=F32)


def _halves(a):
    hi = a.astype(BF16)
    return hi, (a - hi.astype(F32)).astype(BF16)


def _mm_raw(a, b, mode, kind):
    if kind == "bf16":
        return _dot(a.astype(BF16), b.astype(BF16), mode)
    if kind == "lhs01":
        hi, lo = _halves(b)
        a = a.astype(BF16)
        return _dot(a, hi, mode) + _dot(a, lo, mode)
    if kind == "rhs01":
        hi, lo = _halves(a)
        b = b.astype(BF16)
        return _dot(hi, b, mode) + _dot(lo, b, mode)
    a_hi, a_lo = _halves(a)
    b_hi, b_lo = _halves(b)
    return _dot(a_hi, b_hi, mode) + (_dot(a_hi, b_lo, mode) + _dot(a_lo, b_hi, mode))


@functools.partial(jax.custom_vjp, nondiff_argnums=(2, 3))
def _mm(a, b, mode="nn", kind="bf16"):
    return _mm_raw(a, b, mode, kind)


def _mm_fwd(a, b, mode, kind):
    return _mm_raw(a, b, mode, kind), (a, b)


def _mm_bwd(mode, kind, res, g):
    a, b = res
    if kind == "lhs01":
        return jnp.zeros_like(a), _mm_raw(a, g, {"nn": "tn", "tn": "nn"}[mode], "lhs01")
    if kind == "rhs01":
        return _mm_raw(g, b, {"nn": "nt", "nt": "nn"}[mode], "rhs01"), jnp.zeros_like(b)
    if mode == "nn":
        return _mm_raw(g, b, "nt", kind), _mm_raw(a, g, "tn", kind)
    if mode == "nt":
        return _mm_raw(g, b, "nn", kind), _mm_raw(g, a, "tn", kind)
    return _mm_raw(b, g, "nt", kind), _mm_raw(a, g, "nn", kind)


_mm.defvjp(_mm_fwd, _mm_bwd)


def _iota2(shape, axis):
    return lax.broadcasted_iota(jnp.int32, shape, axis)


def _inv_unit_lower_raw(m):
    size = m.shape[-1]
    eye = (_iota2((size, size), 0) == _iota2((size, size), 1)).astype(F32)
    n = -m
    t = eye + n
    p = n
    steps = int(math.log2(size)) - 1
    for _ in range(steps):
        p = _mm_raw(p, p, "nn", "x3")
        t = t + _mm_raw(t, p, "nn", "x3")
    return t


@jax.custom_vjp
def _inv_unit_lower(m):
    return _inv_unit_lower_raw(m)


def _inv_fwd(m):
    t = _inv_unit_lower_raw(m)
    return t, t


def _inv_bwd(t, g):
    return (-_mm_raw(_mm_raw(t, g, "tn", "x3"), t, "nt", "x3"),)


_inv_unit_lower.defvjp(_inv_fwd, _inv_bwd)


@jax.custom_vjp
def _inv_known(m, t):
    return t


_inv_known.defvjp(lambda m, t: (t, t), lambda t, g: (_inv_bwd(t, g)[0], jnp.zeros_like(t)))


def _safe_decay(col, row, keep):
    return jnp.where(keep, jnp.exp(jnp.where(keep, col - row, 0.0)), 0.0)


def _col_to_row(col):
    n = col.shape[-2]
    eye = _iota2((n, n), 0) == _iota2((n, n), 1)
    return jnp.sum(jnp.where(eye, col, 0.0), axis=-2, keepdims=True)


def _cumsum_col(col):
    n = col.shape[-2]
    li, si = _iota2((n, n), 0), _iota2((n, n), 1)
    row = _col_to_row(col)
    c_col = jnp.sum(jnp.where(li >= si, row, 0.0), axis=-1, keepdims=True)
    c_row = jnp.sum(jnp.where(li <= si, col, 0.0), axis=-2, keepdims=True)
    return c_col, c_row


def _gdn_chunk(q, k, v, g, beta, state, t_known=None):
    cl = q.shape[-2]
    li, si = _iota2((cl, cl), 0), _iota2((cl, cl), 1)
    gc_col, gc_row = _cumsum_col(g)
    g_last = jnp.sum(g, axis=-2, keepdims=True)
    dec_strict = _safe_decay(gc_col, gc_row, li > si)
    dec_incl = _safe_decay(gc_col, gc_row, li >= si)
    e_gc = jnp.exp(gc_col)
    qs = q * (HEAD_DIM ** -0.5)
    kb = k * beta
    m = _mm(kb, k, "nt") * dec_strict
    t_inv = _inv_unit_lower(m) if t_known is None else _inv_known(m, t_known)
    u = _mm(t_inv, v * beta)
    w = _mm(t_inv, kb * e_gc)
    a_qk = _mm(qs, k, "nt") * dec_incl
    q_dec = qs * e_gc
    k_end = k * jnp.exp(g_last - gc_col)
    v_new = u - _mm(w, state)
    o = _mm(q_dec, state) + _mm(a_qk, v_new)
    new_state = state * jnp.exp(g_last) + _mm(k_end, v_new, "tn")
    return o, new_state, t_inv


def _gdn_operands(qkv_ref, gt):
    nh, width = N_HEADS, N_HEADS * HEAD_DIM
    heads = lambda off: jnp.stack([qkv_ref[:, off + h * HEAD_DIM:off + (h + 1) * HEAD_DIM] for h in range(nh)])
    cols = lambda off: jnp.stack([gt[:, off + h:off + h + 1] for h in range(nh)])
    return heads(0), heads(width), heads(2 * width), cols(nh), cols(0)


def gdn_fwd(qkv, gates):
    lp = qkv.shape[0]
    nh = N_HEADS
    nc = lp // CHUNK
    width = nh * HEAD_DIM

    def body(qkv_ref, gt_ref, o_ref, s_ref, t_ref, state):
        @pl.when(pl.program_id(0) == 0)
        def _():
            state[...] = jnp.zeros_like(state)

        s_in = state[...]
        s_ref[0] = s_in
        o, s_new, t_inv = _gdn_chunk(*_gdn_operands(qkv_ref, gt_ref[...]), s_in)
        t_ref[0] = t_inv
        for h in range(nh):
            o_ref[:, h * HEAD_DIM:(h + 1) * HEAD_DIM] = o[h]
        state[...] = s_new

    return pl.pallas_call(
        body, name="gdn_fwd", grid=(nc,),
        in_specs=[pl.BlockSpec((CHUNK, 3 * width), lambda c: (c, 0)), pl.BlockSpec((CHUNK, HEAD_DIM), lambda c: (c, 0))],
        out_specs=[pl.BlockSpec((CHUNK, width), lambda c: (c, 0)), pl.BlockSpec((1, nh, HEAD_DIM, HEAD_DIM), lambda c: (c, 0, 0, 0)),
                   pl.BlockSpec((1, nh, CHUNK, CHUNK), lambda c: (c, 0, 0, 0))],
        out_shape=[jax.ShapeDtypeStruct((lp, width), F32), jax.ShapeDtypeStruct((nc, nh, HEAD_DIM, HEAD_DIM), F32),
                   jax.ShapeDtypeStruct((nc, nh, CHUNK, CHUNK), F32)],
        scratch_shapes=[pltpu.VMEM((nh, HEAD_DIM, HEAD_DIM), F32)],
        compiler_params=pltpu.CompilerParams(dimension_semantics=("arbitrary",), vmem_limit_bytes=VMEM_LIMIT),
    )(qkv, gates)


def gdn_bwd(qkv, gates, states, t_invs, d_o):
    lp = qkv.shape[0]
    nh = N_HEADS
    nc = lp // CHUNK
    width = nh * HEAD_DIM

    def body(qkv_ref, gt_ref, s_ref, t_ref, do_ref, dqkv_ref, dgt_ref, d_state):
        @pl.when(pl.program_id(0) == 0)
        def _():
            d_state[...] = jnp.zeros_like(d_state)

        t_known = t_ref[0]
        _, pull = jax.vjp(lambda *xs: _gdn_chunk(*xs, t_known=t_known)[:2], *_gdn_operands(qkv_ref, gt_ref[...]), s_ref[0])
        d_o = jnp.stack([do_ref[:, h * HEAD_DIM:(h + 1) * HEAD_DIM] for h in range(nh)])
        dq, dk, dv, dg, db, ds = pull((d_o, d_state[...]))
        lane = _iota2((CHUNK, HEAD_DIM), 1)
        d_gt = jnp.zeros((CHUNK, HEAD_DIM), F32)
        for h in range(nh):
            for part, val in enumerate((dq, dk, dv)):
                dqkv_ref[:, part * width + h * HEAD_DIM:part * width + (h + 1) * HEAD_DIM] = val[h]
            d_gt = d_gt + jnp.where(lane == h, db[h], 0.0) + jnp.where(lane == nh + h, dg[h], 0.0)
        dgt_ref[...] = d_gt
        d_state[...] = ds

    rev = lambda c: (nc - 1 - c, 0)
    return pl.pallas_call(
        body, name="gdn_bwd", grid=(nc,),
        in_specs=[pl.BlockSpec((CHUNK, 3 * width), rev), pl.BlockSpec((CHUNK, HEAD_DIM), rev),
                  pl.BlockSpec((1, nh, HEAD_DIM, HEAD_DIM), lambda c: (nc - 1 - c, 0, 0, 0)),
                  pl.BlockSpec((1, nh, CHUNK, CHUNK), lambda c: (nc - 1 - c, 0, 0, 0)), pl.BlockSpec((CHUNK, width), rev)],
        out_specs=[pl.BlockSpec((CHUNK, 3 * width), rev), pl.BlockSpec((CHUNK, HEAD_DIM), rev)],
        out_shape=[jax.ShapeDtypeStruct((lp, 3 * width), F32), jax.ShapeDtypeStruct((lp, HEAD_DIM), F32)],
        scratch_shapes=[pltpu.VMEM((nh, HEAD_DIM, HEAD_DIM), F32)],
        compiler_params=pltpu.CompilerParams(dimension_semantics=("arbitrary",), vmem_limit_bytes=VMEM_LIMIT),
    )(qkv, gates, states, t_invs, d_o)


def _head_expand():
    width = SSM_HG * SSM_P
    return (_iota2((SSM_HG, width), 1) // SSM_P == _iota2((SSM_HG, width), 0)).astype(F32)


def _ssd_chunk(x, b, c, dt, la, state):
    cl = x.shape[0]
    li, si = _iota2((cl, cl), 0), _iota2((cl, cl), 1)
    causal = li >= si
    expand = _head_expand()
    tri = causal.astype(F32)
    xs = x * _mm(dt, expand, "nn", "rhs01")
    la_x = _mm(la, expand, "nn", "rhs01")
    cs_x = _mm(tri, la_x, "nn", "lhs01")
    last_x = jnp.sum(la_x, axis=0, keepdims=True)
    cs = _mm(tri, la, "nn", "lhs01")
    scores = _mm(c, b, "nt")
    head_id = _iota2((1, SSM_HG), 1)
    per_tile = HEAD_DIM // SSM_P
    tile_head = _iota2((1, HEAD_DIM), 1) // SSM_P
    within = []
    for t in range(SSM_HG // per_tile):
        xs_t = xs[:, t * HEAD_DIM:(t + 1) * HEAD_DIM]
        acc = jnp.zeros((cl, HEAD_DIM), F32)
        for hh in range(per_tile):
            cs_col = jnp.sum(jnp.where(head_id == t * per_tile + hh, cs, 0.0), axis=1, keepdims=True)
            decay = _safe_decay(cs_col, _col_to_row(cs_col), causal)
            acc = acc + _mm(scores * decay, jnp.where(tile_head == hh, xs_t, 0.0))
        within.append(acc)
    y = _mm(c, state) * jnp.exp(cs_x) + jnp.concatenate(within, axis=1)
    new_state = state * jnp.exp(last_x) + _mm(b, xs * jnp.exp(last_x - cs_x), "tn")
    return y, new_state


GATE_DT = 16


def _place_lanes(v, lo):
    n = v.shape[1]
    sel = (_iota2((n, HEAD_DIM), 1) == _iota2((n, HEAD_DIM), 0) + lo).astype(F32)
    return _mm_raw(v, sel, "nn", "rhs01")


def ssd_fwd(xbc, first, second):
    lp = xbc.shape[0]
    nc = lp // CHUNK
    width = SSM_HG * SSM_P
    b_off, c_off = SSM_INNER, SSM_INNER + SSM_GROUPS * SSM_N

    def body(x_ref, f_ref, s2_ref, y_ref, s_ref, state):
        @pl.when(pl.program_id(0) == 0)
        def _():
            state[...] = jnp.zeros_like(state)

        f, s2 = f_ref[...], s2_ref[...]
        for g in range(SSM_GROUPS):
            lo = GATE_DT + g * SSM_HG
            s_in = state[g]
            s_ref[0, g] = s_in
            y, s_new = _ssd_chunk(x_ref[:, g * width:(g + 1) * width], x_ref[:, b_off + g * SSM_N:b_off + (g + 1) * SSM_N],
                                  x_ref[:, c_off + g * SSM_N:c_off + (g + 1) * SSM_N], f[:, lo:lo + SSM_HG], s2[:, lo:lo + SSM_HG], s_in)
            y_ref[:, g * width:(g + 1) * width] = y
            state[g] = s_new

    row = lambda cols: pl.BlockSpec((CHUNK, cols), lambda k: (k, 0))
    return pl.pallas_call(
        body, name="ssd_fwd", grid=(nc,),
        in_specs=[row(xbc.shape[1]), row(HEAD_DIM), row(HEAD_DIM)],
        out_specs=[row(SSM_INNER), pl.BlockSpec((1, SSM_GROUPS, SSM_N, width), lambda k: (k, 0, 0, 0))],
        out_shape=[jax.ShapeDtypeStruct((lp, SSM_INNER), F32), jax.ShapeDtypeStruct((nc, SSM_GROUPS, SSM_N, width), F32)],
        scratch_shapes=[pltpu.VMEM((SSM_GROUPS, SSM_N, width), F32)],
        compiler_params=pltpu.CompilerParams(dimension_semantics=("arbitrary",), vmem_limit_bytes=VMEM_LIMIT),
    )(xbc, first, second)


def ssd_bwd(xbc, first, second, states, d_y, d_xh):
    lp = xbc.shape[0]
    nc = lp // CHUNK
    width = SSM_HG * SSM_P
    b_off, c_off = SSM_INNER, SSM_INNER + SSM_GROUPS * SSM_N

    def body(x_ref, f_ref, s2_ref, s_ref, dy_ref, dxh_ref, dx_ref, df_ref, ds2_ref, d_state):
        @pl.when(pl.program_id(0) == 0)
        def _():
            d_state[...] = jnp.zeros_like(d_state)

        f, s2 = f_ref[...], s2_ref[...]
        d_f = jnp.zeros((CHUNK, HEAD_DIM), F32)
        d_s2 = jnp.zeros((CHUNK, HEAD_DIM), F32)
        for g in range(SSM_GROUPS):
            lo = GATE_DT + g * SSM_HG
            x_l = slice(g * width, (g + 1) * width)
            b_l = slice(b_off + g * SSM_N, b_off + (g + 1) * SSM_N)
            c_l = slice(c_off + g * SSM_N, c_off + (g + 1) * SSM_N)
            _, pull = jax.vjp(_ssd_chunk, x_ref[:, x_l], x_ref[:, b_l], x_ref[:, c_l], f[:, lo:lo + SSM_HG], s2[:, lo:lo + SSM_HG],
                              s_ref[0, g])
            dx, db, dc, ddt, dla, ds = pull((dy_ref[:, x_l], d_state[g]))
            dx_ref[:, x_l] = dx + dxh_ref[:, x_l]
            dx_ref[:, b_l] = db
            dx_ref[:, c_l] = dc
            d_f = d_f + _place_lanes(ddt, lo)
            d_s2 = d_s2 + _place_lanes(dla, lo)
            d_state[g] = ds
        df_ref[...] = d_f
        ds2_ref[...] = d_s2

    row = lambda cols: pl.BlockSpec((CHUNK, cols), lambda k: (nc - 1 - k, 0))
    gate_shape = jax.ShapeDtypeStruct((lp, HEAD_DIM), F32)
    return pl.pallas_call(
        body, name="ssd_bwd", grid=(nc,),
        in_specs=[row(xbc.shape[1]), row(HEAD_DIM), row(HEAD_DIM),
                  pl.BlockSpec((1, SSM_GROUPS, SSM_N, width), lambda k: (nc - 1 - k, 0, 0, 0)), row(SSM_INNER), row(SSM_INNER)],
        out_specs=[row(xbc.shape[1]), row(HEAD_DIM), row(HEAD_DIM)],
        out_shape=[jax.ShapeDtypeStruct(xbc.shape, F32), gate_shape, gate_shape],
        scratch_shapes=[pltpu.VMEM((SSM_GROUPS, SSM_N, width), F32)],
        compiler_params=pltpu.CompilerParams(dimension_semantics=("arbitrary",), vmem_limit_bytes=VMEM_LIMIT),
    )(xbc, first, second, states, d_y, d_xh)


SB_QROWS = 544


def _mm_tri(a, tri):
    return _mm_raw(a, tri.astype(BF16), "nn", "rhs01")


def _sb_scores(q_scaled, kb, row0, j):
    shape = (q_scaled.shape[0], SB_BLOCK)
    z = _mm_raw(q_scaled, kb, "nt", "bf16")
    q_pos = row0 + _iota2(shape, 0)
    k_pos = j * SB_BLOCK + _iota2(shape, 1)
    valid = (k_pos < q_pos) & (k_pos >= PAD)
    sp = jnp.maximum(z, 0.0) + jnp.log(1.0 + jnp.exp(-jnp.abs(z)))
    lk = jnp.where(valid, -sp, 0.0)
    return z, sp, valid, lk


SB_DEAD = -110.0


def sb_fwd(src, offs, ride=None):
    lp = src.shape[0]
    nh = N_HEADS
    qb = _pick(lp, SB_QROWS, 8)
    scale = HEAD_DIM ** -0.5
    blk = SB_BLOCK

    n_in = len(ride.ins) if ride else 0
    n_out = len(ride.out_shapes) if ride else 0

    def body(*refs):
        q_ref, k_ref, v_ref = refs[:3]
        o_ref = refs[3 + n_in]
        ride_refs = (refs[3:3 + n_in], refs[4 + n_in:4 + n_in + n_out], refs[4 + n_in + n_out:])
        i = pl.program_id(1)
        if ride:
            @pl.when((pl.program_id(0) == 0) & (i == 0))
            def _():
                ride.start(*ride_refs)
        q_scaled = q_ref[...] * scale
        upper = _iota2((blk, blk), 0) > _iota2((blk, blk), 1)
        n_blocks = ((i + 1) * qb + blk - 1) // blk

        def live(state):
            it, _, c = state
            return (it < n_blocks) & (jnp.max(c) > SB_DEAD)

        def step(state):
            it, acc, c = state
            j = n_blocks - 1 - it
            rows = pl.ds(pl.multiple_of(j * blk, blk), blk)
            z, sp, valid, lk = _sb_scores(q_scaled, k_ref[rows, :], i * qb, j)
            later = _mm_tri(lk, upper) + c
            w = jnp.where(valid, jnp.exp(z - sp + later), 0.0)
            acc = acc + _mm_raw(w, v_ref[rows, :], "nn", "bf16")
            return it + 1, acc, c + jnp.sum(lk, axis=1, keepdims=True)

        _, acc, _ = lax.while_loop(live, step, (jnp.int32(0), jnp.zeros((qb, HEAD_DIM), F32), jnp.zeros((qb, 1), F32)))
        o_ref[...] = acc
        if ride:
            @pl.when((pl.program_id(0) == nh - 1) & (i == lp // qb - 1))
            def _():
                ride.finish(*ride_refs)

    qspec = pl.BlockSpec((qb, HEAD_DIM), lambda h, i: (i, offs[0] + h))
    kspec = pl.BlockSpec((lp, HEAD_DIM), lambda h, i: (0, offs[1] + h))
    vspec = pl.BlockSpec((lp, HEAD_DIM), lambda h, i: (0, offs[2] + h))
    ospec = pl.BlockSpec((qb, HEAD_DIM), lambda h, i: (i, h))
    return pl.pallas_call(
        body, name="sb_fwd", grid=(nh, lp // qb), in_specs=[qspec, kspec, vspec] + [ANY] * n_in, out_specs=[ospec] + [ANY] * n_out,
        out_shape=[jax.ShapeDtypeStruct((lp, nh * HEAD_DIM), F32)] + (ride.out_shapes if ride else []),
        scratch_shapes=ride.sems if ride else [],
        input_output_aliases={3 + k: 1 + k for k in range(n_in)} if ride and ride.alias else {},
        compiler_params=pltpu.CompilerParams(dimension_semantics=("arbitrary", "arbitrary"), vmem_limit_bytes=VMEM_LIMIT),
    )(src, src, src, *(ride.ins if ride else []))


def sb_bwd(src, offs, d_o, ride=None):
    lp = src.shape[0]
    nh = N_HEADS
    qb = _pick(lp, SB_QROWS, 8)
    scale = HEAD_DIM ** -0.5
    blk = SB_BLOCK

    n_in = len(ride.ins) if ride else 0
    n_out = len(ride.out_shapes) if ride else 0

    def body(*refs):
        q_ref, k_ref, v_ref, do_ref = refs[:4]
        dq_ref, dk_out, dv_out = refs[4 + n_in:7 + n_in]
        ride_refs = (refs[4:4 + n_in], refs[7 + n_in:7 + n_in + n_out], refs[9 + n_in + n_out:])
        dk_ref, dv_ref = refs[7 + n_in + n_out:9 + n_in + n_out]
        i = pl.program_id(1)
        if ride:
            @pl.when((pl.program_id(0) == 0) & (i == 0))
            def _():
                ride.start(*ride_refs)

        @pl.when(i == 0)
        def _():
            dk_ref[...] = jnp.zeros_like(dk_ref)
            dv_ref[...] = jnp.zeros_like(dv_ref)

        q_scaled = q_ref[...] * scale
        d_out = do_ref[...]
        lower_incl = _iota2((blk, blk), 0) <= _iota2((blk, blk), 1)
        lower = _iota2((blk, blk), 0) < _iota2((blk, blk), 1)
        n_blocks = ((i + 1) * qb + blk - 1) // blk

        def live(state):
            it, c = state
            return (it < n_blocks) & (jnp.max(c) > SB_DEAD)

        def count(state):
            it, c = state
            rows = pl.ds(pl.multiple_of((n_blocks - 1 - it) * blk, blk), blk)
            _, _, _, lk = _sb_scores(q_scaled, k_ref[rows, :], i * qb, n_blocks - 1 - it)
            return it + 1, c + jnp.sum(lk, axis=1, keepdims=True)

        n_live, total = lax.while_loop(live, count, (jnp.int32(0), jnp.zeros((qb, 1), F32)))

        def step(j, carry):
            acc, cp, ep = carry
            rows = pl.ds(pl.multiple_of(j * blk, blk), blk)
            kb = k_ref[rows, :]
            vb = v_ref[rows, :]
            z, sp, valid, lk = _sb_scores(q_scaled, kb, i * qb, j)
            later = total - cp - _mm_tri(lk, lower_incl)
            w = jnp.where(valid, jnp.exp(z - sp + later), 0.0)
            e = w * _mm_raw(d_out, vb, "nt", "bf16")
            before = ep + _mm_tri(e, lower)
            dz = jnp.where(valid, e * jnp.exp(-sp) - before * jnp.exp(z - sp), 0.0)
            dk_ref[rows, :] += _mm_raw(dz, q_scaled, "tn", "bf16")
            dv_ref[rows, :] += _mm_raw(w, d_out, "tn", "bf16")
            acc = acc + _mm_raw(dz, kb, "nn", "bf16")
            return acc, cp + jnp.sum(lk, axis=1, keepdims=True), ep + jnp.sum(e, axis=1, keepdims=True)

        zero_col = jnp.zeros((qb, 1), F32)
        acc, _, _ = lax.fori_loop(n_blocks - n_live, n_blocks, step, (jnp.zeros((qb, HEAD_DIM), F32), zero_col, zero_col))
        dq_ref[...] = (acc * scale).astype(dq_ref.dtype)

        @pl.when(i == lp // qb - 1)
        def _():
            dk_out[...] = dk_ref[...].astype(dk_out.dtype)
            dv_out[...] = dv_ref[...].astype(dv_out.dtype)

        if ride:
            @pl.when((pl.program_id(0) == nh - 1) & (i == lp // qb - 1))
            def _():
                ride.finish(*ride_refs)

    qspec = pl.BlockSpec((qb, HEAD_DIM), lambda h, i: (i, offs[0] + h))
    kspec = pl.BlockSpec((lp, HEAD_DIM), lambda h, i: (0, offs[1] + h))
    vspec = pl.BlockSpec((lp, HEAD_DIM), lambda h, i: (0, offs[2] + h))
    ospec = pl.BlockSpec((qb, HEAD_DIM), lambda h, i: (i, h))
    fullspec = pl.BlockSpec((lp, HEAD_DIM), lambda h, i: (0, h))
    return pl.pallas_call(
        body, name="sb_bwd", grid=(nh, lp // qb), in_specs=[qspec, kspec, vspec, ospec] + [ANY] * n_in,
        out_specs=[ospec, fullspec, fullspec] + [ANY] * n_out,
        out_shape=[jax.ShapeDtypeStruct((lp, nh * HEAD_DIM), BF16)] * 3 + (ride.out_shapes if ride else []),
        scratch_shapes=[pltpu.VMEM((lp, HEAD_DIM), F32)] * 2 + (ride.sems if ride else []),
        input_output_aliases={4 + k: 3 + k for k in range(n_in)} if ride and ride.alias else {},
        compiler_params=pltpu.CompilerParams(dimension_semantics=("arbitrary", "arbitrary"), vmem_limit_bytes=VMEM_LIMIT),
    )(src, src, src, d_o, *(ride.ins if ride else []))


def _pick(n, target, unit):
    if n <= target:
        return n
    best = None
    for d in range(unit, target + 1, unit):
        if n % d == 0:
            best = d
    assert best is not None, (n, target, unit)
    return best


def matmul(a, b, mode="nn", *, name, bm=1088, bn=640, bk=2176, residual=None, out_dtype=F32, b_koff=0, ride=None):
    if mode == "nn":
        (m, k), n = a.shape, b.shape[1]
    elif mode == "nt":
        (m, k), n = a.shape, b.shape[0]
    else:
        (k, m), n = a.shape, b.shape[1]
    assert b_koff == 0 or mode == "nt"
    bm = _pick(m, bm, 128 if mode == "tn" else 8)
    bn = _pick(n, bn, 128 if mode != "nt" else 8)
    bk = _pick(k, bk, 128 if mode != "tn" else 8)
    nk = k // bk
    n_plain = 2 if residual is None else 3
    n_in = len(ride.ins) if ride else 0
    n_out = len(ride.out_shapes) if ride else 0
    steps = (m // bm, n // bn, nk)

    def body(*refs):
        a_ref, b_ref = refs[:2]
        r_ref = None if residual is None else refs[2]
        o_ref = refs[n_plain + n_in]
        acc = refs[n_plain + n_in + 1 + n_out]
        ride_refs = (refs[n_plain:n_plain + n_in], refs[n_plain + n_in + 1:n_plain + n_in + 1 + n_out], refs[n_plain + n_in + 2 + n_out:])
        kk = pl.program_id(2)
        here = [pl.program_id(d) for d in range(3)]
        if ride:
            @pl.when((here[0] == 0) & (here[1] == 0) & (here[2] == 0))
            def _():
                ride.start(*ride_refs)
        part = _mm_raw(a_ref[...], b_ref[...], mode, "bf16")

        @pl.when(kk == 0)
        def _():
            acc[...] = part

        @pl.when(kk > 0)
        def _():
            acc[...] += part

        @pl.when(kk == nk - 1)
        def _():
            res = acc[...]
            if r_ref is not None:
                res = res + r_ref[...]
            o_ref[...] = res.astype(out_dtype)

        if ride:
            @pl.when((here[0] == steps[0] - 1) & (here[1] == steps[1] - 1) & (here[2] == steps[2] - 1))
            def _():
                ride.finish(*ride_refs)

    a_spec = pl.BlockSpec((bk, bm), lambda i, j, kk: (kk, i)) if mode == "tn" else pl.BlockSpec((bm, bk), lambda i, j, kk: (i, kk))
    b_spec = pl.BlockSpec((bn, bk), lambda i, j, kk: (j, b_koff + kk)) if mode == "nt" else pl.BlockSpec((bk, bn), lambda i, j, kk: (kk, j))
    o_spec = pl.BlockSpec((bm, bn), lambda i, j, kk: (i, j))
    ins, specs = [a, b], [a_spec, b_spec]
    if residual is not None:
        ins.append(residual)
        specs.append(o_spec)
    res = pl.pallas_call(
        body, name=name, grid=steps, in_specs=specs + [ANY] * n_in, out_specs=[o_spec] + [ANY] * n_out,
        out_shape=[jax.ShapeDtypeStruct((m, n), out_dtype)] + (ride.out_shapes if ride else []),
        scratch_shapes=[pltpu.VMEM((bm, bn), F32)] + (ride.sems if ride else []),
        compiler_params=pltpu.CompilerParams(dimension_semantics=("arbitrary", "arbitrary", "arbitrary"), vmem_limit_bytes=VMEM_LIMIT),
    )(*ins, *(ride.ins if ride else []))
    return res if ride else res[0]


def _row_specs(rows, params, width, bm):
    row_specs = [pl.BlockSpec((bm, width), (lambda j, i, off=off: (i, off + j))) for _, off in rows]
    par_specs = [pl.BlockSpec((p.shape[0], width) if per_col else p.shape, ((lambda j, i: (0, j)) if per_col else (lambda j, i: (0, 0))))
                 for p, per_col in params]
    return row_specs, par_specs


def rowmap_fwd(name, fn, rows, params, n_out, *, width, ncol, bm, lp, out_dtypes=None):
    out_dtypes = out_dtypes or [F32] * n_out
    row_specs, par_specs = _row_specs(rows, params, width, bm)
    nr = len(rows)

    def body(*refs):
        ins, outs = refs[:nr + len(params)], refs[nr + len(params):]
        row_ids = pl.program_id(1) * bm + _iota2((bm, 1), 0)
        res = fn(row_ids, *[r[...].astype(F32) for r in ins])
        for o_ref, val in zip(outs, res):
            o_ref[...] = val.astype(o_ref.dtype)

    o_spec = pl.BlockSpec((bm, width), lambda j, i: (i, j))
    return pl.pallas_call(
        body, name=name, grid=(ncol, lp // bm), in_specs=row_specs + par_specs, out_specs=[o_spec] * n_out,
        out_shape=[jax.ShapeDtypeStruct((lp, ncol * width), dt) for dt in out_dtypes],
        compiler_params=pltpu.CompilerParams(dimension_semantics=("arbitrary", "arbitrary"), vmem_limit_bytes=VMEM_LIMIT),
    )(*[a for a, _ in rows], *[p for p, _ in params])


def rowmap_bwd(name, fn, rows, params, d_outs, *, width, ncol, bm, lp, d_row_dtypes=None, add_to_first=None):
    d_row_dtypes = d_row_dtypes or [F32] * len(rows)
    extra = [] if add_to_first is None else [add_to_first]
    row_specs, par_specs = _row_specs(rows, params, width, bm)
    nr, npar, nout = len(rows), len(params), len(d_outs)

    def body(*refs):
        ins = refs[:nr + npar]
        dos = refs[nr + npar:nr + npar + nout]
        more = refs[nr + npar + nout:nr + npar + nout + len(extra)]
        d_rows = refs[nr + npar + nout + len(extra):nr + npar + nout + len(extra) + nr]
        d_pars = refs[nr + npar + nout + len(extra) + nr:]
        j, i = pl.program_id(0), pl.program_id(1)
        row_ids = i * bm + _iota2((bm, 1), 0)
        _, pull = jax.vjp(lambda *xs: tuple(fn(row_ids, *xs)), *[r[...].astype(F32) for r in ins])
        grads = pull(tuple(d[...].astype(F32) for d in dos))
        for k, (ref, val) in enumerate(zip(d_rows, grads[:nr])):
            if k == 0 and more:
                val = val + more[0][...]
            ref[...] = val.astype(ref.dtype)
        for ref, val, (_, per_col) in zip(d_pars, grads[nr:], params):
            first = (i == 0) if per_col else ((i == 0) & (j == 0))

            @pl.when(first)
            def _(ref=ref, val=val):
                ref[...] = val

            @pl.when(jnp.logical_not(first))
            def _(ref=ref, val=val):
                ref[...] += val

    o_spec = pl.BlockSpec((bm, width), lambda j, i: (i, j))
    res = pl.pallas_call(
        body, name=name, grid=(ncol, lp // bm), in_specs=row_specs + par_specs + [o_spec] * (nout + len(extra)),
        out_specs=[o_spec] * nr + par_specs,
        out_shape=[jax.ShapeDtypeStruct((lp, ncol * width), dt) for dt in d_row_dtypes]
        + [jax.ShapeDtypeStruct(p.shape, F32) for p, _ in params],
        compiler_params=pltpu.CompilerParams(dimension_semantics=("arbitrary", "arbitrary"), vmem_limit_bytes=VMEM_LIMIT),
    )(*[a for a, _ in rows], *[p for p, _ in params], *d_outs, *extra)
    return res[:nr], res[nr:]


def _silu(x):
    return x * jax.nn.sigmoid(x)


def _softplus(x):
    return jnp.maximum(x, 0.0) + jnp.log(1.0 + jnp.exp(-jnp.abs(x)))


def _real_rows(row_ids):
    return (row_ids >= PAD).astype(F32)


def _f_rmsnorm(row_ids, h, g):
    return (h * lax.rsqrt(jnp.mean(h * h, axis=-1, keepdims=True) + RMS_EPS) * g,)


def _f_small_gates(row_ids, small, bias, a_log):
    lane = _iota2(small.shape, 1)
    t = small + bias
    sp = _softplus(t)
    coef = -jnp.exp(a_log)
    keep = _real_rows(row_ids)
    first = jnp.where(lane < 8, jax.nn.sigmoid(t), jnp.where(lane < 16, coef * sp, jnp.where(lane < 48, sp, 0.0)))
    second = jnp.where((lane >= 8) & (lane < 48), coef * sp, 0.0)
    return first * keep, second * keep


def _f_gate_silu(row_ids, o, z):
    return (o * _silu(z),)


def _f_head_norm_gate(row_ids, o, z, g):
    out = []
    for h in range(o.shape[1] // HEAD_DIM):
        oh = o[:, h * HEAD_DIM:(h + 1) * HEAD_DIM]
        out.append(oh * lax.rsqrt(jnp.mean(oh * oh, axis=-1, keepdims=True) + RMS_EPS) * g)
    return (jnp.concatenate(out, axis=1) * _silu(z),)


def _f_ssm_out(row_ids, y, xh, z, d_skip, g):
    t = (y + d_skip * xh) * _silu(z)
    return (t * lax.rsqrt(jnp.mean(t * t, axis=-1, keepdims=True) + RMS_EPS) * g,)


def _f_merge(row_ids, pa, pb, pc, ga, gb, gc):
    return (jax.nn.sigmoid(ga) * pa + jax.nn.sigmoid(gb) * pb + jax.nn.sigmoid(gc) * pc,)


def _shift_rows(x, s):
    s = s % x.shape[0]
    return x if s == 0 else pltpu.roll(x, s, 0)


def _conv_pre(x, w, b):
    pre = b
    for kk in range(CONV_K):
        pre = pre + w[kk:kk + 1, :] * _shift_rows(x, CONV_K - 1 - kk)
    return pre


def _conv_post(pre, l2_flag, keep):
    act = _silu(pre)
    nrm = act * lax.rsqrt(jnp.sum(act * act, axis=-1, keepdims=True) + L2_EPS)
    return (l2_flag * nrm + (1.0 - l2_flag) * act) * keep


def conv_fwd(name, src, col_off, w, b, n_l2):
    lp, ch = src.shape[0], w.shape[1]

    def body(x_ref, w_ref, b_ref, o_ref):
        l2_flag = (pl.program_id(0) < n_l2).astype(F32)
        keep = _real_rows(_iota2((lp, 1), 0))
        o_ref[...] = _conv_post(_conv_pre(x_ref[...], w_ref[...], b_ref[...]), l2_flag, keep)

    return pl.pallas_call(
        body, name=name, grid=(ch // HEAD_DIM,),
        in_specs=[pl.BlockSpec((lp, HEAD_DIM), lambda j: (0, col_off + j)), pl.BlockSpec((CONV_K, HEAD_DIM), lambda j: (0, j)),
                  pl.BlockSpec((1, HEAD_DIM), lambda j: (0, j))],
        out_specs=pl.BlockSpec((lp, HEAD_DIM), lambda j: (0, j)),
        out_shape=jax.ShapeDtypeStruct((lp, ch), F32),
        compiler_params=pltpu.CompilerParams(dimension_semantics=("arbitrary",), vmem_limit_bytes=VMEM_LIMIT),
    )(src, w, b)


def conv_bwd(name, src, col_off, w, b, n_l2, d_out):
    lp, ch = src.shape[0], w.shape[1]

    def body(x_ref, w_ref, b_ref, do_ref, dx_ref, dw_ref, db_ref):
        l2_flag = (pl.program_id(0) < n_l2).astype(F32)
        keep = _real_rows(_iota2((lp, 1), 0))
        x, wv = x_ref[...], w_ref[...]
        pre = _conv_pre(x, wv, b_ref[...])
        _, pull = jax.vjp(lambda p: _conv_post(p, l2_flag, keep), pre)
        (d_pre,) = pull(do_ref[...])
        dx = jnp.zeros_like(x)
        for kk in range(CONV_K):
            s = CONV_K - 1 - kk
            dx = dx + wv[kk:kk + 1, :] * _shift_rows(d_pre, -s)
            dw_ref[kk:kk + 1, :] = jnp.sum(d_pre * _shift_rows(x, s), axis=0, keepdims=True)
        dx_ref[...] = (dx * keep).astype(dx_ref.dtype)
        db_ref[...] = jnp.sum(d_pre, axis=0, keepdims=True)

    seq = pl.BlockSpec((lp, HEAD_DIM), lambda j: (0, j))
    wspec = pl.BlockSpec((CONV_K, HEAD_DIM), lambda j: (0, j))
    bspec = pl.BlockSpec((1, HEAD_DIM), lambda j: (0, j))
    return pl.pallas_call(
        body, name=name, grid=(ch // HEAD_DIM,),
        in_specs=[pl.BlockSpec((lp, HEAD_DIM), lambda j: (0, col_off + j)), wspec, bspec, seq],
        out_specs=[seq, wspec, bspec],
        out_shape=[jax.ShapeDtypeStruct((lp, ch), BF16), jax.ShapeDtypeStruct(w.shape, F32), jax.ShapeDtypeStruct(b.shape, F32)],
        compiler_params=pltpu.CompilerParams(dimension_semantics=("arbitrary",), vmem_limit_bytes=VMEM_LIMIT),
    )(src, w, b, d_out)


def loss_head(h, target, g):
    lp, d = h.shape
    bm = SB_BLOCK
    first = (PAD + N_META) // bm

    def body(h_ref, t_ref, g_ref, loss_ref, dh_ref, dg_ref):
        i = pl.program_id(0)
        keep = (i >= first).astype(F32)

        def f(hv, gv):
            y = hv * lax.rsqrt(jnp.mean(hv * hv, axis=-1, keepdims=True) + RMS_EPS) * gv
            err = y - t_ref[...]
            return 0.5 * jnp.sum(jnp.mean(err * err, axis=-1, keepdims=True), axis=0, keepdims=True) * keep

        val, pull = jax.vjp(f, h_ref[...], g_ref[...])
        dh, dg = pull(jnp.ones((1, 1), F32))
        dh_ref[...] = dh

        @pl.when(i == 0)
        def _():
            loss_ref[...] = val
            dg_ref[...] = dg

        @pl.when(i > 0)
        def _():
            loss_ref[...] += val
            dg_ref[...] += dg

    row = pl.BlockSpec((bm, d), lambda i: (i, 0))
    return pl.pallas_call(
        body, name="loss_head", grid=(lp // bm,),
        in_specs=[row, pl.BlockSpec((bm, d), lambda i: (jnp.maximum(i - first, 0), 0)), pl.BlockSpec((1, d), lambda i: (0, 0))],
        out_specs=[pl.BlockSpec((1, 1), lambda i: (0, 0)), row, pl.BlockSpec((1, d), lambda i: (0, 0))],
        out_shape=[jax.ShapeDtypeStruct((1, 1), F32), jax.ShapeDtypeStruct((lp, d), F32), jax.ShapeDtypeStruct((1, d), F32)],
        compiler_params=pltpu.CompilerParams(dimension_semantics=("arbitrary",)),
    )(h, target, g)


ADAM_LR, ADAM_B1, ADAM_B2, ADAM_EPS, ADAM_WD, ADAM_STEP = 0.001, 0.9, 0.999, 1e-08, 0.01, 10


def adamw(name, w, g, m, v, echo_g=False):
    lead = w.shape[:-2]
    rows, cols = w.shape[-2:]
    br = _pick(rows, 128, 8)
    n_out = 4 if echo_g else 3

    def body(w_ref, g_ref, m_ref, v_ref, d_ref, nm_ref, nv_ref, *echo):
        gv = g_ref[...]
        nm = ADAM_B1 * m_ref[...] + (1.0 - ADAM_B1) * gv
        nv = ADAM_B2 * v_ref[...] + (1.0 - ADAM_B2) * (gv * gv)
        m_hat = nm / (1.0 - ADAM_B1 ** ADAM_STEP)
        v_hat = nv / (1.0 - ADAM_B2 ** ADAM_STEP)
        d_ref[...] = -ADAM_LR * (m_hat / (jnp.sqrt(v_hat) + ADAM_EPS) + ADAM_WD * w_ref[...])
        nm_ref[...] = nm
        nv_ref[...] = nv
        for e_ref in echo:
            e_ref[...] = gv

    if lead and rows <= 8:
        bl = _pick(lead[0], 32, 1)
        spec = pl.BlockSpec((bl, rows, cols), lambda s: (s, 0, 0))
        grid = (lead[0] // bl,)
    elif lead:
        spec = pl.BlockSpec((None, br, cols), lambda s, i: (s, i, 0))
        grid = (lead[0], rows // br)
    else:
        spec = pl.BlockSpec((br, cols), lambda i: (i, 0))
        grid = (rows // br,)
    return pl.pallas_call(
        body, name=name, grid=grid, in_specs=[spec] * 4, out_specs=[spec] * n_out,
        out_shape=[jax.ShapeDtypeStruct(w.shape, F32)] * n_out,
        compiler_params=pltpu.CompilerParams(dimension_semantics=("arbitrary",) * len(grid), vmem_limit_bytes=VMEM_LIMIT),
    )(w, g, m, v)


MESH = pl.DeviceIdType.MESH
ANY = pl.BlockSpec(memory_space=pl.ANY)
D2D_PIECES = 16
ICI_PIECES = 4


def _place():
    x, y, c = lax.axis_index("x"), lax.axis_index("y"), lax.axis_index("c")
    return x, y, c, [(1 - x, y), (x, 1 - y), (1 - x, 1 - y)]


def _pieces(rows, n, unit):
    per = -(-rows // (n * unit)) * unit
    return [(s, min(per, rows - s)) for s in range(0, rows, per)]


def _row_unit(dtype):
    return 16 if dtype == BF16 else 8


def _scalar(v):
    return jnp.reshape(v, (1,)).astype(jnp.int32)


def place_shard(name, pack):
    rows, cols = pack.shape
    br = _pick(rows, 256, 16)

    def body(m_ref, p_ref, o_ref):
        o_ref[...] = p_ref[...]

    return pl.pallas_call(
        body, name=name,
        grid_spec=pltpu.PrefetchScalarGridSpec(
            num_scalar_prefetch=1, grid=(rows // br,),
            in_specs=[pl.BlockSpec((br, cols), lambda i, m: (i, 0))],
            out_specs=pl.BlockSpec((None, br, cols), lambda i, m: (m[0], i, 0))),
        out_shape=jax.ShapeDtypeStruct((4, rows, cols), pack.dtype),
        compiler_params=pltpu.CompilerParams(dimension_semantics=("arbitrary",), vmem_limit_bytes=VMEM_LIMIT),
    )(_scalar(2 * lax.axis_index("x") + lax.axis_index("y")), pack)


class Ride:
    def __init__(self, ins, out_shapes, alias, sems, start, finish):
        self.ins, self.out_shapes, self.alias, self.sems, self.start, self.finish = list(ins), out_shapes, alias, sems, start, finish


def _gather_parts(o_refs, send_sems, recv_sems):
    x, y, c, chips = _place()
    mine = 2 * x + y

    def half_rows(b, which, start=0, size=None):
        half = o_refs[b].shape[1] // 2
        return pl.ds(pl.multiple_of(which * half + start, _row_unit(o_refs[b].dtype)), half if size is None else size)

    def remote(b, k, slot, rws, to):
        piece = o_refs[b].at[slot, rws, :]
        return pltpu.make_async_remote_copy(src_ref=piece, dst_ref=piece, send_sem=send_sems.at[b, k], recv_sem=recv_sems.at[b, k],
                                            device_id=to, device_id_type=MESH)

    return x, y, c, chips, mine, half_rows, remote


def _gather_start(o_refs, send_sems, recv_sems):
    x, y, c, chips, mine, half_rows, remote = _gather_parts(o_refs, send_sems, recv_sems)
    for b, o_ref in enumerate(o_refs):
        for j, (cx, cy) in enumerate(chips):
            for start, size in _pieces(o_ref.shape[1] // 2, ICI_PIECES, _row_unit(o_ref.dtype)):
                remote(b, j, mine, half_rows(b, c, start, size), (cx, cy, c)).start()


def _gather_finish(o_refs, send_sems, recv_sems):
    x, y, c, chips, mine, half_rows, remote = _gather_parts(o_refs, send_sems, recv_sems)
    sends = []
    for b, o_ref in enumerate(o_refs):
        for j, (cx, cy) in enumerate(chips):
            slot = 2 * cx + cy
            sends.append(remote(b, j, mine, half_rows(b, c), (cx, cy, c)))
            remote(b, j, slot, half_rows(b, c), (cx, cy, c)).wait_recv()
            for start, size in _pieces(o_ref.shape[1] // 2, D2D_PIECES, _row_unit(o_ref.dtype)):
                remote(b, 3 + j, slot, half_rows(b, c, start, size), (x, y, 1 - c)).start()
            sends.append(remote(b, 3 + j, slot, half_rows(b, c), (x, y, 1 - c)))
    for b in range(len(o_refs)):
        for j, (cx, cy) in enumerate(chips):
            remote(b, 3 + j, 2 * cx + cy, half_rows(b, 1 - c), (x, y, 1 - c)).wait_recv()
    for cp in sends:
        cp.wait_send()


def gather_ride(placed):
    n = len(placed)
    return Ride(placed, [jax.ShapeDtypeStruct(p.shape, p.dtype) for p in placed], True,
                [pltpu.SemaphoreType.DMA((n, 6)), pltpu.SemaphoreType.DMA((n, 6))],
                lambda ins, outs, sems: _gather_start(outs, *sems), lambda ins, outs, sems: _gather_finish(outs, *sems))


def gather_shards(name, placed):
    n = len(placed)

    def body(*refs):
        o_refs, sems = refs[n:2 * n], refs[2 * n:]
        _gather_start(o_refs, *sems)
        _gather_finish(o_refs, *sems)

    return pl.pallas_call(
        body, name=name, in_specs=[ANY] * n, out_specs=[ANY] * n,
        out_shape=[jax.ShapeDtypeStruct(p.shape, p.dtype) for p in placed],
        input_output_aliases={i: i for i in range(n)},
        scratch_shapes=[pltpu.SemaphoreType.DMA((n, 6)), pltpu.SemaphoreType.DMA((n, 6))],
    )(*placed)


def _split_copies(g_refs, t_refs, send_sems, recv_sems, start):
    x, y, c, _ = _place()
    waits = []
    for b, (g_ref, t_ref) in enumerate(zip(g_refs, t_refs)):
        half = g_ref.shape[2]

        def copy(slots, first, size):
            rws = pl.ds(first, size)
            return pltpu.make_async_remote_copy(src_ref=g_ref.at[slots, 1 - c, rws, :], dst_ref=t_ref.at[slots, rws, :],
                                                send_sem=send_sems.at[b], recv_sem=recv_sems.at[b], device_id=(x, y, 1 - c),
                                                device_id_type=MESH)

        if start:
            for s in range(4):
                for first, size in _pieces(half, D2D_PIECES // 4, _row_unit(g_ref.dtype)):
                    copy(s, first, size).start()
        else:
            waits.append(copy(slice(None), 0, half))
    return waits


def _split_finish(g_refs, t_refs, send_sems, recv_sems):
    for cp in _split_copies(g_refs, t_refs, send_sems, recv_sems, False):
        cp.wait()


def _split_shapes(bufs):
    return [jax.ShapeDtypeStruct((4,) + g.shape[2:], g.dtype) for g in bufs]


def split_ride(bufs):
    n = len(bufs)
    return Ride(bufs, _split_shapes(bufs), False, [pltpu.SemaphoreType.DMA((n,)), pltpu.SemaphoreType.DMA((n,))],
                lambda ins, outs, sems: _split_copies(ins, outs, *sems, True), lambda ins, outs, sems: _split_finish(ins, outs, *sems))


def pair_split(name, bufs):
    n = len(bufs)

    def body(*refs):
        g_refs, t_refs, sems = refs[:n], refs[n:2 * n], refs[2 * n:]
        _split_copies(g_refs, t_refs, *sems, True)
        _split_finish(g_refs, t_refs, *sems)

    return pl.pallas_call(
        body, name=name, in_specs=[ANY] * n, out_specs=[ANY] * n, out_shape=_split_shapes(bufs),
        scratch_shapes=[pltpu.SemaphoreType.DMA((n,)), pltpu.SemaphoreType.DMA((n,))],
    )(*bufs)


def pair_add(name, g, theirs, transit):
    _, _, half, cols = g.shape
    br = _pick(half, 128, 16)

    def body(c_ref, g_ref, t_ref, o_ref):
        o_ref[...] = (g_ref[...].astype(F32) + t_ref[...].astype(F32)).astype(transit)

    blk = (4, br, cols)
    return pl.pallas_call(
        body, name=name,
        grid_spec=pltpu.PrefetchScalarGridSpec(
            num_scalar_prefetch=1, grid=(half // br,),
            in_specs=[pl.BlockSpec((4, None, br, cols), lambda i, c: (0, c[0], i, 0)), pl.BlockSpec(blk, lambda i, c: (0, i, 0))],
            out_specs=pl.BlockSpec(blk, lambda i, c: (0, i, 0))),
        out_shape=jax.ShapeDtypeStruct((4, half, cols), transit),
        compiler_params=pltpu.CompilerParams(dimension_semantics=("arbitrary",), vmem_limit_bytes=VMEM_LIMIT),
    )(_scalar(lax.axis_index("c")), g, theirs)


def _exchange_copies(a_refs, o_refs, send_sems, recv_sems, start):
    x, y, c, chips = _place()
    mine = 2 * x + y
    waits = []
    for b, (a_ref, o_ref) in enumerate(zip(a_refs, o_refs)):
        rows = a_ref.shape[1]
        for j, (cx, cy) in enumerate(chips):
            def copy(first, size):
                rws = pl.ds(first, size)
                return pltpu.make_async_remote_copy(src_ref=a_ref.at[2 * cx + cy, rws, :], dst_ref=o_ref.at[mine, rws, :],
                                                    send_sem=send_sems.at[b, j], recv_sem=recv_sems.at[b, j],
                                                    device_id=(cx, cy, c), device_id_type=MESH)
            if start:
                for first, size in _pieces(rows, ICI_PIECES, _row_unit(a_ref.dtype)):
                    copy(first, size).start()
            else:
                waits.append(copy(0, rows))
    return waits


def _exchange_finish(a_refs, o_refs, send_sems, recv_sems):
    for cp in _exchange_copies(a_refs, o_refs, send_sems, recv_sems, False):
        cp.wait()


def exchange_ride(parts):
    n = len(parts)
    return Ride(parts, [jax.ShapeDtypeStruct(a.shape, a.dtype) for a in parts], False,
                [pltpu.SemaphoreType.DMA((n, 3)), pltpu.SemaphoreType.DMA((n, 3))],
                lambda ins, outs, sems: _exchange_copies(ins, outs, *sems, True), lambda ins, outs, sems: _exchange_finish(ins, outs, *sems))


def chip_exchange(name, parts):
    n = len(parts)

    def body(*refs):
        a_refs, o_refs, sems = refs[:n], refs[n:2 * n], refs[2 * n:]
        _exchange_copies(a_refs, o_refs, *sems, True)
        _exchange_finish(a_refs, o_refs, *sems)

    return pl.pallas_call(
        body, name=name, in_specs=[ANY] * n, out_specs=[ANY] * n,
        out_shape=[jax.ShapeDtypeStruct(a.shape, a.dtype) for a in parts],
        scratch_shapes=[pltpu.SemaphoreType.DMA((n, 3)), pltpu.SemaphoreType.DMA((n, 3))],
    )(*parts)


def chip_add(name, got, part):
    _, rows, cols = got.shape
    br = _pick(rows, 128, 16)
    nblk = rows // br

    def body(m_ref, c_ref, got_ref, part_ref, o_ref):
        mine = m_ref[0]
        for s in range(4):
            @pl.when(mine == s)
            def _(s=s):
                val = part_ref[...].astype(F32)
                o_ref[...] = val if s == 0 else o_ref[...] + val

            @pl.when(mine != s)
            def _(s=s):
                val = got_ref[s].astype(F32)
                o_ref[...] = val if s == 0 else o_ref[...] + val

    return pl.pallas_call(
        body, name=name,
        grid_spec=pltpu.PrefetchScalarGridSpec(
            num_scalar_prefetch=2, grid=(nblk,),
            in_specs=[pl.BlockSpec((4, br, cols), lambda i, m, c: (0, i, 0)),
                      pl.BlockSpec((None, br, cols), lambda i, m, c: (m[0], i, 0))],
            out_specs=pl.BlockSpec((br, cols), lambda i, m, c: (c[0] * nblk + i, 0))),
        out_shape=jax.ShapeDtypeStruct((2 * rows, cols), F32),
        compiler_params=pltpu.CompilerParams(dimension_semantics=("arbitrary",), vmem_limit_bytes=VMEM_LIMIT),
    )(_scalar(2 * lax.axis_index("x") + lax.axis_index("y")), _scalar(lax.axis_index("c")), got, part)


def pair_join(name, fulls):
    n = len(fulls)

    def body(*refs):
        o_refs = refs[n:2 * n]
        send_sems, recv_sems = refs[2 * n:]
        x, y, c, _ = _place()
        waits = []
        for b, o_ref in enumerate(o_refs):
            half = o_ref.shape[0] // 2
            unit = _row_unit(o_ref.dtype)

            def copy(start, size):
                piece = o_ref.at[pl.ds(pl.multiple_of(c * half + start, unit), size), :]
                return pltpu.make_async_remote_copy(src_ref=piece, dst_ref=piece, send_sem=send_sems.at[b], recv_sem=recv_sems.at[b],
                                                    device_id=(x, y, 1 - c), device_id_type=MESH)

            for start, size in _pieces(half, D2D_PIECES, unit):
                copy(start, size).start()
            waits.append(copy(0, half))
        for cp in waits:
            cp.wait()

    return pl.pallas_call(
        body, name=name, in_specs=[ANY] * n, out_specs=[ANY] * n,
        out_shape=[jax.ShapeDtypeStruct(f.shape, f.dtype) for f in fulls],
        input_output_aliases={i: i for i in range(n)},
        scratch_shapes=[pltpu.SemaphoreType.DMA((n,)), pltpu.SemaphoreType.DMA((n,))],
    )(*fulls)


D_IN = 15920
D_PROJ = 16000
_SEGMENTS = ((0, 8192), (8208, 12816), (12848, 15920), (8192, 8208), (12816, 12848))
OFF_SB_Z, OFF_GDN_QKV, OFF_GDN_Z, OFF_SSM_Z, OFF_SSM_XBC, OFF_GATES, OFF_SMALL = 3072, 4096, 7168, 8192, 10240, 12800, 15872
PACK_C = 1024
WEIGHTS = ("meta_tokens", "norm_g", "w_in", "gdn_conv_w", "gdn_a_log", "gdn_dt_bias", "gdn_norm_g", "ssm_conv_w", "ssm_conv_b",
           "ssm_a_log", "ssm_dt_bias", "ssm_d", "ssm_norm_g", "w_branch_a", "w_branch_b", "w_branch_c", "w_out", "final_norm_g")
SHARDED = ("w_in", "w_branch_a", "w_branch_b", "w_branch_c", "w_out", "gdn_conv_w", "ssm_conv_w", "meta_tokens")
SHARD_AXIS = {"w_in": 2, "w_branch_a": 1, "w_branch_b": 1, "w_branch_c": 1, "w_out": 1, "gdn_conv_w": 2, "ssm_conv_w": 2, "meta_tokens": 1}
BRANCH = ("w_branch_a", "w_branch_b", "w_branch_c", "w_out")
EXACT = ("gdn_conv_w", "ssm_conv_w", "meta_tokens")
REPLICATED = tuple(n for n in WEIGHTS if n not in SHARDED)


def _regrouped_from_shards(shard_cols):
    out = []
    for a, b in _SEGMENTS:
        while a < b:
            chip = a // shard_cols
            stop = min(b, (chip + 1) * shard_cols)
            out.append((chip, a - chip * shard_cols, stop - chip * shard_cols))
            a = stop
    return out


def _shard_from_regrouped(chip, shard_cols):
    lo, hi = chip * shard_cols, (chip + 1) * shard_cols
    out, pos = [], 0
    starts = {}
    for a, b in _SEGMENTS:
        starts[(a, b)] = pos
        pos += b - a
    for a, b in sorted(_SEGMENTS):
        s0, s1 = max(a, lo), min(b, hi)
        if s0 < s1:
            out.append((starts[(a, b)] + s0 - a, starts[(a, b)] + s1 - a))
    return out


def _pack(parts, row_unit=64):
    n = sum(p.shape[0] for p in parts)
    rows = -(-n // (PACK_C * row_unit)) * row_unit
    flat = jnp.concatenate(list(parts) + [jnp.zeros((rows * PACK_C - n,), parts[0].dtype)])
    return flat.reshape(rows, PACK_C)


def _unpack(buf, shapes):
    flat, out, pos = buf.reshape(-1), [], 0
    for shp in shapes:
        n = math.prod(shp)
        out.append(flat[pos:pos + n].reshape(shp))
        pos += n
    return out


def _as_bf16_words(a):
    return lax.bitcast_convert_type(a, BF16).reshape(-1)


BRANCH_ROWS = (D_MODEL // 4, D_MODEL // 4, SSM_INNER // 4, D_MODEL // 4)


def _place_weights(w):
    depth = w["w_in"].shape[0]
    layers = []
    for l in range(depth):
        a = w["w_in"][l].astype(BF16)
        b = jnp.concatenate([w[n][l] for n in BRANCH], axis=0).astype(BF16)
        layers.append([place_shard("place_w_in", a), place_shard("place_branch", b)])
    small = place_shard("place_exact", _pack([_as_bf16_words(w[n]) for n in EXACT]))
    return layers, small


def _exact_weights(w, got_s):
    per_chip = [_unpack(got_s[c], [w[n].shape + (2,) for n in EXACT]) for c in range(4)]
    return {n: jnp.concatenate([lax.bitcast_convert_type(per_chip[c][i], F32) for c in range(4)], axis=SHARD_AXIS[n])
            for i, n in enumerate(EXACT)}


def _proj_weight(got_a):
    d_model, shard_cols = got_a.shape[1:]
    pad = jnp.zeros((d_model, D_PROJ - D_IN), BF16)
    return jnp.concatenate([got_a[c, :, lo:hi] for c, lo, hi in _regrouped_from_shards(shard_cols)] + [pad], axis=1)


def _branch_weights(got_b):
    out, pos = {}, 0
    for n, rows in zip(("wa", "wb", "wc", "wo"), BRANCH_ROWS):
        out[n] = jnp.concatenate([got_b[c, pos:pos + rows] for c in range(4)], axis=0)
        pos += rows
    return out


def _shard(a, axis, s):
    size = a.shape[axis] // 4
    return lax.slice_in_dim(a, s * size, (s + 1) * size, axis=axis)


def _layer_grad_buffers(g, shard_cols):
    buf_a = jnp.stack([jnp.concatenate([g["w_in"][:, lo:hi] for lo, hi in _shard_from_regrouped(s, shard_cols)], axis=1)
                       for s in range(4)]).astype(BF16)
    buf_b = jnp.stack([jnp.concatenate([_shard(g[n], 0, s) for n in BRANCH], axis=0) for s in range(4)]).astype(BF16)
    return [buf_a, buf_b]


def _split_halves(buf):
    return buf.reshape(4, 2, buf.shape[1] // 2, buf.shape[2])


def _start_reduce(tag, bufs, transits):
    bufs = [_split_halves(g) for g in bufs]
    theirs = pair_split(tag + "_pair_split", bufs)
    return [pair_add(f"{tag}_pair_add_{i}", g, t, tr) for i, (g, t, tr) in enumerate(zip(bufs, theirs, transits))]


def _layer_params(w, exact, wp, l):
    lane = lambda v, lo: jnp.pad(v, (lo, HEAD_DIM - lo - v.shape[0]))[None]
    return dict(
        norm_g=w["norm_g"][l][None], wp=wp,
        gdn_conv_w=exact["gdn_conv_w"][l], gdn_conv_b=jnp.zeros((1, 3 * N_HEADS * HEAD_DIM), F32),
        ssm_conv_w=exact["ssm_conv_w"][l], ssm_conv_b=w["ssm_conv_b"][l][None],
        bias_vec=lane(w["gdn_dt_bias"][l], 8) + lane(w["ssm_dt_bias"][l], 16),
        alog_vec=lane(w["gdn_a_log"][l], 8) + lane(w["ssm_a_log"][l], 16),
        gdn_norm_g=w["gdn_norm_g"][l][None], d_skip=jnp.repeat(w["ssm_d"][l], SSM_P)[None], ssm_norm_g=w["ssm_norm_g"][l][None])


def _layer_fwd(h, p, ride):
    lp = h.shape[0]
    bm = _pick(lp, 544, 16)
    kw = dict(bm=bm, lp=lp)
    (u,) = rowmap_fwd("rms_fwd", _f_rmsnorm, [(h, 0)], [(p["norm_g"], False)], 1, width=D_MODEL, ncol=1, out_dtypes=[BF16], **kw)
    proj = matmul(u, p["wp"], "nn", name="proj", bm=lp, bn=640)
    o_a_raw, *rode = sb_fwd(proj, (0, N_HEADS, 2 * N_HEADS), ride)
    p = dict(p, **_branch_weights(rode[0]))
    qkv = conv_fwd("gdn_conv_fwd", proj, OFF_GDN_QKV // HEAD_DIM, p["gdn_conv_w"], p["gdn_conv_b"], 2 * N_HEADS)
    first, second = rowmap_fwd("gates_fwd", _f_small_gates, [(proj, OFF_SMALL // HEAD_DIM)],
                               [(p["bias_vec"], False), (p["alog_vec"], False)], 2, width=HEAD_DIM, ncol=1, **kw)
    o_b_raw, gdn_states, gdn_t = gdn_fwd(qkv, first)
    xbc = conv_fwd("ssm_conv_fwd", proj, OFF_SSM_XBC // HEAD_DIM, p["ssm_conv_w"], p["ssm_conv_b"], 0)
    y_raw, ssd_states = ssd_fwd(xbc, first, second)
    (o_a,) = rowmap_fwd("gate_a_fwd", _f_gate_silu, [(o_a_raw, 0), (proj, OFF_SB_Z // 1024)], [], 1, width=1024, ncol=1,
                        out_dtypes=[BF16], **kw)
    (o_b,) = rowmap_fwd("gate_b_fwd", _f_head_norm_gate, [(o_b_raw, 0), (proj, OFF_GDN_Z // 1024)], [(p["gdn_norm_g"], False)], 1,
                        width=1024, ncol=1, out_dtypes=[BF16], **kw)
    (o_c,) = rowmap_fwd("gate_c_fwd", _f_ssm_out, [(y_raw, 0), (xbc, 0), (proj, OFF_SSM_Z // 1024)],
                        [(p["d_skip"], True), (p["ssm_norm_g"], True)], 1, width=1024, ncol=SSM_GROUPS, out_dtypes=[BF16], **kw)
    pa = matmul(o_a, p["wa"], "nn", name="branch_a", bm=lp // 2, bn=512)
    pb = matmul(o_b, p["wb"], "nn", name="branch_b", bm=lp // 2, bn=512)
    pc = matmul(o_c, p["wc"], "nn", name="branch_c", bm=lp // 2, bn=512)
    merge_rows = [(pa, 0), (pb, 0), (pc, 0)] + [(proj, OFF_GATES // 512 + 2 * i) for i in range(3)]
    (merged,) = rowmap_fwd("merge_fwd", _f_merge, merge_rows, [], 1, width=512, ncol=2, out_dtypes=[BF16], **kw)
    h_out = matmul(merged, p["wo"], "nn", name="out_proj", bm=lp, bn=512, residual=h)
    saved = dict(h=h, u=u, proj=proj, qkv=qkv, first=first, second=second, o_a_raw=o_a_raw, o_b_raw=o_b_raw,
                 gdn_states=gdn_states, gdn_t=gdn_t, xbc=xbc, y_raw=y_raw, ssd_states=ssd_states, o_a=o_a, o_b=o_b, o_c=o_c, pa=pa, pb=pb, pc=pc,
                 merged=merged, params=p)
    return h_out, saved, rode[1:]


def _layer_bwd(d_h, p, s, ride):
    lp = d_h.shape[0]
    bm = _pick(lp, 544, 16)
    kw = dict(bm=bm, lp=lp)
    proj = s["proj"]
    g = {}
    d_merged = matmul(d_h, p["wo"], "nt", name="d_merged", bm=lp, bn=512)
    g["w_out"] = matmul(s["merged"], d_h, "tn", name="g_w_out", bm=512, bn=1024, bk=lp)
    merge_rows = [(s["pa"], 0), (s["pb"], 0), (s["pc"], 0)] + [(proj, OFF_GATES // 512 + 2 * i) for i in range(3)]
    (d_pa, d_pb, d_pc, d_ga, d_gb, d_gc), _ = rowmap_bwd("merge_bwd", _f_merge, merge_rows, [], [d_merged], width=512, ncol=2,
                                                         d_row_dtypes=[BF16] * 6, **kw)
    g["w_branch_a"] = matmul(s["o_a"], d_pa, "tn", name="g_w_a", bm=512, bn=1024, bk=lp)
    g["w_branch_b"] = matmul(s["o_b"], d_pb, "tn", name="g_w_b", bm=512, bn=1024, bk=lp)
    g["w_branch_c"] = matmul(s["o_c"], d_pc, "tn", name="g_w_c", bm=512, bn=1024, bk=lp)
    d_oa = matmul(d_pa, p["wa"], "nt", name="d_o_a", bm=lp, bn=512)
    d_ob = matmul(d_pb, p["wb"], "nt", name="d_o_b", bm=lp, bn=512)
    d_oc = matmul(d_pc, p["wc"], "nt", name="d_o_c", bm=lp, bn=512)
    (d_oa_raw, d_sbz), _ = rowmap_bwd("gate_a_bwd", _f_gate_silu, [(s["o_a_raw"], 0), (proj, OFF_SB_Z // 1024)], [], [d_oa],
                                      width=1024, ncol=1, d_row_dtypes=[F32, BF16], **kw)
    (d_ob_raw, d_gdz), (g["gdn_norm_g"],) = rowmap_bwd(
        "gate_b_bwd", _f_head_norm_gate, [(s["o_b_raw"], 0), (proj, OFF_GDN_Z // 1024)], [(p["gdn_norm_g"], False)], [d_ob],
        width=1024, ncol=1, d_row_dtypes=[F32, BF16], **kw)
    (d_y, d_xh, d_ssz), (g_dskip, g["ssm_norm_g"]) = rowmap_bwd(
        "gate_c_bwd", _f_ssm_out, [(s["y_raw"], 0), (s["xbc"], 0), (proj, OFF_SSM_Z // 1024)],
        [(p["d_skip"], True), (p["ssm_norm_g"], True)], [d_oc], width=1024, ncol=SSM_GROUPS, d_row_dtypes=[F32, F32, BF16], **kw)
    g["gdn_norm_g"], g["ssm_norm_g"] = g["gdn_norm_g"][0], g["ssm_norm_g"][0]
    g["ssm_d"] = g_dskip.reshape(SSM_HEADS, SSM_P).sum(axis=1)
    d_q, d_k, d_v, *rode = sb_bwd(proj, (0, N_HEADS, 2 * N_HEADS), d_oa_raw, ride)
    d_qkv, d_first_gdn = gdn_bwd(s["qkv"], s["first"], s["gdn_states"], s["gdn_t"], d_ob_raw)
    d_xbc_out, d_first_ssd, d_second = ssd_bwd(s["xbc"], s["first"], s["second"], s["ssd_states"], d_y, d_xh)
    d_gdqkv, g["gdn_conv_w"], _ = conv_bwd("gdn_conv_bwd", proj, OFF_GDN_QKV // HEAD_DIM, p["gdn_conv_w"], p["gdn_conv_b"], 2 * N_HEADS,
                                           d_qkv)
    d_xbc, g["ssm_conv_w"], g_cb = conv_bwd("ssm_conv_bwd", proj, OFF_SSM_XBC // HEAD_DIM, p["ssm_conv_w"], p["ssm_conv_b"], 0, d_xbc_out)
    g["ssm_conv_b"] = g_cb[0]
    d_first = d_first_gdn + d_first_ssd
    (d_small,), (g_bias, g_alog) = rowmap_bwd("gates_bwd", _f_small_gates, [(proj, OFF_SMALL // HEAD_DIM)],
                                              [(p["bias_vec"], False), (p["alog_vec"], False)], [d_first, d_second],
                                              width=HEAD_DIM, ncol=1, d_row_dtypes=[BF16], **kw)
    g["gdn_dt_bias"], g["ssm_dt_bias"] = g_bias[0, 8:16], g_bias[0, 16:48]
    g["gdn_a_log"], g["ssm_a_log"] = g_alog[0, 8:16], g_alog[0, 16:48]
    d_proj = jnp.concatenate([d_q, d_k, d_v, d_sbz, d_gdqkv, d_gdz, d_ssz, d_xbc, d_ga, d_gb, d_gc, d_small], axis=1)
    g["w_in"] = matmul(s["u"], d_proj, "tn", name="g_w_in", bm=1024, bn=640, bk=lp, out_dtype=BF16)
    bufs = [_split_halves(buf) for buf in _layer_grad_buffers(g, D_IN // 4)]
    d_u, *theirs = matmul(d_proj, p["wp"], "nt", name="d_u", bm=lp, bn=1024, bk=1600, ride=split_ride(bufs))
    parts = [pair_add(f"grads_pair_add_{i}", buf, t, BF16) for i, (buf, t) in enumerate(zip(bufs, theirs))]
    (d_h_in,), (g_norm,) = rowmap_bwd("rms_bwd", _f_rmsnorm, [(s["h"], 0)], [(p["norm_g"], False)], [d_u], width=D_MODEL, ncol=1,
                                      add_to_first=d_h, **kw)
    g["norm_g"] = g_norm[0]
    return d_h_in, g, rode, parts


def kernel(x, meta_tokens, norm_g, w_in, gdn_conv_w, gdn_a_log, gdn_dt_bias, gdn_norm_g, ssm_conv_w, ssm_conv_b, ssm_a_log, ssm_dt_bias, ssm_d, ssm_norm_g, w_branch_a, w_branch_b, w_branch_c, w_out, final_norm_g, loss_target, m_meta_tokens, m_norm_g, m_w_in, m_gdn_conv_w, m_gdn_a_log, m_gdn_dt_bias, m_gdn_norm_g, m_ssm_conv_w, m_ssm_conv_b, m_ssm_a_log, m_ssm_dt_bias, m_ssm_d, m_ssm_norm_g, m_w_branch_a, m_w_branch_b, m_w_branch_c, m_w_out, m_final_norm_g, v_meta_tokens, v_norm_g, v_w_in, v_gdn_conv_w, v_gdn_a_log, v_gdn_dt_bias, v_gdn_norm_g, v_ssm_conv_w, v_ssm_conv_b, v_ssm_a_log, v_ssm_dt_bias, v_ssm_d, v_ssm_norm_g, v_w_branch_a, v_w_branch_b, v_w_branch_c, v_w_out, v_final_norm_g):
    w = dict(meta_tokens=meta_tokens, norm_g=norm_g, w_in=w_in, gdn_conv_w=gdn_conv_w, gdn_a_log=gdn_a_log, gdn_dt_bias=gdn_dt_bias,
             gdn_norm_g=gdn_norm_g, ssm_conv_w=ssm_conv_w, ssm_conv_b=ssm_conv_b, ssm_a_log=ssm_a_log, ssm_dt_bias=ssm_dt_bias,
             ssm_d=ssm_d, ssm_norm_g=ssm_norm_g, w_branch_a=w_branch_a, w_branch_b=w_branch_b, w_branch_c=w_branch_c, w_out=w_out,
             final_norm_g=final_norm_g)
    m = dict(meta_tokens=m_meta_tokens, norm_g=m_norm_g, w_in=m_w_in, gdn_conv_w=m_gdn_conv_w, gdn_a_log=m_gdn_a_log,
             gdn_dt_bias=m_gdn_dt_bias, gdn_norm_g=m_gdn_norm_g, ssm_conv_w=m_ssm_conv_w, ssm_conv_b=m_ssm_conv_b,
             ssm_a_log=m_ssm_a_log, ssm_dt_bias=m_ssm_dt_bias, ssm_d=m_ssm_d, ssm_norm_g=m_ssm_norm_g, w_branch_a=m_w_branch_a,
             w_branch_b=m_w_branch_b, w_branch_c=m_w_branch_c, w_out=m_w_out, final_norm_g=m_final_norm_g)
    v = dict(meta_tokens=v_meta_tokens, norm_g=v_norm_g, w_in=v_w_in, gdn_conv_w=v_gdn_conv_w, gdn_a_log=v_gdn_a_log,
             gdn_dt_bias=v_gdn_dt_bias, gdn_norm_g=v_gdn_norm_g, ssm_conv_w=v_ssm_conv_w, ssm_conv_b=v_ssm_conv_b,
             ssm_a_log=v_ssm_a_log, ssm_dt_bias=v_ssm_dt_bias, ssm_d=v_ssm_d, ssm_norm_g=v_ssm_norm_g, w_branch_a=v_w_branch_a,
             w_branch_b=v_w_branch_b, w_branch_c=v_w_branch_c, w_out=v_w_out, final_norm_g=v_final_norm_g)
    depth = norm_g.shape[0]
    placed, placed_small = _place_weights(w)
    got_a, got_s = gather_shards("gather_first", [placed[0][0], placed_small])
    exact = _exact_weights(w, got_s)

    h = jnp.concatenate([jnp.zeros((PAD, D_MODEL), F32), exact["meta_tokens"], x[0]], axis=0)
    params, saved = [], []
    for l in range(depth):
        ride = gather_ride([placed[l][1]] + ([placed[l + 1][0]] if l + 1 < depth else []))
        h, s, rode = _layer_fwd(h, _layer_params(w, exact, _proj_weight(got_a), l), ride)
        params.append(s.pop("params"))
        saved.append(s)
        if rode:
            (got_a,) = rode
    loss, d_h, g_final = loss_head(h, loss_target[0], final_norm_g[None])

    layer_grads, reds, waiting = [None] * depth, [None] * depth, None
    for l in reversed(range(depth)):
        d_h, layer_grads[l], rode, parts = _layer_bwd(d_h, params[l], saved[l], exchange_ride(waiting) if waiting else None)
        if waiting:
            reds[l + 1] = [chip_add(f"grads_chip_add_{i}", gt, p) for i, (gt, p) in enumerate(zip(rode, waiting))]
        waiting = parts
    grads = {n: jnp.stack([layer_grads[l][n] for l in range(depth)]) for n in WEIGHTS
             if n not in ("meta_tokens", "final_norm_g", "w_in") + BRANCH}
    grads["meta_tokens"] = d_h[PAD:PAD + N_META]
    grads["final_norm_g"] = g_final[0]
    grad_x = d_h[PAD + N_META:][None]
    buf_s = jnp.stack([_pack([_shard(grads[n], SHARD_AXIS[n], s).astype(BF16).reshape(-1) for n in EXACT], row_unit=32) for s in range(4)])
    small = _pack([grads[n].reshape(-1) for n in REPLICATED], row_unit=32)
    last = waiting + _start_reduce("small_grads", [buf_s, jnp.broadcast_to(small[None], (4,) + small.shape)], [BF16, F32])
    got = chip_exchange("grads_chip_exchange", last)
    sums = [chip_add(f"grads_chip_add_{i}", gt, p) for i, (gt, p) in enumerate(zip(got, last))]
    reds[0] = sums[:2]
    joined = pair_join("grads_pair_join", [r for layer in reds for r in layer] + sums[2:])
    red = {"w_in": jnp.stack(joined[0:2 * depth:2])}
    pos = 0
    for n, rows in zip(BRANCH, BRANCH_ROWS):
        red[n] = jnp.stack([joined[2 * l + 1][pos:pos + rows] for l in range(depth)])
        pos += rows
    red.update(zip(EXACT, _unpack(joined[-2], [w[n].shape for n in EXACT])))
    small_red = joined[-1]
    delta, new_m, new_v = {}, {}, {}
    for n in SHARDED:
        if w[n].shape[-1] % HEAD_DIM:
            to_view, from_view = (lambda a: jnp.transpose(a, (2, 0, 1))), (lambda a: jnp.transpose(a, (1, 2, 0)))
            delta[n], new_m[n], new_v[n], red[n] = [from_view(a) for a in adamw("adamw_" + n, to_view(w[n]), to_view(red[n]),
                                                                                to_view(m[n]), to_view(v[n]), echo_g=True)]
        else:
            delta[n], new_m[n], new_v[n] = adamw("adamw_" + n, w[n], red[n], m[n], v[n])
    pack_small = lambda d: _pack([d[n].reshape(-1) for n in REPLICATED], row_unit=32)
    small = adamw("adamw_small", pack_small(w), small_red, pack_small(m), pack_small(v))
    shapes = [w[n].shape for n in REPLICATED]
    red.update(zip(REPLICATED, _unpack(small_red, shapes)))
    for d, buf in zip((delta, new_m, new_v), small):
        d.update(zip(REPLICATED, _unpack(buf, shapes)))
    total_loss = lax.psum(loss[0, 0], ("x", "y", "c"))
    return (total_loss, grad_x, *[red[n] for n in WEIGHTS], *[delta[n] for n in WEIGHTS], *[new_m[n] for n in WEIGHTS],
            *[new_v[n] for n in WEIGHTS])
```
